```python
import math
import jax, jax.numpy as jnp
from jax import lax
import numpy as np

D_MODEL = 1024
BATCH = 8
SEQ = 2048
DEPTH = 1

D_MIX = D_MODEL
D_ATTN = D_MIX // 2
HEAD_DIM = 64
N_HEADS = D_ATTN // HEAD_DIM
N_KV_HEADS = 2
Q_PER_KV = N_HEADS // N_KV_HEADS
D_KV = N_KV_HEADS * HEAD_DIM
WINDOW = 128
BLOCK = 128
D_SSM = D_MIX - D_ATTN
SSM_GROUP = 16
N_SSM_GROUPS = D_SSM // SSM_GROUP
STATE = 64
D_IN = D_ATTN + 2 * D_KV + D_SSM
N_EXPERTS = 32
TOP_K = 4
D_FF = D_MODEL
SWIGLU_LIMIT = 7.0
SWIGLU_ALPHA = 1.702
EPS = 1e-6
NEG_INF = -1e30

kernel_name = "hybrid_swa_s5_moe_adaln_block"


def rmsnorm(x, gain):
    x32 = x.astype(jnp.float32)
    y = x32 * lax.rsqrt(jnp.mean(x32 * x32, axis=-1, keepdims=True) + EPS)
    return (y * gain.astype(jnp.float32)).astype(x.dtype)


def sliding_window_attention(q, k, v, sinks):
    B, S = q.shape[0], q.shape[1]
    nb = S // BLOCK
    qb = q.reshape(B, nb, BLOCK, N_KV_HEADS, Q_PER_KV, HEAD_DIM)
    pad = ((0, 0), (BLOCK, 0), (0, 0), (0, 0))
    kp = jnp.pad(k, pad).reshape(B, nb + 1, BLOCK, N_KV_HEADS, HEAD_DIM)
    vp = jnp.pad(v, pad).reshape(B, nb + 1, BLOCK, N_KV_HEADS, HEAD_DIM)
    kw = jnp.concatenate([kp[:, :-1], kp[:, 1:]], axis=2)
    vw = jnp.concatenate([vp[:, :-1], vp[:, 1:]], axis=2)
    scores = jnp.einsum('bnqhgd,bnkhd->bnhgqk', qb, kw).astype(jnp.float32)
    scores = scores * (1.0 / math.sqrt(HEAD_DIM))
    qi = jnp.arange(BLOCK)[:, None]
    kj = jnp.arange(2 * BLOCK)[None, :]
    diff = BLOCK + qi - kj
    band = (diff >= 0) & (diff < WINDOW)
    kpos = jnp.arange(nb)[:, None, None] * BLOCK - BLOCK + kj[None]
    mask = band[None] & (kpos >= 0)
    scores = jnp.where(mask[None, :, None, None], scores, NEG_INF)
    sink = sinks.astype(jnp.float32).reshape(N_KV_HEADS, Q_PER_KV)
    sink = jnp.broadcast_to(sink[None, None, :, :, None, None],
                            scores.shape[:-1] + (1,))
    probs = jax.nn.softmax(jnp.concatenate([scores, sink], axis=-1), axis=-1)[..., :-1]
    out = jnp.einsum('bnhgqk,bnkhd->bnqhgd', probs.astype(v.dtype), vw)
    return out.reshape(B, S, D_ATTN)


def _complex_affine_combine(e1, e2):
    a1r, a1i, b1r, b1i = e1
    a2r, a2i, b2r, b2i = e2
    ar = a2r * a1r - a2i * a1i
    ai = a2r * a1i + a2i * a1r
    br = a2r * b1r - a2i * b1i + b2r
    bi = a2r * b1i + a2i * b1r + b2i
    return (ar, ai, br, bi)


def s5_ssm(u, lam_re, lam_im, log_dt, b_re, b_im, c_re, c_im, d_skip):
    B, S = u.shape[0], u.shape[1]
    ug = u.astype(jnp.float32).reshape(B, S, N_SSM_GROUPS, SSM_GROUP)
    lr = lam_re.astype(jnp.float32)
    li = lam_im.astype(jnp.float32)
    dt = jnp.exp(log_dt.astype(jnp.float32))[:, None]
    mag = jnp.exp(lr * dt)
    lb_r = mag * jnp.cos(li * dt)
    lb_i = mag * jnp.sin(li * dt)
    den = lr * lr + li * li
    coef_r = ((lb_r - 1.0) * lr + lb_i * li) / den
    coef_i = (lb_i * lr - (lb_r - 1.0) * li) / den
    br = b_re.astype(jnp.float32)
    bi = b_im.astype(jnp.float32)
    bbar_r = coef_r[..., None] * br - coef_i[..., None] * bi
    bbar_i = coef_r[..., None] * bi + coef_i[..., None] * br
    bu_r = jnp.einsum('bsgh,gph->bsgp', ug, bbar_r)
    bu_i = jnp.einsum('bsgh,gph->bsgp', ug, bbar_i)
    a_r = jnp.broadcast_to(lb_r, bu_r.shape)
    a_i = jnp.broadcast_to(lb_i, bu_i.shape)
    _, _, x_r, x_i = lax.associative_scan(_complex_affine_combine,
                                          (a_r, a_i, bu_r, bu_i), axis=1)
    y = (jnp.einsum('bsgp,ghp->bsgh', x_r, c_re.astype(jnp.float32))
         - jnp.einsum('bsgp,ghp->bsgh', x_i, c_im.astype(jnp.float32)))
    y = y + d_skip.astype(jnp.float32).reshape(N_SSM_GROUPS, SSM_GROUP) * ug
    return y.reshape(B, S, D_SSM).astype(u.dtype)


def moe_ffn(h, w_router, b_router, w_gate_up, b_gate_up, w_down, b_down):
    B, S, D = h.shape
    T = B * S
    ht = h.reshape(T, D)
    logits = (ht @ w_router + b_router).astype(jnp.float32)
    top_val, top_idx = lax.top_k(logits, TOP_K)
    weights = jax.nn.softmax(top_val, axis=-1)
    flat_e = top_idx.reshape(-1)
    order = jnp.argsort(flat_e)
    tok = order // TOP_K
    e_sorted = flat_e[order]
    sizes = jnp.bincount(flat_e, length=N_EXPERTS).astype(jnp.int32)
    xs = ht[tok]
    gu = lax.ragged_dot(xs, w_gate_up, sizes) + b_gate_up[e_sorted]
    gate = jnp.minimum(gu[:, ::2], SWIGLU_LIMIT)
    up = jnp.clip(gu[:, 1::2], -SWIGLU_LIMIT, SWIGLU_LIMIT)
    act = (up + 1.0) * (gate * jax.nn.sigmoid(SWIGLU_ALPHA * gate))
    ys = lax.ragged_dot(act, w_down, sizes) + b_down[e_sorted]
    ys = ys * weights.reshape(-1)[order][:, None].astype(ys.dtype)
    out = jnp.zeros((T, D), ys.dtype).at[tok].add(ys)
    return out.reshape(B, S, D)


def setup_inputs(seed: int = 0) -> dict:
    key = jax.random.key(seed)
    ks = jax.random.split(key, 32)
    L = DEPTH

    def nrm(k, shape, scale):
        return jax.random.normal(k, shape, jnp.float32) * scale

    def gain(k, shape):
        return 1.0 + nrm(k, shape, 0.02)

    lam_im0 = jnp.pi * jnp.arange(STATE, dtype=jnp.float32)
    return {
        "x": nrm(ks[0], (BATCH, SEQ, D_MODEL), 1.0),
        "c": nrm(ks[1], (BATCH, D_MODEL), 1.0),
        "w_ada": nrm(ks[2], (L, D_MODEL, 6 * D_MODEL), 0.5 * D_MODEL ** -0.5),
        "b_ada": nrm(ks[3], (L, 6 * D_MODEL), 0.02),
        "norm_mix": gain(ks[4], (L, D_MODEL)),
        "w_in": nrm(ks[5], (L, D_MODEL, D_IN), D_MODEL ** -0.5),
        "b_in": nrm(ks[6], (L, D_IN), 0.02),
        "q_norm": gain(ks[7], (L, HEAD_DIM)),
        "k_norm": gain(ks[8], (L, HEAD_DIM)),
        "sinks": nrm(ks[9], (L, N_HEADS), 0.5),
        "lam_re": -0.5 + nrm(ks[10], (L, N_SSM_GROUPS, STATE), 0.01),
        "lam_im": lam_im0 + nrm(ks[11], (L, N_SSM_GROUPS, STATE), 0.01),
        "log_dt": jax.random.uniform(ks[12], (L, N_SSM_GROUPS), jnp.float32,
                                     minval=math.log(1e-3), maxval=math.log(1e-1)),
        "b_re": nrm(ks[13], (L, N_SSM_GROUPS, STATE, SSM_GROUP), (2 * SSM_GROUP) ** -0.5),
        "b_im": nrm(ks[14], (L, N_SSM_GROUPS, STATE, SSM_GROUP), (2 * SSM_GROUP) ** -0.5),
        "c_re": nrm(ks[15], (L, N_SSM_GROUPS, SSM_GROUP, STATE), (2 * STATE) ** -0.5),
        "c_im": nrm(ks[16], (L, N_SSM_GROUPS, SSM_GROUP, STATE), (2 * STATE) ** -0.5),
        "d_skip": nrm(ks[17], (L, D_SSM), 1.0),
        "w_glu": nrm(ks[18], (L, D_SSM, D_SSM), D_SSM ** -0.5),
        "b_glu": nrm(ks[19], (L, D_SSM), 0.02),
        "attn_out_norm": gain(ks[20], (L, D_ATTN)),
        "ssm_out_norm": gain(ks[21], (L, D_SSM)),
        "w_out": nrm(ks[22], (L, D_MIX, D_MODEL), D_MIX ** -0.5),
        "norm_ffn": gain(ks[23], (L, D_MODEL)),
        "w_router": nrm(ks[24], (L, D_MODEL, N_EXPERTS), D_MODEL ** -0.5),
        "b_router": nrm(ks[25], (L, N_EXPERTS), 0.01),
        "w_gate_up": nrm(ks[26], (L, N_EXPERTS, D_MODEL, 2 * D_FF), D_MODEL ** -0.5),
        "b_gate_up": nrm(ks[27], (L, N_EXPERTS, 2 * D_FF), 0.02),
        "w_down": nrm(ks[28], (L, N_EXPERTS, D_FF, D_MODEL), D_FF ** -0.5),
        "b_down": nrm(ks[29], (L, N_EXPERTS, D_MODEL), 0.02),
    }


def reference(x, c, w_ada, b_ada, norm_mix, w_in, b_in, q_norm, k_norm, sinks,
              lam_re, lam_im, log_dt, b_re, b_im, c_re, c_im, d_skip, w_glu, b_glu,
              attn_out_norm, ssm_out_norm, w_out, norm_ffn, w_router, b_router,
              w_gate_up, b_gate_up, w_down, b_down):
    B, S = x.shape[0], x.shape[1]
    c_act = jax.nn.silu(c)
    for l in range(DEPTH):
        mod = (c_act @ w_ada[l] + b_ada[l])[:, None, :]
        sh1, sc1, g1, sh2, sc2, g2 = jnp.split(mod, 6, axis=-1)

        h = rmsnorm(x, norm_mix[l]) * (1.0 + sc1) + sh1
        proj = h @ w_in[l] + b_in[l]
        q, k, v, u = jnp.split(proj, [D_ATTN, D_ATTN + D_KV, D_ATTN + 2 * D_KV], axis=-1)
        q = rmsnorm(q.reshape(B, S, N_KV_HEADS, Q_PER_KV, HEAD_DIM), q_norm[l])
        k = rmsnorm(k.reshape(B, S, N_KV_HEADS, HEAD_DIM), k_norm[l])
        v = v.reshape(B, S, N_KV_HEADS, HEAD_DIM)
        attn = sliding_window_attention(q, k, v, sinks[l])

        ssm = s5_ssm(u, lam_re[l], lam_im[l], log_dt[l], b_re[l], b_im[l],
                     c_re[l], c_im[l], d_skip[l])
        ssm = jax.nn.gelu(ssm)
        ssm = ssm * jax.nn.sigmoid(ssm @ w_glu[l] + b_glu[l])

        mixed = jnp.concatenate([rmsnorm(attn, attn_out_norm[l]),
                                 rmsnorm(ssm, ssm_out_norm[l])], axis=-1)
        x = x + g1 * (mixed @ w_out[l])

        h2 = rmsnorm(x, norm_ffn[l]) * (1.0 + sc2) + sh2
        x = x + g2 * moe_ffn(h2, w_router[l], b_router[l], w_gate_up[l],
                             b_gate_up[l], w_down[l], b_down[l])
    return x
```

```python
import functools
import math

import jax
import jax.numpy as jnp
from jax import lax
from jax.experimental import pallas as pl
from jax.experimental.pallas import tpu as pltpu

F32 = jnp.float32
BF16 = jnp.bfloat16
U32 = jnp.uint32
I32 = jnp.int32

D_MODEL = 1024
HEAD_DIM = 64
N_HEADS = 8
N_KV_HEADS = 2
Q_PER_KV = N_HEADS // N_KV_HEADS
D_ATTN = N_HEADS * HEAD_DIM
D_KV = N_KV_HEADS * HEAD_DIM
WINDOW = 128
BLOCK = 128
D_SSM = D_MODEL - D_ATTN
SSM_GROUP = 16
N_GROUPS = D_SSM // SSM_GROUP
STATE = 64
D_IN = D_ATTN + 2 * D_KV + D_SSM
N_EXPERTS = 32
TOP_K = 4
D_FF = D_MODEL
SWIGLU_LIMIT = 7.0
SWIGLU_ALPHA = 1.702
EPS = 1e-6
NEG_INF = -1e30

LANES = 128
SSM_CHUNK = 16
SSM_ROW = SSM_CHUNK * SSM_GROUP
HALF = D_MODEL // 2

INPROJ_ROWS = 512
POST_ROWS = 256
FFN_ROWS = 256
MOVE_ROWS = 256

HIGHEST = lax.Precision.HIGHEST
_ARB = "arbitrary"


def _cparams(n):
    return pltpu.CompilerParams(dimension_semantics=(_ARB,) * n)


def _rms(x, axis=-1):
    return x * lax.rsqrt(jnp.mean(x * x, axis=axis, keepdims=True) + EPS)


def _pack_halves(y):
    hi = lax.bitcast_convert_type(y[:, :HALF].astype(BF16).astype(F32), U32)
    lo = lax.bitcast_convert_type(y[:, HALF:].astype(BF16).astype(F32), U32)
    return (hi & jnp.uint32(0xFFFF0000)) | (lo >> 16)


def _unpack_halves(w):
    hi = lax.bitcast_convert_type(w & jnp.uint32(0xFFFF0000), F32)
    lo = lax.bitcast_convert_type(w << 16, F32)
    return hi, lo


def _adaln_kernel(c_ref, w_ref, b_ref, o_ref):
    c = c_ref[...]
    ca = c * jax.nn.sigmoid(c)
    o_ref[...] = jnp.dot(ca, w_ref[...], preferred_element_type=F32, precision=HIGHEST) + b_ref[...]


def _adaln(c, w_ada, b_ada):
    b, d = c.shape
    n = w_ada.shape[1] // d
    return pl.pallas_call(
        _adaln_kernel,
        grid=(n,),
        in_specs=[pl.BlockSpec((b, d), lambda j: (0, 0)),
                  pl.BlockSpec((d, d), lambda j: (0, j)),
                  pl.BlockSpec((1, d), lambda j: (0, j))],
        out_specs=pl.BlockSpec((b, d), lambda j: (0, j)),
        out_shape=jax.ShapeDtypeStruct((b, n * d), F32),
        compiler_params=_cparams(1),
        name="adaln",
    )(c, w_ada, b_ada.reshape(1, -1))


def _inproj_kernel(x_ref, mod_ref, g_ref, w_ref, b_ref, q_ref, k_ref, v_ref, u_ref):
    x = x_ref[0]
    y = _rms(x) * g_ref[...]
    h = y * (1.0 + mod_ref[0, 1:2, :]) + mod_ref[0, 0:1, :]
    proj = jnp.dot(h.astype(BF16), w_ref[...], preferred_element_type=F32) + b_ref[...]
    q_ref[0] = proj[:, :D_ATTN].astype(BF16)
    k_ref[0] = proj[:, D_ATTN:D_ATTN + D_KV].astype(BF16)
    v_ref[0] = proj[:, D_ATTN + D_KV:D_ATTN + 2 * D_KV].astype(BF16)
    u_ref[0] = proj[:, D_ATTN + 2 * D_KV:].astype(BF16)


def _inproj(x, mod, gain, w_in, b_in):
    b, s, d = x.shape
    ts = min(INPROJ_ROWS, s)
    row = lambda bi, j: (bi, j, 0)
    const = lambda bi, j: (0, 0)
    return pl.pallas_call(
        _inproj_kernel,
        grid=(b, s // ts),
        in_specs=[pl.BlockSpec((1, ts, d), row),
                  pl.BlockSpec((1, 6, d), lambda bi, j: (bi, 0, 0)),
                  pl.BlockSpec((1, d), const),
                  pl.BlockSpec((d, D_IN), const),
                  pl.BlockSpec((1, D_IN), const)],
        out_specs=[pl.BlockSpec((1, ts, D_ATTN), row),
                   pl.BlockSpec((1, ts, D_KV), row),
                   pl.BlockSpec((1, ts, D_KV), row),
                   pl.BlockSpec((1, ts, D_SSM), row)],
        out_shape=[jax.ShapeDtypeStruct((b, s, D_ATTN), BF16),
                   jax.ShapeDtypeStruct((b, s, D_KV), BF16),
                   jax.ShapeDtypeStruct((b, s, D_KV), BF16),
                   jax.ShapeDtypeStruct((b, s, D_SSM), BF16)],
        compiler_params=_cparams(2),
        name="inproj",
    )(x, mod, gain.reshape(1, d), w_in.astype(BF16), b_in.reshape(1, D_IN))


def _half_norm(x, low):
    sq = x * x
    s_lo = jnp.sum(jnp.where(low, sq, 0.0), axis=-1, keepdims=True)
    s_hi = jnp.sum(sq, axis=-1, keepdims=True) - s_lo
    inv = 1.0 / HEAD_DIM
    scale = jnp.where(low, lax.rsqrt(s_lo * inv + EPS), lax.rsqrt(s_hi * inv + EPS))
    return x * scale


def _attn_kernel(sinks_ref, q_ref, k_ref, v_ref, qn_ref, kn_ref, on_ref, o_ref):
    n = pl.program_id(1)
    cur = pl.multiple_of(n * BLOCK, BLOCK)
    prev = pl.multiple_of(jnp.maximum(n - 1, 0) * BLOCK, BLOCK)
    low = lax.broadcasted_iota(I32, (1, LANES), 1) < HEAD_DIM

    kwin = jnp.concatenate([k_ref[0, pl.ds(prev, BLOCK), :], k_ref[0, pl.ds(cur, BLOCK), :]], axis=0).astype(F32)
    vwin = jnp.concatenate([v_ref[0, pl.ds(prev, BLOCK), :], v_ref[0, pl.ds(cur, BLOCK), :]], axis=0).astype(F32)
    kwin = _half_norm(kwin, low) * kn_ref[...]
    kswap = pltpu.roll(kwin, HEAD_DIM, axis=1)
    vswap = pltpu.roll(vwin, HEAD_DIM, axis=1)
    k_dup = [jnp.where(low, kwin, kswap).astype(BF16), jnp.where(low, kswap, kwin).astype(BF16)]
    v_dup = [jnp.where(low, vwin, vswap).astype(BF16), jnp.where(low, vswap, vwin).astype(BF16)]

    rows = Q_PER_KV * BLOCK
    qi = lax.broadcasted_iota(I32, (rows, 2 * BLOCK), 0) % BLOCK
    kj = lax.broadcasted_iota(I32, (rows, 2 * BLOCK), 1)
    diff = BLOCK + qi - kj
    valid = (diff >= 0) & (diff < WINDOW) & ((kj >= BLOCK) | (n > 0))
    rblk = lax.broadcasted_iota(I32, (rows, 1), 0) // BLOCK

    q = q_ref[0].astype(F32)
    out_blocks = []
    for hk in range(N_KV_HEADS):
        qs = []
        for j in range(Q_PER_KV // 2):
            blk = hk * (Q_PER_KV // 2) + j
            qb = _half_norm(q[:, blk * LANES:(blk + 1) * LANES], low) * qn_ref[...] * (1.0 / math.sqrt(HEAD_DIM))
            qs.append(jnp.where(low, qb, 0.0))
            qs.append(jnp.where(low, 0.0, qb))
        qg = jnp.concatenate(qs, axis=0).astype(BF16)
        s = lax.dot_general(qg, k_dup[hk], (((1,), (1,)), ((), ())), preferred_element_type=F32)
        s = jnp.where(valid, s, NEG_INF)
        sink = jnp.zeros((rows, 1), F32)
        for g in range(Q_PER_KV):
            sink = jnp.where(rblk == g, sinks_ref[hk * Q_PER_KV + g], sink)
        m = jnp.maximum(jnp.max(s, axis=-1, keepdims=True), sink)
        p = jnp.exp(s - m)
        den = jnp.sum(p, axis=-1, keepdims=True) + jnp.exp(sink - m)
        o = jnp.dot(p.astype(BF16), v_dup[hk], preferred_element_type=F32) / den
        for j in range(Q_PER_KV // 2):
            ev = o[(2 * j) * BLOCK:(2 * j + 1) * BLOCK]
            od = o[(2 * j + 1) * BLOCK:(2 * j + 2) * BLOCK]
            out_blocks.append(jnp.where(low, ev, od))
    attn = jnp.concatenate(out_blocks, axis=-1)
    o_ref[0] = (_rms(attn) * on_ref[...]).astype(BF16)


def _attention(q, k, v, sinks, q_norm, k_norm, out_norm):
    b, s, _ = q.shape
    tile2 = lambda g: jnp.tile(g.reshape(1, HEAD_DIM), (1, 2))
    return pl.pallas_call(
        _attn_kernel,
        grid=(b, s // BLOCK),
        in_specs=[pl.BlockSpec(memory_space=pltpu.SMEM),
                  pl.BlockSpec((1, BLOCK, D_ATTN), lambda bi, n: (bi, n, 0)),
                  pl.BlockSpec((1, s, D_KV), lambda bi, n: (bi, 0, 0)),
                  pl.BlockSpec((1, s, D_KV), lambda bi, n: (bi, 0, 0)),
                  pl.BlockSpec((1, LANES), lambda bi, n: (0, 0)),
                  pl.BlockSpec((1, LANES), lambda bi, n: (0, 0)),
                  pl.BlockSpec((1, D_ATTN), lambda bi, n: (0, 0))],
        out_specs=pl.BlockSpec((1, BLOCK, D_ATTN), lambda bi, n: (bi, n, 0)),
        out_shape=jax.ShapeDtypeStruct((b, s, D_ATTN), BF16),
        compiler_params=_cparams(2),
        name="attention",
    )(sinks, q, k, v, tile2(q_norm), tile2(k_norm), out_norm.reshape(1, D_ATTN))


def _cpow(rho, th, e):
    mag = jnp.exp(rho * e)
    return mag * jnp.cos(th * e), mag * jnp.sin(th * e)


def _ssm_param_kernel(lr_row, li_row, ld_row, lr_col, li_col, ld_col, bt_r, bt_i, ct_r, ct_i,
                      t_ref, wz_ref, wy_ref, cs_ref):
    lr = lr_row[0]
    li = li_row[0]
    dt = jnp.exp(ld_row[0])
    rho = lr * dt
    th = li * dt
    lb_r, lb_i = _cpow(rho, th, 1.0)
    den = lr * lr + li * li
    coef_r = ((lb_r - 1.0) * lr + lb_i * li) / den
    coef_i = (lb_i * lr - (lb_r - 1.0) * li) / den
    b_r = bt_r[0]
    b_i = bt_i[0]
    bbar_r = coef_r * b_r - coef_i * b_i
    bbar_i = coef_r * b_i + coef_i * b_r
    irow = (lax.broadcasted_iota(I32, (SSM_ROW, 1), 0) // SSM_GROUP).astype(F32)
    imag_lane = lax.broadcasted_iota(I32, (1, LANES), 1) >= STATE

    p_r, p_i = _cpow(rho, th, -irow)
    a_r = bbar_r * p_r - bbar_i * p_i
    a_i = bbar_r * p_i + bbar_i * p_r
    a2c = jnp.where(imag_lane, -a_i, a_r)

    p_r, p_i = _cpow(rho, th, (SSM_CHUNK - 1.0) - irow)
    w_r = bbar_r * p_r - bbar_i * p_i
    w_i = bbar_r * p_i + bbar_i * p_r
    wz_ref[0, :, :LANES] = jnp.where(imag_lane, w_i, w_r).astype(BF16)
    wz_ref[0, :, LANES:] = jnp.where(imag_lane, w_r, w_i).astype(BF16)

    pl_r, pl_i = _cpow(rho, th, float(SSM_CHUNK))
    cs_ref[0, 0:1, :] = pl_r
    cs_ref[0, 1:2, :] = jnp.where(imag_lane, pl_i, -pl_i)

    lrc = lr_col[0]
    lic = li_col[0]
    dtc = jnp.exp(ld_col[0])
    rhoc = lrc * dtc
    thc = lic * dtc
    jcol = (lax.broadcasted_iota(I32, (1, SSM_ROW), 1) // SSM_GROUP).astype(F32)
    imag_row = lax.broadcasted_iota(I32, (LANES, 1), 0) >= STATE
    c_r = ct_r[0]
    c_i = ct_i[0]

    p_r, p_i = _cpow(rhoc, thc, jcol)
    m_r = c_r * p_r - c_i * p_i
    m_i = c_r * p_i + c_i * p_r
    bmc = jnp.where(imag_row, m_i, m_r)
    t = jnp.dot(a2c, bmc, preferred_element_type=F32, precision=HIGHEST)
    ti = lax.broadcasted_iota(I32, (SSM_ROW, SSM_ROW), 0) // SSM_GROUP
    tj = lax.broadcasted_iota(I32, (SSM_ROW, SSM_ROW), 1) // SSM_GROUP
    t_ref[0] = jnp.where(ti <= tj, t, 0.0).astype(BF16)

    p_r, p_i = _cpow(rhoc, thc, jcol + 1.0)
    y_r = c_r * p_r - c_i * p_i
    y_i = c_r * p_i + c_i * p_r
    wy_ref[0] = jnp.where(imag_row, -y_i, y_r).astype(BF16)


def _ssm_params(lam_re, lam_im, log_dt, b_re, b_im, c_re, c_im):
    g = lam_re.shape[0]
    row2 = lambda a: jnp.tile(a.reshape(g, 1, STATE), (1, 1, 2))
    col2 = lambda a: jnp.tile(a.reshape(g, STATE, 1), (1, 2, 1))
    ld = jnp.broadcast_to(log_dt.reshape(g, 1), (g, STATE))
    bt = lambda a: jnp.tile(jnp.swapaxes(a, 1, 2), (1, SSM_CHUNK, 2))
    ct = lambda a: jnp.tile(jnp.swapaxes(a, 1, 2), (1, 2, SSM_CHUNK))
    blk = lambda *shape: pl.BlockSpec((1,) + shape, lambda i: (i, 0, 0))
    return pl.pallas_call(
        _ssm_param_kernel,
        grid=(g,),
        in_specs=[blk(1, LANES)] * 3 + [blk(LANES, 1)] * 3 + [blk(SSM_ROW, LANES)] * 2 + [blk(LANES, SSM_ROW)] * 2,
        out_specs=[blk(SSM_ROW, SSM_ROW), blk(SSM_ROW, SSM_ROW), blk(LANES, SSM_ROW), blk(2, LANES)],
        out_shape=[jax.ShapeDtypeStruct((g, SSM_ROW, SSM_ROW), BF16),
                   jax.ShapeDtypeStruct((g, SSM_ROW, SSM_ROW), BF16),
                   jax.ShapeDtypeStruct((g, LANES, SSM_ROW), BF16),
                   jax.ShapeDtypeStruct((g, 2, LANES), F32)],
        compiler_params=_cparams(1),
        name="ssm_params",
    )(row2(lam_re), row2(lam_im), row2(ld), col2(lam_re), col2(lam_im), col2(ld),
      bt(b_re), bt(b_im), ct(c_re), ct(c_im))


def _ssm_kernel(u_ref, t_ref, wz_ref, wy_ref, cs_ref, d_ref, y_ref, z_scr, s_scr, *, batch, n_chunks):
    u = u_ref[0]
    z_scr[...] = jnp.dot(u, wz_ref[0], preferred_element_type=F32)
    c1 = cs_ref[0, 0:1, :]
    c2 = cs_ref[0, 1:2, :]

    def step(c, carry):
        s1, s2 = carry
        r = pl.multiple_of(c * batch, batch)
        s_scr[pl.ds(r, batch), :] = s1
        z = z_scr[pl.ds(r, batch), :]
        n1 = c1 * s1 + c2 * s2 + z[:, :LANES]
        n2 = c1 * s2 - c2 * s1 + z[:, LANES:]
        return n1, n2

    zero = jnp.zeros((batch, LANES), F32)
    lax.fori_loop(0, n_chunks, step, (zero, zero), unroll=8)
    y = jnp.dot(u, t_ref[0], preferred_element_type=F32)
    y = y + jnp.dot(s_scr[...].astype(BF16), wy_ref[0], preferred_element_type=F32)
    y_ref[0] = y + d_ref[0] * u.astype(F32)


def _ssm(u, t_mat, wz, wy, cs, d_skip):
    b, s, _ = u.shape
    nc = s // SSM_CHUNK
    n = nc * b
    ug = u.reshape(b, nc, SSM_CHUNK, N_GROUPS, SSM_GROUP).transpose(3, 1, 0, 2, 4).reshape(N_GROUPS, n, SSM_ROW)
    d_row = jnp.tile(d_skip.reshape(N_GROUPS, 1, SSM_GROUP), (1, 1, SSM_CHUNK))
    blk = lambda *shape: pl.BlockSpec((1,) + shape, lambda i: (i, 0, 0))
    yg = pl.pallas_call(
        functools.partial(_ssm_kernel, batch=b, n_chunks=nc),
        grid=(N_GROUPS,),
        in_specs=[blk(n, SSM_ROW), blk(SSM_ROW, SSM_ROW), blk(SSM_ROW, SSM_ROW), blk(LANES, SSM_ROW),
                  blk(2, LANES), blk(1, SSM_ROW)],
        out_specs=blk(n, SSM_ROW),
        out_shape=jax.ShapeDtypeStruct((N_GROUPS, n, SSM_ROW), F32),
        scratch_shapes=[pltpu.VMEM((n, SSM_ROW), F32), pltpu.VMEM((n, LANES), F32)],
        compiler_params=_cparams(1),
        name="ssm",
    )(ug, t_mat, wz, wy, cs, d_row)
    return yg.reshape(N_GROUPS, nc, b, SSM_CHUNK, SSM_GROUP).transpose(2, 1, 3, 0, 4).reshape(b, s, D_SSM)


def _post_kernel(x_ref, attn_ref, y_ref, mod_ref, wglu_ref, bglu_ref, sn_ref, wout_ref, nf_ref, wr_ref, br_ref,
                 tri_ref, x1_ref, h2_ref, eidx_ref, wts_ref, rank_ref, cnt_ref, carry_ref):
    @pl.when((pl.program_id(0) == 0) & (pl.program_id(1) == 0))
    def _():
        carry_ref[...] = jnp.zeros_like(carry_ref)

    g = jax.nn.gelu(y_ref[0])
    gate = jax.nn.sigmoid(jnp.dot(g.astype(BF16), wglu_ref[...], preferred_element_type=F32) + bglu_ref[...])
    ssm = _rms(g * gate) * sn_ref[...]
    mixed = jnp.concatenate([attn_ref[0], ssm.astype(BF16)], axis=-1)
    o = jnp.dot(mixed, wout_ref[...], preferred_element_type=F32)
    x1 = x_ref[0] + mod_ref[0, 2:3, :] * o
    x1_ref[0] = x1
    h2 = _rms(x1) * nf_ref[...] * (1.0 + mod_ref[0, 4:5, :]) + mod_ref[0, 3:4, :]
    h2_ref[0] = _pack_halves(h2)

    logits = lax.dot_general(wr_ref[...], h2.astype(BF16), (((1,), (1,)), ((), ())),
                             preferred_element_type=F32) + br_ref[...]
    ts = logits.shape[1]
    iota_e = lax.broadcasted_iota(I32, (N_EXPERTS, ts), 0).astype(F32)
    l = logits
    idxs, vals = [], []
    for _ in range(TOP_K):
        m = jnp.max(l, axis=0, keepdims=True)
        idx = jnp.min(jnp.where(l == m, iota_e, float(N_EXPERTS)), axis=0, keepdims=True)
        idxs.append(idx)
        vals.append(m)
        l = jnp.where(iota_e == idx, -jnp.inf, l)
    es = [jnp.exp(v - vals[0]) for v in vals]
    tot = es[0] + es[1] + es[2] + es[3]
    member = jnp.zeros((N_EXPERTS, ts), F32)
    for idx in idxs:
        member = member + (iota_e == idx).astype(F32)
    before = jnp.dot(member.astype(BF16), tri_ref[...], preferred_element_type=F32) + carry_ref[...]
    for k in range(TOP_K):
        eidx_ref[k:k + 1, :] = idxs[k].astype(I32)
        wts_ref[k:k + 1, :] = es[k] / tot
        rank_ref[k:k + 1, :] = jnp.sum(jnp.where(iota_e == idxs[k], before, 0.0), axis=0, keepdims=True).astype(I32)
    carry = carry_ref[...] + jnp.sum(member, axis=1, keepdims=True)
    carry_ref[...] = carry
    cnt_ref[...] = carry.astype(I32)


def _post(x, attn, yssm, mod, w_glu, b_glu, ssm_norm, w_out, norm_ffn, w_router, b_router):
    b, s, d = x.shape
    ts = min(POST_ROWS, s)
    nt = s // ts
    t = b * s
    row = lambda bi, j: (bi, j, 0)
    const = lambda bi, j: (0, 0)
    tok = lambda bi, j: (0, bi * nt + j)
    tri = (lax.broadcasted_iota(I32, (ts, ts), 0) < lax.broadcasted_iota(I32, (ts, ts), 1)).astype(BF16)
    return pl.pallas_call(
        _post_kernel,
        grid=(b, nt),
        in_specs=[pl.BlockSpec((1, ts, d), row),
                  pl.BlockSpec((1, ts, D_ATTN), row),
                  pl.BlockSpec((1, ts, D_SSM), row),
                  pl.BlockSpec((1, 6, d), lambda bi, j: (bi, 0, 0)),
                  pl.BlockSpec((D_SSM, D_SSM), const),
                  pl.BlockSpec((1, D_SSM), const),
                  pl.BlockSpec((1, D_SSM), const),
                  pl.BlockSpec((d, d), const),
                  pl.BlockSpec((1, d), const),
                  pl.BlockSpec((N_EXPERTS, d), const),
                  pl.BlockSpec((N_EXPERTS, 1), const),
                  pl.BlockSpec((ts, ts), const)],
        out_specs=[pl.BlockSpec((1, ts, d), row),
                   pl.BlockSpec((1, ts, HALF), row),
                   pl.BlockSpec((TOP_K, ts), tok),
                   pl.BlockSpec((TOP_K, ts), tok),
                   pl.BlockSpec((TOP_K, ts), tok),
                   pl.BlockSpec((N_EXPERTS, 1), const)],
        out_shape=[jax.ShapeDtypeStruct((b, s, d), F32),
                   jax.ShapeDtypeStruct((b, s, HALF), U32),
                   jax.ShapeDtypeStruct((TOP_K, t), I32),
                   jax.ShapeDtypeStruct((TOP_K, t), F32),
                   jax.ShapeDtypeStruct((TOP_K, t), I32),
                   jax.ShapeDtypeStruct((N_EXPERTS, 1), I32)],
        scratch_shapes=[pltpu.VMEM((N_EXPERTS, 1), F32)],
        compiler_params=_cparams(2),
        name="post",
    )(x, attn, yssm, mod, w_glu.astype(BF16), b_glu.reshape(1, -1), ssm_norm.reshape(1, -1),
      w_out.astype(BF16), norm_ffn.reshape(1, -1), w_router.T.astype(BF16), b_router.reshape(-1, 1), tri)


def _route_kernel(eidx_ref, rank_ref, cnt_ref, pos_ref, te_ref, nv_ref, *, n_tiles):
    cnt = cnt_ref[...]
    tiles = (cnt + (FFN_ROWS - 1)) // FFN_ROWS
    er = lax.broadcasted_iota(I32, (N_EXPERTS, N_EXPERTS), 0)
    ec = lax.broadcasted_iota(I32, (N_EXPERTS, N_EXPERTS), 1)
    ltri = (ec < er).astype(BF16)
    tiles_b = jnp.broadcast_to(tiles.astype(F32), (N_EXPERTS, LANES)).astype(BF16)
    start_t = jnp.dot(ltri, tiles_b, preferred_element_type=F32)[:, 0:1].astype(I32)
    end_t = start_t + tiles
    start = start_t * FFN_ROWS

    t = eidx_ref.shape[1]
    iota_e = lax.broadcasted_iota(I32, (N_EXPERTS, t), 0)
    for k in range(TOP_K):
        sel = jnp.where(iota_e == eidx_ref[k:k + 1, :], start, 0)
        pos_ref[k:k + 1, :] = jnp.sum(sel, axis=0, keepdims=True) + rank_ref[k:k + 1, :]

    nv = jnp.max(end_t, axis=0, keepdims=True)
    width = te_ref.shape[1]
    ti = jnp.minimum(lax.broadcasted_iota(I32, (N_EXPERTS, width), 1), nv - 1)
    te = jnp.sum((ti >= end_t).astype(I32), axis=0, keepdims=True)
    te_ref[...] = jnp.minimum(te, N_EXPERTS - 1)
    nv_ref[...] = jnp.broadcast_to(nv, nv_ref.shape)


def _route(eidx, rank, cnt, n_tiles):
    t = eidx.shape[1]
    width = -(-n_tiles // LANES) * LANES
    return pl.pallas_call(
        functools.partial(_route_kernel, n_tiles=n_tiles),
        out_shape=[jax.ShapeDtypeStruct((TOP_K, t), I32),
                   jax.ShapeDtypeStruct((1, width), I32),
                   jax.ShapeDtypeStruct((1, LANES), I32)],
        name="route",
    )(eidx, rank, cnt)


def _dispatch_kernel(pos_ref, h_ref, xs_in_ref, xs_ref, sem):
    del xs_in_ref
    tt = h_ref.shape[0]

    def issue(t, carry):
        for k in range(TOP_K):
            pltpu.make_async_copy(h_ref.at[pl.ds(t, 1), :], xs_ref.at[pl.ds(pos_ref[k, t], 1), :], sem).start()
        return carry

    lax.fori_loop(0, tt, issue, 0, unroll=8)
    for _ in range(TOP_K):
        pltpu.make_async_copy(h_ref, xs_ref.at[pl.ds(0, tt), :], sem).wait()


def _dispatch(pos, h2p, n_rows):
    t = h2p.shape[0]
    tt = min(MOVE_ROWS, t)
    return pl.pallas_call(
        _dispatch_kernel,
        grid=(t // tt,),
        in_specs=[pl.BlockSpec((TOP_K, tt), lambda i: (0, i), memory_space=pltpu.SMEM),
                  pl.BlockSpec((tt, HALF), lambda i: (i, 0)),
                  pl.BlockSpec(memory_space=pl.ANY)],
        out_specs=pl.BlockSpec(memory_space=pl.ANY),
        out_shape=jax.ShapeDtypeStruct((n_rows, HALF), U32),
        scratch_shapes=[pltpu.SemaphoreType.DMA],
        input_output_aliases={2: 0},
        compiler_params=_cparams(1),
        name="dispatch",
    )(pos, h2p, jnp.zeros((n_rows, HALF), U32))


def _ffn_kernel(te_ref, nv_ref, xs_ref, wg_ref, wu_ref, bg_ref, bu_ref, wd_ref, bd_ref, ys_ref):
    del te_ref

    @pl.when(pl.program_id(0) < nv_ref[0])
    def _():
        x_hi, x_lo = _unpack_halves(xs_ref[...])
        x_hi = x_hi.astype(BF16)
        x_lo = x_lo.astype(BF16)
        gate = (jnp.dot(x_hi, wg_ref[0, :HALF, :], preferred_element_type=F32)
                + jnp.dot(x_lo, wg_ref[0, HALF:, :], preferred_element_type=F32) + bg_ref[0])
        up = (jnp.dot(x_hi, wu_ref[0, :HALF, :], preferred_element_type=F32)
              + jnp.dot(x_lo, wu_ref[0, HALF:, :], preferred_element_type=F32) + bu_ref[0])
        gate = jnp.minimum(gate, SWIGLU_LIMIT)
        up = jnp.clip(up, -SWIGLU_LIMIT, SWIGLU_LIMIT)
        act = (up + 1.0) * (gate * jax.nn.sigmoid(SWIGLU_ALPHA * gate))
        y = jnp.dot(act.astype(BF16), wd_ref[0], preferred_element_type=F32) + bd_ref[0]
        ys_ref[...] = _pack_halves(y)


def _ffn(te, nv, xs, wg, wu, bg, bu, wd, bd, n_tiles):
    d = D_MODEL
    tile = lambda i, te, nv: (jnp.minimum(i, nv[0] - 1), 0)
    wsel = lambda i, te, nv: (te[i], 0, 0)
    return pl.pallas_call(
        _ffn_kernel,
        grid_spec=pltpu.PrefetchScalarGridSpec(
            num_scalar_prefetch=2,
            grid=(n_tiles,),
            in_specs=[pl.BlockSpec((FFN_ROWS, HALF), tile),
                      pl.BlockSpec((1, d, D_FF), wsel),
                      pl.BlockSpec((1, d, D_FF), wsel),
                      pl.BlockSpec((1, 1, D_FF), wsel),
                      pl.BlockSpec((1, 1, D_FF), wsel),
                      pl.BlockSpec((1, D_FF, d), wsel),
                      pl.BlockSpec((1, 1, d), wsel)],
            out_specs=pl.BlockSpec((FFN_ROWS, HALF), tile),
        ),
        out_shape=jax.ShapeDtypeStruct(xs.shape, U32),
        input_output_aliases={2: 0},
        compiler_params=_cparams(1),
        name="ffn",
    )(te, nv, xs, wg, wu, bg, bu, wd, bd)


def _combine_kernel(pos_ref, x1_ref, w_ref, mod_ref, ys_ref, o_ref, ybuf, sem):
    tt = x1_ref.shape[1]

    def issue(t, carry):
        for k in range(TOP_K):
            pltpu.make_async_copy(ys_ref.at[pl.ds(pos_ref[k, t], 1), :], ybuf.at[k, pl.ds(t, 1), :], sem).start()
        return carry

    lax.fori_loop(0, tt, issue, 0, unroll=8)
    for k in range(TOP_K):
        pltpu.make_async_copy(ys_ref.at[pl.ds(0, tt), :], ybuf.at[k], sem).wait()

    acc_hi = jnp.zeros((tt, HALF), F32)
    acc_lo = jnp.zeros((tt, HALF), F32)
    for k in range(TOP_K):
        hi, lo = _unpack_halves(ybuf[k])
        wk = w_ref[:, k:k + 1]
        acc_hi = acc_hi + wk * hi
        acc_lo = acc_lo + wk * lo
    g2 = mod_ref[0, 5:6, :]
    o_ref[0, :, :HALF] = x1_ref[0, :, :HALF] + g2[:, :HALF] * acc_hi
    o_ref[0, :, HALF:] = x1_ref[0, :, HALF:] + g2[:, HALF:] * acc_lo


def _combine(pos, x1, wts_t, mod, ys):
    b, s, d = x1.shape
    tt = min(MOVE_ROWS, s)
    nt = s // tt
    return pl.pallas_call(
        _combine_kernel,
        grid=(b, nt),
        in_specs=[pl.BlockSpec((TOP_K, tt), lambda bi, j: (0, bi * nt + j), memory_space=pltpu.SMEM),
                  pl.BlockSpec((1, tt, d), lambda bi, j: (bi, j, 0)),
                  pl.BlockSpec((tt, TOP_K), lambda bi, j: (bi * nt + j, 0)),
                  pl.BlockSpec((1, 6, d), lambda bi, j: (bi, 0, 0)),
                  pl.BlockSpec(memory_space=pl.ANY)],
        out_specs=pl.BlockSpec((1, tt, d), lambda bi, j: (bi, j, 0)),
        out_shape=jax.ShapeDtypeStruct((b, s, d), F32),
        scratch_shapes=[pltpu.VMEM((TOP_K, tt, HALF), U32), pltpu.SemaphoreType.DMA],
        compiler_params=_cparams(2),
        name="combine",
    )(pos, x1, wts_t, mod, ys)


def kernel(x, c, w_ada, b_ada, norm_mix, w_in, b_in, q_norm, k_norm, sinks, lam_re, lam_im, log_dt, b_re, b_im,
           c_re, c_im, d_skip, w_glu, b_glu, attn_out_norm, ssm_out_norm, w_out, norm_ffn, w_router, b_router,
           w_gate_up, b_gate_up, w_down, b_down):
    b, s, d = x.shape
    t = b * s
    depth = w_ada.shape[0]
    n_tiles = (t * TOP_K) // FFN_ROWS + N_EXPERTS
    for l in range(depth):
        mod = _adaln(c, w_ada[l], b_ada[l]).reshape(b, 6, d)
        q, k, v, u = _inproj(x, mod, norm_mix[l], w_in[l], b_in[l])
        attn = _attention(q, k, v, sinks[l], q_norm[l], k_norm[l], attn_out_norm[l])
        t_mat, wz, wy, cs = _ssm_params(lam_re[l], lam_im[l], log_dt[l], b_re[l], b_im[l], c_re[l], c_im[l])
        yssm = _ssm(u, t_mat, wz, wy, cs, d_skip[l])
        x1, h2p, eidx, wts, rank, cnt = _post(x, attn, yssm, mod, w_glu[l], b_glu[l], ssm_out_norm[l], w_out[l],
                                              norm_ffn[l], w_router[l], b_router[l])
        pos, te, nv = _route(eidx, rank, cnt, n_tiles)
        xs = _dispatch(pos, h2p.reshape(t, HALF), n_tiles * FFN_ROWS)
        wgu = w_gate_up[l]
        bgu = b_gate_up[l]
        ys = _ffn(te[0, :n_tiles], nv[0, :1], xs,
                  wgu[:, :, 0::2].astype(BF16), wgu[:, :, 1::2].astype(BF16),
                  bgu[:, None, 0::2], bgu[:, None, 1::2],
                  w_down[l].astype(BF16), b_down[l][:, None, :], n_tiles)
        x = _combine(pos, x1, wts.T, mod, ys)
    return x
```

```python
import functools
import math

import jax
import jax.numpy as jnp
from jax import lax
from jax.experimental import pallas as pl
from jax.experimental.pallas import tpu as pltpu

F32 = jnp.float32
BF16 = jnp.bfloat16
U32 = jnp.uint32
I32 = jnp.int32

D_MODEL = 1024
HEAD_DIM = 64
N_HEADS = 8
N_KV_HEADS = 2
Q_PER_KV = N_HEADS // N_KV_HEADS
D_ATTN = N_HEADS * HEAD_DIM
D_KV = N_KV_HEADS * HEAD_DIM
WINDOW = 128
BLOCK = 128
D_SSM = D_MODEL - D_ATTN
SSM_GROUP = 16
N_GROUPS = D_SSM // SSM_GROUP
STATE = 64
D_IN = D_ATTN + 2 * D_KV + D_SSM
N_EXPERTS = 32
TOP_K = 4
D_FF = D_MODEL
SWIGLU_LIMIT = 7.0
SWIGLU_ALPHA = 1.702
EPS = 1e-6
NEG_INF = -1e30

LANES = 128
SSM_CHUNK = 16
SSM_ROW = SSM_CHUNK * SSM_GROUP
HALF = D_MODEL // 2

INPROJ_ROWS = 512
POST_ROWS = 256
FFN_ROWS = 256
MOVE_ROWS = 256
PERM = 256
FFN_VMEM_BYTES = 48 * 1024 * 1024

HIGHEST = lax.Precision.HIGHEST
_ARB = "arbitrary"


def _cparams(n):
    return pltpu.CompilerParams(dimension_semantics=(_ARB,) * n)


def _rms(x, axis=-1):
    return x * lax.rsqrt(jnp.mean(x * x, axis=axis, keepdims=True) + EPS)


def _pack_halves(y):
    hi = lax.bitcast_convert_type(y[:, :HALF].astype(BF16).astype(F32), U32)
    lo = lax.bitcast_convert_type(y[:, HALF:].astype(BF16).astype(F32), U32)
    return (hi & jnp.uint32(0xFFFF0000)) | (lo >> 16)


def _unpack_halves(w):
    hi = lax.bitcast_convert_type(w & jnp.uint32(0xFFFF0000), F32)
    lo = lax.bitcast_convert_type(w << 16, F32)
    return hi, lo


def _adaln_kernel(c_ref, w_ref, b_ref, o_ref):
    c = c_ref[...]
    ca = c * jax.nn.sigmoid(c)
    o_ref[...] = jnp.dot(ca, w_ref[...], preferred_element_type=F32, precision=HIGHEST) + b_ref[...]


def _adaln(c, w_ada, b_ada):
    b, d = c.shape
    n = w_ada.shape[1] // d
    return pl.pallas_call(
        _adaln_kernel,
        grid=(n,),
        in_specs=[pl.BlockSpec((b, d), lambda j: (0, 0)),
                  pl.BlockSpec((d, d), lambda j: (0, j)),
                  pl.BlockSpec((1, d), lambda j: (0, j))],
        out_specs=pl.BlockSpec((b, d), lambda j: (0, j)),
        out_shape=jax.ShapeDtypeStruct((b, n * d), F32),
        compiler_params=_cparams(1),
        name="adaln",
    )(c, w_ada, b_ada.reshape(1, -1))


def _inproj_kernel(x_ref, mod_ref, g_ref, w_ref, b_ref, q_ref, k_ref, v_ref, u_ref):
    x = x_ref[0]
    y = _rms(x) * g_ref[...]
    h = y * (1.0 + mod_ref[0, 1:2, :]) + mod_ref[0, 0:1, :]
    proj = jnp.dot(h.astype(BF16), w_ref[...], preferred_element_type=F32) + b_ref[...]
    q_ref[0] = proj[:, :D_ATTN].astype(BF16)
    k_ref[0] = proj[:, D_ATTN:D_ATTN + D_KV].astype(BF16)
    v_ref[0] = proj[:, D_ATTN + D_KV:D_ATTN + 2 * D_KV].astype(BF16)
    u_ref[0] = proj[:, D_ATTN + 2 * D_KV:].astype(BF16)


def _inproj(x, mod, gain, w_in, b_in):
    b, s, d = x.shape
    ts = min(INPROJ_ROWS, s)
    row = lambda bi, j: (bi, j, 0)
    const = lambda bi, j: (0, 0)
    return pl.pallas_call(
        _inproj_kernel,
        grid=(b, s // ts),
        in_specs=[pl.BlockSpec((1, ts, d), row),
                  pl.BlockSpec((1, 6, d), lambda bi, j: (bi, 0, 0)),
                  pl.BlockSpec((1, d), const),
                  pl.BlockSpec((d, D_IN), const),
                  pl.BlockSpec((1, D_IN), const)],
        out_specs=[pl.BlockSpec((1, ts, D_ATTN), row),
                   pl.BlockSpec((1, ts, D_KV), row),
                   pl.BlockSpec((1, ts, D_KV), row),
                   pl.BlockSpec((1, ts, D_SSM), row)],
        out_shape=[jax.ShapeDtypeStruct((b, s, D_ATTN), BF16),
                   jax.ShapeDtypeStruct((b, s, D_KV), BF16),
                   jax.ShapeDtypeStruct((b, s, D_KV), BF16),
                   jax.ShapeDtypeStruct((b, s, D_SSM), BF16)],
        compiler_params=_cparams(2),
        name="inproj",
    )(x, mod, gain.reshape(1, d), w_in.astype(BF16), b_in.reshape(1, D_IN))


def _half_norm(x, low):
    sq = x * x
    s_lo = jnp.sum(jnp.where(low, sq, 0.0), axis=-1, keepdims=True)
    s_hi = jnp.sum(sq, axis=-1, keepdims=True) - s_lo
    inv = 1.0 / HEAD_DIM
    scale = jnp.where(low, lax.rsqrt(s_lo * inv + EPS), lax.rsqrt(s_hi * inv + EPS))
    return x * scale


def _attn_kernel(sinks_ref, q_ref, k_ref, v_ref, qn_ref, kn_ref, on_ref, o_ref):
    n = pl.program_id(1)
    cur = pl.multiple_of(n * BLOCK, BLOCK)
    prev = pl.multiple_of(jnp.maximum(n - 1, 0) * BLOCK, BLOCK)
    low = lax.broadcasted_iota(I32, (1, LANES), 1) < HEAD_DIM

    kwin = jnp.concatenate([k_ref[0, pl.ds(prev, BLOCK), :], k_ref[0, pl.ds(cur, BLOCK), :]], axis=0).astype(F32)
    vwin = jnp.concatenate([v_ref[0, pl.ds(prev, BLOCK), :], v_ref[0, pl.ds(cur, BLOCK), :]], axis=0).astype(F32)
    kwin = _half_norm(kwin, low) * kn_ref[...]
    kswap = pltpu.roll(kwin, HEAD_DIM, axis=1)
    vswap = pltpu.roll(vwin, HEAD_DIM, axis=1)
    k_dup = [jnp.where(low, kwin, kswap).astype(BF16), jnp.where(low, kswap, kwin).astype(BF16)]
    v_dup = [jnp.where(low, vwin, vswap).astype(BF16), jnp.where(low, vswap, vwin).astype(BF16)]

    rows = Q_PER_KV * BLOCK
    qi = lax.broadcasted_iota(I32, (rows, 2 * BLOCK), 0) % BLOCK
    kj = lax.broadcasted_iota(I32, (rows, 2 * BLOCK), 1)
    diff = BLOCK + qi - kj
    valid = (diff >= 0) & (diff < WINDOW) & ((kj >= BLOCK) | (n > 0))
    rblk = lax.broadcasted_iota(I32, (rows, 1), 0) // BLOCK

    q = q_ref[0].astype(F32)
    out_blocks = []
    for hk in range(N_KV_HEADS):
        qs = []
        for j in range(Q_PER_KV // 2):
            blk = hk * (Q_PER_KV // 2) + j
            qb = _half_norm(q[:, blk * LANES:(blk + 1) * LANES], low) * qn_ref[...] * (1.0 / math.sqrt(HEAD_DIM))
            qs.append(jnp.where(low, qb, 0.0))
            qs.append(jnp.where(low, 0.0, qb))
        qg = jnp.concatenate(qs, axis=0).astype(BF16)
        s = lax.dot_general(qg, k_dup[hk], (((1,), (1,)), ((), ())), preferred_element_type=F32)
        s = jnp.where(valid, s, NEG_INF)
        sink = jnp.zeros((rows, 1), F32)
        for g in range(Q_PER_KV):
            sink = jnp.where(rblk == g, sinks_ref[hk * Q_PER_KV + g], sink)
        m = jnp.maximum(jnp.max(s, axis=-1, keepdims=True), sink)
        p = jnp.exp(s - m)
        den = jnp.sum(p, axis=-1, keepdims=True) + jnp.exp(sink - m)
        o = jnp.dot(p.astype(BF16), v_dup[hk], preferred_element_type=F32) / den
        for j in range(Q_PER_KV // 2):
            ev = o[(2 * j) * BLOCK:(2 * j + 1) * BLOCK]
            od = o[(2 * j + 1) * BLOCK:(2 * j + 2) * BLOCK]
            out_blocks.append(jnp.where(low, ev, od))
    attn = jnp.concatenate(out_blocks, axis=-1)
    o_ref[0] = (_rms(attn) * on_ref[...]).astype(BF16)


def _attention(q, k, v, sinks, q_norm, k_norm, out_norm):
    b, s, _ = q.shape
    tile2 = lambda g: jnp.tile(g.reshape(1, HEAD_DIM), (1, 2))
    return pl.pallas_call(
        _attn_kernel,
        grid=(b, s // BLOCK),
        in_specs=[pl.BlockSpec(memory_space=pltpu.SMEM),
                  pl.BlockSpec((1, BLOCK, D_ATTN), lambda bi, n: (bi, n, 0)),
                  pl.BlockSpec((1, s, D_KV), lambda bi, n: (bi, 0, 0)),
                  pl.BlockSpec((1, s, D_KV), lambda bi, n: (bi, 0, 0)),
                  pl.BlockSpec((1, LANES), lambda bi, n: (0, 0)),
                  pl.BlockSpec((1, LANES), lambda bi, n: (0, 0)),
                  pl.BlockSpec((1, D_ATTN), lambda bi, n: (0, 0))],
        out_specs=pl.BlockSpec((1, BLOCK, D_ATTN), lambda bi, n: (bi, n, 0)),
        out_shape=jax.ShapeDtypeStruct((b, s, D_ATTN), BF16),
        compiler_params=_cparams(2),
        name="attention",
    )(sinks, q, k, v, tile2(q_norm), tile2(k_norm), out_norm.reshape(1, D_ATTN))


def _cpow(rho, th, e):
    mag = jnp.exp(rho * e)
    return mag * jnp.cos(th * e), mag * jnp.sin(th * e)


def _ssm_param_kernel(lr_row, li_row, ld_row, lr_col, li_col, ld_col, bt_r, bt_i, ct_r, ct_i,
                      t_ref, wz_ref, wy_ref, cs_ref):
    lr = lr_row[0]
    li = li_row[0]
    dt = jnp.exp(ld_row[0])
    rho = lr * dt
    th = li * dt
    lb_r, lb_i = _cpow(rho, th, 1.0)
    den = lr * lr + li * li
    coef_r = ((lb_r - 1.0) * lr + lb_i * li) / den
    coef_i = (lb_i * lr - (lb_r - 1.0) * li) / den
    b_r = bt_r[0]
    b_i = bt_i[0]
    bbar_r = coef_r * b_r - coef_i * b_i
    bbar_i = coef_r * b_i + coef_i * b_r
    irow = (lax.broadcasted_iota(I32, (SSM_ROW, 1), 0) // SSM_GROUP).astype(F32)
    imag_lane = lax.broadcasted_iota(I32, (1, LANES), 1) >= STATE

    p_r, p_i = _cpow(rho, th, -irow)
    a_r = bbar_r * p_r - bbar_i * p_i
    a_i = bbar_r * p_i + bbar_i * p_r
    a2c = jnp.where(imag_lane, -a_i, a_r)

    p_r, p_i = _cpow(rho, th, (SSM_CHUNK - 1.0) - irow)
    w_r = bbar_r * p_r - bbar_i * p_i
    w_i = bbar_r * p_i + bbar_i * p_r
    wz_ref[0, :, :LANES] = jnp.where(imag_lane, w_i, w_r).astype(BF16)
    wz_ref[0, :, LANES:] = jnp.where(imag_lane, w_r, w_i).astype(BF16)

    pl_r, pl_i = _cpow(rho, th, float(SSM_CHUNK))
    cs_ref[0, 0:1, :] = pl_r
    cs_ref[0, 1:2, :] = jnp.where(imag_lane, pl_i, -pl_i)

    lrc = lr_col[0]
    lic = li_col[0]
    dtc = jnp.exp(ld_col[0])
    rhoc = lrc * dtc
    thc = lic * dtc
    jcol = (lax.broadcasted_iota(I32, (1, SSM_ROW), 1) // SSM_GROUP).astype(F32)
    imag_row = lax.broadcasted_iota(I32, (LANES, 1), 0) >= STATE
    c_r = ct_r[0]
    c_i = ct_i[0]

    p_r, p_i = _cpow(rhoc, thc, jcol)
    m_r = c_r * p_r - c_i * p_i
    m_i = c_r * p_i + c_i * p_r
    bmc = jnp.where(imag_row, m_i, m_r)
    t = jnp.dot(a2c, bmc, preferred_element_type=F32, precision=HIGHEST)
    ti = lax.broadcasted_iota(I32, (SSM_ROW, SSM_ROW), 0) // SSM_GROUP
    tj = lax.broadcasted_iota(I32, (SSM_ROW, SSM_ROW), 1) // SSM_GROUP
    t_ref[0] = jnp.where(ti <= tj, t, 0.0).astype(BF16)

    p_r, p_i = _cpow(rhoc, thc, jcol + 1.0)
    y_r = c_r * p_r - c_i * p_i
    y_i = c_r * p_i + c_i * p_r
    wy_ref[0] = jnp.where(imag_row, -y_i, y_r).astype(BF16)


def _ssm_params(lam_re, lam_im, log_dt, b_re, b_im, c_re, c_im):
    g = lam_re.shape[0]
    row2 = lambda a: jnp.tile(a.reshape(g, 1, STATE), (1, 1, 2))
    col2 = lambda a: jnp.tile(a.reshape(g, STATE, 1), (1, 2, 1))
    ld = jnp.broadcast_to(log_dt.reshape(g, 1), (g, STATE))
    bt = lambda a: jnp.tile(jnp.swapaxes(a, 1, 2), (1, SSM_CHUNK, 2))
    ct = lambda a: jnp.tile(jnp.swapaxes(a, 1, 2), (1, 2, SSM_CHUNK))
    blk = lambda *shape: pl.BlockSpec((1,) + shape, lambda i: (i, 0, 0))
    return pl.pallas_call(
        _ssm_param_kernel,
        grid=(g,),
        in_specs=[blk(1, LANES)] * 3 + [blk(LANES, 1)] * 3 + [blk(SSM_ROW, LANES)] * 2 + [blk(LANES, SSM_ROW)] * 2,
        out_specs=[blk(SSM_ROW, SSM_ROW), blk(SSM_ROW, SSM_ROW), blk(LANES, SSM_ROW), blk(2, LANES)],
        out_shape=[jax.ShapeDtypeStruct((g, SSM_ROW, SSM_ROW), BF16),
                   jax.ShapeDtypeStruct((g, SSM_ROW, SSM_ROW), BF16),
                   jax.ShapeDtypeStruct((g, LANES, SSM_ROW), BF16),
                   jax.ShapeDtypeStruct((g, 2, LANES), F32)],
        compiler_params=_cparams(1),
        name="ssm_params",
    )(row2(lam_re), row2(lam_im), row2(ld), col2(lam_re), col2(lam_im), col2(ld),
      bt(b_re), bt(b_im), ct(c_re), ct(c_im))


def _ssm_kernel(u_ref, t_ref, wz_ref, wy_ref, cs_ref, d_ref, y_ref, z_scr, s_scr, *, batch, n_chunks):
    u = u_ref[0]
    z_scr[...] = jnp.dot(u, wz_ref[0], preferred_element_type=F32)
    c1 = cs_ref[0, 0:1, :]
    c2 = cs_ref[0, 1:2, :]

    def step(c, carry):
        s1, s2 = carry
        r = pl.multiple_of(c * batch, batch)
        s_scr[pl.ds(r, batch), :] = s1
        z = z_scr[pl.ds(r, batch), :]
        n1 = c1 * s1 + c2 * s2 + z[:, :LANES]
        n2 = c1 * s2 - c2 * s1 + z[:, LANES:]
        return n1, n2

    zero = jnp.zeros((batch, LANES), F32)
    lax.fori_loop(0, n_chunks, step, (zero, zero), unroll=8)
    y = jnp.dot(u, t_ref[0], preferred_element_type=F32)
    y = y + jnp.dot(s_scr[...].astype(BF16), wy_ref[0], preferred_element_type=F32)
    y_ref[0] = y + d_ref[0] * u.astype(F32)


def _ssm(u, t_mat, wz, wy, cs, d_skip):
    b, s, _ = u.shape
    nc = s // SSM_CHUNK
    n = nc * b
    ug = u.reshape(b, nc, SSM_CHUNK, N_GROUPS, SSM_GROUP).transpose(3, 1, 0, 2, 4).reshape(N_GROUPS, n, SSM_ROW)
    d_row = jnp.tile(d_skip.reshape(N_GROUPS, 1, SSM_GROUP), (1, 1, SSM_CHUNK))
    blk = lambda *shape: pl.BlockSpec((1,) + shape, lambda i: (i, 0, 0))
    yg = pl.pallas_call(
        functools.partial(_ssm_kernel, batch=b, n_chunks=nc),
        grid=(N_GROUPS,),
        in_specs=[blk(n, SSM_ROW), blk(SSM_ROW, SSM_ROW), blk(SSM_ROW, SSM_ROW), blk(LANES, SSM_ROW),
                  blk(2, LANES), blk(1, SSM_ROW)],
        out_specs=blk(n, SSM_ROW),
        out_shape=jax.ShapeDtypeStruct((N_GROUPS, n, SSM_ROW), F32),
        scratch_shapes=[pltpu.VMEM((n, SSM_ROW), F32), pltpu.VMEM((n, LANES), F32)],
        compiler_params=_cparams(1),
        name="ssm",
    )(ug, t_mat, wz, wy, cs, d_row)
    return yg.reshape(N_GROUPS, nc, b, SSM_CHUNK, SSM_GROUP).transpose(2, 1, 3, 0, 4).reshape(b, s, D_SSM)


def _post_kernel(x_ref, attn_ref, y_ref, mod_ref, wglu_ref, bglu_ref, sn_ref, wout_ref, nf_ref, wr_ref, br_ref,
                 tri_ref, x1_ref, h2_ref, eidx_ref, wts_ref, rank_ref, cnt_ref, carry_ref):
    @pl.when((pl.program_id(0) == 0) & (pl.program_id(1) == 0))
    def _():
        carry_ref[...] = jnp.zeros_like(carry_ref)

    g = jax.nn.gelu(y_ref[0])
    gate = jax.nn.sigmoid(jnp.dot(g.astype(BF16), wglu_ref[...], preferred_element_type=F32) + bglu_ref[...])
    ssm = _rms(g * gate) * sn_ref[...]
    mixed = jnp.concatenate([attn_ref[0], ssm.astype(BF16)], axis=-1)
    o = jnp.dot(mixed, wout_ref[...], preferred_element_type=F32)
    x1 = x_ref[0] + mod_ref[0, 2:3, :] * o
    x1_ref[0] = x1
    h2 = _rms(x1) * nf_ref[...] * (1.0 + mod_ref[0, 4:5, :]) + mod_ref[0, 3:4, :]
    h2_ref[0] = _pack_halves(h2)

    logits = lax.dot_general(wr_ref[...], h2.astype(BF16), (((1,), (1,)), ((), ())),
                             preferred_element_type=F32) + br_ref[...]
    ts = logits.shape[1]
    iota_e = lax.broadcasted_iota(I32, (N_EXPERTS, ts), 0).astype(F32)
    l = logits
    idxs, vals = [], []
    for _ in range(TOP_K):
        m = jnp.max(l, axis=0, keepdims=True)
        idx = jnp.min(jnp.where(l == m, iota_e, float(N_EXPERTS)), axis=0, keepdims=True)
        idxs.append(idx)
        vals.append(m)
        l = jnp.where(iota_e == idx, -jnp.inf, l)
    es = [jnp.exp(v - vals[0]) for v in vals]
    tot = es[0] + es[1] + es[2] + es[3]
    member = jnp.zeros((N_EXPERTS, ts), F32)
    for idx in idxs:
        member = member + (iota_e == idx).astype(F32)
    before = jnp.dot(member.astype(BF16), tri_ref[...], preferred_element_type=F32) + carry_ref[...]
    for k in range(TOP_K):
        eidx_ref[k:k + 1, :] = idxs[k].astype(I32)
        wts_ref[k:k + 1, :] = es[k] / tot
        rank_ref[k:k + 1, :] = jnp.sum(jnp.where(iota_e == idxs[k], before, 0.0), axis=0, keepdims=True).astype(I32)
    carry = carry_ref[...] + jnp.sum(member, axis=1, keepdims=True)
    carry_ref[...] = carry
    cnt_ref[...] = carry.astype(I32)


def _post(x, attn, yssm, mod, w_glu, b_glu, ssm_norm, w_out, norm_ffn, w_router, b_router):
    b, s, d = x.shape
    ts = min(POST_ROWS, s)
    nt = s // ts
    t = b * s
    row = lambda bi, j: (bi, j, 0)
    const = lambda bi, j: (0, 0)
    tok = lambda bi, j: (0, bi * nt + j)
    tri = (lax.broadcasted_iota(I32, (ts, ts), 0) < lax.broadcasted_iota(I32, (ts, ts), 1)).astype(BF16)
    return pl.pallas_call(
        _post_kernel,
        grid=(b, nt),
        in_specs=[pl.BlockSpec((1, ts, d), row),
                  pl.BlockSpec((1, ts, D_ATTN), row),
                  pl.BlockSpec((1, ts, D_SSM), row),
                  pl.BlockSpec((1, 6, d), lambda bi, j: (bi, 0, 0)),
                  pl.BlockSpec((D_SSM, D_SSM), const),
                  pl.BlockSpec((1, D_SSM), const),
                  pl.BlockSpec((1, D_SSM), const),
                  pl.BlockSpec((d, d), const),
                  pl.BlockSpec((1, d), const),
                  pl.BlockSpec((N_EXPERTS, d), const),
                  pl.BlockSpec((N_EXPERTS, 1), const),
                  pl.BlockSpec((ts, ts), const)],
        out_specs=[pl.BlockSpec((1, ts, d), row),
                   pl.BlockSpec((1, ts, HALF), row),
                   pl.BlockSpec((TOP_K, ts), tok),
                   pl.BlockSpec((TOP_K, ts), tok),
                   pl.BlockSpec((TOP_K, ts), tok),
                   pl.BlockSpec((N_EXPERTS, 1), const)],
        out_shape=[jax.ShapeDtypeStruct((b, s, d), F32),
                   jax.ShapeDtypeStruct((b, s, HALF), U32),
                   jax.ShapeDtypeStruct((TOP_K, t), I32),
                   jax.ShapeDtypeStruct((TOP_K, t), F32),
                   jax.ShapeDtypeStruct((TOP_K, t), I32),
                   jax.ShapeDtypeStruct((N_EXPERTS, 1), I32)],
        scratch_shapes=[pltpu.VMEM((N_EXPERTS, 1), F32)],
        compiler_params=_cparams(2),
        name="post",
    )(x, attn, yssm, mod, w_glu.astype(BF16), b_glu.reshape(1, -1), ssm_norm.reshape(1, -1),
      w_out.astype(BF16), norm_ffn.reshape(1, -1), w_router.T.astype(BF16), b_router.reshape(-1, 1), tri)


def _route_kernel(eidx_ref, rank_ref, cnt_ref, pos_ref, te_ref, nv_ref, *, n_tiles):
    cnt = cnt_ref[...]
    tiles = (cnt + (FFN_ROWS - 1)) // FFN_ROWS
    er = lax.broadcasted_iota(I32, (N_EXPERTS, N_EXPERTS), 0)
    ec = lax.broadcasted_iota(I32, (N_EXPERTS, N_EXPERTS), 1)
    ltri = (ec < er).astype(BF16)
    tiles_b = jnp.broadcast_to(tiles.astype(F32), (N_EXPERTS, LANES)).astype(BF16)
    start_t = jnp.dot(ltri, tiles_b, preferred_element_type=F32)[:, 0:1].astype(I32)
    end_t = start_t + tiles
    start = start_t * FFN_ROWS

    t = eidx_ref.shape[1]
    iota_e = lax.broadcasted_iota(I32, (N_EXPERTS, t), 0)
    for k in range(TOP_K):
        sel = jnp.where(iota_e == eidx_ref[k:k + 1, :], start, 0)
        pos_ref[k:k + 1, :] = jnp.sum(sel, axis=0, keepdims=True) + rank_ref[k:k + 1, :]

    nv = jnp.max(end_t, axis=0, keepdims=True)
    width = te_ref.shape[1]
    ti = jnp.minimum(lax.broadcasted_iota(I32, (N_EXPERTS, width), 1), nv - 1)
    te = jnp.sum((ti >= end_t).astype(I32), axis=0, keepdims=True)
    te_ref[...] = jnp.minimum(te, N_EXPERTS - 1)
    nv_ref[...] = jnp.broadcast_to(nv, nv_ref.shape)


def _route(eidx, rank, cnt, n_tiles):
    t = eidx.shape[1]
    width = -(-n_tiles // LANES) * LANES
    return pl.pallas_call(
        functools.partial(_route_kernel, n_tiles=n_tiles),
        out_shape=[jax.ShapeDtypeStruct((TOP_K, t), I32),
                   jax.ShapeDtypeStruct((1, width), I32),
                   jax.ShapeDtypeStruct((1, LANES), I32)],
        name="route",
    )(eidx, rank, cnt)


def _dispatch_kernel(pos_ref, h_ref, xs_in_ref, xs_ref, sem):
    del xs_in_ref
    tt = h_ref.shape[0]

    def issue(t, carry):
        for k in range(TOP_K):
            pltpu.make_async_copy(h_ref.at[pl.ds(t, 1), :], xs_ref.at[pl.ds(pos_ref[k, t], 1), :], sem).start()
        return carry

    lax.fori_loop(0, tt, issue, 0, unroll=8)
    for _ in range(TOP_K):
        pltpu.make_async_copy(h_ref, xs_ref.at[pl.ds(0, tt), :], sem).wait()


def _dispatch(pos, h2p, n_rows):
    t = h2p.shape[0]
    tt = min(MOVE_ROWS, t)
    return pl.pallas_call(
        _dispatch_kernel,
        grid=(t // tt,),
        in_specs=[pl.BlockSpec((TOP_K, tt), lambda i: (0, i), memory_space=pltpu.SMEM),
                  pl.BlockSpec((tt, HALF), lambda i: (i, 0)),
                  pl.BlockSpec(memory_space=pl.ANY)],
        out_specs=pl.BlockSpec(memory_space=pl.ANY),
        out_shape=jax.ShapeDtypeStruct((n_rows, HALF), U32),
        scratch_shapes=[pltpu.SemaphoreType.DMA],
        input_output_aliases={2: 0},
        compiler_params=_cparams(1),
        name="dispatch",
    )(pos, h2p, jnp.zeros((n_rows, HALF), U32))


def _ffn_kernel(te_ref, nv_ref, xs_ref, wgu_ref, bg_ref, bu_ref, wd_ref, bd_ref, perm_ref, ys_ref,
                wg_scr, wu_scr, wd_scr):
    i = pl.program_id(0)
    valid = i < nv_ref[0]
    new_expert = (i == 0) | (te_ref[i] != te_ref[jnp.maximum(i - 1, 0)])

    @pl.when(valid & new_expert)
    def _():
        for c in range(2 * D_FF // PERM):
            w = wgu_ref[0, :, c * PERM:(c + 1) * PERM].astype(BF16)
            pw = jnp.dot(w, perm_ref[...], preferred_element_type=F32).astype(BF16)
            wg_scr[:, c * (PERM // 2):(c + 1) * (PERM // 2)] = pw[:, :PERM // 2]
            wu_scr[:, c * (PERM // 2):(c + 1) * (PERM // 2)] = pw[:, PERM // 2:]
        wd_scr[...] = wd_ref[0].astype(BF16)

    @pl.when(valid)
    def _():
        x_hi, x_lo = _unpack_halves(xs_ref[...])
        x_hi = x_hi.astype(BF16)
        x_lo = x_lo.astype(BF16)
        gate = (jnp.dot(x_hi, wg_scr[:HALF, :], preferred_element_type=F32)
                + jnp.dot(x_lo, wg_scr[HALF:, :], preferred_element_type=F32) + bg_ref[0])
        up = (jnp.dot(x_hi, wu_scr[:HALF, :], preferred_element_type=F32)
              + jnp.dot(x_lo, wu_scr[HALF:, :], preferred_element_type=F32) + bu_ref[0])
        gate = jnp.minimum(gate, SWIGLU_LIMIT)
        up = jnp.clip(up, -SWIGLU_LIMIT, SWIGLU_LIMIT)
        act = (up + 1.0) * (gate * jax.nn.sigmoid(SWIGLU_ALPHA * gate))
        y = jnp.dot(act.astype(BF16), wd_scr[...], preferred_element_type=F32) + bd_ref[0]
        ys_ref[...] = _pack_halves(y)


def _ffn(te, nv, xs, w_gate_up, bg, bu, w_down, bd, n_tiles):
    d = D_MODEL
    tile = lambda i, te, nv: (jnp.minimum(i, nv[0] - 1), 0)
    wsel = lambda i, te, nv: (te[i], 0, 0)
    r = lax.broadcasted_iota(I32, (PERM, PERM), 0)
    c = lax.broadcasted_iota(I32, (PERM, PERM), 1)
    perm = (r == jnp.where(c < PERM // 2, 2 * c, 2 * (c - PERM // 2) + 1)).astype(BF16)
    return pl.pallas_call(
        _ffn_kernel,
        grid_spec=pltpu.PrefetchScalarGridSpec(
            num_scalar_prefetch=2,
            grid=(n_tiles,),
            in_specs=[pl.BlockSpec((FFN_ROWS, HALF), tile),
                      pl.BlockSpec((1, d, 2 * D_FF), wsel),
                      pl.BlockSpec((1, 1, D_FF), wsel),
                      pl.BlockSpec((1, 1, D_FF), wsel),
                      pl.BlockSpec((1, D_FF, d), wsel),
                      pl.BlockSpec((1, 1, d), wsel),
                      pl.BlockSpec((PERM, PERM), lambda i, te, nv: (0, 0))],
            out_specs=pl.BlockSpec((FFN_ROWS, HALF), tile),
            scratch_shapes=[pltpu.VMEM((d, D_FF), BF16), pltpu.VMEM((d, D_FF), BF16), pltpu.VMEM((D_FF, d), BF16)],
        ),
        out_shape=jax.ShapeDtypeStruct(xs.shape, U32),
        input_output_aliases={2: 0},
        compiler_params=pltpu.CompilerParams(dimension_semantics=(_ARB,), vmem_limit_bytes=FFN_VMEM_BYTES),
        name="ffn",
    )(te, nv, xs, w_gate_up, bg, bu, w_down, bd, perm)


def _combine_kernel(pos_ref, x1_ref, w_ref, mod_ref, ys_ref, o_ref, ybuf, sem):
    tt = x1_ref.shape[1]

    def issue(t, carry):
        for k in range(TOP_K):
            pltpu.make_async_copy(ys_ref.at[pl.ds(pos_ref[k, t], 1), :], ybuf.at[k, pl.ds(t, 1), :], sem).start()
        return carry

    lax.fori_loop(0, tt, issue, 0, unroll=8)
    for k in range(TOP_K):
        pltpu.make_async_copy(ys_ref.at[pl.ds(0, tt), :], ybuf.at[k], sem).wait()

    acc_hi = jnp.zeros((tt, HALF), F32)
    acc_lo = jnp.zeros((tt, HALF), F32)
    for k in range(TOP_K):
        hi, lo = _unpack_halves(ybuf[k])
        wk = w_ref[:, k:k + 1]
        acc_hi = acc_hi + wk * hi
        acc_lo = acc_lo + wk * lo
    g2 = mod_ref[0, 5:6, :]
    o_ref[0, :, :HALF] = x1_ref[0, :, :HALF] + g2[:, :HALF] * acc_hi
    o_ref[0, :, HALF:] = x1_ref[0, :, HALF:] + g2[:, HALF:] * acc_lo


def _combine(pos, x1, wts_t, mod, ys):
    b, s, d = x1.shape
    tt = min(MOVE_ROWS, s)
    nt = s // tt
    return pl.pallas_call(
        _combine_kernel,
        grid=(b, nt),
        in_specs=[pl.BlockSpec((TOP_K, tt), lambda bi, j: (0, bi * nt + j), memory_space=pltpu.SMEM),
                  pl.BlockSpec((1, tt, d), lambda bi, j: (bi, j, 0)),
                  pl.BlockSpec((tt, TOP_K), lambda bi, j: (bi * nt + j, 0)),
                  pl.BlockSpec((1, 6, d), lambda bi, j: (bi, 0, 0)),
                  pl.BlockSpec(memory_space=pl.ANY)],
        out_specs=pl.BlockSpec((1, tt, d), lambda bi, j: (bi, j, 0)),
        out_shape=jax.ShapeDtypeStruct((b, s, d), F32),
        scratch_shapes=[pltpu.VMEM((TOP_K, tt, HALF), U32), pltpu.SemaphoreType.DMA],
        compiler_params=_cparams(2),
        name="combine",
    )(pos, x1, wts_t, mod, ys)


def kernel(x, c, w_ada, b_ada, norm_mix, w_in, b_in, q_norm, k_norm, sinks, lam_re, lam_im, log_dt, b_re, b_im,
           c_re, c_im, d_skip, w_glu, b_glu, attn_out_norm, ssm_out_norm, w_out, norm_ffn, w_router, b_router,
           w_gate_up, b_gate_up, w_down, b_down):
    b, s, d = x.shape
    t = b * s
    depth = w_ada.shape[0]
    n_tiles = (t * TOP_K) // FFN_ROWS + N_EXPERTS
    for l in range(depth):
        mod = _adaln(c, w_ada[l], b_ada[l]).reshape(b, 6, d)
        q, k, v, u = _inproj(x, mod, norm_mix[l], w_in[l], b_in[l])
        attn = _attention(q, k, v, sinks[l], q_norm[l], k_norm[l], attn_out_norm[l])
        t_mat, wz, wy, cs = _ssm_params(lam_re[l], lam_im[l], log_dt[l], b_re[l], b_im[l], c_re[l], c_im[l])
        yssm = _ssm(u, t_mat, wz, wy, cs, d_skip[l])
        x1, h2p, eidx, wts, rank, cnt = _post(x, attn, yssm, mod, w_glu[l], b_glu[l], ssm_out_norm[l], w_out[l],
                                              norm_ffn[l], w_router[l], b_router[l])
        pos, te, nv = _route(eidx, rank, cnt, n_tiles)
        xs = _dispatch(pos, h2p.reshape(t, HALF), n_tiles * FFN_ROWS)
        wgu = w_gate_up[l]
        bgu = b_gate_up[l]
        ys = _ffn(te[0, :n_tiles], nv[0, :1], xs, wgu, bgu[:, None, 0::2], bgu[:, None, 1::2],
                  w_down[l], b_down[l][:, None, :], n_tiles)
        x = _combine(pos, x1, wts.T, mod, ys)
    return x
```

```python
import functools
import math

import jax
import jax.numpy as jnp
from jax import lax
from jax.experimental import pallas as pl
from jax.experimental.pallas import tpu as pltpu

F32 = jnp.float32
BF16 = jnp.bfloat16
U32 = jnp.uint32
I32 = jnp.int32

D_MODEL = 1024
HEAD_DIM = 64
N_HEADS = 8
N_KV_HEADS = 2
Q_PER_KV = N_HEADS // N_KV_HEADS
D_ATTN = N_HEADS * HEAD_DIM
D_KV = N_KV_HEADS * HEAD_DIM
WINDOW = 128
BLOCK = 128
D_SSM = D_MODEL - D_ATTN
SSM_GROUP = 16
N_GROUPS = D_SSM // SSM_GROUP
STATE = 64
D_IN = D_ATTN + 2 * D_KV + D_SSM
N_EXPERTS = 32
TOP_K = 4
D_FF = D_MODEL
SWIGLU_LIMIT = 7.0
SWIGLU_ALPHA = 1.702
EPS = 1e-6
NEG_INF = -1e30

LANES = 128
SSM_CHUNK = 16
SSM_ROW = SSM_CHUNK * SSM_GROUP
HALF = D_MODEL // 2

INPROJ_ROWS = 512
POST_ROWS = 256
FFN_ROWS = 256
MOVE_ROWS = 256
PERM = 256
FFN_VMEM_BYTES = 40 * 1024 * 1024

HIGHEST = lax.Precision.HIGHEST
_ARB = "arbitrary"


def _cparams(n):
    return pltpu.CompilerParams(dimension_semantics=(_ARB,) * n)


def _rms(x, axis=-1):
    return x * lax.rsqrt(jnp.mean(x * x, axis=axis, keepdims=True) + EPS)


def _pack_halves(y):
    hi = lax.bitcast_convert_type(y[:, :HALF].astype(BF16).astype(F32), U32)
    lo = lax.bitcast_convert_type(y[:, HALF:].astype(BF16).astype(F32), U32)
    return (hi & jnp.uint32(0xFFFF0000)) | (lo >> 16)


def _unpack_halves(w):
    hi = lax.bitcast_convert_type(w & jnp.uint32(0xFFFF0000), F32)
    lo = lax.bitcast_convert_type(w << 16, F32)
    return hi, lo


def _adaln_kernel(c_ref, w_ref, b_ref, o_ref):
    c = c_ref[...]
    ca = c * jax.nn.sigmoid(c)
    o_ref[...] = jnp.dot(ca, w_ref[...], preferred_element_type=F32, precision=HIGHEST) + b_ref[...]


def _adaln(c, w_ada, b_ada):
    b, d = c.shape
    n = w_ada.shape[1] // d
    return pl.pallas_call(
        _adaln_kernel,
        grid=(n,),
        in_specs=[pl.BlockSpec((b, d), lambda j: (0, 0)),
                  pl.BlockSpec((d, d), lambda j: (0, j)),
                  pl.BlockSpec((1, d), lambda j: (0, j))],
        out_specs=pl.BlockSpec((b, d), lambda j: (0, j)),
        out_shape=jax.ShapeDtypeStruct((b, n * d), F32),
        compiler_params=_cparams(1),
        name="adaln",
    )(c, w_ada, b_ada.reshape(1, -1))


def _inproj_kernel(x_ref, mod_ref, g_ref, w_ref, b_ref, q_ref, k_ref, v_ref, u_ref):
    x = x_ref[0]
    y = _rms(x) * g_ref[...]
    h = y * (1.0 + mod_ref[0, 1:2, :]) + mod_ref[0, 0:1, :]
    proj = jnp.dot(h.astype(BF16), w_ref[...], preferred_element_type=F32) + b_ref[...]
    q_ref[0] = proj[:, :D_ATTN].astype(BF16)
    k_ref[0] = proj[:, D_ATTN:D_ATTN + D_KV].astype(BF16)
    v_ref[0] = proj[:, D_ATTN + D_KV:D_ATTN + 2 * D_KV].astype(BF16)
    u_ref[0] = proj[:, D_ATTN + 2 * D_KV:].astype(BF16)


def _inproj(x, mod, gain, w_in, b_in):
    b, s, d = x.shape
    ts = min(INPROJ_ROWS, s)
    row = lambda bi, j: (bi, j, 0)
    const = lambda bi, j: (0, 0)
    return pl.pallas_call(
        _inproj_kernel,
        grid=(b, s // ts),
        in_specs=[pl.BlockSpec((1, ts, d), row),
                  pl.BlockSpec((1, 6, d), lambda bi, j: (bi, 0, 0)),
                  pl.BlockSpec((1, d), const),
                  pl.BlockSpec((d, D_IN), const),
                  pl.BlockSpec((1, D_IN), const)],
        out_specs=[pl.BlockSpec((1, ts, D_ATTN), row),
                   pl.BlockSpec((1, ts, D_KV), row),
                   pl.BlockSpec((1, ts, D_KV), row),
                   pl.BlockSpec((1, ts, D_SSM), row)],
        out_shape=[jax.ShapeDtypeStruct((b, s, D_ATTN), BF16),
                   jax.ShapeDtypeStruct((b, s, D_KV), BF16),
                   jax.ShapeDtypeStruct((b, s, D_KV), BF16),
                   jax.ShapeDtypeStruct((b, s, D_SSM), BF16)],
        compiler_params=_cparams(2),
        name="inproj",
    )(x, mod, gain.reshape(1, d), w_in.astype(BF16), b_in.reshape(1, D_IN))


def _half_norm(x, low):
    sq = x * x
    s_lo = jnp.sum(jnp.where(low, sq, 0.0), axis=-1, keepdims=True)
    s_hi = jnp.sum(sq, axis=-1, keepdims=True) - s_lo
    inv = 1.0 / HEAD_DIM
    scale = jnp.where(low, lax.rsqrt(s_lo * inv + EPS), lax.rsqrt(s_hi * inv + EPS))
    return x * scale


def _attn_kernel(sinks_ref, q_ref, k_ref, v_ref, qn_ref, kn_ref, on_ref, o_ref):
    n = pl.program_id(1)
    cur = pl.multiple_of(n * BLOCK, BLOCK)
    prev = pl.multiple_of(jnp.maximum(n - 1, 0) * BLOCK, BLOCK)
    low = lax.broadcasted_iota(I32, (1, LANES), 1) < HEAD_DIM

    kwin = jnp.concatenate([k_ref[0, pl.ds(prev, BLOCK), :], k_ref[0, pl.ds(cur, BLOCK), :]], axis=0).astype(F32)
    vwin = jnp.concatenate([v_ref[0, pl.ds(prev, BLOCK), :], v_ref[0, pl.ds(cur, BLOCK), :]], axis=0).astype(F32)
    kwin = _half_norm(kwin, low) * kn_ref[...]
    kswap = pltpu.roll(kwin, HEAD_DIM, axis=1)
    vswap = pltpu.roll(vwin, HEAD_DIM, axis=1)
    k_dup = [jnp.where(low, kwin, kswap).astype(BF16), jnp.where(low, kswap, kwin).astype(BF16)]
    v_dup = [jnp.where(low, vwin, vswap).astype(BF16), jnp.where(low, vswap, vwin).astype(BF16)]

    rows = Q_PER_KV * BLOCK
    qi = lax.broadcasted_iota(I32, (rows, 2 * BLOCK), 0) % BLOCK
    kj = lax.broadcasted_iota(I32, (rows, 2 * BLOCK), 1)
    diff = BLOCK + qi - kj
    valid = (diff >= 0) & (diff < WINDOW) & ((kj >= BLOCK) | (n > 0))
    rblk = lax.broadcasted_iota(I32, (rows, 1), 0) // BLOCK

    q = q_ref[0].astype(F32)
    out_blocks = []
    for hk in range(N_KV_HEADS):
        qs = []
        for j in range(Q_PER_KV // 2):
            blk = hk * (Q_PER_KV // 2) + j
            qb = _half_norm(q[:, blk * LANES:(blk + 1) * LANES], low) * qn_ref[...] * (1.0 / math.sqrt(HEAD_DIM))
            qs.append(jnp.where(low, qb, 0.0))
            qs.append(jnp.where(low, 0.0, qb))
        qg = jnp.concatenate(qs, axis=0).astype(BF16)
        s = lax.dot_general(qg, k_dup[hk], (((1,), (1,)), ((), ())), preferred_element_type=F32)
        s = jnp.where(valid, s, NEG_INF)
        sink = jnp.zeros((rows, 1), F32)
        for g in range(Q_PER_KV):
            sink = jnp.where(rblk == g, sinks_ref[hk * Q_PER_KV + g], sink)
        m = jnp.maximum(jnp.max(s, axis=-1, keepdims=True), sink)
        p = jnp.exp(s - m)
        den = jnp.sum(p, axis=-1, keepdims=True) + jnp.exp(sink - m)
        o = jnp.dot(p.astype(BF16), v_dup[hk], preferred_element_type=F32) / den
        for j in range(Q_PER_KV // 2):
            ev = o[(2 * j) * BLOCK:(2 * j + 1) * BLOCK]
            od = o[(2 * j + 1) * BLOCK:(2 * j + 2) * BLOCK]
            out_blocks.append(jnp.where(low, ev, od))
    attn = jnp.concatenate(out_blocks, axis=-1)
    o_ref[0] = (_rms(attn) * on_ref[...]).astype(BF16)


def _attention(q, k, v, sinks, q_norm, k_norm, out_norm):
    b, s, _ = q.shape
    tile2 = lambda g: jnp.tile(g.reshape(1, HEAD_DIM), (1, 2))
    return pl.pallas_call(
        _attn_kernel,
        grid=(b, s // BLOCK),
        in_specs=[pl.BlockSpec(memory_space=pltpu.SMEM),
                  pl.BlockSpec((1, BLOCK, D_ATTN), lambda bi, n: (bi, n, 0)),
                  pl.BlockSpec((1, s, D_KV), lambda bi, n: (bi, 0, 0)),
                  pl.BlockSpec((1, s, D_KV), lambda bi, n: (bi, 0, 0)),
                  pl.BlockSpec((1, LANES), lambda bi, n: (0, 0)),
                  pl.BlockSpec((1, LANES), lambda bi, n: (0, 0)),
                  pl.BlockSpec((1, D_ATTN), lambda bi, n: (0, 0))],
        out_specs=pl.BlockSpec((1, BLOCK, D_ATTN), lambda bi, n: (bi, n, 0)),
        out_shape=jax.ShapeDtypeStruct((b, s, D_ATTN), BF16),
        compiler_params=_cparams(2),
        name="attention",
    )(sinks, q, k, v, tile2(q_norm), tile2(k_norm), out_norm.reshape(1, D_ATTN))


def _cpow(rho, th, e):
    mag = jnp.exp(rho * e)
    return mag * jnp.cos(th * e), mag * jnp.sin(th * e)


def _ssm_param_kernel(lr_row, li_row, ld_row, lr_col, li_col, ld_col, bt_r, bt_i, ct_r, ct_i,
                      t_ref, wz_ref, wy_ref, cs_ref):
    lr = lr_row[0]
    li = li_row[0]
    dt = jnp.exp(ld_row[0])
    rho = lr * dt
    th = li * dt
    lb_r, lb_i = _cpow(rho, th, 1.0)
    den = lr * lr + li * li
    coef_r = ((lb_r - 1.0) * lr + lb_i * li) / den
    coef_i = (lb_i * lr - (lb_r - 1.0) * li) / den
    b_r = bt_r[0]
    b_i = bt_i[0]
    bbar_r = coef_r * b_r - coef_i * b_i
    bbar_i = coef_r * b_i + coef_i * b_r
    irow = (lax.broadcasted_iota(I32, (SSM_ROW, 1), 0) // SSM_GROUP).astype(F32)
    imag_lane = lax.broadcasted_iota(I32, (1, LANES), 1) >= STATE

    p_r, p_i = _cpow(rho, th, -irow)
    a_r = bbar_r * p_r - bbar_i * p_i
    a_i = bbar_r * p_i + bbar_i * p_r
    a2c = jnp.where(imag_lane, -a_i, a_r)

    p_r, p_i = _cpow(rho, th, (SSM_CHUNK - 1.0) - irow)
    w_r = bbar_r * p_r - bbar_i * p_i
    w_i = bbar_r * p_i + bbar_i * p_r
    wz_ref[0, :, :LANES] = jnp.where(imag_lane, w_i, w_r).astype(BF16)
    wz_ref[0, :, LANES:] = jnp.where(imag_lane, w_r, w_i).astype(BF16)

    pl_r, pl_i = _cpow(rho, th, float(SSM_CHUNK))
    cs_ref[0, 0:1, :] = pl_r
    cs_ref[0, 1:2, :] = jnp.where(imag_lane, pl_i, -pl_i)

    lrc = lr_col[0]
    lic = li_col[0]
    dtc = jnp.exp(ld_col[0])
    rhoc = lrc * dtc
    thc = lic * dtc
    jcol = (lax.broadcasted_iota(I32, (1, SSM_ROW), 1) // SSM_GROUP).astype(F32)
    imag_row = lax.broadcasted_iota(I32, (LANES, 1), 0) >= STATE
    c_r = ct_r[0]
    c_i = ct_i[0]

    p_r, p_i = _cpow(rhoc, thc, jcol)
    m_r = c_r * p_r - c_i * p_i
    m_i = c_r * p_i + c_i * p_r
    bmc = jnp.where(imag_row, m_i, m_r)
    t = jnp.dot(a2c, bmc, preferred_element_type=F32, precision=HIGHEST)
    ti = lax.broadcasted_iota(I32, (SSM_ROW, SSM_ROW), 0) // SSM_GROUP
    tj = lax.broadcasted_iota(I32, (SSM_ROW, SSM_ROW), 1) // SSM_GROUP
    t_ref[0] = jnp.where(ti <= tj, t, 0.0).astype(BF16)

    p_r, p_i = _cpow(rhoc, thc, jcol + 1.0)
    y_r = c_r * p_r - c_i * p_i
    y_i = c_r * p_i + c_i * p_r
    wy_ref[0] = jnp.where(imag_row, -y_i, y_r).astype(BF16)


def _ssm_params(lam_re, lam_im, log_dt, b_re, b_im, c_re, c_im):
    g = lam_re.shape[0]
    row2 = lambda a: jnp.tile(a.reshape(g, 1, STATE), (1, 1, 2))
    col2 = lambda a: jnp.tile(a.reshape(g, STATE, 1), (1, 2, 1))
    ld = jnp.broadcast_to(log_dt.reshape(g, 1), (g, STATE))
    bt = lambda a: jnp.tile(jnp.swapaxes(a, 1, 2), (1, SSM_CHUNK, 2))
    ct = lambda a: jnp.tile(jnp.swapaxes(a, 1, 2), (1, 2, SSM_CHUNK))
    blk = lambda *shape: pl.BlockSpec((1,) + shape, lambda i: (i, 0, 0))
    return pl.pallas_call(
        _ssm_param_kernel,
        grid=(g,),
        in_specs=[blk(1, LANES)] * 3 + [blk(LANES, 1)] * 3 + [blk(SSM_ROW, LANES)] * 2 + [blk(LANES, SSM_ROW)] * 2,
        out_specs=[blk(SSM_ROW, SSM_ROW), blk(SSM_ROW, SSM_ROW), blk(LANES, SSM_ROW), blk(2, LANES)],
        out_shape=[jax.ShapeDtypeStruct((g, SSM_ROW, SSM_ROW), BF16),
                   jax.ShapeDtypeStruct((g, SSM_ROW, SSM_ROW), BF16),
                   jax.ShapeDtypeStruct((g, LANES, SSM_ROW), BF16),
                   jax.ShapeDtypeStruct((g, 2, LANES), F32)],
        compiler_params=_cparams(1),
        name="ssm_params",
    )(row2(lam_re), row2(lam_im), row2(ld), col2(lam_re), col2(lam_im), col2(ld),
      bt(b_re), bt(b_im), ct(c_re), ct(c_im))


def _ssm_kernel(u_ref, t_ref, wz_ref, wy_ref, cs_ref, d_ref, y_ref, z_scr, s_scr, *, batch, n_chunks):
    u = u_ref[0]
    z_scr[...] = jnp.dot(u, wz_ref[0], preferred_element_type=F32)
    c1 = cs_ref[0, 0:1, :]
    c2 = cs_ref[0, 1:2, :]

    def step(c, carry):
        s1, s2 = carry
        r = pl.multiple_of(c * batch, batch)
        s_scr[pl.ds(r, batch), :] = s1
        z = z_scr[pl.ds(r, batch), :]
        n1 = c1 * s1 + c2 * s2 + z[:, :LANES]
        n2 = c1 * s2 - c2 * s1 + z[:, LANES:]
        return n1, n2

    zero = jnp.zeros((batch, LANES), F32)
    lax.fori_loop(0, n_chunks, step, (zero, zero), unroll=8)
    y = jnp.dot(u, t_ref[0], preferred_element_type=F32)
    y = y + jnp.dot(s_scr[...].astype(BF16), wy_ref[0], preferred_element_type=F32)
    y_ref[0] = y + d_ref[0] * u.astype(F32)


def _ssm(u, t_mat, wz, wy, cs, d_skip):
    b, s, _ = u.shape
    nc = s // SSM_CHUNK
    n = nc * b
    ug = u.reshape(b, nc, SSM_CHUNK, N_GROUPS, SSM_GROUP).transpose(3, 1, 0, 2, 4).reshape(N_GROUPS, n, SSM_ROW)
    d_row = jnp.tile(d_skip.reshape(N_GROUPS, 1, SSM_GROUP), (1, 1, SSM_CHUNK))
    blk = lambda *shape: pl.BlockSpec((1,) + shape, lambda i: (i, 0, 0))
    yg = pl.pallas_call(
        functools.partial(_ssm_kernel, batch=b, n_chunks=nc),
        grid=(N_GROUPS,),
        in_specs=[blk(n, SSM_ROW), blk(SSM_ROW, SSM_ROW), blk(SSM_ROW, SSM_ROW), blk(LANES, SSM_ROW),
                  blk(2, LANES), blk(1, SSM_ROW)],
        out_specs=blk(n, SSM_ROW),
        out_shape=jax.ShapeDtypeStruct((N_GROUPS, n, SSM_ROW), F32),
        scratch_shapes=[pltpu.VMEM((n, SSM_ROW), F32), pltpu.VMEM((n, LANES), F32)],
        compiler_params=_cparams(1),
        name="ssm",
    )(ug, t_mat, wz, wy, cs, d_row)
    return yg.reshape(N_GROUPS, nc, b, SSM_CHUNK, SSM_GROUP).transpose(2, 1, 3, 0, 4).reshape(b, s, D_SSM)


def _post_kernel(x_ref, attn_ref, y_ref, mod_ref, wglu_ref, bglu_ref, sn_ref, wout_ref, nf_ref, wr_ref, br_ref,
                 tri_ref, x1_ref, h2_ref, eidx_ref, wts_ref, rank_ref, cnt_ref, carry_ref):
    @pl.when((pl.program_id(0) == 0) & (pl.program_id(1) == 0))
    def _():
        carry_ref[...] = jnp.zeros_like(carry_ref)

    g = jax.nn.gelu(y_ref[0])
    gate = jax.nn.sigmoid(jnp.dot(g.astype(BF16), wglu_ref[...], preferred_element_type=F32) + bglu_ref[...])
    ssm = _rms(g * gate) * sn_ref[...]
    mixed = jnp.concatenate([attn_ref[0], ssm.astype(BF16)], axis=-1)
    o = jnp.dot(mixed, wout_ref[...], preferred_element_type=F32)
    x1 = x_ref[0] + mod_ref[0, 2:3, :] * o
    x1_ref[0] = x1
    h2 = _rms(x1) * nf_ref[...] * (1.0 + mod_ref[0, 4:5, :]) + mod_ref[0, 3:4, :]
    h2_ref[0] = _pack_halves(h2)

    logits = lax.dot_general(wr_ref[...], h2.astype(BF16), (((1,), (1,)), ((), ())),
                             preferred_element_type=F32) + br_ref[...]
    ts = logits.shape[1]
    iota_e = lax.broadcasted_iota(I32, (N_EXPERTS, ts), 0).astype(F32)
    l = logits
    idxs, vals = [], []
    for _ in range(TOP_K):
        m = jnp.max(l, axis=0, keepdims=True)
        idx = jnp.min(jnp.where(l == m, iota_e, float(N_EXPERTS)), axis=0, keepdims=True)
        idxs.append(idx)
        vals.append(m)
        l = jnp.where(iota_e == idx, -jnp.inf, l)
    es = [jnp.exp(v - vals[0]) for v in vals]
    tot = es[0] + es[1] + es[2] + es[3]
    member = jnp.zeros((N_EXPERTS, ts), F32)
    for idx in idxs:
        member = member + (iota_e == idx).astype(F32)
    before = jnp.dot(member.astype(BF16), tri_ref[...], preferred_element_type=F32) + carry_ref[...]
    for k in range(TOP_K):
        eidx_ref[k:k + 1, :] = idxs[k].astype(I32)
        wts_ref[k:k + 1, :] = es[k] / tot
        rank_ref[k:k + 1, :] = jnp.sum(jnp.where(iota_e == idxs[k], before, 0.0), axis=0, keepdims=True).astype(I32)
    carry = carry_ref[...] + jnp.sum(member, axis=1, keepdims=True)
    carry_ref[...] = carry
    cnt_ref[...] = carry.astype(I32)


def _post(x, attn, yssm, mod, w_glu, b_glu, ssm_norm, w_out, norm_ffn, w_router, b_router):
    b, s, d = x.shape
    ts = min(POST_ROWS, s)
    nt = s // ts
    t = b * s
    row = lambda bi, j: (bi, j, 0)
    const = lambda bi, j: (0, 0)
    tok = lambda bi, j: (0, bi * nt + j)
    tri = (lax.broadcasted_iota(I32, (ts, ts), 0) < lax.broadcasted_iota(I32, (ts, ts), 1)).astype(BF16)
    return pl.pallas_call(
        _post_kernel,
        grid=(b, nt),
        in_specs=[pl.BlockSpec((1, ts, d), row),
                  pl.BlockSpec((1, ts, D_ATTN), row),
                  pl.BlockSpec((1, ts, D_SSM), row),
                  pl.BlockSpec((1, 6, d), lambda bi, j: (bi, 0, 0)),
                  pl.BlockSpec((D_SSM, D_SSM), const),
                  pl.BlockSpec((1, D_SSM), const),
                  pl.BlockSpec((1, D_SSM), const),
                  pl.BlockSpec((d, d), const),
                  pl.BlockSpec((1, d), const),
                  pl.BlockSpec((N_EXPERTS, d), const),
                  pl.BlockSpec((N_EXPERTS, 1), const),
                  pl.BlockSpec((ts, ts), const)],
        out_specs=[pl.BlockSpec((1, ts, d), row),
                   pl.BlockSpec((1, ts, HALF), row),
                   pl.BlockSpec((TOP_K, ts), tok),
                   pl.BlockSpec((TOP_K, ts), tok),
                   pl.BlockSpec((TOP_K, ts), tok),
                   pl.BlockSpec((N_EXPERTS, 1), const)],
        out_shape=[jax.ShapeDtypeStruct((b, s, d), F32),
                   jax.ShapeDtypeStruct((b, s, HALF), U32),
                   jax.ShapeDtypeStruct((TOP_K, t), I32),
                   jax.ShapeDtypeStruct((TOP_K, t), F32),
                   jax.ShapeDtypeStruct((TOP_K, t), I32),
                   jax.ShapeDtypeStruct((N_EXPERTS, 1), I32)],
        scratch_shapes=[pltpu.VMEM((N_EXPERTS, 1), F32)],
        compiler_params=_cparams(2),
        name="post",
    )(x, attn, yssm, mod, w_glu.astype(BF16), b_glu.reshape(1, -1), ssm_norm.reshape(1, -1),
      w_out.astype(BF16), norm_ffn.reshape(1, -1), w_router.T.astype(BF16), b_router.reshape(-1, 1), tri)


def _route_kernel(eidx_ref, rank_ref, cnt_ref, pos_ref, te_ref, nv_ref, nx_ref):
    cnt = cnt_ref[...]
    tiles = (cnt + (FFN_ROWS - 1)) // FFN_ROWS
    er = lax.broadcasted_iota(I32, (N_EXPERTS, N_EXPERTS), 0)
    ec = lax.broadcasted_iota(I32, (N_EXPERTS, N_EXPERTS), 1)
    ltri = (ec < er).astype(BF16)
    tiles_b = jnp.broadcast_to(tiles.astype(F32), (N_EXPERTS, LANES)).astype(BF16)
    start_t = jnp.dot(ltri, tiles_b, preferred_element_type=F32)[:, 0:1].astype(I32)
    end_t = start_t + tiles
    start = start_t * FFN_ROWS

    t = eidx_ref.shape[1]
    iota_e = lax.broadcasted_iota(I32, (N_EXPERTS, t), 0)
    for k in range(TOP_K):
        sel = jnp.where(iota_e == eidx_ref[k:k + 1, :], start, 0)
        pos_ref[k:k + 1, :] = jnp.sum(sel, axis=0, keepdims=True) + rank_ref[k:k + 1, :]

    nv = jnp.max(end_t, axis=0, keepdims=True)
    width = te_ref.shape[1]
    ti = jnp.minimum(lax.broadcasted_iota(I32, (N_EXPERTS, width), 1), nv - 1)
    te = jnp.minimum(jnp.sum((ti >= end_t).astype(I32), axis=0, keepdims=True), N_EXPERTS - 1)
    te_ref[...] = te
    nv_ref[...] = jnp.broadcast_to(nv, nv_ref.shape)
    ie = lax.broadcasted_iota(I32, (N_EXPERTS, width), 0)
    own_end = jnp.sum(jnp.where(ie == te, end_t, 0), axis=0, keepdims=True)
    nxt = jnp.minimum(jnp.sum((own_end >= end_t).astype(I32), axis=0, keepdims=True), N_EXPERTS - 1)
    nx_ref[...] = jnp.where(own_end < nv, nxt, -1)


def _route(eidx, rank, cnt, n_tiles):
    t = eidx.shape[1]
    width = -(-n_tiles // LANES) * LANES
    return pl.pallas_call(
        _route_kernel,
        out_shape=[jax.ShapeDtypeStruct((TOP_K, t), I32),
                   jax.ShapeDtypeStruct((1, width), I32),
                   jax.ShapeDtypeStruct((1, LANES), I32),
                   jax.ShapeDtypeStruct((1, width), I32)],
        name="route",
    )(eidx, rank, cnt)


def _dispatch_kernel(pos_ref, h_ref, xs_in_ref, xs_ref, sem):
    del xs_in_ref
    tt = h_ref.shape[0]

    def issue(t, carry):
        for k in range(TOP_K):
            pltpu.make_async_copy(h_ref.at[pl.ds(t, 1), :], xs_ref.at[pl.ds(pos_ref[k, t], 1), :],
                                  sem).start(priority=k % 2)
        return carry

    lax.fori_loop(0, tt, issue, 0, unroll=8)
    for _ in range(TOP_K):
        pltpu.make_async_copy(h_ref, xs_ref.at[pl.ds(0, tt), :], sem).wait()


def _dispatch(pos, h2p, n_rows):
    t = h2p.shape[0]
    tt = min(MOVE_ROWS, t)
    return pl.pallas_call(
        _dispatch_kernel,
        grid=(t // tt,),
        in_specs=[pl.BlockSpec((TOP_K, tt), lambda i: (0, i), memory_space=pltpu.SMEM),
                  pl.BlockSpec((tt, HALF), lambda i: (i, 0)),
                  pl.BlockSpec(memory_space=pl.ANY)],
        out_specs=pl.BlockSpec(memory_space=pl.ANY),
        out_shape=jax.ShapeDtypeStruct((n_rows, HALF), U32),
        scratch_shapes=[pltpu.SemaphoreType.DMA],
        input_output_aliases={2: 0},
        compiler_params=_cparams(1),
        name="dispatch",
    )(pos, h2p, jnp.zeros((n_rows, HALF), U32))


def _ffn_kernel(te_ref, nv_ref, nx_ref, xs_ref, wgu_hbm, bg_ref, bu_ref, wd_hbm, bd_ref, perm_ref, ys_ref,
                wgu_stage, wd_stage, wg_scr, wu_scr, wd_scr, sems):
    i = pl.program_id(0)
    valid = i < nv_ref[0]
    new_expert = (i == 0) | (te_ref[i] != te_ref[jnp.maximum(i - 1, 0)])

    def stage_copies(e):
        return (pltpu.make_async_copy(wgu_hbm.at[e], wgu_stage, sems.at[0]),
                pltpu.make_async_copy(wd_hbm.at[e], wd_stage, sems.at[1]))

    @pl.when(valid & new_expert)
    def _():
        e = te_ref[i]

        @pl.when(i == 0)
        def _():
            for cp in stage_copies(e):
                cp.start()

        for cp in stage_copies(e):
            cp.wait()
        for c in range(2 * D_FF // PERM):
            w = wgu_stage[:, c * PERM:(c + 1) * PERM].astype(BF16)
            pw = jnp.dot(w, perm_ref[...], preferred_element_type=F32).astype(BF16)
            wg_scr[:, c * (PERM // 2):(c + 1) * (PERM // 2)] = pw[:, :PERM // 2]
            wu_scr[:, c * (PERM // 2):(c + 1) * (PERM // 2)] = pw[:, PERM // 2:]
        wd_scr[...] = wd_stage[...].astype(BF16)

        @pl.when(nx_ref[i] >= 0)
        def _():
            for cp in stage_copies(nx_ref[i]):
                cp.start()

    @pl.when(valid)
    def _():
        x_hi, x_lo = _unpack_halves(xs_ref[...])
        x_hi = x_hi.astype(BF16)
        x_lo = x_lo.astype(BF16)
        gate = (jnp.dot(x_hi, wg_scr[:HALF, :], preferred_element_type=F32)
                + jnp.dot(x_lo, wg_scr[HALF:, :], preferred_element_type=F32) + bg_ref[0])
        up = (jnp.dot(x_hi, wu_scr[:HALF, :], preferred_element_type=F32)
              + jnp.dot(x_lo, wu_scr[HALF:, :], preferred_element_type=F32) + bu_ref[0])
        gate = jnp.minimum(gate, SWIGLU_LIMIT)
        up = jnp.clip(up, -SWIGLU_LIMIT, SWIGLU_LIMIT)
        act = (up + 1.0) * (gate * jax.nn.sigmoid(SWIGLU_ALPHA * gate))
        y = jnp.dot(act.astype(BF16), wd_scr[...], preferred_element_type=F32) + bd_ref[0]
        ys_ref[...] = _pack_halves(y)


def _ffn(te, nv, nx, xs, w_gate_up, bg, bu, w_down, bd, n_tiles):
    d = D_MODEL
    tile = lambda i, te, nv, nx: (jnp.minimum(i, nv[0] - 1), 0)
    wsel = lambda i, te, nv, nx: (te[i], 0, 0)
    r = lax.broadcasted_iota(I32, (PERM, PERM), 0)
    c = lax.broadcasted_iota(I32, (PERM, PERM), 1)
    perm = (r == jnp.where(c < PERM // 2, 2 * c, 2 * (c - PERM // 2) + 1)).astype(BF16)
    return pl.pallas_call(
        _ffn_kernel,
        grid_spec=pltpu.PrefetchScalarGridSpec(
            num_scalar_prefetch=3,
            grid=(n_tiles,),
            in_specs=[pl.BlockSpec((FFN_ROWS, HALF), tile),
                      pl.BlockSpec(memory_space=pl.ANY),
                      pl.BlockSpec((1, 1, D_FF), wsel),
                      pl.BlockSpec((1, 1, D_FF), wsel),
                      pl.BlockSpec(memory_space=pl.ANY),
                      pl.BlockSpec((1, 1, d), wsel),
                      pl.BlockSpec((PERM, PERM), lambda i, te, nv, nx: (0, 0))],
            out_specs=pl.BlockSpec((FFN_ROWS, HALF), tile),
            scratch_shapes=[pltpu.VMEM((d, 2 * D_FF), F32), pltpu.VMEM((D_FF, d), F32),
                            pltpu.VMEM((d, D_FF), BF16), pltpu.VMEM((d, D_FF), BF16), pltpu.VMEM((D_FF, d), BF16),
                            pltpu.SemaphoreType.DMA((2,))],
        ),
        out_shape=jax.ShapeDtypeStruct(xs.shape, U32),
        input_output_aliases={3: 0},
        compiler_params=pltpu.CompilerParams(dimension_semantics=(_ARB,), vmem_limit_bytes=FFN_VMEM_BYTES),
        name="ffn",
    )(te, nv, nx, xs, w_gate_up, bg, bu, w_down, bd, perm)


def _combine_kernel(pos_ref, x1_ref, w_ref, mod_ref, ys_ref, o_ref, ybuf, sem):
    tt = x1_ref.shape[1]

    def issue(t, carry):
        for k in range(TOP_K):
            pltpu.make_async_copy(ys_ref.at[pl.ds(pos_ref[k, t], 1), :], ybuf.at[k, pl.ds(t, 1), :],
                                  sem).start(priority=k % 2)
        return carry

    lax.fori_loop(0, tt, issue, 0, unroll=8)
    for k in range(TOP_K):
        pltpu.make_async_copy(ys_ref.at[pl.ds(0, tt), :], ybuf.at[k], sem).wait()

    acc_hi = jnp.zeros((tt, HALF), F32)
    acc_lo = jnp.zeros((tt, HALF), F32)
    for k in range(TOP_K):
        hi, lo = _unpack_halves(ybuf[k])
        wk = w_ref[:, k:k + 1]
        acc_hi = acc_hi + wk * hi
        acc_lo = acc_lo + wk * lo
    g2 = mod_ref[0, 5:6, :]
    o_ref[0, :, :HALF] = x1_ref[0, :, :HALF] + g2[:, :HALF] * acc_hi
    o_ref[0, :, HALF:] = x1_ref[0, :, HALF:] + g2[:, HALF:] * acc_lo


def _combine(pos, x1, wts_t, mod, ys):
    b, s, d = x1.shape
    tt = min(MOVE_ROWS, s)
    nt = s // tt
    return pl.pallas_call(
        _combine_kernel,
        grid=(b, nt),
        in_specs=[pl.BlockSpec((TOP_K, tt), lambda bi, j: (0, bi * nt + j), memory_space=pltpu.SMEM),
                  pl.BlockSpec((1, tt, d), lambda bi, j: (bi, j, 0)),
                  pl.BlockSpec((tt, TOP_K), lambda bi, j: (bi * nt + j, 0)),
                  pl.BlockSpec((1, 6, d), lambda bi, j: (bi, 0, 0)),
                  pl.BlockSpec(memory_space=pl.ANY)],
        out_specs=pl.BlockSpec((1, tt, d), lambda bi, j: (bi, j, 0)),
        out_shape=jax.ShapeDtypeStruct((b, s, d), F32),
        scratch_shapes=[pltpu.VMEM((TOP_K, tt, HALF), U32), pltpu.SemaphoreType.DMA],
        compiler_params=_cparams(2),
        name="combine",
    )(pos, x1, wts_t, mod, ys)


def kernel(x, c, w_ada, b_ada, norm_mix, w_in, b_in, q_norm, k_norm, sinks, lam_re, lam_im, log_dt, b_re, b_im,
           c_re, c_im, d_skip, w_glu, b_glu, attn_out_norm, ssm_out_norm, w_out, norm_ffn, w_router, b_router,
           w_gate_up, b_gate_up, w_down, b_down):
    b, s, d = x.shape
    t = b * s
    depth = w_ada.shape[0]
    n_tiles = (t * TOP_K) // FFN_ROWS + N_EXPERTS
    for l in range(depth):
        mod = _adaln(c, w_ada[l], b_ada[l]).reshape(b, 6, d)
        q, k, v, u = _inproj(x, mod, norm_mix[l], w_in[l], b_in[l])
        attn = _attention(q, k, v, sinks[l], q_norm[l], k_norm[l], attn_out_norm[l])
        t_mat, wz, wy, cs = _ssm_params(lam_re[l], lam_im[l], log_dt[l], b_re[l], b_im[l], c_re[l], c_im[l])
        yssm = _ssm(u, t_mat, wz, wy, cs, d_skip[l])
        x1, h2p, eidx, wts, rank, cnt = _post(x, attn, yssm, mod, w_glu[l], b_glu[l], ssm_out_norm[l], w_out[l],
                                              norm_ffn[l], w_router[l], b_router[l])
        pos, te, nv, nx = _route(eidx, rank, cnt, n_tiles)
        xs = _dispatch(pos, h2p.reshape(t, HALF), n_tiles * FFN_ROWS)
        wgu = w_gate_up[l]
        bgu = b_gate_up[l]
        ys = _ffn(te[0, :n_tiles], nv[0, :1], nx[0, :n_tiles], xs, wgu, bgu[:, None, 0::2], bgu[:, None, 1::2],
                  w_down[l], b_down[l][:, None, :], n_tiles)
        x = _combine(pos, x1, wts.T, mod, ys)
    return x
```

```python
import functools
import math

import jax
import jax.numpy as jnp
from jax import lax
from jax.experimental import pallas as pl
from jax.experimental.pallas import tpu as pltpu

F32 = jnp.float32
BF16 = jnp.bfloat16
U32 = jnp.uint32
I32 = jnp.int32

D_MODEL = 1024
HEAD_DIM = 64
N_HEADS = 8
N_KV_HEADS = 2
Q_PER_KV = N_HEADS // N_KV_HEADS
D_ATTN = N_HEADS * HEAD_DIM
D_KV = N_KV_HEADS * HEAD_DIM
D_QKV = D_ATTN + 2 * D_KV
WINDOW = 128
BLOCK = 128
D_SSM = D_MODEL - D_ATTN
SSM_GROUP = 16
N_GROUPS = D_SSM // SSM_GROUP
STATE = 64
N_EXPERTS = 32
TOP_K = 4
D_FF = D_MODEL
SWIGLU_LIMIT = 7.0
SWIGLU_ALPHA = 1.702
EPS = 1e-6
NEG_INF = -1e30

LANES = 128
SSM_CHUNK = 16
SSM_ROW = SSM_CHUNK * SSM_GROUP
N_POW = 2 * SSM_CHUNK
HALF = D_MODEL // 2

POS_PER_STEP = 4
ATTN_ROWS = 256
POST_POS = 2
FFN_ROWS = 256
MOVE_ROWS = 256
PERM = 256
FFN_VMEM_BYTES = 40 * 1024 * 1024
ROW_VMEM_BYTES = 48 * 1024 * 1024

HIGHEST = lax.Precision.HIGHEST
_ARB = "arbitrary"


def _cparams(n, vmem=None):
    return pltpu.CompilerParams(dimension_semantics=(_ARB,) * n, vmem_limit_bytes=vmem)


def _rms(x, axis=-1):
    return x * lax.rsqrt(jnp.mean(x * x, axis=axis, keepdims=True) + EPS)


def _pack_halves(y):
    hi = lax.bitcast_convert_type(y[:, :HALF].astype(BF16).astype(F32), U32)
    lo = lax.bitcast_convert_type(y[:, HALF:].astype(BF16).astype(F32), U32)
    return (hi & jnp.uint32(0xFFFF0000)) | (lo >> 16)


def _unpack_halves(w):
    hi = lax.bitcast_convert_type(w & jnp.uint32(0xFFFF0000), F32)
    lo = lax.bitcast_convert_type(w << 16, F32)
    return hi, lo


def _to_lane_blocks(dst, src):
    for kb in range(dst.shape[0]):
        dst[kb] = src[:, kb * LANES:(kb + 1) * LANES]


def _pos_rows(blocks, i, n_chunks):
    return jnp.concatenate([blocks[kb, pl.ds(i, n_chunks, stride=SSM_CHUNK), :] for kb in range(blocks.shape[0])],
                           axis=-1)


def _adaln_kernel(c_ref, w_ref, b_ref, o_ref):
    c = c_ref[...]
    ca = c * jax.nn.sigmoid(c)
    o_ref[...] = jnp.dot(ca, w_ref[...], preferred_element_type=F32, precision=HIGHEST) + b_ref[...]


def _adaln(c, w_ada, b_ada):
    b, d = c.shape
    n = w_ada.shape[1] // d
    return pl.pallas_call(
        _adaln_kernel,
        grid=(n,),
        in_specs=[pl.BlockSpec((b, d), lambda j: (0, 0)),
                  pl.BlockSpec((d, d), lambda j: (0, j)),
                  pl.BlockSpec((1, d), lambda j: (0, j))],
        out_specs=pl.BlockSpec((b, d), lambda j: (0, j)),
        out_shape=jax.ShapeDtypeStruct((b, n * d), F32),
        compiler_params=_cparams(1),
        name="adaln",
    )(c, w_ada, b_ada.reshape(1, -1))


def _inproj_kernel(x_ref, mod_ref, g_ref, wqkv_ref, bqkv_ref, wut_ref, but_ref, q_ref, k_ref, v_ref, ut_ref, xb_scr):
    j = pl.program_id(1)
    nc = ut_ref.shape[3]

    @pl.when(j == 0)
    def _():
        _to_lane_blocks(xb_scr, x_ref[0])

    rows = POS_PER_STEP * nc
    gain = g_ref[...]
    scale = 1.0 + mod_ref[0, 1:2, :]
    shift = mod_ref[0, 0:1, :]

    def norm_mod(x):
        return (_rms(x) * gain * scale + shift).astype(BF16)

    h = norm_mod(x_ref[0, pl.ds(pl.multiple_of(j * rows, rows), rows), :])
    proj = jnp.dot(h, wqkv_ref[...], preferred_element_type=F32) + bqkv_ref[...]
    q_ref[0] = proj[:, :D_ATTN].astype(BF16)
    k_ref[0] = proj[:, D_ATTN:D_ATTN + D_KV].astype(BF16)
    v_ref[0] = proj[:, D_ATTN + D_KV:].astype(BF16)

    hs = jnp.concatenate([norm_mod(_pos_rows(xb_scr, POS_PER_STEP * j + il, nc)) for il in range(POS_PER_STEP)],
                         axis=0)
    ut = lax.dot_general(wut_ref[...], hs, (((1,), (1,)), ((), ())), preferred_element_type=F32) + but_ref[...]
    for il in range(POS_PER_STEP):
        piece = ut[:, il * nc:(il + 1) * nc].astype(BF16)
        ut_ref[0, :, il * SSM_GROUP:(il + 1) * SSM_GROUP, :] = piece.reshape(N_GROUPS, SSM_GROUP, nc)


def _inproj(x, mod, gain, w_in, b_in):
    b, s, d = x.shape
    nc = s // SSM_CHUNK
    rows = POS_PER_STEP * nc
    row = lambda bi, j: (bi, j, 0)
    const = lambda bi, j: (0, 0)
    w_qkv = w_in[:, :D_QKV].astype(BF16)
    w_ut = w_in[:, D_QKV:].T.astype(BF16)
    return pl.pallas_call(
        _inproj_kernel,
        grid=(b, SSM_CHUNK // POS_PER_STEP),
        in_specs=[pl.BlockSpec((1, s, d), lambda bi, j: (bi, 0, 0)),
                  pl.BlockSpec((1, 6, d), lambda bi, j: (bi, 0, 0)),
                  pl.BlockSpec((1, d), const),
                  pl.BlockSpec((d, D_QKV), const),
                  pl.BlockSpec((1, D_QKV), const),
                  pl.BlockSpec((D_SSM, d), const),
                  pl.BlockSpec((D_SSM, 1), const)],
        out_specs=[pl.BlockSpec((1, rows, D_ATTN), row),
                   pl.BlockSpec((1, rows, D_KV), row),
                   pl.BlockSpec((1, rows, D_KV), row),
                   pl.BlockSpec((1, N_GROUPS, POS_PER_STEP * SSM_GROUP, nc), lambda bi, j: (bi, 0, j, 0))],
        out_shape=[jax.ShapeDtypeStruct((b, s, D_ATTN), BF16),
                   jax.ShapeDtypeStruct((b, s, D_KV), BF16),
                   jax.ShapeDtypeStruct((b, s, D_KV), BF16),
                   jax.ShapeDtypeStruct((b, N_GROUPS, SSM_ROW, nc), BF16)],
        scratch_shapes=[pltpu.VMEM((d // LANES, s, LANES), F32)],
        compiler_params=_cparams(2, ROW_VMEM_BYTES),
        name="inproj",
    )(x, mod, gain.reshape(1, d), w_qkv, b_in[:D_QKV].reshape(1, D_QKV), w_ut, b_in[D_QKV:].reshape(D_SSM, 1))


def _half_norm(x, low):
    sq = x * x
    s_lo = jnp.sum(jnp.where(low, sq, 0.0), axis=-1, keepdims=True)
    s_hi = jnp.sum(sq, axis=-1, keepdims=True) - s_lo
    inv = 1.0 / HEAD_DIM
    scale = jnp.where(low, lax.rsqrt(s_lo * inv + EPS), lax.rsqrt(s_hi * inv + EPS))
    return x * scale


def _attn_block(n, q, k_ref, v_ref, sinks_ref, qn, kn, low, valid_band, rblk):
    cur = pl.multiple_of(n * BLOCK, BLOCK)
    prev = pl.multiple_of(jnp.maximum(n - 1, 0) * BLOCK, BLOCK)
    kwin = jnp.concatenate([k_ref[0, pl.ds(prev, BLOCK), :], k_ref[0, pl.ds(cur, BLOCK), :]], axis=0).astype(F32)
    vwin = jnp.concatenate([v_ref[0, pl.ds(prev, BLOCK), :], v_ref[0, pl.ds(cur, BLOCK), :]], axis=0).astype(F32)
    kwin = _half_norm(kwin, low) * kn
    kswap = pltpu.roll(kwin, HEAD_DIM, axis=1)
    vswap = pltpu.roll(vwin, HEAD_DIM, axis=1)
    k_dup = [jnp.where(low, kwin, kswap).astype(BF16), jnp.where(low, kswap, kwin).astype(BF16)]
    v_dup = [jnp.where(low, vwin, vswap).astype(BF16), jnp.where(low, vswap, vwin).astype(BF16)]
    kj = lax.broadcasted_iota(I32, valid_band.shape, 1)
    valid = valid_band & ((kj >= BLOCK) | (n > 0))

    out_blocks = []
    for hk in range(N_KV_HEADS):
        qs = []
        for j in range(Q_PER_KV // 2):
            blk = hk * (Q_PER_KV // 2) + j
            qb = _half_norm(q[:, blk * LANES:(blk + 1) * LANES], low) * qn * (1.0 / math.sqrt(HEAD_DIM))
            qs.append(jnp.where(low, qb, 0.0))
            qs.append(jnp.where(low, 0.0, qb))
        qg = jnp.concatenate(qs, axis=0).astype(BF16)
        s = lax.dot_general(qg, k_dup[hk], (((1,), (1,)), ((), ())), preferred_element_type=F32)
        s = jnp.where(valid, s, NEG_INF)
        sink = jnp.zeros((Q_PER_KV * BLOCK, 1), F32)
        for g in range(Q_PER_KV):
            sink = jnp.where(rblk == g, sinks_ref[hk * Q_PER_KV + g], sink)
        m = jnp.maximum(jnp.max(s, axis=-1, keepdims=True), sink)
        p = jnp.exp(s - m)
        den = jnp.sum(p, axis=-1, keepdims=True) + jnp.exp(sink - m)
        o = jnp.dot(p.astype(BF16), v_dup[hk], preferred_element_type=F32) / den
        for j in range(Q_PER_KV // 2):
            ev = o[(2 * j) * BLOCK:(2 * j + 1) * BLOCK]
            od = o[(2 * j + 1) * BLOCK:(2 * j + 2) * BLOCK]
            out_blocks.append(jnp.where(low, ev, od))
    return jnp.concatenate(out_blocks, axis=-1)


def _attn_kernel(sinks_ref, q_ref, k_ref, v_ref, qn_ref, kn_ref, on_ref, o_ref, a_scr):
    step = pl.program_id(1)
    low = lax.broadcasted_iota(I32, (1, LANES), 1) < HEAD_DIM
    rows = Q_PER_KV * BLOCK
    qi = lax.broadcasted_iota(I32, (rows, 2 * BLOCK), 0) % BLOCK
    kj = lax.broadcasted_iota(I32, (rows, 2 * BLOCK), 1)
    diff = BLOCK + qi - kj
    valid_band = (diff >= 0) & (diff < WINDOW)
    rblk = lax.broadcasted_iota(I32, (rows, 1), 0) // BLOCK
    for qb in range(ATTN_ROWS // BLOCK):
        q = q_ref[0, qb * BLOCK:(qb + 1) * BLOCK, :].astype(F32)
        attn = _attn_block(step * (ATTN_ROWS // BLOCK) + qb, q, k_ref, v_ref, sinks_ref, qn_ref[...], kn_ref[...],
                           low, valid_band, rblk)
        attn = _rms(attn) * on_ref[...]
        for kb in range(D_ATTN // LANES):
            a_scr[kb, qb * BLOCK:(qb + 1) * BLOCK, :] = attn[:, kb * LANES:(kb + 1) * LANES]
    for i in range(SSM_CHUNK):
        o_ref[0, i] = _pos_rows(a_scr, i, ATTN_ROWS // SSM_CHUNK).astype(BF16)


def _attention(q, k, v, sinks, q_norm, k_norm, out_norm):
    b, s, _ = q.shape
    tile2 = lambda g: jnp.tile(g.reshape(1, HEAD_DIM), (1, 2))
    cps = ATTN_ROWS // SSM_CHUNK
    return pl.pallas_call(
        _attn_kernel,
        grid=(b, s // ATTN_ROWS),
        in_specs=[pl.BlockSpec(memory_space=pltpu.SMEM),
                  pl.BlockSpec((1, ATTN_ROWS, D_ATTN), lambda bi, n: (bi, n, 0)),
                  pl.BlockSpec((1, s, D_KV), lambda bi, n: (bi, 0, 0)),
                  pl.BlockSpec((1, s, D_KV), lambda bi, n: (bi, 0, 0)),
                  pl.BlockSpec((1, LANES), lambda bi, n: (0, 0)),
                  pl.BlockSpec((1, LANES), lambda bi, n: (0, 0)),
                  pl.BlockSpec((1, D_ATTN), lambda bi, n: (0, 0))],
        out_specs=pl.BlockSpec((1, SSM_CHUNK, cps, D_ATTN), lambda bi, n: (bi, 0, n, 0)),
        out_shape=jax.ShapeDtypeStruct((b, SSM_CHUNK, s // SSM_CHUNK, D_ATTN), BF16),
        scratch_shapes=[pltpu.VMEM((D_ATTN // LANES, ATTN_ROWS, LANES), F32)],
        compiler_params=_cparams(2),
        name="attention",
    )(sinks, q, k, v, tile2(q_norm), tile2(k_norm), out_norm.reshape(1, D_ATTN))


def _cmul(ar, ai, br, bi):
    return ar * br - ai * bi, ar * bi + ai * br


def _ssm_param_kernel(lam_ref, bre_ref, bim_ref, cre_ref, cim_ref, tt_ref, wz_ref, wyt_ref, cs_ref):
    f32dot = functools.partial(jnp.dot, preferred_element_type=F32, precision=HIGHEST)
    lr = lam_ref[0, 0:1, :]
    li = lam_ref[0, 1:2, :]
    dt = jnp.exp(lam_ref[0, 2:3, :])
    rho = lr * dt
    th = li * dt
    imag_lane = lax.broadcasted_iota(I32, (1, LANES), 1) >= STATE

    kk = (lax.broadcasted_iota(I32, (N_POW, 1), 0) - (SSM_CHUNK - 1)).astype(F32)
    mag = jnp.exp(rho * kk)
    pw_r = mag * jnp.cos(th * kk)
    pw_i = mag * jnp.sin(th * kk)
    lb_r = pw_r[SSM_CHUNK:SSM_CHUNK + 1]
    lb_i = pw_i[SSM_CHUNK:SSM_CHUNK + 1]
    den = lr * lr + li * li
    coef_r = ((lb_r - 1.0) * lr + lb_i * li) / den
    coef_i = (lb_i * lr - (lb_r - 1.0) * li) / den

    rows = lax.broadcasted_iota(I32, (SSM_ROW, 1), 0)
    pos = rows // SSM_GROUP
    chan = rows % SSM_GROUP
    ch_sel = (chan == lax.broadcasted_iota(I32, (1, SSM_GROUP), 1)).astype(F32)
    lane_fold = (lax.broadcasted_iota(I32, (STATE, LANES), 1) % STATE
                 == lax.broadcasted_iota(I32, (STATE, LANES), 0)).astype(F32)
    tab = lax.broadcasted_iota(I32, (1, N_POW), 1)

    def power_rows(k_of_row):
        sel = (tab == k_of_row + (SSM_CHUNK - 1)).astype(F32)
        return f32dot(sel, pw_r), f32dot(sel, pw_i)

    def b_rows(b_ref):
        b2 = jnp.concatenate([b_ref[0], b_ref[0]], axis=0)
        return lax.dot_general(ch_sel, b2, (((1,), (1,)), ((), ())), preferred_element_type=F32, precision=HIGHEST)

    def c_rows(c_ref):
        return f32dot(ch_sel, f32dot(c_ref[0], lane_fold))

    bbar_r, bbar_i = _cmul(coef_r, coef_i, b_rows(bre_ref), b_rows(bim_ref))
    c_r = c_rows(cre_ref)
    c_i = c_rows(cim_ref)

    a_r, a_i = _cmul(bbar_r, bbar_i, *power_rows(-pos))
    a2c = jnp.where(imag_lane, -a_i, a_r)
    m_r, m_i = _cmul(c_r, c_i, *power_rows(pos))
    bmc = jnp.where(imag_lane, m_i, m_r)
    tt = f32dot(bmc, a2c.T)
    causal = pos >= lax.broadcasted_iota(I32, (1, SSM_ROW), 1) // SSM_GROUP
    tt_ref[0] = jnp.where(causal, tt, 0.0).astype(BF16)

    w_r, w_i = _cmul(bbar_r, bbar_i, *power_rows((SSM_CHUNK - 1) - pos))
    wz_ref[0, :, :LANES] = jnp.where(imag_lane, w_i, w_r).astype(BF16)
    wz_ref[0, :, LANES:] = jnp.where(imag_lane, w_r, w_i).astype(BF16)

    y_r, y_i = _cmul(c_r, c_i, *power_rows(pos + 1))
    wyt_ref[0] = jnp.where(imag_lane, -y_i, y_r).astype(BF16)

    cs_ref[0, 0:1, :] = pw_r[N_POW - 1:N_POW]
    cs_ref[0, 1:2, :] = jnp.where(imag_lane, pw_i[N_POW - 1:N_POW], -pw_i[N_POW - 1:N_POW])


def _ssm_params(lam_re, lam_im, log_dt, b_re, b_im, c_re, c_im):
    g = lam_re.shape[0]
    lam = jnp.stack([lam_re, lam_im, jnp.broadcast_to(log_dt[:, None], (g, STATE))], axis=1)
    lam = jnp.concatenate([lam, lam], axis=2)
    blk = lambda *shape: pl.BlockSpec((1,) + shape, lambda i: (i, 0, 0))
    return pl.pallas_call(
        _ssm_param_kernel,
        grid=(g,),
        in_specs=[blk(3, LANES), blk(STATE, SSM_GROUP), blk(STATE, SSM_GROUP), blk(SSM_GROUP, STATE),
                  blk(SSM_GROUP, STATE)],
        out_specs=[blk(SSM_ROW, SSM_ROW), blk(SSM_ROW, SSM_ROW), blk(SSM_ROW, LANES), blk(2, LANES)],
        out_shape=[jax.ShapeDtypeStruct((g, SSM_ROW, SSM_ROW), BF16),
                   jax.ShapeDtypeStruct((g, SSM_ROW, SSM_ROW), BF16),
                   jax.ShapeDtypeStruct((g, SSM_ROW, LANES), BF16),
                   jax.ShapeDtypeStruct((g, 2, LANES), F32)],
        compiler_params=_cparams(1),
        name="ssm_params",
    )(lam, b_re, b_im, c_re, c_im)


def _ssm_kernel(ut_ref, tt_ref, wz_ref, wyt_ref, cs_ref, d_ref, yt_ref, z_scr, s_scr):
    batch, _, _, nc = ut_ref.shape
    ut = jnp.concatenate([ut_ref[b, 0] for b in range(batch)], axis=1)
    _to_lane_blocks(z_scr, lax.dot_general(ut, wz_ref[0], (((0,), (0,)), ((), ())), preferred_element_type=F32))
    c1 = cs_ref[0, 0:1, :]
    c2 = cs_ref[0, 1:2, :]

    def step(c, carry):
        s1, s2 = carry
        rows = pl.ds(c, batch, stride=nc)
        s_scr[rows, :] = s1
        n1 = c1 * s1 + c2 * s2 + z_scr[0, rows, :]
        n2 = c1 * s2 - c2 * s1 + z_scr[1, rows, :]
        return n1, n2

    zero = jnp.zeros((batch, LANES), F32)
    lax.fori_loop(0, nc, step, (zero, zero), unroll=8)
    y = jnp.dot(tt_ref[0], ut, preferred_element_type=F32)
    y = y + lax.dot_general(wyt_ref[0], s_scr[...].astype(BF16), (((1,), (1,)), ((), ())),
                            preferred_element_type=F32)
    y = y + d_ref[0] * ut.astype(F32)
    for b in range(batch):
        yt_ref[b, 0] = y[:, b * nc:(b + 1) * nc]


def _ssm(ut, tt, wz, wyt, cs, d_skip):
    b, g, _, nc = ut.shape
    d_col = jnp.tile(d_skip.reshape(g, 1, SSM_GROUP), (1, SSM_CHUNK, 1)).reshape(g, SSM_ROW, 1)
    blk = lambda *shape: pl.BlockSpec((1,) + shape, lambda i: (i, 0, 0))
    act = pl.BlockSpec((b, 1, SSM_ROW, nc), lambda i: (0, i, 0, 0))
    return pl.pallas_call(
        _ssm_kernel,
        grid=(g,),
        in_specs=[act, blk(SSM_ROW, SSM_ROW), blk(SSM_ROW, SSM_ROW), blk(SSM_ROW, LANES), blk(2, LANES),
                  blk(SSM_ROW, 1)],
        out_specs=act,
        out_shape=jax.ShapeDtypeStruct((b, g, SSM_ROW, nc), F32),
        scratch_shapes=[pltpu.VMEM((SSM_ROW // LANES, b * nc, LANES), F32), pltpu.VMEM((b * nc, LANES), F32)],
        compiler_params=_cparams(1),
        name="ssm",
    )(ut, tt, wz, wyt, cs, d_col)


def _post_kernel(x_ref, attn_ref, yt_ref, mod_ref, wglut_ref, bglu_ref, sn_ref, wout_ref, nf_ref, wr_ref, br_ref,
                 tri_ref, x1_ref, h2_ref, eidx_ref, wts_ref, rank_ref, cnt_ref, carry_ref, xb_scr):
    @pl.when((pl.program_id(0) == 0) & (pl.program_id(1) == 0))
    def _():
        carry_ref[...] = jnp.zeros_like(carry_ref)

    jj = pl.program_id(1)

    @pl.when(jj == 0)
    def _():
        _to_lane_blocks(xb_scr, x_ref[0])

    nc = attn_ref.shape[2]
    ts = POST_POS * nc
    d = x_ref.shape[2]
    yt = jnp.concatenate(
        [yt_ref[0, :, il * SSM_GROUP:(il + 1) * SSM_GROUP, :].reshape(D_SSM, nc) for il in range(POST_POS)], axis=1)
    g = jax.nn.gelu(yt)
    gate = jax.nn.sigmoid(jnp.dot(wglut_ref[...], g.astype(BF16), preferred_element_type=F32) + bglu_ref[...])
    ssm_t = _rms(g * gate, axis=0) * sn_ref[...]
    mixed = jnp.concatenate([attn_ref[0].reshape(ts, D_ATTN), ssm_t.T.astype(BF16)], axis=-1)
    o = jnp.dot(mixed, wout_ref[...], preferred_element_type=F32)
    x = jnp.concatenate([_pos_rows(xb_scr, POST_POS * jj + il, nc) for il in range(POST_POS)], axis=0)
    x1 = x + mod_ref[0, 2:3, :] * o
    x1_ref[0] = x1.reshape(POST_POS, nc, d)
    h2 = _rms(x1) * nf_ref[...] * (1.0 + mod_ref[0, 4:5, :]) + mod_ref[0, 3:4, :]
    h2_ref[0] = _pack_halves(h2).reshape(POST_POS, nc, HALF)

    logits = lax.dot_general(wr_ref[...], h2.astype(BF16), (((1,), (1,)), ((), ())),
                             preferred_element_type=F32) + br_ref[...]
    iota_e = lax.broadcasted_iota(I32, (N_EXPERTS, ts), 0).astype(F32)
    l = logits
    idxs, vals = [], []
    for _ in range(TOP_K):
        m = jnp.max(l, axis=0, keepdims=True)
        idx = jnp.min(jnp.where(l == m, iota_e, float(N_EXPERTS)), axis=0, keepdims=True)
        idxs.append(idx)
        vals.append(m)
        l = jnp.where(iota_e == idx, -jnp.inf, l)
    es = [jnp.exp(v - vals[0]) for v in vals]
    tot = es[0] + es[1] + es[2] + es[3]
    member = jnp.zeros((N_EXPERTS, ts), F32)
    for idx in idxs:
        member = member + (iota_e == idx).astype(F32)
    before = jnp.dot(member.astype(BF16), tri_ref[...], preferred_element_type=F32) + carry_ref[...]
    for k in range(TOP_K):
        eidx_ref[k:k + 1, :] = idxs[k].astype(I32)
        wts_ref[k:k + 1, :] = es[k] / tot
        rank_ref[k:k + 1, :] = jnp.sum(jnp.where(iota_e == idxs[k], before, 0.0), axis=0, keepdims=True).astype(I32)
    carry = carry_ref[...] + jnp.sum(member, axis=1, keepdims=True)
    carry_ref[...] = carry
    cnt_ref[...] = carry.astype(I32)


def _post(x, attn, yt, mod, w_glu, b_glu, ssm_norm, w_out, norm_ffn, w_router, b_router):
    b, s, d = x.shape
    nc = s // SSM_CHUNK
    ts = POST_POS * nc
    nt = SSM_CHUNK // POST_POS
    t = b * s
    pm = lambda bi, j: (bi, j, 0, 0)
    const = lambda bi, j: (0, 0)
    tok = lambda bi, j: (0, bi * nt + j)
    tri = (lax.broadcasted_iota(I32, (ts, ts), 0) < lax.broadcasted_iota(I32, (ts, ts), 1)).astype(BF16)
    col = lambda a: a.reshape(-1, 1)
    return pl.pallas_call(
        _post_kernel,
        grid=(b, nt),
        in_specs=[pl.BlockSpec((1, s, d), lambda bi, j: (bi, 0, 0)),
                  pl.BlockSpec((1, POST_POS, nc, D_ATTN), pm),
                  pl.BlockSpec((1, N_GROUPS, POST_POS * SSM_GROUP, nc), lambda bi, j: (bi, 0, j, 0)),
                  pl.BlockSpec((1, 6, d), lambda bi, j: (bi, 0, 0)),
                  pl.BlockSpec((D_SSM, D_SSM), const),
                  pl.BlockSpec((D_SSM, 1), const),
                  pl.BlockSpec((D_SSM, 1), const),
                  pl.BlockSpec((d, d), const),
                  pl.BlockSpec((1, d), const),
                  pl.BlockSpec((N_EXPERTS, d), const),
                  pl.BlockSpec((N_EXPERTS, 1), const),
                  pl.BlockSpec((ts, ts), const)],
        out_specs=[pl.BlockSpec((1, POST_POS, nc, d), pm),
                   pl.BlockSpec((1, POST_POS, nc, HALF), pm),
                   pl.BlockSpec((TOP_K, ts), tok),
                   pl.BlockSpec((TOP_K, ts), tok),
                   pl.BlockSpec((TOP_K, ts), tok),
                   pl.BlockSpec((N_EXPERTS, 1), const)],
        out_shape=[jax.ShapeDtypeStruct((b, SSM_CHUNK, nc, d), F32),
                   jax.ShapeDtypeStruct((b, SSM_CHUNK, nc, HALF), U32),
                   jax.ShapeDtypeStruct((TOP_K, t), I32),
                   jax.ShapeDtypeStruct((TOP_K, t), F32),
                   jax.ShapeDtypeStruct((TOP_K, t), I32),
                   jax.ShapeDtypeStruct((N_EXPERTS, 1), I32)],
        scratch_shapes=[pltpu.VMEM((N_EXPERTS, 1), F32), pltpu.VMEM((d // LANES, s, LANES), F32)],
        compiler_params=_cparams(2, ROW_VMEM_BYTES),
        name="post",
    )(x, attn, yt, mod, w_glu.T.astype(BF16), col(b_glu), col(ssm_norm), w_out.astype(BF16),
      norm_ffn.reshape(1, -1), w_router.T.astype(BF16), col(b_router), tri)


def _route_kernel(eidx_ref, rank_ref, cnt_ref, pos_ref, te_ref, nv_ref, nx_ref):
    cnt = cnt_ref[...]
    tiles = (cnt + (FFN_ROWS - 1)) // FFN_ROWS
    er = lax.broadcasted_iota(I32, (N_EXPERTS, N_EXPERTS), 0)
    ec = lax.broadcasted_iota(I32, (N_EXPERTS, N_EXPERTS), 1)
    ltri = (ec < er).astype(BF16)
    tiles_b = jnp.broadcast_to(tiles.astype(F32), (N_EXPERTS, LANES)).astype(BF16)
    start_t = jnp.dot(ltri, tiles_b, preferred_element_type=F32)[:, 0:1].astype(I32)
    end_t = start_t + tiles
    start = start_t * FFN_ROWS

    t = eidx_ref.shape[1]
    iota_e = lax.broadcasted_iota(I32, (N_EXPERTS, t), 0)
    for k in range(TOP_K):
        sel = jnp.where(iota_e == eidx_ref[k:k + 1, :], start, 0)
        pos_ref[k:k + 1, :] = jnp.sum(sel, axis=0, keepdims=True) + rank_ref[k:k + 1, :]

    nv = jnp.max(end_t, axis=0, keepdims=True)
    width = te_ref.shape[1]
    ti = jnp.minimum(lax.broadcasted_iota(I32, (N_EXPERTS, width), 1), nv - 1)
    te = jnp.minimum(jnp.sum((ti >= end_t).astype(I32), axis=0, keepdims=True), N_EXPERTS - 1)
    te_ref[...] = te
    nv_ref[...] = jnp.broadcast_to(nv, nv_ref.shape)
    ie = lax.broadcasted_iota(I32, (N_EXPERTS, width), 0)
    own_end = jnp.sum(jnp.where(ie == te, end_t, 0), axis=0, keepdims=True)
    nxt = jnp.minimum(jnp.sum((own_end >= end_t).astype(I32), axis=0, keepdims=True), N_EXPERTS - 1)
    nx_ref[...] = jnp.where(own_end < nv, nxt, -1)


def _route(eidx, rank, cnt, n_tiles):
    t = eidx.shape[1]
    width = -(-n_tiles // LANES) * LANES
    return pl.pallas_call(
        _route_kernel,
        out_shape=[jax.ShapeDtypeStruct((TOP_K, t), I32),
                   jax.ShapeDtypeStruct((1, width), I32),
                   jax.ShapeDtypeStruct((1, LANES), I32),
                   jax.ShapeDtypeStruct((1, width), I32)],
        name="route",
    )(eidx, rank, cnt)


def _dispatch_kernel(pos_ref, h_ref, xs_in_ref, xs_ref, sem):
    del xs_in_ref
    tt = h_ref.shape[0]

    def issue(t, carry):
        for k in range(TOP_K):
            pltpu.make_async_copy(h_ref.at[pl.ds(t, 1), :], xs_ref.at[pl.ds(pos_ref[k, t], 1), :],
                                  sem).start(priority=k % 2)
        return carry

    lax.fori_loop(0, tt, issue, 0, unroll=8)
    for _ in range(TOP_K):
        pltpu.make_async_copy(h_ref, xs_ref.at[pl.ds(0, tt), :], sem).wait()


def _dispatch(pos, h2p, n_rows):
    t = h2p.shape[0]
    tt = min(MOVE_ROWS, t)
    return pl.pallas_call(
        _dispatch_kernel,
        grid=(t // tt,),
        in_specs=[pl.BlockSpec((TOP_K, tt), lambda i: (0, i), memory_space=pltpu.SMEM),
                  pl.BlockSpec((tt, HALF), lambda i: (i, 0)),
                  pl.BlockSpec(memory_space=pl.ANY)],
        out_specs=pl.BlockSpec(memory_space=pl.ANY),
        out_shape=jax.ShapeDtypeStruct((n_rows, HALF), U32),
        scratch_shapes=[pltpu.SemaphoreType.DMA],
        input_output_aliases={2: 0},
        compiler_params=_cparams(1),
        name="dispatch",
    )(pos, h2p, jnp.zeros((n_rows, HALF), U32))


def _ffn_kernel(te_ref, nv_ref, nx_ref, xs_ref, wgu_hbm, bg_ref, bu_ref, wd_hbm, bd_ref, perm_ref, ys_ref,
                wgu_stage, wd_stage, wg_scr, wu_scr, wd_scr, sems):
    i = pl.program_id(0)
    valid = i < nv_ref[0]
    new_expert = (i == 0) | (te_ref[i] != te_ref[jnp.maximum(i - 1, 0)])

    def stage_copies(e):
        return (pltpu.make_async_copy(wgu_hbm.at[e], wgu_stage, sems.at[0]),
                pltpu.make_async_copy(wd_hbm.at[e], wd_stage, sems.at[1]))

    @pl.when(valid & new_expert)
    def _():
        e = te_ref[i]

        @pl.when(i == 0)
        def _():
            for cp in stage_copies(e):
                cp.start()

        for cp in stage_copies(e):
            cp.wait()
        for c in range(2 * D_FF // PERM):
            w = wgu_stage[:, c * PERM:(c + 1) * PERM].astype(BF16)
            pw = jnp.dot(w, perm_ref[...], preferred_element_type=F32).astype(BF16)
            wg_scr[:, c * (PERM // 2):(c + 1) * (PERM // 2)] = pw[:, :PERM // 2]
            wu_scr[:, c * (PERM // 2):(c + 1) * (PERM // 2)] = pw[:, PERM // 2:]
        wd_scr[...] = wd_stage[...].astype(BF16)

        @pl.when(nx_ref[i] >= 0)
        def _():
            for cp in stage_copies(nx_ref[i]):
                cp.start()

    @pl.when(valid)
    def _():
        x_hi, x_lo = _unpack_halves(xs_ref[...])
        x_hi = x_hi.astype(BF16)
        x_lo = x_lo.astype(BF16)
        gate = (jnp.dot(x_hi, wg_scr[:HALF, :], preferred_element_type=F32)
                + jnp.dot(x_lo, wg_scr[HALF:, :], preferred_element_type=F32) + bg_ref[0])
        up = (jnp.dot(x_hi, wu_scr[:HALF, :], preferred_element_type=F32)
              + jnp.dot(x_lo, wu_scr[HALF:, :], preferred_element_type=F32) + bu_ref[0])
        gate = jnp.minimum(gate, SWIGLU_LIMIT)
        up = jnp.clip(up, -SWIGLU_LIMIT, SWIGLU_LIMIT)
        act = (up + 1.0) * (gate * jax.nn.sigmoid(SWIGLU_ALPHA * gate))
        y = jnp.dot(act.astype(BF16), wd_scr[...], preferred_element_type=F32) + bd_ref[0]
        ys_ref[...] = _pack_halves(y)


def _ffn(te, nv, nx, xs, w_gate_up, bg, bu, w_down, bd, n_tiles):
    d = D_MODEL
    tile = lambda i, te, nv, nx: (jnp.minimum(i, nv[0] - 1), 0)
    wsel = lambda i, te, nv, nx: (te[i], 0, 0)
    r = lax.broadcasted_iota(I32, (PERM, PERM), 0)
    c = lax.broadcasted_iota(I32, (PERM, PERM), 1)
    perm = (r == jnp.where(c < PERM // 2, 2 * c, 2 * (c - PERM // 2) + 1)).astype(BF16)
    return pl.pallas_call(
        _ffn_kernel,
        grid_spec=pltpu.PrefetchScalarGridSpec(
            num_scalar_prefetch=3,
            grid=(n_tiles,),
            in_specs=[pl.BlockSpec((FFN_ROWS, HALF), tile),
                      pl.BlockSpec(memory_space=pl.ANY),
                      pl.BlockSpec((1, 1, D_FF), wsel),
                      pl.BlockSpec((1, 1, D_FF), wsel),
                      pl.BlockSpec(memory_space=pl.ANY),
                      pl.BlockSpec((1, 1, d), wsel),
                      pl.BlockSpec((PERM, PERM), lambda i, te, nv, nx: (0, 0))],
            out_specs=pl.BlockSpec((FFN_ROWS, HALF), tile),
            scratch_shapes=[pltpu.VMEM((d, 2 * D_FF), F32), pltpu.VMEM((D_FF, d), F32),
                            pltpu.VMEM((d, D_FF), BF16), pltpu.VMEM((d, D_FF), BF16), pltpu.VMEM((D_FF, d), BF16),
                            pltpu.SemaphoreType.DMA((2,))],
        ),
        out_shape=jax.ShapeDtypeStruct(xs.shape, U32),
        input_output_aliases={3: 0},
        compiler_params=_cparams(1, FFN_VMEM_BYTES),
        name="ffn",
    )(te, nv, nx, xs, w_gate_up, bg, bu, w_down, bd, perm)


def _combine_kernel(pos_ref, x1_ref, w_ref, mod_ref, ys_ref, o_ref, ybuf, ob_scr, sem):
    jj = pl.program_id(1)
    nc = x1_ref.shape[2]
    tt = POST_POS * nc
    d = x1_ref.shape[3]

    def issue(t, carry):
        for k in range(TOP_K):
            pltpu.make_async_copy(ys_ref.at[pl.ds(pos_ref[k, t], 1), :], ybuf.at[k, pl.ds(t, 1), :],
                                  sem).start(priority=k % 2)
        return carry

    lax.fori_loop(0, tt, issue, 0, unroll=8)
    for k in range(TOP_K):
        pltpu.make_async_copy(ys_ref.at[pl.ds(0, tt), :], ybuf.at[k], sem).wait()

    acc_hi = jnp.zeros((tt, HALF), F32)
    acc_lo = jnp.zeros((tt, HALF), F32)
    for k in range(TOP_K):
        hi, lo = _unpack_halves(ybuf[k])
        wk = w_ref[:, k:k + 1]
        acc_hi = acc_hi + wk * hi
        acc_lo = acc_lo + wk * lo
    g2 = mod_ref[0, 5:6, :]
    x1 = x1_ref[0].reshape(tt, d)
    out = jnp.concatenate([x1[:, :HALF] + g2[:, :HALF] * acc_hi, x1[:, HALF:] + g2[:, HALF:] * acc_lo], axis=-1)
    for il in range(POST_POS):
        rows = pl.ds(POST_POS * jj + il, nc, stride=SSM_CHUNK)
        for kb in range(d // LANES):
            ob_scr[kb, rows, :] = out[il * nc:(il + 1) * nc, kb * LANES:(kb + 1) * LANES]

    @pl.when(jj == pl.num_programs(1) - 1)
    def _():
        for kb in range(d // LANES):
            o_ref[0, :, kb * LANES:(kb + 1) * LANES] = ob_scr[kb]


def _combine(pos, x1, wts_t, mod, ys):
    b, _, nc, d = x1.shape
    s = SSM_CHUNK * nc
    tt = POST_POS * nc
    nt = SSM_CHUNK // POST_POS
    return pl.pallas_call(
        _combine_kernel,
        grid=(b, nt),
        in_specs=[pl.BlockSpec((TOP_K, tt), lambda bi, j: (0, bi * nt + j), memory_space=pltpu.SMEM),
                  pl.BlockSpec((1, POST_POS, nc, d), lambda bi, j: (bi, j, 0, 0)),
                  pl.BlockSpec((tt, TOP_K), lambda bi, j: (bi * nt + j, 0)),
                  pl.BlockSpec((1, 6, d), lambda bi, j: (bi, 0, 0)),
                  pl.BlockSpec(memory_space=pl.ANY)],
        out_specs=pl.BlockSpec((1, s, d), lambda bi, j: (bi, 0, 0)),
        out_shape=jax.ShapeDtypeStruct((b, s, d), F32),
        scratch_shapes=[pltpu.VMEM((TOP_K, tt, HALF), U32), pltpu.VMEM((d // LANES, s, LANES), F32),
                        pltpu.SemaphoreType.DMA],
        compiler_params=_cparams(2, ROW_VMEM_BYTES),
        name="combine",
    )(pos, x1, wts_t, mod, ys)


def kernel(x, c, w_ada, b_ada, norm_mix, w_in, b_in, q_norm, k_norm, sinks, lam_re, lam_im, log_dt, b_re, b_im,
           c_re, c_im, d_skip, w_glu, b_glu, attn_out_norm, ssm_out_norm, w_out, norm_ffn, w_router, b_router,
           w_gate_up, b_gate_up, w_down, b_down):
    b, s, d = x.shape
    t = b * s
    depth = w_ada.shape[0]
    n_tiles = (t * TOP_K) // FFN_ROWS + N_EXPERTS
    for l in range(depth):
        mod = _adaln(c, w_ada[l], b_ada[l]).reshape(b, 6, d)
        q, k, v, ut = _inproj(x, mod, norm_mix[l], w_in[l], b_in[l])
        attn = _attention(q, k, v, sinks[l], q_norm[l], k_norm[l], attn_out_norm[l])
        tt, wz, wyt, cs = _ssm_params(lam_re[l], lam_im[l], log_dt[l], b_re[l], b_im[l], c_re[l], c_im[l])
        yt = _ssm(ut, tt, wz, wyt, cs, d_skip[l])
        x1, h2p, eidx, wts, rank, cnt = _post(x, attn, yt, mod, w_glu[l], b_glu[l], ssm_out_norm[l], w_out[l],
                                              norm_ffn[l], w_router[l], b_router[l])
        pos, te, nv, nx = _route(eidx, rank, cnt, n_tiles)
        xs = _dispatch(pos, h2p.reshape(t, HALF), n_tiles * FFN_ROWS)
        wgu = w_gate_up[l]
        bgu = b_gate_up[l]
        ys = _ffn(te[0, :n_tiles], nv[0, :1], nx[0, :n_tiles], xs, wgu, bgu[:, None, 0::2], bgu[:, None, 1::2],
                  w_down[l], b_down[l][:, None, :], n_tiles)
        x = _combine(pos, x1, wts.T, mod, ys)
    return x
```

```python
import functools
import math

import jax
import jax.numpy as jnp
from jax import lax
from jax.experimental import pallas as pl
from jax.experimental.pallas import tpu as pltpu

F32 = jnp.float32
BF16 = jnp.bfloat16
U32 = jnp.uint32
I32 = jnp.int32

D_MODEL = 1024
HEAD_DIM = 64
N_HEADS = 8
N_KV_HEADS = 2
Q_PER_KV = N_HEADS // N_KV_HEADS
D_ATTN = N_HEADS * HEAD_DIM
D_KV = N_KV_HEADS * HEAD_DIM
D_QKV = D_ATTN + 2 * D_KV
WINDOW = 128
BLOCK = 128
D_SSM = D_MODEL - D_ATTN
SSM_GROUP = 16
N_GROUPS = D_SSM // SSM_GROUP
STATE = 64
N_EXPERTS = 32
TOP_K = 4
D_FF = D_MODEL
SWIGLU_LIMIT = 7.0
SWIGLU_ALPHA = 1.702
EPS = 1e-6
NEG_INF = -1e30

LANES = 128
SSM_CHUNK = 16
SSM_ROW = SSM_CHUNK * SSM_GROUP
N_POW = 2 * SSM_CHUNK
HALF = D_MODEL // 2
SLAB = HALF // LANES

POS_PER_STEP = 4
ATTN_ROWS = 256
POST_POS = 2
FFN_ROWS = 256
MOVE_ROWS = 256
PERM = 256
FFN_VMEM_BYTES = 40 * 1024 * 1024
ROW_VMEM_BYTES = 48 * 1024 * 1024

HIGHEST = lax.Precision.HIGHEST
_ARB = "arbitrary"


def _cparams(n, vmem=None):
    return pltpu.CompilerParams(dimension_semantics=(_ARB,) * n, vmem_limit_bytes=vmem)


def _rms(x, axis=-1):
    return x * lax.rsqrt(jnp.mean(x * x, axis=axis, keepdims=True) + EPS)


def _pack_halves(y):
    hi = lax.bitcast_convert_type(y[:, :HALF].astype(BF16).astype(F32), U32)
    lo = lax.bitcast_convert_type(y[:, HALF:].astype(BF16).astype(F32), U32)
    return (hi & jnp.uint32(0xFFFF0000)) | (lo >> 16)


def _unpack_halves(w):
    hi = lax.bitcast_convert_type(w & jnp.uint32(0xFFFF0000), F32)
    lo = lax.bitcast_convert_type(w << 16, F32)
    return hi, lo


def _store_slabs(ref, lead, packed):
    for sb in range(SLAB):
        ref[lead + (slice(None), sb, slice(None))] = packed[:, sb * LANES:(sb + 1) * LANES]


def _load_slabs(ref, lead=()):
    return jnp.concatenate([ref[lead + (slice(None), sb, slice(None))] for sb in range(SLAB)], axis=-1)


def _to_lane_blocks(dst, src):
    for kb in range(dst.shape[0]):
        dst[kb] = src[:, kb * LANES:(kb + 1) * LANES]


def _pos_rows(blocks, i, n_chunks):
    return jnp.concatenate([blocks[kb, pl.ds(i, n_chunks, stride=SSM_CHUNK), :] for kb in range(blocks.shape[0])],
                           axis=-1)


def _adaln_kernel(c_ref, w_ref, b_ref, o_ref):
    c = c_ref[...]
    ca = c * jax.nn.sigmoid(c)
    o_ref[...] = jnp.dot(ca, w_ref[...], preferred_element_type=F32, precision=HIGHEST) + b_ref[...]


def _adaln(c, w_ada, b_ada):
    b, d = c.shape
    n = w_ada.shape[1] // d
    return pl.pallas_call(
        _adaln_kernel,
        grid=(n,),
        in_specs=[pl.BlockSpec((b, d), lambda j: (0, 0)),
                  pl.BlockSpec((d, d), lambda j: (0, j)),
                  pl.BlockSpec((1, d), lambda j: (0, j))],
        out_specs=pl.BlockSpec((b, d), lambda j: (0, j)),
        out_shape=jax.ShapeDtypeStruct((b, n * d), F32),
        compiler_params=_cparams(1),
        name="adaln",
    )(c, w_ada, b_ada.reshape(1, -1))


def _inproj_kernel(x_ref, mod_ref, g_ref, wqkv_ref, bqkv_ref, wut_ref, but_ref, q_ref, k_ref, v_ref, ut_ref, xb_scr):
    j = pl.program_id(1)
    nc = ut_ref.shape[3]

    @pl.when(j == 0)
    def _():
        _to_lane_blocks(xb_scr, x_ref[0])

    rows = POS_PER_STEP * nc
    gain = g_ref[...]
    scale = 1.0 + mod_ref[0, 1:2, :]
    shift = mod_ref[0, 0:1, :]

    def norm_mod(x):
        return (_rms(x) * gain * scale + shift).astype(BF16)

    h = norm_mod(x_ref[0, pl.ds(pl.multiple_of(j * rows, rows), rows), :])
    proj = jnp.dot(h, wqkv_ref[...], preferred_element_type=F32) + bqkv_ref[...]
    q_ref[0] = proj[:, :D_ATTN].astype(BF16)
    k_ref[0] = proj[:, D_ATTN:D_ATTN + D_KV].astype(BF16)
    v_ref[0] = proj[:, D_ATTN + D_KV:].astype(BF16)

    hs = jnp.concatenate([norm_mod(_pos_rows(xb_scr, POS_PER_STEP * j + il, nc)) for il in range(POS_PER_STEP)],
                         axis=0)
    ut = lax.dot_general(wut_ref[...], hs, (((1,), (1,)), ((), ())), preferred_element_type=F32) + but_ref[...]
    for il in range(POS_PER_STEP):
        piece = ut[:, il * nc:(il + 1) * nc].astype(BF16)
        ut_ref[0, :, il * SSM_GROUP:(il + 1) * SSM_GROUP, :] = piece.reshape(N_GROUPS, SSM_GROUP, nc)


def _inproj(x, mod, gain, w_in, b_in):
    b, s, d = x.shape
    nc = s // SSM_CHUNK
    rows = POS_PER_STEP * nc
    row = lambda bi, j: (bi, j, 0)
    const = lambda bi, j: (0, 0)
    w_qkv = w_in[:, :D_QKV].astype(BF16)
    w_ut = w_in[:, D_QKV:].T.astype(BF16)
    return pl.pallas_call(
        _inproj_kernel,
        grid=(b, SSM_CHUNK // POS_PER_STEP),
        in_specs=[pl.BlockSpec((1, s, d), lambda bi, j: (bi, 0, 0)),
                  pl.BlockSpec((1, 6, d), lambda bi, j: (bi, 0, 0)),
                  pl.BlockSpec((1, d), const),
                  pl.BlockSpec((d, D_QKV), const),
                  pl.BlockSpec((1, D_QKV), const),
                  pl.BlockSpec((D_SSM, d), const),
                  pl.BlockSpec((D_SSM, 1), const)],
        out_specs=[pl.BlockSpec((1, rows, D_ATTN), row),
                   pl.BlockSpec((1, rows, D_KV), row),
                   pl.BlockSpec((1, rows, D_KV), row),
                   pl.BlockSpec((1, N_GROUPS, POS_PER_STEP * SSM_GROUP, nc), lambda bi, j: (bi, 0, j, 0))],
        out_shape=[jax.ShapeDtypeStruct((b, s, D_ATTN), BF16),
                   jax.ShapeDtypeStruct((b, s, D_KV), BF16),
                   jax.ShapeDtypeStruct((b, s, D_KV), BF16),
                   jax.ShapeDtypeStruct((b, N_GROUPS, SSM_ROW, nc), BF16)],
        scratch_shapes=[pltpu.VMEM((d // LANES, s, LANES), F32)],
        compiler_params=_cparams(2, ROW_VMEM_BYTES),
        name="inproj",
    )(x, mod, gain.reshape(1, d), w_qkv, b_in[:D_QKV].reshape(1, D_QKV), w_ut, b_in[D_QKV:].reshape(D_SSM, 1))


def _half_norm(x, low):
    sq = x * x
    s_lo = jnp.sum(jnp.where(low, sq, 0.0), axis=-1, keepdims=True)
    s_hi = jnp.sum(sq, axis=-1, keepdims=True) - s_lo
    inv = 1.0 / HEAD_DIM
    scale = jnp.where(low, lax.rsqrt(s_lo * inv + EPS), lax.rsqrt(s_hi * inv + EPS))
    return x * scale


def _attn_block(n, q, k_ref, v_ref, sinks_ref, qn, kn, low, valid_band, rblk):
    cur = pl.multiple_of(n * BLOCK, BLOCK)
    prev = pl.multiple_of(jnp.maximum(n - 1, 0) * BLOCK, BLOCK)
    kwin = jnp.concatenate([k_ref[0, pl.ds(prev, BLOCK), :], k_ref[0, pl.ds(cur, BLOCK), :]], axis=0).astype(F32)
    vwin = jnp.concatenate([v_ref[0, pl.ds(prev, BLOCK), :], v_ref[0, pl.ds(cur, BLOCK), :]], axis=0).astype(F32)
    kwin = _half_norm(kwin, low) * kn
    kswap = pltpu.roll(kwin, HEAD_DIM, axis=1)
    vswap = pltpu.roll(vwin, HEAD_DIM, axis=1)
    k_dup = [jnp.where(low, kwin, kswap).astype(BF16), jnp.where(low, kswap, kwin).astype(BF16)]
    v_dup = [jnp.where(low, vwin, vswap).astype(BF16), jnp.where(low, vswap, vwin).astype(BF16)]
    kj = lax.broadcasted_iota(I32, valid_band.shape, 1)
    valid = valid_band & ((kj >= BLOCK) | (n > 0))

    out_blocks = []
    for hk in range(N_KV_HEADS):
        qs = []
        for j in range(Q_PER_KV // 2):
            blk = hk * (Q_PER_KV // 2) + j
            qb = _half_norm(q[:, blk * LANES:(blk + 1) * LANES], low) * qn * (1.0 / math.sqrt(HEAD_DIM))
            qs.append(jnp.where(low, qb, 0.0))
            qs.append(jnp.where(low, 0.0, qb))
        qg = jnp.concatenate(qs, axis=0).astype(BF16)
        s = lax.dot_general(qg, k_dup[hk], (((1,), (1,)), ((), ())), preferred_element_type=F32)
        s = jnp.where(valid, s, NEG_INF)
        sink = jnp.zeros((Q_PER_KV * BLOCK, 1), F32)
        for g in range(Q_PER_KV):
            sink = jnp.where(rblk == g, sinks_ref[hk * Q_PER_KV + g], sink)
        m = jnp.maximum(jnp.max(s, axis=-1, keepdims=True), sink)
        p = jnp.exp(s - m)
        den = jnp.sum(p, axis=-1, keepdims=True) + jnp.exp(sink - m)
        o = jnp.dot(p.astype(BF16), v_dup[hk], preferred_element_type=F32) / den
        for j in range(Q_PER_KV // 2):
            ev = o[(2 * j) * BLOCK:(2 * j + 1) * BLOCK]
            od = o[(2 * j + 1) * BLOCK:(2 * j + 2) * BLOCK]
            out_blocks.append(jnp.where(low, ev, od))
    return jnp.concatenate(out_blocks, axis=-1)


def _attn_kernel(sinks_ref, q_ref, k_ref, v_ref, qn_ref, kn_ref, on_ref, o_ref, a_scr):
    step = pl.program_id(1)
    low = lax.broadcasted_iota(I32, (1, LANES), 1) < HEAD_DIM
    rows = Q_PER_KV * BLOCK
    qi = lax.broadcasted_iota(I32, (rows, 2 * BLOCK), 0) % BLOCK
    kj = lax.broadcasted_iota(I32, (rows, 2 * BLOCK), 1)
    diff = BLOCK + qi - kj
    valid_band = (diff >= 0) & (diff < WINDOW)
    rblk = lax.broadcasted_iota(I32, (rows, 1), 0) // BLOCK
    for qb in range(ATTN_ROWS // BLOCK):
        q = q_ref[0, qb * BLOCK:(qb + 1) * BLOCK, :].astype(F32)
        attn = _attn_block(step * (ATTN_ROWS // BLOCK) + qb, q, k_ref, v_ref, sinks_ref, qn_ref[...], kn_ref[...],
                           low, valid_band, rblk)
        attn = _rms(attn) * on_ref[...]
        for kb in range(D_ATTN // LANES):
            a_scr[kb, qb * BLOCK:(qb + 1) * BLOCK, :] = attn[:, kb * LANES:(kb + 1) * LANES]
    for i in range(SSM_CHUNK):
        o_ref[0, i] = _pos_rows(a_scr, i, ATTN_ROWS // SSM_CHUNK).astype(BF16)


def _attention(q, k, v, sinks, q_norm, k_norm, out_norm):
    b, s, _ = q.shape
    tile2 = lambda g: jnp.tile(g.reshape(1, HEAD_DIM), (1, 2))
    cps = ATTN_ROWS // SSM_CHUNK
    return pl.pallas_call(
        _attn_kernel,
        grid=(b, s // ATTN_ROWS),
        in_specs=[pl.BlockSpec(memory_space=pltpu.SMEM),
                  pl.BlockSpec((1, ATTN_ROWS, D_ATTN), lambda bi, n: (bi, n, 0)),
                  pl.BlockSpec((1, s, D_KV), lambda bi, n: (bi, 0, 0)),
                  pl.BlockSpec((1, s, D_KV), lambda bi, n: (bi, 0, 0)),
                  pl.BlockSpec((1, LANES), lambda bi, n: (0, 0)),
                  pl.BlockSpec((1, LANES), lambda bi, n: (0, 0)),
                  pl.BlockSpec((1, D_ATTN), lambda bi, n: (0, 0))],
        out_specs=pl.BlockSpec((1, SSM_CHUNK, cps, D_ATTN), lambda bi, n: (bi, 0, n, 0)),
        out_shape=jax.ShapeDtypeStruct((b, SSM_CHUNK, s // SSM_CHUNK, D_ATTN), BF16),
        scratch_shapes=[pltpu.VMEM((D_ATTN // LANES, ATTN_ROWS, LANES), F32)],
        compiler_params=_cparams(2),
        name="attention",
    )(sinks, q, k, v, tile2(q_norm), tile2(k_norm), out_norm.reshape(1, D_ATTN))


def _cmul(ar, ai, br, bi):
    return ar * br - ai * bi, ar * bi + ai * br


def _ssm_param_kernel(lam_ref, bre_ref, bim_ref, cre_ref, cim_ref, tt_ref, wz_ref, wyt_ref, cs_ref):
    f32dot = functools.partial(jnp.dot, preferred_element_type=F32, precision=HIGHEST)
    lr = lam_ref[0, 0:1, :]
    li = lam_ref[0, 1:2, :]
    dt = jnp.exp(lam_ref[0, 2:3, :])
    rho = lr * dt
    th = li * dt
    imag_lane = lax.broadcasted_iota(I32, (1, LANES), 1) >= STATE

    kk = (lax.broadcasted_iota(I32, (N_POW, 1), 0) - (SSM_CHUNK - 1)).astype(F32)
    mag = jnp.exp(rho * kk)
    pw_r = mag * jnp.cos(th * kk)
    pw_i = mag * jnp.sin(th * kk)
    lb_r = pw_r[SSM_CHUNK:SSM_CHUNK + 1]
    lb_i = pw_i[SSM_CHUNK:SSM_CHUNK + 1]
    den = lr * lr + li * li
    coef_r = ((lb_r - 1.0) * lr + lb_i * li) / den
    coef_i = (lb_i * lr - (lb_r - 1.0) * li) / den

    rows = lax.broadcasted_iota(I32, (SSM_ROW, 1), 0)
    pos = rows // SSM_GROUP
    chan = rows % SSM_GROUP
    ch_sel = (chan == lax.broadcasted_iota(I32, (1, SSM_GROUP), 1)).astype(F32)
    lane_fold = (lax.broadcasted_iota(I32, (STATE, LANES), 1) % STATE
                 == lax.broadcasted_iota(I32, (STATE, LANES), 0)).astype(F32)
    tab = lax.broadcasted_iota(I32, (1, N_POW), 1)

    def power_rows(k_of_row):
        sel = (tab == k_of_row + (SSM_CHUNK - 1)).astype(F32)
        return f32dot(sel, pw_r), f32dot(sel, pw_i)

    def b_rows(b_ref):
        b2 = jnp.concatenate([b_ref[0], b_ref[0]], axis=0)
        return lax.dot_general(ch_sel, b2, (((1,), (1,)), ((), ())), preferred_element_type=F32, precision=HIGHEST)

    def c_rows(c_ref):
        return f32dot(ch_sel, f32dot(c_ref[0], lane_fold))

    bbar_r, bbar_i = _cmul(coef_r, coef_i, b_rows(bre_ref), b_rows(bim_ref))
    c_r = c_rows(cre_ref)
    c_i = c_rows(cim_ref)

    a_r, a_i = _cmul(bbar_r, bbar_i, *power_rows(-pos))
    a2c = jnp.where(imag_lane, -a_i, a_r)
    m_r, m_i = _cmul(c_r, c_i, *power_rows(pos))
    bmc = jnp.where(imag_lane, m_i, m_r)
    tt = f32dot(bmc, a2c.T)
    causal = pos >= lax.broadcasted_iota(I32, (1, SSM_ROW), 1) // SSM_GROUP
    tt_ref[0] = jnp.where(causal, tt, 0.0).astype(BF16)

    w_r, w_i = _cmul(bbar_r, bbar_i, *power_rows((SSM_CHUNK - 1) - pos))
    wz_ref[0, :, :LANES] = jnp.where(imag_lane, w_i, w_r).astype(BF16)
    wz_ref[0, :, LANES:] = jnp.where(imag_lane, w_r, w_i).astype(BF16)

    y_r, y_i = _cmul(c_r, c_i, *power_rows(pos + 1))
    wyt_ref[0] = jnp.where(imag_lane, -y_i, y_r).astype(BF16)

    cs_ref[0, 0:1, :] = pw_r[N_POW - 1:N_POW]
    cs_ref[0, 1:2, :] = jnp.where(imag_lane, pw_i[N_POW - 1:N_POW], -pw_i[N_POW - 1:N_POW])


def _ssm_params(lam_re, lam_im, log_dt, b_re, b_im, c_re, c_im):
    g = lam_re.shape[0]
    lam = jnp.stack([lam_re, lam_im, jnp.broadcast_to(log_dt[:, None], (g, STATE))], axis=1)
    lam = jnp.concatenate([lam, lam], axis=2)
    blk = lambda *shape: pl.BlockSpec((1,) + shape, lambda i: (i, 0, 0))
    return pl.pallas_call(
        _ssm_param_kernel,
        grid=(g,),
        in_specs=[blk(3, LANES), blk(STATE, SSM_GROUP), blk(STATE, SSM_GROUP), blk(SSM_GROUP, STATE),
                  blk(SSM_GROUP, STATE)],
        out_specs=[blk(SSM_ROW, SSM_ROW), blk(SSM_ROW, SSM_ROW), blk(SSM_ROW, LANES), blk(2, LANES)],
        out_shape=[jax.ShapeDtypeStruct((g, SSM_ROW, SSM_ROW), BF16),
                   jax.ShapeDtypeStruct((g, SSM_ROW, SSM_ROW), BF16),
                   jax.ShapeDtypeStruct((g, SSM_ROW, LANES), BF16),
                   jax.ShapeDtypeStruct((g, 2, LANES), F32)],
        compiler_params=_cparams(1),
        name="ssm_params",
    )(lam, b_re, b_im, c_re, c_im)


def _ssm_kernel(ut_ref, tt_ref, wz_ref, wyt_ref, cs_ref, d_ref, yt_ref, z_scr, s_scr):
    batch, _, _, nc = ut_ref.shape
    ut = jnp.concatenate([ut_ref[b, 0] for b in range(batch)], axis=1)
    _to_lane_blocks(z_scr, lax.dot_general(ut, wz_ref[0], (((0,), (0,)), ((), ())), preferred_element_type=F32))
    c1 = cs_ref[0, 0:1, :]
    c2 = cs_ref[0, 1:2, :]

    def step(c, carry):
        s1, s2 = carry
        rows = pl.ds(c, batch, stride=nc)
        s_scr[rows, :] = s1
        n1 = c1 * s1 + c2 * s2 + z_scr[0, rows, :]
        n2 = c1 * s2 - c2 * s1 + z_scr[1, rows, :]
        return n1, n2

    zero = jnp.zeros((batch, LANES), F32)
    lax.fori_loop(0, nc, step, (zero, zero), unroll=8)
    y = jnp.dot(tt_ref[0], ut, preferred_element_type=F32)
    y = y + lax.dot_general(wyt_ref[0], s_scr[...].astype(BF16), (((1,), (1,)), ((), ())),
                            preferred_element_type=F32)
    y = y + d_ref[0] * ut.astype(F32)
    for b in range(batch):
        yt_ref[b, 0] = y[:, b * nc:(b + 1) * nc]


def _ssm(ut, tt, wz, wyt, cs, d_skip):
    b, g, _, nc = ut.shape
    d_col = jnp.tile(d_skip.reshape(g, 1, SSM_GROUP), (1, SSM_CHUNK, 1)).reshape(g, SSM_ROW, 1)
    blk = lambda *shape: pl.BlockSpec((1,) + shape, lambda i: (i, 0, 0))
    act = pl.BlockSpec((b, 1, SSM_ROW, nc), lambda i: (0, i, 0, 0))
    return pl.pallas_call(
        _ssm_kernel,
        grid=(g,),
        in_specs=[act, blk(SSM_ROW, SSM_ROW), blk(SSM_ROW, SSM_ROW), blk(SSM_ROW, LANES), blk(2, LANES),
                  blk(SSM_ROW, 1)],
        out_specs=act,
        out_shape=jax.ShapeDtypeStruct((b, g, SSM_ROW, nc), F32),
        scratch_shapes=[pltpu.VMEM((SSM_ROW // LANES, b * nc, LANES), F32), pltpu.VMEM((b * nc, LANES), F32)],
        compiler_params=_cparams(1),
        name="ssm",
    )(ut, tt, wz, wyt, cs, d_col)


def _post_kernel(x_ref, attn_ref, yt_ref, mod_ref, wglut_ref, bglu_ref, sn_ref, wout_ref, nf_ref, wr_ref, br_ref,
                 tri_ref, x1_ref, h2_ref, eidx_ref, wts_ref, rank_ref, cnt_ref, carry_ref, xb_scr):
    @pl.when((pl.program_id(0) == 0) & (pl.program_id(1) == 0))
    def _():
        carry_ref[...] = jnp.zeros_like(carry_ref)

    jj = pl.program_id(1)

    @pl.when(jj == 0)
    def _():
        _to_lane_blocks(xb_scr, x_ref[0])

    nc = attn_ref.shape[2]
    ts = POST_POS * nc
    d = x_ref.shape[2]
    yt = jnp.concatenate(
        [yt_ref[0, :, il * SSM_GROUP:(il + 1) * SSM_GROUP, :].reshape(D_SSM, nc) for il in range(POST_POS)], axis=1)
    g = jax.nn.gelu(yt)
    gate = jax.nn.sigmoid(jnp.dot(wglut_ref[...], g.astype(BF16), preferred_element_type=F32) + bglu_ref[...])
    ssm_t = _rms(g * gate, axis=0) * sn_ref[...]
    mixed = jnp.concatenate([attn_ref[0].reshape(ts, D_ATTN), ssm_t.T.astype(BF16)], axis=-1)
    o = jnp.dot(mixed, wout_ref[...], preferred_element_type=F32)
    x = jnp.concatenate([_pos_rows(xb_scr, POST_POS * jj + il, nc) for il in range(POST_POS)], axis=0)
    x1 = x + mod_ref[0, 2:3, :] * o
    x1_ref[0] = x1.reshape(POST_POS, nc, d)
    h2 = _rms(x1) * nf_ref[...] * (1.0 + mod_ref[0, 4:5, :]) + mod_ref[0, 3:4, :]
    packed = _pack_halves(h2)
    for il in range(POST_POS):
        _store_slabs(h2_ref, (0, il), packed[il * nc:(il + 1) * nc])

    logits = lax.dot_general(wr_ref[...], h2.astype(BF16), (((1,), (1,)), ((), ())),
                             preferred_element_type=F32) + br_ref[...]
    iota_e = lax.broadcasted_iota(I32, (N_EXPERTS, ts), 0).astype(F32)
    l = logits
    idxs, vals = [], []
    for _ in range(TOP_K):
        m = jnp.max(l, axis=0, keepdims=True)
        idx = jnp.min(jnp.where(l == m, iota_e, float(N_EXPERTS)), axis=0, keepdims=True)
        idxs.append(idx)
        vals.append(m)
        l = jnp.where(iota_e == idx, -jnp.inf, l)
    es = [jnp.exp(v - vals[0]) for v in vals]
    tot = es[0] + es[1] + es[2] + es[3]
    member = jnp.zeros((N_EXPERTS, ts), F32)
    for idx in idxs:
        member = member + (iota_e == idx).astype(F32)
    before = jnp.dot(member.astype(BF16), tri_ref[...], preferred_element_type=F32) + carry_ref[...]
    for k in range(TOP_K):
        eidx_ref[k:k + 1, :] = idxs[k].astype(I32)
        wts_ref[k:k + 1, :] = es[k] / tot
        rank_ref[k:k + 1, :] = jnp.sum(jnp.where(iota_e == idxs[k], before, 0.0), axis=0, keepdims=True).astype(I32)
    carry = carry_ref[...] + jnp.sum(member, axis=1, keepdims=True)
    carry_ref[...] = carry
    cnt_ref[...] = carry.astype(I32)


def _post(x, attn, yt, mod, w_glu, b_glu, ssm_norm, w_out, norm_ffn, w_router, b_router):
    b, s, d = x.shape
    nc = s // SSM_CHUNK
    ts = POST_POS * nc
    nt = SSM_CHUNK // POST_POS
    t = b * s
    pm = lambda bi, j: (bi, j, 0, 0)
    const = lambda bi, j: (0, 0)
    tok = lambda bi, j: (0, bi * nt + j)
    tri = (lax.broadcasted_iota(I32, (ts, ts), 0) < lax.broadcasted_iota(I32, (ts, ts), 1)).astype(BF16)
    col = lambda a: a.reshape(-1, 1)
    return pl.pallas_call(
        _post_kernel,
        grid=(b, nt),
        in_specs=[pl.BlockSpec((1, s, d), lambda bi, j: (bi, 0, 0)),
                  pl.BlockSpec((1, POST_POS, nc, D_ATTN), pm),
                  pl.BlockSpec((1, N_GROUPS, POST_POS * SSM_GROUP, nc), lambda bi, j: (bi, 0, j, 0)),
                  pl.BlockSpec((1, 6, d), lambda bi, j: (bi, 0, 0)),
                  pl.BlockSpec((D_SSM, D_SSM), const),
                  pl.BlockSpec((D_SSM, 1), const),
                  pl.BlockSpec((D_SSM, 1), const),
                  pl.BlockSpec((d, d), const),
                  pl.BlockSpec((1, d), const),
                  pl.BlockSpec((N_EXPERTS, d), const),
                  pl.BlockSpec((N_EXPERTS, 1), const),
                  pl.BlockSpec((ts, ts), const)],
        out_specs=[pl.BlockSpec((1, POST_POS, nc, d), pm),
                   pl.BlockSpec((1, POST_POS, nc, SLAB, LANES), lambda bi, j: (bi, j, 0, 0, 0)),
                   pl.BlockSpec((TOP_K, ts), tok),
                   pl.BlockSpec((TOP_K, ts), tok),
                   pl.BlockSpec((TOP_K, ts), tok),
                   pl.BlockSpec((N_EXPERTS, 1), const)],
        out_shape=[jax.ShapeDtypeStruct((b, SSM_CHUNK, nc, d), F32),
                   jax.ShapeDtypeStruct((b, SSM_CHUNK, nc, SLAB, LANES), U32),
                   jax.ShapeDtypeStruct((TOP_K, t), I32),
                   jax.ShapeDtypeStruct((TOP_K, t), F32),
                   jax.ShapeDtypeStruct((TOP_K, t), I32),
                   jax.ShapeDtypeStruct((N_EXPERTS, 1), I32)],
        scratch_shapes=[pltpu.VMEM((N_EXPERTS, 1), F32), pltpu.VMEM((d // LANES, s, LANES), F32)],
        compiler_params=_cparams(2, ROW_VMEM_BYTES),
        name="post",
    )(x, attn, yt, mod, w_glu.T.astype(BF16), col(b_glu), col(ssm_norm), w_out.astype(BF16),
      norm_ffn.reshape(1, -1), w_router.T.astype(BF16), col(b_router), tri)


def _route_kernel(eidx_ref, rank_ref, cnt_ref, pos_ref, te_ref, nv_ref, nx_ref):
    cnt = cnt_ref[...]
    tiles = (cnt + (FFN_ROWS - 1)) // FFN_ROWS
    er = lax.broadcasted_iota(I32, (N_EXPERTS, N_EXPERTS), 0)
    ec = lax.broadcasted_iota(I32, (N_EXPERTS, N_EXPERTS), 1)
    ltri = (ec < er).astype(BF16)
    tiles_b = jnp.broadcast_to(tiles.astype(F32), (N_EXPERTS, LANES)).astype(BF16)
    start_t = jnp.dot(ltri, tiles_b, preferred_element_type=F32)[:, 0:1].astype(I32)
    end_t = start_t + tiles
    start = start_t * FFN_ROWS

    t = eidx_ref.shape[1]
    iota_e = lax.broadcasted_iota(I32, (N_EXPERTS, t), 0)
    for k in range(TOP_K):
        sel = jnp.where(iota_e == eidx_ref[k:k + 1, :], start, 0)
        pos_ref[k:k + 1, :] = jnp.sum(sel, axis=0, keepdims=True) + rank_ref[k:k + 1, :]

    nv = jnp.max(end_t, axis=0, keepdims=True)
    width = te_ref.shape[1]
    ti = jnp.minimum(lax.broadcasted_iota(I32, (N_EXPERTS, width), 1), nv - 1)
    te = jnp.minimum(jnp.sum((ti >= end_t).astype(I32), axis=0, keepdims=True), N_EXPERTS - 1)
    te_ref[...] = te
    nv_ref[...] = jnp.broadcast_to(nv, nv_ref.shape)
    ie = lax.broadcasted_iota(I32, (N_EXPERTS, width), 0)
    own_end = jnp.sum(jnp.where(ie == te, end_t, 0), axis=0, keepdims=True)
    nxt = jnp.minimum(jnp.sum((own_end >= end_t).astype(I32), axis=0, keepdims=True), N_EXPERTS - 1)
    nx_ref[...] = jnp.where(own_end < nv, nxt, -1)


def _route(eidx, rank, cnt, n_tiles):
    t = eidx.shape[1]
    width = -(-n_tiles // LANES) * LANES
    return pl.pallas_call(
        _route_kernel,
        out_shape=[jax.ShapeDtypeStruct((TOP_K, t), I32),
                   jax.ShapeDtypeStruct((1, width), I32),
                   jax.ShapeDtypeStruct((1, LANES), I32),
                   jax.ShapeDtypeStruct((1, width), I32)],
        name="route",
    )(eidx, rank, cnt)


def _dispatch_kernel(pos_ref, h_ref, xs_in_ref, xs_ref, sem):
    del xs_in_ref
    tt = h_ref.shape[0]

    def issue(t, carry):
        for k in range(TOP_K):
            pltpu.make_async_copy(h_ref.at[t], xs_ref.at[pos_ref[TOP_K * t + k]], sem).start(priority=k % 2)
        return carry

    lax.fori_loop(0, tt, issue, 0, unroll=8)
    for _ in range(TOP_K):
        pltpu.make_async_copy(h_ref, xs_ref.at[pl.ds(0, tt)], sem).wait()


def _dispatch(pos_flat, h2p, n_rows):
    t = h2p.shape[0]
    tt = min(MOVE_ROWS, t)
    return pl.pallas_call(
        _dispatch_kernel,
        grid=(t // tt,),
        in_specs=[pl.BlockSpec((TOP_K * tt,), lambda i: (i,), memory_space=pltpu.SMEM),
                  pl.BlockSpec((tt, SLAB, LANES), lambda i: (i, 0, 0)),
                  pl.BlockSpec(memory_space=pl.ANY)],
        out_specs=pl.BlockSpec(memory_space=pl.ANY),
        out_shape=jax.ShapeDtypeStruct((n_rows, SLAB, LANES), U32),
        scratch_shapes=[pltpu.SemaphoreType.DMA],
        input_output_aliases={2: 0},
        compiler_params=_cparams(1),
        name="dispatch",
    )(pos_flat, h2p, jnp.zeros((n_rows, SLAB, LANES), U32))


def _ffn_kernel(te_ref, nv_ref, nx_ref, xs_ref, wgu_hbm, bg_ref, bu_ref, wd_hbm, bd_ref, perm_ref, ys_ref,
                wgu_stage, wd_stage, wg_scr, wu_scr, wd_scr, sems):
    i = pl.program_id(0)
    valid = i < nv_ref[0]
    new_expert = (i == 0) | (te_ref[i] != te_ref[jnp.maximum(i - 1, 0)])

    def stage_copies(e):
        return (pltpu.make_async_copy(wgu_hbm.at[e], wgu_stage, sems.at[0]),
                pltpu.make_async_copy(wd_hbm.at[e], wd_stage, sems.at[1]))

    @pl.when(valid & new_expert)
    def _():
        e = te_ref[i]

        @pl.when(i == 0)
        def _():
            for cp in stage_copies(e):
                cp.start()

        for cp in stage_copies(e):
            cp.wait()
        for c in range(2 * D_FF // PERM):
            w = wgu_stage[:, c * PERM:(c + 1) * PERM].astype(BF16)
            pw = jnp.dot(w, perm_ref[...], preferred_element_type=F32).astype(BF16)
            wg_scr[:, c * (PERM // 2):(c + 1) * (PERM // 2)] = pw[:, :PERM // 2]
            wu_scr[:, c * (PERM // 2):(c + 1) * (PERM // 2)] = pw[:, PERM // 2:]
        wd_scr[...] = wd_stage[...].astype(BF16)

        @pl.when(nx_ref[i] >= 0)
        def _():
            for cp in stage_copies(nx_ref[i]):
                cp.start()

    @pl.when(valid)
    def _():
        x_hi, x_lo = _unpack_halves(_load_slabs(xs_ref))
        x_hi = x_hi.astype(BF16)
        x_lo = x_lo.astype(BF16)
        gate = (jnp.dot(x_hi, wg_scr[:HALF, :], preferred_element_type=F32)
                + jnp.dot(x_lo, wg_scr[HALF:, :], preferred_element_type=F32) + bg_ref[0])
        up = (jnp.dot(x_hi, wu_scr[:HALF, :], preferred_element_type=F32)
              + jnp.dot(x_lo, wu_scr[HALF:, :], preferred_element_type=F32) + bu_ref[0])
        gate = jnp.minimum(gate, SWIGLU_LIMIT)
        up = jnp.clip(up, -SWIGLU_LIMIT, SWIGLU_LIMIT)
        act = (up + 1.0) * (gate * jax.nn.sigmoid(SWIGLU_ALPHA * gate))
        y = jnp.dot(act.astype(BF16), wd_scr[...], preferred_element_type=F32) + bd_ref[0]
        _store_slabs(ys_ref, (), _pack_halves(y))


def _ffn(te, nv, nx, xs, w_gate_up, bg, bu, w_down, bd, n_tiles):
    d = D_MODEL
    tile = lambda i, te, nv, nx: (jnp.minimum(i, nv[0] - 1), 0, 0)
    wsel = lambda i, te, nv, nx: (te[i], 0, 0)
    r = lax.broadcasted_iota(I32, (PERM, PERM), 0)
    c = lax.broadcasted_iota(I32, (PERM, PERM), 1)
    perm = (r == jnp.where(c < PERM // 2, 2 * c, 2 * (c - PERM // 2) + 1)).astype(BF16)
    return pl.pallas_call(
        _ffn_kernel,
        grid_spec=pltpu.PrefetchScalarGridSpec(
            num_scalar_prefetch=3,
            grid=(n_tiles,),
            in_specs=[pl.BlockSpec((FFN_ROWS, SLAB, LANES), tile),
                      pl.BlockSpec(memory_space=pl.ANY),
                      pl.BlockSpec((1, 1, D_FF), wsel),
                      pl.BlockSpec((1, 1, D_FF), wsel),
                      pl.BlockSpec(memory_space=pl.ANY),
                      pl.BlockSpec((1, 1, d), wsel),
                      pl.BlockSpec((PERM, PERM), lambda i, te, nv, nx: (0, 0))],
            out_specs=pl.BlockSpec((FFN_ROWS, SLAB, LANES), tile),
            scratch_shapes=[pltpu.VMEM((d, 2 * D_FF), F32), pltpu.VMEM((D_FF, d), F32),
                            pltpu.VMEM((d, D_FF), BF16), pltpu.VMEM((d, D_FF), BF16), pltpu.VMEM((D_FF, d), BF16),
                            pltpu.SemaphoreType.DMA((2,))],
        ),
        out_shape=jax.ShapeDtypeStruct(xs.shape, U32),
        input_output_aliases={3: 0},
        compiler_params=_cparams(1, FFN_VMEM_BYTES),
        name="ffn",
    )(te, nv, nx, xs, w_gate_up, bg, bu, w_down, bd, perm)


def _combine_kernel(pos_ref, x1_ref, w_ref, mod_ref, ys_ref, o_ref, ybuf, ob_scr, sem):
    jj = pl.program_id(1)
    nc = x1_ref.shape[2]
    tt = POST_POS * nc
    d = x1_ref.shape[3]

    def issue(t, carry):
        for k in range(TOP_K):
            pltpu.make_async_copy(ys_ref.at[pos_ref[TOP_K * t + k]], ybuf.at[k, t], sem).start(priority=k % 2)
        return carry

    lax.fori_loop(0, tt, issue, 0, unroll=8)
    for k in range(TOP_K):
        pltpu.make_async_copy(ys_ref.at[pl.ds(0, tt)], ybuf.at[k], sem).wait()

    acc_hi = jnp.zeros((tt, HALF), F32)
    acc_lo = jnp.zeros((tt, HALF), F32)
    for k in range(TOP_K):
        hi, lo = _unpack_halves(_load_slabs(ybuf, (k,)))
        wk = w_ref[:, k:k + 1]
        acc_hi = acc_hi + wk * hi
        acc_lo = acc_lo + wk * lo
    g2 = mod_ref[0, 5:6, :]
    x1 = x1_ref[0].reshape(tt, d)
    out = jnp.concatenate([x1[:, :HALF] + g2[:, :HALF] * acc_hi, x1[:, HALF:] + g2[:, HALF:] * acc_lo], axis=-1)
    for il in range(POST_POS):
        rows = pl.ds(POST_POS * jj + il, nc, stride=SSM_CHUNK)
        for kb in range(d // LANES):
            ob_scr[kb, rows, :] = out[il * nc:(il + 1) * nc, kb * LANES:(kb + 1) * LANES]

    @pl.when(jj == pl.num_programs(1) - 1)
    def _():
        for kb in range(d // LANES):
            o_ref[0, :, kb * LANES:(kb + 1) * LANES] = ob_scr[kb]


def _combine(pos, x1, wts_t, mod, ys):
    b, _, nc, d = x1.shape
    s = SSM_CHUNK * nc
    tt = POST_POS * nc
    nt = SSM_CHUNK // POST_POS
    return pl.pallas_call(
        _combine_kernel,
        grid=(b, nt),
        in_specs=[pl.BlockSpec((TOP_K * tt,), lambda bi, j: (bi * nt + j,), memory_space=pltpu.SMEM),
                  pl.BlockSpec((1, POST_POS, nc, d), lambda bi, j: (bi, j, 0, 0)),
                  pl.BlockSpec((tt, TOP_K), lambda bi, j: (bi * nt + j, 0)),
                  pl.BlockSpec((1, 6, d), lambda bi, j: (bi, 0, 0)),
                  pl.BlockSpec(memory_space=pl.ANY)],
        out_specs=pl.BlockSpec((1, s, d), lambda bi, j: (bi, 0, 0)),
        out_shape=jax.ShapeDtypeStruct((b, s, d), F32),
        scratch_shapes=[pltpu.VMEM((TOP_K, tt, SLAB, LANES), U32), pltpu.VMEM((d // LANES, s, LANES), F32),
                        pltpu.SemaphoreType.DMA],
        compiler_params=_cparams(2, ROW_VMEM_BYTES),
        name="combine",
    )(pos, x1, wts_t, mod, ys)


def kernel(x, c, w_ada, b_ada, norm_mix, w_in, b_in, q_norm, k_norm, sinks, lam_re, lam_im, log_dt, b_re, b_im,
           c_re, c_im, d_skip, w_glu, b_glu, attn_out_norm, ssm_out_norm, w_out, norm_ffn, w_router, b_router,
           w_gate_up, b_gate_up, w_down, b_down):
    b, s, d = x.shape
    t = b * s
    depth = w_ada.shape[0]
    n_tiles = (t * TOP_K) // FFN_ROWS + N_EXPERTS
    for l in range(depth):
        mod = _adaln(c, w_ada[l], b_ada[l]).reshape(b, 6, d)
        q, k, v, ut = _inproj(x, mod, norm_mix[l], w_in[l], b_in[l])
        attn = _attention(q, k, v, sinks[l], q_norm[l], k_norm[l], attn_out_norm[l])
        tt, wz, wyt, cs = _ssm_params(lam_re[l], lam_im[l], log_dt[l], b_re[l], b_im[l], c_re[l], c_im[l])
        yt = _ssm(ut, tt, wz, wyt, cs, d_skip[l])
        x1, h2p, eidx, wts, rank, cnt = _post(x, attn, yt, mod, w_glu[l], b_glu[l], ssm_out_norm[l], w_out[l],
                                              norm_ffn[l], w_router[l], b_router[l])
        pos, te, nv, nx = _route(eidx, rank, cnt, n_tiles)
        pos_flat = pos.T.reshape(-1)
        xs = _dispatch(pos_flat, h2p.reshape(t, SLAB, LANES), n_tiles * FFN_ROWS)
        wgu = w_gate_up[l]
        bgu = b_gate_up[l]
        ys = _ffn(te[0, :n_tiles], nv[0, :1], nx[0, :n_tiles], xs, wgu, bgu[:, None, 0::2], bgu[:, None, 1::2],
                  w_down[l], b_down[l][:, None, :], n_tiles)
        x = _combine(pos_flat, x1, wts.T, mod, ys)
    return x
```

```python
import functools
import math

import jax
import jax.numpy as jnp
from jax import lax
from jax.experimental import pallas as pl
from jax.experimental.pallas import tpu as pltpu

F32 = jnp.float32
BF16 = jnp.bfloat16
U32 = jnp.uint32
I32 = jnp.int32

D_MODEL = 1024
HEAD_DIM = 64
N_HEADS = 8
N_KV_HEADS = 2
Q_PER_KV = N_HEADS // N_KV_HEADS
D_ATTN = N_HEADS * HEAD_DIM
D_KV = N_KV_HEADS * HEAD_DIM
D_QKV = D_ATTN + 2 * D_KV
WINDOW = 128
BLOCK = 128
D_SSM = D_MODEL - D_ATTN
SSM_GROUP = 16
N_GROUPS = D_SSM // SSM_GROUP
STATE = 64
N_EXPERTS = 32
TOP_K = 4
D_FF = D_MODEL
SWIGLU_LIMIT = 7.0
SWIGLU_ALPHA = 1.702
EPS = 1e-6
NEG_INF = -1e30

LANES = 128
SSM_CHUNK = 16
SSM_ROW = SSM_CHUNK * SSM_GROUP
N_POW = 2 * SSM_CHUNK
HALF = D_MODEL // 2
SLAB = HALF // LANES

POS_PER_STEP = 4
ATTN_ROWS = 512
POST_POS = 2
FFN_ROWS = 256
MOVE_ROWS = 256
PERM = 256
FFN_VMEM_BYTES = 40 * 1024 * 1024
ROW_VMEM_BYTES = 48 * 1024 * 1024

HIGHEST = lax.Precision.HIGHEST
_ARB = "arbitrary"


def _cparams(n, vmem=None):
    return pltpu.CompilerParams(dimension_semantics=(_ARB,) * n, vmem_limit_bytes=vmem)


def _rms(x, axis=-1):
    return x * lax.rsqrt(jnp.mean(x * x, axis=axis, keepdims=True) + EPS)


def _pack_halves(y):
    hi = lax.bitcast_convert_type(y[:, :HALF].astype(BF16).astype(F32), U32)
    lo = lax.bitcast_convert_type(y[:, HALF:].astype(BF16).astype(F32), U32)
    return (hi & jnp.uint32(0xFFFF0000)) | (lo >> 16)


def _unpack_halves(w):
    hi = lax.bitcast_convert_type(w & jnp.uint32(0xFFFF0000), F32)
    lo = lax.bitcast_convert_type(w << 16, F32)
    return hi, lo


def _store_slabs(ref, lead, packed):
    for sb in range(SLAB):
        ref[lead + (slice(None), sb, slice(None))] = packed[:, sb * LANES:(sb + 1) * LANES]


def _load_slabs(ref, lead=()):
    return jnp.concatenate([ref[lead + (slice(None), sb, slice(None))] for sb in range(SLAB)], axis=-1)


def _to_lane_blocks(dst, src):
    for kb in range(dst.shape[0]):
        dst[kb] = src[:, kb * LANES:(kb + 1) * LANES]


def _pos_rows(blocks, i, n_chunks):
    return jnp.concatenate([blocks[kb, pl.ds(i, n_chunks, stride=SSM_CHUNK), :] for kb in range(blocks.shape[0])],
                           axis=-1)


def _adaln_kernel(c_ref, w_ref, b_ref, o_ref):
    c = c_ref[...]
    ca = c * jax.nn.sigmoid(c)
    o_ref[...] = jnp.dot(ca, w_ref[...], preferred_element_type=F32, precision=HIGHEST) + b_ref[...]


def _adaln(c, w_ada, b_ada):
    b, d = c.shape
    n = w_ada.shape[1] // d
    return pl.pallas_call(
        _adaln_kernel,
        grid=(n,),
        in_specs=[pl.BlockSpec((b, d), lambda j: (0, 0)),
                  pl.BlockSpec((d, d), lambda j: (0, j)),
                  pl.BlockSpec((1, d), lambda j: (0, j))],
        out_specs=pl.BlockSpec((b, d), lambda j: (0, j)),
        out_shape=jax.ShapeDtypeStruct((b, n * d), F32),
        compiler_params=_cparams(1),
        name="adaln",
    )(c, w_ada, b_ada.reshape(1, -1))


def _inproj_kernel(x_ref, mod_ref, g_ref, wqkv_ref, bqkv_ref, wut_ref, but_ref, q_ref, k_ref, v_ref, ut_ref, xb_scr):
    j = pl.program_id(1)
    nc = ut_ref.shape[3]

    @pl.when(j == 0)
    def _():
        _to_lane_blocks(xb_scr, x_ref[0])

    rows = POS_PER_STEP * nc
    gain = g_ref[...]
    scale = 1.0 + mod_ref[0, 1:2, :]
    shift = mod_ref[0, 0:1, :]

    def norm_mod(x):
        return (_rms(x) * gain * scale + shift).astype(BF16)

    h = norm_mod(x_ref[0, pl.ds(pl.multiple_of(j * rows, rows), rows), :])
    proj = jnp.dot(h, wqkv_ref[...], preferred_element_type=F32) + bqkv_ref[...]
    q_ref[0] = proj[:, :D_ATTN].astype(BF16)
    k_ref[0] = proj[:, D_ATTN:D_ATTN + D_KV].astype(BF16)
    v_ref[0] = proj[:, D_ATTN + D_KV:].astype(BF16)

    hs = jnp.concatenate([norm_mod(_pos_rows(xb_scr, POS_PER_STEP * j + il, nc)) for il in range(POS_PER_STEP)],
                         axis=0)
    ut = lax.dot_general(wut_ref[...], hs, (((1,), (1,)), ((), ())), preferred_element_type=F32) + but_ref[...]
    for il in range(POS_PER_STEP):
        piece = ut[:, il * nc:(il + 1) * nc].astype(BF16)
        ut_ref[0, :, il * SSM_GROUP:(il + 1) * SSM_GROUP, :] = piece.reshape(N_GROUPS, SSM_GROUP, nc)


def _inproj(x, mod, gain, w_in, b_in):
    b, s, d = x.shape
    nc = s // SSM_CHUNK
    rows = POS_PER_STEP * nc
    row = lambda bi, j: (bi, j, 0)
    const = lambda bi, j: (0, 0)
    w_qkv = w_in[:, :D_QKV].astype(BF16)
    w_ut = w_in[:, D_QKV:].T.astype(BF16)
    return pl.pallas_call(
        _inproj_kernel,
        grid=(b, SSM_CHUNK // POS_PER_STEP),
        in_specs=[pl.BlockSpec((1, s, d), lambda bi, j: (bi, 0, 0)),
                  pl.BlockSpec((1, 6, d), lambda bi, j: (bi, 0, 0)),
                  pl.BlockSpec((1, d), const),
                  pl.BlockSpec((d, D_QKV), const),
                  pl.BlockSpec((1, D_QKV), const),
                  pl.BlockSpec((D_SSM, d), const),
                  pl.BlockSpec((D_SSM, 1), const)],
        out_specs=[pl.BlockSpec((1, rows, D_ATTN), row),
                   pl.BlockSpec((1, rows, D_KV), row),
                   pl.BlockSpec((1, rows, D_KV), row),
                   pl.BlockSpec((1, N_GROUPS, POS_PER_STEP * SSM_GROUP, nc), lambda bi, j: (bi, 0, j, 0))],
        out_shape=[jax.ShapeDtypeStruct((b, s, D_ATTN), BF16),
                   jax.ShapeDtypeStruct((b, s, D_KV), BF16),
                   jax.ShapeDtypeStruct((b, s, D_KV), BF16),
                   jax.ShapeDtypeStruct((b, N_GROUPS, SSM_ROW, nc), BF16)],
        scratch_shapes=[pltpu.VMEM((d // LANES, s, LANES), F32)],
        compiler_params=_cparams(2, ROW_VMEM_BYTES),
        name="inproj",
    )(x, mod, gain.reshape(1, d), w_qkv, b_in[:D_QKV].reshape(1, D_QKV), w_ut, b_in[D_QKV:].reshape(D_SSM, 1))


def _half_norm(x, low):
    sq = x * x
    s_lo = jnp.sum(jnp.where(low, sq, 0.0), axis=-1, keepdims=True)
    s_hi = jnp.sum(sq, axis=-1, keepdims=True) - s_lo
    inv = 1.0 / HEAD_DIM
    scale = jnp.where(low, lax.rsqrt(s_lo * inv + EPS), lax.rsqrt(s_hi * inv + EPS))
    return x * scale


def _attn_block(first, q, k_prev, k_cur, v_prev, v_cur, sinks_ref, qn, low, upper, rblk):
    no_prev = jnp.where(first, NEG_INF, 0.0)
    out_blocks = []
    for hk in range(N_KV_HEADS):
        qs = []
        for j in range(Q_PER_KV // 2):
            blk = hk * (Q_PER_KV // 2) + j
            qb = _half_norm(q[:, blk * LANES:(blk + 1) * LANES], low) * qn * (1.0 / math.sqrt(HEAD_DIM))
            qs.append(jnp.where(low, qb, 0.0))
            qs.append(jnp.where(low, 0.0, qb))
        qg = jnp.concatenate(qs, axis=0).astype(BF16)
        nt = (((1,), (1,)), ((), ()))
        s_prev = lax.dot_general(qg, k_prev[hk], nt, preferred_element_type=F32)
        s_cur = lax.dot_general(qg, k_cur[hk], nt, preferred_element_type=F32)
        s = jnp.where(upper, s_prev + no_prev, s_cur)
        sink = jnp.zeros((Q_PER_KV * BLOCK, 1), F32)
        for g in range(Q_PER_KV):
            sink = jnp.where(rblk == g, sinks_ref[hk * Q_PER_KV + g], sink)
        m = jnp.maximum(jnp.max(s, axis=-1, keepdims=True), sink)
        p = jnp.exp(s - m)
        den = jnp.sum(p, axis=-1, keepdims=True) + jnp.exp(sink - m)
        o = (jnp.dot(jnp.where(upper, p, 0.0).astype(BF16), v_prev[hk], preferred_element_type=F32)
             + jnp.dot(jnp.where(upper, 0.0, p).astype(BF16), v_cur[hk], preferred_element_type=F32)) / den
        for j in range(Q_PER_KV // 2):
            ev = o[(2 * j) * BLOCK:(2 * j + 1) * BLOCK]
            od = o[(2 * j + 1) * BLOCK:(2 * j + 2) * BLOCK]
            out_blocks.append(jnp.where(low, ev, od))
    return jnp.concatenate(out_blocks, axis=-1)


def _attn_kernel(sinks_ref, q_ref, k_ref, v_ref, qn_ref, kn_ref, on_ref, o_ref, a_scr):
    step = pl.program_id(1)
    nq = ATTN_ROWS // BLOCK
    low = lax.broadcasted_iota(I32, (1, LANES), 1) < HEAD_DIM
    rows = Q_PER_KV * BLOCK
    upper = lax.broadcasted_iota(I32, (rows, BLOCK), 1) > lax.broadcasted_iota(I32, (rows, BLOCK), 0) % BLOCK
    rblk = lax.broadcasted_iota(I32, (rows, 1), 0) // BLOCK

    cur = pl.multiple_of(step * ATTN_ROWS, ATTN_ROWS)
    prev = pl.multiple_of(jnp.maximum(step * nq - 1, 0) * BLOCK, BLOCK)
    kall = jnp.concatenate([k_ref[0, pl.ds(prev, BLOCK), :], k_ref[0, pl.ds(cur, ATTN_ROWS), :]], axis=0).astype(F32)
    vall = jnp.concatenate([v_ref[0, pl.ds(prev, BLOCK), :], v_ref[0, pl.ds(cur, ATTN_ROWS), :]], axis=0).astype(F32)
    kall = _half_norm(kall, low) * kn_ref[...]
    kswap = pltpu.roll(kall, HEAD_DIM, axis=1)
    vswap = pltpu.roll(vall, HEAD_DIM, axis=1)
    k_dup = [jnp.where(low, kall, kswap).astype(BF16), jnp.where(low, kswap, kall).astype(BF16)]
    v_dup = [jnp.where(low, vall, vswap).astype(BF16), jnp.where(low, vswap, vall).astype(BF16)]
    blk = lambda a, i: [a[hk][i * BLOCK:(i + 1) * BLOCK] for hk in range(N_KV_HEADS)]

    for qb in range(nq):
        q = q_ref[0, qb * BLOCK:(qb + 1) * BLOCK, :].astype(F32)
        attn = _attn_block((step == 0) if qb == 0 else False, q, blk(k_dup, qb), blk(k_dup, qb + 1),
                           blk(v_dup, qb), blk(v_dup, qb + 1), sinks_ref, qn_ref[...], low, upper, rblk)
        attn = _rms(attn) * on_ref[...]
        for kb in range(D_ATTN // LANES):
            a_scr[kb, qb * BLOCK:(qb + 1) * BLOCK, :] = attn[:, kb * LANES:(kb + 1) * LANES]
    for i in range(SSM_CHUNK):
        o_ref[0, i] = _pos_rows(a_scr, i, ATTN_ROWS // SSM_CHUNK).astype(BF16)


def _attention(q, k, v, sinks, q_norm, k_norm, out_norm):
    b, s, _ = q.shape
    tile2 = lambda g: jnp.tile(g.reshape(1, HEAD_DIM), (1, 2))
    cps = ATTN_ROWS // SSM_CHUNK
    return pl.pallas_call(
        _attn_kernel,
        grid=(b, s // ATTN_ROWS),
        in_specs=[pl.BlockSpec(memory_space=pltpu.SMEM),
                  pl.BlockSpec((1, ATTN_ROWS, D_ATTN), lambda bi, n: (bi, n, 0)),
                  pl.BlockSpec((1, s, D_KV), lambda bi, n: (bi, 0, 0)),
                  pl.BlockSpec((1, s, D_KV), lambda bi, n: (bi, 0, 0)),
                  pl.BlockSpec((1, LANES), lambda bi, n: (0, 0)),
                  pl.BlockSpec((1, LANES), lambda bi, n: (0, 0)),
                  pl.BlockSpec((1, D_ATTN), lambda bi, n: (0, 0))],
        out_specs=pl.BlockSpec((1, SSM_CHUNK, cps, D_ATTN), lambda bi, n: (bi, 0, n, 0)),
        out_shape=jax.ShapeDtypeStruct((b, SSM_CHUNK, s // SSM_CHUNK, D_ATTN), BF16),
        scratch_shapes=[pltpu.VMEM((D_ATTN // LANES, ATTN_ROWS, LANES), F32)],
        compiler_params=_cparams(2),
        name="attention",
    )(sinks, q, k, v, tile2(q_norm), tile2(k_norm), out_norm.reshape(1, D_ATTN))


def _cmul(ar, ai, br, bi):
    return ar * br - ai * bi, ar * bi + ai * br


def _ssm_param_kernel(lam_ref, bre_ref, bim_ref, cre_ref, cim_ref, tt_ref, wz_ref, wyt_ref, cs_ref):
    f32dot = functools.partial(jnp.dot, preferred_element_type=F32, precision=HIGHEST)
    lr = lam_ref[0, 0:1, :]
    li = lam_ref[0, 1:2, :]
    dt = jnp.exp(lam_ref[0, 2:3, :])
    rho = lr * dt
    th = li * dt
    imag_lane = lax.broadcasted_iota(I32, (1, LANES), 1) >= STATE

    kk = (lax.broadcasted_iota(I32, (N_POW, 1), 0) - (SSM_CHUNK - 1)).astype(F32)
    mag = jnp.exp(rho * kk)
    pw_r = mag * jnp.cos(th * kk)
    pw_i = mag * jnp.sin(th * kk)
    lb_r = pw_r[SSM_CHUNK:SSM_CHUNK + 1]
    lb_i = pw_i[SSM_CHUNK:SSM_CHUNK + 1]
    den = lr * lr + li * li
    coef_r = ((lb_r - 1.0) * lr + lb_i * li) / den
    coef_i = (lb_i * lr - (lb_r - 1.0) * li) / den

    eye = (lax.broadcasted_iota(I32, (SSM_GROUP, SSM_GROUP), 0)
           == lax.broadcasted_iota(I32, (SSM_GROUP, SSM_GROUP), 1)).astype(F32)
    lane_fold = (lax.broadcasted_iota(I32, (STATE, LANES), 1) % STATE
                 == lax.broadcasted_iota(I32, (STATE, LANES), 0)).astype(F32)

    def tile_pos(x):
        return jnp.concatenate([x] * SSM_CHUNK, axis=0)

    def power_rows(k_of_pos):
        idx = [k_of_pos(p) + (SSM_CHUNK - 1) for p in range(SSM_CHUNK)]
        rep = lambda t: jnp.concatenate([jnp.broadcast_to(t[r:r + 1], (SSM_GROUP, LANES)) for r in idx], axis=0)
        return rep(pw_r), rep(pw_i)

    def b_rows(b_ref):
        b2 = jnp.concatenate([b_ref[0], b_ref[0]], axis=0)
        return tile_pos(lax.dot_general(eye, b2, (((1,), (1,)), ((), ())), preferred_element_type=F32,
                                        precision=HIGHEST))

    def c_rows(c_ref):
        return tile_pos(f32dot(c_ref[0], lane_fold))

    bbar_r, bbar_i = _cmul(coef_r, coef_i, b_rows(bre_ref), b_rows(bim_ref))
    c_r = c_rows(cre_ref)
    c_i = c_rows(cim_ref)

    a_r, a_i = _cmul(bbar_r, bbar_i, *power_rows(lambda p: -p))
    a2c = jnp.where(imag_lane, -a_i, a_r)
    m_r, m_i = _cmul(c_r, c_i, *power_rows(lambda p: p))
    bmc = jnp.where(imag_lane, m_i, m_r)
    tt = f32dot(bmc, a2c.T)
    causal = (lax.broadcasted_iota(I32, (SSM_ROW, 1), 0) // SSM_GROUP
              >= lax.broadcasted_iota(I32, (1, SSM_ROW), 1) // SSM_GROUP)
    tt_ref[0] = jnp.where(causal, tt, 0.0).astype(BF16)

    w_r, w_i = _cmul(bbar_r, bbar_i, *power_rows(lambda p: SSM_CHUNK - 1 - p))
    wz_ref[0, :, :LANES] = jnp.where(imag_lane, w_i, w_r).astype(BF16)
    wz_ref[0, :, LANES:] = jnp.where(imag_lane, w_r, w_i).astype(BF16)

    y_r, y_i = _cmul(c_r, c_i, *power_rows(lambda p: p + 1))
    wyt_ref[0] = jnp.where(imag_lane, -y_i, y_r).astype(BF16)

    cs_ref[0, 0:1, :] = pw_r[N_POW - 1:N_POW]
    cs_ref[0, 1:2, :] = jnp.where(imag_lane, pw_i[N_POW - 1:N_POW], -pw_i[N_POW - 1:N_POW])


def _ssm_params(lam_re, lam_im, log_dt, b_re, b_im, c_re, c_im):
    g = lam_re.shape[0]
    lam = jnp.stack([lam_re, lam_im, jnp.broadcast_to(log_dt[:, None], (g, STATE))], axis=1)
    lam = jnp.concatenate([lam, lam], axis=2)
    blk = lambda *shape: pl.BlockSpec((1,) + shape, lambda i: (i, 0, 0))
    return pl.pallas_call(
        _ssm_param_kernel,
        grid=(g,),
        in_specs=[blk(3, LANES), blk(STATE, SSM_GROUP), blk(STATE, SSM_GROUP), blk(SSM_GROUP, STATE),
                  blk(SSM_GROUP, STATE)],
        out_specs=[blk(SSM_ROW, SSM_ROW), blk(SSM_ROW, SSM_ROW), blk(SSM_ROW, LANES), blk(2, LANES)],
        out_shape=[jax.ShapeDtypeStruct((g, SSM_ROW, SSM_ROW), BF16),
                   jax.ShapeDtypeStruct((g, SSM_ROW, SSM_ROW), BF16),
                   jax.ShapeDtypeStruct((g, SSM_ROW, LANES), BF16),
                   jax.ShapeDtypeStruct((g, 2, LANES), F32)],
        compiler_params=_cparams(1),
        name="ssm_params",
    )(lam, b_re, b_im, c_re, c_im)


def _ssm_kernel(ut_ref, tt_ref, wz_ref, wyt_ref, cs_ref, d_ref, yt_ref, z_scr, s_scr):
    batch, _, _, nc = ut_ref.shape
    ut = jnp.concatenate([ut_ref[b, 0] for b in range(batch)], axis=1)
    _to_lane_blocks(z_scr, lax.dot_general(ut, wz_ref[0], (((0,), (0,)), ((), ())), preferred_element_type=F32))
    c1 = cs_ref[0, 0:1, :]
    c2 = cs_ref[0, 1:2, :]

    def step(c, carry):
        s1, s2 = carry
        rows = pl.ds(c, batch, stride=nc)
        s_scr[rows, :] = s1
        n1 = c1 * s1 + c2 * s2 + z_scr[0, rows, :]
        n2 = c1 * s2 - c2 * s1 + z_scr[1, rows, :]
        return n1, n2

    zero = jnp.zeros((batch, LANES), F32)
    lax.fori_loop(0, nc, step, (zero, zero), unroll=8)
    y = jnp.dot(tt_ref[0], ut, preferred_element_type=F32)
    y = y + lax.dot_general(wyt_ref[0], s_scr[...].astype(BF16), (((1,), (1,)), ((), ())),
                            preferred_element_type=F32)
    y = y + d_ref[0] * ut.astype(F32)
    for b in range(batch):
        yt_ref[b, 0] = y[:, b * nc:(b + 1) * nc]


def _ssm(ut, tt, wz, wyt, cs, d_skip):
    b, g, _, nc = ut.shape
    d_col = jnp.tile(d_skip.reshape(g, 1, SSM_GROUP), (1, SSM_CHUNK, 1)).reshape(g, SSM_ROW, 1)
    blk = lambda *shape: pl.BlockSpec((1,) + shape, lambda i: (i, 0, 0))
    act = pl.BlockSpec((b, 1, SSM_ROW, nc), lambda i: (0, i, 0, 0))
    return pl.pallas_call(
        _ssm_kernel,
        grid=(g,),
        in_specs=[act, blk(SSM_ROW, SSM_ROW), blk(SSM_ROW, SSM_ROW), blk(SSM_ROW, LANES), blk(2, LANES),
                  blk(SSM_ROW, 1)],
        out_specs=act,
        out_shape=jax.ShapeDtypeStruct((b, g, SSM_ROW, nc), F32),
        scratch_shapes=[pltpu.VMEM((SSM_ROW // LANES, b * nc, LANES), F32), pltpu.VMEM((b * nc, LANES), F32)],
        compiler_params=_cparams(1),
        name="ssm",
    )(ut, tt, wz, wyt, cs, d_col)


def _post_kernel(x_ref, attn_ref, yt_ref, mod_ref, wglut_ref, bglu_ref, sn_ref, wout_ref, nf_ref, wr_ref, br_ref,
                 tri_ref, x1_ref, h2_ref, eidx_ref, wts_ref, rank_ref, cnt_ref, carry_ref, xb_scr):
    @pl.when((pl.program_id(0) == 0) & (pl.program_id(1) == 0))
    def _():
        carry_ref[...] = jnp.zeros_like(carry_ref)

    jj = pl.program_id(1)

    @pl.when(jj == 0)
    def _():
        _to_lane_blocks(xb_scr, x_ref[0])

    nc = attn_ref.shape[2]
    ts = POST_POS * nc
    d = x_ref.shape[2]
    yt = jnp.concatenate(
        [yt_ref[0, :, il * SSM_GROUP:(il + 1) * SSM_GROUP, :].reshape(D_SSM, nc) for il in range(POST_POS)], axis=1)
    g = jax.nn.gelu(yt)
    gate = jax.nn.sigmoid(jnp.dot(wglut_ref[...], g.astype(BF16), preferred_element_type=F32) + bglu_ref[...])
    ssm_t = _rms(g * gate, axis=0) * sn_ref[...]
    mixed = jnp.concatenate([attn_ref[0].reshape(ts, D_ATTN), ssm_t.T.astype(BF16)], axis=-1)
    o = jnp.dot(mixed, wout_ref[...], preferred_element_type=F32)
    x = jnp.concatenate([_pos_rows(xb_scr, POST_POS * jj + il, nc) for il in range(POST_POS)], axis=0)
    x1 = x + mod_ref[0, 2:3, :] * o
    x1_ref[0] = x1.reshape(POST_POS, nc, d)
    h2 = _rms(x1) * nf_ref[...] * (1.0 + mod_ref[0, 4:5, :]) + mod_ref[0, 3:4, :]
    packed = _pack_halves(h2)
    for il in range(POST_POS):
        _store_slabs(h2_ref, (0, il), packed[il * nc:(il + 1) * nc])

    logits = lax.dot_general(wr_ref[...], h2.astype(BF16), (((1,), (1,)), ((), ())),
                             preferred_element_type=F32) + br_ref[...]
    iota_e = lax.broadcasted_iota(I32, (N_EXPERTS, ts), 0).astype(F32)
    l = logits
    idxs, vals = [], []
    for _ in range(TOP_K):
        m = jnp.max(l, axis=0, keepdims=True)
        idx = jnp.min(jnp.where(l == m, iota_e, float(N_EXPERTS)), axis=0, keepdims=True)
        idxs.append(idx)
        vals.append(m)
        l = jnp.where(iota_e == idx, -jnp.inf, l)
    es = [jnp.exp(v - vals[0]) for v in vals]
    tot = es[0] + es[1] + es[2] + es[3]
    member = jnp.zeros((N_EXPERTS, ts), F32)
    for idx in idxs:
        member = member + (iota_e == idx).astype(F32)
    before = jnp.dot(member.astype(BF16), tri_ref[...], preferred_element_type=F32) + carry_ref[...]
    for k in range(TOP_K):
        eidx_ref[k:k + 1, :] = idxs[k].astype(I32)
        wts_ref[k:k + 1, :] = es[k] / tot
        rank_ref[k:k + 1, :] = jnp.sum(jnp.where(iota_e == idxs[k], before, 0.0), axis=0, keepdims=True).astype(I32)
    carry = carry_ref[...] + jnp.sum(member, axis=1, keepdims=True)
    carry_ref[...] = carry
    cnt_ref[...] = carry.astype(I32)


def _post(x, attn, yt, mod, w_glu, b_glu, ssm_norm, w_out, norm_ffn, w_router, b_router):
    b, s, d = x.shape
    nc = s // SSM_CHUNK
    ts = POST_POS * nc
    nt = SSM_CHUNK // POST_POS
    t = b * s
    pm = lambda bi, j: (bi, j, 0, 0)
    const = lambda bi, j: (0, 0)
    tok = lambda bi, j: (0, bi * nt + j)
    tri = (lax.broadcasted_iota(I32, (ts, ts), 0) < lax.broadcasted_iota(I32, (ts, ts), 1)).astype(BF16)
    col = lambda a: a.reshape(-1, 1)
    return pl.pallas_call(
        _post_kernel,
        grid=(b, nt),
        in_specs=[pl.BlockSpec((1, s, d), lambda bi, j: (bi, 0, 0)),
                  pl.BlockSpec((1, POST_POS, nc, D_ATTN), pm),
                  pl.BlockSpec((1, N_GROUPS, POST_POS * SSM_GROUP, nc), lambda bi, j: (bi, 0, j, 0)),
                  pl.BlockSpec((1, 6, d), lambda bi, j: (bi, 0, 0)),
                  pl.BlockSpec((D_SSM, D_SSM), const),
                  pl.BlockSpec((D_SSM, 1), const),
                  pl.BlockSpec((D_SSM, 1), const),
                  pl.BlockSpec((d, d), const),
                  pl.BlockSpec((1, d), const),
                  pl.BlockSpec((N_EXPERTS, d), const),
                  pl.BlockSpec((N_EXPERTS, 1), const),
                  pl.BlockSpec((ts, ts), const)],
        out_specs=[pl.BlockSpec((1, POST_POS, nc, d), pm),
                   pl.BlockSpec((1, POST_POS, nc, SLAB, LANES), lambda bi, j: (bi, j, 0, 0, 0)),
                   pl.BlockSpec((TOP_K, ts), tok),
                   pl.BlockSpec((TOP_K, ts), tok),
                   pl.BlockSpec((TOP_K, ts), tok),
                   pl.BlockSpec((N_EXPERTS, 1), const)],
        out_shape=[jax.ShapeDtypeStruct((b, SSM_CHUNK, nc, d), F32),
                   jax.ShapeDtypeStruct((b, SSM_CHUNK, nc, SLAB, LANES), U32),
                   jax.ShapeDtypeStruct((TOP_K, t), I32),
                   jax.ShapeDtypeStruct((TOP_K, t), F32),
                   jax.ShapeDtypeStruct((TOP_K, t), I32),
                   jax.ShapeDtypeStruct((N_EXPERTS, 1), I32)],
        scratch_shapes=[pltpu.VMEM((N_EXPERTS, 1), F32), pltpu.VMEM((d // LANES, s, LANES), F32)],
        compiler_params=_cparams(2, ROW_VMEM_BYTES),
        name="post",
    )(x, attn, yt, mod, w_glu.T.astype(BF16), col(b_glu), col(ssm_norm), w_out.astype(BF16),
      norm_ffn.reshape(1, -1), w_router.T.astype(BF16), col(b_router), tri)


def _route_kernel(eidx_ref, rank_ref, cnt_ref, pos_ref, te_ref, nv_ref, nx_ref):
    cnt = cnt_ref[...]
    tiles = (cnt + (FFN_ROWS - 1)) // FFN_ROWS
    er = lax.broadcasted_iota(I32, (N_EXPERTS, N_EXPERTS), 0)
    ec = lax.broadcasted_iota(I32, (N_EXPERTS, N_EXPERTS), 1)
    ltri = (ec < er).astype(BF16)
    tiles_b = jnp.broadcast_to(tiles.astype(F32), (N_EXPERTS, LANES)).astype(BF16)
    start_t = jnp.dot(ltri, tiles_b, preferred_element_type=F32)[:, 0:1].astype(I32)
    end_t = start_t + tiles
    start = start_t * FFN_ROWS

    t = eidx_ref.shape[1]
    iota_e = lax.broadcasted_iota(I32, (N_EXPERTS, t), 0)
    for k in range(TOP_K):
        sel = jnp.where(iota_e == eidx_ref[k:k + 1, :], start, 0)
        pos_ref[k:k + 1, :] = jnp.sum(sel, axis=0, keepdims=True) + rank_ref[k:k + 1, :]

    nv = jnp.max(end_t, axis=0, keepdims=True)
    width = te_ref.shape[1]
    ti = jnp.minimum(lax.broadcasted_iota(I32, (N_EXPERTS, width), 1), nv - 1)
    te = jnp.minimum(jnp.sum((ti >= end_t).astype(I32), axis=0, keepdims=True), N_EXPERTS - 1)
    te_ref[...] = te
    nv_ref[...] = jnp.broadcast_to(nv, nv_ref.shape)
    ie = lax.broadcasted_iota(I32, (N_EXPERTS, width), 0)
    own_end = jnp.sum(jnp.where(ie == te, end_t, 0), axis=0, keepdims=True)
    nxt = jnp.minimum(jnp.sum((own_end >= end_t).astype(I32), axis=0, keepdims=True), N_EXPERTS - 1)
    nx_ref[...] = jnp.where(own_end < nv, nxt, -1)


def _route(eidx, rank, cnt, n_tiles):
    t = eidx.shape[1]
    width = -(-n_tiles // LANES) * LANES
    return pl.pallas_call(
        _route_kernel,
        out_shape=[jax.ShapeDtypeStruct((TOP_K, t), I32),
                   jax.ShapeDtypeStruct((1, width), I32),
                   jax.ShapeDtypeStruct((1, LANES), I32),
                   jax.ShapeDtypeStruct((1, width), I32)],
        name="route",
    )(eidx, rank, cnt)


def _dispatch_kernel(pos_ref, h_ref, xs_in_ref, xs_ref, sem):
    del xs_in_ref
    tt = h_ref.shape[0]

    def issue(t, carry):
        for k in range(TOP_K):
            pltpu.make_async_copy(h_ref.at[t], xs_ref.at[pos_ref[TOP_K * t + k]], sem).start(priority=k % 2)
        return carry

    lax.fori_loop(0, tt, issue, 0, unroll=8)
    for _ in range(TOP_K):
        pltpu.make_async_copy(h_ref, xs_ref.at[pl.ds(0, tt)], sem).wait()


def _dispatch(pos_flat, h2p, n_rows):
    t = h2p.shape[0]
    tt = min(MOVE_ROWS, t)
    return pl.pallas_call(
        _dispatch_kernel,
        grid=(t // tt,),
        in_specs=[pl.BlockSpec((TOP_K * tt,), lambda i: (i,), memory_space=pltpu.SMEM),
                  pl.BlockSpec((tt, SLAB, LANES), lambda i: (i, 0, 0)),
                  pl.BlockSpec(memory_space=pl.ANY)],
        out_specs=pl.BlockSpec(memory_space=pl.ANY),
        out_shape=jax.ShapeDtypeStruct((n_rows, SLAB, LANES), U32),
        scratch_shapes=[pltpu.SemaphoreType.DMA],
        input_output_aliases={2: 0},
        compiler_params=_cparams(1),
        name="dispatch",
    )(pos_flat, h2p, jnp.zeros((n_rows, SLAB, LANES), U32))


def _ffn_kernel(te_ref, nv_ref, nx_ref, xs_ref, wgu_hbm, bg_ref, bu_ref, wd_hbm, bd_ref, perm_ref, ys_ref,
                wgu_stage, wd_stage, wg_scr, wu_scr, wd_scr, sems):
    i = pl.program_id(0)
    valid = i < nv_ref[0]
    new_expert = (i == 0) | (te_ref[i] != te_ref[jnp.maximum(i - 1, 0)])

    def stage_copies(e):
        return (pltpu.make_async_copy(wgu_hbm.at[e], wgu_stage, sems.at[0]),
                pltpu.make_async_copy(wd_hbm.at[e], wd_stage, sems.at[1]))

    @pl.when(valid & new_expert)
    def _():
        e = te_ref[i]

        @pl.when(i == 0)
        def _():
            for cp in stage_copies(e):
                cp.start()

        for cp in stage_copies(e):
            cp.wait()
        for c in range(2 * D_FF // PERM):
            w = wgu_stage[:, c * PERM:(c + 1) * PERM].astype(BF16)
            pw = jnp.dot(w, perm_ref[...], preferred_element_type=F32).astype(BF16)
            wg_scr[:, c * (PERM // 2):(c + 1) * (PERM // 2)] = pw[:, :PERM // 2]
            wu_scr[:, c * (PERM // 2):(c + 1) * (PERM // 2)] = pw[:, PERM // 2:]
        wd_scr[...] = wd_stage[...].astype(BF16)

        @pl.when(nx_ref[i] >= 0)
        def _():
            for cp in stage_copies(nx_ref[i]):
                cp.start()

    @pl.when(valid)
    def _():
        x_hi, x_lo = _unpack_halves(_load_slabs(xs_ref))
        x_hi = x_hi.astype(BF16)
        x_lo = x_lo.astype(BF16)
        gate = (jnp.dot(x_hi, wg_scr[:HALF, :], preferred_element_type=F32)
                + jnp.dot(x_lo, wg_scr[HALF:, :], preferred_element_type=F32) + bg_ref[0])
        up = (jnp.dot(x_hi, wu_scr[:HALF, :], preferred_element_type=F32)
              + jnp.dot(x_lo, wu_scr[HALF:, :], preferred_element_type=F32) + bu_ref[0])
        gate = jnp.minimum(gate, SWIGLU_LIMIT)
        up = jnp.clip(up, -SWIGLU_LIMIT, SWIGLU_LIMIT)
        act = (up + 1.0) * (gate * jax.nn.sigmoid(SWIGLU_ALPHA * gate))
        y = jnp.dot(act.astype(BF16), wd_scr[...], preferred_element_type=F32) + bd_ref[0]
        _store_slabs(ys_ref, (), _pack_halves(y))


def _ffn(te, nv, nx, xs, w_gate_up, bg, bu, w_down, bd, n_tiles):
    d = D_MODEL
    tile = lambda i, te, nv, nx: (jnp.minimum(i, nv[0] - 1), 0, 0)
    wsel = lambda i, te, nv, nx: (te[i], 0, 0)
    r = lax.broadcasted_iota(I32, (PERM, PERM), 0)
    c = lax.broadcasted_iota(I32, (PERM, PERM), 1)
    perm = (r == jnp.where(c < PERM // 2, 2 * c, 2 * (c - PERM // 2) + 1)).astype(BF16)
    return pl.pallas_call(
        _ffn_kernel,
        grid_spec=pltpu.PrefetchScalarGridSpec(
            num_scalar_prefetch=3,
            grid=(n_tiles,),
            in_specs=[pl.BlockSpec((FFN_ROWS, SLAB, LANES), tile),
                      pl.BlockSpec(memory_space=pl.ANY),
                      pl.BlockSpec((1, 1, D_FF), wsel),
                      pl.BlockSpec((1, 1, D_FF), wsel),
                      pl.BlockSpec(memory_space=pl.ANY),
                      pl.BlockSpec((1, 1, d), wsel),
                      pl.BlockSpec((PERM, PERM), lambda i, te, nv, nx: (0, 0))],
            out_specs=pl.BlockSpec((FFN_ROWS, SLAB, LANES), tile),
            scratch_shapes=[pltpu.VMEM((d, 2 * D_FF), F32), pltpu.VMEM((D_FF, d), F32),
                            pltpu.VMEM((d, D_FF), BF16), pltpu.VMEM((d, D_FF), BF16), pltpu.VMEM((D_FF, d), BF16),
                            pltpu.SemaphoreType.DMA((2,))],
        ),
        out_shape=jax.ShapeDtypeStruct(xs.shape, U32),
        input_output_aliases={3: 0},
        compiler_params=_cparams(1, FFN_VMEM_BYTES),
        name="ffn",
    )(te, nv, nx, xs, w_gate_up, bg, bu, w_down, bd, perm)


def _combine_kernel(pos_ref, x1_ref, w_ref, mod_ref, ys_ref, o_ref, ybuf, ob_scr, sem):
    jj = pl.program_id(1)
    nc = x1_ref.shape[2]
    tt = POST_POS * nc
    d = x1_ref.shape[3]

    def issue(t, carry):
        for k in range(TOP_K):
            pltpu.make_async_copy(ys_ref.at[pos_ref[TOP_K * t + k]], ybuf.at[k, t], sem).start(priority=k % 2)
        return carry

    lax.fori_loop(0, tt, issue, 0, unroll=8)
    for k in range(TOP_K):
        pltpu.make_async_copy(ys_ref.at[pl.ds(0, tt)], ybuf.at[k], sem).wait()

    acc_hi = jnp.zeros((tt, HALF), F32)
    acc_lo = jnp.zeros((tt, HALF), F32)
    for k in range(TOP_K):
        hi, lo = _unpack_halves(_load_slabs(ybuf, (k,)))
        wk = w_ref[:, k:k + 1]
        acc_hi = acc_hi + wk * hi
        acc_lo = acc_lo + wk * lo
    g2 = mod_ref[0, 5:6, :]
    x1 = x1_ref[0].reshape(tt, d)
    out = jnp.concatenate([x1[:, :HALF] + g2[:, :HALF] * acc_hi, x1[:, HALF:] + g2[:, HALF:] * acc_lo], axis=-1)
    for il in range(POST_POS):
        rows = pl.ds(POST_POS * jj + il, nc, stride=SSM_CHUNK)
        for kb in range(d // LANES):
            ob_scr[kb, rows, :] = out[il * nc:(il + 1) * nc, kb * LANES:(kb + 1) * LANES]

    @pl.when(jj == pl.num_programs(1) - 1)
    def _():
        for kb in range(d // LANES):
            o_ref[0, :, kb * LANES:(kb + 1) * LANES] = ob_scr[kb]


def _combine(pos, x1, wts_t, mod, ys):
    b, _, nc, d = x1.shape
    s = SSM_CHUNK * nc
    tt = POST_POS * nc
    nt = SSM_CHUNK // POST_POS
    return pl.pallas_call(
        _combine_kernel,
        grid=(b, nt),
        in_specs=[pl.BlockSpec((TOP_K * tt,), lambda bi, j: (bi * nt + j,), memory_space=pltpu.SMEM),
                  pl.BlockSpec((1, POST_POS, nc, d), lambda bi, j: (bi, j, 0, 0)),
                  pl.BlockSpec((tt, TOP_K), lambda bi, j: (bi * nt + j, 0)),
                  pl.BlockSpec((1, 6, d), lambda bi, j: (bi, 0, 0)),
                  pl.BlockSpec(memory_space=pl.ANY)],
        out_specs=pl.BlockSpec((1, s, d), lambda bi, j: (bi, 0, 0)),
        out_shape=jax.ShapeDtypeStruct((b, s, d), F32),
        scratch_shapes=[pltpu.VMEM((TOP_K, tt, SLAB, LANES), U32), pltpu.VMEM((d // LANES, s, LANES), F32),
                        pltpu.SemaphoreType.DMA],
        compiler_params=_cparams(2, ROW_VMEM_BYTES),
        name="combine",
    )(pos, x1, wts_t, mod, ys)


def kernel(x, c, w_ada, b_ada, norm_mix, w_in, b_in, q_norm, k_norm, sinks, lam_re, lam_im, log_dt, b_re, b_im,
           c_re, c_im, d_skip, w_glu, b_glu, attn_out_norm, ssm_out_norm, w_out, norm_ffn, w_router, b_router,
           w_gate_up, b_gate_up, w_down, b_down):
    b, s, d = x.shape
    t = b * s
    depth = w_ada.shape[0]
    n_tiles = (t * TOP_K) // FFN_ROWS + N_EXPERTS
    for l in range(depth):
        mod = _adaln(c, w_ada[l], b_ada[l]).reshape(b, 6, d)
        q, k, v, ut = _inproj(x, mod, norm_mix[l], w_in[l], b_in[l])
        attn = _attention(q, k, v, sinks[l], q_norm[l], k_norm[l], attn_out_norm[l])
        tt, wz, wyt, cs = _ssm_params(lam_re[l], lam_im[l], log_dt[l], b_re[l], b_im[l], c_re[l], c_im[l])
        yt = _ssm(ut, tt, wz, wyt, cs, d_skip[l])
        x1, h2p, eidx, wts, rank, cnt = _post(x, attn, yt, mod, w_glu[l], b_glu[l], ssm_out_norm[l], w_out[l],
                                              norm_ffn[l], w_router[l], b_router[l])
        pos, te, nv, nx = _route(eidx, rank, cnt, n_tiles)
        pos_flat = pos.T.reshape(-1)
        xs = _dispatch(pos_flat, h2p.reshape(t, SLAB, LANES), n_tiles * FFN_ROWS)
        wgu = w_gate_up[l]
        bgu = b_gate_up[l]
        ys = _ffn(te[0, :n_tiles], nv[0, :1], nx[0, :n_tiles], xs, wgu, bgu[:, None, 0::2], bgu[:, None, 1::2],
                  w_down[l], b_down[l][:, None, :], n_tiles)
        x = _combine(pos_flat, x1, wts.T, mod, ys)
    return x
```

```python
import functools
import math

import jax
import jax.numpy as jnp
from jax import lax
from jax.experimental import pallas as pl
from jax.experimental.pallas import tpu as pltpu

F32 = jnp.float32
BF16 = jnp.bfloat16
U32 = jnp.uint32
I32 = jnp.int32

D_MODEL = 1024
HEAD_DIM = 64
N_HEADS = 8
N_KV_HEADS = 2
Q_PER_KV = N_HEADS // N_KV_HEADS
D_ATTN = N_HEADS * HEAD_DIM
D_KV = N_KV_HEADS * HEAD_DIM
D_QKV = D_ATTN + 2 * D_KV
WINDOW = 128
BLOCK = 128
D_SSM = D_MODEL - D_ATTN
SSM_GROUP = 16
N_GROUPS = D_SSM // SSM_GROUP
STATE = 64
N_EXPERTS = 32
TOP_K = 4
D_FF = D_MODEL
SWIGLU_LIMIT = 7.0
SWIGLU_ALPHA = 1.702
EPS = 1e-6
NEG_INF = -1e30

LANES = 128
SSM_CHUNK = 16
SSM_ROW = SSM_CHUNK * SSM_GROUP
N_POW = 2 * SSM_CHUNK
HALF = D_MODEL // 2
PANELS = HALF // LANES

POS_PER_STEP = 4
ATTN_ROWS = 512
POST_POS = 2
FFN_ROWS = 256
RUN = 16
RUN_SHIFT = 4
TABW = 128
PERM = 256
FFN_VMEM_BYTES = 40 * 1024 * 1024
ROW_VMEM_BYTES = 48 * 1024 * 1024

HIGHEST = lax.Precision.HIGHEST
_ARB = "arbitrary"


def _cparams(n, vmem=None):
    return pltpu.CompilerParams(dimension_semantics=(_ARB,) * n, vmem_limit_bytes=vmem)


def _rms(x, axis=-1):
    return x * lax.rsqrt(jnp.mean(x * x, axis=axis, keepdims=True) + EPS)


def _pack_halves(y):
    hi = lax.bitcast_convert_type(y[:, :HALF].astype(BF16).astype(F32), U32)
    lo = lax.bitcast_convert_type(y[:, HALF:].astype(BF16).astype(F32), U32)
    return (hi & jnp.uint32(0xFFFF0000)) | (lo >> 16)


def _unpack_halves(w):
    hi = lax.bitcast_convert_type(w & jnp.uint32(0xFFFF0000), F32)
    lo = lax.bitcast_convert_type(w << 16, F32)
    return hi, lo


def _store_panels(ref, packed):
    for pb in range(PANELS):
        ref[pb] = packed[:, pb * LANES:(pb + 1) * LANES]


def _load_panels(ref):
    return jnp.concatenate([ref[pb] for pb in range(PANELS)], axis=-1)


def _to_lane_blocks(dst, src):
    for kb in range(dst.shape[0]):
        dst[kb] = src[:, kb * LANES:(kb + 1) * LANES]


def _pos_rows(blocks, i, n_chunks):
    return jnp.concatenate([blocks[kb, pl.ds(i, n_chunks, stride=SSM_CHUNK), :] for kb in range(blocks.shape[0])],
                           axis=-1)


def _adaln_kernel(c_ref, w_ref, b_ref, o_ref):
    c = c_ref[...]
    ca = c * jax.nn.sigmoid(c)
    o_ref[...] = jnp.dot(ca, w_ref[...], preferred_element_type=F32, precision=HIGHEST) + b_ref[...]


def _adaln(c, w_ada, b_ada):
    b, d = c.shape
    n = w_ada.shape[1] // d
    return pl.pallas_call(
        _adaln_kernel,
        grid=(n,),
        in_specs=[pl.BlockSpec((b, d), lambda j: (0, 0)),
                  pl.BlockSpec((d, d), lambda j: (0, j)),
                  pl.BlockSpec((1, d), lambda j: (0, j))],
        out_specs=pl.BlockSpec((b, d), lambda j: (0, j)),
        out_shape=jax.ShapeDtypeStruct((b, n * d), F32),
        compiler_params=_cparams(1),
        name="adaln",
    )(c, w_ada, b_ada.reshape(1, -1))


def _inproj_kernel(x_ref, mod_ref, g_ref, wqkv_ref, bqkv_ref, wut_ref, but_ref, q_ref, k_ref, v_ref, ut_ref, xb_scr):
    j = pl.program_id(1)
    nc = ut_ref.shape[3]

    @pl.when(j == 0)
    def _():
        _to_lane_blocks(xb_scr, x_ref[0])

    rows = POS_PER_STEP * nc
    gain = g_ref[...]
    scale = 1.0 + mod_ref[0, 1:2, :]
    shift = mod_ref[0, 0:1, :]

    def norm_mod(x):
        return (_rms(x) * gain * scale + shift).astype(BF16)

    h = norm_mod(x_ref[0, pl.ds(pl.multiple_of(j * rows, rows), rows), :])
    proj = jnp.dot(h, wqkv_ref[...], preferred_element_type=F32) + bqkv_ref[...]
    q_ref[0] = proj[:, :D_ATTN].astype(BF16)
    k_ref[0] = proj[:, D_ATTN:D_ATTN + D_KV].astype(BF16)
    v_ref[0] = proj[:, D_ATTN + D_KV:].astype(BF16)

    hs = jnp.concatenate([norm_mod(_pos_rows(xb_scr, POS_PER_STEP * j + il, nc)) for il in range(POS_PER_STEP)],
                         axis=0)
    ut = lax.dot_general(wut_ref[...], hs, (((1,), (1,)), ((), ())), preferred_element_type=F32) + but_ref[...]
    for il in range(POS_PER_STEP):
        piece = ut[:, il * nc:(il + 1) * nc].astype(BF16)
        ut_ref[0, :, il * SSM_GROUP:(il + 1) * SSM_GROUP, :] = piece.reshape(N_GROUPS, SSM_GROUP, nc)


def _inproj(x, mod, gain, w_in, b_in):
    b, s, d = x.shape
    nc = s // SSM_CHUNK
    rows = POS_PER_STEP * nc
    row = lambda bi, j: (bi, j, 0)
    const = lambda bi, j: (0, 0)
    w_qkv = w_in[:, :D_QKV].astype(BF16)
    w_ut = w_in[:, D_QKV:].T.astype(BF16)
    return pl.pallas_call(
        _inproj_kernel,
        grid=(b, SSM_CHUNK // POS_PER_STEP),
        in_specs=[pl.BlockSpec((1, s, d), lambda bi, j: (bi, 0, 0)),
                  pl.BlockSpec((1, 6, d), lambda bi, j: (bi, 0, 0)),
                  pl.BlockSpec((1, d), const),
                  pl.BlockSpec((d, D_QKV), const),
                  pl.BlockSpec((1, D_QKV), const),
                  pl.BlockSpec((D_SSM, d), const),
                  pl.BlockSpec((D_SSM, 1), const)],
        out_specs=[pl.BlockSpec((1, rows, D_ATTN), row),
                   pl.BlockSpec((1, rows, D_KV), row),
                   pl.BlockSpec((1, rows, D_KV), row),
                   pl.BlockSpec((1, N_GROUPS, POS_PER_STEP * SSM_GROUP, nc), lambda bi, j: (bi, 0, j, 0))],
        out_shape=[jax.ShapeDtypeStruct((b, s, D_ATTN), BF16),
                   jax.ShapeDtypeStruct((b, s, D_KV), BF16),
                   jax.ShapeDtypeStruct((b, s, D_KV), BF16),
                   jax.ShapeDtypeStruct((b, N_GROUPS, SSM_ROW, nc), BF16)],
        scratch_shapes=[pltpu.VMEM((d // LANES, s, LANES), F32)],
        compiler_params=_cparams(2, ROW_VMEM_BYTES),
        name="inproj",
    )(x, mod, gain.reshape(1, d), w_qkv, b_in[:D_QKV].reshape(1, D_QKV), w_ut, b_in[D_QKV:].reshape(D_SSM, 1))


def _half_norm(x, low):
    sq = x * x
    s_lo = jnp.sum(jnp.where(low, sq, 0.0), axis=-1, keepdims=True)
    s_hi = jnp.sum(sq, axis=-1, keepdims=True) - s_lo
    inv = 1.0 / HEAD_DIM
    scale = jnp.where(low, lax.rsqrt(s_lo * inv + EPS), lax.rsqrt(s_hi * inv + EPS))
    return x * scale


def _attn_block(first, q, k_prev, k_cur, v_prev, v_cur, sinks_ref, qn, low, upper, rblk):
    no_prev = jnp.where(first, NEG_INF, 0.0)
    out_blocks = []
    for hk in range(N_KV_HEADS):
        qs = []
        for j in range(Q_PER_KV // 2):
            blk = hk * (Q_PER_KV // 2) + j
            qb = _half_norm(q[:, blk * LANES:(blk + 1) * LANES], low) * qn * (1.0 / math.sqrt(HEAD_DIM))
            qs.append(jnp.where(low, qb, 0.0))
            qs.append(jnp.where(low, 0.0, qb))
        qg = jnp.concatenate(qs, axis=0).astype(BF16)
        nt = (((1,), (1,)), ((), ()))
        s_prev = lax.dot_general(qg, k_prev[hk], nt, preferred_element_type=F32)
        s_cur = lax.dot_general(qg, k_cur[hk], nt, preferred_element_type=F32)
        s = jnp.where(upper, s_prev + no_prev, s_cur)
        sink = jnp.zeros((Q_PER_KV * BLOCK, 1), F32)
        for g in range(Q_PER_KV):
            sink = jnp.where(rblk == g, sinks_ref[hk * Q_PER_KV + g], sink)
        m = jnp.maximum(jnp.max(s, axis=-1, keepdims=True), sink)
        p = jnp.exp(s - m)
        den = jnp.sum(p, axis=-1, keepdims=True) + jnp.exp(sink - m)
        o = (jnp.dot(jnp.where(upper, p, 0.0).astype(BF16), v_prev[hk], preferred_element_type=F32)
             + jnp.dot(jnp.where(upper, 0.0, p).astype(BF16), v_cur[hk], preferred_element_type=F32)) / den
        for j in range(Q_PER_KV // 2):
            ev = o[(2 * j) * BLOCK:(2 * j + 1) * BLOCK]
            od = o[(2 * j + 1) * BLOCK:(2 * j + 2) * BLOCK]
            out_blocks.append(jnp.where(low, ev, od))
    return jnp.concatenate(out_blocks, axis=-1)


def _attn_kernel(sinks_ref, q_ref, k_ref, v_ref, qn_ref, kn_ref, on_ref, o_ref, a_scr):
    step = pl.program_id(1)
    nq = ATTN_ROWS // BLOCK
    low = lax.broadcasted_iota(I32, (1, LANES), 1) < HEAD_DIM
    rows = Q_PER_KV * BLOCK
    upper = lax.broadcasted_iota(I32, (rows, BLOCK), 1) > lax.broadcasted_iota(I32, (rows, BLOCK), 0) % BLOCK
    rblk = lax.broadcasted_iota(I32, (rows, 1), 0) // BLOCK

    cur = pl.multiple_of(step * ATTN_ROWS, ATTN_ROWS)
    prev = pl.multiple_of(jnp.maximum(step * nq - 1, 0) * BLOCK, BLOCK)
    kall = jnp.concatenate([k_ref[0, pl.ds(prev, BLOCK), :], k_ref[0, pl.ds(cur, ATTN_ROWS), :]], axis=0).astype(F32)
    vall = jnp.concatenate([v_ref[0, pl.ds(prev, BLOCK), :], v_ref[0, pl.ds(cur, ATTN_ROWS), :]], axis=0).astype(F32)
    kall = _half_norm(kall, low) * kn_ref[...]
    kswap = pltpu.roll(kall, HEAD_DIM, axis=1)
    vswap = pltpu.roll(vall, HEAD_DIM, axis=1)
    k_dup = [jnp.where(low, kall, kswap).astype(BF16), jnp.where(low, kswap, kall).astype(BF16)]
    v_dup = [jnp.where(low, vall, vswap).astype(BF16), jnp.where(low, vswap, vall).astype(BF16)]
    blk = lambda a, i: [a[hk][i * BLOCK:(i + 1) * BLOCK] for hk in range(N_KV_HEADS)]

    for qb in range(nq):
        q = q_ref[0, qb * BLOCK:(qb + 1) * BLOCK, :].astype(F32)
        attn = _attn_block((step == 0) if qb == 0 else False, q, blk(k_dup, qb), blk(k_dup, qb + 1),
                           blk(v_dup, qb), blk(v_dup, qb + 1), sinks_ref, qn_ref[...], low, upper, rblk)
        attn = _rms(attn) * on_ref[...]
        for kb in range(D_ATTN // LANES):
            a_scr[kb, qb * BLOCK:(qb + 1) * BLOCK, :] = attn[:, kb * LANES:(kb + 1) * LANES]
    for i in range(SSM_CHUNK):
        o_ref[0, i] = _pos_rows(a_scr, i, ATTN_ROWS // SSM_CHUNK).astype(BF16)


def _attention(q, k, v, sinks, q_norm, k_norm, out_norm):
    b, s, _ = q.shape
    tile2 = lambda g: jnp.tile(g.reshape(1, HEAD_DIM), (1, 2))
    cps = ATTN_ROWS // SSM_CHUNK
    return pl.pallas_call(
        _attn_kernel,
        grid=(b, s // ATTN_ROWS),
        in_specs=[pl.BlockSpec(memory_space=pltpu.SMEM),
                  pl.BlockSpec((1, ATTN_ROWS, D_ATTN), lambda bi, n: (bi, n, 0)),
                  pl.BlockSpec((1, s, D_KV), lambda bi, n: (bi, 0, 0)),
                  pl.BlockSpec((1, s, D_KV), lambda bi, n: (bi, 0, 0)),
                  pl.BlockSpec((1, LANES), lambda bi, n: (0, 0)),
                  pl.BlockSpec((1, LANES), lambda bi, n: (0, 0)),
                  pl.BlockSpec((1, D_ATTN), lambda bi, n: (0, 0))],
        out_specs=pl.BlockSpec((1, SSM_CHUNK, cps, D_ATTN), lambda bi, n: (bi, 0, n, 0)),
        out_shape=jax.ShapeDtypeStruct((b, SSM_CHUNK, s // SSM_CHUNK, D_ATTN), BF16),
        scratch_shapes=[pltpu.VMEM((D_ATTN // LANES, ATTN_ROWS, LANES), F32)],
        compiler_params=_cparams(2),
        name="attention",
    )(sinks, q, k, v, tile2(q_norm), tile2(k_norm), out_norm.reshape(1, D_ATTN))


def _cmul(ar, ai, br, bi):
    return ar * br - ai * bi, ar * bi + ai * br


def _ssm_param_kernel(lam_ref, bre_ref, bim_ref, cre_ref, cim_ref, tt_ref, wz_ref, wyt_ref, cs_ref):
    f32dot = functools.partial(jnp.dot, preferred_element_type=F32, precision=HIGHEST)
    lr = lam_ref[0, 0:1, :]
    li = lam_ref[0, 1:2, :]
    dt = jnp.exp(lam_ref[0, 2:3, :])
    rho = lr * dt
    th = li * dt
    imag_lane = lax.broadcasted_iota(I32, (1, LANES), 1) >= STATE

    kk = (lax.broadcasted_iota(I32, (N_POW, 1), 0) - (SSM_CHUNK - 1)).astype(F32)
    mag = jnp.exp(rho * kk)
    pw_r = mag * jnp.cos(th * kk)
    pw_i = mag * jnp.sin(th * kk)
    lb_r = pw_r[SSM_CHUNK:SSM_CHUNK + 1]
    lb_i = pw_i[SSM_CHUNK:SSM_CHUNK + 1]
    den = lr * lr + li * li
    coef_r = ((lb_r - 1.0) * lr + lb_i * li) / den
    coef_i = (lb_i * lr - (lb_r - 1.0) * li) / den

    eye = (lax.broadcasted_iota(I32, (SSM_GROUP, SSM_GROUP), 0)
           == lax.broadcasted_iota(I32, (SSM_GROUP, SSM_GROUP), 1)).astype(F32)
    lane_fold = (lax.broadcasted_iota(I32, (STATE, LANES), 1) % STATE
                 == lax.broadcasted_iota(I32, (STATE, LANES), 0)).astype(F32)

    def tile_pos(x):
        return jnp.concatenate([x] * SSM_CHUNK, axis=0)

    def power_rows(k_of_pos):
        idx = [k_of_pos(p) + (SSM_CHUNK - 1) for p in range(SSM_CHUNK)]
        rep = lambda t: jnp.concatenate([jnp.broadcast_to(t[r:r + 1], (SSM_GROUP, LANES)) for r in idx], axis=0)
        return rep(pw_r), rep(pw_i)

    def b_rows(b_ref):
        b2 = jnp.concatenate([b_ref[0], b_ref[0]], axis=0)
        return tile_pos(lax.dot_general(eye, b2, (((1,), (1,)), ((), ())), preferred_element_type=F32,
                                        precision=HIGHEST))

    def c_rows(c_ref):
        return tile_pos(f32dot(c_ref[0], lane_fold))

    bbar_r, bbar_i = _cmul(coef_r, coef_i, b_rows(bre_ref), b_rows(bim_ref))
    c_r = c_rows(cre_ref)
    c_i = c_rows(cim_ref)

    a_r, a_i = _cmul(bbar_r, bbar_i, *power_rows(lambda p: -p))
    a2c = jnp.where(imag_lane, -a_i, a_r)
    m_r, m_i = _cmul(c_r, c_i, *power_rows(lambda p: p))
    bmc = jnp.where(imag_lane, m_i, m_r)
    tt = f32dot(bmc, a2c.T)
    causal = (lax.broadcasted_iota(I32, (SSM_ROW, 1), 0) // SSM_GROUP
              >= lax.broadcasted_iota(I32, (1, SSM_ROW), 1) // SSM_GROUP)
    tt_ref[0] = jnp.where(causal, tt, 0.0).astype(BF16)

    w_r, w_i = _cmul(bbar_r, bbar_i, *power_rows(lambda p: SSM_CHUNK - 1 - p))
    wz_ref[0, :, :LANES] = jnp.where(imag_lane, w_i, w_r).astype(BF16)
    wz_ref[0, :, LANES:] = jnp.where(imag_lane, w_r, w_i).astype(BF16)

    y_r, y_i = _cmul(c_r, c_i, *power_rows(lambda p: p + 1))
    wyt_ref[0] = jnp.where(imag_lane, -y_i, y_r).astype(BF16)

    cs_ref[0, 0:1, :] = pw_r[N_POW - 1:N_POW]
    cs_ref[0, 1:2, :] = jnp.where(imag_lane, pw_i[N_POW - 1:N_POW], -pw_i[N_POW - 1:N_POW])


def _ssm_params(lam_re, lam_im, log_dt, b_re, b_im, c_re, c_im):
    g = lam_re.shape[0]
    lam = jnp.stack([lam_re, lam_im, jnp.broadcast_to(log_dt[:, None], (g, STATE))], axis=1)
    lam = jnp.concatenate([lam, lam], axis=2)
    blk = lambda *shape: pl.BlockSpec((1,) + shape, lambda i: (i, 0, 0))
    return pl.pallas_call(
        _ssm_param_kernel,
        grid=(g,),
        in_specs=[blk(3, LANES), blk(STATE, SSM_GROUP), blk(STATE, SSM_GROUP), blk(SSM_GROUP, STATE),
                  blk(SSM_GROUP, STATE)],
        out_specs=[blk(SSM_ROW, SSM_ROW), blk(SSM_ROW, SSM_ROW), blk(SSM_ROW, LANES), blk(2, LANES)],
        out_shape=[jax.ShapeDtypeStruct((g, SSM_ROW, SSM_ROW), BF16),
                   jax.ShapeDtypeStruct((g, SSM_ROW, SSM_ROW), BF16),
                   jax.ShapeDtypeStruct((g, SSM_ROW, LANES), BF16),
                   jax.ShapeDtypeStruct((g, 2, LANES), F32)],
        compiler_params=_cparams(1),
        name="ssm_params",
    )(lam, b_re, b_im, c_re, c_im)


def _ssm_kernel(ut_ref, tt_ref, wz_ref, wyt_ref, cs_ref, d_ref, yt_ref, z_scr, s_scr):
    batch, _, _, nc = ut_ref.shape
    ut = jnp.concatenate([ut_ref[b, 0] for b in range(batch)], axis=1)
    _to_lane_blocks(z_scr, lax.dot_general(ut, wz_ref[0], (((0,), (0,)), ((), ())), preferred_element_type=F32))
    c1 = cs_ref[0, 0:1, :]
    c2 = cs_ref[0, 1:2, :]

    def step(c, carry):
        s1, s2 = carry
        rows = pl.ds(c, batch, stride=nc)
        s_scr[rows, :] = s1
        n1 = c1 * s1 + c2 * s2 + z_scr[0, rows, :]
        n2 = c1 * s2 - c2 * s1 + z_scr[1, rows, :]
        return n1, n2

    zero = jnp.zeros((batch, LANES), F32)
    lax.fori_loop(0, nc, step, (zero, zero), unroll=8)
    y = jnp.dot(tt_ref[0], ut, preferred_element_type=F32)
    y = y + lax.dot_general(wyt_ref[0], s_scr[...].astype(BF16), (((1,), (1,)), ((), ())),
                            preferred_element_type=F32)
    y = y + d_ref[0] * ut.astype(F32)
    for b in range(batch):
        yt_ref[b, 0] = y[:, b * nc:(b + 1) * nc]


def _ssm(ut, tt, wz, wyt, cs, d_skip):
    b, g, _, nc = ut.shape
    d_col = jnp.tile(d_skip.reshape(g, 1, SSM_GROUP), (1, SSM_CHUNK, 1)).reshape(g, SSM_ROW, 1)
    blk = lambda *shape: pl.BlockSpec((1,) + shape, lambda i: (i, 0, 0))
    act = pl.BlockSpec((b, 1, SSM_ROW, nc), lambda i: (0, i, 0, 0))
    return pl.pallas_call(
        _ssm_kernel,
        grid=(g,),
        in_specs=[act, blk(SSM_ROW, SSM_ROW), blk(SSM_ROW, SSM_ROW), blk(SSM_ROW, LANES), blk(2, LANES),
                  blk(SSM_ROW, 1)],
        out_specs=act,
        out_shape=jax.ShapeDtypeStruct((b, g, SSM_ROW, nc), F32),
        scratch_shapes=[pltpu.VMEM((SSM_ROW // LANES, b * nc, LANES), F32), pltpu.VMEM((b * nc, LANES), F32)],
        compiler_params=_cparams(1),
        name="ssm",
    )(ut, tt, wz, wyt, cs, d_col)


def _post_kernel(x_ref, attn_ref, yt_ref, mod_ref, wglut_ref, bglu_ref, sn_ref, wout_ref, nf_ref, wr_ref, br_ref,
                 tri_ref, x1_ref, h2_ref, eidx_ref, wts_ref, lrank_ref, r0_ref, cnt_ref, carry_ref, xb_scr):
    @pl.when((pl.program_id(0) == 0) & (pl.program_id(1) == 0))
    def _():
        carry_ref[...] = jnp.zeros_like(carry_ref)

    jj = pl.program_id(1)

    @pl.when(jj == 0)
    def _():
        _to_lane_blocks(xb_scr, x_ref[0])

    nc = attn_ref.shape[2]
    ts = POST_POS * nc
    d = x_ref.shape[2]
    yt = jnp.concatenate(
        [yt_ref[0, :, il * SSM_GROUP:(il + 1) * SSM_GROUP, :].reshape(D_SSM, nc) for il in range(POST_POS)], axis=1)
    g = jax.nn.gelu(yt)
    gate = jax.nn.sigmoid(jnp.dot(wglut_ref[...], g.astype(BF16), preferred_element_type=F32) + bglu_ref[...])
    ssm_t = _rms(g * gate, axis=0) * sn_ref[...]
    mixed = jnp.concatenate([attn_ref[0].reshape(ts, D_ATTN), ssm_t.T.astype(BF16)], axis=-1)
    o = jnp.dot(mixed, wout_ref[...], preferred_element_type=F32)
    x = jnp.concatenate([_pos_rows(xb_scr, POST_POS * jj + il, nc) for il in range(POST_POS)], axis=0)
    x1 = x + mod_ref[0, 2:3, :] * o
    x1_ref[0] = x1.reshape(POST_POS, nc, d)
    h2 = _rms(x1) * nf_ref[...] * (1.0 + mod_ref[0, 4:5, :]) + mod_ref[0, 3:4, :]
    h2_ref[0] = h2.astype(BF16).reshape(POST_POS, nc, d)

    logits = lax.dot_general(wr_ref[...], h2.astype(BF16), (((1,), (1,)), ((), ())),
                             preferred_element_type=F32) + br_ref[...]
    iota_e = lax.broadcasted_iota(I32, (N_EXPERTS, ts), 0).astype(F32)
    l = logits
    idxs, vals = [], []
    for _ in range(TOP_K):
        m = jnp.max(l, axis=0, keepdims=True)
        idx = jnp.min(jnp.where(l == m, iota_e, float(N_EXPERTS)), axis=0, keepdims=True)
        idxs.append(idx)
        vals.append(m)
        l = jnp.where(iota_e == idx, -jnp.inf, l)
    es = [jnp.exp(v - vals[0]) for v in vals]
    tot = es[0] + es[1] + es[2] + es[3]
    member = jnp.zeros((N_EXPERTS, ts), F32)
    for idx in idxs:
        member = member + (iota_e == idx).astype(F32)
    before = jnp.dot(member.astype(BF16), tri_ref[...], preferred_element_type=F32)
    for k in range(TOP_K):
        eidx_ref[k:k + 1, :] = idxs[k].astype(I32)
        wts_ref[k:k + 1, :] = es[k] / tot
        lrank_ref[k:k + 1, :] = jnp.sum(jnp.where(iota_e == idxs[k], before, 0.0), axis=0, keepdims=True).astype(I32)
    r0_ref[0] = carry_ref[...].astype(I32)
    carry = carry_ref[...] + jnp.sum(member, axis=1, keepdims=True)
    carry_ref[...] = carry
    cnt_ref[...] = carry.astype(I32)


def _post(x, attn, yt, mod, w_glu, b_glu, ssm_norm, w_out, norm_ffn, w_router, b_router):
    b, s, d = x.shape
    nc = s // SSM_CHUNK
    ts = POST_POS * nc
    nt = SSM_CHUNK // POST_POS
    t = b * s
    pm = lambda bi, j: (bi, j, 0, 0)
    const = lambda bi, j: (0, 0)
    tok = lambda bi, j: (0, bi * nt + j)
    tri = (lax.broadcasted_iota(I32, (ts, ts), 0) < lax.broadcasted_iota(I32, (ts, ts), 1)).astype(BF16)
    col = lambda a: a.reshape(-1, 1)
    return pl.pallas_call(
        _post_kernel,
        grid=(b, nt),
        in_specs=[pl.BlockSpec((1, s, d), lambda bi, j: (bi, 0, 0)),
                  pl.BlockSpec((1, POST_POS, nc, D_ATTN), pm),
                  pl.BlockSpec((1, N_GROUPS, POST_POS * SSM_GROUP, nc), lambda bi, j: (bi, 0, j, 0)),
                  pl.BlockSpec((1, 6, d), lambda bi, j: (bi, 0, 0)),
                  pl.BlockSpec((D_SSM, D_SSM), const),
                  pl.BlockSpec((D_SSM, 1), const),
                  pl.BlockSpec((D_SSM, 1), const),
                  pl.BlockSpec((d, d), const),
                  pl.BlockSpec((1, d), const),
                  pl.BlockSpec((N_EXPERTS, d), const),
                  pl.BlockSpec((N_EXPERTS, 1), const),
                  pl.BlockSpec((ts, ts), const)],
        out_specs=[pl.BlockSpec((1, POST_POS, nc, d), pm),
                   pl.BlockSpec((1, POST_POS, nc, d), pm),
                   pl.BlockSpec((TOP_K, ts), tok),
                   pl.BlockSpec((TOP_K, ts), tok),
                   pl.BlockSpec((TOP_K, ts), tok),
                   pl.BlockSpec((1, N_EXPERTS, 1), lambda bi, j: (bi * nt + j, 0, 0)),
                   pl.BlockSpec((N_EXPERTS, 1), const)],
        out_shape=[jax.ShapeDtypeStruct((b, SSM_CHUNK, nc, d), F32),
                   jax.ShapeDtypeStruct((b, SSM_CHUNK, nc, d), BF16),
                   jax.ShapeDtypeStruct((TOP_K, t), I32),
                   jax.ShapeDtypeStruct((TOP_K, t), F32),
                   jax.ShapeDtypeStruct((TOP_K, t), I32),
                   jax.ShapeDtypeStruct((b * nt, N_EXPERTS, 1), I32),
                   jax.ShapeDtypeStruct((N_EXPERTS, 1), I32)],
        scratch_shapes=[pltpu.VMEM((N_EXPERTS, 1), F32), pltpu.VMEM((d // LANES, s, LANES), F32)],
        compiler_params=_cparams(2, ROW_VMEM_BYTES),
        name="post",
    )(x, attn, yt, mod, w_glu.T.astype(BF16), col(b_glu), col(ssm_norm), w_out.astype(BF16),
      norm_ffn.reshape(1, -1), w_router.T.astype(BF16), col(b_router), tri)


def _route_kernel(eidx_ref, lrank_ref, r0_ref, cnt_ref, ls_ref, tab_ref, te_ref, nv_ref, nx_ref, pad_ref):
    cnt = cnt_ref[...]
    tiles = (cnt + (RUN - 1 + FFN_ROWS - 1)) // FFN_ROWS
    er = lax.broadcasted_iota(I32, (N_EXPERTS, N_EXPERTS), 0)
    ec = lax.broadcasted_iota(I32, (N_EXPERTS, N_EXPERTS), 1)
    ltri = (ec < er).astype(BF16)

    def excl_cumsum(v):
        vb = jnp.broadcast_to(v.astype(F32), (N_EXPERTS, LANES)).astype(BF16)
        return jnp.dot(ltri, vb, preferred_element_type=F32)[:, 0:1].astype(I32)

    start_t = excl_cumsum(tiles)
    end_t = start_t + tiles
    start = start_t * FFN_ROWS
    pad_ref[...] = start + cnt

    nb = r0_ref.shape[0]
    ts = eidx_ref.shape[1] // nb
    iota_e = lax.broadcasted_iota(I32, (N_EXPERTS, ts), 0)
    iota_t = lax.broadcasted_iota(I32, (N_EXPERTS, TABW), 0)
    chunk = lax.broadcasted_iota(I32, (1, TABW), 1)

    def block(b, carry):
        lanes = pl.ds(pl.multiple_of(b * ts, ts), ts)
        sels = [iota_e == eidx_ref[k:k + 1, lanes] for k in range(TOP_K)]
        member = sels[0].astype(I32) + sels[1].astype(I32) + sels[2].astype(I32) + sels[3].astype(I32)
        nch = (jnp.sum(member, axis=1, keepdims=True) + (RUN - 1)) // RUN
        cb = excl_cumsum(nch)
        end_c = cb + nch
        for k in range(TOP_K):
            first = jnp.sum(jnp.where(sels[k], cb, 0), axis=0, keepdims=True)
            lr = lrank_ref[k:k + 1, lanes]
            ls_ref[k:k + 1, lanes] = (first + lax.shift_right_logical(lr, RUN_SHIFT)) * RUN + (lr & (RUN - 1))
        e_of_c = jnp.sum((chunk >= end_c).astype(I32), axis=0, keepdims=True)
        sel_c = iota_t == e_of_c
        first_c = jnp.sum(jnp.where(sel_c, cb, 0), axis=0, keepdims=True)
        slot0_c = jnp.sum(jnp.where(sel_c, start + r0_ref[b], 0), axis=0, keepdims=True)
        n_chunks = jnp.max(end_c, axis=0, keepdims=True)
        row = jnp.where(chunk < n_chunks, slot0_c + (chunk - first_c) * RUN, -1)
        tab_ref[pl.ds(b, 1), :] = jnp.where(chunk == TABW - 1, n_chunks, row)
        return carry

    lax.fori_loop(0, nb, block, 0)

    nv = jnp.max(end_t, axis=0, keepdims=True)
    width = te_ref.shape[1]
    ti = jnp.minimum(lax.broadcasted_iota(I32, (N_EXPERTS, width), 1), nv - 1)
    te = jnp.minimum(jnp.sum((ti >= end_t).astype(I32), axis=0, keepdims=True), N_EXPERTS - 1)
    te_ref[...] = te
    nv_ref[...] = jnp.broadcast_to(nv, nv_ref.shape)
    ie = lax.broadcasted_iota(I32, (N_EXPERTS, width), 0)
    own_end = jnp.sum(jnp.where(ie == te, end_t, 0), axis=0, keepdims=True)
    nxt = jnp.minimum(jnp.sum((own_end >= end_t).astype(I32), axis=0, keepdims=True), N_EXPERTS - 1)
    nx_ref[...] = jnp.where(own_end < nv, nxt, -1)


def _route(eidx, lrank, r0, cnt, n_tiles):
    t = eidx.shape[1]
    nb = r0.shape[0]
    width = -(-n_tiles // LANES) * LANES
    return pl.pallas_call(
        _route_kernel,
        out_shape=[jax.ShapeDtypeStruct((TOP_K, t), I32),
                   jax.ShapeDtypeStruct((nb, TABW), I32),
                   jax.ShapeDtypeStruct((1, width), I32),
                   jax.ShapeDtypeStruct((1, LANES), I32),
                   jax.ShapeDtypeStruct((1, width), I32),
                   jax.ShapeDtypeStruct((N_EXPERTS, 1), I32)],
        name="route",
    )(eidx, lrank, r0, cnt)


def _local_rows(ts):
    return ts * TOP_K + N_EXPERTS * RUN


def _dispatch_kernel(tab_ref, pad_ref, nvt_ref, h_ref, ls_ref, xs_ref, buf, zbuf, sems, zsem):
    b = pl.program_id(0)
    slot = b % 2
    ts = h_ref.shape[0]
    local = buf.shape[2]

    def chunk_copy(sl, blk, c):
        rows = pl.ds(pl.multiple_of(c * RUN, RUN), RUN)
        return pltpu.make_async_copy(buf.at[sl, :, rows, :], xs_ref.at[:, pl.ds(tab_ref[blk * TABW + c], RUN), :],
                                     sems.at[sl])

    def for_chunks(blk, fn):
        def body(c, carry):
            fn(c)
            return carry
        lax.fori_loop(0, tab_ref[blk * TABW + TABW - 1], body, 0)

    @pl.when(b == 0)
    def _():
        zbuf[...] = jnp.zeros_like(zbuf)
        zrows = zbuf.shape[1]
        zero = lambda row: pltpu.make_async_copy(zbuf, xs_ref.at[:, pl.ds(row, zrows), :], zsem)
        for phase in range(3):
            for e in range(phase, N_EXPERTS, 3):
                zero(pad_ref[e]).start()
            for e in range(phase, N_EXPERTS, 3):
                zero(pad_ref[e]).wait()
        ztile = lambda i: pltpu.make_async_copy(zbuf.at[:, pl.ds(0, FFN_ROWS), :],
                                                xs_ref.at[:, pl.ds((nvt_ref[0] + i) * FFN_ROWS, FFN_ROWS), :], zsem)

        def tail_start(i, carry):
            ztile(i).start()
            return carry

        def tail_wait(i, carry):
            ztile(i).wait()
            return carry
        lax.fori_loop(0, nvt_ref[1] - nvt_ref[0], tail_start, 0)
        lax.fori_loop(0, nvt_ref[1] - nvt_ref[0], tail_wait, 0)

    r = lax.broadcasted_iota(I32, (local, ts), 0)
    hit = (r == ls_ref[0:1, :]) | (r == ls_ref[1:2, :]) | (r == ls_ref[2:3, :]) | (r == ls_ref[3:4, :])
    srt = jnp.dot(hit.astype(BF16), h_ref[...], preferred_element_type=F32)
    packed = _pack_halves(srt)
    for pb in range(PANELS):
        buf[slot, pb] = packed[:, pb * LANES:(pb + 1) * LANES]

    @pl.when(b > 0)
    def _():
        for_chunks(b - 1, lambda c: chunk_copy(1 - slot, b - 1, c).wait())

    for_chunks(b, lambda c: chunk_copy(slot, b, c).start())

    @pl.when(b == pl.num_programs(0) - 1)
    def _():
        for_chunks(b, lambda c: chunk_copy(slot, b, c).wait())


def _dispatch(tab, pad, nvt, h2, ls, n_rows):
    t, d = h2.shape
    nb = tab.shape[0] // TABW
    ts = t // nb
    return pl.pallas_call(
        _dispatch_kernel,
        grid_spec=pltpu.PrefetchScalarGridSpec(
            num_scalar_prefetch=3,
            grid=(nb,),
            in_specs=[pl.BlockSpec((ts, d), lambda i, *_: (i, 0)),
                      pl.BlockSpec((TOP_K, ts), lambda i, *_: (0, i))],
            out_specs=pl.BlockSpec(memory_space=pl.ANY),
            scratch_shapes=[pltpu.VMEM((2, PANELS, _local_rows(ts), LANES), U32),
                            pltpu.VMEM((PANELS, FFN_ROWS + RUN, LANES), U32),
                            pltpu.SemaphoreType.DMA((2,)), pltpu.SemaphoreType.DMA],
        ),
        out_shape=jax.ShapeDtypeStruct((PANELS, n_rows, LANES), U32),
        compiler_params=_cparams(1, ROW_VMEM_BYTES),
        name="dispatch",
    )(tab, pad, nvt, h2, ls)


def _ffn_kernel(te_ref, nv_ref, nx_ref, xs_ref, wgu_hbm, bg_ref, bu_ref, wd_hbm, bd_ref, perm_ref, ys_ref,
                wgu_stage, wd_stage, wg_scr, wu_scr, wd_scr, sems):
    i = pl.program_id(0)
    valid = i < nv_ref[0]
    new_expert = (i == 0) | (te_ref[i] != te_ref[jnp.maximum(i - 1, 0)])

    def stage_copies(e):
        return (pltpu.make_async_copy(wgu_hbm.at[e], wgu_stage, sems.at[0]),
                pltpu.make_async_copy(wd_hbm.at[e], wd_stage, sems.at[1]))

    @pl.when(valid & new_expert)
    def _():
        e = te_ref[i]

        @pl.when(i == 0)
        def _():
            for cp in stage_copies(e):
                cp.start()

        for cp in stage_copies(e):
            cp.wait()
        for c in range(2 * D_FF // PERM):
            w = wgu_stage[:, c * PERM:(c + 1) * PERM].astype(BF16)
            pw = jnp.dot(w, perm_ref[...], preferred_element_type=F32).astype(BF16)
            wg_scr[:, c * (PERM // 2):(c + 1) * (PERM // 2)] = pw[:, :PERM // 2]
            wu_scr[:, c * (PERM // 2):(c + 1) * (PERM // 2)] = pw[:, PERM // 2:]
        wd_scr[...] = wd_stage[...].astype(BF16)

        @pl.when(nx_ref[i] >= 0)
        def _():
            for cp in stage_copies(nx_ref[i]):
                cp.start()

    @pl.when(valid)
    def _():
        x_hi, x_lo = _unpack_halves(_load_panels(xs_ref))
        x_hi = x_hi.astype(BF16)
        x_lo = x_lo.astype(BF16)
        gate = (jnp.dot(x_hi, wg_scr[:HALF, :], preferred_element_type=F32)
                + jnp.dot(x_lo, wg_scr[HALF:, :], preferred_element_type=F32) + bg_ref[0])
        up = (jnp.dot(x_hi, wu_scr[:HALF, :], preferred_element_type=F32)
              + jnp.dot(x_lo, wu_scr[HALF:, :], preferred_element_type=F32) + bu_ref[0])
        gate = jnp.minimum(gate, SWIGLU_LIMIT)
        up = jnp.clip(up, -SWIGLU_LIMIT, SWIGLU_LIMIT)
        act = (up + 1.0) * (gate * jax.nn.sigmoid(SWIGLU_ALPHA * gate))
        y = jnp.dot(act.astype(BF16), wd_scr[...], preferred_element_type=F32) + bd_ref[0]
        _store_panels(ys_ref, _pack_halves(y))


def _ffn(te, nv, nx, xs, w_gate_up, bg, bu, w_down, bd, n_tiles):
    d = D_MODEL
    tile = lambda i, te, nv, nx: (0, jnp.minimum(i, nv[0] - 1), 0)
    wsel = lambda i, te, nv, nx: (te[i], 0, 0)
    r = lax.broadcasted_iota(I32, (PERM, PERM), 0)
    c = lax.broadcasted_iota(I32, (PERM, PERM), 1)
    perm = (r == jnp.where(c < PERM // 2, 2 * c, 2 * (c - PERM // 2) + 1)).astype(BF16)
    return pl.pallas_call(
        _ffn_kernel,
        grid_spec=pltpu.PrefetchScalarGridSpec(
            num_scalar_prefetch=3,
            grid=(n_tiles,),
            in_specs=[pl.BlockSpec((PANELS, FFN_ROWS, LANES), tile),
                      pl.BlockSpec(memory_space=pl.ANY),
                      pl.BlockSpec((1, 1, D_FF), wsel),
                      pl.BlockSpec((1, 1, D_FF), wsel),
                      pl.BlockSpec(memory_space=pl.ANY),
                      pl.BlockSpec((1, 1, d), wsel),
                      pl.BlockSpec((PERM, PERM), lambda i, te, nv, nx: (0, 0))],
            out_specs=pl.BlockSpec((PANELS, FFN_ROWS, LANES), tile),
            scratch_shapes=[pltpu.VMEM((d, 2 * D_FF), F32), pltpu.VMEM((D_FF, d), F32),
                            pltpu.VMEM((d, D_FF), BF16), pltpu.VMEM((d, D_FF), BF16), pltpu.VMEM((D_FF, d), BF16),
                            pltpu.SemaphoreType.DMA((2,))],
        ),
        out_shape=jax.ShapeDtypeStruct(xs.shape, U32),
        input_output_aliases={3: 0},
        compiler_params=_cparams(1, FFN_VMEM_BYTES),
        name="ffn",
    )(te, nv, nx, xs, w_gate_up, bg, bu, w_down, bd, perm)


def _combine_kernel(tab_ref, x1_ref, ls_ref, w_ref, mod_ref, ys_ref, o_ref, ybuf, ob_scr, sems):
    jj = pl.program_id(1)
    nt = pl.num_programs(1)
    blk = pl.program_id(0) * nt + jj
    n_blk = pl.num_programs(0) * nt
    slot = blk % 2
    nc = x1_ref.shape[2]
    tt = POST_POS * nc
    d = x1_ref.shape[3]
    local = ybuf.shape[2]

    def chunk_copy(sl, bk, c):
        rows = pl.ds(pl.multiple_of(c * RUN, RUN), RUN)
        return pltpu.make_async_copy(ys_ref.at[:, pl.ds(tab_ref[bk * TABW + c], RUN), :], ybuf.at[sl, :, rows, :],
                                     sems.at[sl])

    def for_chunks(bk, fn):
        def body(c, carry):
            fn(c)
            return carry
        lax.fori_loop(0, tab_ref[bk * TABW + TABW - 1], body, 0)

    @pl.when(blk == 0)
    def _():
        ybuf[...] = jnp.zeros_like(ybuf)
        for_chunks(0, lambda c: chunk_copy(0, 0, c).start())

    @pl.when(blk + 1 < n_blk)
    def _():
        for_chunks(blk + 1, lambda c: chunk_copy(1 - slot, blk + 1, c).start())

    for_chunks(blk, lambda c: chunk_copy(slot, blk, c).wait())

    r = lax.broadcasted_iota(I32, (tt, local), 1)
    wm = jnp.zeros((tt, local), F32)
    for k in range(TOP_K):
        wm = wm + jnp.where(r == ls_ref[:, k:k + 1], w_ref[:, k:k + 1], 0.0)
    wm = wm.astype(BF16)
    halves = [_unpack_halves(ybuf[slot, pb]) for pb in range(PANELS)]
    y_hi = jnp.concatenate([h.astype(BF16) for h, _ in halves], axis=-1)
    y_lo = jnp.concatenate([l.astype(BF16) for _, l in halves], axis=-1)
    acc_hi = jnp.dot(wm, y_hi, preferred_element_type=F32)
    acc_lo = jnp.dot(wm, y_lo, preferred_element_type=F32)
    g2 = mod_ref[0, 5:6, :]
    x1 = x1_ref[0].reshape(tt, d)
    out = jnp.concatenate([x1[:, :HALF] + g2[:, :HALF] * acc_hi, x1[:, HALF:] + g2[:, HALF:] * acc_lo], axis=-1)
    for il in range(POST_POS):
        rows = pl.ds(POST_POS * jj + il, nc, stride=SSM_CHUNK)
        for kb in range(d // LANES):
            ob_scr[kb, rows, :] = out[il * nc:(il + 1) * nc, kb * LANES:(kb + 1) * LANES]

    @pl.when(jj == pl.num_programs(1) - 1)
    def _():
        for kb in range(d // LANES):
            o_ref[0, :, kb * LANES:(kb + 1) * LANES] = ob_scr[kb]


def _combine(tab, x1, ls_t, wts_t, mod, ys):
    b, _, nc, d = x1.shape
    s = SSM_CHUNK * nc
    tt = POST_POS * nc
    nt = SSM_CHUNK // POST_POS
    return pl.pallas_call(
        _combine_kernel,
        grid_spec=pltpu.PrefetchScalarGridSpec(
            num_scalar_prefetch=1,
            grid=(b, nt),
            in_specs=[pl.BlockSpec((1, POST_POS, nc, d), lambda bi, j, *_: (bi, j, 0, 0)),
                      pl.BlockSpec((tt, TOP_K), lambda bi, j, *_: (bi * nt + j, 0)),
                      pl.BlockSpec((tt, TOP_K), lambda bi, j, *_: (bi * nt + j, 0)),
                      pl.BlockSpec((1, 6, d), lambda bi, j, *_: (bi, 0, 0)),
                      pl.BlockSpec(memory_space=pl.ANY)],
            out_specs=pl.BlockSpec((1, s, d), lambda bi, j, *_: (bi, 0, 0)),
            scratch_shapes=[pltpu.VMEM((2, PANELS, _local_rows(tt), LANES), U32),
                            pltpu.VMEM((d // LANES, s, LANES), F32),
                            pltpu.SemaphoreType.DMA((2,))],
        ),
        out_shape=jax.ShapeDtypeStruct((b, s, d), F32),
        compiler_params=_cparams(2, ROW_VMEM_BYTES),
        name="combine",
    )(tab, x1, ls_t, wts_t, mod, ys)


def kernel(x, c, w_ada, b_ada, norm_mix, w_in, b_in, q_norm, k_norm, sinks, lam_re, lam_im, log_dt, b_re, b_im,
           c_re, c_im, d_skip, w_glu, b_glu, attn_out_norm, ssm_out_norm, w_out, norm_ffn, w_router, b_router,
           w_gate_up, b_gate_up, w_down, b_down):
    b, s, d = x.shape
    t = b * s
    depth = w_ada.shape[0]
    n_tiles = -(-(t * TOP_K + N_EXPERTS * (RUN - 1 + FFN_ROWS - 1)) // FFN_ROWS)
    n_alloc = n_tiles + 2
    for l in range(depth):
        mod = _adaln(c, w_ada[l], b_ada[l]).reshape(b, 6, d)
        q, k, v, ut = _inproj(x, mod, norm_mix[l], w_in[l], b_in[l])
        attn = _attention(q, k, v, sinks[l], q_norm[l], k_norm[l], attn_out_norm[l])
        tt, wz, wyt, cs = _ssm_params(lam_re[l], lam_im[l], log_dt[l], b_re[l], b_im[l], c_re[l], c_im[l])
        yt = _ssm(ut, tt, wz, wyt, cs, d_skip[l])
        x1, h2, eidx, wts, lrank, r0, cnt = _post(x, attn, yt, mod, w_glu[l], b_glu[l], ssm_out_norm[l], w_out[l],
                                                  norm_ffn[l], w_router[l], b_router[l])
        ls, tab, te, nv, nx, pad = _route(eidx, lrank, r0, cnt, n_tiles)
        tab = tab.reshape(-1)
        nvt = jnp.stack([nv[0, 0], jnp.int32(n_alloc)])
        xs = _dispatch(tab, pad.reshape(-1), nvt, h2.reshape(t, d), ls, n_alloc * FFN_ROWS)
        wgu = w_gate_up[l]
        bgu = b_gate_up[l]
        ys = _ffn(te[0, :n_tiles], nv[0, :1], nx[0, :n_tiles], xs, wgu, bgu[:, None, 0::2], bgu[:, None, 1::2],
                  w_down[l], b_down[l][:, None, :], n_tiles)
        x = _combine(tab, x1, ls.T, wts.T, mod, ys)
    return x
```

```python
import functools
import math

import jax
import jax.numpy as jnp
from jax import lax
from jax.experimental import pallas as pl
from jax.experimental.pallas import tpu as pltpu

F32 = jnp.float32
BF16 = jnp.bfloat16
U32 = jnp.uint32
I32 = jnp.int32

D_MODEL = 1024
HEAD_DIM = 64
N_HEADS = 8
N_KV_HEADS = 2
Q_PER_KV = N_HEADS // N_KV_HEADS
D_ATTN = N_HEADS * HEAD_DIM
D_KV = N_KV_HEADS * HEAD_DIM
D_QKV = D_ATTN + 2 * D_KV
WINDOW = 128
BLOCK = 128
D_SSM = D_MODEL - D_ATTN
SSM_GROUP = 16
N_GROUPS = D_SSM // SSM_GROUP
STATE = 64
N_EXPERTS = 32
TOP_K = 4
D_FF = D_MODEL
SWIGLU_LIMIT = 7.0
SWIGLU_ALPHA = 1.702
EPS = 1e-6
NEG_INF = -1e30

LANES = 128
SSM_CHUNK = 16
SSM_ROW = SSM_CHUNK * SSM_GROUP
N_POW = 2 * SSM_CHUNK
HALF = D_MODEL // 2
PANELS = HALF // LANES

POS_PER_STEP = 4
ATTN_ROWS = 512
POST_POS = 2
FFN_ROWS = 256
RUN = 16
RUN_SHIFT = 4
TABW = 128
PERM = 256
FFN_VMEM_BYTES = 40 * 1024 * 1024
ROW_VMEM_BYTES = 48 * 1024 * 1024

HIGHEST = lax.Precision.HIGHEST
_ARB = "arbitrary"


def _cparams(n, vmem=None):
    return pltpu.CompilerParams(dimension_semantics=(_ARB,) * n, vmem_limit_bytes=vmem)


def _rms(x, axis=-1):
    return x * lax.rsqrt(jnp.mean(x * x, axis=axis, keepdims=True) + EPS)


def _pack_halves(y):
    hi = lax.bitcast_convert_type(y[:, :HALF].astype(BF16).astype(F32), U32)
    lo = lax.bitcast_convert_type(y[:, HALF:].astype(BF16).astype(F32), U32)
    return (hi & jnp.uint32(0xFFFF0000)) | (lo >> 16)


def _unpack_halves(w):
    hi = lax.bitcast_convert_type(w & jnp.uint32(0xFFFF0000), F32)
    lo = lax.bitcast_convert_type(w << 16, F32)
    return hi, lo


def _store_panels(ref, packed):
    for pb in range(PANELS):
        ref[pb] = packed[:, pb * LANES:(pb + 1) * LANES]


def _load_panels(ref):
    return jnp.concatenate([ref[pb] for pb in range(PANELS)], axis=-1)


def _to_lane_blocks(dst, src):
    for kb in range(dst.shape[0]):
        dst[kb] = src[:, kb * LANES:(kb + 1) * LANES]


def _pos_rows(blocks, i, n_chunks):
    return jnp.concatenate([blocks[kb, pl.ds(i, n_chunks, stride=SSM_CHUNK), :] for kb in range(blocks.shape[0])],
                           axis=-1)


def _adaln_kernel(c_ref, w_ref, b_ref, o_ref):
    c = c_ref[...]
    ca = c * jax.nn.sigmoid(c)
    o_ref[...] = jnp.dot(ca, w_ref[...], preferred_element_type=F32, precision=HIGHEST) + b_ref[...]


def _adaln(c, w_ada, b_ada):
    b, d = c.shape
    n = w_ada.shape[1] // d
    return pl.pallas_call(
        _adaln_kernel,
        grid=(n,),
        in_specs=[pl.BlockSpec((b, d), lambda j: (0, 0)),
                  pl.BlockSpec((d, d), lambda j: (0, j)),
                  pl.BlockSpec((1, d), lambda j: (0, j))],
        out_specs=pl.BlockSpec((b, d), lambda j: (0, j)),
        out_shape=jax.ShapeDtypeStruct((b, n * d), F32),
        compiler_params=_cparams(1),
        name="adaln",
    )(c, w_ada, b_ada.reshape(1, -1))


def _inproj_kernel(x_ref, mod_ref, g_ref, wqkv_ref, bqkv_ref, wut_ref, but_ref, q_ref, k_ref, v_ref, ut_ref, xb_scr):
    j = pl.program_id(1)
    nc = ut_ref.shape[3]

    @pl.when(j == 0)
    def _():
        _to_lane_blocks(xb_scr, x_ref[0])

    rows = POS_PER_STEP * nc
    gain = g_ref[...]
    scale = 1.0 + mod_ref[0, 1:2, :]
    shift = mod_ref[0, 0:1, :]

    def norm_mod(x):
        return (_rms(x) * gain * scale + shift).astype(BF16)

    h = norm_mod(x_ref[0, pl.ds(pl.multiple_of(j * rows, rows), rows), :])
    proj = jnp.dot(h, wqkv_ref[...], preferred_element_type=F32) + bqkv_ref[...]
    q_ref[0] = proj[:, :D_ATTN].astype(BF16)
    k_ref[0] = proj[:, D_ATTN:D_ATTN + D_KV].astype(BF16)
    v_ref[0] = proj[:, D_ATTN + D_KV:].astype(BF16)

    hs = jnp.concatenate([norm_mod(_pos_rows(xb_scr, POS_PER_STEP * j + il, nc)) for il in range(POS_PER_STEP)],
                         axis=0)
    ut = lax.dot_general(wut_ref[...], hs, (((1,), (1,)), ((), ())), preferred_element_type=F32) + but_ref[...]
    for il in range(POS_PER_STEP):
        piece = ut[:, il * nc:(il + 1) * nc].astype(BF16)
        ut_ref[0, :, il * SSM_GROUP:(il + 1) * SSM_GROUP, :] = piece.reshape(N_GROUPS, SSM_GROUP, nc)


def _inproj(x, mod, gain, w_in, b_in):
    b, s, d = x.shape
    nc = s // SSM_CHUNK
    rows = POS_PER_STEP * nc
    row = lambda bi, j: (bi, j, 0)
    const = lambda bi, j: (0, 0)
    w_qkv = w_in[:, :D_QKV].astype(BF16)
    w_ut = w_in[:, D_QKV:].T.astype(BF16)
    return pl.pallas_call(
        _inproj_kernel,
        grid=(b, SSM_CHUNK // POS_PER_STEP),
        in_specs=[pl.BlockSpec((1, s, d), lambda bi, j: (bi, 0, 0)),
                  pl.BlockSpec((1, 6, d), lambda bi, j: (bi, 0, 0)),
                  pl.BlockSpec((1, d), const),
                  pl.BlockSpec((d, D_QKV), const),
                  pl.BlockSpec((1, D_QKV), const),
                  pl.BlockSpec((D_SSM, d), const),
                  pl.BlockSpec((D_SSM, 1), const)],
        out_specs=[pl.BlockSpec((1, rows, D_ATTN), row),
                   pl.BlockSpec((1, rows, D_KV), row),
                   pl.BlockSpec((1, rows, D_KV), row),
                   pl.BlockSpec((1, N_GROUPS, POS_PER_STEP * SSM_GROUP, nc), lambda bi, j: (bi, 0, j, 0))],
        out_shape=[jax.ShapeDtypeStruct((b, s, D_ATTN), BF16),
                   jax.ShapeDtypeStruct((b, s, D_KV), BF16),
                   jax.ShapeDtypeStruct((b, s, D_KV), BF16),
                   jax.ShapeDtypeStruct((b, N_GROUPS, SSM_ROW, nc), BF16)],
        scratch_shapes=[pltpu.VMEM((d // LANES, s, LANES), F32)],
        compiler_params=_cparams(2, ROW_VMEM_BYTES),
        name="inproj",
    )(x, mod, gain.reshape(1, d), w_qkv, b_in[:D_QKV].reshape(1, D_QKV), w_ut, b_in[D_QKV:].reshape(D_SSM, 1))


def _half_norm(x, low):
    sq = x * x
    s_lo = jnp.sum(jnp.where(low, sq, 0.0), axis=-1, keepdims=True)
    s_hi = jnp.sum(sq, axis=-1, keepdims=True) - s_lo
    inv = 1.0 / HEAD_DIM
    scale = jnp.where(low, lax.rsqrt(s_lo * inv + EPS), lax.rsqrt(s_hi * inv + EPS))
    return x * scale


def _attn_block(first, q, k_prev, k_cur, v_prev, v_cur, sinks_ref, qn, low, upper, rblk):
    no_prev = jnp.where(first, NEG_INF, 0.0)
    out_blocks = []
    for hk in range(N_KV_HEADS):
        qs = []
        for j in range(Q_PER_KV // 2):
            blk = hk * (Q_PER_KV // 2) + j
            qb = _half_norm(q[:, blk * LANES:(blk + 1) * LANES], low) * qn * (1.0 / math.sqrt(HEAD_DIM))
            qs.append(jnp.where(low, qb, 0.0))
            qs.append(jnp.where(low, 0.0, qb))
        qg = jnp.concatenate(qs, axis=0).astype(BF16)
        nt = (((1,), (1,)), ((), ()))
        s_prev = lax.dot_general(qg, k_prev[hk], nt, preferred_element_type=F32)
        s_cur = lax.dot_general(qg, k_cur[hk], nt, preferred_element_type=F32)
        s = jnp.where(upper, s_prev + no_prev, s_cur)
        sink = jnp.zeros((Q_PER_KV * BLOCK, 1), F32)
        for g in range(Q_PER_KV):
            sink = jnp.where(rblk == g, sinks_ref[hk * Q_PER_KV + g], sink)
        m = jnp.maximum(jnp.max(s, axis=-1, keepdims=True), sink)
        p = jnp.exp(s - m)
        den = jnp.sum(p, axis=-1, keepdims=True) + jnp.exp(sink - m)
        o = (jnp.dot(jnp.where(upper, p, 0.0).astype(BF16), v_prev[hk], preferred_element_type=F32)
             + jnp.dot(jnp.where(upper, 0.0, p).astype(BF16), v_cur[hk], preferred_element_type=F32)) / den
        for j in range(Q_PER_KV // 2):
            ev = o[(2 * j) * BLOCK:(2 * j + 1) * BLOCK]
            od = o[(2 * j + 1) * BLOCK:(2 * j + 2) * BLOCK]
            out_blocks.append(jnp.where(low, ev, od))
    return jnp.concatenate(out_blocks, axis=-1)


def _attn_kernel(sinks_ref, q_ref, k_ref, v_ref, qn_ref, kn_ref, on_ref, o_ref, a_scr):
    step = pl.program_id(1)
    nq = ATTN_ROWS // BLOCK
    low = lax.broadcasted_iota(I32, (1, LANES), 1) < HEAD_DIM
    rows = Q_PER_KV * BLOCK
    upper = lax.broadcasted_iota(I32, (rows, BLOCK), 1) > lax.broadcasted_iota(I32, (rows, BLOCK), 0) % BLOCK
    rblk = lax.broadcasted_iota(I32, (rows, 1), 0) // BLOCK

    cur = pl.multiple_of(step * ATTN_ROWS, ATTN_ROWS)
    prev = pl.multiple_of(jnp.maximum(step * nq - 1, 0) * BLOCK, BLOCK)
    kall = jnp.concatenate([k_ref[0, pl.ds(prev, BLOCK), :], k_ref[0, pl.ds(cur, ATTN_ROWS), :]], axis=0).astype(F32)
    vall = jnp.concatenate([v_ref[0, pl.ds(prev, BLOCK), :], v_ref[0, pl.ds(cur, ATTN_ROWS), :]], axis=0).astype(F32)
    kall = _half_norm(kall, low) * kn_ref[...]
    kswap = pltpu.roll(kall, HEAD_DIM, axis=1)
    vswap = pltpu.roll(vall, HEAD_DIM, axis=1)
    k_dup = [jnp.where(low, kall, kswap).astype(BF16), jnp.where(low, kswap, kall).astype(BF16)]
    v_dup = [jnp.where(low, vall, vswap).astype(BF16), jnp.where(low, vswap, vall).astype(BF16)]
    blk = lambda a, i: [a[hk][i * BLOCK:(i + 1) * BLOCK] for hk in range(N_KV_HEADS)]

    for qb in range(nq):
        q = q_ref[0, qb * BLOCK:(qb + 1) * BLOCK, :].astype(F32)
        attn = _attn_block((step == 0) if qb == 0 else False, q, blk(k_dup, qb), blk(k_dup, qb + 1),
                           blk(v_dup, qb), blk(v_dup, qb + 1), sinks_ref, qn_ref[...], low, upper, rblk)
        attn = _rms(attn) * on_ref[...]
        for kb in range(D_ATTN // LANES):
            a_scr[kb, qb * BLOCK:(qb + 1) * BLOCK, :] = attn[:, kb * LANES:(kb + 1) * LANES]
    for i in range(SSM_CHUNK):
        o_ref[0, i] = _pos_rows(a_scr, i, ATTN_ROWS // SSM_CHUNK).astype(BF16)


def _attention(q, k, v, sinks, q_norm, k_norm, out_norm):
    b, s, _ = q.shape
    tile2 = lambda g: jnp.tile(g.reshape(1, HEAD_DIM), (1, 2))
    cps = ATTN_ROWS // SSM_CHUNK
    return pl.pallas_call(
        _attn_kernel,
        grid=(b, s // ATTN_ROWS),
        in_specs=[pl.BlockSpec(memory_space=pltpu.SMEM),
                  pl.BlockSpec((1, ATTN_ROWS, D_ATTN), lambda bi, n: (bi, n, 0)),
                  pl.BlockSpec((1, s, D_KV), lambda bi, n: (bi, 0, 0)),
                  pl.BlockSpec((1, s, D_KV), lambda bi, n: (bi, 0, 0)),
                  pl.BlockSpec((1, LANES), lambda bi, n: (0, 0)),
                  pl.BlockSpec((1, LANES), lambda bi, n: (0, 0)),
                  pl.BlockSpec((1, D_ATTN), lambda bi, n: (0, 0))],
        out_specs=pl.BlockSpec((1, SSM_CHUNK, cps, D_ATTN), lambda bi, n: (bi, 0, n, 0)),
        out_shape=jax.ShapeDtypeStruct((b, SSM_CHUNK, s // SSM_CHUNK, D_ATTN), BF16),
        scratch_shapes=[pltpu.VMEM((D_ATTN // LANES, ATTN_ROWS, LANES), F32)],
        compiler_params=_cparams(2),
        name="attention",
    )(sinks, q, k, v, tile2(q_norm), tile2(k_norm), out_norm.reshape(1, D_ATTN))


def _cmul(ar, ai, br, bi):
    return ar * br - ai * bi, ar * bi + ai * br


def _ssm_param_kernel(lam_ref, bre_ref, bim_ref, cre_ref, cim_ref, tt_ref, wz_ref, wyt_ref, cs_ref):
    f32dot = functools.partial(jnp.dot, preferred_element_type=F32, precision=HIGHEST)
    lr = lam_ref[0, 0:1, :]
    li = lam_ref[0, 1:2, :]
    dt = jnp.exp(lam_ref[0, 2:3, :])
    rho = lr * dt
    th = li * dt
    imag_lane = lax.broadcasted_iota(I32, (1, LANES), 1) >= STATE

    kk = (lax.broadcasted_iota(I32, (N_POW, 1), 0) - (SSM_CHUNK - 1)).astype(F32)
    mag = jnp.exp(rho * kk)
    pw_r = mag * jnp.cos(th * kk)
    pw_i = mag * jnp.sin(th * kk)
    lb_r = pw_r[SSM_CHUNK:SSM_CHUNK + 1]
    lb_i = pw_i[SSM_CHUNK:SSM_CHUNK + 1]
    den = lr * lr + li * li
    coef_r = ((lb_r - 1.0) * lr + lb_i * li) / den
    coef_i = (lb_i * lr - (lb_r - 1.0) * li) / den

    eye = (lax.broadcasted_iota(I32, (SSM_GROUP, SSM_GROUP), 0)
           == lax.broadcasted_iota(I32, (SSM_GROUP, SSM_GROUP), 1)).astype(F32)
    lane_fold = (lax.broadcasted_iota(I32, (STATE, LANES), 1) % STATE
                 == lax.broadcasted_iota(I32, (STATE, LANES), 0)).astype(F32)

    def tile_pos(x):
        return jnp.concatenate([x] * SSM_CHUNK, axis=0)

    def power_rows(k_of_pos):
        idx = [k_of_pos(p) + (SSM_CHUNK - 1) for p in range(SSM_CHUNK)]
        rep = lambda t: jnp.concatenate([jnp.broadcast_to(t[r:r + 1], (SSM_GROUP, LANES)) for r in idx], axis=0)
        return rep(pw_r), rep(pw_i)

    def b_rows(b_ref):
        b2 = jnp.concatenate([b_ref[0], b_ref[0]], axis=0)
        return tile_pos(lax.dot_general(eye, b2, (((1,), (1,)), ((), ())), preferred_element_type=F32,
                                        precision=HIGHEST))

    def c_rows(c_ref):
        return tile_pos(f32dot(c_ref[0], lane_fold))

    bbar_r, bbar_i = _cmul(coef_r, coef_i, b_rows(bre_ref), b_rows(bim_ref))
    c_r = c_rows(cre_ref)
    c_i = c_rows(cim_ref)

    a_r, a_i = _cmul(bbar_r, bbar_i, *power_rows(lambda p: -p))
    a2c = jnp.where(imag_lane, -a_i, a_r)
    m_r, m_i = _cmul(c_r, c_i, *power_rows(lambda p: p))
    bmc = jnp.where(imag_lane, m_i, m_r)
    tt = f32dot(bmc, a2c.T)
    causal = (lax.broadcasted_iota(I32, (SSM_ROW, 1), 0) // SSM_GROUP
              >= lax.broadcasted_iota(I32, (1, SSM_ROW), 1) // SSM_GROUP)
    tt_ref[0] = jnp.where(causal, tt, 0.0).astype(BF16)

    w_r, w_i = _cmul(bbar_r, bbar_i, *power_rows(lambda p: SSM_CHUNK - 1 - p))
    wz_ref[0, :, :LANES] = jnp.where(imag_lane, w_i, w_r).astype(BF16)
    wz_ref[0, :, LANES:] = jnp.where(imag_lane, w_r, w_i).astype(BF16)

    y_r, y_i = _cmul(c_r, c_i, *power_rows(lambda p: p + 1))
    wyt_ref[0] = jnp.where(imag_lane, -y_i, y_r).astype(BF16)

    cs_ref[0, 0:1, :] = pw_r[N_POW - 1:N_POW]
    cs_ref[0, 1:2, :] = jnp.where(imag_lane, pw_i[N_POW - 1:N_POW], -pw_i[N_POW - 1:N_POW])


def _ssm_params(lam_re, lam_im, log_dt, b_re, b_im, c_re, c_im):
    g = lam_re.shape[0]
    lam = jnp.stack([lam_re, lam_im, jnp.broadcast_to(log_dt[:, None], (g, STATE))], axis=1)
    lam = jnp.concatenate([lam, lam], axis=2)
    blk = lambda *shape: pl.BlockSpec((1,) + shape, lambda i: (i, 0, 0))
    return pl.pallas_call(
        _ssm_param_kernel,
        grid=(g,),
        in_specs=[blk(3, LANES), blk(STATE, SSM_GROUP), blk(STATE, SSM_GROUP), blk(SSM_GROUP, STATE),
                  blk(SSM_GROUP, STATE)],
        out_specs=[blk(SSM_ROW, SSM_ROW), blk(SSM_ROW, SSM_ROW), blk(SSM_ROW, LANES), blk(2, LANES)],
        out_shape=[jax.ShapeDtypeStruct((g, SSM_ROW, SSM_ROW), BF16),
                   jax.ShapeDtypeStruct((g, SSM_ROW, SSM_ROW), BF16),
                   jax.ShapeDtypeStruct((g, SSM_ROW, LANES), BF16),
                   jax.ShapeDtypeStruct((g, 2, LANES), F32)],
        compiler_params=_cparams(1),
        name="ssm_params",
    )(lam, b_re, b_im, c_re, c_im)


def _ssm_kernel(ut_ref, tt_ref, wz_ref, wyt_ref, cs_ref, d_ref, yt_ref, z_scr, s_scr):
    batch, _, _, nc = ut_ref.shape
    ut = jnp.concatenate([ut_ref[b, 0] for b in range(batch)], axis=1)
    _to_lane_blocks(z_scr, lax.dot_general(ut, wz_ref[0], (((0,), (0,)), ((), ())), preferred_element_type=F32))
    c1 = cs_ref[0, 0:1, :]
    c2 = cs_ref[0, 1:2, :]

    def step(c, carry):
        s1, s2 = carry
        rows = pl.ds(c, batch, stride=nc)
        s_scr[rows, :] = s1
        n1 = c1 * s1 + c2 * s2 + z_scr[0, rows, :]
        n2 = c1 * s2 - c2 * s1 + z_scr[1, rows, :]
        return n1, n2

    zero = jnp.zeros((batch, LANES), F32)
    lax.fori_loop(0, nc, step, (zero, zero), unroll=8)
    y = jnp.dot(tt_ref[0], ut, preferred_element_type=F32)
    y = y + lax.dot_general(wyt_ref[0], s_scr[...].astype(BF16), (((1,), (1,)), ((), ())),
                            preferred_element_type=F32)
    y = y + d_ref[0] * ut.astype(F32)
    for b in range(batch):
        yt_ref[b, 0] = y[:, b * nc:(b + 1) * nc]


def _ssm(ut, tt, wz, wyt, cs, d_skip):
    b, g, _, nc = ut.shape
    d_col = jnp.tile(d_skip.reshape(g, 1, SSM_GROUP), (1, SSM_CHUNK, 1)).reshape(g, SSM_ROW, 1)
    blk = lambda *shape: pl.BlockSpec((1,) + shape, lambda i: (i, 0, 0))
    act = pl.BlockSpec((b, 1, SSM_ROW, nc), lambda i: (0, i, 0, 0))
    return pl.pallas_call(
        _ssm_kernel,
        grid=(g,),
        in_specs=[act, blk(SSM_ROW, SSM_ROW), blk(SSM_ROW, SSM_ROW), blk(SSM_ROW, LANES), blk(2, LANES),
                  blk(SSM_ROW, 1)],
        out_specs=act,
        out_shape=jax.ShapeDtypeStruct((b, g, SSM_ROW, nc), F32),
        scratch_shapes=[pltpu.VMEM((SSM_ROW // LANES, b * nc, LANES), F32), pltpu.VMEM((b * nc, LANES), F32)],
        compiler_params=_cparams(1),
        name="ssm",
    )(ut, tt, wz, wyt, cs, d_col)


def _post_kernel(x_ref, attn_ref, yt_ref, mod_ref, wglut_ref, bglu_ref, sn_ref, wout_ref, nf_ref, wr_ref, br_ref,
                 tri_ref, x1_ref, h2_ref, eidx_ref, wts_ref, lrank_ref, r0_ref, cnt_ref, carry_ref, xb_scr):
    @pl.when((pl.program_id(0) == 0) & (pl.program_id(1) == 0))
    def _():
        carry_ref[...] = jnp.zeros_like(carry_ref)

    jj = pl.program_id(1)

    @pl.when(jj == 0)
    def _():
        _to_lane_blocks(xb_scr, x_ref[0])

    nc = attn_ref.shape[2]
    ts = POST_POS * nc
    d = x_ref.shape[2]
    yt = jnp.concatenate(
        [yt_ref[0, :, il * SSM_GROUP:(il + 1) * SSM_GROUP, :].reshape(D_SSM, nc) for il in range(POST_POS)], axis=1)
    g = jax.nn.gelu(yt)
    gate = jax.nn.sigmoid(jnp.dot(wglut_ref[...], g.astype(BF16), preferred_element_type=F32) + bglu_ref[...])
    ssm_t = _rms(g * gate, axis=0) * sn_ref[...]
    mixed = jnp.concatenate([attn_ref[0].reshape(ts, D_ATTN), ssm_t.T.astype(BF16)], axis=-1)
    o = jnp.dot(mixed, wout_ref[...], preferred_element_type=F32)
    x = jnp.concatenate([_pos_rows(xb_scr, POST_POS * jj + il, nc) for il in range(POST_POS)], axis=0)
    x1 = x + mod_ref[0, 2:3, :] * o
    x1_ref[0] = x1.reshape(POST_POS, nc, d)
    h2 = _rms(x1) * nf_ref[...] * (1.0 + mod_ref[0, 4:5, :]) + mod_ref[0, 3:4, :]
    h2_ref[0] = h2.astype(BF16).reshape(POST_POS, nc, d)

    logits = lax.dot_general(wr_ref[...], h2.astype(BF16), (((1,), (1,)), ((), ())),
                             preferred_element_type=F32) + br_ref[...]
    iota_e = lax.broadcasted_iota(I32, (N_EXPERTS, ts), 0).astype(F32)
    l = logits
    idxs, vals = [], []
    for _ in range(TOP_K):
        m = jnp.max(l, axis=0, keepdims=True)
        idx = jnp.min(jnp.where(l == m, iota_e, float(N_EXPERTS)), axis=0, keepdims=True)
        idxs.append(idx)
        vals.append(m)
        l = jnp.where(iota_e == idx, -jnp.inf, l)
    es = [jnp.exp(v - vals[0]) for v in vals]
    tot = es[0] + es[1] + es[2] + es[3]
    member = jnp.zeros((N_EXPERTS, ts), F32)
    for idx in idxs:
        member = member + (iota_e == idx).astype(F32)
    before = jnp.dot(member.astype(BF16), tri_ref[...], preferred_element_type=F32)
    for k in range(TOP_K):
        eidx_ref[k:k + 1, :] = idxs[k].astype(I32)
        wts_ref[k:k + 1, :] = es[k] / tot
        lrank_ref[k:k + 1, :] = jnp.sum(jnp.where(iota_e == idxs[k], before, 0.0), axis=0, keepdims=True).astype(I32)
    r0_ref[0] = carry_ref[...].astype(I32)
    carry = carry_ref[...] + jnp.sum(member, axis=1, keepdims=True)
    carry_ref[...] = carry
    cnt_ref[...] = carry.astype(I32)


def _post(x, attn, yt, mod, w_glu, b_glu, ssm_norm, w_out, norm_ffn, w_router, b_router):
    b, s, d = x.shape
    nc = s // SSM_CHUNK
    ts = POST_POS * nc
    nt = SSM_CHUNK // POST_POS
    t = b * s
    pm = lambda bi, j: (bi, j, 0, 0)
    const = lambda bi, j: (0, 0)
    tok = lambda bi, j: (0, bi * nt + j)
    tri = (lax.broadcasted_iota(I32, (ts, ts), 0) < lax.broadcasted_iota(I32, (ts, ts), 1)).astype(BF16)
    col = lambda a: a.reshape(-1, 1)
    return pl.pallas_call(
        _post_kernel,
        grid=(b, nt),
        in_specs=[pl.BlockSpec((1, s, d), lambda bi, j: (bi, 0, 0)),
                  pl.BlockSpec((1, POST_POS, nc, D_ATTN), pm),
                  pl.BlockSpec((1, N_GROUPS, POST_POS * SSM_GROUP, nc), lambda bi, j: (bi, 0, j, 0)),
                  pl.BlockSpec((1, 6, d), lambda bi, j: (bi, 0, 0)),
                  pl.BlockSpec((D_SSM, D_SSM), const),
                  pl.BlockSpec((D_SSM, 1), const),
                  pl.BlockSpec((D_SSM, 1), const),
                  pl.BlockSpec((d, d), const),
                  pl.BlockSpec((1, d), const),
                  pl.BlockSpec((N_EXPERTS, d), const),
                  pl.BlockSpec((N_EXPERTS, 1), const),
                  pl.BlockSpec((ts, ts), const)],
        out_specs=[pl.BlockSpec((1, POST_POS, nc, d), pm),
                   pl.BlockSpec((1, POST_POS, nc, d), pm),
                   pl.BlockSpec((TOP_K, ts), tok),
                   pl.BlockSpec((TOP_K, ts), tok),
                   pl.BlockSpec((TOP_K, ts), tok),
                   pl.BlockSpec((1, N_EXPERTS, 1), lambda bi, j: (bi * nt + j, 0, 0)),
                   pl.BlockSpec((N_EXPERTS, 1), const)],
        out_shape=[jax.ShapeDtypeStruct((b, SSM_CHUNK, nc, d), F32),
                   jax.ShapeDtypeStruct((b, SSM_CHUNK, nc, d), BF16),
                   jax.ShapeDtypeStruct((TOP_K, t), I32),
                   jax.ShapeDtypeStruct((TOP_K, t), F32),
                   jax.ShapeDtypeStruct((TOP_K, t), I32),
                   jax.ShapeDtypeStruct((b * nt, N_EXPERTS, 1), I32),
                   jax.ShapeDtypeStruct((N_EXPERTS, 1), I32)],
        scratch_shapes=[pltpu.VMEM((N_EXPERTS, 1), F32), pltpu.VMEM((d // LANES, s, LANES), F32)],
        compiler_params=_cparams(2, ROW_VMEM_BYTES),
        name="post",
    )(x, attn, yt, mod, w_glu.T.astype(BF16), col(b_glu), col(ssm_norm), w_out.astype(BF16),
      norm_ffn.reshape(1, -1), w_router.T.astype(BF16), col(b_router), tri)


def _route_kernel(eidx_ref, lrank_ref, r0_ref, cnt_ref, ls_ref, tab_ref, te_ref, nv_ref, nx_ref, pad_ref):
    cnt = cnt_ref[...]
    tiles = (cnt + (RUN - 1 + FFN_ROWS - 1)) // FFN_ROWS
    er = lax.broadcasted_iota(I32, (N_EXPERTS, N_EXPERTS), 0)
    ec = lax.broadcasted_iota(I32, (N_EXPERTS, N_EXPERTS), 1)
    ltri = (ec < er).astype(BF16)

    def excl_cumsum(v):
        vb = jnp.broadcast_to(v.astype(F32), (N_EXPERTS, LANES)).astype(BF16)
        return jnp.dot(ltri, vb, preferred_element_type=F32)[:, 0:1].astype(I32)

    start_t = excl_cumsum(tiles)
    end_t = start_t + tiles
    start = start_t * FFN_ROWS
    pad_ref[...] = start + cnt

    nb = r0_ref.shape[0]
    ts = eidx_ref.shape[1] // nb
    iota_e = lax.broadcasted_iota(I32, (N_EXPERTS, ts), 0)
    iota_t = lax.broadcasted_iota(I32, (N_EXPERTS, TABW), 0)
    chunk = lax.broadcasted_iota(I32, (1, TABW), 1)

    def block(b, carry):
        lanes = pl.ds(pl.multiple_of(b * ts, ts), ts)
        sels = [iota_e == eidx_ref[k:k + 1, lanes] for k in range(TOP_K)]
        member = sels[0].astype(I32) + sels[1].astype(I32) + sels[2].astype(I32) + sels[3].astype(I32)
        nch = (jnp.sum(member, axis=1, keepdims=True) + (RUN - 1)) // RUN
        cb = excl_cumsum(nch)
        end_c = cb + nch
        for k in range(TOP_K):
            first = jnp.sum(jnp.where(sels[k], cb, 0), axis=0, keepdims=True)
            lr = lrank_ref[k:k + 1, lanes]
            ls_ref[k:k + 1, lanes] = (first + lax.shift_right_logical(lr, RUN_SHIFT)) * RUN + (lr & (RUN - 1))
        e_of_c = jnp.sum((chunk >= end_c).astype(I32), axis=0, keepdims=True)
        sel_c = iota_t == e_of_c
        first_c = jnp.sum(jnp.where(sel_c, cb, 0), axis=0, keepdims=True)
        slot0_c = jnp.sum(jnp.where(sel_c, start + r0_ref[b], 0), axis=0, keepdims=True)
        n_chunks = jnp.max(end_c, axis=0, keepdims=True)
        row = jnp.where(chunk < n_chunks, slot0_c + (chunk - first_c) * RUN, -1)
        tab_ref[pl.ds(b, 1), :] = jnp.where(chunk == TABW - 1, n_chunks, row)
        return carry

    lax.fori_loop(0, nb, block, 0)

    nv = jnp.max(end_t, axis=0, keepdims=True)
    width = te_ref.shape[1]
    ti = jnp.minimum(lax.broadcasted_iota(I32, (N_EXPERTS, width), 1), nv - 1)
    te = jnp.minimum(jnp.sum((ti >= end_t).astype(I32), axis=0, keepdims=True), N_EXPERTS - 1)
    te_ref[...] = te
    nv_ref[...] = jnp.broadcast_to(nv, nv_ref.shape)
    ie = lax.broadcasted_iota(I32, (N_EXPERTS, width), 0)
    own_end = jnp.sum(jnp.where(ie == te, end_t, 0), axis=0, keepdims=True)
    nxt = jnp.minimum(jnp.sum((own_end >= end_t).astype(I32), axis=0, keepdims=True), N_EXPERTS - 1)
    nx_ref[...] = jnp.where(own_end < nv, nxt, -1)


def _route(eidx, lrank, r0, cnt, n_tiles):
    t = eidx.shape[1]
    nb = r0.shape[0]
    width = -(-n_tiles // LANES) * LANES
    return pl.pallas_call(
        _route_kernel,
        out_shape=[jax.ShapeDtypeStruct((TOP_K, t), I32),
                   jax.ShapeDtypeStruct((nb, TABW), I32),
                   jax.ShapeDtypeStruct((1, width), I32),
                   jax.ShapeDtypeStruct((1, LANES), I32),
                   jax.ShapeDtypeStruct((1, width), I32),
                   jax.ShapeDtypeStruct((N_EXPERTS, 1), I32)],
        name="route",
    )(eidx, lrank, r0, cnt)


def _for_chunk_pairs(n, fn):
    def body(i, carry):
        fn(2 * i, 0)

        @pl.when(2 * i + 1 < n)
        def _():
            fn(2 * i + 1, 1)
        return carry
    lax.fori_loop(0, lax.shift_right_logical(n + 1, 1), body, 0)


def _local_rows(ts):
    return ts * TOP_K + N_EXPERTS * RUN


def _dispatch_kernel(tab_ref, pad_ref, nvt_ref, h_ref, ls_ref, xs_ref, buf, zbuf, sems, zsem):
    b = pl.program_id(0)
    slot = b % 2
    ts = h_ref.shape[0]
    local = buf.shape[2]

    def chunk_copy(sl, blk, c):
        rows = pl.ds(pl.multiple_of(c * RUN, RUN), RUN)
        return pltpu.make_async_copy(buf.at[sl, :, rows, :], xs_ref.at[:, pl.ds(tab_ref[blk * TABW + c], RUN), :],
                                     sems.at[sl])

    def for_chunks(blk, fn):
        _for_chunk_pairs(tab_ref[blk * TABW + TABW - 1], fn)

    @pl.when(b == 0)
    def _():
        zbuf[...] = jnp.zeros_like(zbuf)
        zrows = zbuf.shape[1]
        zero = lambda row: pltpu.make_async_copy(zbuf, xs_ref.at[:, pl.ds(row, zrows), :], zsem)
        for phase in range(3):
            for e in range(phase, N_EXPERTS, 3):
                zero(pad_ref[e]).start()
            for e in range(phase, N_EXPERTS, 3):
                zero(pad_ref[e]).wait()
        ztile = lambda i: pltpu.make_async_copy(zbuf.at[:, pl.ds(0, FFN_ROWS), :],
                                                xs_ref.at[:, pl.ds((nvt_ref[0] + i) * FFN_ROWS, FFN_ROWS), :], zsem)

        def tail_start(i, carry):
            ztile(i).start()
            return carry

        def tail_wait(i, carry):
            ztile(i).wait()
            return carry
        lax.fori_loop(0, nvt_ref[1] - nvt_ref[0], tail_start, 0)
        lax.fori_loop(0, nvt_ref[1] - nvt_ref[0], tail_wait, 0)

    r = lax.broadcasted_iota(I32, (local, ts), 0)
    hit = (r == ls_ref[0:1, :]) | (r == ls_ref[1:2, :]) | (r == ls_ref[2:3, :]) | (r == ls_ref[3:4, :])
    srt = jnp.dot(hit.astype(BF16), h_ref[...], preferred_element_type=F32)
    packed = _pack_halves(srt)
    for pb in range(PANELS):
        buf[slot, pb] = packed[:, pb * LANES:(pb + 1) * LANES]

    @pl.when(b > 0)
    def _():
        for_chunks(b - 1, lambda c, p: chunk_copy(1 - slot, b - 1, c).wait())

    for_chunks(b, lambda c, p: chunk_copy(slot, b, c).start(priority=p))

    @pl.when(b == pl.num_programs(0) - 1)
    def _():
        for_chunks(b, lambda c, p: chunk_copy(slot, b, c).wait())


def _dispatch(tab, pad, nvt, h2, ls, n_rows):
    t, d = h2.shape
    nb = tab.shape[0] // TABW
    ts = t // nb
    return pl.pallas_call(
        _dispatch_kernel,
        grid_spec=pltpu.PrefetchScalarGridSpec(
            num_scalar_prefetch=3,
            grid=(nb,),
            in_specs=[pl.BlockSpec((ts, d), lambda i, *_: (i, 0)),
                      pl.BlockSpec((TOP_K, ts), lambda i, *_: (0, i))],
            out_specs=pl.BlockSpec(memory_space=pl.ANY),
            scratch_shapes=[pltpu.VMEM((2, PANELS, _local_rows(ts), LANES), U32),
                            pltpu.VMEM((PANELS, FFN_ROWS + RUN, LANES), U32),
                            pltpu.SemaphoreType.DMA((2,)), pltpu.SemaphoreType.DMA],
        ),
        out_shape=jax.ShapeDtypeStruct((PANELS, n_rows, LANES), U32),
        compiler_params=_cparams(1, ROW_VMEM_BYTES),
        name="dispatch",
    )(tab, pad, nvt, h2, ls)


def _ffn_kernel(te_ref, nv_ref, nx_ref, xs_ref, wgu_hbm, bg_ref, bu_ref, wd_hbm, bd_ref, perm_ref, ys_ref,
                wgu_stage, wd_stage, wg_scr, wu_scr, wd_scr, sems):
    i = pl.program_id(0)
    valid = i < nv_ref[0]
    new_expert = (i == 0) | (te_ref[i] != te_ref[jnp.maximum(i - 1, 0)])

    def stage_copies(e):
        return (pltpu.make_async_copy(wgu_hbm.at[e], wgu_stage, sems.at[0]),
                pltpu.make_async_copy(wd_hbm.at[e], wd_stage, sems.at[1]))

    @pl.when(valid & new_expert)
    def _():
        e = te_ref[i]

        @pl.when(i == 0)
        def _():
            for cp in stage_copies(e):
                cp.start()

        for cp in stage_copies(e):
            cp.wait()
        for c in range(2 * D_FF // PERM):
            w = wgu_stage[:, c * PERM:(c + 1) * PERM].astype(BF16)
            pw = jnp.dot(w, perm_ref[...], preferred_element_type=F32).astype(BF16)
            wg_scr[:, c * (PERM // 2):(c + 1) * (PERM // 2)] = pw[:, :PERM // 2]
            wu_scr[:, c * (PERM // 2):(c + 1) * (PERM // 2)] = pw[:, PERM // 2:]
        wd_scr[...] = wd_stage[...].astype(BF16)

        @pl.when(nx_ref[i] >= 0)
        def _():
            for cp in stage_copies(nx_ref[i]):
                cp.start()

    @pl.when(valid)
    def _():
        x_hi, x_lo = _unpack_halves(_load_panels(xs_ref))
        x_hi = x_hi.astype(BF16)
        x_lo = x_lo.astype(BF16)
        gate = (jnp.dot(x_hi, wg_scr[:HALF, :], preferred_element_type=F32)
                + jnp.dot(x_lo, wg_scr[HALF:, :], preferred_element_type=F32) + bg_ref[0])
        up = (jnp.dot(x_hi, wu_scr[:HALF, :], preferred_element_type=F32)
              + jnp.dot(x_lo, wu_scr[HALF:, :], preferred_element_type=F32) + bu_ref[0])
        gate = jnp.minimum(gate, SWIGLU_LIMIT)
        up = jnp.clip(up, -SWIGLU_LIMIT, SWIGLU_LIMIT)
        act = (up + 1.0) * (gate * jax.nn.sigmoid(SWIGLU_ALPHA * gate))
        y = jnp.dot(act.astype(BF16), wd_scr[...], preferred_element_type=F32) + bd_ref[0]
        _store_panels(ys_ref, _pack_halves(y))


def _ffn(te, nv, nx, xs, w_gate_up, bg, bu, w_down, bd, n_tiles):
    d = D_MODEL
    tile = lambda i, te, nv, nx: (0, jnp.minimum(i, nv[0] - 1), 0)
    wsel = lambda i, te, nv, nx: (te[i], 0, 0)
    r = lax.broadcasted_iota(I32, (PERM, PERM), 0)
    c = lax.broadcasted_iota(I32, (PERM, PERM), 1)
    perm = (r == jnp.where(c < PERM // 2, 2 * c, 2 * (c - PERM // 2) + 1)).astype(BF16)
    return pl.pallas_call(
        _ffn_kernel,
        grid_spec=pltpu.PrefetchScalarGridSpec(
            num_scalar_prefetch=3,
            grid=(n_tiles,),
            in_specs=[pl.BlockSpec((PANELS, FFN_ROWS, LANES), tile),
                      pl.BlockSpec(memory_space=pl.ANY),
                      pl.BlockSpec((1, 1, D_FF), wsel),
                      pl.BlockSpec((1, 1, D_FF), wsel),
                      pl.BlockSpec(memory_space=pl.ANY),
                      pl.BlockSpec((1, 1, d), wsel),
                      pl.BlockSpec((PERM, PERM), lambda i, te, nv, nx: (0, 0))],
            out_specs=pl.BlockSpec((PANELS, FFN_ROWS, LANES), tile),
            scratch_shapes=[pltpu.VMEM((d, 2 * D_FF), F32), pltpu.VMEM((D_FF, d), F32),
                            pltpu.VMEM((d, D_FF), BF16), pltpu.VMEM((d, D_FF), BF16), pltpu.VMEM((D_FF, d), BF16),
                            pltpu.SemaphoreType.DMA((2,))],
        ),
        out_shape=jax.ShapeDtypeStruct(xs.shape, U32),
        input_output_aliases={3: 0},
        compiler_params=_cparams(1, FFN_VMEM_BYTES),
        name="ffn",
    )(te, nv, nx, xs, w_gate_up, bg, bu, w_down, bd, perm)


def _combine_kernel(tab_ref, x1_ref, ls_ref, w_ref, mod_ref, ys_ref, o_ref, ybuf, ob_scr, sems):
    jj = pl.program_id(1)
    nt = pl.num_programs(1)
    blk = pl.program_id(0) * nt + jj
    n_blk = pl.num_programs(0) * nt
    slot = blk % 2
    nc = x1_ref.shape[2]
    tt = POST_POS * nc
    d = x1_ref.shape[3]
    local = ybuf.shape[2]

    def chunk_copy(sl, bk, c):
        rows = pl.ds(pl.multiple_of(c * RUN, RUN), RUN)
        return pltpu.make_async_copy(ys_ref.at[:, pl.ds(tab_ref[bk * TABW + c], RUN), :], ybuf.at[sl, :, rows, :],
                                     sems.at[sl])

    def for_chunks(bk, fn):
        _for_chunk_pairs(tab_ref[bk * TABW + TABW - 1], fn)

    @pl.when(blk == 0)
    def _():
        ybuf[...] = jnp.zeros_like(ybuf)
        for_chunks(0, lambda c, p: chunk_copy(0, 0, c).start(priority=p))

    @pl.when(blk + 1 < n_blk)
    def _():
        for_chunks(blk + 1, lambda c, p: chunk_copy(1 - slot, blk + 1, c).start(priority=p))

    for_chunks(blk, lambda c, p: chunk_copy(slot, blk, c).wait())

    r = lax.broadcasted_iota(I32, (tt, local), 1)
    wm = jnp.zeros((tt, local), F32)
    for k in range(TOP_K):
        wm = wm + jnp.where(r == ls_ref[:, k:k + 1], w_ref[:, k:k + 1], 0.0)
    wm = wm.astype(BF16)
    halves = [_unpack_halves(ybuf[slot, pb]) for pb in range(PANELS)]
    y_hi = jnp.concatenate([h.astype(BF16) for h, _ in halves], axis=-1)
    y_lo = jnp.concatenate([l.astype(BF16) for _, l in halves], axis=-1)
    acc_hi = jnp.dot(wm, y_hi, preferred_element_type=F32)
    acc_lo = jnp.dot(wm, y_lo, preferred_element_type=F32)
    g2 = mod_ref[0, 5:6, :]
    x1 = x1_ref[0].reshape(tt, d)
    out = jnp.concatenate([x1[:, :HALF] + g2[:, :HALF] * acc_hi, x1[:, HALF:] + g2[:, HALF:] * acc_lo], axis=-1)
    for il in range(POST_POS):
        rows = pl.ds(POST_POS * jj + il, nc, stride=SSM_CHUNK)
        for kb in range(d // LANES):
            ob_scr[kb, rows, :] = out[il * nc:(il + 1) * nc, kb * LANES:(kb + 1) * LANES]

    @pl.when(jj == pl.num_programs(1) - 1)
    def _():
        for kb in range(d // LANES):
            o_ref[0, :, kb * LANES:(kb + 1) * LANES] = ob_scr[kb]


def _combine(tab, x1, ls_t, wts_t, mod, ys):
    b, _, nc, d = x1.shape
    s = SSM_CHUNK * nc
    tt = POST_POS * nc
    nt = SSM_CHUNK // POST_POS
    return pl.pallas_call(
        _combine_kernel,
        grid_spec=pltpu.PrefetchScalarGridSpec(
            num_scalar_prefetch=1,
            grid=(b, nt),
            in_specs=[pl.BlockSpec((1, POST_POS, nc, d), lambda bi, j, *_: (bi, j, 0, 0)),
                      pl.BlockSpec((tt, TOP_K), lambda bi, j, *_: (bi * nt + j, 0)),
                      pl.BlockSpec((tt, TOP_K), lambda bi, j, *_: (bi * nt + j, 0)),
                      pl.BlockSpec((1, 6, d), lambda bi, j, *_: (bi, 0, 0)),
                      pl.BlockSpec(memory_space=pl.ANY)],
            out_specs=pl.BlockSpec((1, s, d), lambda bi, j, *_: (bi, 0, 0)),
            scratch_shapes=[pltpu.VMEM((2, PANELS, _local_rows(tt), LANES), U32),
                            pltpu.VMEM((d // LANES, s, LANES), F32),
                            pltpu.SemaphoreType.DMA((2,))],
        ),
        out_shape=jax.ShapeDtypeStruct((b, s, d), F32),
        compiler_params=_cparams(2, ROW_VMEM_BYTES),
        name="combine",
    )(tab, x1, ls_t, wts_t, mod, ys)


def kernel(x, c, w_ada, b_ada, norm_mix, w_in, b_in, q_norm, k_norm, sinks, lam_re, lam_im, log_dt, b_re, b_im,
           c_re, c_im, d_skip, w_glu, b_glu, attn_out_norm, ssm_out_norm, w_out, norm_ffn, w_router, b_router,
           w_gate_up, b_gate_up, w_down, b_down):
    b, s, d = x.shape
    t = b * s
    depth = w_ada.shape[0]
    n_tiles = -(-(t * TOP_K + N_EXPERTS * (RUN - 1 + FFN_ROWS - 1)) // FFN_ROWS)
    n_alloc = n_tiles + 2
    for l in range(depth):
        mod = _adaln(c, w_ada[l], b_ada[l]).reshape(b, 6, d)
        q, k, v, ut = _inproj(x, mod, norm_mix[l], w_in[l], b_in[l])
        attn = _attention(q, k, v, sinks[l], q_norm[l], k_norm[l], attn_out_norm[l])
        tt, wz, wyt, cs = _ssm_params(lam_re[l], lam_im[l], log_dt[l], b_re[l], b_im[l], c_re[l], c_im[l])
        yt = _ssm(ut, tt, wz, wyt, cs, d_skip[l])
        x1, h2, eidx, wts, lrank, r0, cnt = _post(x, attn, yt, mod, w_glu[l], b_glu[l], ssm_out_norm[l], w_out[l],
                                                  norm_ffn[l], w_router[l], b_router[l])
        ls, tab, te, nv, nx, pad = _route(eidx, lrank, r0, cnt, n_tiles)
        tab = tab.reshape(-1)
        nvt = jnp.stack([nv[0, 0], jnp.int32(n_alloc)])
        xs = _dispatch(tab, pad.reshape(-1), nvt, h2.reshape(t, d), ls, n_alloc * FFN_ROWS)
        wgu = w_gate_up[l]
        bgu = b_gate_up[l]
        ys = _ffn(te[0, :n_tiles], nv[0, :1], nx[0, :n_tiles], xs, wgu, bgu[:, None, 0::2], bgu[:, None, 1::2],
                  w_down[l], b_down[l][:, None, :], n_tiles)
        x = _combine(tab, x1, ls.T, wts.T, mod, ys)
    return x
```

```python
import functools
import math

import jax
import jax.numpy as jnp
from jax import lax
from jax.experimental import pallas as pl
from jax.experimental.pallas import tpu as pltpu

F32 = jnp.float32
BF16 = jnp.bfloat16
U32 = jnp.uint32
I32 = jnp.int32

D_MODEL = 1024
HEAD_DIM = 64
N_HEADS = 8
N_KV_HEADS = 2
Q_PER_KV = N_HEADS // N_KV_HEADS
D_ATTN = N_HEADS * HEAD_DIM
D_KV = N_KV_HEADS * HEAD_DIM
D_QKV = D_ATTN + 2 * D_KV
WINDOW = 128
BLOCK = 128
D_SSM = D_MODEL - D_ATTN
SSM_GROUP = 16
N_GROUPS = D_SSM // SSM_GROUP
STATE = 64
N_EXPERTS = 32
TOP_K = 4
D_FF = D_MODEL
SWIGLU_LIMIT = 7.0
SWIGLU_ALPHA = 1.702
EPS = 1e-6
NEG_INF = -1e30

LANES = 128
SSM_CHUNK = 16
SSM_ROW = SSM_CHUNK * SSM_GROUP
N_POW = 2 * SSM_CHUNK
HALF = D_MODEL // 2
PANELS = HALF // LANES

POS_PER_STEP = 4
ATTN_ROWS = 512
POST_POS = 2
FFN_ROWS = 256
RUN = 16
RUN_SHIFT = 4
TABW = 128
PERM = 256
FFN_VMEM_BYTES = 40 * 1024 * 1024
ROW_VMEM_BYTES = 48 * 1024 * 1024

HIGHEST = lax.Precision.HIGHEST
_ARB = "arbitrary"


def _cparams(n, vmem=None):
    return pltpu.CompilerParams(dimension_semantics=(_ARB,) * n, vmem_limit_bytes=vmem)


def _rms(x, axis=-1):
    return x * lax.rsqrt(jnp.mean(x * x, axis=axis, keepdims=True) + EPS)


def _pack_halves(y):
    hi = lax.bitcast_convert_type(y[:, :HALF].astype(BF16).astype(F32), U32)
    lo = lax.bitcast_convert_type(y[:, HALF:].astype(BF16).astype(F32), U32)
    return (hi & jnp.uint32(0xFFFF0000)) | (lo >> 16)


def _unpack_halves(w):
    hi = lax.bitcast_convert_type(w & jnp.uint32(0xFFFF0000), F32)
    lo = lax.bitcast_convert_type(w << 16, F32)
    return hi, lo


def _store_panels(ref, packed):
    for pb in range(PANELS):
        ref[pb] = packed[:, pb * LANES:(pb + 1) * LANES]


def _load_panels(ref):
    return jnp.concatenate([ref[pb] for pb in range(PANELS)], axis=-1)


def _prefetch_pos_rows(x4_hbm, buf, sems, n_pos):
    bi = pl.program_id(0)
    j = pl.program_id(1)
    nj = pl.num_programs(1)
    g = bi * nj + j
    slot = g % 2

    def copies(sl, b_, j_):
        return [pltpu.make_async_copy(x4_hbm.at[b_, :, n_pos * j_ + il, :], buf.at[sl, il], sems.at[sl])
                for il in range(n_pos)]

    @pl.when(g == 0)
    def _():
        for cp in copies(0, 0, 0):
            cp.start()

    @pl.when(g + 1 < pl.num_programs(0) * nj)
    def _():
        wrap = j + 1 == nj
        for cp in copies(1 - slot, jnp.where(wrap, bi + 1, bi), jnp.where(wrap, 0, j + 1)):
            cp.start()

    for cp in copies(slot, bi, j):
        cp.wait()
    return slot


def _to_lane_blocks(dst, src):
    for kb in range(dst.shape[0]):
        dst[kb] = src[:, kb * LANES:(kb + 1) * LANES]


def _pos_rows(blocks, i, n_chunks):
    return jnp.concatenate([blocks[kb, pl.ds(i, n_chunks, stride=SSM_CHUNK), :] for kb in range(blocks.shape[0])],
                           axis=-1)


def _adaln_kernel(c_ref, w_ref, b_ref, o_ref):
    c = c_ref[...]
    ca = c * jax.nn.sigmoid(c)
    o_ref[...] = jnp.dot(ca, w_ref[...], preferred_element_type=F32, precision=HIGHEST) + b_ref[...]


def _adaln(c, w_ada, b_ada):
    b, d = c.shape
    n = w_ada.shape[1] // d
    return pl.pallas_call(
        _adaln_kernel,
        grid=(n,),
        in_specs=[pl.BlockSpec((b, d), lambda j: (0, 0)),
                  pl.BlockSpec((d, d), lambda j: (0, j)),
                  pl.BlockSpec((1, d), lambda j: (0, j))],
        out_specs=pl.BlockSpec((b, d), lambda j: (0, j)),
        out_shape=jax.ShapeDtypeStruct((b, n * d), F32),
        compiler_params=_cparams(1),
        name="adaln",
    )(c, w_ada, b_ada.reshape(1, -1))


def _inproj_kernel(x4_hbm, x_ref, mod_ref, g_ref, wqkv_ref, bqkv_ref, wut_ref, but_ref, q_ref, k_ref, v_ref, ut_ref,
                   xp_buf, sems):
    nc = ut_ref.shape[3]
    slot = _prefetch_pos_rows(x4_hbm, xp_buf, sems, POS_PER_STEP)
    gain = g_ref[...]
    scale = 1.0 + mod_ref[0, 1:2, :]
    shift = mod_ref[0, 0:1, :]

    def norm_mod(x):
        return (_rms(x) * gain * scale + shift).astype(BF16)

    proj = jnp.dot(norm_mod(x_ref[0]), wqkv_ref[...], preferred_element_type=F32) + bqkv_ref[...]
    q_ref[0] = proj[:, :D_ATTN].astype(BF16)
    k_ref[0] = proj[:, D_ATTN:D_ATTN + D_KV].astype(BF16)
    v_ref[0] = proj[:, D_ATTN + D_KV:].astype(BF16)

    hs = jnp.concatenate([norm_mod(xp_buf[slot, il]) for il in range(POS_PER_STEP)], axis=0)
    ut = lax.dot_general(wut_ref[...], hs, (((1,), (1,)), ((), ())), preferred_element_type=F32) + but_ref[...]
    for il in range(POS_PER_STEP):
        piece = ut[:, il * nc:(il + 1) * nc].astype(BF16)
        ut_ref[0, :, il * SSM_GROUP:(il + 1) * SSM_GROUP, :] = piece.reshape(N_GROUPS, SSM_GROUP, nc)


def _inproj(x, mod, gain, w_in, b_in):
    b, s, d = x.shape
    nc = s // SSM_CHUNK
    rows = POS_PER_STEP * nc
    row = lambda bi, j: (bi, j, 0)
    const = lambda bi, j: (0, 0)
    w_qkv = w_in[:, :D_QKV].astype(BF16)
    w_ut = w_in[:, D_QKV:].T.astype(BF16)
    return pl.pallas_call(
        _inproj_kernel,
        grid=(b, SSM_CHUNK // POS_PER_STEP),
        in_specs=[pl.BlockSpec(memory_space=pl.ANY),
                  pl.BlockSpec((1, rows, d), row),
                  pl.BlockSpec((1, 6, d), lambda bi, j: (bi, 0, 0)),
                  pl.BlockSpec((1, d), const),
                  pl.BlockSpec((d, D_QKV), const),
                  pl.BlockSpec((1, D_QKV), const),
                  pl.BlockSpec((D_SSM, d), const),
                  pl.BlockSpec((D_SSM, 1), const)],
        out_specs=[pl.BlockSpec((1, rows, D_ATTN), row),
                   pl.BlockSpec((1, rows, D_KV), row),
                   pl.BlockSpec((1, rows, D_KV), row),
                   pl.BlockSpec((1, N_GROUPS, POS_PER_STEP * SSM_GROUP, nc), lambda bi, j: (bi, 0, j, 0))],
        out_shape=[jax.ShapeDtypeStruct((b, s, D_ATTN), BF16),
                   jax.ShapeDtypeStruct((b, s, D_KV), BF16),
                   jax.ShapeDtypeStruct((b, s, D_KV), BF16),
                   jax.ShapeDtypeStruct((b, N_GROUPS, SSM_ROW, nc), BF16)],
        scratch_shapes=[pltpu.VMEM((2, POS_PER_STEP, nc, d), F32), pltpu.SemaphoreType.DMA((2,))],
        compiler_params=_cparams(2),
        name="inproj",
    )(x.reshape(b, nc, SSM_CHUNK, d), x, mod, gain.reshape(1, d), w_qkv, b_in[:D_QKV].reshape(1, D_QKV), w_ut, b_in[D_QKV:].reshape(D_SSM, 1))


def _half_norm(x, low):
    sq = x * x
    s_lo = jnp.sum(jnp.where(low, sq, 0.0), axis=-1, keepdims=True)
    s_hi = jnp.sum(sq, axis=-1, keepdims=True) - s_lo
    inv = 1.0 / HEAD_DIM
    scale = jnp.where(low, lax.rsqrt(s_lo * inv + EPS), lax.rsqrt(s_hi * inv + EPS))
    return x * scale


def _attn_block(first, q, k_prev, k_cur, v_prev, v_cur, sinks_ref, qn, low, upper, rblk):
    no_prev = jnp.where(first, NEG_INF, 0.0)
    out_blocks = []
    for hk in range(N_KV_HEADS):
        qs = []
        for j in range(Q_PER_KV // 2):
            blk = hk * (Q_PER_KV // 2) + j
            qb = _half_norm(q[:, blk * LANES:(blk + 1) * LANES], low) * qn * (1.0 / math.sqrt(HEAD_DIM))
            qs.append(jnp.where(low, qb, 0.0))
            qs.append(jnp.where(low, 0.0, qb))
        qg = jnp.concatenate(qs, axis=0).astype(BF16)
        nt = (((1,), (1,)), ((), ()))
        s_prev = lax.dot_general(qg, k_prev[hk], nt, preferred_element_type=F32)
        s_cur = lax.dot_general(qg, k_cur[hk], nt, preferred_element_type=F32)
        s = jnp.where(upper, s_prev + no_prev, s_cur)
        sink = jnp.zeros((Q_PER_KV * BLOCK, 1), F32)
        for g in range(Q_PER_KV):
            sink = jnp.where(rblk == g, sinks_ref[hk * Q_PER_KV + g], sink)
        m = jnp.maximum(jnp.max(s, axis=-1, keepdims=True), sink)
        p = jnp.exp(s - m)
        den = jnp.sum(p, axis=-1, keepdims=True) + jnp.exp(sink - m)
        o = (jnp.dot(jnp.where(upper, p, 0.0).astype(BF16), v_prev[hk], preferred_element_type=F32)
             + jnp.dot(jnp.where(upper, 0.0, p).astype(BF16), v_cur[hk], preferred_element_type=F32)) / den
        for j in range(Q_PER_KV // 2):
            ev = o[(2 * j) * BLOCK:(2 * j + 1) * BLOCK]
            od = o[(2 * j + 1) * BLOCK:(2 * j + 2) * BLOCK]
            out_blocks.append(jnp.where(low, ev, od))
    return jnp.concatenate(out_blocks, axis=-1)


def _attn_kernel(sinks_ref, q_ref, k_ref, v_ref, qn_ref, kn_ref, on_ref, o_ref, a_scr):
    step = pl.program_id(1)
    nq = ATTN_ROWS // BLOCK
    low = lax.broadcasted_iota(I32, (1, LANES), 1) < HEAD_DIM
    rows = Q_PER_KV * BLOCK
    upper = lax.broadcasted_iota(I32, (rows, BLOCK), 1) > lax.broadcasted_iota(I32, (rows, BLOCK), 0) % BLOCK
    rblk = lax.broadcasted_iota(I32, (rows, 1), 0) // BLOCK

    cur = pl.multiple_of(step * ATTN_ROWS, ATTN_ROWS)
    prev = pl.multiple_of(jnp.maximum(step * nq - 1, 0) * BLOCK, BLOCK)
    kall = jnp.concatenate([k_ref[0, pl.ds(prev, BLOCK), :], k_ref[0, pl.ds(cur, ATTN_ROWS), :]], axis=0).astype(F32)
    vall = jnp.concatenate([v_ref[0, pl.ds(prev, BLOCK), :], v_ref[0, pl.ds(cur, ATTN_ROWS), :]], axis=0).astype(F32)
    kall = _half_norm(kall, low) * kn_ref[...]
    kswap = pltpu.roll(kall, HEAD_DIM, axis=1)
    vswap = pltpu.roll(vall, HEAD_DIM, axis=1)
    k_dup = [jnp.where(low, kall, kswap).astype(BF16), jnp.where(low, kswap, kall).astype(BF16)]
    v_dup = [jnp.where(low, vall, vswap).astype(BF16), jnp.where(low, vswap, vall).astype(BF16)]
    blk = lambda a, i: [a[hk][i * BLOCK:(i + 1) * BLOCK] for hk in range(N_KV_HEADS)]

    for qb in range(nq):
        q = q_ref[0, qb * BLOCK:(qb + 1) * BLOCK, :].astype(F32)
        attn = _attn_block((step == 0) if qb == 0 else False, q, blk(k_dup, qb), blk(k_dup, qb + 1),
                           blk(v_dup, qb), blk(v_dup, qb + 1), sinks_ref, qn_ref[...], low, upper, rblk)
        attn = _rms(attn) * on_ref[...]
        for kb in range(D_ATTN // LANES):
            a_scr[kb, qb * BLOCK:(qb + 1) * BLOCK, :] = attn[:, kb * LANES:(kb + 1) * LANES]
    for i in range(SSM_CHUNK):
        o_ref[0, i] = _pos_rows(a_scr, i, ATTN_ROWS // SSM_CHUNK).astype(BF16)


def _attention(q, k, v, sinks, q_norm, k_norm, out_norm):
    b, s, _ = q.shape
    tile2 = lambda g: jnp.tile(g.reshape(1, HEAD_DIM), (1, 2))
    cps = ATTN_ROWS // SSM_CHUNK
    return pl.pallas_call(
        _attn_kernel,
        grid=(b, s // ATTN_ROWS),
        in_specs=[pl.BlockSpec(memory_space=pltpu.SMEM),
                  pl.BlockSpec((1, ATTN_ROWS, D_ATTN), lambda bi, n: (bi, n, 0)),
                  pl.BlockSpec((1, s, D_KV), lambda bi, n: (bi, 0, 0)),
                  pl.BlockSpec((1, s, D_KV), lambda bi, n: (bi, 0, 0)),
                  pl.BlockSpec((1, LANES), lambda bi, n: (0, 0)),
                  pl.BlockSpec((1, LANES), lambda bi, n: (0, 0)),
                  pl.BlockSpec((1, D_ATTN), lambda bi, n: (0, 0))],
        out_specs=pl.BlockSpec((1, SSM_CHUNK, cps, D_ATTN), lambda bi, n: (bi, 0, n, 0)),
        out_shape=jax.ShapeDtypeStruct((b, SSM_CHUNK, s // SSM_CHUNK, D_ATTN), BF16),
        scratch_shapes=[pltpu.VMEM((D_ATTN // LANES, ATTN_ROWS, LANES), F32)],
        compiler_params=_cparams(2),
        name="attention",
    )(sinks, q, k, v, tile2(q_norm), tile2(k_norm), out_norm.reshape(1, D_ATTN))


def _cmul(ar, ai, br, bi):
    return ar * br - ai * bi, ar * bi + ai * br


def _ssm_param_kernel(lam_ref, bre_ref, bim_ref, cre_ref, cim_ref, tt_ref, wz_ref, wyt_ref, cs_ref):
    f32dot = functools.partial(jnp.dot, preferred_element_type=F32, precision=HIGHEST)
    lr = lam_ref[0, 0:1, :]
    li = lam_ref[0, 1:2, :]
    dt = jnp.exp(lam_ref[0, 2:3, :])
    rho = lr * dt
    th = li * dt
    imag_lane = lax.broadcasted_iota(I32, (1, LANES), 1) >= STATE

    kk = (lax.broadcasted_iota(I32, (N_POW, 1), 0) - (SSM_CHUNK - 1)).astype(F32)
    mag = jnp.exp(rho * kk)
    pw_r = mag * jnp.cos(th * kk)
    pw_i = mag * jnp.sin(th * kk)
    lb_r = pw_r[SSM_CHUNK:SSM_CHUNK + 1]
    lb_i = pw_i[SSM_CHUNK:SSM_CHUNK + 1]
    den = lr * lr + li * li
    coef_r = ((lb_r - 1.0) * lr + lb_i * li) / den
    coef_i = (lb_i * lr - (lb_r - 1.0) * li) / den

    eye = (lax.broadcasted_iota(I32, (SSM_GROUP, SSM_GROUP), 0)
           == lax.broadcasted_iota(I32, (SSM_GROUP, SSM_GROUP), 1)).astype(F32)
    lane_fold = (lax.broadcasted_iota(I32, (STATE, LANES), 1) % STATE
                 == lax.broadcasted_iota(I32, (STATE, LANES), 0)).astype(F32)

    def tile_pos(x):
        return jnp.concatenate([x] * SSM_CHUNK, axis=0)

    def power_rows(k_of_pos):
        idx = [k_of_pos(p) + (SSM_CHUNK - 1) for p in range(SSM_CHUNK)]
        rep = lambda t: jnp.concatenate([jnp.broadcast_to(t[r:r + 1], (SSM_GROUP, LANES)) for r in idx], axis=0)
        return rep(pw_r), rep(pw_i)

    def b_rows(b_ref):
        b2 = jnp.concatenate([b_ref[0], b_ref[0]], axis=0)
        return tile_pos(lax.dot_general(eye, b2, (((1,), (1,)), ((), ())), preferred_element_type=F32,
                                        precision=HIGHEST))

    def c_rows(c_ref):
        return tile_pos(f32dot(c_ref[0], lane_fold))

    bbar_r, bbar_i = _cmul(coef_r, coef_i, b_rows(bre_ref), b_rows(bim_ref))
    c_r = c_rows(cre_ref)
    c_i = c_rows(cim_ref)

    a_r, a_i = _cmul(bbar_r, bbar_i, *power_rows(lambda p: -p))
    a2c = jnp.where(imag_lane, -a_i, a_r)
    m_r, m_i = _cmul(c_r, c_i, *power_rows(lambda p: p))
    bmc = jnp.where(imag_lane, m_i, m_r)
    tt = f32dot(bmc, a2c.T)
    causal = (lax.broadcasted_iota(I32, (SSM_ROW, 1), 0) // SSM_GROUP
              >= lax.broadcasted_iota(I32, (1, SSM_ROW), 1) // SSM_GROUP)
    tt_ref[0] = jnp.where(causal, tt, 0.0).astype(BF16)

    w_r, w_i = _cmul(bbar_r, bbar_i, *power_rows(lambda p: SSM_CHUNK - 1 - p))
    wz_ref[0, :, :LANES] = jnp.where(imag_lane, w_i, w_r).astype(BF16)
    wz_ref[0, :, LANES:] = jnp.where(imag_lane, w_r, w_i).astype(BF16)

    y_r, y_i = _cmul(c_r, c_i, *power_rows(lambda p: p + 1))
    wyt_ref[0] = jnp.where(imag_lane, -y_i, y_r).astype(BF16)

    cs_ref[0, 0:1, :] = pw_r[N_POW - 1:N_POW]
    cs_ref[0, 1:2, :] = jnp.where(imag_lane, pw_i[N_POW - 1:N_POW], -pw_i[N_POW - 1:N_POW])


def _ssm_params(lam_re, lam_im, log_dt, b_re, b_im, c_re, c_im):
    g = lam_re.shape[0]
    lam = jnp.stack([lam_re, lam_im, jnp.broadcast_to(log_dt[:, None], (g, STATE))], axis=1)
    lam = jnp.concatenate([lam, lam], axis=2)
    blk = lambda *shape: pl.BlockSpec((1,) + shape, lambda i: (i, 0, 0))
    return pl.pallas_call(
        _ssm_param_kernel,
        grid=(g,),
        in_specs=[blk(3, LANES), blk(STATE, SSM_GROUP), blk(STATE, SSM_GROUP), blk(SSM_GROUP, STATE),
                  blk(SSM_GROUP, STATE)],
        out_specs=[blk(SSM_ROW, SSM_ROW), blk(SSM_ROW, SSM_ROW), blk(SSM_ROW, LANES), blk(2, LANES)],
        out_shape=[jax.ShapeDtypeStruct((g, SSM_ROW, SSM_ROW), BF16),
                   jax.ShapeDtypeStruct((g, SSM_ROW, SSM_ROW), BF16),
                   jax.ShapeDtypeStruct((g, SSM_ROW, LANES), BF16),
                   jax.ShapeDtypeStruct((g, 2, LANES), F32)],
        compiler_params=_cparams(1),
        name="ssm_params",
    )(lam, b_re, b_im, c_re, c_im)


def _ssm_kernel(ut_ref, tt_ref, wz_ref, wyt_ref, cs_ref, d_ref, yt_ref, z_scr, s_scr):
    batch, _, _, nc = ut_ref.shape
    ut = jnp.concatenate([ut_ref[b, 0] for b in range(batch)], axis=1)
    _to_lane_blocks(z_scr, lax.dot_general(ut, wz_ref[0], (((0,), (0,)), ((), ())), preferred_element_type=F32))
    c1 = cs_ref[0, 0:1, :]
    c2 = cs_ref[0, 1:2, :]

    def step(c, carry):
        s1, s2 = carry
        rows = pl.ds(c, batch, stride=nc)
        s_scr[rows, :] = s1
        n1 = c1 * s1 + c2 * s2 + z_scr[0, rows, :]
        n2 = c1 * s2 - c2 * s1 + z_scr[1, rows, :]
        return n1, n2

    zero = jnp.zeros((batch, LANES), F32)
    lax.fori_loop(0, nc, step, (zero, zero), unroll=8)
    y = jnp.dot(tt_ref[0], ut, preferred_element_type=F32)
    y = y + lax.dot_general(wyt_ref[0], s_scr[...].astype(BF16), (((1,), (1,)), ((), ())),
                            preferred_element_type=F32)
    y = y + d_ref[0] * ut.astype(F32)
    for b in range(batch):
        yt_ref[b, 0] = y[:, b * nc:(b + 1) * nc]


def _ssm(ut, tt, wz, wyt, cs, d_skip):
    b, g, _, nc = ut.shape
    d_col = jnp.tile(d_skip.reshape(g, 1, SSM_GROUP), (1, SSM_CHUNK, 1)).reshape(g, SSM_ROW, 1)
    blk = lambda *shape: pl.BlockSpec((1,) + shape, lambda i: (i, 0, 0))
    act = pl.BlockSpec((b, 1, SSM_ROW, nc), lambda i: (0, i, 0, 0))
    return pl.pallas_call(
        _ssm_kernel,
        grid=(g,),
        in_specs=[act, blk(SSM_ROW, SSM_ROW), blk(SSM_ROW, SSM_ROW), blk(SSM_ROW, LANES), blk(2, LANES),
                  blk(SSM_ROW, 1)],
        out_specs=act,
        out_shape=jax.ShapeDtypeStruct((b, g, SSM_ROW, nc), F32),
        scratch_shapes=[pltpu.VMEM((SSM_ROW // LANES, b * nc, LANES), F32), pltpu.VMEM((b * nc, LANES), F32)],
        compiler_params=_cparams(1),
        name="ssm",
    )(ut, tt, wz, wyt, cs, d_col)


def _post_kernel(x4_hbm, attn_ref, yt_ref, mod_ref, wglut_ref, bglu_ref, sn_ref, wout_ref, nf_ref, wr_ref, br_ref,
                 tri_ref, x1_ref, h2_ref, eidx_ref, wts_ref, lrank_ref, r0_ref, cnt_ref, carry_ref, xp_buf, sems):
    @pl.when((pl.program_id(0) == 0) & (pl.program_id(1) == 0))
    def _():
        carry_ref[...] = jnp.zeros_like(carry_ref)

    slot = _prefetch_pos_rows(x4_hbm, xp_buf, sems, POST_POS)
    nc = attn_ref.shape[2]
    ts = POST_POS * nc
    d = x1_ref.shape[3]
    yt = jnp.concatenate(
        [yt_ref[0, :, il * SSM_GROUP:(il + 1) * SSM_GROUP, :].reshape(D_SSM, nc) for il in range(POST_POS)], axis=1)
    g = jax.nn.gelu(yt)
    gate = jax.nn.sigmoid(jnp.dot(wglut_ref[...], g.astype(BF16), preferred_element_type=F32) + bglu_ref[...])
    ssm_t = _rms(g * gate, axis=0) * sn_ref[...]
    mixed = jnp.concatenate([attn_ref[0].reshape(ts, D_ATTN), ssm_t.T.astype(BF16)], axis=-1)
    o = jnp.dot(mixed, wout_ref[...], preferred_element_type=F32)
    x = jnp.concatenate([xp_buf[slot, il] for il in range(POST_POS)], axis=0)
    x1 = x + mod_ref[0, 2:3, :] * o
    x1_ref[0] = x1.reshape(POST_POS, nc, d)
    h2 = _rms(x1) * nf_ref[...] * (1.0 + mod_ref[0, 4:5, :]) + mod_ref[0, 3:4, :]
    h2_ref[0] = h2.astype(BF16).reshape(POST_POS, nc, d)

    logits = lax.dot_general(wr_ref[...], h2.astype(BF16), (((1,), (1,)), ((), ())),
                             preferred_element_type=F32) + br_ref[...]
    iota_e = lax.broadcasted_iota(I32, (N_EXPERTS, ts), 0).astype(F32)
    l = logits
    idxs, vals = [], []
    for _ in range(TOP_K):
        m = jnp.max(l, axis=0, keepdims=True)
        idx = jnp.min(jnp.where(l == m, iota_e, float(N_EXPERTS)), axis=0, keepdims=True)
        idxs.append(idx)
        vals.append(m)
        l = jnp.where(iota_e == idx, -jnp.inf, l)
    es = [jnp.exp(v - vals[0]) for v in vals]
    tot = es[0] + es[1] + es[2] + es[3]
    member = jnp.zeros((N_EXPERTS, ts), F32)
    for idx in idxs:
        member = member + (iota_e == idx).astype(F32)
    before = jnp.dot(member.astype(BF16), tri_ref[...], preferred_element_type=F32)
    for k in range(TOP_K):
        eidx_ref[k:k + 1, :] = idxs[k].astype(I32)
        wts_ref[k:k + 1, :] = es[k] / tot
        lrank_ref[k:k + 1, :] = jnp.sum(jnp.where(iota_e == idxs[k], before, 0.0), axis=0, keepdims=True).astype(I32)
    r0_ref[0] = carry_ref[...].astype(I32)
    carry = carry_ref[...] + jnp.sum(member, axis=1, keepdims=True)
    carry_ref[...] = carry
    cnt_ref[...] = carry.astype(I32)


def _post(x, attn, yt, mod, w_glu, b_glu, ssm_norm, w_out, norm_ffn, w_router, b_router):
    b, s, d = x.shape
    nc = s // SSM_CHUNK
    ts = POST_POS * nc
    nt = SSM_CHUNK // POST_POS
    t = b * s
    pm = lambda bi, j: (bi, j, 0, 0)
    const = lambda bi, j: (0, 0)
    tok = lambda bi, j: (0, bi * nt + j)
    tri = (lax.broadcasted_iota(I32, (ts, ts), 0) < lax.broadcasted_iota(I32, (ts, ts), 1)).astype(BF16)
    col = lambda a: a.reshape(-1, 1)
    return pl.pallas_call(
        _post_kernel,
        grid=(b, nt),
        in_specs=[pl.BlockSpec(memory_space=pl.ANY),
                  pl.BlockSpec((1, POST_POS, nc, D_ATTN), pm),
                  pl.BlockSpec((1, N_GROUPS, POST_POS * SSM_GROUP, nc), lambda bi, j: (bi, 0, j, 0)),
                  pl.BlockSpec((1, 6, d), lambda bi, j: (bi, 0, 0)),
                  pl.BlockSpec((D_SSM, D_SSM), const),
                  pl.BlockSpec((D_SSM, 1), const),
                  pl.BlockSpec((D_SSM, 1), const),
                  pl.BlockSpec((d, d), const),
                  pl.BlockSpec((1, d), const),
                  pl.BlockSpec((N_EXPERTS, d), const),
                  pl.BlockSpec((N_EXPERTS, 1), const),
                  pl.BlockSpec((ts, ts), const)],
        out_specs=[pl.BlockSpec((1, POST_POS, nc, d), pm),
                   pl.BlockSpec((1, POST_POS, nc, d), pm),
                   pl.BlockSpec((TOP_K, ts), tok),
                   pl.BlockSpec((TOP_K, ts), tok),
                   pl.BlockSpec((TOP_K, ts), tok),
                   pl.BlockSpec((1, N_EXPERTS, 1), lambda bi, j: (bi * nt + j, 0, 0)),
                   pl.BlockSpec((N_EXPERTS, 1), const)],
        out_shape=[jax.ShapeDtypeStruct((b, SSM_CHUNK, nc, d), F32),
                   jax.ShapeDtypeStruct((b, SSM_CHUNK, nc, d), BF16),
                   jax.ShapeDtypeStruct((TOP_K, t), I32),
                   jax.ShapeDtypeStruct((TOP_K, t), F32),
                   jax.ShapeDtypeStruct((TOP_K, t), I32),
                   jax.ShapeDtypeStruct((b * nt, N_EXPERTS, 1), I32),
                   jax.ShapeDtypeStruct((N_EXPERTS, 1), I32)],
        scratch_shapes=[pltpu.VMEM((N_EXPERTS, 1), F32), pltpu.VMEM((2, POST_POS, nc, d), F32),
                        pltpu.SemaphoreType.DMA((2,))],
        compiler_params=_cparams(2),
        name="post",
    )(x.reshape(b, nc, SSM_CHUNK, d), attn, yt, mod, w_glu.T.astype(BF16), col(b_glu), col(ssm_norm), w_out.astype(BF16),
      norm_ffn.reshape(1, -1), w_router.T.astype(BF16), col(b_router), tri)


def _route_kernel(eidx_ref, lrank_ref, r0_ref, cnt_ref, ls_ref, tab_ref, te_ref, nv_ref, nx_ref, pad_ref):
    cnt = cnt_ref[...]
    tiles = (cnt + (RUN - 1 + FFN_ROWS - 1)) // FFN_ROWS
    er = lax.broadcasted_iota(I32, (N_EXPERTS, N_EXPERTS), 0)
    ec = lax.broadcasted_iota(I32, (N_EXPERTS, N_EXPERTS), 1)
    ltri = (ec < er).astype(BF16)

    def excl_cumsum(v):
        vb = jnp.broadcast_to(v.astype(F32), (N_EXPERTS, LANES)).astype(BF16)
        return jnp.dot(ltri, vb, preferred_element_type=F32)[:, 0:1].astype(I32)

    start_t = excl_cumsum(tiles)
    end_t = start_t + tiles
    start = start_t * FFN_ROWS
    pad_ref[...] = start + cnt

    nb = r0_ref.shape[0]
    ts = eidx_ref.shape[1] // nb
    iota_e = lax.broadcasted_iota(I32, (N_EXPERTS, ts), 0)
    iota_t = lax.broadcasted_iota(I32, (N_EXPERTS, TABW), 0)
    chunk = lax.broadcasted_iota(I32, (1, TABW), 1)

    def block(b, carry):
        lanes = pl.ds(pl.multiple_of(b * ts, ts), ts)
        sels = [iota_e == eidx_ref[k:k + 1, lanes] for k in range(TOP_K)]
        member = sels[0].astype(I32) + sels[1].astype(I32) + sels[2].astype(I32) + sels[3].astype(I32)
        nch = (jnp.sum(member, axis=1, keepdims=True) + (RUN - 1)) // RUN
        cb = excl_cumsum(nch)
        end_c = cb + nch
        for k in range(TOP_K):
            first = jnp.sum(jnp.where(sels[k], cb, 0), axis=0, keepdims=True)
            lr = lrank_ref[k:k + 1, lanes]
            ls_ref[k:k + 1, lanes] = (first + lax.shift_right_logical(lr, RUN_SHIFT)) * RUN + (lr & (RUN - 1))
        e_of_c = jnp.sum((chunk >= end_c).astype(I32), axis=0, keepdims=True)
        sel_c = iota_t == e_of_c
        first_c = jnp.sum(jnp.where(sel_c, cb, 0), axis=0, keepdims=True)
        slot0_c = jnp.sum(jnp.where(sel_c, start + r0_ref[b], 0), axis=0, keepdims=True)
        n_chunks = jnp.max(end_c, axis=0, keepdims=True)
        row = jnp.where(chunk < n_chunks, slot0_c + (chunk - first_c) * RUN, -1)
        tab_ref[pl.ds(b, 1), :] = jnp.where(chunk == TABW - 1, n_chunks, row)
        return carry

    lax.fori_loop(0, nb, block, 0)

    nv = jnp.max(end_t, axis=0, keepdims=True)
    width = te_ref.shape[1]
    ti = jnp.minimum(lax.broadcasted_iota(I32, (N_EXPERTS, width), 1), nv - 1)
    te = jnp.minimum(jnp.sum((ti >= end_t).astype(I32), axis=0, keepdims=True), N_EXPERTS - 1)
    te_ref[...] = te
    nv_ref[...] = jnp.broadcast_to(nv, nv_ref.shape)
    ie = lax.broadcasted_iota(I32, (N_EXPERTS, width), 0)
    own_end = jnp.sum(jnp.where(ie == te, end_t, 0), axis=0, keepdims=True)
    nxt = jnp.minimum(jnp.sum((own_end >= end_t).astype(I32), axis=0, keepdims=True), N_EXPERTS - 1)
    nx_ref[...] = jnp.where(own_end < nv, nxt, -1)


def _route(eidx, lrank, r0, cnt, n_tiles):
    t = eidx.shape[1]
    nb = r0.shape[0]
    width = -(-n_tiles // LANES) * LANES
    return pl.pallas_call(
        _route_kernel,
        out_shape=[jax.ShapeDtypeStruct((TOP_K, t), I32),
                   jax.ShapeDtypeStruct((nb, TABW), I32),
                   jax.ShapeDtypeStruct((1, width), I32),
                   jax.ShapeDtypeStruct((1, LANES), I32),
                   jax.ShapeDtypeStruct((1, width), I32),
                   jax.ShapeDtypeStruct((N_EXPERTS, 1), I32)],
        name="route",
    )(eidx, lrank, r0, cnt)


def _for_chunk_pairs(n, fn):
    def body(i, carry):
        fn(2 * i, 0)

        @pl.when(2 * i + 1 < n)
        def _():
            fn(2 * i + 1, 1)
        return carry
    lax.fori_loop(0, lax.shift_right_logical(n + 1, 1), body, 0)


def _local_rows(ts):
    return ts * TOP_K + N_EXPERTS * RUN


def _dispatch_kernel(tab_ref, pad_ref, nvt_ref, h_ref, ls_ref, xs_ref, buf, zbuf, sems, zsem):
    b = pl.program_id(0)
    slot = b % 2
    ts = h_ref.shape[0]
    local = buf.shape[2]

    def chunk_copy(sl, blk, c):
        rows = pl.ds(pl.multiple_of(c * RUN, RUN), RUN)
        return pltpu.make_async_copy(buf.at[sl, :, rows, :], xs_ref.at[:, pl.ds(tab_ref[blk * TABW + c], RUN), :],
                                     sems.at[sl])

    def for_chunks(blk, fn):
        _for_chunk_pairs(tab_ref[blk * TABW + TABW - 1], fn)

    @pl.when(b == 0)
    def _():
        zbuf[...] = jnp.zeros_like(zbuf)
        zrows = zbuf.shape[1]
        zero = lambda row: pltpu.make_async_copy(zbuf, xs_ref.at[:, pl.ds(row, zrows), :], zsem)
        for phase in range(3):
            for e in range(phase, N_EXPERTS, 3):
                zero(pad_ref[e]).start()
            for e in range(phase, N_EXPERTS, 3):
                zero(pad_ref[e]).wait()
        ztile = lambda i: pltpu.make_async_copy(zbuf.at[:, pl.ds(0, FFN_ROWS), :],
                                                xs_ref.at[:, pl.ds((nvt_ref[0] + i) * FFN_ROWS, FFN_ROWS), :], zsem)

        def tail_start(i, carry):
            ztile(i).start()
            return carry

        def tail_wait(i, carry):
            ztile(i).wait()
            return carry
        lax.fori_loop(0, nvt_ref[1] - nvt_ref[0], tail_start, 0)
        lax.fori_loop(0, nvt_ref[1] - nvt_ref[0], tail_wait, 0)

    r = lax.broadcasted_iota(I32, (local, ts), 0)
    hit = (r == ls_ref[0:1, :]) | (r == ls_ref[1:2, :]) | (r == ls_ref[2:3, :]) | (r == ls_ref[3:4, :])
    srt = jnp.dot(hit.astype(BF16), h_ref[...], preferred_element_type=F32)
    packed = _pack_halves(srt)
    for pb in range(PANELS):
        buf[slot, pb] = packed[:, pb * LANES:(pb + 1) * LANES]

    @pl.when(b > 0)
    def _():
        for_chunks(b - 1, lambda c, p: chunk_copy(1 - slot, b - 1, c).wait())

    for_chunks(b, lambda c, p: chunk_copy(slot, b, c).start(priority=p))

    @pl.when(b == pl.num_programs(0) - 1)
    def _():
        for_chunks(b, lambda c, p: chunk_copy(slot, b, c).wait())


def _dispatch(tab, pad, nvt, h2, ls, n_rows):
    t, d = h2.shape
    nb = tab.shape[0] // TABW
    ts = t // nb
    return pl.pallas_call(
        _dispatch_kernel,
        grid_spec=pltpu.PrefetchScalarGridSpec(
            num_scalar_prefetch=3,
            grid=(nb,),
            in_specs=[pl.BlockSpec((ts, d), lambda i, *_: (i, 0)),
                      pl.BlockSpec((TOP_K, ts), lambda i, *_: (0, i))],
            out_specs=pl.BlockSpec(memory_space=pl.ANY),
            scratch_shapes=[pltpu.VMEM((2, PANELS, _local_rows(ts), LANES), U32),
                            pltpu.VMEM((PANELS, FFN_ROWS + RUN, LANES), U32),
                            pltpu.SemaphoreType.DMA((2,)), pltpu.SemaphoreType.DMA],
        ),
        out_shape=jax.ShapeDtypeStruct((PANELS, n_rows, LANES), U32),
        compiler_params=_cparams(1, ROW_VMEM_BYTES),
        name="dispatch",
    )(tab, pad, nvt, h2, ls)


def _ffn_kernel(te_ref, nv_ref, nx_ref, xs_ref, wgu_hbm, bg_ref, bu_ref, wd_hbm, bd_ref, perm_ref, ys_ref,
                wgu_stage, wd_stage, wg_scr, wu_scr, wd_scr, sems):
    i = pl.program_id(0)
    valid = i < nv_ref[0]
    new_expert = (i == 0) | (te_ref[i] != te_ref[jnp.maximum(i - 1, 0)])

    def stage_copies(e):
        return (pltpu.make_async_copy(wgu_hbm.at[e], wgu_stage, sems.at[0]),
                pltpu.make_async_copy(wd_hbm.at[e], wd_stage, sems.at[1]))

    @pl.when(valid & new_expert)
    def _():
        e = te_ref[i]

        @pl.when(i == 0)
        def _():
            for cp in stage_copies(e):
                cp.start()

        for cp in stage_copies(e):
            cp.wait()
        for c in range(2 * D_FF // PERM):
            w = wgu_stage[:, c * PERM:(c + 1) * PERM].astype(BF16)
            pw = jnp.dot(w, perm_ref[...], preferred_element_type=F32).astype(BF16)
            wg_scr[:, c * (PERM // 2):(c + 1) * (PERM // 2)] = pw[:, :PERM // 2]
            wu_scr[:, c * (PERM // 2):(c + 1) * (PERM // 2)] = pw[:, PERM // 2:]
        wd_scr[...] = wd_stage[...].astype(BF16)

        @pl.when(nx_ref[i] >= 0)
        def _():
            for cp in stage_copies(nx_ref[i]):
                cp.start()

    @pl.when(valid)
    def _():
        x_hi, x_lo = _unpack_halves(_load_panels(xs_ref))
        x_hi = x_hi.astype(BF16)
        x_lo = x_lo.astype(BF16)
        gate = (jnp.dot(x_hi, wg_scr[:HALF, :], preferred_element_type=F32)
                + jnp.dot(x_lo, wg_scr[HALF:, :], preferred_element_type=F32) + bg_ref[0])
        up = (jnp.dot(x_hi, wu_scr[:HALF, :], preferred_element_type=F32)
              + jnp.dot(x_lo, wu_scr[HALF:, :], preferred_element_type=F32) + bu_ref[0])
        gate = jnp.minimum(gate, SWIGLU_LIMIT)
        up = jnp.clip(up, -SWIGLU_LIMIT, SWIGLU_LIMIT)
        act = (up + 1.0) * (gate * jax.nn.sigmoid(SWIGLU_ALPHA * gate))
        y = jnp.dot(act.astype(BF16), wd_scr[...], preferred_element_type=F32) + bd_ref[0]
        _store_panels(ys_ref, _pack_halves(y))


def _ffn(te, nv, nx, xs, w_gate_up, bg, bu, w_down, bd, n_tiles):
    d = D_MODEL
    tile = lambda i, te, nv, nx: (0, jnp.minimum(i, nv[0] - 1), 0)
    wsel = lambda i, te, nv, nx: (te[i], 0, 0)
    r = lax.broadcasted_iota(I32, (PERM, PERM), 0)
    c = lax.broadcasted_iota(I32, (PERM, PERM), 1)
    perm = (r == jnp.where(c < PERM // 2, 2 * c, 2 * (c - PERM // 2) + 1)).astype(BF16)
    return pl.pallas_call(
        _ffn_kernel,
        grid_spec=pltpu.PrefetchScalarGridSpec(
            num_scalar_prefetch=3,
            grid=(n_tiles,),
            in_specs=[pl.BlockSpec((PANELS, FFN_ROWS, LANES), tile),
                      pl.BlockSpec(memory_space=pl.ANY),
                      pl.BlockSpec((1, 1, D_FF), wsel),
                      pl.BlockSpec((1, 1, D_FF), wsel),
                      pl.BlockSpec(memory_space=pl.ANY),
                      pl.BlockSpec((1, 1, d), wsel),
                      pl.BlockSpec((PERM, PERM), lambda i, te, nv, nx: (0, 0))],
            out_specs=pl.BlockSpec((PANELS, FFN_ROWS, LANES), tile),
            scratch_shapes=[pltpu.VMEM((d, 2 * D_FF), F32), pltpu.VMEM((D_FF, d), F32),
                            pltpu.VMEM((d, D_FF), BF16), pltpu.VMEM((d, D_FF), BF16), pltpu.VMEM((D_FF, d), BF16),
                            pltpu.SemaphoreType.DMA((2,))],
        ),
        out_shape=jax.ShapeDtypeStruct(xs.shape, U32),
        input_output_aliases={3: 0},
        compiler_params=_cparams(1, FFN_VMEM_BYTES),
        name="ffn",
    )(te, nv, nx, xs, w_gate_up, bg, bu, w_down, bd, perm)


def _combine_kernel(tab_ref, x1_ref, ls_ref, w_ref, mod_ref, ys_ref, o4_hbm, ybuf, ob_buf, sems, osems, *, n_blk):
    jj = pl.program_id(1)
    blk = pl.program_id(0) * pl.num_programs(1) + jj
    slot = blk % 2
    nc = x1_ref.shape[2]
    tt = POST_POS * nc
    d = x1_ref.shape[3]
    local = ybuf.shape[2]

    def chunk_copy(sl, bk, c):
        rows = pl.ds(pl.multiple_of(c * RUN, RUN), RUN)
        return pltpu.make_async_copy(ys_ref.at[:, pl.ds(tab_ref[bk * TABW + c], RUN), :], ybuf.at[sl, :, rows, :],
                                     sems.at[sl])

    def for_chunks(bk, fn):
        _for_chunk_pairs(tab_ref[bk * TABW + TABW - 1], fn)

    @pl.when(blk == 0)
    def _():
        ybuf[...] = jnp.zeros_like(ybuf)
        for_chunks(0, lambda c, p: chunk_copy(0, 0, c).start(priority=p))

    @pl.when(blk + 1 < n_blk)
    def _():
        for_chunks(blk + 1, lambda c, p: chunk_copy(1 - slot, blk + 1, c).start(priority=p))

    for_chunks(blk, lambda c, p: chunk_copy(slot, blk, c).wait())

    r = lax.broadcasted_iota(I32, (tt, local), 1)
    wm = jnp.zeros((tt, local), F32)
    for k in range(TOP_K):
        wm = wm + jnp.where(r == ls_ref[:, k:k + 1], w_ref[:, k:k + 1], 0.0)
    wm = wm.astype(BF16)
    halves = [_unpack_halves(ybuf[slot, pb]) for pb in range(PANELS)]
    y_hi = jnp.concatenate([h.astype(BF16) for h, _ in halves], axis=-1)
    y_lo = jnp.concatenate([l.astype(BF16) for _, l in halves], axis=-1)
    acc_hi = jnp.dot(wm, y_hi, preferred_element_type=F32)
    acc_lo = jnp.dot(wm, y_lo, preferred_element_type=F32)
    g2 = mod_ref[0, 5:6, :]
    x1 = x1_ref[0].reshape(tt, d)
    out = jnp.concatenate([x1[:, :HALF] + g2[:, :HALF] * acc_hi, x1[:, HALF:] + g2[:, HALF:] * acc_lo], axis=-1)

    def out_copies(sl, b_, j_):
        return [pltpu.make_async_copy(ob_buf.at[sl, il], o4_hbm.at[b_, :, POST_POS * j_ + il, :], osems.at[sl])
                for il in range(POST_POS)]

    @pl.when(blk >= 2)
    def _():
        for cp in out_copies(slot, 0, 0):
            cp.wait()

    for il in range(POST_POS):
        ob_buf[slot, il] = out[il * nc:(il + 1) * nc]
    for cp in out_copies(slot, pl.program_id(0), jj):
        cp.start()

    @pl.when(blk == n_blk - 1)
    def _():
        for cp in out_copies(slot, 0, 0):
            cp.wait()
        if n_blk > 1:
            for cp in out_copies(1 - slot, 0, 0):
                cp.wait()


def _combine(tab, x1, ls_t, wts_t, mod, ys):
    b, _, nc, d = x1.shape
    s = SSM_CHUNK * nc
    tt = POST_POS * nc
    nt = SSM_CHUNK // POST_POS
    o4 = pl.pallas_call(
        functools.partial(_combine_kernel, n_blk=b * nt),
        grid_spec=pltpu.PrefetchScalarGridSpec(
            num_scalar_prefetch=1,
            grid=(b, nt),
            in_specs=[pl.BlockSpec((1, POST_POS, nc, d), lambda bi, j, *_: (bi, j, 0, 0)),
                      pl.BlockSpec((tt, TOP_K), lambda bi, j, *_: (bi * nt + j, 0)),
                      pl.BlockSpec((tt, TOP_K), lambda bi, j, *_: (bi * nt + j, 0)),
                      pl.BlockSpec((1, 6, d), lambda bi, j, *_: (bi, 0, 0)),
                      pl.BlockSpec(memory_space=pl.ANY)],
            out_specs=pl.BlockSpec(memory_space=pl.ANY),
            scratch_shapes=[pltpu.VMEM((2, PANELS, _local_rows(tt), LANES), U32),
                            pltpu.VMEM((2, POST_POS, nc, d), F32),
                            pltpu.SemaphoreType.DMA((2,)), pltpu.SemaphoreType.DMA((2,))],
        ),
        out_shape=jax.ShapeDtypeStruct((b, nc, SSM_CHUNK, d), F32),
        compiler_params=_cparams(2),
        name="combine",
    )(tab, x1, ls_t, wts_t, mod, ys)
    return o4.reshape(b, s, d)


def kernel(x, c, w_ada, b_ada, norm_mix, w_in, b_in, q_norm, k_norm, sinks, lam_re, lam_im, log_dt, b_re, b_im,
           c_re, c_im, d_skip, w_glu, b_glu, attn_out_norm, ssm_out_norm, w_out, norm_ffn, w_router, b_router,
           w_gate_up, b_gate_up, w_down, b_down):
    b, s, d = x.shape
    t = b * s
    depth = w_ada.shape[0]
    n_tiles = -(-(t * TOP_K + N_EXPERTS * (RUN - 1 + FFN_ROWS - 1)) // FFN_ROWS)
    n_alloc = n_tiles + 2
    for l in range(depth):
        mod = _adaln(c, w_ada[l], b_ada[l]).reshape(b, 6, d)
        q, k, v, ut = _inproj(x, mod, norm_mix[l], w_in[l], b_in[l])
        attn = _attention(q, k, v, sinks[l], q_norm[l], k_norm[l], attn_out_norm[l])
        tt, wz, wyt, cs = _ssm_params(lam_re[l], lam_im[l], log_dt[l], b_re[l], b_im[l], c_re[l], c_im[l])
        yt = _ssm(ut, tt, wz, wyt, cs, d_skip[l])
        x1, h2, eidx, wts, lrank, r0, cnt = _post(x, attn, yt, mod, w_glu[l], b_glu[l], ssm_out_norm[l], w_out[l],
                                                  norm_ffn[l], w_router[l], b_router[l])
        ls, tab, te, nv, nx, pad = _route(eidx, lrank, r0, cnt, n_tiles)
        tab = tab.reshape(-1)
        nvt = jnp.stack([nv[0, 0], jnp.int32(n_alloc)])
        xs = _dispatch(tab, pad.reshape(-1), nvt, h2.reshape(t, d), ls, n_alloc * FFN_ROWS)
        wgu = w_gate_up[l]
        bgu = b_gate_up[l]
        ys = _ffn(te[0, :n_tiles], nv[0, :1], nx[0, :n_tiles], xs, wgu, bgu[:, None, 0::2], bgu[:, None, 1::2],
                  w_down[l], b_down[l][:, None, :], n_tiles)
        x = _combine(tab, x1, ls.T, wts.T, mod, ys)
    return x
```

```python
import functools
import math

import jax
import jax.numpy as jnp
from jax import lax
from jax.experimental import pallas as pl
from jax.experimental.pallas import tpu as pltpu

F32 = jnp.float32
BF16 = jnp.bfloat16
U32 = jnp.uint32
I32 = jnp.int32

D_MODEL = 1024
HEAD_DIM = 64
N_HEADS = 8
N_KV_HEADS = 2
Q_PER_KV = N_HEADS // N_KV_HEADS
D_ATTN = N_HEADS * HEAD_DIM
D_KV = N_KV_HEADS * HEAD_DIM
D_QKV = D_ATTN + 2 * D_KV
WINDOW = 128
BLOCK = 128
D_SSM = D_MODEL - D_ATTN
SSM_GROUP = 16
N_GROUPS = D_SSM // SSM_GROUP
STATE = 64
N_EXPERTS = 32
TOP_K = 4
D_FF = D_MODEL
SWIGLU_LIMIT = 7.0
SWIGLU_ALPHA = 1.702
EPS = 1e-6
NEG_INF = -1e30

LANES = 128
SSM_CHUNK = 16
SSM_ROW = SSM_CHUNK * SSM_GROUP
N_POW = 2 * SSM_CHUNK
HALF = D_MODEL // 2
PANELS = HALF // LANES

POS_PER_STEP = 4
ATTN_ROWS = 512
POST_POS = 2
FFN_ROWS = 256
RUN = 16
RUN_SHIFT = 4
TABW = 128
PERM = 256
FFN_VMEM_BYTES = 40 * 1024 * 1024
ROW_VMEM_BYTES = 48 * 1024 * 1024

HIGHEST = lax.Precision.HIGHEST
_ARB = "arbitrary"


def _cparams(n, vmem=None):
    return pltpu.CompilerParams(dimension_semantics=(_ARB,) * n, vmem_limit_bytes=vmem)


def _rms(x, axis=-1):
    return x * lax.rsqrt(jnp.mean(x * x, axis=axis, keepdims=True) + EPS)


def _pack_halves(y):
    hi = lax.bitcast_convert_type(y[:, :HALF].astype(BF16).astype(F32), U32)
    lo = lax.bitcast_convert_type(y[:, HALF:].astype(BF16).astype(F32), U32)
    return (hi & jnp.uint32(0xFFFF0000)) | (lo >> 16)


def _pack_exact_halves(y):
    return lax.bitcast_convert_type(y[:, :HALF], U32) | (lax.bitcast_convert_type(y[:, HALF:], U32) >> 16)


def _unpack_halves(w):
    hi = lax.bitcast_convert_type(w & jnp.uint32(0xFFFF0000), F32)
    lo = lax.bitcast_convert_type(w << 16, F32)
    return hi, lo


def _store_panels(ref, packed):
    for pb in range(PANELS):
        ref[pb] = packed[:, pb * LANES:(pb + 1) * LANES]


def _load_panels(ref):
    return jnp.concatenate([ref[pb] for pb in range(PANELS)], axis=-1)


def _prefetch_pos_rows(x4_hbm, buf, sems, n_pos):
    bi = pl.program_id(0)
    j = pl.program_id(1)
    nj = pl.num_programs(1)
    g = bi * nj + j
    slot = g % 2

    def copies(sl, b_, j_):
        return [pltpu.make_async_copy(x4_hbm.at[b_, :, n_pos * j_ + il, :], buf.at[sl, il], sems.at[sl])
                for il in range(n_pos)]

    @pl.when(g == 0)
    def _():
        for cp in copies(0, 0, 0):
            cp.start()

    @pl.when(g + 1 < pl.num_programs(0) * nj)
    def _():
        wrap = j + 1 == nj
        for cp in copies(1 - slot, jnp.where(wrap, bi + 1, bi), jnp.where(wrap, 0, j + 1)):
            cp.start()

    for cp in copies(slot, bi, j):
        cp.wait()
    return slot


def _to_lane_blocks(dst, src):
    for kb in range(dst.shape[0]):
        dst[kb] = src[:, kb * LANES:(kb + 1) * LANES]


def _adaln_kernel(c_ref, w_ref, b_ref, o_ref):
    c = c_ref[...]
    ca = c * jax.nn.sigmoid(c)
    o_ref[...] = jnp.dot(ca, w_ref[...], preferred_element_type=F32, precision=HIGHEST) + b_ref[...]


def _adaln(c, w_ada, b_ada):
    b, d = c.shape
    n = w_ada.shape[1] // d
    return pl.pallas_call(
        _adaln_kernel,
        grid=(n,),
        in_specs=[pl.BlockSpec((b, d), lambda j: (0, 0)),
                  pl.BlockSpec((d, d), lambda j: (0, j)),
                  pl.BlockSpec((1, d), lambda j: (0, j))],
        out_specs=pl.BlockSpec((b, d), lambda j: (0, j)),
        out_shape=jax.ShapeDtypeStruct((b, n * d), F32),
        compiler_params=_cparams(1),
        name="adaln",
    )(c, w_ada, b_ada.reshape(1, -1))


def _inproj_kernel(x4_hbm, x_ref, mod_ref, g_ref, wqkv_ref, bqkv_ref, wut_ref, but_ref, q_ref, k_ref, v_ref, ut_ref,
                   xp_buf, sems):
    nc = ut_ref.shape[3]
    slot = _prefetch_pos_rows(x4_hbm, xp_buf, sems, POS_PER_STEP)
    gain = g_ref[...]
    scale = 1.0 + mod_ref[0, 1:2, :]
    shift = mod_ref[0, 0:1, :]

    def norm_mod(x):
        return (_rms(x) * gain * scale + shift).astype(BF16)

    proj = jnp.dot(norm_mod(x_ref[0]), wqkv_ref[...], preferred_element_type=F32) + bqkv_ref[...]
    q_ref[0] = proj[:, :D_ATTN].astype(BF16)
    k_ref[0] = proj[:, D_ATTN:D_ATTN + D_KV].astype(BF16)
    v_ref[0] = proj[:, D_ATTN + D_KV:].astype(BF16)

    hs = jnp.concatenate([norm_mod(xp_buf[slot, il]) for il in range(POS_PER_STEP)], axis=0)
    ut = lax.dot_general(wut_ref[...], hs, (((1,), (1,)), ((), ())), preferred_element_type=F32) + but_ref[...]
    for il in range(POS_PER_STEP):
        piece = ut[:, il * nc:(il + 1) * nc].astype(BF16)
        ut_ref[0, :, il * SSM_GROUP:(il + 1) * SSM_GROUP, :] = piece.reshape(N_GROUPS, SSM_GROUP, nc)


def _inproj(x, mod, gain, w_in, b_in):
    b, s, d = x.shape
    nc = s // SSM_CHUNK
    rows = POS_PER_STEP * nc
    row = lambda bi, j: (bi, j, 0)
    const = lambda bi, j: (0, 0)
    w_qkv = w_in[:, :D_QKV].astype(BF16)
    w_ut = w_in[:, D_QKV:].T.astype(BF16)
    return pl.pallas_call(
        _inproj_kernel,
        grid=(b, SSM_CHUNK // POS_PER_STEP),
        in_specs=[pl.BlockSpec(memory_space=pl.ANY),
                  pl.BlockSpec((1, rows, d), row),
                  pl.BlockSpec((1, 6, d), lambda bi, j: (bi, 0, 0)),
                  pl.BlockSpec((1, d), const),
                  pl.BlockSpec((d, D_QKV), const),
                  pl.BlockSpec((1, D_QKV), const),
                  pl.BlockSpec((D_SSM, d), const),
                  pl.BlockSpec((D_SSM, 1), const)],
        out_specs=[pl.BlockSpec((1, rows, D_ATTN), row),
                   pl.BlockSpec((1, rows, D_KV), row),
                   pl.BlockSpec((1, rows, D_KV), row),
                   pl.BlockSpec((1, N_GROUPS, POS_PER_STEP * SSM_GROUP, nc), lambda bi, j: (bi, 0, j, 0))],
        out_shape=[jax.ShapeDtypeStruct((b, s, D_ATTN), BF16),
                   jax.ShapeDtypeStruct((b, s, D_KV), BF16),
                   jax.ShapeDtypeStruct((b, s, D_KV), BF16),
                   jax.ShapeDtypeStruct((b, N_GROUPS, SSM_ROW, nc), BF16)],
        scratch_shapes=[pltpu.VMEM((2, POS_PER_STEP, nc, d), F32), pltpu.SemaphoreType.DMA((2,))],
        compiler_params=_cparams(2),
        name="inproj",
    )(x.reshape(b, nc, SSM_CHUNK, d), x, mod, gain.reshape(1, d), w_qkv, b_in[:D_QKV].reshape(1, D_QKV), w_ut, b_in[D_QKV:].reshape(D_SSM, 1))


def _half_norm(x, low):
    sq = x * x
    s_lo = jnp.sum(jnp.where(low, sq, 0.0), axis=-1, keepdims=True)
    s_hi = jnp.sum(sq, axis=-1, keepdims=True) - s_lo
    inv = 1.0 / HEAD_DIM
    scale = jnp.where(low, lax.rsqrt(s_lo * inv + EPS), lax.rsqrt(s_hi * inv + EPS))
    return x * scale


def _attn_block(first, q, k_prev, k_cur, v_prev, v_cur, sinks_ref, qn, low, upper, rblk):
    no_prev = jnp.where(first, NEG_INF, 0.0)
    out_blocks = []
    for hk in range(N_KV_HEADS):
        qs = []
        for j in range(Q_PER_KV // 2):
            blk = hk * (Q_PER_KV // 2) + j
            qb = _half_norm(q[:, blk * LANES:(blk + 1) * LANES], low) * qn * (1.0 / math.sqrt(HEAD_DIM))
            qs.append(jnp.where(low, qb, 0.0))
            qs.append(jnp.where(low, 0.0, qb))
        qg = jnp.concatenate(qs, axis=0).astype(BF16)
        nt = (((1,), (1,)), ((), ()))
        s_prev = lax.dot_general(qg, k_prev[hk], nt, preferred_element_type=F32)
        s_cur = lax.dot_general(qg, k_cur[hk], nt, preferred_element_type=F32)
        s = jnp.where(upper, s_prev + no_prev, s_cur)
        sink = jnp.zeros((Q_PER_KV * BLOCK, 1), F32)
        for g in range(Q_PER_KV):
            sink = jnp.where(rblk == g, sinks_ref[hk * Q_PER_KV + g], sink)
        m = jnp.maximum(jnp.max(s, axis=-1, keepdims=True), sink)
        p = jnp.exp(s - m)
        den = jnp.sum(p, axis=-1, keepdims=True) + jnp.exp(sink - m)
        o = (jnp.dot(jnp.where(upper, p, 0.0).astype(BF16), v_prev[hk], preferred_element_type=F32)
             + jnp.dot(jnp.where(upper, 0.0, p).astype(BF16), v_cur[hk], preferred_element_type=F32)) / den
        for j in range(Q_PER_KV // 2):
            ev = o[(2 * j) * BLOCK:(2 * j + 1) * BLOCK]
            od = o[(2 * j + 1) * BLOCK:(2 * j + 2) * BLOCK]
            out_blocks.append(jnp.where(low, ev, od))
    return jnp.concatenate(out_blocks, axis=-1)


def _attn_kernel(sinks_ref, q_ref, k_ref, v_ref, qn_ref, kn_ref, on_ref, o_hbm, a_buf, sems, *, n_steps):
    step = pl.program_id(1)
    g = pl.program_id(0) * pl.num_programs(1) + step
    slot = g % 2
    cps = ATTN_ROWS // SSM_CHUNK
    nq = ATTN_ROWS // BLOCK

    def out_copies(sl, b_, s_):
        return [pltpu.make_async_copy(a_buf.at[sl, :, i, :], o_hbm.at[b_, i, pl.ds(s_ * cps, cps), :], sems.at[sl])
                for i in range(SSM_CHUNK)]

    @pl.when(g >= 2)
    def _():
        for cp in out_copies(slot, 0, 0):
            cp.wait()

    low = lax.broadcasted_iota(I32, (1, LANES), 1) < HEAD_DIM
    rows = Q_PER_KV * BLOCK
    upper = lax.broadcasted_iota(I32, (rows, BLOCK), 1) > lax.broadcasted_iota(I32, (rows, BLOCK), 0) % BLOCK
    rblk = lax.broadcasted_iota(I32, (rows, 1), 0) // BLOCK

    cur = pl.multiple_of(step * ATTN_ROWS, ATTN_ROWS)
    prev = pl.multiple_of(jnp.maximum(step * nq - 1, 0) * BLOCK, BLOCK)
    kall = jnp.concatenate([k_ref[0, pl.ds(prev, BLOCK), :], k_ref[0, pl.ds(cur, ATTN_ROWS), :]], axis=0).astype(F32)
    vall = jnp.concatenate([v_ref[0, pl.ds(prev, BLOCK), :], v_ref[0, pl.ds(cur, ATTN_ROWS), :]], axis=0).astype(F32)
    kall = _half_norm(kall, low) * kn_ref[...]
    kswap = pltpu.roll(kall, HEAD_DIM, axis=1)
    vswap = pltpu.roll(vall, HEAD_DIM, axis=1)
    k_dup = [jnp.where(low, kall, kswap).astype(BF16), jnp.where(low, kswap, kall).astype(BF16)]
    v_dup = [jnp.where(low, vall, vswap).astype(BF16), jnp.where(low, vswap, vall).astype(BF16)]
    blk = lambda a, i: [a[hk][i * BLOCK:(i + 1) * BLOCK] for hk in range(N_KV_HEADS)]

    for qb in range(nq):
        q = q_ref[0, qb * BLOCK:(qb + 1) * BLOCK, :].astype(F32)
        attn = _attn_block((step == 0) if qb == 0 else False, q, blk(k_dup, qb), blk(k_dup, qb + 1),
                           blk(v_dup, qb), blk(v_dup, qb + 1), sinks_ref, qn_ref[...], low, upper, rblk)
        attn = _rms(attn) * on_ref[...]
        cpb = BLOCK // SSM_CHUNK
        a_buf[slot, qb * cpb:(qb + 1) * cpb] = attn.reshape(cpb, SSM_CHUNK, D_ATTN)

    for cp in out_copies(slot, pl.program_id(0), step):
        cp.start()

    @pl.when(g == n_steps - 1)
    def _():
        for cp in out_copies(slot, 0, 0):
            cp.wait()
        if n_steps > 1:
            for cp in out_copies(1 - slot, 0, 0):
                cp.wait()


def _attention(q, k, v, sinks, q_norm, k_norm, out_norm):
    b, s, _ = q.shape
    tile2 = lambda g: jnp.tile(g.reshape(1, HEAD_DIM), (1, 2))
    cps = ATTN_ROWS // SSM_CHUNK
    return pl.pallas_call(
        functools.partial(_attn_kernel, n_steps=b * (s // ATTN_ROWS)),
        grid=(b, s // ATTN_ROWS),
        in_specs=[pl.BlockSpec(memory_space=pltpu.SMEM),
                  pl.BlockSpec((1, ATTN_ROWS, D_ATTN), lambda bi, n: (bi, n, 0)),
                  pl.BlockSpec((1, s, D_KV), lambda bi, n: (bi, 0, 0)),
                  pl.BlockSpec((1, s, D_KV), lambda bi, n: (bi, 0, 0)),
                  pl.BlockSpec((1, LANES), lambda bi, n: (0, 0)),
                  pl.BlockSpec((1, LANES), lambda bi, n: (0, 0)),
                  pl.BlockSpec((1, D_ATTN), lambda bi, n: (0, 0))],
        out_specs=pl.BlockSpec(memory_space=pl.ANY),
        out_shape=jax.ShapeDtypeStruct((b, SSM_CHUNK, s // SSM_CHUNK, D_ATTN), F32),
        scratch_shapes=[pltpu.VMEM((2, cps, SSM_CHUNK, D_ATTN), F32), pltpu.SemaphoreType.DMA((2,))],
        compiler_params=_cparams(2),
        name="attention",
    )(sinks, q, k, v, tile2(q_norm), tile2(k_norm), out_norm.reshape(1, D_ATTN))


def _cmul(ar, ai, br, bi):
    return ar * br - ai * bi, ar * bi + ai * br


def _ssm_param_kernel(lam_ref, bre_ref, bim_ref, cre_ref, cim_ref, tt_ref, wz_ref, wyt_ref, cs_ref):
    f32dot = functools.partial(jnp.dot, preferred_element_type=F32, precision=HIGHEST)
    lr = lam_ref[0, 0:1, :]
    li = lam_ref[0, 1:2, :]
    dt = jnp.exp(lam_ref[0, 2:3, :])
    rho = lr * dt
    th = li * dt
    imag_lane = lax.broadcasted_iota(I32, (1, LANES), 1) >= STATE

    kk = (lax.broadcasted_iota(I32, (N_POW, 1), 0) - (SSM_CHUNK - 1)).astype(F32)
    mag = jnp.exp(rho * kk)
    pw_r = mag * jnp.cos(th * kk)
    pw_i = mag * jnp.sin(th * kk)
    lb_r = pw_r[SSM_CHUNK:SSM_CHUNK + 1]
    lb_i = pw_i[SSM_CHUNK:SSM_CHUNK + 1]
    den = lr * lr + li * li
    coef_r = ((lb_r - 1.0) * lr + lb_i * li) / den
    coef_i = (lb_i * lr - (lb_r - 1.0) * li) / den

    eye = (lax.broadcasted_iota(I32, (SSM_GROUP, SSM_GROUP), 0)
           == lax.broadcasted_iota(I32, (SSM_GROUP, SSM_GROUP), 1)).astype(F32)
    lane_fold = (lax.broadcasted_iota(I32, (STATE, LANES), 1) % STATE
                 == lax.broadcasted_iota(I32, (STATE, LANES), 0)).astype(F32)

    def tile_pos(x):
        return jnp.concatenate([x] * SSM_CHUNK, axis=0)

    def power_rows(k_of_pos):
        idx = [k_of_pos(p) + (SSM_CHUNK - 1) for p in range(SSM_CHUNK)]
        rep = lambda t: jnp.concatenate([jnp.broadcast_to(t[r:r + 1], (SSM_GROUP, LANES)) for r in idx], axis=0)
        return rep(pw_r), rep(pw_i)

    def b_rows(b_ref):
        b2 = jnp.concatenate([b_ref[0], b_ref[0]], axis=0)
        return tile_pos(lax.dot_general(eye, b2, (((1,), (1,)), ((), ())), preferred_element_type=F32,
                                        precision=HIGHEST))

    def c_rows(c_ref):
        return tile_pos(f32dot(c_ref[0], lane_fold))

    bbar_r, bbar_i = _cmul(coef_r, coef_i, b_rows(bre_ref), b_rows(bim_ref))
    c_r = c_rows(cre_ref)
    c_i = c_rows(cim_ref)

    a_r, a_i = _cmul(bbar_r, bbar_i, *power_rows(lambda p: -p))
    a2c = jnp.where(imag_lane, -a_i, a_r)
    m_r, m_i = _cmul(c_r, c_i, *power_rows(lambda p: p))
    bmc = jnp.where(imag_lane, m_i, m_r)
    tt = f32dot(bmc, a2c.T)
    causal = (lax.broadcasted_iota(I32, (SSM_ROW, 1), 0) // SSM_GROUP
              >= lax.broadcasted_iota(I32, (1, SSM_ROW), 1) // SSM_GROUP)
    tt_ref[0] = jnp.where(causal, tt, 0.0).astype(BF16)

    w_r, w_i = _cmul(bbar_r, bbar_i, *power_rows(lambda p: SSM_CHUNK - 1 - p))
    wz_ref[0, :, :LANES] = jnp.where(imag_lane, w_i, w_r).astype(BF16)
    wz_ref[0, :, LANES:] = jnp.where(imag_lane, w_r, w_i).astype(BF16)

    y_r, y_i = _cmul(c_r, c_i, *power_rows(lambda p: p + 1))
    wyt_ref[0] = jnp.where(imag_lane, -y_i, y_r).astype(BF16)

    cs_ref[0, 0:1, :] = pw_r[N_POW - 1:N_POW]
    cs_ref[0, 1:2, :] = jnp.where(imag_lane, pw_i[N_POW - 1:N_POW], -pw_i[N_POW - 1:N_POW])


def _ssm_params(lam_re, lam_im, log_dt, b_re, b_im, c_re, c_im):
    g = lam_re.shape[0]
    lam = jnp.stack([lam_re, lam_im, jnp.broadcast_to(log_dt[:, None], (g, STATE))], axis=1)
    lam = jnp.concatenate([lam, lam], axis=2)
    blk = lambda *shape: pl.BlockSpec((1,) + shape, lambda i: (i, 0, 0))
    return pl.pallas_call(
        _ssm_param_kernel,
        grid=(g,),
        in_specs=[blk(3, LANES), blk(STATE, SSM_GROUP), blk(STATE, SSM_GROUP), blk(SSM_GROUP, STATE),
                  blk(SSM_GROUP, STATE)],
        out_specs=[blk(SSM_ROW, SSM_ROW), blk(SSM_ROW, SSM_ROW), blk(SSM_ROW, LANES), blk(2, LANES)],
        out_shape=[jax.ShapeDtypeStruct((g, SSM_ROW, SSM_ROW), BF16),
                   jax.ShapeDtypeStruct((g, SSM_ROW, SSM_ROW), BF16),
                   jax.ShapeDtypeStruct((g, SSM_ROW, LANES), BF16),
                   jax.ShapeDtypeStruct((g, 2, LANES), F32)],
        compiler_params=_cparams(1),
        name="ssm_params",
    )(lam, b_re, b_im, c_re, c_im)


def _ssm_kernel(ut_ref, tt_ref, wz_ref, wyt_ref, cs_ref, d_ref, yt_ref, z_scr, s_scr):
    batch, _, _, nc = ut_ref.shape
    ut = jnp.concatenate([ut_ref[b, 0] for b in range(batch)], axis=1)
    _to_lane_blocks(z_scr, lax.dot_general(ut, wz_ref[0], (((0,), (0,)), ((), ())), preferred_element_type=F32))
    c1 = cs_ref[0, 0:1, :]
    c2 = cs_ref[0, 1:2, :]

    def step(c, carry):
        s1, s2 = carry
        rows = pl.ds(c, batch, stride=nc)
        s_scr[rows, :] = s1
        n1 = c1 * s1 + c2 * s2 + z_scr[0, rows, :]
        n2 = c1 * s2 - c2 * s1 + z_scr[1, rows, :]
        return n1, n2

    zero = jnp.zeros((batch, LANES), F32)
    lax.fori_loop(0, nc, step, (zero, zero), unroll=8)
    y = jnp.dot(tt_ref[0], ut, preferred_element_type=F32)
    y = y + lax.dot_general(wyt_ref[0], s_scr[...].astype(BF16), (((1,), (1,)), ((), ())),
                            preferred_element_type=F32)
    y = y + d_ref[0] * ut.astype(F32)
    for b in range(batch):
        yt_ref[b, 0] = y[:, b * nc:(b + 1) * nc]


def _ssm(ut, tt, wz, wyt, cs, d_skip):
    b, g, _, nc = ut.shape
    d_col = jnp.tile(d_skip.reshape(g, 1, SSM_GROUP), (1, SSM_CHUNK, 1)).reshape(g, SSM_ROW, 1)
    blk = lambda *shape: pl.BlockSpec((1,) + shape, lambda i: (i, 0, 0))
    act = pl.BlockSpec((b, 1, SSM_ROW, nc), lambda i: (0, i, 0, 0))
    return pl.pallas_call(
        _ssm_kernel,
        grid=(g,),
        in_specs=[act, blk(SSM_ROW, SSM_ROW), blk(SSM_ROW, SSM_ROW), blk(SSM_ROW, LANES), blk(2, LANES),
                  blk(SSM_ROW, 1)],
        out_specs=act,
        out_shape=jax.ShapeDtypeStruct((b, g, SSM_ROW, nc), F32),
        scratch_shapes=[pltpu.VMEM((SSM_ROW // LANES, b * nc, LANES), F32), pltpu.VMEM((b * nc, LANES), F32)],
        compiler_params=_cparams(1),
        name="ssm",
    )(ut, tt, wz, wyt, cs, d_col)


def _post_kernel(x4_hbm, attn_ref, yt_ref, mod_ref, wglut_ref, bglu_ref, sn_ref, wout_ref, nf_ref, wr_ref, br_ref,
                 tri_ref, x1_ref, h2_ref, eidx_ref, wts_ref, lrank_ref, r0_ref, cnt_ref, carry_ref, xp_buf, sems):
    @pl.when((pl.program_id(0) == 0) & (pl.program_id(1) == 0))
    def _():
        carry_ref[...] = jnp.zeros_like(carry_ref)

    slot = _prefetch_pos_rows(x4_hbm, xp_buf, sems, POST_POS)
    nc = attn_ref.shape[2]
    ts = POST_POS * nc
    d = x1_ref.shape[3]
    yt = jnp.concatenate(
        [yt_ref[0, :, il * SSM_GROUP:(il + 1) * SSM_GROUP, :].reshape(D_SSM, nc) for il in range(POST_POS)], axis=1)
    g = jax.nn.gelu(yt)
    gate = jax.nn.sigmoid(jnp.dot(wglut_ref[...], g.astype(BF16), preferred_element_type=F32) + bglu_ref[...])
    ssm_t = _rms(g * gate, axis=0) * sn_ref[...]
    mixed = jnp.concatenate([attn_ref[0].reshape(ts, D_ATTN).astype(BF16), ssm_t.T.astype(BF16)], axis=-1)
    o = jnp.dot(mixed, wout_ref[...], preferred_element_type=F32)
    x = jnp.concatenate([xp_buf[slot, il] for il in range(POST_POS)], axis=0)
    x1 = x + mod_ref[0, 2:3, :] * o
    x1_ref[0] = x1.reshape(POST_POS, nc, d)
    h2 = _rms(x1) * nf_ref[...] * (1.0 + mod_ref[0, 4:5, :]) + mod_ref[0, 3:4, :]
    h2_ref[0] = h2.astype(BF16).reshape(POST_POS, nc, d)

    logits = lax.dot_general(wr_ref[...], h2.astype(BF16), (((1,), (1,)), ((), ())),
                             preferred_element_type=F32) + br_ref[...]
    iota_e = lax.broadcasted_iota(I32, (N_EXPERTS, ts), 0).astype(F32)
    l = logits
    idxs, vals = [], []
    for _ in range(TOP_K):
        m = jnp.max(l, axis=0, keepdims=True)
        idx = jnp.min(jnp.where(l == m, iota_e, float(N_EXPERTS)), axis=0, keepdims=True)
        idxs.append(idx)
        vals.append(m)
        l = jnp.where(iota_e == idx, -jnp.inf, l)
    es = [jnp.exp(v - vals[0]) for v in vals]
    tot = es[0] + es[1] + es[2] + es[3]
    member = jnp.zeros((N_EXPERTS, ts), F32)
    for idx in idxs:
        member = member + (iota_e == idx).astype(F32)
    before = jnp.dot(member.astype(BF16), tri_ref[...], preferred_element_type=F32)
    for k in range(TOP_K):
        eidx_ref[k:k + 1, :] = idxs[k].astype(I32)
        wts_ref[k:k + 1, :] = es[k] / tot
        lrank_ref[k:k + 1, :] = jnp.sum(jnp.where(iota_e == idxs[k], before, 0.0), axis=0, keepdims=True).astype(I32)
    r0_ref[0] = carry_ref[...].astype(I32)
    carry = carry_ref[...] + jnp.sum(member, axis=1, keepdims=True)
    carry_ref[...] = carry
    cnt_ref[...] = carry.astype(I32)


def _post(x, attn, yt, mod, w_glu, b_glu, ssm_norm, w_out, norm_ffn, w_router, b_router):
    b, s, d = x.shape
    nc = s // SSM_CHUNK
    ts = POST_POS * nc
    nt = SSM_CHUNK // POST_POS
    t = b * s
    pm = lambda bi, j: (bi, j, 0, 0)
    const = lambda bi, j: (0, 0)
    tok = lambda bi, j: (0, bi * nt + j)
    tri = (lax.broadcasted_iota(I32, (ts, ts), 0) < lax.broadcasted_iota(I32, (ts, ts), 1)).astype(BF16)
    col = lambda a: a.reshape(-1, 1)
    return pl.pallas_call(
        _post_kernel,
        grid=(b, nt),
        in_specs=[pl.BlockSpec(memory_space=pl.ANY),
                  pl.BlockSpec((1, POST_POS, nc, D_ATTN), pm),
                  pl.BlockSpec((1, N_GROUPS, POST_POS * SSM_GROUP, nc), lambda bi, j: (bi, 0, j, 0)),
                  pl.BlockSpec((1, 6, d), lambda bi, j: (bi, 0, 0)),
                  pl.BlockSpec((D_SSM, D_SSM), const),
                  pl.BlockSpec((D_SSM, 1), const),
                  pl.BlockSpec((D_SSM, 1), const),
                  pl.BlockSpec((d, d), const),
                  pl.BlockSpec((1, d), const),
                  pl.BlockSpec((N_EXPERTS, d), const),
                  pl.BlockSpec((N_EXPERTS, 1), const),
                  pl.BlockSpec((ts, ts), const)],
        out_specs=[pl.BlockSpec((1, POST_POS, nc, d), pm),
                   pl.BlockSpec((1, POST_POS, nc, d), pm),
                   pl.BlockSpec((TOP_K, ts), tok),
                   pl.BlockSpec((TOP_K, ts), tok),
                   pl.BlockSpec((TOP_K, ts), tok),
                   pl.BlockSpec((1, N_EXPERTS, 1), lambda bi, j: (bi * nt + j, 0, 0)),
                   pl.BlockSpec((N_EXPERTS, 1), const)],
        out_shape=[jax.ShapeDtypeStruct((b, SSM_CHUNK, nc, d), F32),
                   jax.ShapeDtypeStruct((b, SSM_CHUNK, nc, d), BF16),
                   jax.ShapeDtypeStruct((TOP_K, t), I32),
                   jax.ShapeDtypeStruct((TOP_K, t), F32),
                   jax.ShapeDtypeStruct((TOP_K, t), I32),
                   jax.ShapeDtypeStruct((b * nt, N_EXPERTS, 1), I32),
                   jax.ShapeDtypeStruct((N_EXPERTS, 1), I32)],
        scratch_shapes=[pltpu.VMEM((N_EXPERTS, 1), F32), pltpu.VMEM((2, POST_POS, nc, d), F32),
                        pltpu.SemaphoreType.DMA((2,))],
        compiler_params=_cparams(2),
        name="post",
    )(x.reshape(b, nc, SSM_CHUNK, d), attn, yt, mod, w_glu.T.astype(BF16), col(b_glu), col(ssm_norm), w_out.astype(BF16),
      norm_ffn.reshape(1, -1), w_router.T.astype(BF16), col(b_router), tri)


def _route_kernel(eidx_ref, lrank_ref, r0_ref, cnt_ref, ls_ref, tab_ref, te_ref, nv_ref, nx_ref, pad_ref):
    cnt = cnt_ref[...]
    tiles = (cnt + (RUN - 1 + FFN_ROWS - 1)) // FFN_ROWS
    er = lax.broadcasted_iota(I32, (N_EXPERTS, N_EXPERTS), 0)
    ec = lax.broadcasted_iota(I32, (N_EXPERTS, N_EXPERTS), 1)
    ltri = (ec < er).astype(BF16)

    def excl_cumsum(v):
        vb = jnp.broadcast_to(v.astype(F32), (N_EXPERTS, LANES)).astype(BF16)
        return jnp.dot(ltri, vb, preferred_element_type=F32)[:, 0:1].astype(I32)

    start_t = excl_cumsum(tiles)
    end_t = start_t + tiles
    start = start_t * FFN_ROWS
    pad_ref[...] = start + cnt

    nb = r0_ref.shape[0]
    ts = eidx_ref.shape[1] // nb
    iota_e = lax.broadcasted_iota(I32, (N_EXPERTS, ts), 0)
    iota_t = lax.broadcasted_iota(I32, (N_EXPERTS, TABW), 0)
    chunk = lax.broadcasted_iota(I32, (1, TABW), 1)

    def block(b, carry):
        lanes = pl.ds(pl.multiple_of(b * ts, ts), ts)
        sels = [iota_e == eidx_ref[k:k + 1, lanes] for k in range(TOP_K)]
        member = sels[0].astype(I32) + sels[1].astype(I32) + sels[2].astype(I32) + sels[3].astype(I32)
        nch = (jnp.sum(member, axis=1, keepdims=True) + (RUN - 1)) // RUN
        cb = excl_cumsum(nch)
        end_c = cb + nch
        for k in range(TOP_K):
            first = jnp.sum(jnp.where(sels[k], cb, 0), axis=0, keepdims=True)
            lr = lrank_ref[k:k + 1, lanes]
            ls_ref[k:k + 1, lanes] = (first + lax.shift_right_logical(lr, RUN_SHIFT)) * RUN + (lr & (RUN - 1))
        e_of_c = jnp.sum((chunk >= end_c).astype(I32), axis=0, keepdims=True)
        sel_c = iota_t == e_of_c
        first_c = jnp.sum(jnp.where(sel_c, cb, 0), axis=0, keepdims=True)
        slot0_c = jnp.sum(jnp.where(sel_c, start + r0_ref[b], 0), axis=0, keepdims=True)
        n_chunks = jnp.max(end_c, axis=0, keepdims=True)
        row = jnp.where(chunk < n_chunks, slot0_c + (chunk - first_c) * RUN, -1)
        tab_ref[pl.ds(b, 1), :] = jnp.where(chunk == TABW - 1, n_chunks, row)
        return carry

    lax.fori_loop(0, nb, block, 0)

    nv = jnp.max(end_t, axis=0, keepdims=True)
    width = te_ref.shape[1]
    ti = jnp.minimum(lax.broadcasted_iota(I32, (N_EXPERTS, width), 1), nv - 1)
    te = jnp.minimum(jnp.sum((ti >= end_t).astype(I32), axis=0, keepdims=True), N_EXPERTS - 1)
    te_ref[...] = te
    nv_ref[...] = jnp.broadcast_to(nv, nv_ref.shape)
    ie = lax.broadcasted_iota(I32, (N_EXPERTS, width), 0)
    own_end = jnp.sum(jnp.where(ie == te, end_t, 0), axis=0, keepdims=True)
    nxt = jnp.minimum(jnp.sum((own_end >= end_t).astype(I32), axis=0, keepdims=True), N_EXPERTS - 1)
    nx_ref[...] = jnp.where(own_end < nv, nxt, -1)


def _route(eidx, lrank, r0, cnt, n_tiles):
    t = eidx.shape[1]
    nb = r0.shape[0]
    width = -(-n_tiles // LANES) * LANES
    return pl.pallas_call(
        _route_kernel,
        out_shape=[jax.ShapeDtypeStruct((TOP_K, t), I32),
                   jax.ShapeDtypeStruct((nb, TABW), I32),
                   jax.ShapeDtypeStruct((1, width), I32),
                   jax.ShapeDtypeStruct((1, LANES), I32),
                   jax.ShapeDtypeStruct((1, width), I32),
                   jax.ShapeDtypeStruct((N_EXPERTS, 1), I32)],
        name="route",
    )(eidx, lrank, r0, cnt)


def _for_chunk_pairs(n, fn):
    def body(i, carry):
        fn(2 * i, 0)

        @pl.when(2 * i + 1 < n)
        def _():
            fn(2 * i + 1, 1)
        return carry
    lax.fori_loop(0, lax.shift_right_logical(n + 1, 1), body, 0)


def _local_rows(ts):
    return ts * TOP_K + N_EXPERTS * RUN


def _dispatch_kernel(tab_ref, pad_ref, nvt_ref, h_ref, ls_ref, xs_ref, buf, zbuf, sems, zsem):
    b = pl.program_id(0)
    slot = b % 2
    ts = h_ref.shape[0]
    local = buf.shape[2]

    def chunk_copy(sl, blk, c):
        rows = pl.ds(pl.multiple_of(c * RUN, RUN), RUN)
        return pltpu.make_async_copy(buf.at[sl, :, rows, :], xs_ref.at[:, pl.ds(tab_ref[blk * TABW + c], RUN), :],
                                     sems.at[sl])

    def for_chunks(blk, fn):
        _for_chunk_pairs(tab_ref[blk * TABW + TABW - 1], fn)

    @pl.when(b == 0)
    def _():
        zbuf[...] = jnp.zeros_like(zbuf)
        zrows = zbuf.shape[1]
        zero = lambda row: pltpu.make_async_copy(zbuf, xs_ref.at[:, pl.ds(row, zrows), :], zsem)
        for phase in range(3):
            for e in range(phase, N_EXPERTS, 3):
                zero(pad_ref[e]).start()
            for e in range(phase, N_EXPERTS, 3):
                zero(pad_ref[e]).wait()
        ztile = lambda i: pltpu.make_async_copy(zbuf.at[:, pl.ds(0, FFN_ROWS), :],
                                                xs_ref.at[:, pl.ds((nvt_ref[0] + i) * FFN_ROWS, FFN_ROWS), :], zsem)

        def tail_start(i, carry):
            ztile(i).start()
            return carry

        def tail_wait(i, carry):
            ztile(i).wait()
            return carry
        lax.fori_loop(0, nvt_ref[1] - nvt_ref[0], tail_start, 0)
        lax.fori_loop(0, nvt_ref[1] - nvt_ref[0], tail_wait, 0)

    r = lax.broadcasted_iota(I32, (local, ts), 0)
    hit = (r == ls_ref[0:1, :]) | (r == ls_ref[1:2, :]) | (r == ls_ref[2:3, :]) | (r == ls_ref[3:4, :])
    srt = jnp.dot(hit.astype(BF16), h_ref[...], preferred_element_type=F32)
    packed = _pack_exact_halves(srt)
    for pb in range(PANELS):
        buf[slot, pb] = packed[:, pb * LANES:(pb + 1) * LANES]

    @pl.when(b > 0)
    def _():
        for_chunks(b - 1, lambda c, p: chunk_copy(1 - slot, b - 1, c).wait())

    for_chunks(b, lambda c, p: chunk_copy(slot, b, c).start(priority=p))

    @pl.when(b == pl.num_programs(0) - 1)
    def _():
        for_chunks(b, lambda c, p: chunk_copy(slot, b, c).wait())


def _dispatch(tab, pad, nvt, h2, ls, n_rows):
    t, d = h2.shape
    nb = tab.shape[0] // TABW
    ts = t // nb
    return pl.pallas_call(
        _dispatch_kernel,
        grid_spec=pltpu.PrefetchScalarGridSpec(
            num_scalar_prefetch=3,
            grid=(nb,),
            in_specs=[pl.BlockSpec((ts, d), lambda i, *_: (i, 0)),
                      pl.BlockSpec((TOP_K, ts), lambda i, *_: (0, i))],
            out_specs=pl.BlockSpec(memory_space=pl.ANY),
            scratch_shapes=[pltpu.VMEM((2, PANELS, _local_rows(ts), LANES), U32),
                            pltpu.VMEM((PANELS, FFN_ROWS + RUN, LANES), U32),
                            pltpu.SemaphoreType.DMA((2,)), pltpu.SemaphoreType.DMA],
        ),
        out_shape=jax.ShapeDtypeStruct((PANELS, n_rows, LANES), U32),
        compiler_params=_cparams(1, ROW_VMEM_BYTES),
        name="dispatch",
    )(tab, pad, nvt, h2, ls)


def _ffn_kernel(te_ref, nv_ref, nx_ref, xs_ref, wgu_hbm, bg_ref, bu_ref, wd_hbm, bd_ref, perm_ref, ys_ref,
                wgu_stage, wd_stage, wg_scr, wu_scr, wd_scr, sems):
    i = pl.program_id(0)
    valid = i < nv_ref[0]
    new_expert = (i == 0) | (te_ref[i] != te_ref[jnp.maximum(i - 1, 0)])

    def stage_copies(e):
        return (pltpu.make_async_copy(wgu_hbm.at[e], wgu_stage, sems.at[0]),
                pltpu.make_async_copy(wd_hbm.at[e], wd_stage, sems.at[1]))

    @pl.when(valid & new_expert)
    def _():
        e = te_ref[i]

        @pl.when(i == 0)
        def _():
            for cp in stage_copies(e):
                cp.start()

        for cp in stage_copies(e):
            cp.wait()
        for c in range(2 * D_FF // PERM):
            w = wgu_stage[:, c * PERM:(c + 1) * PERM].astype(BF16)
            pw = jnp.dot(w, perm_ref[...], preferred_element_type=F32).astype(BF16)
            wg_scr[:, c * (PERM // 2):(c + 1) * (PERM // 2)] = pw[:, :PERM // 2]
            wu_scr[:, c * (PERM // 2):(c + 1) * (PERM // 2)] = pw[:, PERM // 2:]
        wd_scr[...] = wd_stage[...].astype(BF16)

        @pl.when(nx_ref[i] >= 0)
        def _():
            for cp in stage_copies(nx_ref[i]):
                cp.start()

    @pl.when(valid)
    def _():
        x_hi, x_lo = _unpack_halves(_load_panels(xs_ref))
        x_hi = x_hi.astype(BF16)
        x_lo = x_lo.astype(BF16)
        gate = (jnp.dot(x_hi, wg_scr[:HALF, :], preferred_element_type=F32)
                + jnp.dot(x_lo, wg_scr[HALF:, :], preferred_element_type=F32) + bg_ref[0])
        up = (jnp.dot(x_hi, wu_scr[:HALF, :], preferred_element_type=F32)
              + jnp.dot(x_lo, wu_scr[HALF:, :], preferred_element_type=F32) + bu_ref[0])
        gate = jnp.minimum(gate, SWIGLU_LIMIT)
        up = jnp.clip(up, -SWIGLU_LIMIT, SWIGLU_LIMIT)
        act = (up + 1.0) * (gate * jax.nn.sigmoid(SWIGLU_ALPHA * gate))
        y = jnp.dot(act.astype(BF16), wd_scr[...], preferred_element_type=F32) + bd_ref[0]
        _store_panels(ys_ref, _pack_halves(y))


def _ffn(te, nv, nx, xs, w_gate_up, bg, bu, w_down, bd, n_tiles):
    d = D_MODEL
    tile = lambda i, te, nv, nx: (0, jnp.minimum(i, nv[0] - 1), 0)
    wsel = lambda i, te, nv, nx: (te[i], 0, 0)
    r = lax.broadcasted_iota(I32, (PERM, PERM), 0)
    c = lax.broadcasted_iota(I32, (PERM, PERM), 1)
    perm = (r == jnp.where(c < PERM // 2, 2 * c, 2 * (c - PERM // 2) + 1)).astype(BF16)
    return pl.pallas_call(
        _ffn_kernel,
        grid_spec=pltpu.PrefetchScalarGridSpec(
            num_scalar_prefetch=3,
            grid=(n_tiles,),
            in_specs=[pl.BlockSpec((PANELS, FFN_ROWS, LANES), tile),
                      pl.BlockSpec(memory_space=pl.ANY),
                      pl.BlockSpec((1, 1, D_FF), wsel),
                      pl.BlockSpec((1, 1, D_FF), wsel),
                      pl.BlockSpec(memory_space=pl.ANY),
                      pl.BlockSpec((1, 1, d), wsel),
                      pl.BlockSpec((PERM, PERM), lambda i, te, nv, nx: (0, 0))],
            out_specs=pl.BlockSpec((PANELS, FFN_ROWS, LANES), tile),
            scratch_shapes=[pltpu.VMEM((d, 2 * D_FF), F32), pltpu.VMEM((D_FF, d), F32),
                            pltpu.VMEM((d, D_FF), BF16), pltpu.VMEM((d, D_FF), BF16), pltpu.VMEM((D_FF, d), BF16),
                            pltpu.SemaphoreType.DMA((2,))],
        ),
        out_shape=jax.ShapeDtypeStruct(xs.shape, U32),
        input_output_aliases={3: 0},
        compiler_params=_cparams(1, FFN_VMEM_BYTES),
        name="ffn",
    )(te, nv, nx, xs, w_gate_up, bg, bu, w_down, bd, perm)


def _combine_kernel(tab_ref, x1_ref, ls_ref, w_ref, mod_ref, ys_ref, o4_hbm, ybuf, ob_buf, sems, osems, *, n_blk):
    jj = pl.program_id(1)
    blk = pl.program_id(0) * pl.num_programs(1) + jj
    slot = blk % 2
    nc = x1_ref.shape[2]
    tt = POST_POS * nc
    d = x1_ref.shape[3]
    local = ybuf.shape[2]

    def chunk_copy(sl, bk, c):
        rows = pl.ds(pl.multiple_of(c * RUN, RUN), RUN)
        return pltpu.make_async_copy(ys_ref.at[:, pl.ds(tab_ref[bk * TABW + c], RUN), :], ybuf.at[sl, :, rows, :],
                                     sems.at[sl])

    def for_chunks(bk, fn):
        _for_chunk_pairs(tab_ref[bk * TABW + TABW - 1], fn)

    @pl.when(blk == 0)
    def _():
        ybuf[...] = jnp.zeros_like(ybuf)
        for_chunks(0, lambda c, p: chunk_copy(0, 0, c).start(priority=p))

    @pl.when(blk + 1 < n_blk)
    def _():
        for_chunks(blk + 1, lambda c, p: chunk_copy(1 - slot, blk + 1, c).start(priority=p))

    for_chunks(blk, lambda c, p: chunk_copy(slot, blk, c).wait())

    r = lax.broadcasted_iota(I32, (tt, local), 1)
    wm = jnp.zeros((tt, local), F32)
    for k in range(TOP_K):
        wm = wm + jnp.where(r == ls_ref[:, k:k + 1], w_ref[:, k:k + 1], 0.0)
    wm = wm.astype(BF16)
    halves = [_unpack_halves(ybuf[slot, pb]) for pb in range(PANELS)]
    y_hi = jnp.concatenate([h.astype(BF16) for h, _ in halves], axis=-1)
    y_lo = jnp.concatenate([l.astype(BF16) for _, l in halves], axis=-1)
    acc_hi = jnp.dot(wm, y_hi, preferred_element_type=F32)
    acc_lo = jnp.dot(wm, y_lo, preferred_element_type=F32)
    g2 = mod_ref[0, 5:6, :]
    x1 = x1_ref[0].reshape(tt, d)
    out = jnp.concatenate([x1[:, :HALF] + g2[:, :HALF] * acc_hi, x1[:, HALF:] + g2[:, HALF:] * acc_lo], axis=-1)

    def out_copies(sl, b_, j_):
        return [pltpu.make_async_copy(ob_buf.at[sl, il], o4_hbm.at[b_, :, POST_POS * j_ + il, :], osems.at[sl])
                for il in range(POST_POS)]

    @pl.when(blk >= 2)
    def _():
        for cp in out_copies(slot, 0, 0):
            cp.wait()

    for il in range(POST_POS):
        ob_buf[slot, il] = out[il * nc:(il + 1) * nc]
    for cp in out_copies(slot, pl.program_id(0), jj):
        cp.start()

    @pl.when(blk == n_blk - 1)
    def _():
        for cp in out_copies(slot, 0, 0):
            cp.wait()
        if n_blk > 1:
            for cp in out_copies(1 - slot, 0, 0):
                cp.wait()


def _combine(tab, x1, ls_t, wts_t, mod, ys):
    b, _, nc, d = x1.shape
    s = SSM_CHUNK * nc
    tt = POST_POS * nc
    nt = SSM_CHUNK // POST_POS
    o4 = pl.pallas_call(
        functools.partial(_combine_kernel, n_blk=b * nt),
        grid_spec=pltpu.PrefetchScalarGridSpec(
            num_scalar_prefetch=1,
            grid=(b, nt),
            in_specs=[pl.BlockSpec((1, POST_POS, nc, d), lambda bi, j, *_: (bi, j, 0, 0)),
                      pl.BlockSpec((tt, TOP_K), lambda bi, j, *_: (bi * nt + j, 0)),
                      pl.BlockSpec((tt, TOP_K), lambda bi, j, *_: (bi * nt + j, 0)),
                      pl.BlockSpec((1, 6, d), lambda bi, j, *_: (bi, 0, 0)),
                      pl.BlockSpec(memory_space=pl.ANY)],
            out_specs=pl.BlockSpec(memory_space=pl.ANY),
            scratch_shapes=[pltpu.VMEM((2, PANELS, _local_rows(tt), LANES), U32),
                            pltpu.VMEM((2, POST_POS, nc, d), F32),
                            pltpu.SemaphoreType.DMA((2,)), pltpu.SemaphoreType.DMA((2,))],
        ),
        out_shape=jax.ShapeDtypeStruct((b, nc, SSM_CHUNK, d), F32),
        compiler_params=_cparams(2),
        name="combine",
    )(tab, x1, ls_t, wts_t, mod, ys)
    return o4.reshape(b, s, d)


def kernel(x, c, w_ada, b_ada, norm_mix, w_in, b_in, q_norm, k_norm, sinks, lam_re, lam_im, log_dt, b_re, b_im,
           c_re, c_im, d_skip, w_glu, b_glu, attn_out_norm, ssm_out_norm, w_out, norm_ffn, w_router, b_router,
           w_gate_up, b_gate_up, w_down, b_down):
    b, s, d = x.shape
    t = b * s
    depth = w_ada.shape[0]
    n_tiles = -(-(t * TOP_K + N_EXPERTS * (RUN - 1 + FFN_ROWS - 1)) // FFN_ROWS)
    n_alloc = n_tiles + 2
    for l in range(depth):
        mod = _adaln(c, w_ada[l], b_ada[l]).reshape(b, 6, d)
        q, k, v, ut = _inproj(x, mod, norm_mix[l], w_in[l], b_in[l])
        attn = _attention(q, k, v, sinks[l], q_norm[l], k_norm[l], attn_out_norm[l])
        tt, wz, wyt, cs = _ssm_params(lam_re[l], lam_im[l], log_dt[l], b_re[l], b_im[l], c_re[l], c_im[l])
        yt = _ssm(ut, tt, wz, wyt, cs, d_skip[l])
        x1, h2, eidx, wts, lrank, r0, cnt = _post(x, attn, yt, mod, w_glu[l], b_glu[l], ssm_out_norm[l], w_out[l],
                                                  norm_ffn[l], w_router[l], b_router[l])
        ls, tab, te, nv, nx, pad = _route(eidx, lrank, r0, cnt, n_tiles)
        tab = tab.reshape(-1)
        nvt = jnp.stack([nv[0, 0], jnp.int32(n_alloc)])
        xs = _dispatch(tab, pad.reshape(-1), nvt, h2.reshape(t, d), ls, n_alloc * FFN_ROWS)
        wgu = w_gate_up[l]
        bgu = b_gate_up[l]
        ys = _ffn(te[0, :n_tiles], nv[0, :1], nx[0, :n_tiles], xs, wgu, bgu[:, None, 0::2], bgu[:, None, 1::2],
                  w_down[l], b_down[l][:, None, :], n_tiles)
        x = _combine(tab, x1, ls.T, wts.T, mod, ys)
    return x
```

```python
import functools
import math

import jax
import jax.numpy as jnp
from jax import lax
from jax.experimental import pallas as pl
from jax.experimental.pallas import tpu as pltpu

F32 = jnp.float32
BF16 = jnp.bfloat16
U32 = jnp.uint32
I32 = jnp.int32

D_MODEL = 1024
HEAD_DIM = 64
N_HEADS = 8
N_KV_HEADS = 2
Q_PER_KV = N_HEADS // N_KV_HEADS
D_ATTN = N_HEADS * HEAD_DIM
D_KV = N_KV_HEADS * HEAD_DIM
D_QKV = D_ATTN + 2 * D_KV
WINDOW = 128
BLOCK = 128
D_SSM = D_MODEL - D_ATTN
SSM_GROUP = 16
N_GROUPS = D_SSM // SSM_GROUP
STATE = 64
N_EXPERTS = 32
TOP_K = 4
D_FF = D_MODEL
SWIGLU_LIMIT = 7.0
SWIGLU_ALPHA = 1.702
EPS = 1e-6
NEG_INF = -1e30

LANES = 128
SSM_CHUNK = 16
SSM_ROW = SSM_CHUNK * SSM_GROUP
N_POW = 2 * SSM_CHUNK
HALF = D_MODEL // 2
PANELS = HALF // LANES

POS_PER_STEP = 4
ATTN_ROWS = 512
POST_POS = 2
FFN_ROWS = 256
RUN = 32
RUN_SHIFT = 5
TABW = 128
PERM = 256
FFN_VMEM_BYTES = 40 * 1024 * 1024
ROW_VMEM_BYTES = 48 * 1024 * 1024

HIGHEST = lax.Precision.HIGHEST
_ARB = "arbitrary"


def _cparams(n, vmem=None):
    return pltpu.CompilerParams(dimension_semantics=(_ARB,) * n, vmem_limit_bytes=vmem)


def _rms(x, axis=-1):
    return x * lax.rsqrt(jnp.mean(x * x, axis=axis, keepdims=True) + EPS)


def _pack_halves(y):
    hi = lax.bitcast_convert_type(y[:, :HALF].astype(BF16).astype(F32), U32)
    lo = lax.bitcast_convert_type(y[:, HALF:].astype(BF16).astype(F32), U32)
    return (hi & jnp.uint32(0xFFFF0000)) | (lo >> 16)


def _pack_exact_halves(y):
    return lax.bitcast_convert_type(y[:, :HALF], U32) | (lax.bitcast_convert_type(y[:, HALF:], U32) >> 16)


def _unpack_halves(w):
    hi = lax.bitcast_convert_type(w & jnp.uint32(0xFFFF0000), F32)
    lo = lax.bitcast_convert_type(w << 16, F32)
    return hi, lo


def _store_panels(ref, packed):
    for pb in range(PANELS):
        ref[pb] = packed[:, pb * LANES:(pb + 1) * LANES]


def _load_panels(ref):
    return jnp.concatenate([ref[pb] for pb in range(PANELS)], axis=-1)


def _prefetch_pos_rows(x4_hbm, buf, sems, n_pos):
    bi = pl.program_id(0)
    j = pl.program_id(1)
    nj = pl.num_programs(1)
    g = bi * nj + j
    slot = g % 2

    def copies(sl, b_, j_):
        return [pltpu.make_async_copy(x4_hbm.at[b_, :, n_pos * j_ + il, :], buf.at[sl, il], sems.at[sl])
                for il in range(n_pos)]

    @pl.when(g == 0)
    def _():
        for cp in copies(0, 0, 0):
            cp.start()

    @pl.when(g + 1 < pl.num_programs(0) * nj)
    def _():
        wrap = j + 1 == nj
        for cp in copies(1 - slot, jnp.where(wrap, bi + 1, bi), jnp.where(wrap, 0, j + 1)):
            cp.start()

    for cp in copies(slot, bi, j):
        cp.wait()
    return slot


def _to_lane_blocks(dst, src):
    for kb in range(dst.shape[0]):
        dst[kb] = src[:, kb * LANES:(kb + 1) * LANES]


def _adaln_kernel(c_ref, w_ref, b_ref, o_ref):
    c = c_ref[...]
    ca = c * jax.nn.sigmoid(c)
    o_ref[...] = jnp.dot(ca, w_ref[...], preferred_element_type=F32, precision=HIGHEST) + b_ref[...]


def _adaln(c, w_ada, b_ada):
    b, d = c.shape
    n = w_ada.shape[1] // d
    return pl.pallas_call(
        _adaln_kernel,
        grid=(n,),
        in_specs=[pl.BlockSpec((b, d), lambda j: (0, 0)),
                  pl.BlockSpec((d, d), lambda j: (0, j)),
                  pl.BlockSpec((1, d), lambda j: (0, j))],
        out_specs=pl.BlockSpec((b, d), lambda j: (0, j)),
        out_shape=jax.ShapeDtypeStruct((b, n * d), F32),
        compiler_params=_cparams(1),
        name="adaln",
    )(c, w_ada, b_ada.reshape(1, -1))


def _inproj_kernel(x4_hbm, x_ref, mod_ref, g_ref, wqkv_ref, bqkv_ref, wut_ref, but_ref, q_ref, k_ref, v_ref, ut_ref,
                   xp_buf, sems):
    nc = ut_ref.shape[3]
    slot = _prefetch_pos_rows(x4_hbm, xp_buf, sems, POS_PER_STEP)
    gain = g_ref[...]
    scale = 1.0 + mod_ref[0, 1:2, :]
    shift = mod_ref[0, 0:1, :]

    def norm_mod(x):
        return (_rms(x) * gain * scale + shift).astype(BF16)

    proj = jnp.dot(norm_mod(x_ref[0]), wqkv_ref[...], preferred_element_type=F32) + bqkv_ref[...]
    q_ref[0] = proj[:, :D_ATTN].astype(BF16)
    k_ref[0] = proj[:, D_ATTN:D_ATTN + D_KV].astype(BF16)
    v_ref[0] = proj[:, D_ATTN + D_KV:].astype(BF16)

    hs = jnp.concatenate([norm_mod(xp_buf[slot, il]) for il in range(POS_PER_STEP)], axis=0)
    ut = lax.dot_general(wut_ref[...], hs, (((1,), (1,)), ((), ())), preferred_element_type=F32) + but_ref[...]
    for il in range(POS_PER_STEP):
        piece = ut[:, il * nc:(il + 1) * nc].astype(BF16)
        ut_ref[0, :, il * SSM_GROUP:(il + 1) * SSM_GROUP, :] = piece.reshape(N_GROUPS, SSM_GROUP, nc)


def _inproj(x, mod, gain, w_in, b_in):
    b, s, d = x.shape
    nc = s // SSM_CHUNK
    rows = POS_PER_STEP * nc
    row = lambda bi, j: (bi, j, 0)
    const = lambda bi, j: (0, 0)
    w_qkv = w_in[:, :D_QKV].astype(BF16)
    w_ut = w_in[:, D_QKV:].T.astype(BF16)
    return pl.pallas_call(
        _inproj_kernel,
        grid=(b, SSM_CHUNK // POS_PER_STEP),
        in_specs=[pl.BlockSpec(memory_space=pl.ANY),
                  pl.BlockSpec((1, rows, d), row),
                  pl.BlockSpec((1, 6, d), lambda bi, j: (bi, 0, 0)),
                  pl.BlockSpec((1, d), const),
                  pl.BlockSpec((d, D_QKV), const),
                  pl.BlockSpec((1, D_QKV), const),
                  pl.BlockSpec((D_SSM, d), const),
                  pl.BlockSpec((D_SSM, 1), const)],
        out_specs=[pl.BlockSpec((1, rows, D_ATTN), row),
                   pl.BlockSpec((1, rows, D_KV), row),
                   pl.BlockSpec((1, rows, D_KV), row),
                   pl.BlockSpec((1, N_GROUPS, POS_PER_STEP * SSM_GROUP, nc), lambda bi, j: (bi, 0, j, 0))],
        out_shape=[jax.ShapeDtypeStruct((b, s, D_ATTN), BF16),
                   jax.ShapeDtypeStruct((b, s, D_KV), BF16),
                   jax.ShapeDtypeStruct((b, s, D_KV), BF16),
                   jax.ShapeDtypeStruct((b, N_GROUPS, SSM_ROW, nc), BF16)],
        scratch_shapes=[pltpu.VMEM((2, POS_PER_STEP, nc, d), F32), pltpu.SemaphoreType.DMA((2,))],
        compiler_params=_cparams(2),
        name="inproj",
    )(x.reshape(b, nc, SSM_CHUNK, d), x, mod, gain.reshape(1, d), w_qkv, b_in[:D_QKV].reshape(1, D_QKV), w_ut, b_in[D_QKV:].reshape(D_SSM, 1))


def _half_norm(x, low):
    sq = x * x
    s_lo = jnp.sum(jnp.where(low, sq, 0.0), axis=-1, keepdims=True)
    s_hi = jnp.sum(sq, axis=-1, keepdims=True) - s_lo
    inv = 1.0 / HEAD_DIM
    scale = jnp.where(low, lax.rsqrt(s_lo * inv + EPS), lax.rsqrt(s_hi * inv + EPS))
    return x * scale


def _attn_block(first, q, k_prev, k_cur, v_prev, v_cur, sinks_ref, qn, low, upper, rblk):
    no_prev = jnp.where(first, NEG_INF, 0.0)
    out_blocks = []
    for hk in range(N_KV_HEADS):
        qs = []
        for j in range(Q_PER_KV // 2):
            blk = hk * (Q_PER_KV // 2) + j
            qb = _half_norm(q[:, blk * LANES:(blk + 1) * LANES], low) * qn * (1.0 / math.sqrt(HEAD_DIM))
            qs.append(jnp.where(low, qb, 0.0))
            qs.append(jnp.where(low, 0.0, qb))
        qg = jnp.concatenate(qs, axis=0).astype(BF16)
        nt = (((1,), (1,)), ((), ()))
        s_prev = lax.dot_general(qg, k_prev[hk], nt, preferred_element_type=F32)
        s_cur = lax.dot_general(qg, k_cur[hk], nt, preferred_element_type=F32)
        s = jnp.where(upper, s_prev + no_prev, s_cur)
        sink = jnp.zeros((Q_PER_KV * BLOCK, 1), F32)
        for g in range(Q_PER_KV):
            sink = jnp.where(rblk == g, sinks_ref[hk * Q_PER_KV + g], sink)
        m = jnp.maximum(jnp.max(s, axis=-1, keepdims=True), sink)
        p = jnp.exp(s - m)
        den = jnp.sum(p, axis=-1, keepdims=True) + jnp.exp(sink - m)
        o = (jnp.dot(jnp.where(upper, p, 0.0).astype(BF16), v_prev[hk], preferred_element_type=F32)
             + jnp.dot(jnp.where(upper, 0.0, p).astype(BF16), v_cur[hk], preferred_element_type=F32)) / den
        for j in range(Q_PER_KV // 2):
            ev = o[(2 * j) * BLOCK:(2 * j + 1) * BLOCK]
            od = o[(2 * j + 1) * BLOCK:(2 * j + 2) * BLOCK]
            out_blocks.append(jnp.where(low, ev, od))
    return jnp.concatenate(out_blocks, axis=-1)


def _attn_kernel(sinks_ref, q_ref, k_ref, v_ref, qn_ref, kn_ref, on_ref, o_hbm, a_buf, sems, *, n_steps):
    step = pl.program_id(1)
    g = pl.program_id(0) * pl.num_programs(1) + step
    slot = g % 2
    cps = ATTN_ROWS // SSM_CHUNK
    nq = ATTN_ROWS // BLOCK

    def out_copies(sl, b_, s_):
        return [pltpu.make_async_copy(a_buf.at[sl, :, i, :], o_hbm.at[b_, i, pl.ds(s_ * cps, cps), :], sems.at[sl])
                for i in range(SSM_CHUNK)]

    @pl.when(g >= 2)
    def _():
        for cp in out_copies(slot, 0, 0):
            cp.wait()

    low = lax.broadcasted_iota(I32, (1, LANES), 1) < HEAD_DIM
    rows = Q_PER_KV * BLOCK
    upper = lax.broadcasted_iota(I32, (rows, BLOCK), 1) > lax.broadcasted_iota(I32, (rows, BLOCK), 0) % BLOCK
    rblk = lax.broadcasted_iota(I32, (rows, 1), 0) // BLOCK

    cur = pl.multiple_of(step * ATTN_ROWS, ATTN_ROWS)
    prev = pl.multiple_of(jnp.maximum(step * nq - 1, 0) * BLOCK, BLOCK)
    kall = jnp.concatenate([k_ref[0, pl.ds(prev, BLOCK), :], k_ref[0, pl.ds(cur, ATTN_ROWS), :]], axis=0).astype(F32)
    vall = jnp.concatenate([v_ref[0, pl.ds(prev, BLOCK), :], v_ref[0, pl.ds(cur, ATTN_ROWS), :]], axis=0).astype(F32)
    kall = _half_norm(kall, low) * kn_ref[...]
    kswap = pltpu.roll(kall, HEAD_DIM, axis=1)
    vswap = pltpu.roll(vall, HEAD_DIM, axis=1)
    k_dup = [jnp.where(low, kall, kswap).astype(BF16), jnp.where(low, kswap, kall).astype(BF16)]
    v_dup = [jnp.where(low, vall, vswap).astype(BF16), jnp.where(low, vswap, vall).astype(BF16)]
    blk = lambda a, i: [a[hk][i * BLOCK:(i + 1) * BLOCK] for hk in range(N_KV_HEADS)]

    for qb in range(nq):
        q = q_ref[0, qb * BLOCK:(qb + 1) * BLOCK, :].astype(F32)
        attn = _attn_block((step == 0) if qb == 0 else False, q, blk(k_dup, qb), blk(k_dup, qb + 1),
                           blk(v_dup, qb), blk(v_dup, qb + 1), sinks_ref, qn_ref[...], low, upper, rblk)
        attn = _rms(attn) * on_ref[...]
        cpb = BLOCK // SSM_CHUNK
        a_buf[slot, qb * cpb:(qb + 1) * cpb] = attn.reshape(cpb, SSM_CHUNK, D_ATTN)

    for cp in out_copies(slot, pl.program_id(0), step):
        cp.start()

    @pl.when(g == n_steps - 1)
    def _():
        for cp in out_copies(slot, 0, 0):
            cp.wait()
        if n_steps > 1:
            for cp in out_copies(1 - slot, 0, 0):
                cp.wait()


def _attention(q, k, v, sinks, q_norm, k_norm, out_norm):
    b, s, _ = q.shape
    tile2 = lambda g: jnp.tile(g.reshape(1, HEAD_DIM), (1, 2))
    cps = ATTN_ROWS // SSM_CHUNK
    return pl.pallas_call(
        functools.partial(_attn_kernel, n_steps=b * (s // ATTN_ROWS)),
        grid=(b, s // ATTN_ROWS),
        in_specs=[pl.BlockSpec(memory_space=pltpu.SMEM),
                  pl.BlockSpec((1, ATTN_ROWS, D_ATTN), lambda bi, n: (bi, n, 0)),
                  pl.BlockSpec((1, s, D_KV), lambda bi, n: (bi, 0, 0)),
                  pl.BlockSpec((1, s, D_KV), lambda bi, n: (bi, 0, 0)),
                  pl.BlockSpec((1, LANES), lambda bi, n: (0, 0)),
                  pl.BlockSpec((1, LANES), lambda bi, n: (0, 0)),
                  pl.BlockSpec((1, D_ATTN), lambda bi, n: (0, 0))],
        out_specs=pl.BlockSpec(memory_space=pl.ANY),
        out_shape=jax.ShapeDtypeStruct((b, SSM_CHUNK, s // SSM_CHUNK, D_ATTN), F32),
        scratch_shapes=[pltpu.VMEM((2, cps, SSM_CHUNK, D_ATTN), F32), pltpu.SemaphoreType.DMA((2,))],
        compiler_params=_cparams(2),
        name="attention",
    )(sinks, q, k, v, tile2(q_norm), tile2(k_norm), out_norm.reshape(1, D_ATTN))


def _cmul(ar, ai, br, bi):
    return ar * br - ai * bi, ar * bi + ai * br


def _ssm_param_kernel(lam_ref, bre_ref, bim_ref, cre_ref, cim_ref, tt_ref, wz_ref, wyt_ref, cs_ref):
    f32dot = functools.partial(jnp.dot, preferred_element_type=F32, precision=HIGHEST)
    lr = lam_ref[0, 0:1, :]
    li = lam_ref[0, 1:2, :]
    dt = jnp.exp(lam_ref[0, 2:3, :])
    rho = lr * dt
    th = li * dt
    imag_lane = lax.broadcasted_iota(I32, (1, LANES), 1) >= STATE

    kk = (lax.broadcasted_iota(I32, (N_POW, 1), 0) - (SSM_CHUNK - 1)).astype(F32)
    mag = jnp.exp(rho * kk)
    pw_r = mag * jnp.cos(th * kk)
    pw_i = mag * jnp.sin(th * kk)
    lb_r = pw_r[SSM_CHUNK:SSM_CHUNK + 1]
    lb_i = pw_i[SSM_CHUNK:SSM_CHUNK + 1]
    den = lr * lr + li * li
    coef_r = ((lb_r - 1.0) * lr + lb_i * li) / den
    coef_i = (lb_i * lr - (lb_r - 1.0) * li) / den

    eye = (lax.broadcasted_iota(I32, (SSM_GROUP, SSM_GROUP), 0)
           == lax.broadcasted_iota(I32, (SSM_GROUP, SSM_GROUP), 1)).astype(F32)
    lane_fold = (lax.broadcasted_iota(I32, (STATE, LANES), 1) % STATE
                 == lax.broadcasted_iota(I32, (STATE, LANES), 0)).astype(F32)

    def tile_pos(x):
        return jnp.concatenate([x] * SSM_CHUNK, axis=0)

    def power_rows(k_of_pos):
        idx = [k_of_pos(p) + (SSM_CHUNK - 1) for p in range(SSM_CHUNK)]
        rep = lambda t: jnp.concatenate([jnp.broadcast_to(t[r:r + 1], (SSM_GROUP, LANES)) for r in idx], axis=0)
        return rep(pw_r), rep(pw_i)

    def b_rows(b_ref):
        b2 = jnp.concatenate([b_ref[0], b_ref[0]], axis=0)
        return tile_pos(lax.dot_general(eye, b2, (((1,), (1,)), ((), ())), preferred_element_type=F32,
                                        precision=HIGHEST))

    def c_rows(c_ref):
        return tile_pos(f32dot(c_ref[0], lane_fold))

    bbar_r, bbar_i = _cmul(coef_r, coef_i, b_rows(bre_ref), b_rows(bim_ref))
    c_r = c_rows(cre_ref)
    c_i = c_rows(cim_ref)

    a_r, a_i = _cmul(bbar_r, bbar_i, *power_rows(lambda p: -p))
    a2c = jnp.where(imag_lane, -a_i, a_r)
    m_r, m_i = _cmul(c_r, c_i, *power_rows(lambda p: p))
    bmc = jnp.where(imag_lane, m_i, m_r)
    tt = f32dot(bmc, a2c.T)
    causal = (lax.broadcasted_iota(I32, (SSM_ROW, 1), 0) // SSM_GROUP
              >= lax.broadcasted_iota(I32, (1, SSM_ROW), 1) // SSM_GROUP)
    tt_ref[0] = jnp.where(causal, tt, 0.0).astype(BF16)

    w_r, w_i = _cmul(bbar_r, bbar_i, *power_rows(lambda p: SSM_CHUNK - 1 - p))
    wz_ref[0, :, :LANES] = jnp.where(imag_lane, w_i, w_r).astype(BF16)
    wz_ref[0, :, LANES:] = jnp.where(imag_lane, w_r, w_i).astype(BF16)

    y_r, y_i = _cmul(c_r, c_i, *power_rows(lambda p: p + 1))
    wyt_ref[0] = jnp.where(imag_lane, -y_i, y_r).astype(BF16)

    cs_ref[0, 0:1, :] = pw_r[N_POW - 1:N_POW]
    cs_ref[0, 1:2, :] = jnp.where(imag_lane, pw_i[N_POW - 1:N_POW], -pw_i[N_POW - 1:N_POW])


def _ssm_params(lam_re, lam_im, log_dt, b_re, b_im, c_re, c_im):
    g = lam_re.shape[0]
    lam = jnp.stack([lam_re, lam_im, jnp.broadcast_to(log_dt[:, None], (g, STATE))], axis=1)
    lam = jnp.concatenate([lam, lam], axis=2)
    blk = lambda *shape: pl.BlockSpec((1,) + shape, lambda i: (i, 0, 0))
    return pl.pallas_call(
        _ssm_param_kernel,
        grid=(g,),
        in_specs=[blk(3, LANES), blk(STATE, SSM_GROUP), blk(STATE, SSM_GROUP), blk(SSM_GROUP, STATE),
                  blk(SSM_GROUP, STATE)],
        out_specs=[blk(SSM_ROW, SSM_ROW), blk(SSM_ROW, SSM_ROW), blk(SSM_ROW, LANES), blk(2, LANES)],
        out_shape=[jax.ShapeDtypeStruct((g, SSM_ROW, SSM_ROW), BF16),
                   jax.ShapeDtypeStruct((g, SSM_ROW, SSM_ROW), BF16),
                   jax.ShapeDtypeStruct((g, SSM_ROW, LANES), BF16),
                   jax.ShapeDtypeStruct((g, 2, LANES), F32)],
        compiler_params=_cparams(1),
        name="ssm_params",
    )(lam, b_re, b_im, c_re, c_im)


def _ssm_kernel(ut_ref, tt_ref, wz_ref, wyt_ref, cs_ref, d_ref, yt_ref, z_scr, s_scr):
    batch, _, _, nc = ut_ref.shape
    ut = jnp.concatenate([ut_ref[b, 0] for b in range(batch)], axis=1)
    _to_lane_blocks(z_scr, lax.dot_general(ut, wz_ref[0], (((0,), (0,)), ((), ())), preferred_element_type=F32))
    c1 = cs_ref[0, 0:1, :]
    c2 = cs_ref[0, 1:2, :]

    def step(c, carry):
        s1, s2 = carry
        rows = pl.ds(c, batch, stride=nc)
        s_scr[rows, :] = s1
        n1 = c1 * s1 + c2 * s2 + z_scr[0, rows, :]
        n2 = c1 * s2 - c2 * s1 + z_scr[1, rows, :]
        return n1, n2

    zero = jnp.zeros((batch, LANES), F32)
    lax.fori_loop(0, nc, step, (zero, zero), unroll=8)
    y = jnp.dot(tt_ref[0], ut, preferred_element_type=F32)
    y = y + lax.dot_general(wyt_ref[0], s_scr[...].astype(BF16), (((1,), (1,)), ((), ())),
                            preferred_element_type=F32)
    y = y + d_ref[0] * ut.astype(F32)
    for b in range(batch):
        yt_ref[b, 0] = y[:, b * nc:(b + 1) * nc]


def _ssm(ut, tt, wz, wyt, cs, d_skip):
    b, g, _, nc = ut.shape
    d_col = jnp.tile(d_skip.reshape(g, 1, SSM_GROUP), (1, SSM_CHUNK, 1)).reshape(g, SSM_ROW, 1)
    blk = lambda *shape: pl.BlockSpec((1,) + shape, lambda i: (i, 0, 0))
    act = pl.BlockSpec((b, 1, SSM_ROW, nc), lambda i: (0, i, 0, 0))
    return pl.pallas_call(
        _ssm_kernel,
        grid=(g,),
        in_specs=[act, blk(SSM_ROW, SSM_ROW), blk(SSM_ROW, SSM_ROW), blk(SSM_ROW, LANES), blk(2, LANES),
                  blk(SSM_ROW, 1)],
        out_specs=act,
        out_shape=jax.ShapeDtypeStruct((b, g, SSM_ROW, nc), F32),
        scratch_shapes=[pltpu.VMEM((SSM_ROW // LANES, b * nc, LANES), F32), pltpu.VMEM((b * nc, LANES), F32)],
        compiler_params=_cparams(1),
        name="ssm",
    )(ut, tt, wz, wyt, cs, d_col)


def _post_kernel(x4_hbm, attn_ref, yt_ref, mod_ref, wglut_ref, bglu_ref, sn_ref, wout_ref, nf_ref, wr_ref, br_ref,
                 tri_ref, x1_ref, h2_ref, eidx_ref, wts_ref, lrank_ref, r0_ref, cnt_ref, carry_ref, xp_buf, sems):
    @pl.when((pl.program_id(0) == 0) & (pl.program_id(1) == 0))
    def _():
        carry_ref[...] = jnp.zeros_like(carry_ref)

    slot = _prefetch_pos_rows(x4_hbm, xp_buf, sems, POST_POS)
    nc = attn_ref.shape[2]
    ts = POST_POS * nc
    d = x1_ref.shape[3]
    yt = jnp.concatenate(
        [yt_ref[0, :, il * SSM_GROUP:(il + 1) * SSM_GROUP, :].reshape(D_SSM, nc) for il in range(POST_POS)], axis=1)
    g = jax.nn.gelu(yt)
    gate = jax.nn.sigmoid(jnp.dot(wglut_ref[...], g.astype(BF16), preferred_element_type=F32) + bglu_ref[...])
    ssm_t = _rms(g * gate, axis=0) * sn_ref[...]
    mixed = jnp.concatenate([attn_ref[0].reshape(ts, D_ATTN).astype(BF16), ssm_t.T.astype(BF16)], axis=-1)
    o = jnp.dot(mixed, wout_ref[...], preferred_element_type=F32)
    x = jnp.concatenate([xp_buf[slot, il] for il in range(POST_POS)], axis=0)
    x1 = x + mod_ref[0, 2:3, :] * o
    x1_ref[0] = x1.reshape(POST_POS, nc, d)
    h2 = _rms(x1) * nf_ref[...] * (1.0 + mod_ref[0, 4:5, :]) + mod_ref[0, 3:4, :]
    h2_ref[0] = h2.astype(BF16).reshape(POST_POS, nc, d)

    logits = lax.dot_general(wr_ref[...], h2.astype(BF16), (((1,), (1,)), ((), ())),
                             preferred_element_type=F32) + br_ref[...]
    iota_e = lax.broadcasted_iota(I32, (N_EXPERTS, ts), 0).astype(F32)
    l = logits
    idxs, vals = [], []
    for _ in range(TOP_K):
        m = jnp.max(l, axis=0, keepdims=True)
        idx = jnp.min(jnp.where(l == m, iota_e, float(N_EXPERTS)), axis=0, keepdims=True)
        idxs.append(idx)
        vals.append(m)
        l = jnp.where(iota_e == idx, -jnp.inf, l)
    es = [jnp.exp(v - vals[0]) for v in vals]
    tot = es[0] + es[1] + es[2] + es[3]
    member = jnp.zeros((N_EXPERTS, ts), F32)
    for idx in idxs:
        member = member + (iota_e == idx).astype(F32)
    before = jnp.dot(member.astype(BF16), tri_ref[...], preferred_element_type=F32)
    for k in range(TOP_K):
        eidx_ref[k:k + 1, :] = idxs[k].astype(I32)
        wts_ref[k:k + 1, :] = es[k] / tot
        lrank_ref[k:k + 1, :] = jnp.sum(jnp.where(iota_e == idxs[k], before, 0.0), axis=0, keepdims=True).astype(I32)
    r0_ref[0] = carry_ref[...].astype(I32)
    carry = carry_ref[...] + jnp.sum(member, axis=1, keepdims=True)
    carry_ref[...] = carry
    cnt_ref[...] = carry.astype(I32)


def _post(x, attn, yt, mod, w_glu, b_glu, ssm_norm, w_out, norm_ffn, w_router, b_router):
    b, s, d = x.shape
    nc = s // SSM_CHUNK
    ts = POST_POS * nc
    nt = SSM_CHUNK // POST_POS
    t = b * s
    pm = lambda bi, j: (bi, j, 0, 0)
    const = lambda bi, j: (0, 0)
    tok = lambda bi, j: (0, bi * nt + j)
    tri = (lax.broadcasted_iota(I32, (ts, ts), 0) < lax.broadcasted_iota(I32, (ts, ts), 1)).astype(BF16)
    col = lambda a: a.reshape(-1, 1)
    return pl.pallas_call(
        _post_kernel,
        grid=(b, nt),
        in_specs=[pl.BlockSpec(memory_space=pl.ANY),
                  pl.BlockSpec((1, POST_POS, nc, D_ATTN), pm),
                  pl.BlockSpec((1, N_GROUPS, POST_POS * SSM_GROUP, nc), lambda bi, j: (bi, 0, j, 0)),
                  pl.BlockSpec((1, 6, d), lambda bi, j: (bi, 0, 0)),
                  pl.BlockSpec((D_SSM, D_SSM), const),
                  pl.BlockSpec((D_SSM, 1), const),
                  pl.BlockSpec((D_SSM, 1), const),
                  pl.BlockSpec((d, d), const),
                  pl.BlockSpec((1, d), const),
                  pl.BlockSpec((N_EXPERTS, d), const),
                  pl.BlockSpec((N_EXPERTS, 1), const),
                  pl.BlockSpec((ts, ts), const)],
        out_specs=[pl.BlockSpec((1, POST_POS, nc, d), pm),
                   pl.BlockSpec((1, POST_POS, nc, d), pm),
                   pl.BlockSpec((TOP_K, ts), tok),
                   pl.BlockSpec((TOP_K, ts), tok),
                   pl.BlockSpec((TOP_K, ts), tok),
                   pl.BlockSpec((1, N_EXPERTS, 1), lambda bi, j: (bi * nt + j, 0, 0)),
                   pl.BlockSpec((N_EXPERTS, 1), const)],
        out_shape=[jax.ShapeDtypeStruct((b, SSM_CHUNK, nc, d), F32),
                   jax.ShapeDtypeStruct((b, SSM_CHUNK, nc, d), BF16),
                   jax.ShapeDtypeStruct((TOP_K, t), I32),
                   jax.ShapeDtypeStruct((TOP_K, t), F32),
                   jax.ShapeDtypeStruct((TOP_K, t), I32),
                   jax.ShapeDtypeStruct((b * nt, N_EXPERTS, 1), I32),
                   jax.ShapeDtypeStruct((N_EXPERTS, 1), I32)],
        scratch_shapes=[pltpu.VMEM((N_EXPERTS, 1), F32), pltpu.VMEM((2, POST_POS, nc, d), F32),
                        pltpu.SemaphoreType.DMA((2,))],
        compiler_params=_cparams(2),
        name="post",
    )(x.reshape(b, nc, SSM_CHUNK, d), attn, yt, mod, w_glu.T.astype(BF16), col(b_glu), col(ssm_norm), w_out.astype(BF16),
      norm_ffn.reshape(1, -1), w_router.T.astype(BF16), col(b_router), tri)


def _route_kernel(eidx_ref, lrank_ref, r0_ref, cnt_ref, ls_ref, tab_ref, te_ref, nv_ref, nx_ref, pad_ref):
    cnt = cnt_ref[...]
    tiles = (cnt + (RUN - 1 + FFN_ROWS - 1)) // FFN_ROWS
    er = lax.broadcasted_iota(I32, (N_EXPERTS, N_EXPERTS), 0)
    ec = lax.broadcasted_iota(I32, (N_EXPERTS, N_EXPERTS), 1)
    ltri = (ec < er).astype(BF16)

    def excl_cumsum(v):
        vb = jnp.broadcast_to(v.astype(F32), (N_EXPERTS, LANES)).astype(BF16)
        return jnp.dot(ltri, vb, preferred_element_type=F32)[:, 0:1].astype(I32)

    start_t = excl_cumsum(tiles)
    end_t = start_t + tiles
    start = start_t * FFN_ROWS
    pad_ref[...] = start + cnt

    nb = r0_ref.shape[0]
    ts = eidx_ref.shape[1] // nb
    iota_e = lax.broadcasted_iota(I32, (N_EXPERTS, ts), 0)
    iota_t = lax.broadcasted_iota(I32, (N_EXPERTS, TABW), 0)
    chunk = lax.broadcasted_iota(I32, (1, TABW), 1)

    def block(b, carry):
        lanes = pl.ds(pl.multiple_of(b * ts, ts), ts)
        sels = [iota_e == eidx_ref[k:k + 1, lanes] for k in range(TOP_K)]
        member = sels[0].astype(I32) + sels[1].astype(I32) + sels[2].astype(I32) + sels[3].astype(I32)
        nch = (jnp.sum(member, axis=1, keepdims=True) + (RUN - 1)) // RUN
        cb = excl_cumsum(nch)
        end_c = cb + nch
        for k in range(TOP_K):
            first = jnp.sum(jnp.where(sels[k], cb, 0), axis=0, keepdims=True)
            lr = lrank_ref[k:k + 1, lanes]
            ls_ref[k:k + 1, lanes] = (first + lax.shift_right_logical(lr, RUN_SHIFT)) * RUN + (lr & (RUN - 1))
        e_of_c = jnp.sum((chunk >= end_c).astype(I32), axis=0, keepdims=True)
        sel_c = iota_t == e_of_c
        first_c = jnp.sum(jnp.where(sel_c, cb, 0), axis=0, keepdims=True)
        slot0_c = jnp.sum(jnp.where(sel_c, start + r0_ref[b], 0), axis=0, keepdims=True)
        n_chunks = jnp.max(end_c, axis=0, keepdims=True)
        row = jnp.where(chunk < n_chunks, slot0_c + (chunk - first_c) * RUN, -1)
        tab_ref[pl.ds(b, 1), :] = jnp.where(chunk == TABW - 1, n_chunks, row)
        return carry

    lax.fori_loop(0, nb, block, 0)

    nv = jnp.max(end_t, axis=0, keepdims=True)
    width = te_ref.shape[1]
    ti = jnp.minimum(lax.broadcasted_iota(I32, (N_EXPERTS, width), 1), nv - 1)
    te = jnp.minimum(jnp.sum((ti >= end_t).astype(I32), axis=0, keepdims=True), N_EXPERTS - 1)
    te_ref[...] = te
    nv_ref[...] = jnp.broadcast_to(nv, nv_ref.shape)
    ie = lax.broadcasted_iota(I32, (N_EXPERTS, width), 0)
    own_end = jnp.sum(jnp.where(ie == te, end_t, 0), axis=0, keepdims=True)
    nxt = jnp.minimum(jnp.sum((own_end >= end_t).astype(I32), axis=0, keepdims=True), N_EXPERTS - 1)
    nx_ref[...] = jnp.where(own_end < nv, nxt, -1)


def _route(eidx, lrank, r0, cnt, n_tiles):
    t = eidx.shape[1]
    nb = r0.shape[0]
    width = -(-n_tiles // LANES) * LANES
    return pl.pallas_call(
        _route_kernel,
        out_shape=[jax.ShapeDtypeStruct((TOP_K, t), I32),
                   jax.ShapeDtypeStruct((nb, TABW), I32),
                   jax.ShapeDtypeStruct((1, width), I32),
                   jax.ShapeDtypeStruct((1, LANES), I32),
                   jax.ShapeDtypeStruct((1, width), I32),
                   jax.ShapeDtypeStruct((N_EXPERTS, 1), I32)],
        name="route",
    )(eidx, lrank, r0, cnt)


def _for_chunk_pairs(n, fn):
    def body(i, carry):
        fn(2 * i, 0)

        @pl.when(2 * i + 1 < n)
        def _():
            fn(2 * i + 1, 1)
        return carry
    lax.fori_loop(0, lax.shift_right_logical(n + 1, 1), body, 0)


def _local_rows(ts):
    return ts * TOP_K + N_EXPERTS * RUN


def _dispatch_kernel(tab_ref, pad_ref, nvt_ref, h_ref, ls_ref, xs_ref, buf, zbuf, sems, zsem):
    b = pl.program_id(0)
    slot = b % 2
    ts = h_ref.shape[0]
    local = buf.shape[2]

    def chunk_copy(sl, blk, c):
        rows = pl.ds(pl.multiple_of(c * RUN, RUN), RUN)
        return pltpu.make_async_copy(buf.at[sl, :, rows, :], xs_ref.at[:, pl.ds(tab_ref[blk * TABW + c], RUN), :],
                                     sems.at[sl])

    def for_chunks(blk, fn):
        _for_chunk_pairs(tab_ref[blk * TABW + TABW - 1], fn)

    @pl.when(b == 0)
    def _():
        zbuf[...] = jnp.zeros_like(zbuf)
        zrows = zbuf.shape[1]
        zero = lambda row: pltpu.make_async_copy(zbuf, xs_ref.at[:, pl.ds(row, zrows), :], zsem)
        for phase in range(3):
            for e in range(phase, N_EXPERTS, 3):
                zero(pad_ref[e]).start()
            for e in range(phase, N_EXPERTS, 3):
                zero(pad_ref[e]).wait()
        ztile = lambda i: pltpu.make_async_copy(zbuf.at[:, pl.ds(0, FFN_ROWS), :],
                                                xs_ref.at[:, pl.ds((nvt_ref[0] + i) * FFN_ROWS, FFN_ROWS), :], zsem)

        def tail_start(i, carry):
            ztile(i).start()
            return carry

        def tail_wait(i, carry):
            ztile(i).wait()
            return carry
        lax.fori_loop(0, nvt_ref[1] - nvt_ref[0], tail_start, 0)
        lax.fori_loop(0, nvt_ref[1] - nvt_ref[0], tail_wait, 0)

    r = lax.broadcasted_iota(I32, (local, ts), 0)
    hit = (r == ls_ref[0:1, :]) | (r == ls_ref[1:2, :]) | (r == ls_ref[2:3, :]) | (r == ls_ref[3:4, :])
    srt = jnp.dot(hit.astype(BF16), h_ref[...], preferred_element_type=F32)
    packed = _pack_exact_halves(srt)
    for pb in range(PANELS):
        buf[slot, pb] = packed[:, pb * LANES:(pb + 1) * LANES]

    @pl.when(b > 0)
    def _():
        for_chunks(b - 1, lambda c, p: chunk_copy(1 - slot, b - 1, c).wait())

    for_chunks(b, lambda c, p: chunk_copy(slot, b, c).start(priority=p))

    @pl.when(b == pl.num_programs(0) - 1)
    def _():
        for_chunks(b, lambda c, p: chunk_copy(slot, b, c).wait())


def _dispatch(tab, pad, nvt, h2, ls, n_rows):
    t, d = h2.shape
    nb = tab.shape[0] // TABW
    ts = t // nb
    return pl.pallas_call(
        _dispatch_kernel,
        grid_spec=pltpu.PrefetchScalarGridSpec(
            num_scalar_prefetch=3,
            grid=(nb,),
            in_specs=[pl.BlockSpec((ts, d), lambda i, *_: (i, 0)),
                      pl.BlockSpec((TOP_K, ts), lambda i, *_: (0, i))],
            out_specs=pl.BlockSpec(memory_space=pl.ANY),
            scratch_shapes=[pltpu.VMEM((2, PANELS, _local_rows(ts), LANES), U32),
                            pltpu.VMEM((PANELS, FFN_ROWS + RUN, LANES), U32),
                            pltpu.SemaphoreType.DMA((2,)), pltpu.SemaphoreType.DMA],
        ),
        out_shape=jax.ShapeDtypeStruct((PANELS, n_rows, LANES), U32),
        compiler_params=_cparams(1, ROW_VMEM_BYTES),
        name="dispatch",
    )(tab, pad, nvt, h2, ls)


def _ffn_kernel(te_ref, nv_ref, nx_ref, xs_ref, wgu_hbm, bg_ref, bu_ref, wd_hbm, bd_ref, perm_ref, ys_ref,
                wgu_stage, wd_stage, wg_scr, wu_scr, wd_scr, sems):
    i = pl.program_id(0)
    valid = i < nv_ref[0]
    new_expert = (i == 0) | (te_ref[i] != te_ref[jnp.maximum(i - 1, 0)])

    def stage_copies(e):
        return (pltpu.make_async_copy(wgu_hbm.at[e], wgu_stage, sems.at[0]),
                pltpu.make_async_copy(wd_hbm.at[e], wd_stage, sems.at[1]))

    @pl.when(valid & new_expert)
    def _():
        e = te_ref[i]

        @pl.when(i == 0)
        def _():
            for cp in stage_copies(e):
                cp.start()

        for cp in stage_copies(e):
            cp.wait()
        for c in range(2 * D_FF // PERM):
            w = wgu_stage[:, c * PERM:(c + 1) * PERM].astype(BF16)
            pw = jnp.dot(w, perm_ref[...], preferred_element_type=F32).astype(BF16)
            wg_scr[:, c * (PERM // 2):(c + 1) * (PERM // 2)] = pw[:, :PERM // 2]
            wu_scr[:, c * (PERM // 2):(c + 1) * (PERM // 2)] = pw[:, PERM // 2:]
        wd_scr[...] = wd_stage[...].astype(BF16)

        @pl.when(nx_ref[i] >= 0)
        def _():
            for cp in stage_copies(nx_ref[i]):
                cp.start()

    @pl.when(valid)
    def _():
        x_hi, x_lo = _unpack_halves(_load_panels(xs_ref))
        x_hi = x_hi.astype(BF16)
        x_lo = x_lo.astype(BF16)
        gate = (jnp.dot(x_hi, wg_scr[:HALF, :], preferred_element_type=F32)
                + jnp.dot(x_lo, wg_scr[HALF:, :], preferred_element_type=F32) + bg_ref[0])
        up = (jnp.dot(x_hi, wu_scr[:HALF, :], preferred_element_type=F32)
              + jnp.dot(x_lo, wu_scr[HALF:, :], preferred_element_type=F32) + bu_ref[0])
        gate = jnp.minimum(gate, SWIGLU_LIMIT)
        up = jnp.clip(up, -SWIGLU_LIMIT, SWIGLU_LIMIT)
        act = (up + 1.0) * (gate * jax.nn.sigmoid(SWIGLU_ALPHA * gate))
        y = jnp.dot(act.astype(BF16), wd_scr[...], preferred_element_type=F32) + bd_ref[0]
        _store_panels(ys_ref, _pack_halves(y))


def _ffn(te, nv, nx, xs, w_gate_up, bg, bu, w_down, bd, n_tiles):
    d = D_MODEL
    tile = lambda i, te, nv, nx: (0, jnp.minimum(i, nv[0] - 1), 0)
    wsel = lambda i, te, nv, nx: (te[i], 0, 0)
    r = lax.broadcasted_iota(I32, (PERM, PERM), 0)
    c = lax.broadcasted_iota(I32, (PERM, PERM), 1)
    perm = (r == jnp.where(c < PERM // 2, 2 * c, 2 * (c - PERM // 2) + 1)).astype(BF16)
    return pl.pallas_call(
        _ffn_kernel,
        grid_spec=pltpu.PrefetchScalarGridSpec(
            num_scalar_prefetch=3,
            grid=(n_tiles,),
            in_specs=[pl.BlockSpec((PANELS, FFN_ROWS, LANES), tile),
                      pl.BlockSpec(memory_space=pl.ANY),
                      pl.BlockSpec((1, 1, D_FF), wsel),
                      pl.BlockSpec((1, 1, D_FF), wsel),
                      pl.BlockSpec(memory_space=pl.ANY),
                      pl.BlockSpec((1, 1, d), wsel),
                      pl.BlockSpec((PERM, PERM), lambda i, te, nv, nx: (0, 0))],
            out_specs=pl.BlockSpec((PANELS, FFN_ROWS, LANES), tile),
            scratch_shapes=[pltpu.VMEM((d, 2 * D_FF), F32), pltpu.VMEM((D_FF, d), F32),
                            pltpu.VMEM((d, D_FF), BF16), pltpu.VMEM((d, D_FF), BF16), pltpu.VMEM((D_FF, d), BF16),
                            pltpu.SemaphoreType.DMA((2,))],
        ),
        out_shape=jax.ShapeDtypeStruct(xs.shape, U32),
        input_output_aliases={3: 0},
        compiler_params=_cparams(1, FFN_VMEM_BYTES),
        name="ffn",
    )(te, nv, nx, xs, w_gate_up, bg, bu, w_down, bd, perm)


def _combine_kernel(tab_ref, x1_ref, ls_ref, w_ref, mod_ref, ys_ref, o4_hbm, ybuf, ob_buf, sems, osems, *, n_blk):
    jj = pl.program_id(1)
    blk = pl.program_id(0) * pl.num_programs(1) + jj
    slot = blk % 2
    nc = x1_ref.shape[2]
    tt = POST_POS * nc
    d = x1_ref.shape[3]
    local = ybuf.shape[2]

    def chunk_copy(sl, bk, c):
        rows = pl.ds(pl.multiple_of(c * RUN, RUN), RUN)
        return pltpu.make_async_copy(ys_ref.at[:, pl.ds(tab_ref[bk * TABW + c], RUN), :], ybuf.at[sl, :, rows, :],
                                     sems.at[sl])

    def for_chunks(bk, fn):
        _for_chunk_pairs(tab_ref[bk * TABW + TABW - 1], fn)

    @pl.when(blk == 0)
    def _():
        ybuf[...] = jnp.zeros_like(ybuf)
        for_chunks(0, lambda c, p: chunk_copy(0, 0, c).start(priority=p))

    @pl.when(blk + 1 < n_blk)
    def _():
        for_chunks(blk + 1, lambda c, p: chunk_copy(1 - slot, blk + 1, c).start(priority=p))

    for_chunks(blk, lambda c, p: chunk_copy(slot, blk, c).wait())

    r = lax.broadcasted_iota(I32, (tt, local), 1)
    wm = jnp.zeros((tt, local), F32)
    for k in range(TOP_K):
        wm = wm + jnp.where(r == ls_ref[:, k:k + 1], w_ref[:, k:k + 1], 0.0)
    wm = wm.astype(BF16)
    halves = [_unpack_halves(ybuf[slot, pb]) for pb in range(PANELS)]
    y_hi = jnp.concatenate([h.astype(BF16) for h, _ in halves], axis=-1)
    y_lo = jnp.concatenate([l.astype(BF16) for _, l in halves], axis=-1)
    acc_hi = jnp.dot(wm, y_hi, preferred_element_type=F32)
    acc_lo = jnp.dot(wm, y_lo, preferred_element_type=F32)
    g2 = mod_ref[0, 5:6, :]
    x1 = x1_ref[0].reshape(tt, d)
    out = jnp.concatenate([x1[:, :HALF] + g2[:, :HALF] * acc_hi, x1[:, HALF:] + g2[:, HALF:] * acc_lo], axis=-1)

    def out_copies(sl, b_, j_):
        return [pltpu.make_async_copy(ob_buf.at[sl, il], o4_hbm.at[b_, :, POST_POS * j_ + il, :], osems.at[sl])
                for il in range(POST_POS)]

    @pl.when(blk >= 2)
    def _():
        for cp in out_copies(slot, 0, 0):
            cp.wait()

    for il in range(POST_POS):
        ob_buf[slot, il] = out[il * nc:(il + 1) * nc]
    for cp in out_copies(slot, pl.program_id(0), jj):
        cp.start()

    @pl.when(blk == n_blk - 1)
    def _():
        for cp in out_copies(slot, 0, 0):
            cp.wait()
        if n_blk > 1:
            for cp in out_copies(1 - slot, 0, 0):
                cp.wait()


def _combine(tab, x1, ls_t, wts_t, mod, ys):
    b, _, nc, d = x1.shape
    s = SSM_CHUNK * nc
    tt = POST_POS * nc
    nt = SSM_CHUNK // POST_POS
    o4 = pl.pallas_call(
        functools.partial(_combine_kernel, n_blk=b * nt),
        grid_spec=pltpu.PrefetchScalarGridSpec(
            num_scalar_prefetch=1,
            grid=(b, nt),
            in_specs=[pl.BlockSpec((1, POST_POS, nc, d), lambda bi, j, *_: (bi, j, 0, 0)),
                      pl.BlockSpec((tt, TOP_K), lambda bi, j, *_: (bi * nt + j, 0)),
                      pl.BlockSpec((tt, TOP_K), lambda bi, j, *_: (bi * nt + j, 0)),
                      pl.BlockSpec((1, 6, d), lambda bi, j, *_: (bi, 0, 0)),
                      pl.BlockSpec(memory_space=pl.ANY)],
            out_specs=pl.BlockSpec(memory_space=pl.ANY),
            scratch_shapes=[pltpu.VMEM((2, PANELS, _local_rows(tt), LANES), U32),
                            pltpu.VMEM((2, POST_POS, nc, d), F32),
                            pltpu.SemaphoreType.DMA((2,)), pltpu.SemaphoreType.DMA((2,))],
        ),
        out_shape=jax.ShapeDtypeStruct((b, nc, SSM_CHUNK, d), F32),
        compiler_params=_cparams(2),
        name="combine",
    )(tab, x1, ls_t, wts_t, mod, ys)
    return o4.reshape(b, s, d)


def kernel(x, c, w_ada, b_ada, norm_mix, w_in, b_in, q_norm, k_norm, sinks, lam_re, lam_im, log_dt, b_re, b_im,
           c_re, c_im, d_skip, w_glu, b_glu, attn_out_norm, ssm_out_norm, w_out, norm_ffn, w_router, b_router,
           w_gate_up, b_gate_up, w_down, b_down):
    b, s, d = x.shape
    t = b * s
    depth = w_ada.shape[0]
    n_tiles = -(-(t * TOP_K + N_EXPERTS * (RUN - 1 + FFN_ROWS - 1)) // FFN_ROWS)
    n_alloc = n_tiles + 2
    for l in range(depth):
        mod = _adaln(c, w_ada[l], b_ada[l]).reshape(b, 6, d)
        q, k, v, ut = _inproj(x, mod, norm_mix[l], w_in[l], b_in[l])
        attn = _attention(q, k, v, sinks[l], q_norm[l], k_norm[l], attn_out_norm[l])
        tt, wz, wyt, cs = _ssm_params(lam_re[l], lam_im[l], log_dt[l], b_re[l], b_im[l], c_re[l], c_im[l])
        yt = _ssm(ut, tt, wz, wyt, cs, d_skip[l])
        x1, h2, eidx, wts, lrank, r0, cnt = _post(x, attn, yt, mod, w_glu[l], b_glu[l], ssm_out_norm[l], w_out[l],
                                                  norm_ffn[l], w_router[l], b_router[l])
        ls, tab, te, nv, nx, pad = _route(eidx, lrank, r0, cnt, n_tiles)
        tab = tab.reshape(-1)
        nvt = jnp.stack([nv[0, 0], jnp.int32(n_alloc)])
        xs = _dispatch(tab, pad.reshape(-1), nvt, h2.reshape(t, d), ls, n_alloc * FFN_ROWS)
        wgu = w_gate_up[l]
        bgu = b_gate_up[l]
        ys = _ffn(te[0, :n_tiles], nv[0, :1], nx[0, :n_tiles], xs, wgu, bgu[:, None, 0::2], bgu[:, None, 1::2],
                  w_down[l], b_down[l][:, None, :], n_tiles)
        x = _combine(tab, x1, ls.T, wts.T, mod, ys)
    return x
```

```python
import functools
import math

import jax
import jax.numpy as jnp
from jax import lax
from jax.experimental import pallas as pl
from jax.experimental.pallas import tpu as pltpu

F32 = jnp.float32
BF16 = jnp.bfloat16
U32 = jnp.uint32
I32 = jnp.int32

D_MODEL = 1024
HEAD_DIM = 64
N_HEADS = 8
N_KV_HEADS = 2
Q_PER_KV = N_HEADS // N_KV_HEADS
D_ATTN = N_HEADS * HEAD_DIM
D_KV = N_KV_HEADS * HEAD_DIM
D_QKV = D_ATTN + 2 * D_KV
WINDOW = 128
BLOCK = 128
D_SSM = D_MODEL - D_ATTN
SSM_GROUP = 16
N_GROUPS = D_SSM // SSM_GROUP
STATE = 64
N_EXPERTS = 32
TOP_K = 4
D_FF = D_MODEL
SWIGLU_LIMIT = 7.0
SWIGLU_ALPHA = 1.702
EPS = 1e-6
NEG_INF = -1e30

LANES = 128
SSM_CHUNK = 16
SSM_ROW = SSM_CHUNK * SSM_GROUP
N_POW = 2 * SSM_CHUNK
PANEL_COLS = 2 * LANES
PANELS = D_MODEL // PANEL_COLS

POS_PER_STEP = 4
ATTN_ROWS = 512
POST_POS = 2
FFN_ROWS = 256
RUN = 16
RUN_SHIFT = 4
TABW = 128
PERM = 256
FFN_VMEM_BYTES = 40 * 1024 * 1024
ROW_VMEM_BYTES = 48 * 1024 * 1024

HIGHEST = lax.Precision.HIGHEST
_ARB = "arbitrary"


def _cparams(n, vmem=None):
    return pltpu.CompilerParams(dimension_semantics=(_ARB,) * n, vmem_limit_bytes=vmem)


def _rms(x, axis=-1):
    return x * lax.rsqrt(jnp.mean(x * x, axis=axis, keepdims=True) + EPS)


def _pack_panel(y, exact=False):
    hi, lo = y[:, :LANES], y[:, LANES:]
    if not exact:
        hi = hi.astype(BF16).astype(F32)
        lo = lo.astype(BF16).astype(F32)
    return lax.bitcast_convert_type(hi, U32) | (lax.bitcast_convert_type(lo, U32) >> 16)


def _unpack_panels(words):
    cols = []
    for w in words:
        cols.append(lax.bitcast_convert_type(w & jnp.uint32(0xFFFF0000), F32).astype(BF16))
        cols.append(lax.bitcast_convert_type(w << 16, F32).astype(BF16))
    return jnp.concatenate(cols, axis=-1)


def _prefetch_pos_rows(x4_hbm, buf, sems, n_pos):
    bi = pl.program_id(0)
    j = pl.program_id(1)
    nj = pl.num_programs(1)
    g = bi * nj + j
    slot = g % 2

    def copies(sl, b_, j_):
        return [pltpu.make_async_copy(x4_hbm.at[b_, :, n_pos * j_ + il, :], buf.at[sl, il], sems.at[sl])
                for il in range(n_pos)]

    @pl.when(g == 0)
    def _():
        for cp in copies(0, 0, 0):
            cp.start()

    @pl.when(g + 1 < pl.num_programs(0) * nj)
    def _():
        wrap = j + 1 == nj
        for cp in copies(1 - slot, jnp.where(wrap, bi + 1, bi), jnp.where(wrap, 0, j + 1)):
            cp.start()

    for cp in copies(slot, bi, j):
        cp.wait()
    return slot


def _to_lane_blocks(dst, src):
    for kb in range(dst.shape[0]):
        dst[kb] = src[:, kb * LANES:(kb + 1) * LANES]


def _adaln_kernel(c_ref, w_ref, b_ref, o_ref):
    c = c_ref[...]
    ca = c * jax.nn.sigmoid(c)
    o_ref[...] = jnp.dot(ca, w_ref[...], preferred_element_type=F32, precision=HIGHEST) + b_ref[...]


def _adaln(c, w_ada, b_ada):
    b, d = c.shape
    n = w_ada.shape[1] // d
    return pl.pallas_call(
        _adaln_kernel,
        grid=(n,),
        in_specs=[pl.BlockSpec((b, d), lambda j: (0, 0)),
                  pl.BlockSpec((d, d), lambda j: (0, j)),
                  pl.BlockSpec((1, d), lambda j: (0, j))],
        out_specs=pl.BlockSpec((b, d), lambda j: (0, j)),
        out_shape=jax.ShapeDtypeStruct((b, n * d), F32),
        compiler_params=_cparams(1),
        name="adaln",
    )(c, w_ada, b_ada.reshape(1, -1))


def _inproj_kernel(x4_hbm, x_ref, mod_ref, g_ref, wqkv_ref, bqkv_ref, wut_ref, but_ref, q_ref, k_ref, v_ref, ut_ref,
                   xp_buf, sems):
    nc = ut_ref.shape[3]
    slot = _prefetch_pos_rows(x4_hbm, xp_buf, sems, POS_PER_STEP)
    gain = g_ref[...]
    scale = 1.0 + mod_ref[0, 1:2, :]
    shift = mod_ref[0, 0:1, :]

    def norm_mod(x):
        return (_rms(x) * gain * scale + shift).astype(BF16)

    proj = jnp.dot(norm_mod(x_ref[0]), wqkv_ref[...], preferred_element_type=F32) + bqkv_ref[...]
    q_ref[0] = proj[:, :D_ATTN].astype(BF16)
    k_ref[0] = proj[:, D_ATTN:D_ATTN + D_KV].astype(BF16)
    v_ref[0] = proj[:, D_ATTN + D_KV:].astype(BF16)

    hs = jnp.concatenate([norm_mod(xp_buf[slot, il]) for il in range(POS_PER_STEP)], axis=0)
    ut = lax.dot_general(wut_ref[...], hs, (((1,), (1,)), ((), ())), preferred_element_type=F32) + but_ref[...]
    for il in range(POS_PER_STEP):
        piece = ut[:, il * nc:(il + 1) * nc].astype(BF16)
        ut_ref[0, :, il * SSM_GROUP:(il + 1) * SSM_GROUP, :] = piece.reshape(N_GROUPS, SSM_GROUP, nc)


def _inproj(x, mod, gain, w_in, b_in):
    b, s, d = x.shape
    nc = s // SSM_CHUNK
    rows = POS_PER_STEP * nc
    row = lambda bi, j: (bi, j, 0)
    const = lambda bi, j: (0, 0)
    w_qkv = w_in[:, :D_QKV].astype(BF16)
    w_ut = w_in[:, D_QKV:].T.astype(BF16)
    return pl.pallas_call(
        _inproj_kernel,
        grid=(b, SSM_CHUNK // POS_PER_STEP),
        in_specs=[pl.BlockSpec(memory_space=pl.ANY),
                  pl.BlockSpec((1, rows, d), row),
                  pl.BlockSpec((1, 6, d), lambda bi, j: (bi, 0, 0)),
                  pl.BlockSpec((1, d), const),
                  pl.BlockSpec((d, D_QKV), const),
                  pl.BlockSpec((1, D_QKV), const),
                  pl.BlockSpec((D_SSM, d), const),
                  pl.BlockSpec((D_SSM, 1), const)],
        out_specs=[pl.BlockSpec((1, rows, D_ATTN), row),
                   pl.BlockSpec((1, rows, D_KV), row),
                   pl.BlockSpec((1, rows, D_KV), row),
                   pl.BlockSpec((1, N_GROUPS, POS_PER_STEP * SSM_GROUP, nc), lambda bi, j: (bi, 0, j, 0))],
        out_shape=[jax.ShapeDtypeStruct((b, s, D_ATTN), BF16),
                   jax.ShapeDtypeStruct((b, s, D_KV), BF16),
                   jax.ShapeDtypeStruct((b, s, D_KV), BF16),
                   jax.ShapeDtypeStruct((b, N_GROUPS, SSM_ROW, nc), BF16)],
        scratch_shapes=[pltpu.VMEM((2, POS_PER_STEP, nc, d), F32), pltpu.SemaphoreType.DMA((2,))],
        compiler_params=_cparams(2),
        name="inproj",
    )(x.reshape(b, nc, SSM_CHUNK, d), x, mod, gain.reshape(1, d), w_qkv, b_in[:D_QKV].reshape(1, D_QKV), w_ut, b_in[D_QKV:].reshape(D_SSM, 1))


def _half_norm(x, low):
    sq = x * x
    s_lo = jnp.sum(jnp.where(low, sq, 0.0), axis=-1, keepdims=True)
    s_hi = jnp.sum(sq, axis=-1, keepdims=True) - s_lo
    inv = 1.0 / HEAD_DIM
    scale = jnp.where(low, lax.rsqrt(s_lo * inv + EPS), lax.rsqrt(s_hi * inv + EPS))
    return x * scale


def _attn_block(first, q, k_prev, k_cur, v_prev, v_cur, sinks_ref, qn, low, upper, rblk):
    no_prev = jnp.where(first, NEG_INF, 0.0)
    out_blocks = []
    for hk in range(N_KV_HEADS):
        qs = []
        for j in range(Q_PER_KV // 2):
            blk = hk * (Q_PER_KV // 2) + j
            qb = _half_norm(q[:, blk * LANES:(blk + 1) * LANES], low) * qn * (1.0 / math.sqrt(HEAD_DIM))
            qs.append(jnp.where(low, qb, 0.0))
            qs.append(jnp.where(low, 0.0, qb))
        qg = jnp.concatenate(qs, axis=0).astype(BF16)
        nt = (((1,), (1,)), ((), ()))
        s_prev = lax.dot_general(qg, k_prev[hk], nt, preferred_element_type=F32)
        s_cur = lax.dot_general(qg, k_cur[hk], nt, preferred_element_type=F32)
        s = jnp.where(upper, s_prev + no_prev, s_cur)
        sink = jnp.zeros((Q_PER_KV * BLOCK, 1), F32)
        for g in range(Q_PER_KV):
            sink = jnp.where(rblk == g, sinks_ref[hk * Q_PER_KV + g], sink)
        m = jnp.maximum(jnp.max(s, axis=-1, keepdims=True), sink)
        p = jnp.exp(s - m)
        den = jnp.sum(p, axis=-1, keepdims=True) + jnp.exp(sink - m)
        o = (jnp.dot(jnp.where(upper, p, 0.0).astype(BF16), v_prev[hk], preferred_element_type=F32)
             + jnp.dot(jnp.where(upper, 0.0, p).astype(BF16), v_cur[hk], preferred_element_type=F32)) / den
        for j in range(Q_PER_KV // 2):
            ev = o[(2 * j) * BLOCK:(2 * j + 1) * BLOCK]
            od = o[(2 * j + 1) * BLOCK:(2 * j + 2) * BLOCK]
            out_blocks.append(jnp.where(low, ev, od))
    return jnp.concatenate(out_blocks, axis=-1)


def _attn_kernel(sinks_ref, q_ref, k_ref, v_ref, qn_ref, kn_ref, on_ref, o_hbm, a_buf, sems, *, n_steps):
    step = pl.program_id(1)
    g = pl.program_id(0) * pl.num_programs(1) + step
    slot = g % 2
    cps = ATTN_ROWS // SSM_CHUNK
    nq = ATTN_ROWS // BLOCK

    def out_copies(sl, b_, s_):
        return [pltpu.make_async_copy(a_buf.at[sl, :, i, :], o_hbm.at[b_, i, pl.ds(s_ * cps, cps), :], sems.at[sl])
                for i in range(SSM_CHUNK)]

    @pl.when(g >= 2)
    def _():
        for cp in out_copies(slot, 0, 0):
            cp.wait()

    low = lax.broadcasted_iota(I32, (1, LANES), 1) < HEAD_DIM
    rows = Q_PER_KV * BLOCK
    upper = lax.broadcasted_iota(I32, (rows, BLOCK), 1) > lax.broadcasted_iota(I32, (rows, BLOCK), 0) % BLOCK
    rblk = lax.broadcasted_iota(I32, (rows, 1), 0) // BLOCK

    cur = pl.multiple_of(step * ATTN_ROWS, ATTN_ROWS)
    prev = pl.multiple_of(jnp.maximum(step * nq - 1, 0) * BLOCK, BLOCK)
    kall = jnp.concatenate([k_ref[0, pl.ds(prev, BLOCK), :], k_ref[0, pl.ds(cur, ATTN_ROWS), :]], axis=0).astype(F32)
    vall = jnp.concatenate([v_ref[0, pl.ds(prev, BLOCK), :], v_ref[0, pl.ds(cur, ATTN_ROWS), :]], axis=0).astype(F32)
    kall = _half_norm(kall, low) * kn_ref[...]
    kswap = pltpu.roll(kall, HEAD_DIM, axis=1)
    vswap = pltpu.roll(vall, HEAD_DIM, axis=1)
    k_dup = [jnp.where(low, kall, kswap).astype(BF16), jnp.where(low, kswap, kall).astype(BF16)]
    v_dup = [jnp.where(low, vall, vswap).astype(BF16), jnp.where(low, vswap, vall).astype(BF16)]
    blk = lambda a, i: [a[hk][i * BLOCK:(i + 1) * BLOCK] for hk in range(N_KV_HEADS)]

    for qb in range(nq):
        q = q_ref[0, qb * BLOCK:(qb + 1) * BLOCK, :].astype(F32)
        attn = _attn_block((step == 0) if qb == 0 else False, q, blk(k_dup, qb), blk(k_dup, qb + 1),
                           blk(v_dup, qb), blk(v_dup, qb + 1), sinks_ref, qn_ref[...], low, upper, rblk)
        attn = _rms(attn) * on_ref[...]
        cpb = BLOCK // SSM_CHUNK
        a_buf[slot, qb * cpb:(qb + 1) * cpb] = attn.reshape(cpb, SSM_CHUNK, D_ATTN)

    for cp in out_copies(slot, pl.program_id(0), step):
        cp.start()

    @pl.when(g == n_steps - 1)
    def _():
        for cp in out_copies(slot, 0, 0):
            cp.wait()
        if n_steps > 1:
            for cp in out_copies(1 - slot, 0, 0):
                cp.wait()


def _attention(q, k, v, sinks, q_norm, k_norm, out_norm):
    b, s, _ = q.shape
    tile2 = lambda g: jnp.tile(g.reshape(1, HEAD_DIM), (1, 2))
    cps = ATTN_ROWS // SSM_CHUNK
    return pl.pallas_call(
        functools.partial(_attn_kernel, n_steps=b * (s // ATTN_ROWS)),
        grid=(b, s // ATTN_ROWS),
        in_specs=[pl.BlockSpec(memory_space=pltpu.SMEM),
                  pl.BlockSpec((1, ATTN_ROWS, D_ATTN), lambda bi, n: (bi, n, 0)),
                  pl.BlockSpec((1, s, D_KV), lambda bi, n: (bi, 0, 0)),
                  pl.BlockSpec((1, s, D_KV), lambda bi, n: (bi, 0, 0)),
                  pl.BlockSpec((1, LANES), lambda bi, n: (0, 0)),
                  pl.BlockSpec((1, LANES), lambda bi, n: (0, 0)),
                  pl.BlockSpec((1, D_ATTN), lambda bi, n: (0, 0))],
        out_specs=pl.BlockSpec(memory_space=pl.ANY),
        out_shape=jax.ShapeDtypeStruct((b, SSM_CHUNK, s // SSM_CHUNK, D_ATTN), F32),
        scratch_shapes=[pltpu.VMEM((2, cps, SSM_CHUNK, D_ATTN), F32), pltpu.SemaphoreType.DMA((2,))],
        compiler_params=_cparams(2),
        name="attention",
    )(sinks, q, k, v, tile2(q_norm), tile2(k_norm), out_norm.reshape(1, D_ATTN))


def _cmul(ar, ai, br, bi):
    return ar * br - ai * bi, ar * bi + ai * br


def _ssm_param_kernel(lam_ref, bre_ref, bim_ref, cre_ref, cim_ref, tt_ref, wz_ref, wyt_ref, cs_ref):
    f32dot = functools.partial(jnp.dot, preferred_element_type=F32, precision=HIGHEST)
    lr = lam_ref[0, 0:1, :]
    li = lam_ref[0, 1:2, :]
    dt = jnp.exp(lam_ref[0, 2:3, :])
    rho = lr * dt
    th = li * dt
    imag_lane = lax.broadcasted_iota(I32, (1, LANES), 1) >= STATE

    kk = (lax.broadcasted_iota(I32, (N_POW, 1), 0) - (SSM_CHUNK - 1)).astype(F32)
    mag = jnp.exp(rho * kk)
    pw_r = mag * jnp.cos(th * kk)
    pw_i = mag * jnp.sin(th * kk)
    lb_r = pw_r[SSM_CHUNK:SSM_CHUNK + 1]
    lb_i = pw_i[SSM_CHUNK:SSM_CHUNK + 1]
    den = lr * lr + li * li
    coef_r = ((lb_r - 1.0) * lr + lb_i * li) / den
    coef_i = (lb_i * lr - (lb_r - 1.0) * li) / den

    eye = (lax.broadcasted_iota(I32, (SSM_GROUP, SSM_GROUP), 0)
           == lax.broadcasted_iota(I32, (SSM_GROUP, SSM_GROUP), 1)).astype(F32)
    lane_fold = (lax.broadcasted_iota(I32, (STATE, LANES), 1) % STATE
                 == lax.broadcasted_iota(I32, (STATE, LANES), 0)).astype(F32)

    def tile_pos(x):
        return jnp.concatenate([x] * SSM_CHUNK, axis=0)

    def power_rows(k_of_pos):
        idx = [k_of_pos(p) + (SSM_CHUNK - 1) for p in range(SSM_CHUNK)]
        rep = lambda t: jnp.concatenate([jnp.broadcast_to(t[r:r + 1], (SSM_GROUP, LANES)) for r in idx], axis=0)
        return rep(pw_r), rep(pw_i)

    def b_rows(b_ref):
        b2 = jnp.concatenate([b_ref[0], b_ref[0]], axis=0)
        return tile_pos(lax.dot_general(eye, b2, (((1,), (1,)), ((), ())), preferred_element_type=F32,
                                        precision=HIGHEST))

    def c_rows(c_ref):
        return tile_pos(f32dot(c_ref[0], lane_fold))

    bbar_r, bbar_i = _cmul(coef_r, coef_i, b_rows(bre_ref), b_rows(bim_ref))
    c_r = c_rows(cre_ref)
    c_i = c_rows(cim_ref)

    a_r, a_i = _cmul(bbar_r, bbar_i, *power_rows(lambda p: -p))
    a2c = jnp.where(imag_lane, -a_i, a_r)
    m_r, m_i = _cmul(c_r, c_i, *power_rows(lambda p: p))
    bmc = jnp.where(imag_lane, m_i, m_r)
    tt = f32dot(bmc, a2c.T)
    causal = (lax.broadcasted_iota(I32, (SSM_ROW, 1), 0) // SSM_GROUP
              >= lax.broadcasted_iota(I32, (1, SSM_ROW), 1) // SSM_GROUP)
    tt_ref[0] = jnp.where(causal, tt, 0.0).astype(BF16)

    w_r, w_i = _cmul(bbar_r, bbar_i, *power_rows(lambda p: SSM_CHUNK - 1 - p))
    wz_ref[0, :, :LANES] = jnp.where(imag_lane, w_i, w_r).astype(BF16)
    wz_ref[0, :, LANES:] = jnp.where(imag_lane, w_r, w_i).astype(BF16)

    y_r, y_i = _cmul(c_r, c_i, *power_rows(lambda p: p + 1))
    wyt_ref[0] = jnp.where(imag_lane, -y_i, y_r).astype(BF16)

    cs_ref[0, 0:1, :] = pw_r[N_POW - 1:N_POW]
    cs_ref[0, 1:2, :] = jnp.where(imag_lane, pw_i[N_POW - 1:N_POW], -pw_i[N_POW - 1:N_POW])


def _ssm_params(lam_re, lam_im, log_dt, b_re, b_im, c_re, c_im):
    g = lam_re.shape[0]
    lam = jnp.stack([lam_re, lam_im, jnp.broadcast_to(log_dt[:, None], (g, STATE))], axis=1)
    lam = jnp.concatenate([lam, lam], axis=2)
    blk = lambda *shape: pl.BlockSpec((1,) + shape, lambda i: (i, 0, 0))
    return pl.pallas_call(
        _ssm_param_kernel,
        grid=(g,),
        in_specs=[blk(3, LANES), blk(STATE, SSM_GROUP), blk(STATE, SSM_GROUP), blk(SSM_GROUP, STATE),
                  blk(SSM_GROUP, STATE)],
        out_specs=[blk(SSM_ROW, SSM_ROW), blk(SSM_ROW, SSM_ROW), blk(SSM_ROW, LANES), blk(2, LANES)],
        out_shape=[jax.ShapeDtypeStruct((g, SSM_ROW, SSM_ROW), BF16),
                   jax.ShapeDtypeStruct((g, SSM_ROW, SSM_ROW), BF16),
                   jax.ShapeDtypeStruct((g, SSM_ROW, LANES), BF16),
                   jax.ShapeDtypeStruct((g, 2, LANES), F32)],
        compiler_params=_cparams(1),
        name="ssm_params",
    )(lam, b_re, b_im, c_re, c_im)


def _ssm_kernel(ut_ref, tt_ref, wz_ref, wyt_ref, cs_ref, d_ref, yt_ref, z_scr, s_scr):
    batch, _, _, nc = ut_ref.shape
    ut = jnp.concatenate([ut_ref[b, 0] for b in range(batch)], axis=1)
    _to_lane_blocks(z_scr, lax.dot_general(ut, wz_ref[0], (((0,), (0,)), ((), ())), preferred_element_type=F32))
    c1 = cs_ref[0, 0:1, :]
    c2 = cs_ref[0, 1:2, :]

    def step(c, carry):
        s1, s2 = carry
        rows = pl.ds(c, batch, stride=nc)
        s_scr[rows, :] = s1
        n1 = c1 * s1 + c2 * s2 + z_scr[0, rows, :]
        n2 = c1 * s2 - c2 * s1 + z_scr[1, rows, :]
        return n1, n2

    zero = jnp.zeros((batch, LANES), F32)
    lax.fori_loop(0, nc, step, (zero, zero), unroll=8)
    y = jnp.dot(tt_ref[0], ut, preferred_element_type=F32)
    y = y + lax.dot_general(wyt_ref[0], s_scr[...].astype(BF16), (((1,), (1,)), ((), ())),
                            preferred_element_type=F32)
    y = y + d_ref[0] * ut.astype(F32)
    for b in range(batch):
        yt_ref[b, 0] = y[:, b * nc:(b + 1) * nc]


def _ssm(ut, tt, wz, wyt, cs, d_skip):
    b, g, _, nc = ut.shape
    d_col = jnp.tile(d_skip.reshape(g, 1, SSM_GROUP), (1, SSM_CHUNK, 1)).reshape(g, SSM_ROW, 1)
    blk = lambda *shape: pl.BlockSpec((1,) + shape, lambda i: (i, 0, 0))
    act = pl.BlockSpec((b, 1, SSM_ROW, nc), lambda i: (0, i, 0, 0))
    return pl.pallas_call(
        _ssm_kernel,
        grid=(g,),
        in_specs=[act, blk(SSM_ROW, SSM_ROW), blk(SSM_ROW, SSM_ROW), blk(SSM_ROW, LANES), blk(2, LANES),
                  blk(SSM_ROW, 1)],
        out_specs=act,
        out_shape=jax.ShapeDtypeStruct((b, g, SSM_ROW, nc), F32),
        scratch_shapes=[pltpu.VMEM((SSM_ROW // LANES, b * nc, LANES), F32), pltpu.VMEM((b * nc, LANES), F32)],
        compiler_params=_cparams(1),
        name="ssm",
    )(ut, tt, wz, wyt, cs, d_col)


def _post_kernel(x4_hbm, attn_ref, yt_ref, mod_ref, wglut_ref, bglu_ref, sn_ref, wout_ref, nf_ref, wr_ref, br_ref,
                 tri_ref, x1_ref, h2_ref, eidx_ref, wts_ref, lrank_ref, r0_ref, cnt_ref, carry_ref, xp_buf, sems):
    @pl.when((pl.program_id(0) == 0) & (pl.program_id(1) == 0))
    def _():
        carry_ref[...] = jnp.zeros_like(carry_ref)

    slot = _prefetch_pos_rows(x4_hbm, xp_buf, sems, POST_POS)
    nc = attn_ref.shape[2]
    ts = POST_POS * nc
    d = x1_ref.shape[3]
    yt = jnp.concatenate(
        [yt_ref[0, :, il * SSM_GROUP:(il + 1) * SSM_GROUP, :].reshape(D_SSM, nc) for il in range(POST_POS)], axis=1)
    g = jax.nn.gelu(yt)
    gate = jax.nn.sigmoid(jnp.dot(wglut_ref[...], g.astype(BF16), preferred_element_type=F32) + bglu_ref[...])
    ssm_t = _rms(g * gate, axis=0) * sn_ref[...]
    mixed = jnp.concatenate([attn_ref[0].reshape(ts, D_ATTN).astype(BF16), ssm_t.T.astype(BF16)], axis=-1)
    o = jnp.dot(mixed, wout_ref[...], preferred_element_type=F32)
    x = jnp.concatenate([xp_buf[slot, il] for il in range(POST_POS)], axis=0)
    x1 = x + mod_ref[0, 2:3, :] * o
    x1_ref[0] = x1.reshape(POST_POS, nc, d)
    h2 = _rms(x1) * nf_ref[...] * (1.0 + mod_ref[0, 4:5, :]) + mod_ref[0, 3:4, :]
    h2_ref[0] = h2.astype(BF16).reshape(POST_POS, nc, d)

    logits = lax.dot_general(wr_ref[...], h2.astype(BF16), (((1,), (1,)), ((), ())),
                             preferred_element_type=F32) + br_ref[...]
    iota_e = lax.broadcasted_iota(I32, (N_EXPERTS, ts), 0).astype(F32)
    l = logits
    idxs, vals = [], []
    for _ in range(TOP_K):
        m = jnp.max(l, axis=0, keepdims=True)
        idx = jnp.min(jnp.where(l == m, iota_e, float(N_EXPERTS)), axis=0, keepdims=True)
        idxs.append(idx)
        vals.append(m)
        l = jnp.where(iota_e == idx, -jnp.inf, l)
    es = [jnp.exp(v - vals[0]) for v in vals]
    tot = es[0] + es[1] + es[2] + es[3]
    member = jnp.zeros((N_EXPERTS, ts), F32)
    for idx in idxs:
        member = member + (iota_e == idx).astype(F32)
    before = jnp.dot(member.astype(BF16), tri_ref[...], preferred_element_type=F32)
    for k in range(TOP_K):
        eidx_ref[k:k + 1, :] = idxs[k].astype(I32)
        wts_ref[k:k + 1, :] = es[k] / tot
        lrank_ref[k:k + 1, :] = jnp.sum(jnp.where(iota_e == idxs[k], before, 0.0), axis=0, keepdims=True).astype(I32)
    r0_ref[0] = carry_ref[...].astype(I32)
    carry = carry_ref[...] + jnp.sum(member, axis=1, keepdims=True)
    carry_ref[...] = carry
    cnt_ref[...] = carry.astype(I32)


def _post(x, attn, yt, mod, w_glu, b_glu, ssm_norm, w_out, norm_ffn, w_router, b_router):
    b, s, d = x.shape
    nc = s // SSM_CHUNK
    ts = POST_POS * nc
    nt = SSM_CHUNK // POST_POS
    t = b * s
    pm = lambda bi, j: (bi, j, 0, 0)
    const = lambda bi, j: (0, 0)
    tok = lambda bi, j: (0, bi * nt + j)
    tri = (lax.broadcasted_iota(I32, (ts, ts), 0) < lax.broadcasted_iota(I32, (ts, ts), 1)).astype(BF16)
    col = lambda a: a.reshape(-1, 1)
    return pl.pallas_call(
        _post_kernel,
        grid=(b, nt),
        in_specs=[pl.BlockSpec(memory_space=pl.ANY),
                  pl.BlockSpec((1, POST_POS, nc, D_ATTN), pm),
                  pl.BlockSpec((1, N_GROUPS, POST_POS * SSM_GROUP, nc), lambda bi, j: (bi, 0, j, 0)),
                  pl.BlockSpec((1, 6, d), lambda bi, j: (bi, 0, 0)),
                  pl.BlockSpec((D_SSM, D_SSM), const),
                  pl.BlockSpec((D_SSM, 1), const),
                  pl.BlockSpec((D_SSM, 1), const),
                  pl.BlockSpec((d, d), const),
                  pl.BlockSpec((1, d), const),
                  pl.BlockSpec((N_EXPERTS, d), const),
                  pl.BlockSpec((N_EXPERTS, 1), const),
                  pl.BlockSpec((ts, ts), const)],
        out_specs=[pl.BlockSpec((1, POST_POS, nc, d), pm),
                   pl.BlockSpec((1, POST_POS, nc, d), pm),
                   pl.BlockSpec((TOP_K, ts), tok),
                   pl.BlockSpec((TOP_K, ts), tok),
                   pl.BlockSpec((TOP_K, ts), tok),
                   pl.BlockSpec((1, N_EXPERTS, 1), lambda bi, j: (bi * nt + j, 0, 0)),
                   pl.BlockSpec((N_EXPERTS, 1), const)],
        out_shape=[jax.ShapeDtypeStruct((b, SSM_CHUNK, nc, d), F32),
                   jax.ShapeDtypeStruct((b, SSM_CHUNK, nc, d), BF16),
                   jax.ShapeDtypeStruct((TOP_K, t), I32),
                   jax.ShapeDtypeStruct((TOP_K, t), F32),
                   jax.ShapeDtypeStruct((TOP_K, t), I32),
                   jax.ShapeDtypeStruct((b * nt, N_EXPERTS, 1), I32),
                   jax.ShapeDtypeStruct((N_EXPERTS, 1), I32)],
        scratch_shapes=[pltpu.VMEM((N_EXPERTS, 1), F32), pltpu.VMEM((2, POST_POS, nc, d), F32),
                        pltpu.SemaphoreType.DMA((2,))],
        compiler_params=_cparams(2),
        name="post",
    )(x.reshape(b, nc, SSM_CHUNK, d), attn, yt, mod, w_glu.T.astype(BF16), col(b_glu), col(ssm_norm), w_out.astype(BF16),
      norm_ffn.reshape(1, -1), w_router.T.astype(BF16), col(b_router), tri)


def _route_kernel(eidx_ref, lrank_ref, r0_ref, cnt_ref, ls_ref, tab_ref, te_ref, nv_ref, nx_ref, pad_ref):
    cnt = cnt_ref[...]
    tiles = (cnt + (RUN - 1 + FFN_ROWS - 1)) // FFN_ROWS
    er = lax.broadcasted_iota(I32, (N_EXPERTS, N_EXPERTS), 0)
    ec = lax.broadcasted_iota(I32, (N_EXPERTS, N_EXPERTS), 1)
    ltri = (ec < er).astype(BF16)

    def excl_cumsum(v):
        vb = jnp.broadcast_to(v.astype(F32), (N_EXPERTS, LANES)).astype(BF16)
        return jnp.dot(ltri, vb, preferred_element_type=F32)[:, 0:1].astype(I32)

    start_t = excl_cumsum(tiles)
    end_t = start_t + tiles
    start = start_t * FFN_ROWS
    pad_ref[...] = start + cnt

    nb = r0_ref.shape[0]
    ts = eidx_ref.shape[1] // nb
    iota_e = lax.broadcasted_iota(I32, (N_EXPERTS, ts), 0)
    iota_t = lax.broadcasted_iota(I32, (N_EXPERTS, TABW), 0)
    chunk = lax.broadcasted_iota(I32, (1, TABW), 1)

    def block(b, carry):
        lanes = pl.ds(pl.multiple_of(b * ts, ts), ts)
        sels = [iota_e == eidx_ref[k:k + 1, lanes] for k in range(TOP_K)]
        member = sels[0].astype(I32) + sels[1].astype(I32) + sels[2].astype(I32) + sels[3].astype(I32)
        nch = (jnp.sum(member, axis=1, keepdims=True) + (RUN - 1)) // RUN
        cb = excl_cumsum(nch)
        end_c = cb + nch
        for k in range(TOP_K):
            first = jnp.sum(jnp.where(sels[k], cb, 0), axis=0, keepdims=True)
            lr = lrank_ref[k:k + 1, lanes]
            ls_ref[k:k + 1, lanes] = (first + lax.shift_right_logical(lr, RUN_SHIFT)) * RUN + (lr & (RUN - 1))
        e_of_c = jnp.sum((chunk >= end_c).astype(I32), axis=0, keepdims=True)
        sel_c = iota_t == e_of_c
        first_c = jnp.sum(jnp.where(sel_c, cb, 0), axis=0, keepdims=True)
        slot0_c = jnp.sum(jnp.where(sel_c, start + r0_ref[b], 0), axis=0, keepdims=True)
        n_chunks = jnp.max(end_c, axis=0, keepdims=True)
        row = jnp.where(chunk < n_chunks, slot0_c + (chunk - first_c) * RUN, -1)
        tab_ref[pl.ds(b, 1), :] = jnp.where(chunk == TABW - 1, n_chunks, row)
        return carry

    lax.fori_loop(0, nb, block, 0)

    nv = jnp.max(end_t, axis=0, keepdims=True)
    width = te_ref.shape[1]
    ti = jnp.minimum(lax.broadcasted_iota(I32, (N_EXPERTS, width), 1), nv - 1)
    te = jnp.minimum(jnp.sum((ti >= end_t).astype(I32), axis=0, keepdims=True), N_EXPERTS - 1)
    te_ref[...] = te
    nv_ref[...] = jnp.broadcast_to(nv, nv_ref.shape)
    ie = lax.broadcasted_iota(I32, (N_EXPERTS, width), 0)
    own_end = jnp.sum(jnp.where(ie == te, end_t, 0), axis=0, keepdims=True)
    nxt = jnp.minimum(jnp.sum((own_end >= end_t).astype(I32), axis=0, keepdims=True), N_EXPERTS - 1)
    nx_ref[...] = jnp.where(own_end < nv, nxt, -1)


def _route(eidx, lrank, r0, cnt, n_tiles):
    t = eidx.shape[1]
    nb = r0.shape[0]
    width = -(-n_tiles // LANES) * LANES
    return pl.pallas_call(
        _route_kernel,
        out_shape=[jax.ShapeDtypeStruct((TOP_K, t), I32),
                   jax.ShapeDtypeStruct((nb, TABW), I32),
                   jax.ShapeDtypeStruct((1, width), I32),
                   jax.ShapeDtypeStruct((1, LANES), I32),
                   jax.ShapeDtypeStruct((1, width), I32),
                   jax.ShapeDtypeStruct((N_EXPERTS, 1), I32)],
        name="route",
    )(eidx, lrank, r0, cnt)


def _for_chunk_pairs(n, fn):
    def body(i, carry):
        fn(2 * i, 0)

        @pl.when(2 * i + 1 < n)
        def _():
            fn(2 * i + 1, 1)
        return carry
    lax.fori_loop(0, lax.shift_right_logical(n + 1, 1), body, 0)


def _local_rows(ts):
    return ts * TOP_K + N_EXPERTS * RUN


def _dispatch_kernel(tab_ref, pad_ref, nvt_ref, h_ref, ls_ref, xs_ref, buf, zbuf, sems, zsem):
    b = pl.program_id(0)
    slot = b % 2
    ts = h_ref.shape[0]
    local = buf.shape[2]

    def chunk_copy(sl, blk, c):
        rows = pl.ds(pl.multiple_of(c * RUN, RUN), RUN)
        return pltpu.make_async_copy(buf.at[sl, :, rows, :], xs_ref.at[:, pl.ds(tab_ref[blk * TABW + c], RUN), :],
                                     sems.at[sl])

    def for_chunks(blk, fn):
        _for_chunk_pairs(tab_ref[blk * TABW + TABW - 1], fn)

    @pl.when(b == 0)
    def _():
        zbuf[...] = jnp.zeros_like(zbuf)
        zrows = zbuf.shape[1]
        zero = lambda row: pltpu.make_async_copy(zbuf, xs_ref.at[:, pl.ds(row, zrows), :], zsem)
        for phase in range(3):
            for e in range(phase, N_EXPERTS, 3):
                zero(pad_ref[e]).start()
            for e in range(phase, N_EXPERTS, 3):
                zero(pad_ref[e]).wait()
        ztile = lambda i: pltpu.make_async_copy(zbuf.at[:, pl.ds(0, FFN_ROWS), :],
                                                xs_ref.at[:, pl.ds((nvt_ref[0] + i) * FFN_ROWS, FFN_ROWS), :], zsem)

        def tail_start(i, carry):
            ztile(i).start()
            return carry

        def tail_wait(i, carry):
            ztile(i).wait()
            return carry
        lax.fori_loop(0, nvt_ref[1] - nvt_ref[0], tail_start, 0)
        lax.fori_loop(0, nvt_ref[1] - nvt_ref[0], tail_wait, 0)

    r = lax.broadcasted_iota(I32, (local, ts), 0)
    hit = (r == ls_ref[0:1, :]) | (r == ls_ref[1:2, :]) | (r == ls_ref[2:3, :]) | (r == ls_ref[3:4, :])
    hit = hit.astype(BF16)
    for pb in range(PANELS):
        srt = jnp.dot(hit, h_ref[:, pb * PANEL_COLS:(pb + 1) * PANEL_COLS], preferred_element_type=F32)
        buf[slot, pb] = _pack_panel(srt, exact=True)

    @pl.when(b > 0)
    def _():
        for_chunks(b - 1, lambda c, p: chunk_copy(1 - slot, b - 1, c).wait())

    for_chunks(b, lambda c, p: chunk_copy(slot, b, c).start(priority=p))

    @pl.when(b == pl.num_programs(0) - 1)
    def _():
        for_chunks(b, lambda c, p: chunk_copy(slot, b, c).wait())


def _dispatch(tab, pad, nvt, h2, ls, n_rows):
    t, d = h2.shape
    nb = tab.shape[0] // TABW
    ts = t // nb
    return pl.pallas_call(
        _dispatch_kernel,
        grid_spec=pltpu.PrefetchScalarGridSpec(
            num_scalar_prefetch=3,
            grid=(nb,),
            in_specs=[pl.BlockSpec((ts, d), lambda i, *_: (i, 0)),
                      pl.BlockSpec((TOP_K, ts), lambda i, *_: (0, i))],
            out_specs=pl.BlockSpec(memory_space=pl.ANY),
            scratch_shapes=[pltpu.VMEM((2, PANELS, _local_rows(ts), LANES), U32),
                            pltpu.VMEM((PANELS, FFN_ROWS + RUN, LANES), U32),
                            pltpu.SemaphoreType.DMA((2,)), pltpu.SemaphoreType.DMA],
        ),
        out_shape=jax.ShapeDtypeStruct((PANELS, n_rows, LANES), U32),
        compiler_params=_cparams(1, ROW_VMEM_BYTES),
        name="dispatch",
    )(tab, pad, nvt, h2, ls)


def _ffn_kernel(te_ref, nv_ref, nx_ref, xs_ref, wgu_hbm, bg_ref, bu_ref, wd_hbm, bd_ref, perm_ref, ys_ref,
                wgu_stage, wd_stage, wg_scr, wu_scr, wd_scr, sems):
    p = pl.program_id(0)
    t0 = 2 * p
    t1 = t0 + 1
    e0 = te_ref[t0]
    e1 = te_ref[t1]
    v0 = t0 < nv_ref[0]
    v1 = t1 < nv_ref[0]
    new0 = (p == 0) | (e0 != te_ref[jnp.maximum(t0 - 1, 0)])
    same = v1 & (e1 == e0)

    def stage_copies(e):
        return (pltpu.make_async_copy(wgu_hbm.at[e], wgu_stage, sems.at[0]),
                pltpu.make_async_copy(wd_hbm.at[e], wd_stage, sems.at[1]))

    def load_expert(t, first):
        e = te_ref[t]
        if first:
            @pl.when(p == 0)
            def _():
                for cp in stage_copies(e):
                    cp.start()

        for cp in stage_copies(e):
            cp.wait()
        for c in range(2 * D_FF // PERM):
            w = wgu_stage[:, c * PERM:(c + 1) * PERM].astype(BF16)
            pw = jnp.dot(w, perm_ref[...], preferred_element_type=F32).astype(BF16)
            wg_scr[:, c * (PERM // 2):(c + 1) * (PERM // 2)] = pw[:, :PERM // 2]
            wu_scr[:, c * (PERM // 2):(c + 1) * (PERM // 2)] = pw[:, PERM // 2:]
        wd_scr[...] = wd_stage[...].astype(BF16)

        @pl.when(nx_ref[t] >= 0)
        def _():
            for cp in stage_copies(nx_ref[t]):
                cp.start()

    def run(lo, n, e):
        x = _unpack_panels([xs_ref[pb, lo:lo + n, :] for pb in range(PANELS)])
        gate = jnp.dot(x, wg_scr[...], preferred_element_type=F32) + bg_ref[e]
        up = jnp.dot(x, wu_scr[...], preferred_element_type=F32) + bu_ref[e]
        gate = jnp.minimum(gate, SWIGLU_LIMIT)
        up = jnp.clip(up, -SWIGLU_LIMIT, SWIGLU_LIMIT)
        act = ((up + 1.0) * (gate * jax.nn.sigmoid(SWIGLU_ALPHA * gate))).astype(BF16)
        bd = bd_ref[e]
        for pb in range(PANELS):
            cols = slice(pb * PANEL_COLS, (pb + 1) * PANEL_COLS)
            y = jnp.dot(act, wd_scr[:, cols], preferred_element_type=F32) + bd[:, cols]
            ys_ref[pb, lo:lo + n, :] = _pack_panel(y)

    @pl.when(v0 & new0)
    def _():
        load_expert(t0, True)

    @pl.when(same)
    def _():
        run(0, 2 * FFN_ROWS, e0)

    @pl.when(v0 & jnp.logical_not(same))
    def _():
        run(0, FFN_ROWS, e0)

    @pl.when(v1 & jnp.logical_not(same))
    def _():
        load_expert(t1, False)
        run(FFN_ROWS, FFN_ROWS, e1)


def _ffn(te, nv, nx, xs, w_gate_up, bg, bu, w_down, bd, n_tiles):
    d = D_MODEL
    pair = lambda i, te, nv, nx: (0, jnp.minimum(i, lax.shift_right_logical(nv[0] - 1, 1)), 0)
    whole = lambda i, te, nv, nx: (0, 0, 0)
    r = lax.broadcasted_iota(I32, (PERM, PERM), 0)
    c = lax.broadcasted_iota(I32, (PERM, PERM), 1)
    perm = (r == jnp.where(c < PERM // 2, 2 * c, 2 * (c - PERM // 2) + 1)).astype(BF16)
    return pl.pallas_call(
        _ffn_kernel,
        grid_spec=pltpu.PrefetchScalarGridSpec(
            num_scalar_prefetch=3,
            grid=(n_tiles // 2,),
            in_specs=[pl.BlockSpec((PANELS, 2 * FFN_ROWS, LANES), pair),
                      pl.BlockSpec(memory_space=pl.ANY),
                      pl.BlockSpec((N_EXPERTS, 1, D_FF), whole),
                      pl.BlockSpec((N_EXPERTS, 1, D_FF), whole),
                      pl.BlockSpec(memory_space=pl.ANY),
                      pl.BlockSpec((N_EXPERTS, 1, d), whole),
                      pl.BlockSpec((PERM, PERM), lambda i, te, nv, nx: (0, 0))],
            out_specs=pl.BlockSpec((PANELS, 2 * FFN_ROWS, LANES), pair),
            scratch_shapes=[pltpu.VMEM((d, 2 * D_FF), F32), pltpu.VMEM((D_FF, d), F32),
                            pltpu.VMEM((d, D_FF), BF16), pltpu.VMEM((d, D_FF), BF16), pltpu.VMEM((D_FF, d), BF16),
                            pltpu.SemaphoreType.DMA((2,))],
        ),
        out_shape=jax.ShapeDtypeStruct(xs.shape, U32),
        input_output_aliases={3: 0},
        compiler_params=_cparams(1, FFN_VMEM_BYTES),
        name="ffn",
    )(te, nv, nx, xs, w_gate_up, bg, bu, w_down, bd, perm)


def _combine_kernel(tab_ref, x1_ref, ls_ref, w_ref, mod_ref, ys_ref, o4_hbm, ybuf, ob_buf, sems, osems, *, n_blk):
    jj = pl.program_id(1)
    blk = pl.program_id(0) * pl.num_programs(1) + jj
    slot = blk % 2
    nc = x1_ref.shape[2]
    tt = POST_POS * nc
    d = x1_ref.shape[3]
    local = ybuf.shape[2]

    def chunk_copy(sl, bk, c):
        rows = pl.ds(pl.multiple_of(c * RUN, RUN), RUN)
        return pltpu.make_async_copy(ys_ref.at[:, pl.ds(tab_ref[bk * TABW + c], RUN), :], ybuf.at[sl, :, rows, :],
                                     sems.at[sl])

    def for_chunks(bk, fn):
        _for_chunk_pairs(tab_ref[bk * TABW + TABW - 1], fn)

    @pl.when(blk == 0)
    def _():
        ybuf[...] = jnp.zeros_like(ybuf)
        for_chunks(0, lambda c, p: chunk_copy(0, 0, c).start(priority=p))

    @pl.when(blk + 1 < n_blk)
    def _():
        for_chunks(blk + 1, lambda c, p: chunk_copy(1 - slot, blk + 1, c).start(priority=p))

    for_chunks(blk, lambda c, p: chunk_copy(slot, blk, c).wait())

    r = lax.broadcasted_iota(I32, (tt, local), 1)
    wm = jnp.zeros((tt, local), F32)
    for k in range(TOP_K):
        wm = wm + jnp.where(r == ls_ref[:, k:k + 1], w_ref[:, k:k + 1], 0.0)
    wm = wm.astype(BF16)
    acc = jnp.dot(wm, _unpack_panels([ybuf[slot, pb] for pb in range(PANELS)]), preferred_element_type=F32)
    out = x1_ref[0].reshape(tt, d) + mod_ref[0, 5:6, :] * acc

    def out_copies(sl, b_, j_):
        return [pltpu.make_async_copy(ob_buf.at[sl, il], o4_hbm.at[b_, :, POST_POS * j_ + il, :], osems.at[sl])
                for il in range(POST_POS)]

    @pl.when(blk >= 2)
    def _():
        for cp in out_copies(slot, 0, 0):
            cp.wait()

    for il in range(POST_POS):
        ob_buf[slot, il] = out[il * nc:(il + 1) * nc]
    for cp in out_copies(slot, pl.program_id(0), jj):
        cp.start()

    @pl.when(blk == n_blk - 1)
    def _():
        for cp in out_copies(slot, 0, 0):
            cp.wait()
        if n_blk > 1:
            for cp in out_copies(1 - slot, 0, 0):
                cp.wait()


def _combine(tab, x1, ls_t, wts_t, mod, ys):
    b, _, nc, d = x1.shape
    s = SSM_CHUNK * nc
    tt = POST_POS * nc
    nt = SSM_CHUNK // POST_POS
    o4 = pl.pallas_call(
        functools.partial(_combine_kernel, n_blk=b * nt),
        grid_spec=pltpu.PrefetchScalarGridSpec(
            num_scalar_prefetch=1,
            grid=(b, nt),
            in_specs=[pl.BlockSpec((1, POST_POS, nc, d), lambda bi, j, *_: (bi, j, 0, 0)),
                      pl.BlockSpec((tt, TOP_K), lambda bi, j, *_: (bi * nt + j, 0)),
                      pl.BlockSpec((tt, TOP_K), lambda bi, j, *_: (bi * nt + j, 0)),
                      pl.BlockSpec((1, 6, d), lambda bi, j, *_: (bi, 0, 0)),
                      pl.BlockSpec(memory_space=pl.ANY)],
            out_specs=pl.BlockSpec(memory_space=pl.ANY),
            scratch_shapes=[pltpu.VMEM((2, PANELS, _local_rows(tt), LANES), U32),
                            pltpu.VMEM((2, POST_POS, nc, d), F32),
                            pltpu.SemaphoreType.DMA((2,)), pltpu.SemaphoreType.DMA((2,))],
        ),
        out_shape=jax.ShapeDtypeStruct((b, nc, SSM_CHUNK, d), F32),
        compiler_params=_cparams(2),
        name="combine",
    )(tab, x1, ls_t, wts_t, mod, ys)
    return o4.reshape(b, s, d)


def kernel(x, c, w_ada, b_ada, norm_mix, w_in, b_in, q_norm, k_norm, sinks, lam_re, lam_im, log_dt, b_re, b_im,
           c_re, c_im, d_skip, w_glu, b_glu, attn_out_norm, ssm_out_norm, w_out, norm_ffn, w_router, b_router,
           w_gate_up, b_gate_up, w_down, b_down):
    b, s, d = x.shape
    t = b * s
    depth = w_ada.shape[0]
    n_tiles = -(-(t * TOP_K + N_EXPERTS * (RUN - 1 + FFN_ROWS - 1)) // FFN_ROWS)
    n_tiles += n_tiles % 2
    n_alloc = n_tiles + 2
    for l in range(depth):
        mod = _adaln(c, w_ada[l], b_ada[l]).reshape(b, 6, d)
        q, k, v, ut = _inproj(x, mod, norm_mix[l], w_in[l], b_in[l])
        attn = _attention(q, k, v, sinks[l], q_norm[l], k_norm[l], attn_out_norm[l])
        tt, wz, wyt, cs = _ssm_params(lam_re[l], lam_im[l], log_dt[l], b_re[l], b_im[l], c_re[l], c_im[l])
        yt = _ssm(ut, tt, wz, wyt, cs, d_skip[l])
        x1, h2, eidx, wts, lrank, r0, cnt = _post(x, attn, yt, mod, w_glu[l], b_glu[l], ssm_out_norm[l], w_out[l],
                                                  norm_ffn[l], w_router[l], b_router[l])
        ls, tab, te, nv, nx, pad = _route(eidx, lrank, r0, cnt, n_tiles)
        tab = tab.reshape(-1)
        nvt = jnp.stack([nv[0, 0], jnp.int32(n_alloc)])
        xs = _dispatch(tab, pad.reshape(-1), nvt, h2.reshape(t, d), ls, n_alloc * FFN_ROWS)
        wgu = w_gate_up[l]
        bgu = b_gate_up[l]
        ys = _ffn(te[0, :n_tiles], nv[0, :1], nx[0, :n_tiles], xs, wgu, bgu[:, None, 0::2], bgu[:, None, 1::2],
                  w_down[l], b_down[l][:, None, :], n_tiles)
        x = _combine(tab, x1, ls.T, wts.T, mod, ys)
    return x
```

```python
import functools
import math

import jax
import jax.numpy as jnp
from jax import lax
from jax.experimental import pallas as pl
from jax.experimental.pallas import tpu as pltpu

F32 = jnp.float32
BF16 = jnp.bfloat16
U32 = jnp.uint32
I32 = jnp.int32

D_MODEL = 1024
HEAD_DIM = 64
N_HEADS = 8
N_KV_HEADS = 2
Q_PER_KV = N_HEADS // N_KV_HEADS
D_ATTN = N_HEADS * HEAD_DIM
D_KV = N_KV_HEADS * HEAD_DIM
D_QKV = D_ATTN + 2 * D_KV
WINDOW = 128
BLOCK = 128
D_SSM = D_MODEL - D_ATTN
SSM_GROUP = 16
N_GROUPS = D_SSM // SSM_GROUP
STATE = 64
N_EXPERTS = 32
TOP_K = 4
D_FF = D_MODEL
SWIGLU_LIMIT = 7.0
SWIGLU_ALPHA = 1.702
EPS = 1e-6
NEG_INF = -1e30

LANES = 128
SSM_CHUNK = 16
SSM_ROW = SSM_CHUNK * SSM_GROUP
N_POW = 2 * SSM_CHUNK
PANEL_COLS = 2 * LANES
PANELS = D_MODEL // PANEL_COLS

POS_PER_STEP = 4
ATTN_ROWS = 512
POST_POS = 2
FFN_ROWS = 256
RUN = 16
RUN_SHIFT = 4
TABW = 128
PERM = 256
FFN_VMEM_BYTES = 40 * 1024 * 1024
ROW_VMEM_BYTES = 48 * 1024 * 1024

HIGHEST = lax.Precision.HIGHEST
_ARB = "arbitrary"


def _cparams(n, vmem=None):
    return pltpu.CompilerParams(dimension_semantics=(_ARB,) * n, vmem_limit_bytes=vmem)


def _rms(x, axis=-1):
    return x * lax.rsqrt(jnp.mean(x * x, axis=axis, keepdims=True) + EPS)


def _pack_panel(y, exact=False):
    hi, lo = y[:, :LANES], y[:, LANES:]
    if not exact:
        hi = hi.astype(BF16).astype(F32)
        lo = lo.astype(BF16).astype(F32)
    return lax.bitcast_convert_type(hi, U32) | (lax.bitcast_convert_type(lo, U32) >> 16)


def _unpack_panels(words):
    cols = []
    for w in words:
        cols.append(lax.bitcast_convert_type(w & jnp.uint32(0xFFFF0000), F32).astype(BF16))
        cols.append(lax.bitcast_convert_type(w << 16, F32).astype(BF16))
    return jnp.concatenate(cols, axis=-1)


def _prefetch_pos_rows(x4_hbm, buf, sems, n_pos):
    bi = pl.program_id(0)
    j = pl.program_id(1)
    nj = pl.num_programs(1)
    g = bi * nj + j
    slot = g % 2

    def copies(sl, b_, j_):
        return [pltpu.make_async_copy(x4_hbm.at[b_, :, n_pos * j_ + il, :], buf.at[sl, il], sems.at[sl])
                for il in range(n_pos)]

    @pl.when(g == 0)
    def _():
        for cp in copies(0, 0, 0):
            cp.start()

    @pl.when(g + 1 < pl.num_programs(0) * nj)
    def _():
        wrap = j + 1 == nj
        for cp in copies(1 - slot, jnp.where(wrap, bi + 1, bi), jnp.where(wrap, 0, j + 1)):
            cp.start()

    for cp in copies(slot, bi, j):
        cp.wait()
    return slot


def _to_lane_blocks(dst, src):
    for kb in range(dst.shape[0]):
        dst[kb] = src[:, kb * LANES:(kb + 1) * LANES]


def _adaln_kernel(c_ref, w_ref, b_ref, o_ref):
    c = c_ref[...]
    ca = c * jax.nn.sigmoid(c)
    o_ref[...] = jnp.dot(ca, w_ref[...], preferred_element_type=F32, precision=HIGHEST) + b_ref[...]


def _adaln(c, w_ada, b_ada):
    b, d = c.shape
    n = w_ada.shape[1] // d
    return pl.pallas_call(
        _adaln_kernel,
        grid=(n,),
        in_specs=[pl.BlockSpec((b, d), lambda j: (0, 0)),
                  pl.BlockSpec((d, d), lambda j: (0, j)),
                  pl.BlockSpec((1, d), lambda j: (0, j))],
        out_specs=pl.BlockSpec((b, d), lambda j: (0, j)),
        out_shape=jax.ShapeDtypeStruct((b, n * d), F32),
        compiler_params=_cparams(1),
        name="adaln",
    )(c, w_ada, b_ada.reshape(1, -1))


def _inproj_kernel(x4_hbm, x_ref, mod_ref, g_ref, wqkv_ref, bqkv_ref, wut_ref, but_ref, q_ref, k_ref, v_ref, ut_ref,
                   xp_buf, sems):
    nc = ut_ref.shape[3]
    slot = _prefetch_pos_rows(x4_hbm, xp_buf, sems, POS_PER_STEP)
    gain = g_ref[...]
    scale = 1.0 + mod_ref[0, 1:2, :]
    shift = mod_ref[0, 0:1, :]

    def norm_mod(x):
        return (_rms(x) * gain * scale + shift).astype(BF16)

    proj = jnp.dot(norm_mod(x_ref[0]), wqkv_ref[...], preferred_element_type=F32) + bqkv_ref[...]
    q_ref[0] = proj[:, :D_ATTN].astype(BF16)
    k_ref[0] = proj[:, D_ATTN:D_ATTN + D_KV].astype(BF16)
    v_ref[0] = proj[:, D_ATTN + D_KV:].astype(BF16)

    hs = jnp.concatenate([norm_mod(xp_buf[slot, il]) for il in range(POS_PER_STEP)], axis=0)
    ut = lax.dot_general(wut_ref[...], hs, (((1,), (1,)), ((), ())), preferred_element_type=F32) + but_ref[...]
    for il in range(POS_PER_STEP):
        piece = ut[:, il * nc:(il + 1) * nc].astype(BF16)
        ut_ref[0, :, il * SSM_GROUP:(il + 1) * SSM_GROUP, :] = piece.reshape(N_GROUPS, SSM_GROUP, nc)


def _inproj(x, mod, gain, w_in, b_in):
    b, s, d = x.shape
    nc = s // SSM_CHUNK
    rows = POS_PER_STEP * nc
    row = lambda bi, j: (bi, j, 0)
    const = lambda bi, j: (0, 0)
    w_qkv = w_in[:, :D_QKV].astype(BF16)
    w_ut = w_in[:, D_QKV:].T.astype(BF16)
    return pl.pallas_call(
        _inproj_kernel,
        grid=(b, SSM_CHUNK // POS_PER_STEP),
        in_specs=[pl.BlockSpec(memory_space=pl.ANY),
                  pl.BlockSpec((1, rows, d), row),
                  pl.BlockSpec((1, 6, d), lambda bi, j: (bi, 0, 0)),
                  pl.BlockSpec((1, d), const),
                  pl.BlockSpec((d, D_QKV), const),
                  pl.BlockSpec((1, D_QKV), const),
                  pl.BlockSpec((D_SSM, d), const),
                  pl.BlockSpec((D_SSM, 1), const)],
        out_specs=[pl.BlockSpec((1, rows, D_ATTN), row),
                   pl.BlockSpec((1, rows, D_KV), row),
                   pl.BlockSpec((1, rows, D_KV), row),
                   pl.BlockSpec((1, N_GROUPS, POS_PER_STEP * SSM_GROUP, nc), lambda bi, j: (bi, 0, j, 0))],
        out_shape=[jax.ShapeDtypeStruct((b, s, D_ATTN), BF16),
                   jax.ShapeDtypeStruct((b, s, D_KV), BF16),
                   jax.ShapeDtypeStruct((b, s, D_KV), BF16),
                   jax.ShapeDtypeStruct((b, N_GROUPS, SSM_ROW, nc), BF16)],
        scratch_shapes=[pltpu.VMEM((2, POS_PER_STEP, nc, d), F32), pltpu.SemaphoreType.DMA((2,))],
        compiler_params=_cparams(2),
        name="inproj",
    )(x.reshape(b, nc, SSM_CHUNK, d), x, mod, gain.reshape(1, d), w_qkv, b_in[:D_QKV].reshape(1, D_QKV), w_ut, b_in[D_QKV:].reshape(D_SSM, 1))


def _half_norm(x, low):
    sq = x * x
    s_lo = jnp.sum(jnp.where(low, sq, 0.0), axis=-1, keepdims=True)
    s_hi = jnp.sum(sq, axis=-1, keepdims=True) - s_lo
    inv = 1.0 / HEAD_DIM
    scale = jnp.where(low, lax.rsqrt(s_lo * inv + EPS), lax.rsqrt(s_hi * inv + EPS))
    return x * scale


def _attn_block(first, q, k_prev, k_cur, v_prev, v_cur, sinks_ref, qn, low, upper, rblk):
    no_prev = jnp.where(first, NEG_INF, 0.0)
    out_blocks = []
    for hk in range(N_KV_HEADS):
        qs = []
        for j in range(Q_PER_KV // 2):
            blk = hk * (Q_PER_KV // 2) + j
            qb = _half_norm(q[:, blk * LANES:(blk + 1) * LANES], low) * qn * (1.0 / math.sqrt(HEAD_DIM))
            qs.append(jnp.where(low, qb, 0.0))
            qs.append(jnp.where(low, 0.0, qb))
        qg = jnp.concatenate(qs, axis=0).astype(BF16)
        nt = (((1,), (1,)), ((), ()))
        s_prev = lax.dot_general(qg, k_prev[hk], nt, preferred_element_type=F32)
        s_cur = lax.dot_general(qg, k_cur[hk], nt, preferred_element_type=F32)
        s = jnp.where(upper, s_prev + no_prev, s_cur)
        sink = jnp.zeros((Q_PER_KV * BLOCK, 1), F32)
        for g in range(Q_PER_KV):
            sink = jnp.where(rblk == g, sinks_ref[hk * Q_PER_KV + g], sink)
        m = jnp.maximum(jnp.max(s, axis=-1, keepdims=True), sink)
        p = jnp.exp(s - m)
        den = jnp.sum(p, axis=-1, keepdims=True) + jnp.exp(sink - m)
        o = (jnp.dot(jnp.where(upper, p, 0.0).astype(BF16), v_prev[hk], preferred_element_type=F32)
             + jnp.dot(jnp.where(upper, 0.0, p).astype(BF16), v_cur[hk], preferred_element_type=F32)) / den
        for j in range(Q_PER_KV // 2):
            ev = o[(2 * j) * BLOCK:(2 * j + 1) * BLOCK]
            od = o[(2 * j + 1) * BLOCK:(2 * j + 2) * BLOCK]
            out_blocks.append(jnp.where(low, ev, od))
    return jnp.concatenate(out_blocks, axis=-1)


def _attn_kernel(sinks_ref, q_ref, k_ref, v_ref, qn_ref, kn_ref, on_ref, o_hbm, a_buf, sems, *, n_steps):
    step = pl.program_id(1)
    g = pl.program_id(0) * pl.num_programs(1) + step
    slot = g % 2
    cps = ATTN_ROWS // SSM_CHUNK
    nq = ATTN_ROWS // BLOCK

    def out_copies(sl, b_, s_):
        return [pltpu.make_async_copy(a_buf.at[sl, :, i, :], o_hbm.at[b_, i, pl.ds(s_ * cps, cps), :], sems.at[sl])
                for i in range(SSM_CHUNK)]

    @pl.when(g >= 2)
    def _():
        for cp in out_copies(slot, 0, 0):
            cp.wait()

    low = lax.broadcasted_iota(I32, (1, LANES), 1) < HEAD_DIM
    rows = Q_PER_KV * BLOCK
    upper = lax.broadcasted_iota(I32, (rows, BLOCK), 1) > lax.broadcasted_iota(I32, (rows, BLOCK), 0) % BLOCK
    rblk = lax.broadcasted_iota(I32, (rows, 1), 0) // BLOCK

    cur = pl.multiple_of(step * ATTN_ROWS, ATTN_ROWS)
    prev = pl.multiple_of(jnp.maximum(step * nq - 1, 0) * BLOCK, BLOCK)
    kall = jnp.concatenate([k_ref[0, pl.ds(prev, BLOCK), :], k_ref[0, pl.ds(cur, ATTN_ROWS), :]], axis=0).astype(F32)
    vall = jnp.concatenate([v_ref[0, pl.ds(prev, BLOCK), :], v_ref[0, pl.ds(cur, ATTN_ROWS), :]], axis=0).astype(F32)
    kall = _half_norm(kall, low) * kn_ref[...]
    kswap = pltpu.roll(kall, HEAD_DIM, axis=1)
    vswap = pltpu.roll(vall, HEAD_DIM, axis=1)
    k_dup = [jnp.where(low, kall, kswap).astype(BF16), jnp.where(low, kswap, kall).astype(BF16)]
    v_dup = [jnp.where(low, vall, vswap).astype(BF16), jnp.where(low, vswap, vall).astype(BF16)]
    blk = lambda a, i: [a[hk][i * BLOCK:(i + 1) * BLOCK] for hk in range(N_KV_HEADS)]

    for qb in range(nq):
        q = q_ref[0, qb * BLOCK:(qb + 1) * BLOCK, :].astype(F32)
        attn = _attn_block((step == 0) if qb == 0 else False, q, blk(k_dup, qb), blk(k_dup, qb + 1),
                           blk(v_dup, qb), blk(v_dup, qb + 1), sinks_ref, qn_ref[...], low, upper, rblk)
        attn = _rms(attn) * on_ref[...]
        cpb = BLOCK // SSM_CHUNK
        a_buf[slot, qb * cpb:(qb + 1) * cpb] = attn.reshape(cpb, SSM_CHUNK, D_ATTN)

    for cp in out_copies(slot, pl.program_id(0), step):
        cp.start()

    @pl.when(g == n_steps - 1)
    def _():
        for cp in out_copies(slot, 0, 0):
            cp.wait()
        if n_steps > 1:
            for cp in out_copies(1 - slot, 0, 0):
                cp.wait()


def _attention(q, k, v, sinks, q_norm, k_norm, out_norm):
    b, s, _ = q.shape
    tile2 = lambda g: jnp.tile(g.reshape(1, HEAD_DIM), (1, 2))
    cps = ATTN_ROWS // SSM_CHUNK
    return pl.pallas_call(
        functools.partial(_attn_kernel, n_steps=b * (s // ATTN_ROWS)),
        grid=(b, s // ATTN_ROWS),
        in_specs=[pl.BlockSpec(memory_space=pltpu.SMEM),
                  pl.BlockSpec((1, ATTN_ROWS, D_ATTN), lambda bi, n: (bi, n, 0)),
                  pl.BlockSpec((1, s, D_KV), lambda bi, n: (bi, 0, 0)),
                  pl.BlockSpec((1, s, D_KV), lambda bi, n: (bi, 0, 0)),
                  pl.BlockSpec((1, LANES), lambda bi, n: (0, 0)),
                  pl.BlockSpec((1, LANES), lambda bi, n: (0, 0)),
                  pl.BlockSpec((1, D_ATTN), lambda bi, n: (0, 0))],
        out_specs=pl.BlockSpec(memory_space=pl.ANY),
        out_shape=jax.ShapeDtypeStruct((b, SSM_CHUNK, s // SSM_CHUNK, D_ATTN), F32),
        scratch_shapes=[pltpu.VMEM((2, cps, SSM_CHUNK, D_ATTN), F32), pltpu.SemaphoreType.DMA((2,))],
        compiler_params=_cparams(2),
        name="attention",
    )(sinks, q, k, v, tile2(q_norm), tile2(k_norm), out_norm.reshape(1, D_ATTN))


def _cmul(ar, ai, br, bi):
    return ar * br - ai * bi, ar * bi + ai * br


def _ssm_param_kernel(lam_ref, bre_ref, bim_ref, cre_ref, cim_ref, tt_ref, wz_ref, wyt_ref, cs_ref):
    f32dot = functools.partial(jnp.dot, preferred_element_type=F32, precision=HIGHEST)
    lr = lam_ref[0, 0:1, :]
    li = lam_ref[0, 1:2, :]
    dt = jnp.exp(lam_ref[0, 2:3, :])
    rho = lr * dt
    th = li * dt
    imag_lane = lax.broadcasted_iota(I32, (1, LANES), 1) >= STATE

    kk = (lax.broadcasted_iota(I32, (N_POW, 1), 0) - (SSM_CHUNK - 1)).astype(F32)
    mag = jnp.exp(rho * kk)
    pw_r = mag * jnp.cos(th * kk)
    pw_i = mag * jnp.sin(th * kk)
    lb_r = pw_r[SSM_CHUNK:SSM_CHUNK + 1]
    lb_i = pw_i[SSM_CHUNK:SSM_CHUNK + 1]
    den = lr * lr + li * li
    coef_r = ((lb_r - 1.0) * lr + lb_i * li) / den
    coef_i = (lb_i * lr - (lb_r - 1.0) * li) / den

    eye = (lax.broadcasted_iota(I32, (SSM_GROUP, SSM_GROUP), 0)
           == lax.broadcasted_iota(I32, (SSM_GROUP, SSM_GROUP), 1)).astype(F32)
    lane_fold = (lax.broadcasted_iota(I32, (STATE, LANES), 1) % STATE
                 == lax.broadcasted_iota(I32, (STATE, LANES), 0)).astype(F32)

    def tile_pos(x):
        return jnp.concatenate([x] * SSM_CHUNK, axis=0)

    def power_rows(k_of_pos):
        idx = [k_of_pos(p) + (SSM_CHUNK - 1) for p in range(SSM_CHUNK)]
        rep = lambda t: jnp.concatenate([jnp.broadcast_to(t[r:r + 1], (SSM_GROUP, LANES)) for r in idx], axis=0)
        return rep(pw_r), rep(pw_i)

    def b_rows(b_ref):
        b2 = jnp.concatenate([b_ref[0], b_ref[0]], axis=0)
        return tile_pos(lax.dot_general(eye, b2, (((1,), (1,)), ((), ())), preferred_element_type=F32,
                                        precision=HIGHEST))

    def c_rows(c_ref):
        return tile_pos(f32dot(c_ref[0], lane_fold))

    bbar_r, bbar_i = _cmul(coef_r, coef_i, b_rows(bre_ref), b_rows(bim_ref))
    c_r = c_rows(cre_ref)
    c_i = c_rows(cim_ref)

    a_r, a_i = _cmul(bbar_r, bbar_i, *power_rows(lambda p: -p))
    a2c = jnp.where(imag_lane, -a_i, a_r)
    m_r, m_i = _cmul(c_r, c_i, *power_rows(lambda p: p))
    bmc = jnp.where(imag_lane, m_i, m_r)
    tt = f32dot(bmc, a2c.T)
    causal = (lax.broadcasted_iota(I32, (SSM_ROW, 1), 0) // SSM_GROUP
              >= lax.broadcasted_iota(I32, (1, SSM_ROW), 1) // SSM_GROUP)
    tt_ref[0] = jnp.where(causal, tt, 0.0).astype(BF16)

    w_r, w_i = _cmul(bbar_r, bbar_i, *power_rows(lambda p: SSM_CHUNK - 1 - p))
    wz_ref[0, :, :LANES] = jnp.where(imag_lane, w_i, w_r).astype(BF16)
    wz_ref[0, :, LANES:] = jnp.where(imag_lane, w_r, w_i).astype(BF16)

    y_r, y_i = _cmul(c_r, c_i, *power_rows(lambda p: p + 1))
    wyt_ref[0] = jnp.where(imag_lane, -y_i, y_r).astype(BF16)

    cs_ref[0, 0:1, :] = pw_r[N_POW - 1:N_POW]
    cs_ref[0, 1:2, :] = jnp.where(imag_lane, pw_i[N_POW - 1:N_POW], -pw_i[N_POW - 1:N_POW])


def _ssm_params(lam_re, lam_im, log_dt, b_re, b_im, c_re, c_im):
    g = lam_re.shape[0]
    lam = jnp.stack([lam_re, lam_im, jnp.broadcast_to(log_dt[:, None], (g, STATE))], axis=1)
    lam = jnp.concatenate([lam, lam], axis=2)
    blk = lambda *shape: pl.BlockSpec((1,) + shape, lambda i: (i, 0, 0))
    return pl.pallas_call(
        _ssm_param_kernel,
        grid=(g,),
        in_specs=[blk(3, LANES), blk(STATE, SSM_GROUP), blk(STATE, SSM_GROUP), blk(SSM_GROUP, STATE),
                  blk(SSM_GROUP, STATE)],
        out_specs=[blk(SSM_ROW, SSM_ROW), blk(SSM_ROW, SSM_ROW), blk(SSM_ROW, LANES), blk(2, LANES)],
        out_shape=[jax.ShapeDtypeStruct((g, SSM_ROW, SSM_ROW), BF16),
                   jax.ShapeDtypeStruct((g, SSM_ROW, SSM_ROW), BF16),
                   jax.ShapeDtypeStruct((g, SSM_ROW, LANES), BF16),
                   jax.ShapeDtypeStruct((g, 2, LANES), F32)],
        compiler_params=_cparams(1),
        name="ssm_params",
    )(lam, b_re, b_im, c_re, c_im)


def _ssm_kernel(ut_ref, tt_ref, wz_ref, wyt_ref, cs_ref, d_ref, yt_ref, z_scr, s_scr):
    batch, _, _, nc = ut_ref.shape
    ut = jnp.concatenate([ut_ref[b, 0] for b in range(batch)], axis=1)
    _to_lane_blocks(z_scr, lax.dot_general(ut, wz_ref[0], (((0,), (0,)), ((), ())), preferred_element_type=F32))
    c1 = cs_ref[0, 0:1, :]
    c2 = cs_ref[0, 1:2, :]

    def step(c, carry):
        s1, s2 = carry
        rows = pl.ds(c, batch, stride=nc)
        s_scr[rows, :] = s1
        n1 = c1 * s1 + c2 * s2 + z_scr[0, rows, :]
        n2 = c1 * s2 - c2 * s1 + z_scr[1, rows, :]
        return n1, n2

    zero = jnp.zeros((batch, LANES), F32)
    lax.fori_loop(0, nc, step, (zero, zero), unroll=8)
    y = jnp.dot(tt_ref[0], ut, preferred_element_type=F32)
    y = y + lax.dot_general(wyt_ref[0], s_scr[...].astype(BF16), (((1,), (1,)), ((), ())),
                            preferred_element_type=F32)
    y = y + d_ref[0] * ut.astype(F32)
    for b in range(batch):
        yt_ref[b, 0] = y[:, b * nc:(b + 1) * nc]


def _ssm(ut, tt, wz, wyt, cs, d_skip):
    b, g, _, nc = ut.shape
    d_col = jnp.tile(d_skip.reshape(g, 1, SSM_GROUP), (1, SSM_CHUNK, 1)).reshape(g, SSM_ROW, 1)
    blk = lambda *shape: pl.BlockSpec((1,) + shape, lambda i: (i, 0, 0))
    act = pl.BlockSpec((b, 1, SSM_ROW, nc), lambda i: (0, i, 0, 0))
    return pl.pallas_call(
        _ssm_kernel,
        grid=(g,),
        in_specs=[act, blk(SSM_ROW, SSM_ROW), blk(SSM_ROW, SSM_ROW), blk(SSM_ROW, LANES), blk(2, LANES),
                  blk(SSM_ROW, 1)],
        out_specs=act,
        out_shape=jax.ShapeDtypeStruct((b, g, SSM_ROW, nc), F32),
        scratch_shapes=[pltpu.VMEM((SSM_ROW // LANES, b * nc, LANES), F32), pltpu.VMEM((b * nc, LANES), F32)],
        compiler_params=_cparams(1),
        name="ssm",
    )(ut, tt, wz, wyt, cs, d_col)


def _post_kernel(x4_hbm, attn_ref, yt_ref, mod_ref, wglut_ref, bglu_ref, sn_ref, wout_ref, nf_ref, wr_ref, br_ref,
                 tri_ref, x1_ref, h2_ref, eidx_ref, wts_ref, lrank_ref, r0_ref, cnt_ref, carry_ref, xp_buf, sems):
    @pl.when((pl.program_id(0) == 0) & (pl.program_id(1) == 0))
    def _():
        carry_ref[...] = jnp.zeros_like(carry_ref)

    slot = _prefetch_pos_rows(x4_hbm, xp_buf, sems, POST_POS)
    nc = attn_ref.shape[2]
    ts = POST_POS * nc
    d = x1_ref.shape[3]
    yt = jnp.concatenate(
        [yt_ref[0, :, il * SSM_GROUP:(il + 1) * SSM_GROUP, :].reshape(D_SSM, nc) for il in range(POST_POS)], axis=1)
    g = jax.nn.gelu(yt)
    gate = jax.nn.sigmoid(jnp.dot(wglut_ref[...], g.astype(BF16), preferred_element_type=F32) + bglu_ref[...])
    ssm_t = _rms(g * gate, axis=0) * sn_ref[...]
    mixed = jnp.concatenate([attn_ref[0].reshape(ts, D_ATTN).astype(BF16), ssm_t.T.astype(BF16)], axis=-1)
    o = jnp.dot(mixed, wout_ref[...], preferred_element_type=F32)
    x = jnp.concatenate([xp_buf[slot, il] for il in range(POST_POS)], axis=0)
    x1 = x + mod_ref[0, 2:3, :] * o
    x1_ref[0] = x1.reshape(POST_POS, nc, d)
    h2 = _rms(x1) * nf_ref[...] * (1.0 + mod_ref[0, 4:5, :]) + mod_ref[0, 3:4, :]
    h2_ref[0] = h2.astype(BF16).reshape(POST_POS, nc, d)

    logits = lax.dot_general(wr_ref[...], h2.astype(BF16), (((1,), (1,)), ((), ())),
                             preferred_element_type=F32) + br_ref[...]
    iota_e = lax.broadcasted_iota(I32, (N_EXPERTS, ts), 0).astype(F32)
    l = logits
    idxs, vals = [], []
    for _ in range(TOP_K):
        m = jnp.max(l, axis=0, keepdims=True)
        idx = jnp.min(jnp.where(l == m, iota_e, float(N_EXPERTS)), axis=0, keepdims=True)
        idxs.append(idx)
        vals.append(m)
        l = jnp.where(iota_e == idx, -jnp.inf, l)
    es = [jnp.exp(v - vals[0]) for v in vals]
    tot = es[0] + es[1] + es[2] + es[3]
    member = jnp.zeros((N_EXPERTS, ts), F32)
    for idx in idxs:
        member = member + (iota_e == idx).astype(F32)
    before = jnp.dot(member.astype(BF16), tri_ref[...], preferred_element_type=F32)
    for k in range(TOP_K):
        eidx_ref[k:k + 1, :] = idxs[k].astype(I32)
        wts_ref[k:k + 1, :] = es[k] / tot
        lrank_ref[k:k + 1, :] = jnp.sum(jnp.where(iota_e == idxs[k], before, 0.0), axis=0, keepdims=True).astype(I32)
    r0_ref[0] = carry_ref[...].astype(I32)
    carry = carry_ref[...] + jnp.sum(member, axis=1, keepdims=True)
    carry_ref[...] = carry
    cnt_ref[...] = carry.astype(I32)


def _post(x, attn, yt, mod, w_glu, b_glu, ssm_norm, w_out, norm_ffn, w_router, b_router):
    b, s, d = x.shape
    nc = s // SSM_CHUNK
    ts = POST_POS * nc
    nt = SSM_CHUNK // POST_POS
    t = b * s
    pm = lambda bi, j: (bi, j, 0, 0)
    const = lambda bi, j: (0, 0)
    tok = lambda bi, j: (0, bi * nt + j)
    tri = (lax.broadcasted_iota(I32, (ts, ts), 0) < lax.broadcasted_iota(I32, (ts, ts), 1)).astype(BF16)
    col = lambda a: a.reshape(-1, 1)
    return pl.pallas_call(
        _post_kernel,
        grid=(b, nt),
        in_specs=[pl.BlockSpec(memory_space=pl.ANY),
                  pl.BlockSpec((1, POST_POS, nc, D_ATTN), pm),
                  pl.BlockSpec((1, N_GROUPS, POST_POS * SSM_GROUP, nc), lambda bi, j: (bi, 0, j, 0)),
                  pl.BlockSpec((1, 6, d), lambda bi, j: (bi, 0, 0)),
                  pl.BlockSpec((D_SSM, D_SSM), const),
                  pl.BlockSpec((D_SSM, 1), const),
                  pl.BlockSpec((D_SSM, 1), const),
                  pl.BlockSpec((d, d), const),
                  pl.BlockSpec((1, d), const),
                  pl.BlockSpec((N_EXPERTS, d), const),
                  pl.BlockSpec((N_EXPERTS, 1), const),
                  pl.BlockSpec((ts, ts), const)],
        out_specs=[pl.BlockSpec((1, POST_POS, nc, d), pm),
                   pl.BlockSpec((1, POST_POS, nc, d), pm),
                   pl.BlockSpec((TOP_K, ts), tok),
                   pl.BlockSpec((TOP_K, ts), tok),
                   pl.BlockSpec((TOP_K, ts), tok),
                   pl.BlockSpec((1, N_EXPERTS, 1), lambda bi, j: (bi * nt + j, 0, 0)),
                   pl.BlockSpec((N_EXPERTS, 1), const)],
        out_shape=[jax.ShapeDtypeStruct((b, SSM_CHUNK, nc, d), F32),
                   jax.ShapeDtypeStruct((b, SSM_CHUNK, nc, d), BF16),
                   jax.ShapeDtypeStruct((TOP_K, t), I32),
                   jax.ShapeDtypeStruct((TOP_K, t), F32),
                   jax.ShapeDtypeStruct((TOP_K, t), I32),
                   jax.ShapeDtypeStruct((b * nt, N_EXPERTS, 1), I32),
                   jax.ShapeDtypeStruct((N_EXPERTS, 1), I32)],
        scratch_shapes=[pltpu.VMEM((N_EXPERTS, 1), F32), pltpu.VMEM((2, POST_POS, nc, d), F32),
                        pltpu.SemaphoreType.DMA((2,))],
        compiler_params=_cparams(2),
        name="post",
    )(x.reshape(b, nc, SSM_CHUNK, d), attn, yt, mod, w_glu.T.astype(BF16), col(b_glu), col(ssm_norm), w_out.astype(BF16),
      norm_ffn.reshape(1, -1), w_router.T.astype(BF16), col(b_router), tri)


def _route_kernel(eidx_ref, lrank_ref, r0_ref, cnt_ref, ls_ref, tab_ref, te_ref, nv_ref, nx_ref, pad_ref):
    cnt = cnt_ref[...]
    tiles = (cnt + (RUN - 1 + FFN_ROWS - 1)) // FFN_ROWS
    er = lax.broadcasted_iota(I32, (N_EXPERTS, N_EXPERTS), 0)
    ec = lax.broadcasted_iota(I32, (N_EXPERTS, N_EXPERTS), 1)
    ltri = (ec < er).astype(BF16)

    def excl_cumsum(v):
        vb = jnp.broadcast_to(v.astype(F32), (N_EXPERTS, LANES)).astype(BF16)
        return jnp.dot(ltri, vb, preferred_element_type=F32)[:, 0:1].astype(I32)

    start_t = excl_cumsum(tiles)
    end_t = start_t + tiles
    start = start_t * FFN_ROWS
    pad_ref[...] = start + cnt

    nb = r0_ref.shape[0]
    ts = eidx_ref.shape[1] // nb
    iota_e = lax.broadcasted_iota(I32, (N_EXPERTS, ts), 0)
    iota_t = lax.broadcasted_iota(I32, (N_EXPERTS, TABW), 0)
    chunk = lax.broadcasted_iota(I32, (1, TABW), 1)

    def block(b, carry):
        lanes = pl.ds(pl.multiple_of(b * ts, ts), ts)
        sels = [iota_e == eidx_ref[k:k + 1, lanes] for k in range(TOP_K)]
        member = sels[0].astype(I32) + sels[1].astype(I32) + sels[2].astype(I32) + sels[3].astype(I32)
        nch = (jnp.sum(member, axis=1, keepdims=True) + (RUN - 1)) // RUN
        cb = excl_cumsum(nch)
        end_c = cb + nch
        for k in range(TOP_K):
            first = jnp.sum(jnp.where(sels[k], cb, 0), axis=0, keepdims=True)
            lr = lrank_ref[k:k + 1, lanes]
            ls_ref[k:k + 1, lanes] = (first + lax.shift_right_logical(lr, RUN_SHIFT)) * RUN + (lr & (RUN - 1))
        e_of_c = jnp.sum((chunk >= end_c).astype(I32), axis=0, keepdims=True)
        sel_c = iota_t == e_of_c
        first_c = jnp.sum(jnp.where(sel_c, cb, 0), axis=0, keepdims=True)
        slot0_c = jnp.sum(jnp.where(sel_c, start + r0_ref[b], 0), axis=0, keepdims=True)
        n_chunks = jnp.max(end_c, axis=0, keepdims=True)
        row = jnp.where(chunk < n_chunks, slot0_c + (chunk - first_c) * RUN, -1)
        tab_ref[pl.ds(b, 1), :] = jnp.where(chunk == TABW - 1, n_chunks, row)
        return carry

    lax.fori_loop(0, nb, block, 0)

    nv = jnp.max(end_t, axis=0, keepdims=True)
    width = te_ref.shape[1]
    ti = jnp.minimum(lax.broadcasted_iota(I32, (N_EXPERTS, width), 1), nv - 1)
    te = jnp.minimum(jnp.sum((ti >= end_t).astype(I32), axis=0, keepdims=True), N_EXPERTS - 1)
    te_ref[...] = te
    nv_ref[...] = jnp.broadcast_to(nv, nv_ref.shape)
    ie = lax.broadcasted_iota(I32, (N_EXPERTS, width), 0)
    own_end = jnp.sum(jnp.where(ie == te, end_t, 0), axis=0, keepdims=True)
    nxt = jnp.minimum(jnp.sum((own_end >= end_t).astype(I32), axis=0, keepdims=True), N_EXPERTS - 1)
    nx_ref[...] = jnp.where(own_end < nv, nxt, -1)


def _route(eidx, lrank, r0, cnt, n_tiles):
    t = eidx.shape[1]
    nb = r0.shape[0]
    width = -(-n_tiles // LANES) * LANES
    return pl.pallas_call(
        _route_kernel,
        out_shape=[jax.ShapeDtypeStruct((TOP_K, t), I32),
                   jax.ShapeDtypeStruct((nb, TABW), I32),
                   jax.ShapeDtypeStruct((1, width), I32),
                   jax.ShapeDtypeStruct((1, LANES), I32),
                   jax.ShapeDtypeStruct((1, width), I32),
                   jax.ShapeDtypeStruct((N_EXPERTS, 1), I32)],
        name="route",
    )(eidx, lrank, r0, cnt)


def _for_chunk_pairs(n, fn):
    def body(i, carry):
        fn(2 * i, 0)

        @pl.when(2 * i + 1 < n)
        def _():
            fn(2 * i + 1, 1)
        return carry
    lax.fori_loop(0, lax.shift_right_logical(n + 1, 1), body, 0)


def _local_rows(ts):
    return ts * TOP_K + N_EXPERTS * RUN


def _dispatch_kernel(tab_ref, pad_ref, nvt_ref, h_ref, ls_ref, xs_ref, buf, zbuf, sems, zsem):
    b = pl.program_id(0)
    slot = b % 2
    ts = h_ref.shape[0]
    local = buf.shape[2]

    def chunk_copy(sl, blk, c):
        rows = pl.ds(pl.multiple_of(c * RUN, RUN), RUN)
        return pltpu.make_async_copy(buf.at[sl, :, rows, :], xs_ref.at[:, pl.ds(tab_ref[blk * TABW + c], RUN), :],
                                     sems.at[sl])

    def for_chunks(blk, fn):
        _for_chunk_pairs(tab_ref[blk * TABW + TABW - 1], fn)

    @pl.when(b == 0)
    def _():
        zbuf[...] = jnp.zeros_like(zbuf)
        zrows = zbuf.shape[1]
        zero = lambda row: pltpu.make_async_copy(zbuf, xs_ref.at[:, pl.ds(row, zrows), :], zsem)
        for phase in range(3):
            for e in range(phase, N_EXPERTS, 3):
                zero(pad_ref[e]).start()
            for e in range(phase, N_EXPERTS, 3):
                zero(pad_ref[e]).wait()
        ztile = lambda i: pltpu.make_async_copy(zbuf.at[:, pl.ds(0, FFN_ROWS), :],
                                                xs_ref.at[:, pl.ds((nvt_ref[0] + i) * FFN_ROWS, FFN_ROWS), :], zsem)

        def tail_start(i, carry):
            ztile(i).start()
            return carry

        def tail_wait(i, carry):
            ztile(i).wait()
            return carry
        lax.fori_loop(0, nvt_ref[1] - nvt_ref[0], tail_start, 0)
        lax.fori_loop(0, nvt_ref[1] - nvt_ref[0], tail_wait, 0)

    r = lax.broadcasted_iota(I32, (local, ts), 0)
    hit = (r == ls_ref[0:1, :]) | (r == ls_ref[1:2, :]) | (r == ls_ref[2:3, :]) | (r == ls_ref[3:4, :])
    hit = hit.astype(BF16)
    for pb in range(PANELS):
        srt = jnp.dot(hit, h_ref[:, pb * PANEL_COLS:(pb + 1) * PANEL_COLS], preferred_element_type=F32)
        buf[slot, pb] = _pack_panel(srt, exact=True)

    @pl.when(b > 0)
    def _():
        for_chunks(b - 1, lambda c, p: chunk_copy(1 - slot, b - 1, c).wait())

    for_chunks(b, lambda c, p: chunk_copy(slot, b, c).start(priority=p))

    @pl.when(b == pl.num_programs(0) - 1)
    def _():
        for_chunks(b, lambda c, p: chunk_copy(slot, b, c).wait())


def _dispatch(tab, pad, nvt, h2, ls, n_rows):
    t, d = h2.shape
    nb = tab.shape[0] // TABW
    ts = t // nb
    return pl.pallas_call(
        _dispatch_kernel,
        grid_spec=pltpu.PrefetchScalarGridSpec(
            num_scalar_prefetch=3,
            grid=(nb,),
            in_specs=[pl.BlockSpec((ts, d), lambda i, *_: (i, 0)),
                      pl.BlockSpec((TOP_K, ts), lambda i, *_: (0, i))],
            out_specs=pl.BlockSpec(memory_space=pl.ANY),
            scratch_shapes=[pltpu.VMEM((2, PANELS, _local_rows(ts), LANES), U32),
                            pltpu.VMEM((PANELS, FFN_ROWS + RUN, LANES), U32),
                            pltpu.SemaphoreType.DMA((2,)), pltpu.SemaphoreType.DMA],
        ),
        out_shape=jax.ShapeDtypeStruct((PANELS, n_rows, LANES), U32),
        compiler_params=_cparams(1, ROW_VMEM_BYTES),
        name="dispatch",
    )(tab, pad, nvt, h2, ls)


def _ffn_kernel(te_ref, nv_ref, nx_ref, xs_ref, wgu_hbm, bgu_ref, wd_hbm, bd_ref, perm_ref, ys_ref,
                wgu_stage, wd_stage, wg_scr, wu_scr, wd_scr, bg_scr, bu_scr, sems):
    p = pl.program_id(0)
    t0 = 2 * p
    t1 = t0 + 1
    e0 = te_ref[t0]
    e1 = te_ref[t1]
    v0 = t0 < nv_ref[0]
    v1 = t1 < nv_ref[0]
    new0 = (p == 0) | (e0 != te_ref[jnp.maximum(t0 - 1, 0)])
    same = v1 & (e1 == e0)

    def stage_copies(e):
        return (pltpu.make_async_copy(wgu_hbm.at[e], wgu_stage, sems.at[0]),
                pltpu.make_async_copy(wd_hbm.at[e], wd_stage, sems.at[1]))

    def load_expert(t, first):
        e = te_ref[t]
        if first:
            @pl.when(p == 0)
            def _():
                for cp in stage_copies(e):
                    cp.start()

        for cp in stage_copies(e):
            cp.wait()
        bias = bgu_ref[e]
        for c in range(2 * D_FF // PERM):
            cols = slice(c * PERM, (c + 1) * PERM)
            half = slice(c * (PERM // 2), (c + 1) * (PERM // 2))
            w = wgu_stage[:, cols].astype(BF16)
            pw = jnp.dot(w, perm_ref[...], preferred_element_type=F32).astype(BF16)
            wg_scr[:, half] = pw[:, :PERM // 2]
            wu_scr[:, half] = pw[:, PERM // 2:]
            b1 = bias[:, cols].astype(BF16)
            r1 = bias[:, cols] - b1.astype(F32)
            b2 = r1.astype(BF16)
            b3 = (r1 - b2.astype(F32)).astype(BF16)
            terms = jnp.concatenate([b1, b2, b3, jnp.zeros((5, PERM), BF16)], axis=0)
            pb = jnp.sum(jnp.dot(terms, perm_ref[...], preferred_element_type=F32), axis=0, keepdims=True)
            bg_scr[:, half] = pb[:, :PERM // 2]
            bu_scr[:, half] = pb[:, PERM // 2:]
        wd_scr[...] = wd_stage[...].astype(BF16)

        @pl.when(nx_ref[t] >= 0)
        def _():
            for cp in stage_copies(nx_ref[t]):
                cp.start()

    def run(lo, n, e):
        x = _unpack_panels([xs_ref[pb, lo:lo + n, :] for pb in range(PANELS)])
        gate = jnp.dot(x, wg_scr[...], preferred_element_type=F32) + bg_scr[...]
        up = jnp.dot(x, wu_scr[...], preferred_element_type=F32) + bu_scr[...]
        gate = jnp.minimum(gate, SWIGLU_LIMIT)
        up = jnp.clip(up, -SWIGLU_LIMIT, SWIGLU_LIMIT)
        act = ((up + 1.0) * (gate * jax.nn.sigmoid(SWIGLU_ALPHA * gate))).astype(BF16)
        bd = bd_ref[e]
        for pb in range(PANELS):
            cols = slice(pb * PANEL_COLS, (pb + 1) * PANEL_COLS)
            y = jnp.dot(act, wd_scr[:, cols], preferred_element_type=F32) + bd[:, cols]
            ys_ref[pb, lo:lo + n, :] = _pack_panel(y)

    @pl.when(v0 & new0)
    def _():
        load_expert(t0, True)

    @pl.when(same)
    def _():
        run(0, 2 * FFN_ROWS, e0)

    @pl.when(v0 & jnp.logical_not(same))
    def _():
        run(0, FFN_ROWS, e0)

    @pl.when(v1 & jnp.logical_not(same))
    def _():
        load_expert(t1, False)
        run(FFN_ROWS, FFN_ROWS, e1)

    @pl.when(v0 & jnp.logical_not(v1))
    def _():
        ys_ref[:, FFN_ROWS:, :] = xs_ref[:, FFN_ROWS:, :]


def _ffn(te, nv, nx, xs, w_gate_up, bgu, w_down, bd, n_tiles):
    d = D_MODEL
    pair = lambda i, te, nv, nx: (0, jnp.minimum(i, lax.shift_right_logical(nv[0] - 1, 1)), 0)
    whole = lambda i, te, nv, nx: (0, 0, 0)
    r = lax.broadcasted_iota(I32, (PERM, PERM), 0)
    c = lax.broadcasted_iota(I32, (PERM, PERM), 1)
    perm = (r == jnp.where(c < PERM // 2, 2 * c, 2 * (c - PERM // 2) + 1)).astype(BF16)
    return pl.pallas_call(
        _ffn_kernel,
        grid_spec=pltpu.PrefetchScalarGridSpec(
            num_scalar_prefetch=3,
            grid=(n_tiles // 2,),
            in_specs=[pl.BlockSpec((PANELS, 2 * FFN_ROWS, LANES), pair),
                      pl.BlockSpec(memory_space=pl.ANY),
                      pl.BlockSpec((N_EXPERTS, 1, 2 * D_FF), whole),
                      pl.BlockSpec(memory_space=pl.ANY),
                      pl.BlockSpec((N_EXPERTS, 1, d), whole),
                      pl.BlockSpec((PERM, PERM), lambda i, te, nv, nx: (0, 0))],
            out_specs=pl.BlockSpec((PANELS, 2 * FFN_ROWS, LANES), pair),
            scratch_shapes=[pltpu.VMEM((d, 2 * D_FF), F32), pltpu.VMEM((D_FF, d), F32),
                            pltpu.VMEM((d, D_FF), BF16), pltpu.VMEM((d, D_FF), BF16), pltpu.VMEM((D_FF, d), BF16),
                            pltpu.VMEM((1, D_FF), F32), pltpu.VMEM((1, D_FF), F32),
                            pltpu.SemaphoreType.DMA((2,))],
        ),
        out_shape=jax.ShapeDtypeStruct(xs.shape, U32),
        input_output_aliases={3: 0},
        compiler_params=_cparams(1, FFN_VMEM_BYTES),
        name="ffn",
    )(te, nv, nx, xs, w_gate_up, bgu, w_down, bd, perm)


def _combine_kernel(tab_ref, x1_ref, ls_ref, w_ref, mod_ref, ys_ref, o4_hbm, ybuf, ob_buf, sems, osems, *, n_blk):
    jj = pl.program_id(1)
    blk = pl.program_id(0) * pl.num_programs(1) + jj
    slot = blk % 2
    nc = x1_ref.shape[2]
    tt = POST_POS * nc
    d = x1_ref.shape[3]
    local = ybuf.shape[2]

    def chunk_copy(sl, bk, c):
        rows = pl.ds(pl.multiple_of(c * RUN, RUN), RUN)
        return pltpu.make_async_copy(ys_ref.at[:, pl.ds(tab_ref[bk * TABW + c], RUN), :], ybuf.at[sl, :, rows, :],
                                     sems.at[sl])

    def for_chunks(bk, fn):
        _for_chunk_pairs(tab_ref[bk * TABW + TABW - 1], fn)

    @pl.when(blk == 0)
    def _():
        ybuf[...] = jnp.zeros_like(ybuf)
        for_chunks(0, lambda c, p: chunk_copy(0, 0, c).start(priority=p))

    @pl.when(blk + 1 < n_blk)
    def _():
        for_chunks(blk + 1, lambda c, p: chunk_copy(1 - slot, blk + 1, c).start(priority=p))

    for_chunks(blk, lambda c, p: chunk_copy(slot, blk, c).wait())

    to_cols = lambda a: jnp.concatenate([a, jnp.zeros_like(a)], axis=0).T
    ls_c = to_cols(ls_ref[...].astype(F32))
    w_c = to_cols(w_ref[...])
    r = lax.broadcasted_iota(I32, (tt, local), 1).astype(F32)
    wm = jnp.zeros((tt, local), F32)
    for k in range(TOP_K):
        wm = wm + jnp.where(r == ls_c[:, k:k + 1], w_c[:, k:k + 1], 0.0)
    wm = wm.astype(BF16)
    acc = jnp.dot(wm, _unpack_panels([ybuf[slot, pb] for pb in range(PANELS)]), preferred_element_type=F32)
    out = x1_ref[0].reshape(tt, d) + mod_ref[0, 5:6, :] * acc

    def out_copies(sl, b_, j_):
        return [pltpu.make_async_copy(ob_buf.at[sl, il], o4_hbm.at[b_, :, POST_POS * j_ + il, :], osems.at[sl])
                for il in range(POST_POS)]

    @pl.when(blk >= 2)
    def _():
        for cp in out_copies(slot, 0, 0):
            cp.wait()

    for il in range(POST_POS):
        ob_buf[slot, il] = out[il * nc:(il + 1) * nc]
    for cp in out_copies(slot, pl.program_id(0), jj):
        cp.start()

    @pl.when(blk == n_blk - 1)
    def _():
        for cp in out_copies(slot, 0, 0):
            cp.wait()
        if n_blk > 1:
            for cp in out_copies(1 - slot, 0, 0):
                cp.wait()


def _combine(tab, x1, ls, wts, mod, ys):
    b, _, nc, d = x1.shape
    s = SSM_CHUNK * nc
    tt = POST_POS * nc
    nt = SSM_CHUNK // POST_POS
    o4 = pl.pallas_call(
        functools.partial(_combine_kernel, n_blk=b * nt),
        grid_spec=pltpu.PrefetchScalarGridSpec(
            num_scalar_prefetch=1,
            grid=(b, nt),
            in_specs=[pl.BlockSpec((1, POST_POS, nc, d), lambda bi, j, *_: (bi, j, 0, 0)),
                      pl.BlockSpec((TOP_K, tt), lambda bi, j, *_: (0, bi * nt + j)),
                      pl.BlockSpec((TOP_K, tt), lambda bi, j, *_: (0, bi * nt + j)),
                      pl.BlockSpec((1, 6, d), lambda bi, j, *_: (bi, 0, 0)),
                      pl.BlockSpec(memory_space=pl.ANY)],
            out_specs=pl.BlockSpec(memory_space=pl.ANY),
            scratch_shapes=[pltpu.VMEM((2, PANELS, _local_rows(tt), LANES), U32),
                            pltpu.VMEM((2, POST_POS, nc, d), F32),
                            pltpu.SemaphoreType.DMA((2,)), pltpu.SemaphoreType.DMA((2,))],
        ),
        out_shape=jax.ShapeDtypeStruct((b, nc, SSM_CHUNK, d), F32),
        compiler_params=_cparams(2),
        name="combine",
    )(tab, x1, ls, wts, mod, ys)
    return o4.reshape(b, s, d)


def kernel(x, c, w_ada, b_ada, norm_mix, w_in, b_in, q_norm, k_norm, sinks, lam_re, lam_im, log_dt, b_re, b_im,
           c_re, c_im, d_skip, w_glu, b_glu, attn_out_norm, ssm_out_norm, w_out, norm_ffn, w_router, b_router,
           w_gate_up, b_gate_up, w_down, b_down):
    b, s, d = x.shape
    t = b * s
    depth = w_ada.shape[0]
    n_tiles = -(-(t * TOP_K + N_EXPERTS * (RUN - 1 + FFN_ROWS - 1)) // FFN_ROWS)
    n_tiles += n_tiles % 2
    n_alloc = n_tiles + 2
    for l in range(depth):
        mod = _adaln(c, w_ada[l], b_ada[l]).reshape(b, 6, d)
        q, k, v, ut = _inproj(x, mod, norm_mix[l], w_in[l], b_in[l])
        attn = _attention(q, k, v, sinks[l], q_norm[l], k_norm[l], attn_out_norm[l])
        tt, wz, wyt, cs = _ssm_params(lam_re[l], lam_im[l], log_dt[l], b_re[l], b_im[l], c_re[l], c_im[l])
        yt = _ssm(ut, tt, wz, wyt, cs, d_skip[l])
        x1, h2, eidx, wts, lrank, r0, cnt = _post(x, attn, yt, mod, w_glu[l], b_glu[l], ssm_out_norm[l], w_out[l],
                                                  norm_ffn[l], w_router[l], b_router[l])
        ls, tab, te, nv, nx, pad = _route(eidx, lrank, r0, cnt, n_tiles)
        tab = tab.reshape(-1)
        nvt = jnp.stack([nv[0, 0], jnp.int32(n_alloc)])
        xs = _dispatch(tab, pad.reshape(-1), nvt, h2.reshape(t, d), ls, n_alloc * FFN_ROWS)
        ys = _ffn(te[0, :n_tiles], nv[0, :1], nx[0, :n_tiles], xs, w_gate_up[l], b_gate_up[l][:, None, :],
                  w_down[l], b_down[l][:, None, :], n_tiles)
        x = _combine(tab, x1, ls, wts, mod, ys)
    return x
```

```python
import functools
import math

import jax
import jax.numpy as jnp
from jax import lax
from jax.experimental import pallas as pl
from jax.experimental.pallas import tpu as pltpu

F32 = jnp.float32
BF16 = jnp.bfloat16
U32 = jnp.uint32
I32 = jnp.int32

D_MODEL = 1024
HEAD_DIM = 64
N_HEADS = 8
N_KV_HEADS = 2
Q_PER_KV = N_HEADS // N_KV_HEADS
D_ATTN = N_HEADS * HEAD_DIM
D_KV = N_KV_HEADS * HEAD_DIM
D_QKV = D_ATTN + 2 * D_KV
WINDOW = 128
BLOCK = 128
D_SSM = D_MODEL - D_ATTN
SSM_GROUP = 16
N_GROUPS = D_SSM // SSM_GROUP
STATE = 64
N_EXPERTS = 32
TOP_K = 4
D_FF = D_MODEL
SWIGLU_LIMIT = 7.0
SWIGLU_ALPHA = 1.702
EPS = 1e-6
NEG_INF = -1e30

LANES = 128
SSM_CHUNK = 16
SSM_ROW = SSM_CHUNK * SSM_GROUP
N_POW = 2 * SSM_CHUNK
PANEL_COLS = 2 * LANES
PANELS = D_MODEL // PANEL_COLS

SSM_GROUPS_PER_STEP = 4
POS_PER_STEP = 8
ATTN_ROWS = 512
POST_POS = 2
POST_SUB = 2
FFN_ROWS = 256
RUN = 16
RUN_SHIFT = 4
TABW = 128
PERM = 256
FFN_VMEM_BYTES = 40 * 1024 * 1024
ROW_VMEM_BYTES = 48 * 1024 * 1024

HIGHEST = lax.Precision.HIGHEST
_ARB = "arbitrary"


def _cparams(n, vmem=None):
    return pltpu.CompilerParams(dimension_semantics=(_ARB,) * n, vmem_limit_bytes=vmem)


def _rms(x, axis=-1):
    return x * lax.rsqrt(jnp.mean(x * x, axis=axis, keepdims=True) + EPS)


def _pack_panel(y, exact=False):
    hi, lo = y[:, :LANES], y[:, LANES:]
    if not exact:
        hi = hi.astype(BF16).astype(F32)
        lo = lo.astype(BF16).astype(F32)
    return lax.bitcast_convert_type(hi, U32) | (lax.bitcast_convert_type(lo, U32) >> 16)


def _unpack_panels(words):
    cols = []
    for w in words:
        cols.append(lax.bitcast_convert_type(w & jnp.uint32(0xFFFF0000), F32).astype(BF16))
        cols.append(lax.bitcast_convert_type(w << 16, F32).astype(BF16))
    return jnp.concatenate(cols, axis=-1)


def _prefetch_pos_rows(x4_hbm, buf, sems, n_pos):
    bi = pl.program_id(0)
    j = pl.program_id(1)
    nj = pl.num_programs(1)
    g = bi * nj + j
    slot = g % 2

    def copies(sl, b_, j_):
        return [pltpu.make_async_copy(x4_hbm.at[b_, :, n_pos * j_ + il, :], buf.at[sl, il], sems.at[sl])
                for il in range(n_pos)]

    @pl.when(g == 0)
    def _():
        for cp in copies(0, 0, 0):
            cp.start()

    @pl.when(g + 1 < pl.num_programs(0) * nj)
    def _():
        wrap = j + 1 == nj
        for cp in copies(1 - slot, jnp.where(wrap, bi + 1, bi), jnp.where(wrap, 0, j + 1)):
            cp.start()

    for cp in copies(slot, bi, j):
        cp.wait()
    return slot


def _to_lane_blocks(dst, src):
    for kb in range(dst.shape[0]):
        dst[kb] = src[:, kb * LANES:(kb + 1) * LANES]


def _adaln_kernel(c_ref, w_ref, b_ref, o_ref):
    c = c_ref[...]
    ca = c * jax.nn.sigmoid(c)
    o_ref[...] = jnp.dot(ca, w_ref[...], preferred_element_type=F32, precision=HIGHEST) + b_ref[...]


def _adaln(c, w_ada, b_ada):
    b, d = c.shape
    n = w_ada.shape[1] // d
    return pl.pallas_call(
        _adaln_kernel,
        grid=(n,),
        in_specs=[pl.BlockSpec((b, d), lambda j: (0, 0)),
                  pl.BlockSpec((d, d), lambda j: (0, j)),
                  pl.BlockSpec((1, d), lambda j: (0, j))],
        out_specs=pl.BlockSpec((b, d), lambda j: (0, j)),
        out_shape=jax.ShapeDtypeStruct((b, n * d), F32),
        compiler_params=_cparams(1),
        name="adaln",
    )(c, w_ada, b_ada.reshape(1, -1))


def _inproj_kernel(x4_hbm, x_ref, mod_ref, g_ref, wqkv_ref, bqkv_ref, wut_ref, but_ref, q_ref, k_ref, v_ref, ut_ref,
                   xp_buf, sems):
    nc = ut_ref.shape[3]
    slot = _prefetch_pos_rows(x4_hbm, xp_buf, sems, POS_PER_STEP)
    gain = g_ref[...]
    scale = 1.0 + mod_ref[0, 1:2, :]
    shift = mod_ref[0, 0:1, :]

    def norm_mod(x):
        return (_rms(x) * gain * scale + shift).astype(BF16)

    proj = jnp.dot(norm_mod(x_ref[0]), wqkv_ref[...], preferred_element_type=F32) + bqkv_ref[...]
    q_ref[0] = proj[:, :D_ATTN].astype(BF16)
    k_ref[0] = proj[:, D_ATTN:D_ATTN + D_KV].astype(BF16)
    v_ref[0] = proj[:, D_ATTN + D_KV:].astype(BF16)

    hs = jnp.concatenate([norm_mod(xp_buf[slot, il]) for il in range(POS_PER_STEP)], axis=0)
    ut = lax.dot_general(wut_ref[...], hs, (((1,), (1,)), ((), ())), preferred_element_type=F32) + but_ref[...]
    for il in range(POS_PER_STEP):
        piece = ut[:, il * nc:(il + 1) * nc].astype(BF16)
        ut_ref[0, :, il * SSM_GROUP:(il + 1) * SSM_GROUP, :] = piece.reshape(N_GROUPS, SSM_GROUP, nc)


def _inproj(x, mod, gain, w_in, b_in):
    b, s, d = x.shape
    nc = s // SSM_CHUNK
    rows = POS_PER_STEP * nc
    row = lambda bi, j: (bi, j, 0)
    const = lambda bi, j: (0, 0)
    w_qkv = w_in[:, :D_QKV].astype(BF16)
    w_ut = w_in[:, D_QKV:].T.astype(BF16)
    return pl.pallas_call(
        _inproj_kernel,
        grid=(b, SSM_CHUNK // POS_PER_STEP),
        in_specs=[pl.BlockSpec(memory_space=pl.ANY),
                  pl.BlockSpec((1, rows, d), row),
                  pl.BlockSpec((1, 6, d), lambda bi, j: (bi, 0, 0)),
                  pl.BlockSpec((1, d), const),
                  pl.BlockSpec((d, D_QKV), const),
                  pl.BlockSpec((1, D_QKV), const),
                  pl.BlockSpec((D_SSM, d), const),
                  pl.BlockSpec((D_SSM, 1), const)],
        out_specs=[pl.BlockSpec((1, rows, D_ATTN), row),
                   pl.BlockSpec((1, rows, D_KV), row),
                   pl.BlockSpec((1, rows, D_KV), row),
                   pl.BlockSpec((1, N_GROUPS, POS_PER_STEP * SSM_GROUP, nc), lambda bi, j: (bi, 0, j, 0))],
        out_shape=[jax.ShapeDtypeStruct((b, s, D_ATTN), BF16),
                   jax.ShapeDtypeStruct((b, s, D_KV), BF16),
                   jax.ShapeDtypeStruct((b, s, D_KV), BF16),
                   jax.ShapeDtypeStruct((b, N_GROUPS, SSM_ROW, nc), BF16)],
        scratch_shapes=[pltpu.VMEM((2, POS_PER_STEP, nc, d), F32), pltpu.SemaphoreType.DMA((2,))],
        compiler_params=_cparams(2),
        name="inproj",
    )(x.reshape(b, nc, SSM_CHUNK, d), x, mod, gain.reshape(1, d), w_qkv, b_in[:D_QKV].reshape(1, D_QKV), w_ut, b_in[D_QKV:].reshape(D_SSM, 1))


def _half_norm(x, low):
    sq = x * x
    s_lo = jnp.sum(jnp.where(low, sq, 0.0), axis=-1, keepdims=True)
    s_hi = jnp.sum(sq, axis=-1, keepdims=True) - s_lo
    inv = 1.0 / HEAD_DIM
    scale = jnp.where(low, lax.rsqrt(s_lo * inv + EPS), lax.rsqrt(s_hi * inv + EPS))
    return x * scale


def _attn_block(first, q, k_prev, k_cur, v_prev, v_cur, sinks_ref, qn, low, upper, rblk):
    no_prev = jnp.where(first, NEG_INF, 0.0)
    out_blocks = []
    for hk in range(N_KV_HEADS):
        qs = []
        for j in range(Q_PER_KV // 2):
            blk = hk * (Q_PER_KV // 2) + j
            qb = _half_norm(q[:, blk * LANES:(blk + 1) * LANES], low) * qn * (1.0 / math.sqrt(HEAD_DIM))
            qs.append(jnp.where(low, qb, 0.0))
            qs.append(jnp.where(low, 0.0, qb))
        qg = jnp.concatenate(qs, axis=0).astype(BF16)
        nt = (((1,), (1,)), ((), ()))
        s_prev = lax.dot_general(qg, k_prev[hk], nt, preferred_element_type=F32)
        s_cur = lax.dot_general(qg, k_cur[hk], nt, preferred_element_type=F32)
        s = jnp.where(upper, s_prev + no_prev, s_cur)
        sink = jnp.zeros((Q_PER_KV * BLOCK, 1), F32)
        for g in range(Q_PER_KV):
            sink = jnp.where(rblk == g, sinks_ref[hk * Q_PER_KV + g], sink)
        m = jnp.maximum(jnp.max(s, axis=-1, keepdims=True), sink)
        p = jnp.exp(s - m)
        den = jnp.sum(p, axis=-1, keepdims=True) + jnp.exp(sink - m)
        o = (jnp.dot(jnp.where(upper, p, 0.0).astype(BF16), v_prev[hk], preferred_element_type=F32)
             + jnp.dot(jnp.where(upper, 0.0, p).astype(BF16), v_cur[hk], preferred_element_type=F32)) / den
        for j in range(Q_PER_KV // 2):
            ev = o[(2 * j) * BLOCK:(2 * j + 1) * BLOCK]
            od = o[(2 * j + 1) * BLOCK:(2 * j + 2) * BLOCK]
            out_blocks.append(jnp.where(low, ev, od))
    return jnp.concatenate(out_blocks, axis=-1)


def _attn_kernel(sinks_ref, q_ref, k_ref, v_ref, qn_ref, kn_ref, on_ref, o_hbm, a_buf, sems, *, n_steps):
    step = pl.program_id(1)
    g = pl.program_id(0) * pl.num_programs(1) + step
    slot = g % 2
    cps = ATTN_ROWS // SSM_CHUNK
    nq = ATTN_ROWS // BLOCK

    def out_copies(sl, b_, s_):
        return [pltpu.make_async_copy(a_buf.at[sl, :, i, :], o_hbm.at[b_, i, pl.ds(s_ * cps, cps), :], sems.at[sl])
                for i in range(SSM_CHUNK)]

    @pl.when(g >= 2)
    def _():
        for cp in out_copies(slot, 0, 0):
            cp.wait()

    low = lax.broadcasted_iota(I32, (1, LANES), 1) < HEAD_DIM
    rows = Q_PER_KV * BLOCK
    upper = lax.broadcasted_iota(I32, (rows, BLOCK), 1) > lax.broadcasted_iota(I32, (rows, BLOCK), 0) % BLOCK
    rblk = lax.broadcasted_iota(I32, (rows, 1), 0) // BLOCK

    cur = pl.multiple_of(step * ATTN_ROWS, ATTN_ROWS)
    prev = pl.multiple_of(jnp.maximum(step * nq - 1, 0) * BLOCK, BLOCK)
    kall = jnp.concatenate([k_ref[0, pl.ds(prev, BLOCK), :], k_ref[0, pl.ds(cur, ATTN_ROWS), :]], axis=0).astype(F32)
    vall = jnp.concatenate([v_ref[0, pl.ds(prev, BLOCK), :], v_ref[0, pl.ds(cur, ATTN_ROWS), :]], axis=0).astype(F32)
    kall = _half_norm(kall, low) * kn_ref[...]
    kswap = pltpu.roll(kall, HEAD_DIM, axis=1)
    vswap = pltpu.roll(vall, HEAD_DIM, axis=1)
    k_dup = [jnp.where(low, kall, kswap).astype(BF16), jnp.where(low, kswap, kall).astype(BF16)]
    v_dup = [jnp.where(low, vall, vswap).astype(BF16), jnp.where(low, vswap, vall).astype(BF16)]
    blk = lambda a, i: [a[hk][i * BLOCK:(i + 1) * BLOCK] for hk in range(N_KV_HEADS)]

    for qb in range(nq):
        q = q_ref[0, qb * BLOCK:(qb + 1) * BLOCK, :].astype(F32)
        attn = _attn_block((step == 0) if qb == 0 else False, q, blk(k_dup, qb), blk(k_dup, qb + 1),
                           blk(v_dup, qb), blk(v_dup, qb + 1), sinks_ref, qn_ref[...], low, upper, rblk)
        attn = _rms(attn) * on_ref[...]
        cpb = BLOCK // SSM_CHUNK
        a_buf[slot, qb * cpb:(qb + 1) * cpb] = attn.reshape(cpb, SSM_CHUNK, D_ATTN)

    for cp in out_copies(slot, pl.program_id(0), step):
        cp.start()

    @pl.when(g == n_steps - 1)
    def _():
        for cp in out_copies(slot, 0, 0):
            cp.wait()
        if n_steps > 1:
            for cp in out_copies(1 - slot, 0, 0):
                cp.wait()


def _attention(q, k, v, sinks, q_norm, k_norm, out_norm):
    b, s, _ = q.shape
    tile2 = lambda g: jnp.tile(g.reshape(1, HEAD_DIM), (1, 2))
    cps = ATTN_ROWS // SSM_CHUNK
    return pl.pallas_call(
        functools.partial(_attn_kernel, n_steps=b * (s // ATTN_ROWS)),
        grid=(b, s // ATTN_ROWS),
        in_specs=[pl.BlockSpec(memory_space=pltpu.SMEM),
                  pl.BlockSpec((1, ATTN_ROWS, D_ATTN), lambda bi, n: (bi, n, 0)),
                  pl.BlockSpec((1, s, D_KV), lambda bi, n: (bi, 0, 0)),
                  pl.BlockSpec((1, s, D_KV), lambda bi, n: (bi, 0, 0)),
                  pl.BlockSpec((1, LANES), lambda bi, n: (0, 0)),
                  pl.BlockSpec((1, LANES), lambda bi, n: (0, 0)),
                  pl.BlockSpec((1, D_ATTN), lambda bi, n: (0, 0))],
        out_specs=pl.BlockSpec(memory_space=pl.ANY),
        out_shape=jax.ShapeDtypeStruct((b, SSM_CHUNK, s // SSM_CHUNK, D_ATTN), F32),
        scratch_shapes=[pltpu.VMEM((2, cps, SSM_CHUNK, D_ATTN), F32), pltpu.SemaphoreType.DMA((2,))],
        compiler_params=_cparams(2),
        name="attention",
    )(sinks, q, k, v, tile2(q_norm), tile2(k_norm), out_norm.reshape(1, D_ATTN))


def _cmul(ar, ai, br, bi):
    return ar * br - ai * bi, ar * bi + ai * br


def _ssm_param_kernel(lam_ref, bre_ref, bim_ref, cre_ref, cim_ref, tt_ref, wz_ref, wyt_ref, cs_ref):
    f32dot = functools.partial(jnp.dot, preferred_element_type=F32, precision=HIGHEST)
    lr = lam_ref[0, 0:1, :]
    li = lam_ref[0, 1:2, :]
    dt = jnp.exp(lam_ref[0, 2:3, :])
    rho = lr * dt
    th = li * dt
    imag_lane = lax.broadcasted_iota(I32, (1, LANES), 1) >= STATE

    kk = (lax.broadcasted_iota(I32, (N_POW, 1), 0) - (SSM_CHUNK - 1)).astype(F32)
    mag = jnp.exp(rho * kk)
    pw_r = mag * jnp.cos(th * kk)
    pw_i = mag * jnp.sin(th * kk)
    lb_r = pw_r[SSM_CHUNK:SSM_CHUNK + 1]
    lb_i = pw_i[SSM_CHUNK:SSM_CHUNK + 1]
    den = lr * lr + li * li
    coef_r = ((lb_r - 1.0) * lr + lb_i * li) / den
    coef_i = (lb_i * lr - (lb_r - 1.0) * li) / den

    eye = (lax.broadcasted_iota(I32, (SSM_GROUP, SSM_GROUP), 0)
           == lax.broadcasted_iota(I32, (SSM_GROUP, SSM_GROUP), 1)).astype(F32)
    lane_fold = (lax.broadcasted_iota(I32, (STATE, LANES), 1) % STATE
                 == lax.broadcasted_iota(I32, (STATE, LANES), 0)).astype(F32)

    def tile_pos(x):
        return jnp.concatenate([x] * SSM_CHUNK, axis=0)

    def power_rows(k_of_pos):
        idx = [k_of_pos(p) + (SSM_CHUNK - 1) for p in range(SSM_CHUNK)]
        rep = lambda t: jnp.concatenate([jnp.broadcast_to(t[r:r + 1], (SSM_GROUP, LANES)) for r in idx], axis=0)
        return rep(pw_r), rep(pw_i)

    def b_rows(b_ref):
        b2 = jnp.concatenate([b_ref[0], b_ref[0]], axis=0)
        return tile_pos(lax.dot_general(eye, b2, (((1,), (1,)), ((), ())), preferred_element_type=F32,
                                        precision=HIGHEST))

    def c_rows(c_ref):
        return tile_pos(f32dot(c_ref[0], lane_fold))

    bbar_r, bbar_i = _cmul(coef_r, coef_i, b_rows(bre_ref), b_rows(bim_ref))
    c_r = c_rows(cre_ref)
    c_i = c_rows(cim_ref)

    a_r, a_i = _cmul(bbar_r, bbar_i, *power_rows(lambda p: -p))
    a2c = jnp.where(imag_lane, -a_i, a_r)
    m_r, m_i = _cmul(c_r, c_i, *power_rows(lambda p: p))
    bmc = jnp.where(imag_lane, m_i, m_r)
    tt = f32dot(bmc, a2c.T)
    causal = (lax.broadcasted_iota(I32, (SSM_ROW, 1), 0) // SSM_GROUP
              >= lax.broadcasted_iota(I32, (1, SSM_ROW), 1) // SSM_GROUP)
    tt_ref[0] = jnp.where(causal, tt, 0.0).astype(BF16)

    w_r, w_i = _cmul(bbar_r, bbar_i, *power_rows(lambda p: SSM_CHUNK - 1 - p))
    wz_ref[0, :, :LANES] = jnp.where(imag_lane, w_i, w_r).astype(BF16)
    wz_ref[0, :, LANES:] = jnp.where(imag_lane, w_r, w_i).astype(BF16)

    y_r, y_i = _cmul(c_r, c_i, *power_rows(lambda p: p + 1))
    wyt_ref[0] = jnp.where(imag_lane, -y_i, y_r).astype(BF16)

    cs_ref[0, 0:1, :] = pw_r[N_POW - 1:N_POW]
    cs_ref[0, 1:2, :] = jnp.where(imag_lane, pw_i[N_POW - 1:N_POW], -pw_i[N_POW - 1:N_POW])


def _ssm_params(lam_re, lam_im, log_dt, b_re, b_im, c_re, c_im):
    g = lam_re.shape[0]
    lam = jnp.stack([lam_re, lam_im, jnp.broadcast_to(log_dt[:, None], (g, STATE))], axis=1)
    lam = jnp.concatenate([lam, lam], axis=2)
    blk = lambda *shape: pl.BlockSpec((1,) + shape, lambda i: (i, 0, 0))
    return pl.pallas_call(
        _ssm_param_kernel,
        grid=(g,),
        in_specs=[blk(3, LANES), blk(STATE, SSM_GROUP), blk(STATE, SSM_GROUP), blk(SSM_GROUP, STATE),
                  blk(SSM_GROUP, STATE)],
        out_specs=[blk(SSM_ROW, SSM_ROW), blk(SSM_ROW, SSM_ROW), blk(SSM_ROW, LANES), blk(2, LANES)],
        out_shape=[jax.ShapeDtypeStruct((g, SSM_ROW, SSM_ROW), BF16),
                   jax.ShapeDtypeStruct((g, SSM_ROW, SSM_ROW), BF16),
                   jax.ShapeDtypeStruct((g, SSM_ROW, LANES), BF16),
                   jax.ShapeDtypeStruct((g, 2, LANES), F32)],
        compiler_params=_cparams(1),
        name="ssm_params",
    )(lam, b_re, b_im, c_re, c_im)


def _ssm_kernel(ut_ref, tt_ref, wz_ref, wyt_ref, cs_ref, d_ref, yt_ref, z_scr, s_scr):
    batch, ng, _, nc = ut_ref.shape
    uts = [jnp.concatenate([ut_ref[b, gi] for b in range(batch)], axis=1) for gi in range(ng)]
    for gi in range(ng):
        z = lax.dot_general(uts[gi], wz_ref[gi], (((0,), (0,)), ((), ())), preferred_element_type=F32)
        _to_lane_blocks(z_scr.at[gi], z)
    c1 = [cs_ref[gi, 0:1, :] for gi in range(ng)]
    c2 = [cs_ref[gi, 1:2, :] for gi in range(ng)]

    def step(c, carry):
        rows = pl.ds(c, batch, stride=nc)
        out = []
        for gi in range(ng):
            s1, s2 = carry[gi]
            s_scr[gi, rows, :] = s1
            out.append((c1[gi] * s1 + c2[gi] * s2 + z_scr[gi, 0, rows, :],
                        c1[gi] * s2 - c2[gi] * s1 + z_scr[gi, 1, rows, :]))
        return tuple(out)

    zero = jnp.zeros((batch, LANES), F32)
    lax.fori_loop(0, nc, step, ((zero, zero),) * ng, unroll=8)
    for gi in range(ng):
        y = jnp.dot(tt_ref[gi], uts[gi], preferred_element_type=F32)
        y = y + lax.dot_general(wyt_ref[gi], s_scr[gi].astype(BF16), (((1,), (1,)), ((), ())),
                                preferred_element_type=F32)
        y = y + d_ref[gi] * uts[gi].astype(F32)
        for b in range(batch):
            yt_ref[b, gi] = y[:, b * nc:(b + 1) * nc]


def _ssm(ut, tt, wz, wyt, cs, d_skip):
    b, g, _, nc = ut.shape
    ng = SSM_GROUPS_PER_STEP
    d_col = jnp.tile(d_skip.reshape(g, 1, SSM_GROUP), (1, SSM_CHUNK, 1)).reshape(g, SSM_ROW, 1)
    blk = lambda *shape: pl.BlockSpec((ng,) + shape, lambda i: (i, 0, 0))
    act = pl.BlockSpec((b, ng, SSM_ROW, nc), lambda i: (0, i, 0, 0))
    return pl.pallas_call(
        _ssm_kernel,
        grid=(g // ng,),
        in_specs=[act, blk(SSM_ROW, SSM_ROW), blk(SSM_ROW, SSM_ROW), blk(SSM_ROW, LANES), blk(2, LANES),
                  blk(SSM_ROW, 1)],
        out_specs=act,
        out_shape=jax.ShapeDtypeStruct((b, g, SSM_ROW, nc), F32),
        scratch_shapes=[pltpu.VMEM((ng, SSM_ROW // LANES, b * nc, LANES), F32), pltpu.VMEM((ng, b * nc, LANES), F32)],
        compiler_params=_cparams(1),
        name="ssm",
    )(ut, tt, wz, wyt, cs, d_col)


def _post_kernel(x4_hbm, attn_ref, yt_ref, mod_ref, wglut_ref, bglu_ref, sn_ref, wout_ref, nf_ref, wr_ref, br_ref,
                 tri_ref, x1_ref, h2_ref, eidx_ref, wts_ref, lrank_ref, r0_ref, cnt_ref, carry_ref, xp_buf, sems):
    @pl.when((pl.program_id(0) == 0) & (pl.program_id(1) == 0))
    def _():
        carry_ref[...] = jnp.zeros_like(carry_ref)

    slot = _prefetch_pos_rows(x4_hbm, xp_buf, sems, POST_SUB * POST_POS)
    nc = attn_ref.shape[2]
    ts = POST_POS * nc
    d = x1_ref.shape[3]
    iota_e = lax.broadcasted_iota(I32, (N_EXPERTS, ts), 0).astype(F32)
    counts = []
    for sub in range(POST_SUB):
        pos = range(sub * POST_POS, (sub + 1) * POST_POS)
        lanes = slice(sub * ts, (sub + 1) * ts)
        yt = jnp.concatenate(
            [yt_ref[0, :, il * SSM_GROUP:(il + 1) * SSM_GROUP, :].reshape(D_SSM, nc) for il in pos], axis=1)
        g = jax.nn.gelu(yt)
        gate = jax.nn.sigmoid(jnp.dot(wglut_ref[...], g.astype(BF16), preferred_element_type=F32) + bglu_ref[...])
        ssm_t = _rms(g * gate, axis=0) * sn_ref[...]
        attn = attn_ref[0, sub * POST_POS:(sub + 1) * POST_POS].reshape(ts, D_ATTN)
        mixed = jnp.concatenate([attn.astype(BF16), ssm_t.T.astype(BF16)], axis=-1)
        o = jnp.dot(mixed, wout_ref[...], preferred_element_type=F32)
        x = jnp.concatenate([xp_buf[slot, il] for il in pos], axis=0)
        x1 = x + mod_ref[0, 2:3, :] * o
        x1_ref[0, sub * POST_POS:(sub + 1) * POST_POS] = x1.reshape(POST_POS, nc, d)
        h2 = _rms(x1) * nf_ref[...] * (1.0 + mod_ref[0, 4:5, :]) + mod_ref[0, 3:4, :]
        h2_ref[0, sub * POST_POS:(sub + 1) * POST_POS] = h2.astype(BF16).reshape(POST_POS, nc, d)

        logits = lax.dot_general(wr_ref[...], h2.astype(BF16), (((1,), (1,)), ((), ())),
                                 preferred_element_type=F32) + br_ref[...]
        l = logits
        idxs, vals = [], []
        for _ in range(TOP_K):
            m = jnp.max(l, axis=0, keepdims=True)
            idx = jnp.min(jnp.where(l == m, iota_e, float(N_EXPERTS)), axis=0, keepdims=True)
            idxs.append(idx)
            vals.append(m)
            l = jnp.where(iota_e == idx, -jnp.inf, l)
        es = [jnp.exp(v - vals[0]) for v in vals]
        tot = es[0] + es[1] + es[2] + es[3]
        member = jnp.zeros((N_EXPERTS, ts), F32)
        for idx in idxs:
            member = member + (iota_e == idx).astype(F32)
        before = jnp.dot(member.astype(BF16), tri_ref[...], preferred_element_type=F32)
        for k in range(TOP_K):
            eidx_ref[k:k + 1, lanes] = idxs[k].astype(I32)
            wts_ref[k:k + 1, lanes] = es[k] / tot
            lrank_ref[k:k + 1, lanes] = jnp.sum(jnp.where(iota_e == idxs[k], before, 0.0), axis=0,
                                                keepdims=True).astype(I32)
        counts.append(jnp.sum(member, axis=1, keepdims=True))

    carry = carry_ref[...]
    for sub in range(POST_SUB):
        r0_ref[sub] = carry.astype(I32)
        carry = carry + counts[sub]
    carry_ref[...] = carry
    cnt_ref[...] = carry.astype(I32)


def _post(x, attn, yt, mod, w_glu, b_glu, ssm_norm, w_out, norm_ffn, w_router, b_router):
    b, s, d = x.shape
    nc = s // SSM_CHUNK
    ts = POST_POS * nc
    npos = POST_SUB * POST_POS
    nt = SSM_CHUNK // npos
    t = b * s
    pm = lambda bi, j: (bi, j, 0, 0)
    const = lambda bi, j: (0, 0)
    tok = lambda bi, j: (0, bi * nt + j)
    tri = (lax.broadcasted_iota(I32, (ts, ts), 0) < lax.broadcasted_iota(I32, (ts, ts), 1)).astype(BF16)
    col = lambda a: a.reshape(-1, 1)
    return pl.pallas_call(
        _post_kernel,
        grid=(b, nt),
        in_specs=[pl.BlockSpec(memory_space=pl.ANY),
                  pl.BlockSpec((1, npos, nc, D_ATTN), pm),
                  pl.BlockSpec((1, N_GROUPS, npos * SSM_GROUP, nc), lambda bi, j: (bi, 0, j, 0)),
                  pl.BlockSpec((1, 6, d), lambda bi, j: (bi, 0, 0)),
                  pl.BlockSpec((D_SSM, D_SSM), const),
                  pl.BlockSpec((D_SSM, 1), const),
                  pl.BlockSpec((D_SSM, 1), const),
                  pl.BlockSpec((d, d), const),
                  pl.BlockSpec((1, d), const),
                  pl.BlockSpec((N_EXPERTS, d), const),
                  pl.BlockSpec((N_EXPERTS, 1), const),
                  pl.BlockSpec((ts, ts), const)],
        out_specs=[pl.BlockSpec((1, npos, nc, d), pm),
                   pl.BlockSpec((1, npos, nc, d), pm),
                   pl.BlockSpec((TOP_K, POST_SUB * ts), tok),
                   pl.BlockSpec((TOP_K, POST_SUB * ts), tok),
                   pl.BlockSpec((TOP_K, POST_SUB * ts), tok),
                   pl.BlockSpec((POST_SUB, N_EXPERTS, 1), lambda bi, j: (bi * nt + j, 0, 0)),
                   pl.BlockSpec((N_EXPERTS, 1), const)],
        out_shape=[jax.ShapeDtypeStruct((b, SSM_CHUNK, nc, d), F32),
                   jax.ShapeDtypeStruct((b, SSM_CHUNK, nc, d), BF16),
                   jax.ShapeDtypeStruct((TOP_K, t), I32),
                   jax.ShapeDtypeStruct((TOP_K, t), F32),
                   jax.ShapeDtypeStruct((TOP_K, t), I32),
                   jax.ShapeDtypeStruct((b * nt * POST_SUB, N_EXPERTS, 1), I32),
                   jax.ShapeDtypeStruct((N_EXPERTS, 1), I32)],
        scratch_shapes=[pltpu.VMEM((N_EXPERTS, 1), F32), pltpu.VMEM((2, npos, nc, d), F32),
                        pltpu.SemaphoreType.DMA((2,))],
        compiler_params=_cparams(2),
        name="post",
    )(x.reshape(b, nc, SSM_CHUNK, d), attn, yt, mod, w_glu.T.astype(BF16), col(b_glu), col(ssm_norm), w_out.astype(BF16),
      norm_ffn.reshape(1, -1), w_router.T.astype(BF16), col(b_router), tri)


def _route_kernel(eidx_ref, lrank_ref, r0_ref, cnt_ref, ls_ref, tab_ref, te_ref, nv_ref, nx_ref, pad_ref):
    cnt = cnt_ref[...]
    tiles = (cnt + (RUN - 1 + FFN_ROWS - 1)) // FFN_ROWS
    er = lax.broadcasted_iota(I32, (N_EXPERTS, N_EXPERTS), 0)
    ec = lax.broadcasted_iota(I32, (N_EXPERTS, N_EXPERTS), 1)
    ltri = (ec < er).astype(BF16)

    def excl_cumsum(v):
        vb = jnp.broadcast_to(v.astype(F32), (N_EXPERTS, LANES)).astype(BF16)
        return jnp.dot(ltri, vb, preferred_element_type=F32)[:, 0:1].astype(I32)

    start_t = excl_cumsum(tiles)
    end_t = start_t + tiles
    start = start_t * FFN_ROWS
    pad_ref[...] = start + cnt

    nb = r0_ref.shape[0]
    ts = eidx_ref.shape[1] // nb
    iota_e = lax.broadcasted_iota(I32, (N_EXPERTS, ts), 0)
    iota_t = lax.broadcasted_iota(I32, (N_EXPERTS, TABW), 0)
    chunk = lax.broadcasted_iota(I32, (1, TABW), 1)

    def block(b, carry):
        lanes = pl.ds(pl.multiple_of(b * ts, ts), ts)
        sels = [iota_e == eidx_ref[k:k + 1, lanes] for k in range(TOP_K)]
        member = sels[0].astype(I32) + sels[1].astype(I32) + sels[2].astype(I32) + sels[3].astype(I32)
        nch = (jnp.sum(member, axis=1, keepdims=True) + (RUN - 1)) // RUN
        cb = excl_cumsum(nch)
        end_c = cb + nch
        for k in range(TOP_K):
            first = jnp.sum(jnp.where(sels[k], cb, 0), axis=0, keepdims=True)
            lr = lrank_ref[k:k + 1, lanes]
            ls_ref[k:k + 1, lanes] = (first + lax.shift_right_logical(lr, RUN_SHIFT)) * RUN + (lr & (RUN - 1))
        e_of_c = jnp.sum((chunk >= end_c).astype(I32), axis=0, keepdims=True)
        sel_c = iota_t == e_of_c
        first_c = jnp.sum(jnp.where(sel_c, cb, 0), axis=0, keepdims=True)
        slot0_c = jnp.sum(jnp.where(sel_c, start + r0_ref[b], 0), axis=0, keepdims=True)
        n_chunks = jnp.max(end_c, axis=0, keepdims=True)
        row = jnp.where(chunk < n_chunks, slot0_c + (chunk - first_c) * RUN, -1)
        tab_ref[pl.ds(b, 1), :] = jnp.where(chunk == TABW - 1, n_chunks, row)
        return carry

    lax.fori_loop(0, nb, block, 0)

    nv = jnp.max(end_t, axis=0, keepdims=True)
    width = te_ref.shape[1]
    ti = jnp.minimum(lax.broadcasted_iota(I32, (N_EXPERTS, width), 1), nv - 1)
    te = jnp.minimum(jnp.sum((ti >= end_t).astype(I32), axis=0, keepdims=True), N_EXPERTS - 1)
    te_ref[...] = te
    nv_ref[...] = jnp.broadcast_to(nv, nv_ref.shape)
    ie = lax.broadcasted_iota(I32, (N_EXPERTS, width), 0)
    own_end = jnp.sum(jnp.where(ie == te, end_t, 0), axis=0, keepdims=True)
    nxt = jnp.minimum(jnp.sum((own_end >= end_t).astype(I32), axis=0, keepdims=True), N_EXPERTS - 1)
    nx_ref[...] = jnp.where(own_end < nv, nxt, -1)


def _route(eidx, lrank, r0, cnt, n_tiles):
    t = eidx.shape[1]
    nb = r0.shape[0]
    width = -(-n_tiles // LANES) * LANES
    return pl.pallas_call(
        _route_kernel,
        out_shape=[jax.ShapeDtypeStruct((TOP_K, t), I32),
                   jax.ShapeDtypeStruct((nb, TABW), I32),
                   jax.ShapeDtypeStruct((1, width), I32),
                   jax.ShapeDtypeStruct((1, LANES), I32),
                   jax.ShapeDtypeStruct((1, width), I32),
                   jax.ShapeDtypeStruct((N_EXPERTS, 1), I32)],
        name="route",
    )(eidx, lrank, r0, cnt)


def _for_chunk_pairs(n, fn):
    def body(i, carry):
        fn(2 * i, 0)

        @pl.when(2 * i + 1 < n)
        def _():
            fn(2 * i + 1, 1)
        return carry
    lax.fori_loop(0, lax.shift_right_logical(n + 1, 1), body, 0)


def _local_rows(ts):
    return ts * TOP_K + N_EXPERTS * RUN


def _dispatch_kernel(tab_ref, pad_ref, nvt_ref, h_ref, ls_ref, xs_ref, buf, zbuf, sems, zsem):
    b = pl.program_id(0)
    slot = b % 2
    ts = h_ref.shape[0]
    local = buf.shape[2]

    def chunk_copy(sl, blk, c):
        rows = pl.ds(pl.multiple_of(c * RUN, RUN), RUN)
        return pltpu.make_async_copy(buf.at[sl, :, rows, :], xs_ref.at[:, pl.ds(tab_ref[blk * TABW + c], RUN), :],
                                     sems.at[sl])

    def for_chunks(blk, fn):
        _for_chunk_pairs(tab_ref[blk * TABW + TABW - 1], fn)

    @pl.when(b == 0)
    def _():
        zbuf[...] = jnp.zeros_like(zbuf)
        zrows = zbuf.shape[1]
        zero = lambda row: pltpu.make_async_copy(zbuf, xs_ref.at[:, pl.ds(row, zrows), :], zsem)
        for phase in range(3):
            for e in range(phase, N_EXPERTS, 3):
                zero(pad_ref[e]).start()
            for e in range(phase, N_EXPERTS, 3):
                zero(pad_ref[e]).wait()
        ztile = lambda i: pltpu.make_async_copy(zbuf.at[:, pl.ds(0, FFN_ROWS), :],
                                                xs_ref.at[:, pl.ds((nvt_ref[0] + i) * FFN_ROWS, FFN_ROWS), :], zsem)

        def tail_start(i, carry):
            ztile(i).start()
            return carry

        def tail_wait(i, carry):
            ztile(i).wait()
            return carry
        lax.fori_loop(0, nvt_ref[1] - nvt_ref[0], tail_start, 0)
        lax.fori_loop(0, nvt_ref[1] - nvt_ref[0], tail_wait, 0)

    r = lax.broadcasted_iota(I32, (local, ts), 0)
    hit = (r == ls_ref[0:1, :]) | (r == ls_ref[1:2, :]) | (r == ls_ref[2:3, :]) | (r == ls_ref[3:4, :])
    hit = hit.astype(BF16)
    for pb in range(PANELS):
        srt = jnp.dot(hit, h_ref[:, pb * PANEL_COLS:(pb + 1) * PANEL_COLS], preferred_element_type=F32)
        buf[slot, pb] = _pack_panel(srt, exact=True)

    @pl.when(b > 0)
    def _():
        for_chunks(b - 1, lambda c, p: chunk_copy(1 - slot, b - 1, c).wait())

    for_chunks(b, lambda c, p: chunk_copy(slot, b, c).start(priority=p))

    @pl.when(b == pl.num_programs(0) - 1)
    def _():
        for_chunks(b, lambda c, p: chunk_copy(slot, b, c).wait())


def _dispatch(tab, pad, nvt, h2, ls, n_rows):
    t, d = h2.shape
    nb = tab.shape[0] // TABW
    ts = t // nb
    return pl.pallas_call(
        _dispatch_kernel,
        grid_spec=pltpu.PrefetchScalarGridSpec(
            num_scalar_prefetch=3,
            grid=(nb,),
            in_specs=[pl.BlockSpec((ts, d), lambda i, *_: (i, 0)),
                      pl.BlockSpec((TOP_K, ts), lambda i, *_: (0, i))],
            out_specs=pl.BlockSpec(memory_space=pl.ANY),
            scratch_shapes=[pltpu.VMEM((2, PANELS, _local_rows(ts), LANES), U32),
                            pltpu.VMEM((PANELS, FFN_ROWS + RUN, LANES), U32),
                            pltpu.SemaphoreType.DMA((2,)), pltpu.SemaphoreType.DMA],
        ),
        out_shape=jax.ShapeDtypeStruct((PANELS, n_rows, LANES), U32),
        compiler_params=_cparams(1, ROW_VMEM_BYTES),
        name="dispatch",
    )(tab, pad, nvt, h2, ls)


def _ffn_kernel(te_ref, nv_ref, nx_ref, xs_ref, wgu_hbm, bgu_ref, wd_hbm, bd_ref, perm_ref, ys_ref,
                wgu_stage, wd_stage, wg_scr, wu_scr, wd_scr, bg_scr, bu_scr, sems):
    p = pl.program_id(0)
    t0 = 2 * p
    t1 = t0 + 1
    e0 = te_ref[t0]
    e1 = te_ref[t1]
    v0 = t0 < nv_ref[0]
    v1 = t1 < nv_ref[0]
    new0 = (p == 0) | (e0 != te_ref[jnp.maximum(t0 - 1, 0)])
    same = v1 & (e1 == e0)

    def stage_copies(e):
        return (pltpu.make_async_copy(wgu_hbm.at[e], wgu_stage, sems.at[0]),
                pltpu.make_async_copy(wd_hbm.at[e], wd_stage, sems.at[1]))

    def load_expert(t, first):
        e = te_ref[t]
        if first:
            @pl.when(p == 0)
            def _():
                for cp in stage_copies(e):
                    cp.start()

        for cp in stage_copies(e):
            cp.wait()
        bias = bgu_ref[e]
        for c in range(2 * D_FF // PERM):
            cols = slice(c * PERM, (c + 1) * PERM)
            half = slice(c * (PERM // 2), (c + 1) * (PERM // 2))
            w = wgu_stage[:, cols].astype(BF16)
            pw = jnp.dot(w, perm_ref[...], preferred_element_type=F32).astype(BF16)
            wg_scr[:, half] = pw[:, :PERM // 2]
            wu_scr[:, half] = pw[:, PERM // 2:]
            b1 = bias[:, cols].astype(BF16)
            r1 = bias[:, cols] - b1.astype(F32)
            b2 = r1.astype(BF16)
            b3 = (r1 - b2.astype(F32)).astype(BF16)
            terms = jnp.concatenate([b1, b2, b3, jnp.zeros((5, PERM), BF16)], axis=0)
            pb = jnp.sum(jnp.dot(terms, perm_ref[...], preferred_element_type=F32), axis=0, keepdims=True)
            bg_scr[:, half] = pb[:, :PERM // 2]
            bu_scr[:, half] = pb[:, PERM // 2:]
        wd_scr[...] = wd_stage[...].astype(BF16)

        @pl.when(nx_ref[t] >= 0)
        def _():
            for cp in stage_copies(nx_ref[t]):
                cp.start()

    def run(lo, n, e):
        x = _unpack_panels([xs_ref[pb, lo:lo + n, :] for pb in range(PANELS)])
        gate = jnp.dot(x, wg_scr[...], preferred_element_type=F32) + bg_scr[...]
        up = jnp.dot(x, wu_scr[...], preferred_element_type=F32) + bu_scr[...]
        gate = jnp.minimum(gate, SWIGLU_LIMIT)
        up = jnp.clip(up, -SWIGLU_LIMIT, SWIGLU_LIMIT)
        act = ((up + 1.0) * (gate * jax.nn.sigmoid(SWIGLU_ALPHA * gate))).astype(BF16)
        bd = bd_ref[e]
        for pb in range(PANELS):
            cols = slice(pb * PANEL_COLS, (pb + 1) * PANEL_COLS)
            y = jnp.dot(act, wd_scr[:, cols], preferred_element_type=F32) + bd[:, cols]
            ys_ref[pb, lo:lo + n, :] = _pack_panel(y)

    @pl.when(v0 & new0)
    def _():
        load_expert(t0, True)

    @pl.when(same)
    def _():
        run(0, 2 * FFN_ROWS, e0)

    @pl.when(v0 & jnp.logical_not(same))
    def _():
        run(0, FFN_ROWS, e0)

    @pl.when(v1 & jnp.logical_not(same))
    def _():
        load_expert(t1, False)
        run(FFN_ROWS, FFN_ROWS, e1)

    @pl.when(v0 & jnp.logical_not(v1))
    def _():
        ys_ref[:, FFN_ROWS:, :] = xs_ref[:, FFN_ROWS:, :]


def _ffn(te, nv, nx, xs, w_gate_up, bgu, w_down, bd, n_tiles):
    d = D_MODEL
    pair = lambda i, te, nv, nx: (0, jnp.minimum(i, lax.shift_right_logical(nv[0] - 1, 1)), 0)
    whole = lambda i, te, nv, nx: (0, 0, 0)
    r = lax.broadcasted_iota(I32, (PERM, PERM), 0)
    c = lax.broadcasted_iota(I32, (PERM, PERM), 1)
    perm = (r == jnp.where(c < PERM // 2, 2 * c, 2 * (c - PERM // 2) + 1)).astype(BF16)
    return pl.pallas_call(
        _ffn_kernel,
        grid_spec=pltpu.PrefetchScalarGridSpec(
            num_scalar_prefetch=3,
            grid=(n_tiles // 2,),
            in_specs=[pl.BlockSpec((PANELS, 2 * FFN_ROWS, LANES), pair),
                      pl.BlockSpec(memory_space=pl.ANY),
                      pl.BlockSpec((N_EXPERTS, 1, 2 * D_FF), whole),
                      pl.BlockSpec(memory_space=pl.ANY),
                      pl.BlockSpec((N_EXPERTS, 1, d), whole),
                      pl.BlockSpec((PERM, PERM), lambda i, te, nv, nx: (0, 0))],
            out_specs=pl.BlockSpec((PANELS, 2 * FFN_ROWS, LANES), pair),
            scratch_shapes=[pltpu.VMEM((d, 2 * D_FF), F32), pltpu.VMEM((D_FF, d), F32),
                            pltpu.VMEM((d, D_FF), BF16), pltpu.VMEM((d, D_FF), BF16), pltpu.VMEM((D_FF, d), BF16),
                            pltpu.VMEM((1, D_FF), F32), pltpu.VMEM((1, D_FF), F32),
                            pltpu.SemaphoreType.DMA((2,))],
        ),
        out_shape=jax.ShapeDtypeStruct(xs.shape, U32),
        input_output_aliases={3: 0},
        compiler_params=_cparams(1, FFN_VMEM_BYTES),
        name="ffn",
    )(te, nv, nx, xs, w_gate_up, bgu, w_down, bd, perm)


def _combine_kernel(tab_ref, x1_ref, ls_ref, w_ref, mod_ref, ys_ref, o4_hbm, ybuf, ob_buf, sems, osems, *, n_blk):
    jj = pl.program_id(1)
    blk = pl.program_id(0) * pl.num_programs(1) + jj
    slot = blk % 2
    nc = x1_ref.shape[2]
    tt = POST_POS * nc
    d = x1_ref.shape[3]
    local = ybuf.shape[2]

    def chunk_copy(sl, bk, c):
        rows = pl.ds(pl.multiple_of(c * RUN, RUN), RUN)
        return pltpu.make_async_copy(ys_ref.at[:, pl.ds(tab_ref[bk * TABW + c], RUN), :], ybuf.at[sl, :, rows, :],
                                     sems.at[sl])

    def for_chunks(bk, fn):
        _for_chunk_pairs(tab_ref[bk * TABW + TABW - 1], fn)

    @pl.when(blk == 0)
    def _():
        ybuf[...] = jnp.zeros_like(ybuf)
        for_chunks(0, lambda c, p: chunk_copy(0, 0, c).start(priority=p))

    @pl.when(blk + 1 < n_blk)
    def _():
        for_chunks(blk + 1, lambda c, p: chunk_copy(1 - slot, blk + 1, c).start(priority=p))

    for_chunks(blk, lambda c, p: chunk_copy(slot, blk, c).wait())

    to_cols = lambda a: jnp.concatenate([a, jnp.zeros_like(a)], axis=0).T
    ls_c = to_cols(ls_ref[...].astype(F32))
    w_c = to_cols(w_ref[...])
    r = lax.broadcasted_iota(I32, (tt, local), 1).astype(F32)
    wm = jnp.zeros((tt, local), F32)
    for k in range(TOP_K):
        wm = wm + jnp.where(r == ls_c[:, k:k + 1], w_c[:, k:k + 1], 0.0)
    wm = wm.astype(BF16)
    acc = jnp.dot(wm, _unpack_panels([ybuf[slot, pb] for pb in range(PANELS)]), preferred_element_type=F32)
    out = x1_ref[0].reshape(tt, d) + mod_ref[0, 5:6, :] * acc

    def out_copies(sl, b_, j_):
        return [pltpu.make_async_copy(ob_buf.at[sl, il], o4_hbm.at[b_, :, POST_POS * j_ + il, :], osems.at[sl])
                for il in range(POST_POS)]

    @pl.when(blk >= 2)
    def _():
        for cp in out_copies(slot, 0, 0):
            cp.wait()

    for il in range(POST_POS):
        ob_buf[slot, il] = out[il * nc:(il + 1) * nc]
    for cp in out_copies(slot, pl.program_id(0), jj):
        cp.start()

    @pl.when(blk == n_blk - 1)
    def _():
        for cp in out_copies(slot, 0, 0):
            cp.wait()
        if n_blk > 1:
            for cp in out_copies(1 - slot, 0, 0):
                cp.wait()


def _combine(tab, x1, ls, wts, mod, ys):
    b, _, nc, d = x1.shape
    s = SSM_CHUNK * nc
    tt = POST_POS * nc
    nt = SSM_CHUNK // POST_POS
    o4 = pl.pallas_call(
        functools.partial(_combine_kernel, n_blk=b * nt),
        grid_spec=pltpu.PrefetchScalarGridSpec(
            num_scalar_prefetch=1,
            grid=(b, nt),
            in_specs=[pl.BlockSpec((1, POST_POS, nc, d), lambda bi, j, *_: (bi, j, 0, 0)),
                      pl.BlockSpec((TOP_K, tt), lambda bi, j, *_: (0, bi * nt + j)),
                      pl.BlockSpec((TOP_K, tt), lambda bi, j, *_: (0, bi * nt + j)),
                      pl.BlockSpec((1, 6, d), lambda bi, j, *_: (bi, 0, 0)),
                      pl.BlockSpec(memory_space=pl.ANY)],
            out_specs=pl.BlockSpec(memory_space=pl.ANY),
            scratch_shapes=[pltpu.VMEM((2, PANELS, _local_rows(tt), LANES), U32),
                            pltpu.VMEM((2, POST_POS, nc, d), F32),
                            pltpu.SemaphoreType.DMA((2,)), pltpu.SemaphoreType.DMA((2,))],
        ),
        out_shape=jax.ShapeDtypeStruct((b, nc, SSM_CHUNK, d), F32),
        compiler_params=_cparams(2),
        name="combine",
    )(tab, x1, ls, wts, mod, ys)
    return o4.reshape(b, s, d)


def kernel(x, c, w_ada, b_ada, norm_mix, w_in, b_in, q_norm, k_norm, sinks, lam_re, lam_im, log_dt, b_re, b_im,
           c_re, c_im, d_skip, w_glu, b_glu, attn_out_norm, ssm_out_norm, w_out, norm_ffn, w_router, b_router,
           w_gate_up, b_gate_up, w_down, b_down):
    b, s, d = x.shape
    t = b * s
    depth = w_ada.shape[0]
    n_tiles = -(-(t * TOP_K + N_EXPERTS * (RUN - 1 + FFN_ROWS - 1)) // FFN_ROWS)
    n_tiles += n_tiles % 2
    n_alloc = n_tiles + 2
    for l in range(depth):
        mod = _adaln(c, w_ada[l], b_ada[l]).reshape(b, 6, d)
        q, k, v, ut = _inproj(x, mod, norm_mix[l], w_in[l], b_in[l])
        attn = _attention(q, k, v, sinks[l], q_norm[l], k_norm[l], attn_out_norm[l])
        tt, wz, wyt, cs = _ssm_params(lam_re[l], lam_im[l], log_dt[l], b_re[l], b_im[l], c_re[l], c_im[l])
        yt = _ssm(ut, tt, wz, wyt, cs, d_skip[l])
        x1, h2, eidx, wts, lrank, r0, cnt = _post(x, attn, yt, mod, w_glu[l], b_glu[l], ssm_out_norm[l], w_out[l],
                                                  norm_ffn[l], w_router[l], b_router[l])
        ls, tab, te, nv, nx, pad = _route(eidx, lrank, r0, cnt, n_tiles)
        tab = tab.reshape(-1)
        nvt = jnp.stack([nv[0, 0], jnp.int32(n_alloc)])
        xs = _dispatch(tab, pad.reshape(-1), nvt, h2.reshape(t, d), ls, n_alloc * FFN_ROWS)
        ys = _ffn(te[0, :n_tiles], nv[0, :1], nx[0, :n_tiles], xs, w_gate_up[l], b_gate_up[l][:, None, :],
                  w_down[l], b_down[l][:, None, :], n_tiles)
        x = _combine(tab, x1, ls, wts, mod, ys)
    return x
```

```python
import functools
import math

import jax
import jax.numpy as jnp
from jax import lax
from jax.experimental import pallas as pl
from jax.experimental.pallas import tpu as pltpu

F32 = jnp.float32
BF16 = jnp.bfloat16
U32 = jnp.uint32
I32 = jnp.int32

D_MODEL = 1024
HEAD_DIM = 64
N_HEADS = 8
N_KV_HEADS = 2
Q_PER_KV = N_HEADS // N_KV_HEADS
D_ATTN = N_HEADS * HEAD_DIM
D_KV = N_KV_HEADS * HEAD_DIM
D_QKV = D_ATTN + 2 * D_KV
WINDOW = 128
BLOCK = 128
D_SSM = D_MODEL - D_ATTN
SSM_GROUP = 16
N_GROUPS = D_SSM // SSM_GROUP
STATE = 64
N_EXPERTS = 32
TOP_K = 4
D_FF = D_MODEL
SWIGLU_LIMIT = 7.0
SWIGLU_ALPHA = 1.702
EPS = 1e-6
NEG_INF = -1e30

LANES = 128
SSM_CHUNK = 16
SSM_ROW = SSM_CHUNK * SSM_GROUP
N_POW = 2 * SSM_CHUNK
PANEL_COLS = 2 * LANES
PANELS = D_MODEL // PANEL_COLS

SSM_GROUPS_PER_STEP = 4
POS_PER_STEP = 8
ATTN_ROWS = 512
POST_POS = 2
POST_SUB = 4
FFN_ROWS = 256
RUN = 16
RUN_SHIFT = 4
TABW = 128
PERM = 256
FFN_VMEM_BYTES = 40 * 1024 * 1024
ROW_VMEM_BYTES = 48 * 1024 * 1024

HIGHEST = lax.Precision.HIGHEST
_ARB = "arbitrary"


def _cparams(n, vmem=None):
    return pltpu.CompilerParams(dimension_semantics=(_ARB,) * n, vmem_limit_bytes=vmem)


def _rms(x, axis=-1):
    return x * lax.rsqrt(jnp.mean(x * x, axis=axis, keepdims=True) + EPS)


def _pack_panel(y, exact=False):
    hi, lo = y[:, :LANES], y[:, LANES:]
    if not exact:
        hi = hi.astype(BF16).astype(F32)
        lo = lo.astype(BF16).astype(F32)
    return lax.bitcast_convert_type(hi, U32) | (lax.bitcast_convert_type(lo, U32) >> 16)


def _unpack_panels(words):
    cols = []
    for w in words:
        cols.append(lax.bitcast_convert_type(w & jnp.uint32(0xFFFF0000), F32).astype(BF16))
        cols.append(lax.bitcast_convert_type(w << 16, F32).astype(BF16))
    return jnp.concatenate(cols, axis=-1)


def _prefetch_pos_rows(x4_hbm, buf, sems, n_pos):
    bi = pl.program_id(0)
    j = pl.program_id(1)
    nj = pl.num_programs(1)
    g = bi * nj + j
    slot = g % 2

    def copies(sl, b_, j_):
        return [pltpu.make_async_copy(x4_hbm.at[b_, :, n_pos * j_ + il, :], buf.at[sl, il], sems.at[sl])
                for il in range(n_pos)]

    @pl.when(g == 0)
    def _():
        for cp in copies(0, 0, 0):
            cp.start()

    @pl.when(g + 1 < pl.num_programs(0) * nj)
    def _():
        wrap = j + 1 == nj
        for cp in copies(1 - slot, jnp.where(wrap, bi + 1, bi), jnp.where(wrap, 0, j + 1)):
            cp.start()

    for cp in copies(slot, bi, j):
        cp.wait()
    return slot


def _to_lane_blocks(dst, src):
    for kb in range(dst.shape[0]):
        dst[kb] = src[:, kb * LANES:(kb + 1) * LANES]


def _adaln_kernel(c_ref, w_ref, b_ref, o_ref):
    c = c_ref[...]
    ca = c * jax.nn.sigmoid(c)
    o_ref[...] = jnp.dot(ca, w_ref[...], preferred_element_type=F32, precision=HIGHEST) + b_ref[...]


def _adaln(c, w_ada, b_ada):
    b, d = c.shape
    n = w_ada.shape[1] // d
    return pl.pallas_call(
        _adaln_kernel,
        grid=(n,),
        in_specs=[pl.BlockSpec((b, d), lambda j: (0, 0)),
                  pl.BlockSpec((d, d), lambda j: (0, j)),
                  pl.BlockSpec((1, d), lambda j: (0, j))],
        out_specs=pl.BlockSpec((b, d), lambda j: (0, j)),
        out_shape=jax.ShapeDtypeStruct((b, n * d), F32),
        compiler_params=_cparams(1),
        name="adaln",
    )(c, w_ada, b_ada.reshape(1, -1))


def _inproj_kernel(x4_hbm, x_ref, mod_ref, g_ref, wqkv_ref, bqkv_ref, wut_ref, but_ref, q_ref, k_ref, v_ref, ut_ref,
                   xp_buf, sems):
    nc = ut_ref.shape[3]
    slot = _prefetch_pos_rows(x4_hbm, xp_buf, sems, POS_PER_STEP)
    gain = g_ref[...]
    scale = 1.0 + mod_ref[0, 1:2, :]
    shift = mod_ref[0, 0:1, :]

    def norm_mod(x):
        return (_rms(x) * gain * scale + shift).astype(BF16)

    proj = jnp.dot(norm_mod(x_ref[0]), wqkv_ref[...], preferred_element_type=F32) + bqkv_ref[...]
    q_ref[0] = proj[:, :D_ATTN].astype(BF16)
    k_ref[0] = proj[:, D_ATTN:D_ATTN + D_KV].astype(BF16)
    v_ref[0] = proj[:, D_ATTN + D_KV:].astype(BF16)

    hs = jnp.concatenate([norm_mod(xp_buf[slot, il]) for il in range(POS_PER_STEP)], axis=0)
    ut = lax.dot_general(wut_ref[...], hs, (((1,), (1,)), ((), ())), preferred_element_type=F32) + but_ref[...]
    for il in range(POS_PER_STEP):
        piece = ut[:, il * nc:(il + 1) * nc].astype(BF16)
        ut_ref[0, :, il * SSM_GROUP:(il + 1) * SSM_GROUP, :] = piece.reshape(N_GROUPS, SSM_GROUP, nc)


def _inproj(x, mod, gain, w_in, b_in):
    b, s, d = x.shape
    nc = s // SSM_CHUNK
    rows = POS_PER_STEP * nc
    row = lambda bi, j: (bi, j, 0)
    const = lambda bi, j: (0, 0)
    w_qkv = w_in[:, :D_QKV].astype(BF16)
    w_ut = w_in[:, D_QKV:].T.astype(BF16)
    return pl.pallas_call(
        _inproj_kernel,
        grid=(b, SSM_CHUNK // POS_PER_STEP),
        in_specs=[pl.BlockSpec(memory_space=pl.ANY),
                  pl.BlockSpec((1, rows, d), row),
                  pl.BlockSpec((1, 6, d), lambda bi, j: (bi, 0, 0)),
                  pl.BlockSpec((1, d), const),
                  pl.BlockSpec((d, D_QKV), const),
                  pl.BlockSpec((1, D_QKV), const),
                  pl.BlockSpec((D_SSM, d), const),
                  pl.BlockSpec((D_SSM, 1), const)],
        out_specs=[pl.BlockSpec((1, rows, D_ATTN), row),
                   pl.BlockSpec((1, rows, D_KV), row),
                   pl.BlockSpec((1, rows, D_KV), row),
                   pl.BlockSpec((1, N_GROUPS, POS_PER_STEP * SSM_GROUP, nc), lambda bi, j: (bi, 0, j, 0))],
        out_shape=[jax.ShapeDtypeStruct((b, s, D_ATTN), BF16),
                   jax.ShapeDtypeStruct((b, s, D_KV), BF16),
                   jax.ShapeDtypeStruct((b, s, D_KV), BF16),
                   jax.ShapeDtypeStruct((b, N_GROUPS, SSM_ROW, nc), BF16)],
        scratch_shapes=[pltpu.VMEM((2, POS_PER_STEP, nc, d), F32), pltpu.SemaphoreType.DMA((2,))],
        compiler_params=_cparams(2),
        name="inproj",
    )(x.reshape(b, nc, SSM_CHUNK, d), x, mod, gain.reshape(1, d), w_qkv, b_in[:D_QKV].reshape(1, D_QKV), w_ut, b_in[D_QKV:].reshape(D_SSM, 1))


def _half_norm(x, low):
    sq = x * x
    s_lo = jnp.sum(jnp.where(low, sq, 0.0), axis=-1, keepdims=True)
    s_hi = jnp.sum(sq, axis=-1, keepdims=True) - s_lo
    inv = 1.0 / HEAD_DIM
    scale = jnp.where(low, lax.rsqrt(s_lo * inv + EPS), lax.rsqrt(s_hi * inv + EPS))
    return x * scale


def _attn_block(first, q, k_prev, k_cur, v_prev, v_cur, sinks_ref, qn, low, upper, rblk):
    no_prev = jnp.where(first, NEG_INF, 0.0)
    out_blocks = []
    for hk in range(N_KV_HEADS):
        qs = []
        for j in range(Q_PER_KV // 2):
            blk = hk * (Q_PER_KV // 2) + j
            qb = _half_norm(q[:, blk * LANES:(blk + 1) * LANES], low) * qn * (1.0 / math.sqrt(HEAD_DIM))
            qs.append(jnp.where(low, qb, 0.0))
            qs.append(jnp.where(low, 0.0, qb))
        qg = jnp.concatenate(qs, axis=0).astype(BF16)
        nt = (((1,), (1,)), ((), ()))
        s_prev = lax.dot_general(qg, k_prev[hk], nt, preferred_element_type=F32)
        s_cur = lax.dot_general(qg, k_cur[hk], nt, preferred_element_type=F32)
        s = jnp.where(upper, s_prev + no_prev, s_cur)
        sink = jnp.zeros((Q_PER_KV * BLOCK, 1), F32)
        for g in range(Q_PER_KV):
            sink = jnp.where(rblk == g, sinks_ref[hk * Q_PER_KV + g], sink)
        m = jnp.maximum(jnp.max(s, axis=-1, keepdims=True), sink)
        p = jnp.exp(s - m)
        den = jnp.sum(p, axis=-1, keepdims=True) + jnp.exp(sink - m)
        o = (jnp.dot(jnp.where(upper, p, 0.0).astype(BF16), v_prev[hk], preferred_element_type=F32)
             + jnp.dot(jnp.where(upper, 0.0, p).astype(BF16), v_cur[hk], preferred_element_type=F32)) / den
        for j in range(Q_PER_KV // 2):
            ev = o[(2 * j) * BLOCK:(2 * j + 1) * BLOCK]
            od = o[(2 * j + 1) * BLOCK:(2 * j + 2) * BLOCK]
            out_blocks.append(jnp.where(low, ev, od))
    return jnp.concatenate(out_blocks, axis=-1)


def _attn_kernel(sinks_ref, q_ref, k_ref, v_ref, qn_ref, kn_ref, on_ref, o_hbm, a_buf, sems, *, n_steps):
    step = pl.program_id(1)
    g = pl.program_id(0) * pl.num_programs(1) + step
    slot = g % 2
    cps = ATTN_ROWS // SSM_CHUNK
    nq = ATTN_ROWS // BLOCK

    def out_copies(sl, b_, s_):
        return [pltpu.make_async_copy(a_buf.at[sl, :, i, :], o_hbm.at[b_, i, pl.ds(s_ * cps, cps), :], sems.at[sl])
                for i in range(SSM_CHUNK)]

    @pl.when(g >= 2)
    def _():
        for cp in out_copies(slot, 0, 0):
            cp.wait()

    low = lax.broadcasted_iota(I32, (1, LANES), 1) < HEAD_DIM
    rows = Q_PER_KV * BLOCK
    upper = lax.broadcasted_iota(I32, (rows, BLOCK), 1) > lax.broadcasted_iota(I32, (rows, BLOCK), 0) % BLOCK
    rblk = lax.broadcasted_iota(I32, (rows, 1), 0) // BLOCK

    cur = pl.multiple_of(step * ATTN_ROWS, ATTN_ROWS)
    prev = pl.multiple_of(jnp.maximum(step * nq - 1, 0) * BLOCK, BLOCK)
    kall = jnp.concatenate([k_ref[0, pl.ds(prev, BLOCK), :], k_ref[0, pl.ds(cur, ATTN_ROWS), :]], axis=0).astype(F32)
    vall = jnp.concatenate([v_ref[0, pl.ds(prev, BLOCK), :], v_ref[0, pl.ds(cur, ATTN_ROWS), :]], axis=0).astype(F32)
    kall = _half_norm(kall, low) * kn_ref[...]
    kswap = pltpu.roll(kall, HEAD_DIM, axis=1)
    vswap = pltpu.roll(vall, HEAD_DIM, axis=1)
    k_dup = [jnp.where(low, kall, kswap).astype(BF16), jnp.where(low, kswap, kall).astype(BF16)]
    v_dup = [jnp.where(low, vall, vswap).astype(BF16), jnp.where(low, vswap, vall).astype(BF16)]
    blk = lambda a, i: [a[hk][i * BLOCK:(i + 1) * BLOCK] for hk in range(N_KV_HEADS)]

    for qb in range(nq):
        q = q_ref[0, qb * BLOCK:(qb + 1) * BLOCK, :].astype(F32)
        attn = _attn_block((step == 0) if qb == 0 else False, q, blk(k_dup, qb), blk(k_dup, qb + 1),
                           blk(v_dup, qb), blk(v_dup, qb + 1), sinks_ref, qn_ref[...], low, upper, rblk)
        attn = _rms(attn) * on_ref[...]
        cpb = BLOCK // SSM_CHUNK
        a_buf[slot, qb * cpb:(qb + 1) * cpb] = attn.reshape(cpb, SSM_CHUNK, D_ATTN)

    for cp in out_copies(slot, pl.program_id(0), step):
        cp.start()

    @pl.when(g == n_steps - 1)
    def _():
        for cp in out_copies(slot, 0, 0):
            cp.wait()
        if n_steps > 1:
            for cp in out_copies(1 - slot, 0, 0):
                cp.wait()


def _attention(q, k, v, sinks, q_norm, k_norm, out_norm):
    b, s, _ = q.shape
    tile2 = lambda g: jnp.tile(g.reshape(1, HEAD_DIM), (1, 2))
    cps = ATTN_ROWS // SSM_CHUNK
    return pl.pallas_call(
        functools.partial(_attn_kernel, n_steps=b * (s // ATTN_ROWS)),
        grid=(b, s // ATTN_ROWS),
        in_specs=[pl.BlockSpec(memory_space=pltpu.SMEM),
                  pl.BlockSpec((1, ATTN_ROWS, D_ATTN), lambda bi, n: (bi, n, 0)),
                  pl.BlockSpec((1, s, D_KV), lambda bi, n: (bi, 0, 0)),
                  pl.BlockSpec((1, s, D_KV), lambda bi, n: (bi, 0, 0)),
                  pl.BlockSpec((1, LANES), lambda bi, n: (0, 0)),
                  pl.BlockSpec((1, LANES), lambda bi, n: (0, 0)),
                  pl.BlockSpec((1, D_ATTN), lambda bi, n: (0, 0))],
        out_specs=pl.BlockSpec(memory_space=pl.ANY),
        out_shape=jax.ShapeDtypeStruct((b, SSM_CHUNK, s // SSM_CHUNK, D_ATTN), F32),
        scratch_shapes=[pltpu.VMEM((2, cps, SSM_CHUNK, D_ATTN), F32), pltpu.SemaphoreType.DMA((2,))],
        compiler_params=_cparams(2),
        name="attention",
    )(sinks, q, k, v, tile2(q_norm), tile2(k_norm), out_norm.reshape(1, D_ATTN))


def _cmul(ar, ai, br, bi):
    return ar * br - ai * bi, ar * bi + ai * br


def _ssm_param_kernel(*refs):
    for gi in range(refs[0].shape[0]):
        _ssm_param_group(gi, *refs)


def _ssm_param_group(gi, lam_ref, bre_ref, bim_ref, cre_ref, cim_ref, tt_ref, wz_ref, wyt_ref, cs_ref):
    f32dot = functools.partial(jnp.dot, preferred_element_type=F32, precision=HIGHEST)
    lr = lam_ref[gi, 0:1, :]
    li = lam_ref[gi, 1:2, :]
    dt = jnp.exp(lam_ref[gi, 2:3, :])
    rho = lr * dt
    th = li * dt
    imag_lane = lax.broadcasted_iota(I32, (1, LANES), 1) >= STATE

    kk = (lax.broadcasted_iota(I32, (N_POW, 1), 0) - (SSM_CHUNK - 1)).astype(F32)
    mag = jnp.exp(rho * kk)
    pw_r = mag * jnp.cos(th * kk)
    pw_i = mag * jnp.sin(th * kk)
    lb_r = pw_r[SSM_CHUNK:SSM_CHUNK + 1]
    lb_i = pw_i[SSM_CHUNK:SSM_CHUNK + 1]
    den = lr * lr + li * li
    coef_r = ((lb_r - 1.0) * lr + lb_i * li) / den
    coef_i = (lb_i * lr - (lb_r - 1.0) * li) / den

    eye = (lax.broadcasted_iota(I32, (SSM_GROUP, SSM_GROUP), 0)
           == lax.broadcasted_iota(I32, (SSM_GROUP, SSM_GROUP), 1)).astype(F32)
    lane_fold = (lax.broadcasted_iota(I32, (STATE, LANES), 1) % STATE
                 == lax.broadcasted_iota(I32, (STATE, LANES), 0)).astype(F32)

    def tile_pos(x):
        return jnp.concatenate([x] * SSM_CHUNK, axis=0)

    def power_rows(k_of_pos):
        idx = [k_of_pos(p) + (SSM_CHUNK - 1) for p in range(SSM_CHUNK)]
        rep = lambda t: jnp.concatenate([jnp.broadcast_to(t[r:r + 1], (SSM_GROUP, LANES)) for r in idx], axis=0)
        return rep(pw_r), rep(pw_i)

    def b_rows(b_ref):
        b2 = jnp.concatenate([b_ref[gi], b_ref[gi]], axis=0)
        return tile_pos(lax.dot_general(eye, b2, (((1,), (1,)), ((), ())), preferred_element_type=F32,
                                        precision=HIGHEST))

    def c_rows(c_ref):
        return tile_pos(f32dot(c_ref[gi], lane_fold))

    bbar_r, bbar_i = _cmul(coef_r, coef_i, b_rows(bre_ref), b_rows(bim_ref))
    c_r = c_rows(cre_ref)
    c_i = c_rows(cim_ref)

    a_r, a_i = _cmul(bbar_r, bbar_i, *power_rows(lambda p: -p))
    a2c = jnp.where(imag_lane, -a_i, a_r)
    m_r, m_i = _cmul(c_r, c_i, *power_rows(lambda p: p))
    bmc = jnp.where(imag_lane, m_i, m_r)
    tt = f32dot(bmc, a2c.T)
    causal = (lax.broadcasted_iota(I32, (SSM_ROW, 1), 0) // SSM_GROUP
              >= lax.broadcasted_iota(I32, (1, SSM_ROW), 1) // SSM_GROUP)
    tt_ref[gi] = jnp.where(causal, tt, 0.0).astype(BF16)

    w_r, w_i = _cmul(bbar_r, bbar_i, *power_rows(lambda p: SSM_CHUNK - 1 - p))
    wz_ref[gi, :, :LANES] = jnp.where(imag_lane, w_i, w_r).astype(BF16)
    wz_ref[gi, :, LANES:] = jnp.where(imag_lane, w_r, w_i).astype(BF16)

    y_r, y_i = _cmul(c_r, c_i, *power_rows(lambda p: p + 1))
    wyt_ref[gi] = jnp.where(imag_lane, -y_i, y_r).astype(BF16)

    cs_ref[gi, 0:1, :] = pw_r[N_POW - 1:N_POW]
    cs_ref[gi, 1:2, :] = jnp.where(imag_lane, pw_i[N_POW - 1:N_POW], -pw_i[N_POW - 1:N_POW])


def _ssm_params(lam_re, lam_im, log_dt, b_re, b_im, c_re, c_im):
    g = lam_re.shape[0]
    lam = jnp.stack([lam_re, lam_im, jnp.broadcast_to(log_dt[:, None], (g, STATE))], axis=1)
    lam = jnp.concatenate([lam, lam], axis=2)
    ng = SSM_GROUPS_PER_STEP
    blk = lambda *shape: pl.BlockSpec((ng,) + shape, lambda i: (i, 0, 0))
    return pl.pallas_call(
        _ssm_param_kernel,
        grid=(g // ng,),
        in_specs=[blk(3, LANES), blk(STATE, SSM_GROUP), blk(STATE, SSM_GROUP), blk(SSM_GROUP, STATE),
                  blk(SSM_GROUP, STATE)],
        out_specs=[blk(SSM_ROW, SSM_ROW), blk(SSM_ROW, SSM_ROW), blk(SSM_ROW, LANES), blk(2, LANES)],
        out_shape=[jax.ShapeDtypeStruct((g, SSM_ROW, SSM_ROW), BF16),
                   jax.ShapeDtypeStruct((g, SSM_ROW, SSM_ROW), BF16),
                   jax.ShapeDtypeStruct((g, SSM_ROW, LANES), BF16),
                   jax.ShapeDtypeStruct((g, 2, LANES), F32)],
        compiler_params=_cparams(1),
        name="ssm_params",
    )(lam, b_re, b_im, c_re, c_im)


def _ssm_kernel(ut_ref, tt_ref, wz_ref, wyt_ref, cs_ref, d_ref, yt_ref, z_scr, s_scr):
    batch, ng, _, nc = ut_ref.shape
    uts = [jnp.concatenate([ut_ref[b, gi] for b in range(batch)], axis=1) for gi in range(ng)]
    for gi in range(ng):
        z = lax.dot_general(uts[gi], wz_ref[gi], (((0,), (0,)), ((), ())), preferred_element_type=F32)
        _to_lane_blocks(z_scr.at[gi], z)
    c1 = [cs_ref[gi, 0:1, :] for gi in range(ng)]
    c2 = [cs_ref[gi, 1:2, :] for gi in range(ng)]

    def step(c, carry):
        rows = pl.ds(c, batch, stride=nc)
        out = []
        for gi in range(ng):
            s1, s2 = carry[gi]
            s_scr[gi, rows, :] = s1
            out.append((c1[gi] * s1 + c2[gi] * s2 + z_scr[gi, 0, rows, :],
                        c1[gi] * s2 - c2[gi] * s1 + z_scr[gi, 1, rows, :]))
        return tuple(out)

    zero = jnp.zeros((batch, LANES), F32)
    lax.fori_loop(0, nc, step, ((zero, zero),) * ng, unroll=8)
    for gi in range(ng):
        y = jnp.dot(tt_ref[gi], uts[gi], preferred_element_type=F32)
        y = y + lax.dot_general(wyt_ref[gi], s_scr[gi].astype(BF16), (((1,), (1,)), ((), ())),
                                preferred_element_type=F32)
        y = y + d_ref[gi] * uts[gi].astype(F32)
        for b in range(batch):
            yt_ref[b, gi] = y[:, b * nc:(b + 1) * nc]


def _ssm(ut, tt, wz, wyt, cs, d_skip):
    b, g, _, nc = ut.shape
    ng = SSM_GROUPS_PER_STEP
    d_col = jnp.tile(d_skip.reshape(g, 1, SSM_GROUP), (1, SSM_CHUNK, 1)).reshape(g, SSM_ROW, 1)
    blk = lambda *shape: pl.BlockSpec((ng,) + shape, lambda i: (i, 0, 0))
    act = pl.BlockSpec((b, ng, SSM_ROW, nc), lambda i: (0, i, 0, 0))
    return pl.pallas_call(
        _ssm_kernel,
        grid=(g // ng,),
        in_specs=[act, blk(SSM_ROW, SSM_ROW), blk(SSM_ROW, SSM_ROW), blk(SSM_ROW, LANES), blk(2, LANES),
                  blk(SSM_ROW, 1)],
        out_specs=act,
        out_shape=jax.ShapeDtypeStruct((b, g, SSM_ROW, nc), F32),
        scratch_shapes=[pltpu.VMEM((ng, SSM_ROW // LANES, b * nc, LANES), F32), pltpu.VMEM((ng, b * nc, LANES), F32)],
        compiler_params=_cparams(1),
        name="ssm",
    )(ut, tt, wz, wyt, cs, d_col)


def _post_kernel(x4_hbm, attn_ref, yt_ref, mod_ref, wglut_ref, bglu_ref, sn_ref, wout_ref, nf_ref, wr_ref, br_ref,
                 tri_ref, x1_ref, h2_ref, eidx_ref, wts_ref, lrank_ref, r0_ref, cnt_ref, carry_ref, xp_buf, sems):
    @pl.when((pl.program_id(0) == 0) & (pl.program_id(1) == 0))
    def _():
        carry_ref[...] = jnp.zeros_like(carry_ref)

    slot = _prefetch_pos_rows(x4_hbm, xp_buf, sems, POST_SUB * POST_POS)
    nc = attn_ref.shape[2]
    ts = POST_POS * nc
    d = x1_ref.shape[3]
    iota_e = lax.broadcasted_iota(I32, (N_EXPERTS, ts), 0).astype(F32)
    counts = []
    for sub in range(POST_SUB):
        pos = range(sub * POST_POS, (sub + 1) * POST_POS)
        lanes = slice(sub * ts, (sub + 1) * ts)
        yt = jnp.concatenate(
            [yt_ref[0, :, il * SSM_GROUP:(il + 1) * SSM_GROUP, :].reshape(D_SSM, nc) for il in pos], axis=1)
        g = jax.nn.gelu(yt)
        gate = jax.nn.sigmoid(jnp.dot(wglut_ref[...], g.astype(BF16), preferred_element_type=F32) + bglu_ref[...])
        ssm_t = _rms(g * gate, axis=0) * sn_ref[...]
        attn = attn_ref[0, sub * POST_POS:(sub + 1) * POST_POS].reshape(ts, D_ATTN)
        mixed = jnp.concatenate([attn.astype(BF16), ssm_t.T.astype(BF16)], axis=-1)
        o = jnp.dot(mixed, wout_ref[...], preferred_element_type=F32)
        x = jnp.concatenate([xp_buf[slot, il] for il in pos], axis=0)
        x1 = x + mod_ref[0, 2:3, :] * o
        x1_ref[0, sub * POST_POS:(sub + 1) * POST_POS] = x1.reshape(POST_POS, nc, d)
        h2 = _rms(x1) * nf_ref[...] * (1.0 + mod_ref[0, 4:5, :]) + mod_ref[0, 3:4, :]
        h2_ref[0, sub * POST_POS:(sub + 1) * POST_POS] = h2.astype(BF16).reshape(POST_POS, nc, d)

        logits = lax.dot_general(wr_ref[...], h2.astype(BF16), (((1,), (1,)), ((), ())),
                                 preferred_element_type=F32) + br_ref[...]
        l = logits
        idxs, vals = [], []
        for _ in range(TOP_K):
            m = jnp.max(l, axis=0, keepdims=True)
            idx = jnp.min(jnp.where(l == m, iota_e, float(N_EXPERTS)), axis=0, keepdims=True)
            idxs.append(idx)
            vals.append(m)
            l = jnp.where(iota_e == idx, -jnp.inf, l)
        es = [jnp.exp(v - vals[0]) for v in vals]
        tot = es[0] + es[1] + es[2] + es[3]
        member = jnp.zeros((N_EXPERTS, ts), F32)
        for idx in idxs:
            member = member + (iota_e == idx).astype(F32)
        before = jnp.dot(member.astype(BF16), tri_ref[...], preferred_element_type=F32)
        for k in range(TOP_K):
            eidx_ref[k:k + 1, lanes] = idxs[k].astype(I32)
            wts_ref[k:k + 1, lanes] = es[k] / tot
            lrank_ref[k:k + 1, lanes] = jnp.sum(jnp.where(iota_e == idxs[k], before, 0.0), axis=0,
                                                keepdims=True).astype(I32)
        counts.append(jnp.sum(member, axis=1, keepdims=True))

    carry = carry_ref[...]
    for sub in range(POST_SUB):
        r0_ref[sub] = carry.astype(I32)
        carry = carry + counts[sub]
    carry_ref[...] = carry
    cnt_ref[...] = carry.astype(I32)


def _post(x, attn, yt, mod, w_glu, b_glu, ssm_norm, w_out, norm_ffn, w_router, b_router):
    b, s, d = x.shape
    nc = s // SSM_CHUNK
    ts = POST_POS * nc
    npos = POST_SUB * POST_POS
    nt = SSM_CHUNK // npos
    t = b * s
    pm = lambda bi, j: (bi, j, 0, 0)
    const = lambda bi, j: (0, 0)
    tok = lambda bi, j: (0, bi * nt + j)
    tri = (lax.broadcasted_iota(I32, (ts, ts), 0) < lax.broadcasted_iota(I32, (ts, ts), 1)).astype(BF16)
    col = lambda a: a.reshape(-1, 1)
    return pl.pallas_call(
        _post_kernel,
        grid=(b, nt),
        in_specs=[pl.BlockSpec(memory_space=pl.ANY),
                  pl.BlockSpec((1, npos, nc, D_ATTN), pm),
                  pl.BlockSpec((1, N_GROUPS, npos * SSM_GROUP, nc), lambda bi, j: (bi, 0, j, 0)),
                  pl.BlockSpec((1, 6, d), lambda bi, j: (bi, 0, 0)),
                  pl.BlockSpec((D_SSM, D_SSM), const),
                  pl.BlockSpec((D_SSM, 1), const),
                  pl.BlockSpec((D_SSM, 1), const),
                  pl.BlockSpec((d, d), const),
                  pl.BlockSpec((1, d), const),
                  pl.BlockSpec((N_EXPERTS, d), const),
                  pl.BlockSpec((N_EXPERTS, 1), const),
                  pl.BlockSpec((ts, ts), const)],
        out_specs=[pl.BlockSpec((1, npos, nc, d), pm),
                   pl.BlockSpec((1, npos, nc, d), pm),
                   pl.BlockSpec((TOP_K, POST_SUB * ts), tok),
                   pl.BlockSpec((TOP_K, POST_SUB * ts), tok),
                   pl.BlockSpec((TOP_K, POST_SUB * ts), tok),
                   pl.BlockSpec((POST_SUB, N_EXPERTS, 1), lambda bi, j: (bi * nt + j, 0, 0)),
                   pl.BlockSpec((N_EXPERTS, 1), const)],
        out_shape=[jax.ShapeDtypeStruct((b, SSM_CHUNK, nc, d), F32),
                   jax.ShapeDtypeStruct((b, SSM_CHUNK, nc, d), BF16),
                   jax.ShapeDtypeStruct((TOP_K, t), I32),
                   jax.ShapeDtypeStruct((TOP_K, t), F32),
                   jax.ShapeDtypeStruct((TOP_K, t), I32),
                   jax.ShapeDtypeStruct((b * nt * POST_SUB, N_EXPERTS, 1), I32),
                   jax.ShapeDtypeStruct((N_EXPERTS, 1), I32)],
        scratch_shapes=[pltpu.VMEM((N_EXPERTS, 1), F32), pltpu.VMEM((2, npos, nc, d), F32),
                        pltpu.SemaphoreType.DMA((2,))],
        compiler_params=_cparams(2),
        name="post",
    )(x.reshape(b, nc, SSM_CHUNK, d), attn, yt, mod, w_glu.T.astype(BF16), col(b_glu), col(ssm_norm), w_out.astype(BF16),
      norm_ffn.reshape(1, -1), w_router.T.astype(BF16), col(b_router), tri)


def _route_kernel(eidx_ref, lrank_ref, r0_ref, cnt_ref, ls_ref, tab_ref, te_ref, nv_ref, nx_ref, pad_ref):
    cnt = cnt_ref[...]
    tiles = (cnt + (RUN - 1 + FFN_ROWS - 1)) // FFN_ROWS
    er = lax.broadcasted_iota(I32, (N_EXPERTS, N_EXPERTS), 0)
    ec = lax.broadcasted_iota(I32, (N_EXPERTS, N_EXPERTS), 1)
    ltri = (ec < er).astype(BF16)

    def excl_cumsum(v):
        vb = jnp.broadcast_to(v.astype(F32), (N_EXPERTS, LANES)).astype(BF16)
        return jnp.dot(ltri, vb, preferred_element_type=F32)[:, 0:1].astype(I32)

    start_t = excl_cumsum(tiles)
    end_t = start_t + tiles
    start = start_t * FFN_ROWS
    pad_ref[...] = start + cnt

    nb = r0_ref.shape[0]
    ts = eidx_ref.shape[1] // nb
    iota_e = lax.broadcasted_iota(I32, (N_EXPERTS, ts), 0)
    iota_t = lax.broadcasted_iota(I32, (N_EXPERTS, TABW), 0)
    chunk = lax.broadcasted_iota(I32, (1, TABW), 1)

    def block(b, carry):
        lanes = pl.ds(pl.multiple_of(b * ts, ts), ts)
        sels = [iota_e == eidx_ref[k:k + 1, lanes] for k in range(TOP_K)]
        member = sels[0].astype(I32) + sels[1].astype(I32) + sels[2].astype(I32) + sels[3].astype(I32)
        nch = (jnp.sum(member, axis=1, keepdims=True) + (RUN - 1)) // RUN
        cb = excl_cumsum(nch)
        end_c = cb + nch
        for k in range(TOP_K):
            first = jnp.sum(jnp.where(sels[k], cb, 0), axis=0, keepdims=True)
            lr = lrank_ref[k:k + 1, lanes]
            ls_ref[k:k + 1, lanes] = (first + lax.shift_right_logical(lr, RUN_SHIFT)) * RUN + (lr & (RUN - 1))
        e_of_c = jnp.sum((chunk >= end_c).astype(I32), axis=0, keepdims=True)
        sel_c = iota_t == e_of_c
        first_c = jnp.sum(jnp.where(sel_c, cb, 0), axis=0, keepdims=True)
        slot0_c = jnp.sum(jnp.where(sel_c, start + r0_ref[b], 0), axis=0, keepdims=True)
        n_chunks = jnp.max(end_c, axis=0, keepdims=True)
        row = jnp.where(chunk < n_chunks, slot0_c + (chunk - first_c) * RUN, -1)
        tab_ref[pl.ds(b, 1), :] = jnp.where(chunk == TABW - 1, n_chunks, row)
        return carry

    lax.fori_loop(0, nb, block, 0, unroll=2)

    nv = jnp.max(end_t, axis=0, keepdims=True)
    width = te_ref.shape[1]
    ti = jnp.minimum(lax.broadcasted_iota(I32, (N_EXPERTS, width), 1), nv - 1)
    te = jnp.minimum(jnp.sum((ti >= end_t).astype(I32), axis=0, keepdims=True), N_EXPERTS - 1)
    te_ref[...] = te
    nv_ref[...] = jnp.broadcast_to(nv, nv_ref.shape)
    ie = lax.broadcasted_iota(I32, (N_EXPERTS, width), 0)
    own_end = jnp.sum(jnp.where(ie == te, end_t, 0), axis=0, keepdims=True)
    nxt = jnp.minimum(jnp.sum((own_end >= end_t).astype(I32), axis=0, keepdims=True), N_EXPERTS - 1)
    nx_ref[...] = jnp.where(own_end < nv, nxt, -1)


def _route(eidx, lrank, r0, cnt, n_tiles):
    t = eidx.shape[1]
    nb = r0.shape[0]
    width = -(-n_tiles // LANES) * LANES
    return pl.pallas_call(
        _route_kernel,
        out_shape=[jax.ShapeDtypeStruct((TOP_K, t), I32),
                   jax.ShapeDtypeStruct((nb, TABW), I32),
                   jax.ShapeDtypeStruct((1, width), I32),
                   jax.ShapeDtypeStruct((1, LANES), I32),
                   jax.ShapeDtypeStruct((1, width), I32),
                   jax.ShapeDtypeStruct((N_EXPERTS, 1), I32)],
        name="route",
    )(eidx, lrank, r0, cnt)


def _for_chunk_pairs(n, fn):
    def body(i, carry):
        fn(2 * i, 0)

        @pl.when(2 * i + 1 < n)
        def _():
            fn(2 * i + 1, 1)
        return carry
    lax.fori_loop(0, lax.shift_right_logical(n + 1, 1), body, 0)


def _local_rows(ts):
    return ts * TOP_K + N_EXPERTS * RUN


def _dispatch_kernel(tab_ref, pad_ref, nvt_ref, h_ref, ls_ref, xs_ref, buf, zbuf, sems, zsem):
    b = pl.program_id(0)
    slot = b % 2
    ts = h_ref.shape[0]
    local = buf.shape[2]

    def chunk_copy(sl, blk, c):
        rows = pl.ds(pl.multiple_of(c * RUN, RUN), RUN)
        return pltpu.make_async_copy(buf.at[sl, :, rows, :], xs_ref.at[:, pl.ds(tab_ref[blk * TABW + c], RUN), :],
                                     sems.at[sl])

    def for_chunks(blk, fn):
        _for_chunk_pairs(tab_ref[blk * TABW + TABW - 1], fn)

    @pl.when(b == 0)
    def _():
        zbuf[...] = jnp.zeros_like(zbuf)
        zrows = zbuf.shape[1]
        zero = lambda row: pltpu.make_async_copy(zbuf, xs_ref.at[:, pl.ds(row, zrows), :], zsem)
        for phase in range(3):
            for e in range(phase, N_EXPERTS, 3):
                zero(pad_ref[e]).start()
            for e in range(phase, N_EXPERTS, 3):
                zero(pad_ref[e]).wait()
        ztile = lambda i: pltpu.make_async_copy(zbuf.at[:, pl.ds(0, FFN_ROWS), :],
                                                xs_ref.at[:, pl.ds((nvt_ref[0] + i) * FFN_ROWS, FFN_ROWS), :], zsem)

        def tail_start(i, carry):
            ztile(i).start()
            return carry

        def tail_wait(i, carry):
            ztile(i).wait()
            return carry
        lax.fori_loop(0, nvt_ref[1] - nvt_ref[0], tail_start, 0)
        lax.fori_loop(0, nvt_ref[1] - nvt_ref[0], tail_wait, 0)

    r = lax.broadcasted_iota(I32, (local, ts), 0)
    hit = (r == ls_ref[0:1, :]) | (r == ls_ref[1:2, :]) | (r == ls_ref[2:3, :]) | (r == ls_ref[3:4, :])
    hit = hit.astype(BF16)
    for pb in range(PANELS):
        srt = jnp.dot(hit, h_ref[:, pb * PANEL_COLS:(pb + 1) * PANEL_COLS], preferred_element_type=F32)
        buf[slot, pb] = _pack_panel(srt, exact=True)

    @pl.when(b > 0)
    def _():
        for_chunks(b - 1, lambda c, p: chunk_copy(1 - slot, b - 1, c).wait())

    for_chunks(b, lambda c, p: chunk_copy(slot, b, c).start(priority=p))

    @pl.when(b == pl.num_programs(0) - 1)
    def _():
        for_chunks(b, lambda c, p: chunk_copy(slot, b, c).wait())


def _dispatch(tab, pad, nvt, h2, ls, n_rows):
    t, d = h2.shape
    nb = tab.shape[0] // TABW
    ts = t // nb
    return pl.pallas_call(
        _dispatch_kernel,
        grid_spec=pltpu.PrefetchScalarGridSpec(
            num_scalar_prefetch=3,
            grid=(nb,),
            in_specs=[pl.BlockSpec((ts, d), lambda i, *_: (i, 0)),
                      pl.BlockSpec((TOP_K, ts), lambda i, *_: (0, i))],
            out_specs=pl.BlockSpec(memory_space=pl.ANY),
            scratch_shapes=[pltpu.VMEM((2, PANELS, _local_rows(ts), LANES), U32),
                            pltpu.VMEM((PANELS, FFN_ROWS + RUN, LANES), U32),
                            pltpu.SemaphoreType.DMA((2,)), pltpu.SemaphoreType.DMA],
        ),
        out_shape=jax.ShapeDtypeStruct((PANELS, n_rows, LANES), U32),
        compiler_params=_cparams(1, ROW_VMEM_BYTES),
        name="dispatch",
    )(tab, pad, nvt, h2, ls)


def _ffn_kernel(te_ref, nv_ref, nx_ref, xs_ref, wgu_hbm, bgu_ref, wd_hbm, bd_ref, perm_ref, ys_ref,
                wgu_stage, wd_stage, wg_scr, wu_scr, wd_scr, bg_scr, bu_scr, sems):
    p = pl.program_id(0)
    t0 = 2 * p
    t1 = t0 + 1
    e0 = te_ref[t0]
    e1 = te_ref[t1]
    v0 = t0 < nv_ref[0]
    v1 = t1 < nv_ref[0]
    new0 = (p == 0) | (e0 != te_ref[jnp.maximum(t0 - 1, 0)])
    same = v1 & (e1 == e0)

    def stage_copies(e):
        return (pltpu.make_async_copy(wgu_hbm.at[e], wgu_stage, sems.at[0]),
                pltpu.make_async_copy(wd_hbm.at[e], wd_stage, sems.at[1]))

    def load_expert(t, first):
        e = te_ref[t]
        if first:
            @pl.when(p == 0)
            def _():
                for cp in stage_copies(e):
                    cp.start()

        for cp in stage_copies(e):
            cp.wait()
        bias = bgu_ref[e]
        for c in range(2 * D_FF // PERM):
            cols = slice(c * PERM, (c + 1) * PERM)
            half = slice(c * (PERM // 2), (c + 1) * (PERM // 2))
            w = wgu_stage[:, cols].astype(BF16)
            pw = jnp.dot(w, perm_ref[...], preferred_element_type=F32).astype(BF16)
            wg_scr[:, half] = pw[:, :PERM // 2]
            wu_scr[:, half] = pw[:, PERM // 2:]
            b1 = bias[:, cols].astype(BF16)
            r1 = bias[:, cols] - b1.astype(F32)
            b2 = r1.astype(BF16)
            b3 = (r1 - b2.astype(F32)).astype(BF16)
            terms = jnp.concatenate([b1, b2, b3, jnp.zeros((5, PERM), BF16)], axis=0)
            pb = jnp.sum(jnp.dot(terms, perm_ref[...], preferred_element_type=F32), axis=0, keepdims=True)
            bg_scr[:, half] = pb[:, :PERM // 2]
            bu_scr[:, half] = pb[:, PERM // 2:]
        wd_scr[...] = wd_stage[...].astype(BF16)

        @pl.when(nx_ref[t] >= 0)
        def _():
            for cp in stage_copies(nx_ref[t]):
                cp.start()

    def run(lo, n, e):
        x = _unpack_panels([xs_ref[pb, lo:lo + n, :] for pb in range(PANELS)])
        gate = jnp.dot(x, wg_scr[...], preferred_element_type=F32) + bg_scr[...]
        up = jnp.dot(x, wu_scr[...], preferred_element_type=F32) + bu_scr[...]
        gate = jnp.minimum(gate, SWIGLU_LIMIT)
        up = jnp.clip(up, -SWIGLU_LIMIT, SWIGLU_LIMIT)
        act = ((up + 1.0) * (gate * jax.nn.sigmoid(SWIGLU_ALPHA * gate))).astype(BF16)
        bd = bd_ref[e]
        for pb in range(PANELS):
            cols = slice(pb * PANEL_COLS, (pb + 1) * PANEL_COLS)
            y = jnp.dot(act, wd_scr[:, cols], preferred_element_type=F32) + bd[:, cols]
            ys_ref[pb, lo:lo + n, :] = _pack_panel(y)

    @pl.when(v0 & new0)
    def _():
        load_expert(t0, True)

    @pl.when(same)
    def _():
        run(0, 2 * FFN_ROWS, e0)

    @pl.when(v0 & jnp.logical_not(same))
    def _():
        run(0, FFN_ROWS, e0)

    @pl.when(v1 & jnp.logical_not(same))
    def _():
        load_expert(t1, False)
        run(FFN_ROWS, FFN_ROWS, e1)

    @pl.when(v0 & jnp.logical_not(v1))
    def _():
        ys_ref[:, FFN_ROWS:, :] = xs_ref[:, FFN_ROWS:, :]


def _ffn(te, nv, nx, xs, w_gate_up, bgu, w_down, bd, n_tiles):
    d = D_MODEL
    pair = lambda i, te, nv, nx: (0, jnp.minimum(i, lax.shift_right_logical(nv[0] - 1, 1)), 0)
    whole = lambda i, te, nv, nx: (0, 0, 0)
    r = lax.broadcasted_iota(I32, (PERM, PERM), 0)
    c = lax.broadcasted_iota(I32, (PERM, PERM), 1)
    perm = (r == jnp.where(c < PERM // 2, 2 * c, 2 * (c - PERM // 2) + 1)).astype(BF16)
    return pl.pallas_call(
        _ffn_kernel,
        grid_spec=pltpu.PrefetchScalarGridSpec(
            num_scalar_prefetch=3,
            grid=(n_tiles // 2,),
            in_specs=[pl.BlockSpec((PANELS, 2 * FFN_ROWS, LANES), pair),
                      pl.BlockSpec(memory_space=pl.ANY),
                      pl.BlockSpec((N_EXPERTS, 1, 2 * D_FF), whole),
                      pl.BlockSpec(memory_space=pl.ANY),
                      pl.BlockSpec((N_EXPERTS, 1, d), whole),
                      pl.BlockSpec((PERM, PERM), lambda i, te, nv, nx: (0, 0))],
            out_specs=pl.BlockSpec((PANELS, 2 * FFN_ROWS, LANES), pair),
            scratch_shapes=[pltpu.VMEM((d, 2 * D_FF), F32), pltpu.VMEM((D_FF, d), F32),
                            pltpu.VMEM((d, D_FF), BF16), pltpu.VMEM((d, D_FF), BF16), pltpu.VMEM((D_FF, d), BF16),
                            pltpu.VMEM((1, D_FF), F32), pltpu.VMEM((1, D_FF), F32),
                            pltpu.SemaphoreType.DMA((2,))],
        ),
        out_shape=jax.ShapeDtypeStruct(xs.shape, U32),
        input_output_aliases={3: 0},
        compiler_params=_cparams(1, FFN_VMEM_BYTES),
        name="ffn",
    )(te, nv, nx, xs, w_gate_up, bgu, w_down, bd, perm)


def _combine_kernel(tab_ref, x1_ref, ls_ref, w_ref, mod_ref, ys_ref, o4_hbm, ybuf, ob_buf, sems, osems, *, n_blk):
    jj = pl.program_id(1)
    blk = pl.program_id(0) * pl.num_programs(1) + jj
    slot = blk % 2
    nc = x1_ref.shape[2]
    tt = POST_POS * nc
    d = x1_ref.shape[3]
    local = ybuf.shape[2]

    def chunk_copy(sl, bk, c):
        rows = pl.ds(pl.multiple_of(c * RUN, RUN), RUN)
        return pltpu.make_async_copy(ys_ref.at[:, pl.ds(tab_ref[bk * TABW + c], RUN), :], ybuf.at[sl, :, rows, :],
                                     sems.at[sl])

    def for_chunks(bk, fn):
        _for_chunk_pairs(tab_ref[bk * TABW + TABW - 1], fn)

    @pl.when(blk == 0)
    def _():
        ybuf[...] = jnp.zeros_like(ybuf)
        for_chunks(0, lambda c, p: chunk_copy(0, 0, c).start(priority=p))

    @pl.when(blk + 1 < n_blk)
    def _():
        for_chunks(blk + 1, lambda c, p: chunk_copy(1 - slot, blk + 1, c).start(priority=p))

    for_chunks(blk, lambda c, p: chunk_copy(slot, blk, c).wait())

    to_cols = lambda a: jnp.concatenate([a, jnp.zeros_like(a)], axis=0).T
    ls_c = to_cols(ls_ref[...].astype(F32))
    w_c = to_cols(w_ref[...])
    r = lax.broadcasted_iota(I32, (tt, local), 1).astype(F32)
    wm = jnp.zeros((tt, local), F32)
    for k in range(TOP_K):
        wm = wm + jnp.where(r == ls_c[:, k:k + 1], w_c[:, k:k + 1], 0.0)
    wm = wm.astype(BF16)
    acc = jnp.dot(wm, _unpack_panels([ybuf[slot, pb] for pb in range(PANELS)]), preferred_element_type=F32)
    out = x1_ref[0].reshape(tt, d) + mod_ref[0, 5:6, :] * acc

    def out_copies(sl, b_, j_):
        return [pltpu.make_async_copy(ob_buf.at[sl, il], o4_hbm.at[b_, :, POST_POS * j_ + il, :], osems.at[sl])
                for il in range(POST_POS)]

    @pl.when(blk >= 2)
    def _():
        for cp in out_copies(slot, 0, 0):
            cp.wait()

    for il in range(POST_POS):
        ob_buf[slot, il] = out[il * nc:(il + 1) * nc]
    for cp in out_copies(slot, pl.program_id(0), jj):
        cp.start()

    @pl.when(blk == n_blk - 1)
    def _():
        for cp in out_copies(slot, 0, 0):
            cp.wait()
        if n_blk > 1:
            for cp in out_copies(1 - slot, 0, 0):
                cp.wait()


def _combine(tab, x1, ls, wts, mod, ys):
    b, _, nc, d = x1.shape
    s = SSM_CHUNK * nc
    tt = POST_POS * nc
    nt = SSM_CHUNK // POST_POS
    o4 = pl.pallas_call(
        functools.partial(_combine_kernel, n_blk=b * nt),
        grid_spec=pltpu.PrefetchScalarGridSpec(
            num_scalar_prefetch=1,
            grid=(b, nt),
            in_specs=[pl.BlockSpec((1, POST_POS, nc, d), lambda bi, j, *_: (bi, j, 0, 0)),
                      pl.BlockSpec((TOP_K, tt), lambda bi, j, *_: (0, bi * nt + j)),
                      pl.BlockSpec((TOP_K, tt), lambda bi, j, *_: (0, bi * nt + j)),
                      pl.BlockSpec((1, 6, d), lambda bi, j, *_: (bi, 0, 0)),
                      pl.BlockSpec(memory_space=pl.ANY)],
            out_specs=pl.BlockSpec(memory_space=pl.ANY),
            scratch_shapes=[pltpu.VMEM((2, PANELS, _local_rows(tt), LANES), U32),
                            pltpu.VMEM((2, POST_POS, nc, d), F32),
                            pltpu.SemaphoreType.DMA((2,)), pltpu.SemaphoreType.DMA((2,))],
        ),
        out_shape=jax.ShapeDtypeStruct((b, nc, SSM_CHUNK, d), F32),
        compiler_params=_cparams(2),
        name="combine",
    )(tab, x1, ls, wts, mod, ys)
    return o4.reshape(b, s, d)


def kernel(x, c, w_ada, b_ada, norm_mix, w_in, b_in, q_norm, k_norm, sinks, lam_re, lam_im, log_dt, b_re, b_im,
           c_re, c_im, d_skip, w_glu, b_glu, attn_out_norm, ssm_out_norm, w_out, norm_ffn, w_router, b_router,
           w_gate_up, b_gate_up, w_down, b_down):
    b, s, d = x.shape
    t = b * s
    depth = w_ada.shape[0]
    n_tiles = -(-(t * TOP_K + N_EXPERTS * (RUN - 1 + FFN_ROWS - 1)) // FFN_ROWS)
    n_tiles += n_tiles % 2
    n_alloc = n_tiles + 2
    for l in range(depth):
        mod = _adaln(c, w_ada[l], b_ada[l]).reshape(b, 6, d)
        q, k, v, ut = _inproj(x, mod, norm_mix[l], w_in[l], b_in[l])
        attn = _attention(q, k, v, sinks[l], q_norm[l], k_norm[l], attn_out_norm[l])
        tt, wz, wyt, cs = _ssm_params(lam_re[l], lam_im[l], log_dt[l], b_re[l], b_im[l], c_re[l], c_im[l])
        yt = _ssm(ut, tt, wz, wyt, cs, d_skip[l])
        x1, h2, eidx, wts, lrank, r0, cnt = _post(x, attn, yt, mod, w_glu[l], b_glu[l], ssm_out_norm[l], w_out[l],
                                                  norm_ffn[l], w_router[l], b_router[l])
        ls, tab, te, nv, nx, pad = _route(eidx, lrank, r0, cnt, n_tiles)
        tab = tab.reshape(-1)
        nvt = jnp.stack([nv[0, 0], jnp.int32(n_alloc)])
        xs = _dispatch(tab, pad.reshape(-1), nvt, h2.reshape(t, d), ls, n_alloc * FFN_ROWS)
        ys = _ffn(te[0, :n_tiles], nv[0, :1], nx[0, :n_tiles], xs, w_gate_up[l], b_gate_up[l][:, None, :],
                  w_down[l], b_down[l][:, None, :], n_tiles)
        x = _combine(tab, x1, ls, wts, mod, ys)
    return x
```

```python
import functools
import math

import jax
import jax.numpy as jnp
from jax import lax
from jax.experimental import pallas as pl
from jax.experimental.pallas import tpu as pltpu

F32 = jnp.float32
BF16 = jnp.bfloat16
U32 = jnp.uint32
I32 = jnp.int32

D_MODEL = 1024
HEAD_DIM = 64
N_HEADS = 8
N_KV_HEADS = 2
Q_PER_KV = N_HEADS // N_KV_HEADS
D_ATTN = N_HEADS * HEAD_DIM
D_KV = N_KV_HEADS * HEAD_DIM
D_QKV = D_ATTN + 2 * D_KV
WINDOW = 128
BLOCK = 128
D_SSM = D_MODEL - D_ATTN
SSM_GROUP = 16
N_GROUPS = D_SSM // SSM_GROUP
STATE = 64
N_EXPERTS = 32
TOP_K = 4
D_FF = D_MODEL
SWIGLU_LIMIT = 7.0
SWIGLU_ALPHA = 1.702
EPS = 1e-6
NEG_INF = -1e30

LANES = 128
SSM_CHUNK = 16
SSM_ROW = SSM_CHUNK * SSM_GROUP
N_POW = 2 * SSM_CHUNK
PANEL_COLS = 2 * LANES
PANELS = D_MODEL // PANEL_COLS

SSM_GROUPS_PER_STEP = 4
POS_PER_STEP = 8
ATTN_ROWS = 512
POST_POS = 2
POST_SUB = 4
FFN_ROWS = 256
RUN = 16
RUN_SHIFT = 4
TABW = 128
PERM = 256
FFN_VMEM_BYTES = 40 * 1024 * 1024
ROW_VMEM_BYTES = 48 * 1024 * 1024

HIGHEST = lax.Precision.HIGHEST
_ARB = "arbitrary"


def _cparams(n, vmem=None):
    return pltpu.CompilerParams(dimension_semantics=(_ARB,) * n, vmem_limit_bytes=vmem)


def _rms(x, axis=-1):
    return x * lax.rsqrt(jnp.mean(x * x, axis=axis, keepdims=True) + EPS)


def _pack_panel(y, exact=False):
    hi, lo = y[:, :LANES], y[:, LANES:]
    if not exact:
        hi = hi.astype(BF16).astype(F32)
        lo = lo.astype(BF16).astype(F32)
    return lax.bitcast_convert_type(hi, U32) | (lax.bitcast_convert_type(lo, U32) >> 16)


def _unpack_panels(words):
    cols = []
    for w in words:
        cols.append(lax.bitcast_convert_type(w & jnp.uint32(0xFFFF0000), F32).astype(BF16))
        cols.append(lax.bitcast_convert_type(w << 16, F32).astype(BF16))
    return jnp.concatenate(cols, axis=-1)


def _prefetch_pos_rows(x4_hbm, buf, sems, n_pos):
    bi = pl.program_id(0)
    j = pl.program_id(1)
    nj = pl.num_programs(1)
    g = bi * nj + j
    slot = g % 2

    def copies(sl, b_, j_):
        return [pltpu.make_async_copy(x4_hbm.at[b_, :, n_pos * j_ + il, :], buf.at[sl, il], sems.at[sl])
                for il in range(n_pos)]

    @pl.when(g == 0)
    def _():
        for cp in copies(0, 0, 0):
            cp.start()

    @pl.when(g + 1 < pl.num_programs(0) * nj)
    def _():
        wrap = j + 1 == nj
        for cp in copies(1 - slot, jnp.where(wrap, bi + 1, bi), jnp.where(wrap, 0, j + 1)):
            cp.start()

    for cp in copies(slot, bi, j):
        cp.wait()
    return slot


def _to_lane_blocks(dst, src):
    for kb in range(dst.shape[0]):
        dst[kb] = src[:, kb * LANES:(kb + 1) * LANES]


def _adaln_kernel(c_ref, w_ref, b_ref, o_ref):
    c = c_ref[...]
    ca = c * jax.nn.sigmoid(c)
    o_ref[...] = jnp.dot(ca, w_ref[...], preferred_element_type=F32, precision=HIGHEST) + b_ref[...]


def _adaln(c, w_ada, b_ada):
    b, d = c.shape
    n = w_ada.shape[1] // d
    return pl.pallas_call(
        _adaln_kernel,
        grid=(n,),
        in_specs=[pl.BlockSpec((b, d), lambda j: (0, 0)),
                  pl.BlockSpec((d, d), lambda j: (0, j)),
                  pl.BlockSpec((1, d), lambda j: (0, j))],
        out_specs=pl.BlockSpec((b, d), lambda j: (0, j)),
        out_shape=jax.ShapeDtypeStruct((b, n * d), F32),
        compiler_params=_cparams(1),
        name="adaln",
    )(c, w_ada, b_ada.reshape(1, -1))


def _inproj_kernel(x4_hbm, x_ref, mod_ref, g_ref, wqkv_ref, bqkv_ref, wut_ref, but_ref, q_ref, k_ref, v_ref, ut_ref,
                   xp_buf, sems):
    nc = ut_ref.shape[3]
    slot = _prefetch_pos_rows(x4_hbm, xp_buf, sems, POS_PER_STEP)
    gain = g_ref[...]
    scale = 1.0 + mod_ref[0, 1:2, :]
    shift = mod_ref[0, 0:1, :]

    def norm_mod(x):
        return (_rms(x) * gain * scale + shift).astype(BF16)

    proj = jnp.dot(norm_mod(x_ref[0]), wqkv_ref[...], preferred_element_type=F32) + bqkv_ref[...]
    q_ref[0] = proj[:, :D_ATTN].astype(BF16)
    k_ref[0] = proj[:, D_ATTN:D_ATTN + D_KV].astype(BF16)
    v_ref[0] = proj[:, D_ATTN + D_KV:].astype(BF16)

    hs = jnp.concatenate([norm_mod(xp_buf[slot, il]) for il in range(POS_PER_STEP)], axis=0)
    ut = lax.dot_general(wut_ref[...], hs, (((1,), (1,)), ((), ())), preferred_element_type=F32) + but_ref[...]
    for il in range(POS_PER_STEP):
        piece = ut[:, il * nc:(il + 1) * nc].astype(BF16)
        ut_ref[0, :, il * SSM_GROUP:(il + 1) * SSM_GROUP, :] = piece.reshape(N_GROUPS, SSM_GROUP, nc)


def _inproj(x, mod, gain, w_in, b_in):
    b, s, d = x.shape
    nc = s // SSM_CHUNK
    rows = POS_PER_STEP * nc
    row = lambda bi, j: (bi, j, 0)
    const = lambda bi, j: (0, 0)
    w_qkv = w_in[:, :D_QKV].astype(BF16)
    w_ut = w_in[:, D_QKV:].T.astype(BF16)
    return pl.pallas_call(
        _inproj_kernel,
        grid=(b, SSM_CHUNK // POS_PER_STEP),
        in_specs=[pl.BlockSpec(memory_space=pl.ANY),
                  pl.BlockSpec((1, rows, d), row),
                  pl.BlockSpec((1, 6, d), lambda bi, j: (bi, 0, 0)),
                  pl.BlockSpec((1, d), const),
                  pl.BlockSpec((d, D_QKV), const),
                  pl.BlockSpec((1, D_QKV), const),
                  pl.BlockSpec((D_SSM, d), const),
                  pl.BlockSpec((D_SSM, 1), const)],
        out_specs=[pl.BlockSpec((1, rows, D_ATTN), row),
                   pl.BlockSpec((1, rows, D_KV), row),
                   pl.BlockSpec((1, rows, D_KV), row),
                   pl.BlockSpec((1, N_GROUPS, POS_PER_STEP * SSM_GROUP, nc), lambda bi, j: (bi, 0, j, 0))],
        out_shape=[jax.ShapeDtypeStruct((b, s, D_ATTN), BF16),
                   jax.ShapeDtypeStruct((b, s, D_KV), BF16),
                   jax.ShapeDtypeStruct((b, s, D_KV), BF16),
                   jax.ShapeDtypeStruct((b, N_GROUPS, SSM_ROW, nc), BF16)],
        scratch_shapes=[pltpu.VMEM((2, POS_PER_STEP, nc, d), F32), pltpu.SemaphoreType.DMA((2,))],
        compiler_params=_cparams(2),
        name="inproj",
    )(x.reshape(b, nc, SSM_CHUNK, d), x, mod, gain.reshape(1, d), w_qkv, b_in[:D_QKV].reshape(1, D_QKV), w_ut, b_in[D_QKV:].reshape(D_SSM, 1))


def _half_norm(x, low):
    sq = x * x
    s_lo = jnp.sum(jnp.where(low, sq, 0.0), axis=-1, keepdims=True)
    s_hi = jnp.sum(sq, axis=-1, keepdims=True) - s_lo
    inv = 1.0 / HEAD_DIM
    scale = jnp.where(low, lax.rsqrt(s_lo * inv + EPS), lax.rsqrt(s_hi * inv + EPS))
    return x * scale


def _attn_block(first, q, k_prev, k_cur, vt_prev, vt_cur, sinks_ref, qn, low, upper_t, lane_blk, low_rows):
    no_prev = jnp.where(first, NEG_INF, 0.0)
    out_blocks = []
    for hk in range(N_KV_HEADS):
        qs = []
        for j in range(Q_PER_KV // 2):
            blk = hk * (Q_PER_KV // 2) + j
            qb = _half_norm(q[:, blk * LANES:(blk + 1) * LANES], low) * qn * (1.0 / math.sqrt(HEAD_DIM))
            qs.append(jnp.where(low, qb, 0.0))
            qs.append(jnp.where(low, 0.0, qb))
        qg = jnp.concatenate(qs, axis=0).astype(BF16)
        nt = (((1,), (1,)), ((), ()))
        st_prev = lax.dot_general(k_prev[hk], qg, nt, preferred_element_type=F32)
        st_cur = lax.dot_general(k_cur[hk], qg, nt, preferred_element_type=F32)
        st = jnp.where(upper_t, st_prev + no_prev, st_cur)
        sink = jnp.zeros((1, Q_PER_KV * BLOCK), F32)
        for g in range(Q_PER_KV):
            sink = jnp.where(lane_blk == g, sinks_ref[hk * Q_PER_KV + g], sink)
        m = jnp.maximum(jnp.max(st, axis=0, keepdims=True), sink)
        pt = jnp.exp(st - m)
        den = jnp.sum(pt, axis=0, keepdims=True) + jnp.exp(sink - m)
        ot = (jnp.dot(vt_prev[hk], jnp.where(upper_t, pt, 0.0).astype(BF16), preferred_element_type=F32)
              + jnp.dot(vt_cur[hk], jnp.where(upper_t, 0.0, pt).astype(BF16), preferred_element_type=F32)) / den
        for j in range(Q_PER_KV // 2):
            ev = ot[:, (2 * j) * BLOCK:(2 * j + 1) * BLOCK]
            od = ot[:, (2 * j + 1) * BLOCK:(2 * j + 2) * BLOCK]
            out_blocks.append(jnp.where(low_rows, ev, od))
    return jnp.concatenate(out_blocks, axis=0)


def _attn_kernel(sinks_ref, q_ref, k_ref, v_ref, qn_ref, kn_ref, on_ref, o_hbm, a_buf, sems, *, n_steps):
    step = pl.program_id(1)
    g = pl.program_id(0) * pl.num_programs(1) + step
    slot = g % 2
    cps = ATTN_ROWS // SSM_CHUNK
    nq = ATTN_ROWS // BLOCK

    def out_copies(sl, b_, s_):
        return [pltpu.make_async_copy(a_buf.at[sl, :, i, :], o_hbm.at[b_, i, pl.ds(s_ * cps, cps), :], sems.at[sl])
                for i in range(SSM_CHUNK)]

    @pl.when(g >= 2)
    def _():
        for cp in out_copies(slot, 0, 0):
            cp.wait()

    low = lax.broadcasted_iota(I32, (1, LANES), 1) < HEAD_DIM
    low_rows = lax.broadcasted_iota(I32, (BLOCK, 1), 0) < HEAD_DIM
    cols = Q_PER_KV * BLOCK
    upper_t = lax.broadcasted_iota(I32, (BLOCK, cols), 0) > lax.broadcasted_iota(I32, (BLOCK, cols), 1) % BLOCK
    lane_blk = lax.broadcasted_iota(I32, (1, cols), 1) // BLOCK

    cur = pl.multiple_of(step * ATTN_ROWS, ATTN_ROWS)
    prev = pl.multiple_of(jnp.maximum(step * nq - 1, 0) * BLOCK, BLOCK)
    kall = jnp.concatenate([k_ref[0, pl.ds(prev, BLOCK), :], k_ref[0, pl.ds(cur, ATTN_ROWS), :]], axis=0).astype(F32)
    vall = jnp.concatenate([v_ref[0, pl.ds(prev, BLOCK), :], v_ref[0, pl.ds(cur, ATTN_ROWS), :]], axis=0).astype(F32)
    kall = _half_norm(kall, low) * kn_ref[...]
    kswap = pltpu.roll(kall, HEAD_DIM, axis=1)
    vswap = pltpu.roll(vall, HEAD_DIM, axis=1)
    k_dup = [jnp.where(low, kall, kswap).astype(BF16), jnp.where(low, kswap, kall).astype(BF16)]
    vt_dup = [jnp.where(low, vall, vswap).T.astype(BF16), jnp.where(low, vswap, vall).T.astype(BF16)]
    kblk = lambda i: [k_dup[hk][i * BLOCK:(i + 1) * BLOCK] for hk in range(N_KV_HEADS)]
    vblk = lambda i: [vt_dup[hk][:, i * BLOCK:(i + 1) * BLOCK] for hk in range(N_KV_HEADS)]

    for qb in range(nq):
        q = q_ref[0, qb * BLOCK:(qb + 1) * BLOCK, :].astype(F32)
        attn_t = _attn_block((step == 0) if qb == 0 else False, q, kblk(qb), kblk(qb + 1), vblk(qb), vblk(qb + 1),
                             sinks_ref, qn_ref[...], low, upper_t, lane_blk, low_rows)
        attn = (_rms(attn_t, axis=0) * on_ref[...]).T
        cpb = BLOCK // SSM_CHUNK
        a_buf[slot, qb * cpb:(qb + 1) * cpb] = attn.reshape(cpb, SSM_CHUNK, D_ATTN)

    for cp in out_copies(slot, pl.program_id(0), step):
        cp.start()

    @pl.when(g == n_steps - 1)
    def _():
        for cp in out_copies(slot, 0, 0):
            cp.wait()
        if n_steps > 1:
            for cp in out_copies(1 - slot, 0, 0):
                cp.wait()


def _attention(q, k, v, sinks, q_norm, k_norm, out_norm):
    b, s, _ = q.shape
    tile2 = lambda g: jnp.tile(g.reshape(1, HEAD_DIM), (1, 2))
    cps = ATTN_ROWS // SSM_CHUNK
    return pl.pallas_call(
        functools.partial(_attn_kernel, n_steps=b * (s // ATTN_ROWS)),
        grid=(b, s // ATTN_ROWS),
        in_specs=[pl.BlockSpec(memory_space=pltpu.SMEM),
                  pl.BlockSpec((1, ATTN_ROWS, D_ATTN), lambda bi, n: (bi, n, 0)),
                  pl.BlockSpec((1, s, D_KV), lambda bi, n: (bi, 0, 0)),
                  pl.BlockSpec((1, s, D_KV), lambda bi, n: (bi, 0, 0)),
                  pl.BlockSpec((1, LANES), lambda bi, n: (0, 0)),
                  pl.BlockSpec((1, LANES), lambda bi, n: (0, 0)),
                  pl.BlockSpec((D_ATTN, 1), lambda bi, n: (0, 0))],
        out_specs=pl.BlockSpec(memory_space=pl.ANY),
        out_shape=jax.ShapeDtypeStruct((b, SSM_CHUNK, s // SSM_CHUNK, D_ATTN), F32),
        scratch_shapes=[pltpu.VMEM((2, cps, SSM_CHUNK, D_ATTN), F32), pltpu.SemaphoreType.DMA((2,))],
        compiler_params=_cparams(2),
        name="attention",
    )(sinks, q, k, v, tile2(q_norm), tile2(k_norm), out_norm.reshape(D_ATTN, 1))


def _cmul(ar, ai, br, bi):
    return ar * br - ai * bi, ar * bi + ai * br


def _ssm_param_kernel(*refs):
    for gi in range(refs[0].shape[0]):
        _ssm_param_group(gi, *refs)


def _ssm_param_group(gi, lam_ref, bre_ref, bim_ref, cre_ref, cim_ref, tt_ref, wz_ref, wyt_ref, cs_ref):
    f32dot = functools.partial(jnp.dot, preferred_element_type=F32, precision=HIGHEST)
    lr = lam_ref[gi, 0:1, :]
    li = lam_ref[gi, 1:2, :]
    dt = jnp.exp(lam_ref[gi, 2:3, :])
    rho = lr * dt
    th = li * dt
    imag_lane = lax.broadcasted_iota(I32, (1, LANES), 1) >= STATE

    kk = (lax.broadcasted_iota(I32, (N_POW, 1), 0) - (SSM_CHUNK - 1)).astype(F32)
    mag = jnp.exp(rho * kk)
    pw_r = mag * jnp.cos(th * kk)
    pw_i = mag * jnp.sin(th * kk)
    lb_r = pw_r[SSM_CHUNK:SSM_CHUNK + 1]
    lb_i = pw_i[SSM_CHUNK:SSM_CHUNK + 1]
    den = lr * lr + li * li
    coef_r = ((lb_r - 1.0) * lr + lb_i * li) / den
    coef_i = (lb_i * lr - (lb_r - 1.0) * li) / den

    eye = (lax.broadcasted_iota(I32, (SSM_GROUP, SSM_GROUP), 0)
           == lax.broadcasted_iota(I32, (SSM_GROUP, SSM_GROUP), 1)).astype(F32)
    lane_fold = (lax.broadcasted_iota(I32, (STATE, LANES), 1) % STATE
                 == lax.broadcasted_iota(I32, (STATE, LANES), 0)).astype(F32)

    def tile_pos(x):
        return jnp.concatenate([x] * SSM_CHUNK, axis=0)

    def power_rows(k_of_pos):
        idx = [k_of_pos(p) + (SSM_CHUNK - 1) for p in range(SSM_CHUNK)]
        rep = lambda t: jnp.concatenate([jnp.broadcast_to(t[r:r + 1], (SSM_GROUP, LANES)) for r in idx], axis=0)
        return rep(pw_r), rep(pw_i)

    def b_rows(b_ref):
        b2 = jnp.concatenate([b_ref[gi], b_ref[gi]], axis=0)
        return tile_pos(lax.dot_general(eye, b2, (((1,), (1,)), ((), ())), preferred_element_type=F32,
                                        precision=HIGHEST))

    def c_rows(c_ref):
        return tile_pos(f32dot(c_ref[gi], lane_fold))

    bbar_r, bbar_i = _cmul(coef_r, coef_i, b_rows(bre_ref), b_rows(bim_ref))
    c_r = c_rows(cre_ref)
    c_i = c_rows(cim_ref)

    a_r, a_i = _cmul(bbar_r, bbar_i, *power_rows(lambda p: -p))
    a2c = jnp.where(imag_lane, -a_i, a_r)
    m_r, m_i = _cmul(c_r, c_i, *power_rows(lambda p: p))
    bmc = jnp.where(imag_lane, m_i, m_r)
    tt = f32dot(bmc, a2c.T)
    causal = (lax.broadcasted_iota(I32, (SSM_ROW, 1), 0) // SSM_GROUP
              >= lax.broadcasted_iota(I32, (1, SSM_ROW), 1) // SSM_GROUP)
    tt_ref[gi] = jnp.where(causal, tt, 0.0).astype(BF16)

    w_r, w_i = _cmul(bbar_r, bbar_i, *power_rows(lambda p: SSM_CHUNK - 1 - p))
    wz_ref[gi, :, :LANES] = jnp.where(imag_lane, w_i, w_r).astype(BF16)
    wz_ref[gi, :, LANES:] = jnp.where(imag_lane, w_r, w_i).astype(BF16)

    y_r, y_i = _cmul(c_r, c_i, *power_rows(lambda p: p + 1))
    wyt_ref[gi] = jnp.where(imag_lane, -y_i, y_r).astype(BF16)

    cs_ref[gi, 0:1, :] = pw_r[N_POW - 1:N_POW]
    cs_ref[gi, 1:2, :] = jnp.where(imag_lane, pw_i[N_POW - 1:N_POW], -pw_i[N_POW - 1:N_POW])


def _ssm_params(lam_re, lam_im, log_dt, b_re, b_im, c_re, c_im):
    g = lam_re.shape[0]
    lam = jnp.stack([lam_re, lam_im, jnp.broadcast_to(log_dt[:, None], (g, STATE))], axis=1)
    lam = jnp.concatenate([lam, lam], axis=2)
    ng = SSM_GROUPS_PER_STEP
    blk = lambda *shape: pl.BlockSpec((ng,) + shape, lambda i: (i, 0, 0))
    return pl.pallas_call(
        _ssm_param_kernel,
        grid=(g // ng,),
        in_specs=[blk(3, LANES), blk(STATE, SSM_GROUP), blk(STATE, SSM_GROUP), blk(SSM_GROUP, STATE),
                  blk(SSM_GROUP, STATE)],
        out_specs=[blk(SSM_ROW, SSM_ROW), blk(SSM_ROW, SSM_ROW), blk(SSM_ROW, LANES), blk(2, LANES)],
        out_shape=[jax.ShapeDtypeStruct((g, SSM_ROW, SSM_ROW), BF16),
                   jax.ShapeDtypeStruct((g, SSM_ROW, SSM_ROW), BF16),
                   jax.ShapeDtypeStruct((g, SSM_ROW, LANES), BF16),
                   jax.ShapeDtypeStruct((g, 2, LANES), F32)],
        compiler_params=_cparams(1),
        name="ssm_params",
    )(lam, b_re, b_im, c_re, c_im)


def _ssm_kernel(ut_ref, tt_ref, wz_ref, wyt_ref, cs_ref, d_ref, yt_ref, z_scr, s_scr):
    batch, ng, _, nc = ut_ref.shape
    uts = [jnp.concatenate([ut_ref[b, gi] for b in range(batch)], axis=1) for gi in range(ng)]
    for gi in range(ng):
        z = lax.dot_general(uts[gi], wz_ref[gi], (((0,), (0,)), ((), ())), preferred_element_type=F32)
        _to_lane_blocks(z_scr.at[gi], z)
    c1 = [cs_ref[gi, 0:1, :] for gi in range(ng)]
    c2 = [cs_ref[gi, 1:2, :] for gi in range(ng)]

    def step(c, carry):
        rows = pl.ds(c, batch, stride=nc)
        out = []
        for gi in range(ng):
            s1, s2 = carry[gi]
            s_scr[gi, rows, :] = s1
            out.append((c1[gi] * s1 + c2[gi] * s2 + z_scr[gi, 0, rows, :],
                        c1[gi] * s2 - c2[gi] * s1 + z_scr[gi, 1, rows, :]))
        return tuple(out)

    zero = jnp.zeros((batch, LANES), F32)
    lax.fori_loop(0, nc, step, ((zero, zero),) * ng, unroll=8)
    for gi in range(ng):
        y = jnp.dot(tt_ref[gi], uts[gi], preferred_element_type=F32)
        y = y + lax.dot_general(wyt_ref[gi], s_scr[gi].astype(BF16), (((1,), (1,)), ((), ())),
                                preferred_element_type=F32)
        y = y + d_ref[gi] * uts[gi].astype(F32)
        for b in range(batch):
            yt_ref[b, gi] = y[:, b * nc:(b + 1) * nc]


def _ssm(ut, tt, wz, wyt, cs, d_skip):
    b, g, _, nc = ut.shape
    ng = SSM_GROUPS_PER_STEP
    d_col = jnp.tile(d_skip.reshape(g, 1, SSM_GROUP), (1, SSM_CHUNK, 1)).reshape(g, SSM_ROW, 1)
    blk = lambda *shape: pl.BlockSpec((ng,) + shape, lambda i: (i, 0, 0))
    act = pl.BlockSpec((b, ng, SSM_ROW, nc), lambda i: (0, i, 0, 0))
    return pl.pallas_call(
        _ssm_kernel,
        grid=(g // ng,),
        in_specs=[act, blk(SSM_ROW, SSM_ROW), blk(SSM_ROW, SSM_ROW), blk(SSM_ROW, LANES), blk(2, LANES),
                  blk(SSM_ROW, 1)],
        out_specs=act,
        out_shape=jax.ShapeDtypeStruct((b, g, SSM_ROW, nc), F32),
        scratch_shapes=[pltpu.VMEM((ng, SSM_ROW // LANES, b * nc, LANES), F32), pltpu.VMEM((ng, b * nc, LANES), F32)],
        compiler_params=_cparams(1),
        name="ssm",
    )(ut, tt, wz, wyt, cs, d_col)


def _post_kernel(x4_hbm, attn_ref, yt_ref, mod_ref, wglut_ref, bglu_ref, sn_ref, wout_ref, nf_ref, wr_ref, br_ref,
                 tri_ref, x1_ref, h2_ref, eidx_ref, wts_ref, lrank_ref, r0_ref, cnt_ref, carry_ref, xp_buf, sems):
    @pl.when((pl.program_id(0) == 0) & (pl.program_id(1) == 0))
    def _():
        carry_ref[...] = jnp.zeros_like(carry_ref)

    slot = _prefetch_pos_rows(x4_hbm, xp_buf, sems, POST_SUB * POST_POS)
    nc = attn_ref.shape[2]
    ts = POST_POS * nc
    d = x1_ref.shape[3]
    iota_e = lax.broadcasted_iota(I32, (N_EXPERTS, ts), 0).astype(F32)
    counts = []
    for sub in range(POST_SUB):
        pos = range(sub * POST_POS, (sub + 1) * POST_POS)
        lanes = slice(sub * ts, (sub + 1) * ts)
        yt = jnp.concatenate(
            [yt_ref[0, :, il * SSM_GROUP:(il + 1) * SSM_GROUP, :].reshape(D_SSM, nc) for il in pos], axis=1)
        g = jax.nn.gelu(yt)
        gate = jax.nn.sigmoid(jnp.dot(wglut_ref[...], g.astype(BF16), preferred_element_type=F32) + bglu_ref[...])
        ssm_t = _rms(g * gate, axis=0) * sn_ref[...]
        attn = attn_ref[0, sub * POST_POS:(sub + 1) * POST_POS].reshape(ts, D_ATTN)
        mixed = jnp.concatenate([attn.astype(BF16), ssm_t.T.astype(BF16)], axis=-1)
        o = jnp.dot(mixed, wout_ref[...], preferred_element_type=F32)
        x = jnp.concatenate([xp_buf[slot, il] for il in pos], axis=0)
        x1 = x + mod_ref[0, 2:3, :] * o
        x1_ref[0, sub * POST_POS:(sub + 1) * POST_POS] = x1.reshape(POST_POS, nc, d)
        h2 = _rms(x1) * nf_ref[...] * (1.0 + mod_ref[0, 4:5, :]) + mod_ref[0, 3:4, :]
        h2_ref[0, sub * POST_POS:(sub + 1) * POST_POS] = h2.astype(BF16).reshape(POST_POS, nc, d)

        logits = lax.dot_general(wr_ref[...], h2.astype(BF16), (((1,), (1,)), ((), ())),
                                 preferred_element_type=F32) + br_ref[...]
        l = logits
        idxs, vals = [], []
        for _ in range(TOP_K):
            m = jnp.max(l, axis=0, keepdims=True)
            idx = jnp.min(jnp.where(l == m, iota_e, float(N_EXPERTS)), axis=0, keepdims=True)
            idxs.append(idx)
            vals.append(m)
            l = jnp.where(iota_e == idx, -jnp.inf, l)
        es = [jnp.exp(v - vals[0]) for v in vals]
        tot = es[0] + es[1] + es[2] + es[3]
        member = jnp.zeros((N_EXPERTS, ts), F32)
        for idx in idxs:
            member = member + (iota_e == idx).astype(F32)
        before = jnp.dot(member.astype(BF16), tri_ref[...], preferred_element_type=F32)
        for k in range(TOP_K):
            eidx_ref[k:k + 1, lanes] = idxs[k].astype(I32)
            wts_ref[k:k + 1, lanes] = es[k] / tot
            lrank_ref[k:k + 1, lanes] = jnp.sum(jnp.where(iota_e == idxs[k], before, 0.0), axis=0,
                                                keepdims=True).astype(I32)
        counts.append(jnp.sum(member, axis=1, keepdims=True))

    carry = carry_ref[...]
    for sub in range(POST_SUB):
        r0_ref[sub] = carry.astype(I32)
        carry = carry + counts[sub]
    carry_ref[...] = carry
    cnt_ref[...] = carry.astype(I32)


def _post(x, attn, yt, mod, w_glu, b_glu, ssm_norm, w_out, norm_ffn, w_router, b_router):
    b, s, d = x.shape
    nc = s // SSM_CHUNK
    ts = POST_POS * nc
    npos = POST_SUB * POST_POS
    nt = SSM_CHUNK // npos
    t = b * s
    pm = lambda bi, j: (bi, j, 0, 0)
    const = lambda bi, j: (0, 0)
    tok = lambda bi, j: (0, bi * nt + j)
    tri = (lax.broadcasted_iota(I32, (ts, ts), 0) < lax.broadcasted_iota(I32, (ts, ts), 1)).astype(BF16)
    col = lambda a: a.reshape(-1, 1)
    return pl.pallas_call(
        _post_kernel,
        grid=(b, nt),
        in_specs=[pl.BlockSpec(memory_space=pl.ANY),
                  pl.BlockSpec((1, npos, nc, D_ATTN), pm),
                  pl.BlockSpec((1, N_GROUPS, npos * SSM_GROUP, nc), lambda bi, j: (bi, 0, j, 0)),
                  pl.BlockSpec((1, 6, d), lambda bi, j: (bi, 0, 0)),
                  pl.BlockSpec((D_SSM, D_SSM), const),
                  pl.BlockSpec((D_SSM, 1), const),
                  pl.BlockSpec((D_SSM, 1), const),
                  pl.BlockSpec((d, d), const),
                  pl.BlockSpec((1, d), const),
                  pl.BlockSpec((N_EXPERTS, d), const),
                  pl.BlockSpec((N_EXPERTS, 1), const),
                  pl.BlockSpec((ts, ts), const)],
        out_specs=[pl.BlockSpec((1, npos, nc, d), pm),
                   pl.BlockSpec((1, npos, nc, d), pm),
                   pl.BlockSpec((TOP_K, POST_SUB * ts), tok),
                   pl.BlockSpec((TOP_K, POST_SUB * ts), tok),
                   pl.BlockSpec((TOP_K, POST_SUB * ts), tok),
                   pl.BlockSpec((POST_SUB, N_EXPERTS, 1), lambda bi, j: (bi * nt + j, 0, 0)),
                   pl.BlockSpec((N_EXPERTS, 1), const)],
        out_shape=[jax.ShapeDtypeStruct((b, SSM_CHUNK, nc, d), F32),
                   jax.ShapeDtypeStruct((b, SSM_CHUNK, nc, d), BF16),
                   jax.ShapeDtypeStruct((TOP_K, t), I32),
                   jax.ShapeDtypeStruct((TOP_K, t), F32),
                   jax.ShapeDtypeStruct((TOP_K, t), I32),
                   jax.ShapeDtypeStruct((b * nt * POST_SUB, N_EXPERTS, 1), I32),
                   jax.ShapeDtypeStruct((N_EXPERTS, 1), I32)],
        scratch_shapes=[pltpu.VMEM((N_EXPERTS, 1), F32), pltpu.VMEM((2, npos, nc, d), F32),
                        pltpu.SemaphoreType.DMA((2,))],
        compiler_params=_cparams(2),
        name="post",
    )(x.reshape(b, nc, SSM_CHUNK, d), attn, yt, mod, w_glu.T.astype(BF16), col(b_glu), col(ssm_norm), w_out.astype(BF16),
      norm_ffn.reshape(1, -1), w_router.T.astype(BF16), col(b_router), tri)


def _route_kernel(eidx_ref, lrank_ref, r0_ref, cnt_ref, ls_ref, tab_ref, te_ref, nv_ref, nx_ref, pad_ref):
    cnt = cnt_ref[...]
    tiles = (cnt + (RUN - 1 + FFN_ROWS - 1)) // FFN_ROWS
    er = lax.broadcasted_iota(I32, (N_EXPERTS, N_EXPERTS), 0)
    ec = lax.broadcasted_iota(I32, (N_EXPERTS, N_EXPERTS), 1)
    ltri = (ec < er).astype(BF16)

    def excl_cumsum(v):
        vb = jnp.broadcast_to(v.astype(F32), (N_EXPERTS, LANES)).astype(BF16)
        return jnp.dot(ltri, vb, preferred_element_type=F32)[:, 0:1].astype(I32)

    start_t = excl_cumsum(tiles)
    end_t = start_t + tiles
    start = start_t * FFN_ROWS
    pad_ref[...] = start + cnt

    nb = r0_ref.shape[0]
    ts = eidx_ref.shape[1] // nb
    iota_e = lax.broadcasted_iota(I32, (N_EXPERTS, ts), 0)
    iota_t = lax.broadcasted_iota(I32, (N_EXPERTS, TABW), 0)
    chunk = lax.broadcasted_iota(I32, (1, TABW), 1)

    def block(b, carry):
        lanes = pl.ds(pl.multiple_of(b * ts, ts), ts)
        sels = [iota_e == eidx_ref[k:k + 1, lanes] for k in range(TOP_K)]
        member = sels[0].astype(I32) + sels[1].astype(I32) + sels[2].astype(I32) + sels[3].astype(I32)
        nch = (jnp.sum(member, axis=1, keepdims=True) + (RUN - 1)) // RUN
        cb = excl_cumsum(nch)
        end_c = cb + nch
        for k in range(TOP_K):
            first = jnp.sum(jnp.where(sels[k], cb, 0), axis=0, keepdims=True)
            lr = lrank_ref[k:k + 1, lanes]
            ls_ref[k:k + 1, lanes] = (first + lax.shift_right_logical(lr, RUN_SHIFT)) * RUN + (lr & (RUN - 1))
        e_of_c = jnp.sum((chunk >= end_c).astype(I32), axis=0, keepdims=True)
        sel_c = iota_t == e_of_c
        first_c = jnp.sum(jnp.where(sel_c, cb, 0), axis=0, keepdims=True)
        slot0_c = jnp.sum(jnp.where(sel_c, start + r0_ref[b], 0), axis=0, keepdims=True)
        n_chunks = jnp.max(end_c, axis=0, keepdims=True)
        row = jnp.where(chunk < n_chunks, slot0_c + (chunk - first_c) * RUN, -1)
        tab_ref[pl.ds(b, 1), :] = jnp.where(chunk == TABW - 1, n_chunks, row)
        return carry

    lax.fori_loop(0, nb, block, 0, unroll=2)

    nv = jnp.max(end_t, axis=0, keepdims=True)
    width = te_ref.shape[1]
    ti = jnp.minimum(lax.broadcasted_iota(I32, (N_EXPERTS, width), 1), nv - 1)
    te = jnp.minimum(jnp.sum((ti >= end_t).astype(I32), axis=0, keepdims=True), N_EXPERTS - 1)
    te_ref[...] = te
    nv_ref[...] = jnp.broadcast_to(nv, nv_ref.shape)
    ie = lax.broadcasted_iota(I32, (N_EXPERTS, width), 0)
    own_end = jnp.sum(jnp.where(ie == te, end_t, 0), axis=0, keepdims=True)
    nxt = jnp.minimum(jnp.sum((own_end >= end_t).astype(I32), axis=0, keepdims=True), N_EXPERTS - 1)
    nx_ref[...] = jnp.where(own_end < nv, nxt, -1)


def _route(eidx, lrank, r0, cnt, n_tiles):
    t = eidx.shape[1]
    nb = r0.shape[0]
    width = -(-n_tiles // LANES) * LANES
    return pl.pallas_call(
        _route_kernel,
        out_shape=[jax.ShapeDtypeStruct((TOP_K, t), I32),
                   jax.ShapeDtypeStruct((nb, TABW), I32),
                   jax.ShapeDtypeStruct((1, width), I32),
                   jax.ShapeDtypeStruct((1, LANES), I32),
                   jax.ShapeDtypeStruct((1, width), I32),
                   jax.ShapeDtypeStruct((N_EXPERTS, 1), I32)],
        name="route",
    )(eidx, lrank, r0, cnt)


def _for_chunk_pairs(n, fn):
    def body(i, carry):
        fn(2 * i, 0)

        @pl.when(2 * i + 1 < n)
        def _():
            fn(2 * i + 1, 1)
        return carry
    lax.fori_loop(0, lax.shift_right_logical(n + 1, 1), body, 0)


def _local_rows(ts):
    return ts * TOP_K + N_EXPERTS * RUN


def _dispatch_kernel(tab_ref, pad_ref, nvt_ref, h_ref, ls_ref, xs_ref, buf, zbuf, sems, zsem):
    b = pl.program_id(0)
    slot = b % 2
    ts = h_ref.shape[0]
    local = buf.shape[2]

    def chunk_copy(sl, blk, c):
        rows = pl.ds(pl.multiple_of(c * RUN, RUN), RUN)
        return pltpu.make_async_copy(buf.at[sl, :, rows, :], xs_ref.at[:, pl.ds(tab_ref[blk * TABW + c], RUN), :],
                                     sems.at[sl])

    def for_chunks(blk, fn):
        _for_chunk_pairs(tab_ref[blk * TABW + TABW - 1], fn)

    @pl.when(b == 0)
    def _():
        zbuf[...] = jnp.zeros_like(zbuf)
        zrows = zbuf.shape[1]
        zero = lambda row: pltpu.make_async_copy(zbuf, xs_ref.at[:, pl.ds(row, zrows), :], zsem)
        for phase in range(3):
            for e in range(phase, N_EXPERTS, 3):
                zero(pad_ref[e]).start()
            for e in range(phase, N_EXPERTS, 3):
                zero(pad_ref[e]).wait()
        ztile = lambda i: pltpu.make_async_copy(zbuf.at[:, pl.ds(0, FFN_ROWS), :],
                                                xs_ref.at[:, pl.ds((nvt_ref[0] + i) * FFN_ROWS, FFN_ROWS), :], zsem)

        def tail_start(i, carry):
            ztile(i).start()
            return carry

        def tail_wait(i, carry):
            ztile(i).wait()
            return carry
        lax.fori_loop(0, nvt_ref[1] - nvt_ref[0], tail_start, 0)
        lax.fori_loop(0, nvt_ref[1] - nvt_ref[0], tail_wait, 0)

    r = lax.broadcasted_iota(I32, (local, ts), 0)
    hit = (r == ls_ref[0:1, :]) | (r == ls_ref[1:2, :]) | (r == ls_ref[2:3, :]) | (r == ls_ref[3:4, :])
    hit = hit.astype(BF16)
    for pb in range(PANELS):
        srt = jnp.dot(hit, h_ref[:, pb * PANEL_COLS:(pb + 1) * PANEL_COLS], preferred_element_type=F32)
        buf[slot, pb] = _pack_panel(srt, exact=True)

    @pl.when(b > 0)
    def _():
        for_chunks(b - 1, lambda c, p: chunk_copy(1 - slot, b - 1, c).wait())

    for_chunks(b, lambda c, p: chunk_copy(slot, b, c).start(priority=p))

    @pl.when(b == pl.num_programs(0) - 1)
    def _():
        for_chunks(b, lambda c, p: chunk_copy(slot, b, c).wait())


def _dispatch(tab, pad, nvt, h2, ls, n_rows):
    t, d = h2.shape
    nb = tab.shape[0] // TABW
    ts = t // nb
    return pl.pallas_call(
        _dispatch_kernel,
        grid_spec=pltpu.PrefetchScalarGridSpec(
            num_scalar_prefetch=3,
            grid=(nb,),
            in_specs=[pl.BlockSpec((ts, d), lambda i, *_: (i, 0)),
                      pl.BlockSpec((TOP_K, ts), lambda i, *_: (0, i))],
            out_specs=pl.BlockSpec(memory_space=pl.ANY),
            scratch_shapes=[pltpu.VMEM((2, PANELS, _local_rows(ts), LANES), U32),
                            pltpu.VMEM((PANELS, FFN_ROWS + RUN, LANES), U32),
                            pltpu.SemaphoreType.DMA((2,)), pltpu.SemaphoreType.DMA],
        ),
        out_shape=jax.ShapeDtypeStruct((PANELS, n_rows, LANES), U32),
        compiler_params=_cparams(1, ROW_VMEM_BYTES),
        name="dispatch",
    )(tab, pad, nvt, h2, ls)


def _ffn_kernel(te_ref, nv_ref, nx_ref, xs_ref, wgu_hbm, bgu_ref, wd_hbm, bd_ref, perm_ref, ys_ref,
                wgu_stage, wd_stage, wg_scr, wu_scr, wd_scr, bg_scr, bu_scr, sems):
    p = pl.program_id(0)
    t0 = 2 * p
    t1 = t0 + 1
    e0 = te_ref[t0]
    e1 = te_ref[t1]
    v0 = t0 < nv_ref[0]
    v1 = t1 < nv_ref[0]
    new0 = (p == 0) | (e0 != te_ref[jnp.maximum(t0 - 1, 0)])
    same = v1 & (e1 == e0)

    def stage_copies(e):
        return (pltpu.make_async_copy(wgu_hbm.at[e], wgu_stage, sems.at[0]),
                pltpu.make_async_copy(wd_hbm.at[e], wd_stage, sems.at[1]))

    def load_expert(t, first):
        e = te_ref[t]
        if first:
            @pl.when(p == 0)
            def _():
                for cp in stage_copies(e):
                    cp.start()

        for cp in stage_copies(e):
            cp.wait()
        bias = bgu_ref[e]
        for c in range(2 * D_FF // PERM):
            cols = slice(c * PERM, (c + 1) * PERM)
            half = slice(c * (PERM // 2), (c + 1) * (PERM // 2))
            w = wgu_stage[:, cols].astype(BF16)
            pw = jnp.dot(w, perm_ref[...], preferred_element_type=F32).astype(BF16)
            wg_scr[:, half] = pw[:, :PERM // 2]
            wu_scr[:, half] = pw[:, PERM // 2:]
            b1 = bias[:, cols].astype(BF16)
            r1 = bias[:, cols] - b1.astype(F32)
            b2 = r1.astype(BF16)
            b3 = (r1 - b2.astype(F32)).astype(BF16)
            terms = jnp.concatenate([b1, b2, b3, jnp.zeros((5, PERM), BF16)], axis=0)
            pb = jnp.sum(jnp.dot(terms, perm_ref[...], preferred_element_type=F32), axis=0, keepdims=True)
            bg_scr[:, half] = pb[:, :PERM // 2]
            bu_scr[:, half] = pb[:, PERM // 2:]
        wd_scr[...] = wd_stage[...].astype(BF16)

        @pl.when(nx_ref[t] >= 0)
        def _():
            for cp in stage_copies(nx_ref[t]):
                cp.start()

    def run(lo, n, e):
        x = _unpack_panels([xs_ref[pb, lo:lo + n, :] for pb in range(PANELS)])
        gate = jnp.dot(x, wg_scr[...], preferred_element_type=F32) + bg_scr[...]
        up = jnp.dot(x, wu_scr[...], preferred_element_type=F32) + bu_scr[...]
        gate = jnp.minimum(gate, SWIGLU_LIMIT)
        up = jnp.clip(up, -SWIGLU_LIMIT, SWIGLU_LIMIT)
        act = ((up + 1.0) * (gate * jax.nn.sigmoid(SWIGLU_ALPHA * gate))).astype(BF16)
        bd = bd_ref[e]
        for pb in range(PANELS):
            cols = slice(pb * PANEL_COLS, (pb + 1) * PANEL_COLS)
            y = jnp.dot(act, wd_scr[:, cols], preferred_element_type=F32) + bd[:, cols]
            ys_ref[pb, lo:lo + n, :] = _pack_panel(y)

    @pl.when(v0 & new0)
    def _():
        load_expert(t0, True)

    @pl.when(same)
    def _():
        run(0, 2 * FFN_ROWS, e0)

    @pl.when(v0 & jnp.logical_not(same))
    def _():
        run(0, FFN_ROWS, e0)

    @pl.when(v1 & jnp.logical_not(same))
    def _():
        load_expert(t1, False)
        run(FFN_ROWS, FFN_ROWS, e1)

    @pl.when(v0 & jnp.logical_not(v1))
    def _():
        ys_ref[:, FFN_ROWS:, :] = xs_ref[:, FFN_ROWS:, :]


def _ffn(te, nv, nx, xs, w_gate_up, bgu, w_down, bd, n_tiles):
    d = D_MODEL
    pair = lambda i, te, nv, nx: (0, jnp.minimum(i, lax.shift_right_logical(nv[0] - 1, 1)), 0)
    whole = lambda i, te, nv, nx: (0, 0, 0)
    r = lax.broadcasted_iota(I32, (PERM, PERM), 0)
    c = lax.broadcasted_iota(I32, (PERM, PERM), 1)
    perm = (r == jnp.where(c < PERM // 2, 2 * c, 2 * (c - PERM // 2) + 1)).astype(BF16)
    return pl.pallas_call(
        _ffn_kernel,
        grid_spec=pltpu.PrefetchScalarGridSpec(
            num_scalar_prefetch=3,
            grid=(n_tiles // 2,),
            in_specs=[pl.BlockSpec((PANELS, 2 * FFN_ROWS, LANES), pair),
                      pl.BlockSpec(memory_space=pl.ANY),
                      pl.BlockSpec((N_EXPERTS, 1, 2 * D_FF), whole),
                      pl.BlockSpec(memory_space=pl.ANY),
                      pl.BlockSpec((N_EXPERTS, 1, d), whole),
                      pl.BlockSpec((PERM, PERM), lambda i, te, nv, nx: (0, 0))],
            out_specs=pl.BlockSpec((PANELS, 2 * FFN_ROWS, LANES), pair),
            scratch_shapes=[pltpu.VMEM((d, 2 * D_FF), F32), pltpu.VMEM((D_FF, d), F32),
                            pltpu.VMEM((d, D_FF), BF16), pltpu.VMEM((d, D_FF), BF16), pltpu.VMEM((D_FF, d), BF16),
                            pltpu.VMEM((1, D_FF), F32), pltpu.VMEM((1, D_FF), F32),
                            pltpu.SemaphoreType.DMA((2,))],
        ),
        out_shape=jax.ShapeDtypeStruct(xs.shape, U32),
        input_output_aliases={3: 0},
        compiler_params=_cparams(1, FFN_VMEM_BYTES),
        name="ffn",
    )(te, nv, nx, xs, w_gate_up, bgu, w_down, bd, perm)


def _combine_kernel(tab_ref, x1_ref, ls_ref, w_ref, mod_ref, ys_ref, o4_hbm, ybuf, ob_buf, sems, osems, *, n_blk):
    jj = pl.program_id(1)
    blk = pl.program_id(0) * pl.num_programs(1) + jj
    slot = blk % 2
    nc = x1_ref.shape[2]
    tt = POST_POS * nc
    d = x1_ref.shape[3]
    local = ybuf.shape[2]

    def chunk_copy(sl, bk, c):
        rows = pl.ds(pl.multiple_of(c * RUN, RUN), RUN)
        return pltpu.make_async_copy(ys_ref.at[:, pl.ds(tab_ref[bk * TABW + c], RUN), :], ybuf.at[sl, :, rows, :],
                                     sems.at[sl])

    def for_chunks(bk, fn):
        _for_chunk_pairs(tab_ref[bk * TABW + TABW - 1], fn)

    @pl.when(blk == 0)
    def _():
        ybuf[...] = jnp.zeros_like(ybuf)
        for_chunks(0, lambda c, p: chunk_copy(0, 0, c).start(priority=p))

    @pl.when(blk + 1 < n_blk)
    def _():
        for_chunks(blk + 1, lambda c, p: chunk_copy(1 - slot, blk + 1, c).start(priority=p))

    for_chunks(blk, lambda c, p: chunk_copy(slot, blk, c).wait())

    to_cols = lambda a: jnp.concatenate([a, jnp.zeros_like(a)], axis=0).T
    ls_c = to_cols(ls_ref[...].astype(F32))
    w_c = to_cols(w_ref[...])
    r = lax.broadcasted_iota(I32, (tt, local), 1).astype(F32)
    wm = jnp.zeros((tt, local), F32)
    for k in range(TOP_K):
        wm = wm + jnp.where(r == ls_c[:, k:k + 1], w_c[:, k:k + 1], 0.0)
    wm = wm.astype(BF16)
    acc = jnp.dot(wm, _unpack_panels([ybuf[slot, pb] for pb in range(PANELS)]), preferred_element_type=F32)
    out = x1_ref[0].reshape(tt, d) + mod_ref[0, 5:6, :] * acc

    def out_copies(sl, b_, j_):
        return [pltpu.make_async_copy(ob_buf.at[sl, il], o4_hbm.at[b_, :, POST_POS * j_ + il, :], osems.at[sl])
                for il in range(POST_POS)]

    @pl.when(blk >= 2)
    def _():
        for cp in out_copies(slot, 0, 0):
            cp.wait()

    for il in range(POST_POS):
        ob_buf[slot, il] = out[il * nc:(il + 1) * nc]
    for cp in out_copies(slot, pl.program_id(0), jj):
        cp.start()

    @pl.when(blk == n_blk - 1)
    def _():
        for cp in out_copies(slot, 0, 0):
            cp.wait()
        if n_blk > 1:
            for cp in out_copies(1 - slot, 0, 0):
                cp.wait()


def _combine(tab, x1, ls, wts, mod, ys):
    b, _, nc, d = x1.shape
    s = SSM_CHUNK * nc
    tt = POST_POS * nc
    nt = SSM_CHUNK // POST_POS
    o4 = pl.pallas_call(
        functools.partial(_combine_kernel, n_blk=b * nt),
        grid_spec=pltpu.PrefetchScalarGridSpec(
            num_scalar_prefetch=1,
            grid=(b, nt),
            in_specs=[pl.BlockSpec((1, POST_POS, nc, d), lambda bi, j, *_: (bi, j, 0, 0)),
                      pl.BlockSpec((TOP_K, tt), lambda bi, j, *_: (0, bi * nt + j)),
                      pl.BlockSpec((TOP_K, tt), lambda bi, j, *_: (0, bi * nt + j)),
                      pl.BlockSpec((1, 6, d), lambda bi, j, *_: (bi, 0, 0)),
                      pl.BlockSpec(memory_space=pl.ANY)],
            out_specs=pl.BlockSpec(memory_space=pl.ANY),
            scratch_shapes=[pltpu.VMEM((2, PANELS, _local_rows(tt), LANES), U32),
                            pltpu.VMEM((2, POST_POS, nc, d), F32),
                            pltpu.SemaphoreType.DMA((2,)), pltpu.SemaphoreType.DMA((2,))],
        ),
        out_shape=jax.ShapeDtypeStruct((b, nc, SSM_CHUNK, d), F32),
        compiler_params=_cparams(2),
        name="combine",
    )(tab, x1, ls, wts, mod, ys)
    return o4.reshape(b, s, d)


def kernel(x, c, w_ada, b_ada, norm_mix, w_in, b_in, q_norm, k_norm, sinks, lam_re, lam_im, log_dt, b_re, b_im,
           c_re, c_im, d_skip, w_glu, b_glu, attn_out_norm, ssm_out_norm, w_out, norm_ffn, w_router, b_router,
           w_gate_up, b_gate_up, w_down, b_down):
    b, s, d = x.shape
    t = b * s
    depth = w_ada.shape[0]
    n_tiles = -(-(t * TOP_K + N_EXPERTS * (RUN - 1 + FFN_ROWS - 1)) // FFN_ROWS)
    n_tiles += n_tiles % 2
    n_alloc = n_tiles + 2
    for l in range(depth):
        mod = _adaln(c, w_ada[l], b_ada[l]).reshape(b, 6, d)
        q, k, v, ut = _inproj(x, mod, norm_mix[l], w_in[l], b_in[l])
        attn = _attention(q, k, v, sinks[l], q_norm[l], k_norm[l], attn_out_norm[l])
        tt, wz, wyt, cs = _ssm_params(lam_re[l], lam_im[l], log_dt[l], b_re[l], b_im[l], c_re[l], c_im[l])
        yt = _ssm(ut, tt, wz, wyt, cs, d_skip[l])
        x1, h2, eidx, wts, lrank, r0, cnt = _post(x, attn, yt, mod, w_glu[l], b_glu[l], ssm_out_norm[l], w_out[l],
                                                  norm_ffn[l], w_router[l], b_router[l])
        ls, tab, te, nv, nx, pad = _route(eidx, lrank, r0, cnt, n_tiles)
        tab = tab.reshape(-1)
        nvt = jnp.stack([nv[0, 0], jnp.int32(n_alloc)])
        xs = _dispatch(tab, pad.reshape(-1), nvt, h2.reshape(t, d), ls, n_alloc * FFN_ROWS)
        ys = _ffn(te[0, :n_tiles], nv[0, :1], nx[0, :n_tiles], xs, w_gate_up[l], b_gate_up[l][:, None, :],
                  w_down[l], b_down[l][:, None, :], n_tiles)
        x = _combine(tab, x1, ls, wts, mod, ys)
    return x
```

```python
import functools
import math

import jax
import jax.numpy as jnp
from jax import lax
from jax.experimental import pallas as pl
from jax.experimental.pallas import tpu as pltpu

F32 = jnp.float32
BF16 = jnp.bfloat16
U32 = jnp.uint32
I32 = jnp.int32

D_MODEL = 1024
HEAD_DIM = 64
N_HEADS = 8
N_KV_HEADS = 2
Q_PER_KV = N_HEADS // N_KV_HEADS
D_ATTN = N_HEADS * HEAD_DIM
D_KV = N_KV_HEADS * HEAD_DIM
D_QKV = D_ATTN + 2 * D_KV
WINDOW = 128
BLOCK = 128
D_SSM = D_MODEL - D_ATTN
SSM_GROUP = 16
N_GROUPS = D_SSM // SSM_GROUP
STATE = 64
N_EXPERTS = 32
TOP_K = 4
D_FF = D_MODEL
SWIGLU_LIMIT = 7.0
SWIGLU_ALPHA = 1.702
EPS = 1e-6
NEG_INF = -1e30

LANES = 128
SSM_CHUNK = 16
SSM_ROW = SSM_CHUNK * SSM_GROUP
N_POW = 2 * SSM_CHUNK
PANEL_COLS = 2 * LANES
PANELS = D_MODEL // PANEL_COLS

SSM_GROUPS_PER_STEP = 4
POS_PER_STEP = 8
ATTN_ROWS = 512
POST_POS = 2
POST_SUB = 4
FFN_ROWS = 256
RUN = 8
RUN_SHIFT = 3
BIG = 2 * RUN
TABW = 128
SMALL0 = 80
PERM = 256
FFN_VMEM_BYTES = 40 * 1024 * 1024
ROW_VMEM_BYTES = 48 * 1024 * 1024

HIGHEST = lax.Precision.HIGHEST
_ARB = "arbitrary"


def _cparams(n, vmem=None):
    return pltpu.CompilerParams(dimension_semantics=(_ARB,) * n, vmem_limit_bytes=vmem)


def _rms(x, axis=-1):
    return x * lax.rsqrt(jnp.mean(x * x, axis=axis, keepdims=True) + EPS)


def _pack_panel(y, exact=False):
    hi, lo = y[:, :LANES], y[:, LANES:]
    if not exact:
        hi = hi.astype(BF16).astype(F32)
        lo = lo.astype(BF16).astype(F32)
    return lax.bitcast_convert_type(hi, U32) | (lax.bitcast_convert_type(lo, U32) >> 16)


def _unpack_panels(words):
    cols = []
    for w in words:
        cols.append(lax.bitcast_convert_type(w & jnp.uint32(0xFFFF0000), F32).astype(BF16))
        cols.append(lax.bitcast_convert_type(w << 16, F32).astype(BF16))
    return jnp.concatenate(cols, axis=-1)


def _prefetch_pos_rows(x4_hbm, buf, sems, n_pos):
    bi = pl.program_id(0)
    j = pl.program_id(1)
    nj = pl.num_programs(1)
    g = bi * nj + j
    slot = g % 2

    def copies(sl, b_, j_):
        return [pltpu.make_async_copy(x4_hbm.at[b_, :, n_pos * j_ + il, :], buf.at[sl, il], sems.at[sl])
                for il in range(n_pos)]

    @pl.when(g == 0)
    def _():
        for cp in copies(0, 0, 0):
            cp.start()

    @pl.when(g + 1 < pl.num_programs(0) * nj)
    def _():
        wrap = j + 1 == nj
        for cp in copies(1 - slot, jnp.where(wrap, bi + 1, bi), jnp.where(wrap, 0, j + 1)):
            cp.start()

    for cp in copies(slot, bi, j):
        cp.wait()
    return slot


def _to_lane_blocks(dst, src):
    for kb in range(dst.shape[0]):
        dst[kb] = src[:, kb * LANES:(kb + 1) * LANES]


def _adaln_kernel(c_ref, w_ref, b_ref, o_ref):
    c = c_ref[...]
    ca = c * jax.nn.sigmoid(c)
    o_ref[...] = jnp.dot(ca, w_ref[...], preferred_element_type=F32, precision=HIGHEST) + b_ref[...]


def _adaln(c, w_ada, b_ada):
    b, d = c.shape
    n = w_ada.shape[1] // d
    return pl.pallas_call(
        _adaln_kernel,
        grid=(n,),
        in_specs=[pl.BlockSpec((b, d), lambda j: (0, 0)),
                  pl.BlockSpec((d, d), lambda j: (0, j)),
                  pl.BlockSpec((1, d), lambda j: (0, j))],
        out_specs=pl.BlockSpec((b, d), lambda j: (0, j)),
        out_shape=jax.ShapeDtypeStruct((b, n * d), F32),
        compiler_params=_cparams(1),
        name="adaln",
    )(c, w_ada, b_ada.reshape(1, -1))


def _inproj_kernel(x4_hbm, x_ref, mod_ref, g_ref, wqkv_ref, bqkv_ref, wut_ref, but_ref, q_ref, k_ref, v_ref, ut_ref,
                   xp_buf, sems):
    nc = ut_ref.shape[3]
    slot = _prefetch_pos_rows(x4_hbm, xp_buf, sems, POS_PER_STEP)
    gain = g_ref[...]
    scale = 1.0 + mod_ref[0, 1:2, :]
    shift = mod_ref[0, 0:1, :]

    def norm_mod(x):
        return (_rms(x) * gain * scale + shift).astype(BF16)

    proj = jnp.dot(norm_mod(x_ref[0]), wqkv_ref[...], preferred_element_type=F32) + bqkv_ref[...]
    q_ref[0] = proj[:, :D_ATTN].astype(BF16)
    k_ref[0] = proj[:, D_ATTN:D_ATTN + D_KV].astype(BF16)
    v_ref[0] = proj[:, D_ATTN + D_KV:].astype(BF16)

    hs = jnp.concatenate([norm_mod(xp_buf[slot, il]) for il in range(POS_PER_STEP)], axis=0)
    ut = lax.dot_general(wut_ref[...], hs, (((1,), (1,)), ((), ())), preferred_element_type=F32) + but_ref[...]
    for il in range(POS_PER_STEP):
        piece = ut[:, il * nc:(il + 1) * nc].astype(BF16)
        ut_ref[0, :, il * SSM_GROUP:(il + 1) * SSM_GROUP, :] = piece.reshape(N_GROUPS, SSM_GROUP, nc)


def _inproj(x, mod, gain, w_in, b_in):
    b, s, d = x.shape
    nc = s // SSM_CHUNK
    rows = POS_PER_STEP * nc
    row = lambda bi, j: (bi, j, 0)
    const = lambda bi, j: (0, 0)
    w_qkv = w_in[:, :D_QKV].astype(BF16)
    w_ut = w_in[:, D_QKV:].T.astype(BF16)
    return pl.pallas_call(
        _inproj_kernel,
        grid=(b, SSM_CHUNK // POS_PER_STEP),
        in_specs=[pl.BlockSpec(memory_space=pl.ANY),
                  pl.BlockSpec((1, rows, d), row),
                  pl.BlockSpec((1, 6, d), lambda bi, j: (bi, 0, 0)),
                  pl.BlockSpec((1, d), const),
                  pl.BlockSpec((d, D_QKV), const),
                  pl.BlockSpec((1, D_QKV), const),
                  pl.BlockSpec((D_SSM, d), const),
                  pl.BlockSpec((D_SSM, 1), const)],
        out_specs=[pl.BlockSpec((1, rows, D_ATTN), row),
                   pl.BlockSpec((1, rows, D_KV), row),
                   pl.BlockSpec((1, rows, D_KV), row),
                   pl.BlockSpec((1, N_GROUPS, POS_PER_STEP * SSM_GROUP, nc), lambda bi, j: (bi, 0, j, 0))],
        out_shape=[jax.ShapeDtypeStruct((b, s, D_ATTN), BF16),
                   jax.ShapeDtypeStruct((b, s, D_KV), BF16),
                   jax.ShapeDtypeStruct((b, s, D_KV), BF16),
                   jax.ShapeDtypeStruct((b, N_GROUPS, SSM_ROW, nc), BF16)],
        scratch_shapes=[pltpu.VMEM((2, POS_PER_STEP, nc, d), F32), pltpu.SemaphoreType.DMA((2,))],
        compiler_params=_cparams(2),
        name="inproj",
    )(x.reshape(b, nc, SSM_CHUNK, d), x, mod, gain.reshape(1, d), w_qkv, b_in[:D_QKV].reshape(1, D_QKV), w_ut, b_in[D_QKV:].reshape(D_SSM, 1))


def _half_norm(x, low):
    sq = x * x
    s_lo = jnp.sum(jnp.where(low, sq, 0.0), axis=-1, keepdims=True)
    s_hi = jnp.sum(sq, axis=-1, keepdims=True) - s_lo
    inv = 1.0 / HEAD_DIM
    scale = jnp.where(low, lax.rsqrt(s_lo * inv + EPS), lax.rsqrt(s_hi * inv + EPS))
    return x * scale


def _attn_block(first, q, k_prev, k_cur, v_prev, v_cur, sinks_ref, qn, low, upper, rblk):
    no_prev = jnp.where(first, NEG_INF, 0.0)
    out_blocks = []
    for hk in range(N_KV_HEADS):
        qs = []
        for j in range(Q_PER_KV // 2):
            blk = hk * (Q_PER_KV // 2) + j
            qb = _half_norm(q[:, blk * LANES:(blk + 1) * LANES], low) * qn * (1.0 / math.sqrt(HEAD_DIM))
            qs.append(jnp.where(low, qb, 0.0))
            qs.append(jnp.where(low, 0.0, qb))
        qg = jnp.concatenate(qs, axis=0).astype(BF16)
        nt = (((1,), (1,)), ((), ()))
        s_prev = lax.dot_general(qg, k_prev[hk], nt, preferred_element_type=F32)
        s_cur = lax.dot_general(qg, k_cur[hk], nt, preferred_element_type=F32)
        s = jnp.where(upper, s_prev + no_prev, s_cur)
        sink = jnp.zeros((Q_PER_KV * BLOCK, 1), F32)
        for g in range(Q_PER_KV):
            sink = jnp.where(rblk == g, sinks_ref[hk * Q_PER_KV + g], sink)
        m = jnp.maximum(jnp.max(s, axis=-1, keepdims=True), sink)
        p = jnp.exp(s - m)
        den = jnp.sum(p, axis=-1, keepdims=True) + jnp.exp(sink - m)
        o = (jnp.dot(jnp.where(upper, p, 0.0).astype(BF16), v_prev[hk], preferred_element_type=F32)
             + jnp.dot(jnp.where(upper, 0.0, p).astype(BF16), v_cur[hk], preferred_element_type=F32)) / den
        for j in range(Q_PER_KV // 2):
            ev = o[(2 * j) * BLOCK:(2 * j + 1) * BLOCK]
            od = o[(2 * j + 1) * BLOCK:(2 * j + 2) * BLOCK]
            out_blocks.append(jnp.where(low, ev, od))
    return jnp.concatenate(out_blocks, axis=-1)


def _attn_kernel(sinks_ref, q_ref, k_ref, v_ref, qn_ref, kn_ref, on_ref, o_hbm, a_buf, sems, *, n_steps):
    step = pl.program_id(1)
    g = pl.program_id(0) * pl.num_programs(1) + step
    slot = g % 2
    cps = ATTN_ROWS // SSM_CHUNK
    nq = ATTN_ROWS // BLOCK

    def out_copies(sl, b_, s_):
        return [pltpu.make_async_copy(a_buf.at[sl, :, i, :], o_hbm.at[b_, i, pl.ds(s_ * cps, cps), :], sems.at[sl])
                for i in range(SSM_CHUNK)]

    @pl.when(g >= 2)
    def _():
        for cp in out_copies(slot, 0, 0):
            cp.wait()

    low = lax.broadcasted_iota(I32, (1, LANES), 1) < HEAD_DIM
    rows = Q_PER_KV * BLOCK
    upper = lax.broadcasted_iota(I32, (rows, BLOCK), 1) > lax.broadcasted_iota(I32, (rows, BLOCK), 0) % BLOCK
    rblk = lax.broadcasted_iota(I32, (rows, 1), 0) // BLOCK

    cur = pl.multiple_of(step * ATTN_ROWS, ATTN_ROWS)
    prev = pl.multiple_of(jnp.maximum(step * nq - 1, 0) * BLOCK, BLOCK)
    kall = jnp.concatenate([k_ref[0, pl.ds(prev, BLOCK), :], k_ref[0, pl.ds(cur, ATTN_ROWS), :]], axis=0).astype(F32)
    vall = jnp.concatenate([v_ref[0, pl.ds(prev, BLOCK), :], v_ref[0, pl.ds(cur, ATTN_ROWS), :]], axis=0).astype(F32)
    kall = _half_norm(kall, low) * kn_ref[...]
    kswap = pltpu.roll(kall, HEAD_DIM, axis=1)
    vswap = pltpu.roll(vall, HEAD_DIM, axis=1)
    k_dup = [jnp.where(low, kall, kswap).astype(BF16), jnp.where(low, kswap, kall).astype(BF16)]
    v_dup = [jnp.where(low, vall, vswap).astype(BF16), jnp.where(low, vswap, vall).astype(BF16)]
    blk = lambda a, i: [a[hk][i * BLOCK:(i + 1) * BLOCK] for hk in range(N_KV_HEADS)]

    for qb in range(nq):
        q = q_ref[0, qb * BLOCK:(qb + 1) * BLOCK, :].astype(F32)
        attn = _attn_block((step == 0) if qb == 0 else False, q, blk(k_dup, qb), blk(k_dup, qb + 1),
                           blk(v_dup, qb), blk(v_dup, qb + 1), sinks_ref, qn_ref[...], low, upper, rblk)
        attn = _rms(attn) * on_ref[...]
        cpb = BLOCK // SSM_CHUNK
        a_buf[slot, qb * cpb:(qb + 1) * cpb] = attn.reshape(cpb, SSM_CHUNK, D_ATTN)

    for cp in out_copies(slot, pl.program_id(0), step):
        cp.start()

    @pl.when(g == n_steps - 1)
    def _():
        for cp in out_copies(slot, 0, 0):
            cp.wait()
        if n_steps > 1:
            for cp in out_copies(1 - slot, 0, 0):
                cp.wait()


def _attention(q, k, v, sinks, q_norm, k_norm, out_norm):
    b, s, _ = q.shape
    tile2 = lambda g: jnp.tile(g.reshape(1, HEAD_DIM), (1, 2))
    cps = ATTN_ROWS // SSM_CHUNK
    return pl.pallas_call(
        functools.partial(_attn_kernel, n_steps=b * (s // ATTN_ROWS)),
        grid=(b, s // ATTN_ROWS),
        in_specs=[pl.BlockSpec(memory_space=pltpu.SMEM),
                  pl.BlockSpec((1, ATTN_ROWS, D_ATTN), lambda bi, n: (bi, n, 0)),
                  pl.BlockSpec((1, s, D_KV), lambda bi, n: (bi, 0, 0)),
                  pl.BlockSpec((1, s, D_KV), lambda bi, n: (bi, 0, 0)),
                  pl.BlockSpec((1, LANES), lambda bi, n: (0, 0)),
                  pl.BlockSpec((1, LANES), lambda bi, n: (0, 0)),
                  pl.BlockSpec((1, D_ATTN), lambda bi, n: (0, 0))],
        out_specs=pl.BlockSpec(memory_space=pl.ANY),
        out_shape=jax.ShapeDtypeStruct((b, SSM_CHUNK, s // SSM_CHUNK, D_ATTN), F32),
        scratch_shapes=[pltpu.VMEM((2, cps, SSM_CHUNK, D_ATTN), F32), pltpu.SemaphoreType.DMA((2,))],
        compiler_params=_cparams(2),
        name="attention",
    )(sinks, q, k, v, tile2(q_norm), tile2(k_norm), out_norm.reshape(1, D_ATTN))


def _cmul(ar, ai, br, bi):
    return ar * br - ai * bi, ar * bi + ai * br


def _ssm_param_kernel(*refs):
    for gi in range(refs[0].shape[0]):
        _ssm_param_group(gi, *refs)


def _ssm_param_group(gi, lam_ref, bre_ref, bim_ref, cre_ref, cim_ref, tt_ref, wz_ref, wyt_ref, cs_ref):
    f32dot = functools.partial(jnp.dot, preferred_element_type=F32, precision=HIGHEST)
    lr = lam_ref[gi, 0:1, :]
    li = lam_ref[gi, 1:2, :]
    dt = jnp.exp(lam_ref[gi, 2:3, :])
    rho = lr * dt
    th = li * dt
    imag_lane = lax.broadcasted_iota(I32, (1, LANES), 1) >= STATE

    kk = (lax.broadcasted_iota(I32, (N_POW, 1), 0) - (SSM_CHUNK - 1)).astype(F32)
    mag = jnp.exp(rho * kk)
    pw_r = mag * jnp.cos(th * kk)
    pw_i = mag * jnp.sin(th * kk)
    lb_r = pw_r[SSM_CHUNK:SSM_CHUNK + 1]
    lb_i = pw_i[SSM_CHUNK:SSM_CHUNK + 1]
    den = lr * lr + li * li
    coef_r = ((lb_r - 1.0) * lr + lb_i * li) / den
    coef_i = (lb_i * lr - (lb_r - 1.0) * li) / den

    eye = (lax.broadcasted_iota(I32, (SSM_GROUP, SSM_GROUP), 0)
           == lax.broadcasted_iota(I32, (SSM_GROUP, SSM_GROUP), 1)).astype(F32)
    lane_fold = (lax.broadcasted_iota(I32, (STATE, LANES), 1) % STATE
                 == lax.broadcasted_iota(I32, (STATE, LANES), 0)).astype(F32)

    def tile_pos(x):
        return jnp.concatenate([x] * SSM_CHUNK, axis=0)

    def power_rows(k_of_pos):
        idx = [k_of_pos(p) + (SSM_CHUNK - 1) for p in range(SSM_CHUNK)]
        rep = lambda t: jnp.concatenate([jnp.broadcast_to(t[r:r + 1], (SSM_GROUP, LANES)) for r in idx], axis=0)
        return rep(pw_r), rep(pw_i)

    def b_rows(b_ref):
        b2 = jnp.concatenate([b_ref[gi], b_ref[gi]], axis=0)
        return tile_pos(lax.dot_general(eye, b2, (((1,), (1,)), ((), ())), preferred_element_type=F32,
                                        precision=HIGHEST))

    def c_rows(c_ref):
        return tile_pos(f32dot(c_ref[gi], lane_fold))

    bbar_r, bbar_i = _cmul(coef_r, coef_i, b_rows(bre_ref), b_rows(bim_ref))
    c_r = c_rows(cre_ref)
    c_i = c_rows(cim_ref)

    a_r, a_i = _cmul(bbar_r, bbar_i, *power_rows(lambda p: -p))
    a2c = jnp.where(imag_lane, -a_i, a_r)
    m_r, m_i = _cmul(c_r, c_i, *power_rows(lambda p: p))
    bmc = jnp.where(imag_lane, m_i, m_r)
    tt = f32dot(bmc, a2c.T)
    causal = (lax.broadcasted_iota(I32, (SSM_ROW, 1), 0) // SSM_GROUP
              >= lax.broadcasted_iota(I32, (1, SSM_ROW), 1) // SSM_GROUP)
    tt_ref[gi] = jnp.where(causal, tt, 0.0).astype(BF16)

    w_r, w_i = _cmul(bbar_r, bbar_i, *power_rows(lambda p: SSM_CHUNK - 1 - p))
    wz_ref[gi, :, :LANES] = jnp.where(imag_lane, w_i, w_r).astype(BF16)
    wz_ref[gi, :, LANES:] = jnp.where(imag_lane, w_r, w_i).astype(BF16)

    y_r, y_i = _cmul(c_r, c_i, *power_rows(lambda p: p + 1))
    wyt_ref[gi] = jnp.where(imag_lane, -y_i, y_r).astype(BF16)

    cs_ref[gi, 0:1, :] = pw_r[N_POW - 1:N_POW]
    cs_ref[gi, 1:2, :] = jnp.where(imag_lane, pw_i[N_POW - 1:N_POW], -pw_i[N_POW - 1:N_POW])


def _ssm_params(lam_re, lam_im, log_dt, b_re, b_im, c_re, c_im):
    g = lam_re.shape[0]
    lam = jnp.stack([lam_re, lam_im, jnp.broadcast_to(log_dt[:, None], (g, STATE))], axis=1)
    lam = jnp.concatenate([lam, lam], axis=2)
    ng = SSM_GROUPS_PER_STEP
    blk = lambda *shape: pl.BlockSpec((ng,) + shape, lambda i: (i, 0, 0))
    return pl.pallas_call(
        _ssm_param_kernel,
        grid=(g // ng,),
        in_specs=[blk(3, LANES), blk(STATE, SSM_GROUP), blk(STATE, SSM_GROUP), blk(SSM_GROUP, STATE),
                  blk(SSM_GROUP, STATE)],
        out_specs=[blk(SSM_ROW, SSM_ROW), blk(SSM_ROW, SSM_ROW), blk(SSM_ROW, LANES), blk(2, LANES)],
        out_shape=[jax.ShapeDtypeStruct((g, SSM_ROW, SSM_ROW), BF16),
                   jax.ShapeDtypeStruct((g, SSM_ROW, SSM_ROW), BF16),
                   jax.ShapeDtypeStruct((g, SSM_ROW, LANES), BF16),
                   jax.ShapeDtypeStruct((g, 2, LANES), F32)],
        compiler_params=_cparams(1),
        name="ssm_params",
    )(lam, b_re, b_im, c_re, c_im)


def _ssm_kernel(ut_ref, tt_ref, wz_ref, wyt_ref, cs_ref, d_ref, yt_ref, z_scr, s_scr):
    batch, ng, _, nc = ut_ref.shape
    uts = [jnp.concatenate([ut_ref[b, gi] for b in range(batch)], axis=1) for gi in range(ng)]
    for gi in range(ng):
        z = lax.dot_general(uts[gi], wz_ref[gi], (((0,), (0,)), ((), ())), preferred_element_type=F32)
        _to_lane_blocks(z_scr.at[gi], z)
    c1 = [cs_ref[gi, 0:1, :] for gi in range(ng)]
    c2 = [cs_ref[gi, 1:2, :] for gi in range(ng)]

    def step(c, carry):
        rows = pl.ds(c, batch, stride=nc)
        out = []
        for gi in range(ng):
            s1, s2 = carry[gi]
            s_scr[gi, rows, :] = s1
            out.append((c1[gi] * s1 + c2[gi] * s2 + z_scr[gi, 0, rows, :],
                        c1[gi] * s2 - c2[gi] * s1 + z_scr[gi, 1, rows, :]))
        return tuple(out)

    zero = jnp.zeros((batch, LANES), F32)
    lax.fori_loop(0, nc, step, ((zero, zero),) * ng, unroll=8)
    for gi in range(ng):
        y = jnp.dot(tt_ref[gi], uts[gi], preferred_element_type=F32)
        y = y + lax.dot_general(wyt_ref[gi], s_scr[gi].astype(BF16), (((1,), (1,)), ((), ())),
                                preferred_element_type=F32)
        y = y + d_ref[gi] * uts[gi].astype(F32)
        for b in range(batch):
            yt_ref[b, gi] = y[:, b * nc:(b + 1) * nc]


def _ssm(ut, tt, wz, wyt, cs, d_skip):
    b, g, _, nc = ut.shape
    ng = SSM_GROUPS_PER_STEP
    d_col = jnp.tile(d_skip.reshape(g, 1, SSM_GROUP), (1, SSM_CHUNK, 1)).reshape(g, SSM_ROW, 1)
    blk = lambda *shape: pl.BlockSpec((ng,) + shape, lambda i: (i, 0, 0))
    act = pl.BlockSpec((b, ng, SSM_ROW, nc), lambda i: (0, i, 0, 0))
    return pl.pallas_call(
        _ssm_kernel,
        grid=(g // ng,),
        in_specs=[act, blk(SSM_ROW, SSM_ROW), blk(SSM_ROW, SSM_ROW), blk(SSM_ROW, LANES), blk(2, LANES),
                  blk(SSM_ROW, 1)],
        out_specs=act,
        out_shape=jax.ShapeDtypeStruct((b, g, SSM_ROW, nc), F32),
        scratch_shapes=[pltpu.VMEM((ng, SSM_ROW // LANES, b * nc, LANES), F32), pltpu.VMEM((ng, b * nc, LANES), F32)],
        compiler_params=_cparams(1),
        name="ssm",
    )(ut, tt, wz, wyt, cs, d_col)


def _post_kernel(x4_hbm, attn_ref, yt_ref, mod_ref, wglut_ref, bglu_ref, sn_ref, wout_ref, nf_ref, wr_ref, br_ref,
                 tri_ref, x1_ref, h2_ref, eidx_ref, wts_ref, lrank_ref, r0_ref, cnt_ref, carry_ref, xp_buf, sems):
    @pl.when((pl.program_id(0) == 0) & (pl.program_id(1) == 0))
    def _():
        carry_ref[...] = jnp.zeros_like(carry_ref)

    slot = _prefetch_pos_rows(x4_hbm, xp_buf, sems, POST_SUB * POST_POS)
    nc = attn_ref.shape[2]
    ts = POST_POS * nc
    d = x1_ref.shape[3]
    iota_e = lax.broadcasted_iota(I32, (N_EXPERTS, ts), 0).astype(F32)
    counts = []
    for sub in range(POST_SUB):
        pos = range(sub * POST_POS, (sub + 1) * POST_POS)
        lanes = slice(sub * ts, (sub + 1) * ts)
        yt = jnp.concatenate(
            [yt_ref[0, :, il * SSM_GROUP:(il + 1) * SSM_GROUP, :].reshape(D_SSM, nc) for il in pos], axis=1)
        g = jax.nn.gelu(yt)
        gate = jax.nn.sigmoid(jnp.dot(wglut_ref[...], g.astype(BF16), preferred_element_type=F32) + bglu_ref[...])
        ssm_t = _rms(g * gate, axis=0) * sn_ref[...]
        attn = attn_ref[0, sub * POST_POS:(sub + 1) * POST_POS].reshape(ts, D_ATTN)
        mixed = jnp.concatenate([attn.astype(BF16), ssm_t.T.astype(BF16)], axis=-1)
        o = jnp.dot(mixed, wout_ref[...], preferred_element_type=F32)
        x = jnp.concatenate([xp_buf[slot, il] for il in pos], axis=0)
        x1 = x + mod_ref[0, 2:3, :] * o
        x1_ref[0, sub * POST_POS:(sub + 1) * POST_POS] = x1.reshape(POST_POS, nc, d)
        h2 = _rms(x1) * nf_ref[...] * (1.0 + mod_ref[0, 4:5, :]) + mod_ref[0, 3:4, :]
        h2_ref[0, sub * POST_POS:(sub + 1) * POST_POS] = h2.astype(BF16).reshape(POST_POS, nc, d)

        logits = lax.dot_general(wr_ref[...], h2.astype(BF16), (((1,), (1,)), ((), ())),
                                 preferred_element_type=F32) + br_ref[...]
        l = logits
        idxs, vals = [], []
        for _ in range(TOP_K):
            m = jnp.max(l, axis=0, keepdims=True)
            idx = jnp.min(jnp.where(l == m, iota_e, float(N_EXPERTS)), axis=0, keepdims=True)
            idxs.append(idx)
            vals.append(m)
            l = jnp.where(iota_e == idx, -jnp.inf, l)
        es = [jnp.exp(v - vals[0]) for v in vals]
        tot = es[0] + es[1] + es[2] + es[3]
        member = jnp.zeros((N_EXPERTS, ts), F32)
        for idx in idxs:
            member = member + (iota_e == idx).astype(F32)
        before = jnp.dot(member.astype(BF16), tri_ref[...], preferred_element_type=F32)
        for k in range(TOP_K):
            eidx_ref[k:k + 1, lanes] = idxs[k].astype(I32)
            wts_ref[k:k + 1, lanes] = es[k] / tot
            lrank_ref[k:k + 1, lanes] = jnp.sum(jnp.where(iota_e == idxs[k], before, 0.0), axis=0,
                                                keepdims=True).astype(I32)
        counts.append(jnp.sum(member, axis=1, keepdims=True))

    carry = carry_ref[...]
    for sub in range(POST_SUB):
        r0_ref[sub] = carry.astype(I32)
        carry = carry + counts[sub]
    carry_ref[...] = carry
    cnt_ref[...] = carry.astype(I32)


def _post(x, attn, yt, mod, w_glu, b_glu, ssm_norm, w_out, norm_ffn, w_router, b_router):
    b, s, d = x.shape
    nc = s // SSM_CHUNK
    ts = POST_POS * nc
    npos = POST_SUB * POST_POS
    nt = SSM_CHUNK // npos
    t = b * s
    pm = lambda bi, j: (bi, j, 0, 0)
    const = lambda bi, j: (0, 0)
    tok = lambda bi, j: (0, bi * nt + j)
    tri = (lax.broadcasted_iota(I32, (ts, ts), 0) < lax.broadcasted_iota(I32, (ts, ts), 1)).astype(BF16)
    col = lambda a: a.reshape(-1, 1)
    return pl.pallas_call(
        _post_kernel,
        grid=(b, nt),
        in_specs=[pl.BlockSpec(memory_space=pl.ANY),
                  pl.BlockSpec((1, npos, nc, D_ATTN), pm),
                  pl.BlockSpec((1, N_GROUPS, npos * SSM_GROUP, nc), lambda bi, j: (bi, 0, j, 0)),
                  pl.BlockSpec((1, 6, d), lambda bi, j: (bi, 0, 0)),
                  pl.BlockSpec((D_SSM, D_SSM), const),
                  pl.BlockSpec((D_SSM, 1), const),
                  pl.BlockSpec((D_SSM, 1), const),
                  pl.BlockSpec((d, d), const),
                  pl.BlockSpec((1, d), const),
                  pl.BlockSpec((N_EXPERTS, d), const),
                  pl.BlockSpec((N_EXPERTS, 1), const),
                  pl.BlockSpec((ts, ts), const)],
        out_specs=[pl.BlockSpec((1, npos, nc, d), pm),
                   pl.BlockSpec((1, npos, nc, d), pm),
                   pl.BlockSpec((TOP_K, POST_SUB * ts), tok),
                   pl.BlockSpec((TOP_K, POST_SUB * ts), tok),
                   pl.BlockSpec((TOP_K, POST_SUB * ts), tok),
                   pl.BlockSpec((POST_SUB, N_EXPERTS, 1), lambda bi, j: (bi * nt + j, 0, 0)),
                   pl.BlockSpec((N_EXPERTS, 1), const)],
        out_shape=[jax.ShapeDtypeStruct((b, SSM_CHUNK, nc, d), F32),
                   jax.ShapeDtypeStruct((b, SSM_CHUNK, nc, d), BF16),
                   jax.ShapeDtypeStruct((TOP_K, t), I32),
                   jax.ShapeDtypeStruct((TOP_K, t), F32),
                   jax.ShapeDtypeStruct((TOP_K, t), I32),
                   jax.ShapeDtypeStruct((b * nt * POST_SUB, N_EXPERTS, 1), I32),
                   jax.ShapeDtypeStruct((N_EXPERTS, 1), I32)],
        scratch_shapes=[pltpu.VMEM((N_EXPERTS, 1), F32), pltpu.VMEM((2, npos, nc, d), F32),
                        pltpu.SemaphoreType.DMA((2,))],
        compiler_params=_cparams(2),
        name="post",
    )(x.reshape(b, nc, SSM_CHUNK, d), attn, yt, mod, w_glu.T.astype(BF16), col(b_glu), col(ssm_norm), w_out.astype(BF16),
      norm_ffn.reshape(1, -1), w_router.T.astype(BF16), col(b_router), tri)


def _route_kernel(eidx_ref, lrank_ref, r0_ref, cnt_ref, ls_ref, tab_ref, te_ref, nv_ref, nx_ref, pad_ref):
    cnt = cnt_ref[...]
    tiles = (cnt + (RUN - 1 + FFN_ROWS - 1)) // FFN_ROWS
    er = lax.broadcasted_iota(I32, (N_EXPERTS, N_EXPERTS), 0)
    ec = lax.broadcasted_iota(I32, (N_EXPERTS, N_EXPERTS), 1)
    ltri = (ec < er).astype(BF16)

    def excl_cumsum(v):
        vb = jnp.broadcast_to(v.astype(F32), (N_EXPERTS, LANES)).astype(BF16)
        return jnp.dot(ltri, vb, preferred_element_type=F32)[:, 0:1].astype(I32)

    start_t = excl_cumsum(tiles)
    end_t = start_t + tiles
    start = start_t * FFN_ROWS
    pad_ref[...] = start + cnt

    nb = r0_ref.shape[0]
    ts = eidx_ref.shape[1] // nb
    iota_e = lax.broadcasted_iota(I32, (N_EXPERTS, ts), 0)
    iota_t = lax.broadcasted_iota(I32, (N_EXPERTS, TABW), 0)
    lane = lax.broadcasted_iota(I32, (1, TABW), 1)

    def block(b, carry):
        lanes = pl.ds(pl.multiple_of(b * ts, ts), ts)
        sels = [iota_e == eidx_ref[k:k + 1, lanes] for k in range(TOP_K)]
        member = sels[0].astype(I32) + sels[1].astype(I32) + sels[2].astype(I32) + sels[3].astype(I32)
        units = lax.shift_right_logical(jnp.sum(member, axis=1, keepdims=True) + (RUN - 1), RUN_SHIFT)
        u0 = excl_cumsum(units)
        for k in range(TOP_K):
            first = jnp.sum(jnp.where(sels[k], u0, 0), axis=0, keepdims=True)
            ls_ref[k:k + 1, lanes] = first * RUN + lrank_ref[k:k + 1, lanes]
        n_big = lax.shift_right_logical(units, 1)
        n_small = units & 1
        slot0 = start + r0_ref[b]

        def chunk_rows(idx, counts):
            c0 = excl_cumsum(counts)
            sel = iota_t == jnp.sum((idx >= c0 + counts).astype(I32), axis=0, keepdims=True)
            pick = lambda v: jnp.sum(jnp.where(sel, v, 0), axis=0, keepdims=True)
            j = idx - pick(c0)
            return pick(slot0), pick(u0), j, pick(n_big), idx < jnp.max(c0 + counts, axis=0, keepdims=True)

        s_b, u_b, j_b, _, ok_b = chunk_rows(lane, n_big)
        s_s, u_s, _, nb_s, ok_s = chunk_rows(lane - SMALL0, n_small)
        small = lane >= SMALL0
        slot = jnp.where(small, s_s + nb_s * BIG, s_b + j_b * BIG)
        local = jnp.where(small, (u_s + 2 * nb_s) * RUN, (u_b + 2 * j_b) * RUN)
        ok = (small & ok_s) | (jnp.logical_not(small) & ok_b)
        counts = jnp.where(lane == TABW - 2, jnp.sum(n_big, axis=0, keepdims=True),
                           jnp.sum(n_small, axis=0, keepdims=True))
        tab_ref[b, 0:1, :] = jnp.where(lane >= TABW - 2, counts, jnp.where(ok, slot, -1))
        tab_ref[b, 1:2, :] = jnp.where(ok, local, 0)
        return carry

    lax.fori_loop(0, nb, block, 0, unroll=2)

    nv = jnp.max(end_t, axis=0, keepdims=True)
    width = te_ref.shape[1]
    ti = jnp.minimum(lax.broadcasted_iota(I32, (N_EXPERTS, width), 1), nv - 1)
    te = jnp.minimum(jnp.sum((ti >= end_t).astype(I32), axis=0, keepdims=True), N_EXPERTS - 1)
    te_ref[...] = te
    nv_ref[...] = jnp.broadcast_to(nv, nv_ref.shape)
    ie = lax.broadcasted_iota(I32, (N_EXPERTS, width), 0)
    own_end = jnp.sum(jnp.where(ie == te, end_t, 0), axis=0, keepdims=True)
    nxt = jnp.minimum(jnp.sum((own_end >= end_t).astype(I32), axis=0, keepdims=True), N_EXPERTS - 1)
    nx_ref[...] = jnp.where(own_end < nv, nxt, -1)


def _route(eidx, lrank, r0, cnt, n_tiles):
    t = eidx.shape[1]
    nb = r0.shape[0]
    width = -(-n_tiles // LANES) * LANES
    return pl.pallas_call(
        _route_kernel,
        out_shape=[jax.ShapeDtypeStruct((TOP_K, t), I32),
                   jax.ShapeDtypeStruct((nb, 2, TABW), I32),
                   jax.ShapeDtypeStruct((1, width), I32),
                   jax.ShapeDtypeStruct((1, LANES), I32),
                   jax.ShapeDtypeStruct((1, width), I32),
                   jax.ShapeDtypeStruct((N_EXPERTS, 1), I32)],
        name="route",
    )(eidx, lrank, r0, cnt)


def _for_chunk_pairs(n, fn):
    def body(i, carry):
        fn(2 * i, 0)

        @pl.when(2 * i + 1 < n)
        def _():
            fn(2 * i + 1, 1)
        return carry
    lax.fori_loop(0, lax.shift_right_logical(n + 1, 1), body, 0)


def _for_block_chunks(tab_ref, blk, fn):
    base = blk * (2 * TABW)
    for first, count_lane, n_rows in ((0, TABW - 2, BIG), (SMALL0, TABW - 1, RUN)):
        def visit(c, parity, first=first, n_rows=n_rows):
            fn(tab_ref[base + first + c], pl.multiple_of(tab_ref[base + TABW + first + c], RUN), n_rows, parity)
        _for_chunk_pairs(tab_ref[base + count_lane], visit)


def _local_rows(ts):
    return ts * TOP_K + N_EXPERTS * RUN


def _dispatch_kernel(tab_ref, pad_ref, nvt_ref, h_ref, ls_ref, xs_ref, buf, zbuf, sems, zsem):
    b = pl.program_id(0)
    slot = b % 2
    ts = h_ref.shape[0]
    local = buf.shape[2]

    def chunk_copy(sl, slot_row, local_row, n):
        return pltpu.make_async_copy(buf.at[sl, :, pl.ds(local_row, n), :], xs_ref.at[:, pl.ds(slot_row, n), :],
                                     sems.at[sl])

    @pl.when(b == 0)
    def _():
        zbuf[...] = jnp.zeros_like(zbuf)
        zrows = zbuf.shape[1]
        zero = lambda row: pltpu.make_async_copy(zbuf, xs_ref.at[:, pl.ds(row, zrows), :], zsem)
        for phase in range(3):
            for e in range(phase, N_EXPERTS, 3):
                zero(pad_ref[e]).start()
            for e in range(phase, N_EXPERTS, 3):
                zero(pad_ref[e]).wait()
        ztile = lambda i: pltpu.make_async_copy(zbuf.at[:, pl.ds(0, FFN_ROWS), :],
                                                xs_ref.at[:, pl.ds((nvt_ref[0] + i) * FFN_ROWS, FFN_ROWS), :], zsem)

        def tail_start(i, carry):
            ztile(i).start()
            return carry

        def tail_wait(i, carry):
            ztile(i).wait()
            return carry
        lax.fori_loop(0, nvt_ref[1] - nvt_ref[0], tail_start, 0)
        lax.fori_loop(0, nvt_ref[1] - nvt_ref[0], tail_wait, 0)

    r = lax.broadcasted_iota(I32, (local, ts), 0)
    hit = (r == ls_ref[0:1, :]) | (r == ls_ref[1:2, :]) | (r == ls_ref[2:3, :]) | (r == ls_ref[3:4, :])
    hit = hit.astype(BF16)
    for pb in range(PANELS):
        srt = jnp.dot(hit, h_ref[:, pb * PANEL_COLS:(pb + 1) * PANEL_COLS], preferred_element_type=F32)
        buf[slot, pb] = _pack_panel(srt, exact=True)

    @pl.when(b > 0)
    def _():
        _for_block_chunks(tab_ref, b - 1, lambda s, l, n, p: chunk_copy(1 - slot, s, l, n).wait())

    _for_block_chunks(tab_ref, b, lambda s, l, n, p: chunk_copy(slot, s, l, n).start(priority=p))

    @pl.when(b == pl.num_programs(0) - 1)
    def _():
        _for_block_chunks(tab_ref, b, lambda s, l, n, p: chunk_copy(slot, s, l, n).wait())


def _dispatch(tab, pad, nvt, h2, ls, n_rows):
    t, d = h2.shape
    nb = tab.shape[0] // (2 * TABW)
    ts = t // nb
    return pl.pallas_call(
        _dispatch_kernel,
        grid_spec=pltpu.PrefetchScalarGridSpec(
            num_scalar_prefetch=3,
            grid=(nb,),
            in_specs=[pl.BlockSpec((ts, d), lambda i, *_: (i, 0)),
                      pl.BlockSpec((TOP_K, ts), lambda i, *_: (0, i))],
            out_specs=pl.BlockSpec(memory_space=pl.ANY),
            scratch_shapes=[pltpu.VMEM((2, PANELS, _local_rows(ts), LANES), U32),
                            pltpu.VMEM((PANELS, FFN_ROWS + RUN, LANES), U32),
                            pltpu.SemaphoreType.DMA((2,)), pltpu.SemaphoreType.DMA],
        ),
        out_shape=jax.ShapeDtypeStruct((PANELS, n_rows, LANES), U32),
        compiler_params=_cparams(1, ROW_VMEM_BYTES),
        name="dispatch",
    )(tab, pad, nvt, h2, ls)


def _ffn_kernel(te_ref, nv_ref, nx_ref, xs_ref, wgu_hbm, bgu_ref, wd_hbm, bd_ref, perm_ref, ys_ref,
                wgu_stage, wd_stage, wg_scr, wu_scr, wd_scr, bg_scr, bu_scr, sems):
    p = pl.program_id(0)
    t0 = 2 * p
    t1 = t0 + 1
    e0 = te_ref[t0]
    e1 = te_ref[t1]
    v0 = t0 < nv_ref[0]
    v1 = t1 < nv_ref[0]
    new0 = (p == 0) | (e0 != te_ref[jnp.maximum(t0 - 1, 0)])
    same = v1 & (e1 == e0)

    def stage_copies(e):
        return (pltpu.make_async_copy(wgu_hbm.at[e], wgu_stage, sems.at[0]),
                pltpu.make_async_copy(wd_hbm.at[e], wd_stage, sems.at[1]))

    def load_expert(t, first):
        e = te_ref[t]
        if first:
            @pl.when(p == 0)
            def _():
                for cp in stage_copies(e):
                    cp.start()

        for cp in stage_copies(e):
            cp.wait()
        bias = bgu_ref[e]
        for c in range(2 * D_FF // PERM):
            cols = slice(c * PERM, (c + 1) * PERM)
            half = slice(c * (PERM // 2), (c + 1) * (PERM // 2))
            w = wgu_stage[:, cols].astype(BF16)
            pw = jnp.dot(w, perm_ref[...], preferred_element_type=F32).astype(BF16)
            wg_scr[:, half] = pw[:, :PERM // 2]
            wu_scr[:, half] = pw[:, PERM // 2:]
            b1 = bias[:, cols].astype(BF16)
            r1 = bias[:, cols] - b1.astype(F32)
            b2 = r1.astype(BF16)
            b3 = (r1 - b2.astype(F32)).astype(BF16)
            terms = jnp.concatenate([b1, b2, b3, jnp.zeros((5, PERM), BF16)], axis=0)
            pb = jnp.sum(jnp.dot(terms, perm_ref[...], preferred_element_type=F32), axis=0, keepdims=True)
            bg_scr[:, half] = pb[:, :PERM // 2]
            bu_scr[:, half] = pb[:, PERM // 2:]
        wd_scr[...] = wd_stage[...].astype(BF16)

        @pl.when(nx_ref[t] >= 0)
        def _():
            for cp in stage_copies(nx_ref[t]):
                cp.start()

    def run(lo, n, e):
        x = _unpack_panels([xs_ref[pb, lo:lo + n, :] for pb in range(PANELS)])
        gate = jnp.dot(x, wg_scr[...], preferred_element_type=F32) + bg_scr[...]
        up = jnp.dot(x, wu_scr[...], preferred_element_type=F32) + bu_scr[...]
        gate = jnp.minimum(gate, SWIGLU_LIMIT)
        up = jnp.clip(up, -SWIGLU_LIMIT, SWIGLU_LIMIT)
        act = ((up + 1.0) * (gate * jax.nn.sigmoid(SWIGLU_ALPHA * gate))).astype(BF16)
        bd = bd_ref[e]
        for pb in range(PANELS):
            cols = slice(pb * PANEL_COLS, (pb + 1) * PANEL_COLS)
            y = jnp.dot(act, wd_scr[:, cols], preferred_element_type=F32) + bd[:, cols]
            ys_ref[pb, lo:lo + n, :] = _pack_panel(y)

    @pl.when(v0 & new0)
    def _():
        load_expert(t0, True)

    @pl.when(same)
    def _():
        run(0, 2 * FFN_ROWS, e0)

    @pl.when(v0 & jnp.logical_not(same))
    def _():
        run(0, FFN_ROWS, e0)

    @pl.when(v1 & jnp.logical_not(same))
    def _():
        load_expert(t1, False)
        run(FFN_ROWS, FFN_ROWS, e1)

    @pl.when(v0 & jnp.logical_not(v1))
    def _():
        ys_ref[:, FFN_ROWS:, :] = xs_ref[:, FFN_ROWS:, :]


def _ffn(te, nv, nx, xs, w_gate_up, bgu, w_down, bd, n_tiles):
    d = D_MODEL
    pair = lambda i, te, nv, nx: (0, jnp.minimum(i, lax.shift_right_logical(nv[0] - 1, 1)), 0)
    whole = lambda i, te, nv, nx: (0, 0, 0)
    r = lax.broadcasted_iota(I32, (PERM, PERM), 0)
    c = lax.broadcasted_iota(I32, (PERM, PERM), 1)
    perm = (r == jnp.where(c < PERM // 2, 2 * c, 2 * (c - PERM // 2) + 1)).astype(BF16)
    return pl.pallas_call(
        _ffn_kernel,
        grid_spec=pltpu.PrefetchScalarGridSpec(
            num_scalar_prefetch=3,
            grid=(n_tiles // 2,),
            in_specs=[pl.BlockSpec((PANELS, 2 * FFN_ROWS, LANES), pair),
                      pl.BlockSpec(memory_space=pl.ANY),
                      pl.BlockSpec((N_EXPERTS, 1, 2 * D_FF), whole),
                      pl.BlockSpec(memory_space=pl.ANY),
                      pl.BlockSpec((N_EXPERTS, 1, d), whole),
                      pl.BlockSpec((PERM, PERM), lambda i, te, nv, nx: (0, 0))],
            out_specs=pl.BlockSpec((PANELS, 2 * FFN_ROWS, LANES), pair),
            scratch_shapes=[pltpu.VMEM((d, 2 * D_FF), F32), pltpu.VMEM((D_FF, d), F32),
                            pltpu.VMEM((d, D_FF), BF16), pltpu.VMEM((d, D_FF), BF16), pltpu.VMEM((D_FF, d), BF16),
                            pltpu.VMEM((1, D_FF), F32), pltpu.VMEM((1, D_FF), F32),
                            pltpu.SemaphoreType.DMA((2,))],
        ),
        out_shape=jax.ShapeDtypeStruct(xs.shape, U32),
        input_output_aliases={3: 0},
        compiler_params=_cparams(1, FFN_VMEM_BYTES),
        name="ffn",
    )(te, nv, nx, xs, w_gate_up, bgu, w_down, bd, perm)


def _combine_kernel(tab_ref, x1_ref, ls_ref, w_ref, mod_ref, ys_ref, o4_hbm, ybuf, ob_buf, sems, osems, *, n_blk):
    jj = pl.program_id(1)
    blk = pl.program_id(0) * pl.num_programs(1) + jj
    slot = blk % 2
    nc = x1_ref.shape[2]
    tt = POST_POS * nc
    d = x1_ref.shape[3]
    local = ybuf.shape[2]

    def chunk_copy(sl, slot_row, local_row, n):
        return pltpu.make_async_copy(ys_ref.at[:, pl.ds(slot_row, n), :], ybuf.at[sl, :, pl.ds(local_row, n), :],
                                     sems.at[sl])

    @pl.when(blk == 0)
    def _():
        ybuf[...] = jnp.zeros_like(ybuf)
        _for_block_chunks(tab_ref, 0, lambda s, l, n, p: chunk_copy(0, s, l, n).start(priority=p))

    @pl.when(blk + 1 < n_blk)
    def _():
        _for_block_chunks(tab_ref, blk + 1, lambda s, l, n, p: chunk_copy(1 - slot, s, l, n).start(priority=p))

    _for_block_chunks(tab_ref, blk, lambda s, l, n, p: chunk_copy(slot, s, l, n).wait())

    to_cols = lambda a: jnp.concatenate([a, jnp.zeros_like(a)], axis=0).T
    ls_c = to_cols(ls_ref[...].astype(F32))
    w_c = to_cols(w_ref[...])
    r = lax.broadcasted_iota(I32, (tt, local), 1).astype(F32)
    wm = jnp.zeros((tt, local), F32)
    for k in range(TOP_K):
        wm = wm + jnp.where(r == ls_c[:, k:k + 1], w_c[:, k:k + 1], 0.0)
    wm = wm.astype(BF16)
    acc = jnp.dot(wm, _unpack_panels([ybuf[slot, pb] for pb in range(PANELS)]), preferred_element_type=F32)
    out = x1_ref[0].reshape(tt, d) + mod_ref[0, 5:6, :] * acc

    def out_copies(sl, b_, j_):
        return [pltpu.make_async_copy(ob_buf.at[sl, il], o4_hbm.at[b_, :, POST_POS * j_ + il, :], osems.at[sl])
                for il in range(POST_POS)]

    @pl.when(blk >= 2)
    def _():
        for cp in out_copies(slot, 0, 0):
            cp.wait()

    for il in range(POST_POS):
        ob_buf[slot, il] = out[il * nc:(il + 1) * nc]
    for cp in out_copies(slot, pl.program_id(0), jj):
        cp.start()

    @pl.when(blk == n_blk - 1)
    def _():
        for cp in out_copies(slot, 0, 0):
            cp.wait()
        if n_blk > 1:
            for cp in out_copies(1 - slot, 0, 0):
                cp.wait()


def _combine(tab, x1, ls, wts, mod, ys):
    b, _, nc, d = x1.shape
    s = SSM_CHUNK * nc
    tt = POST_POS * nc
    nt = SSM_CHUNK // POST_POS
    o4 = pl.pallas_call(
        functools.partial(_combine_kernel, n_blk=b * nt),
        grid_spec=pltpu.PrefetchScalarGridSpec(
            num_scalar_prefetch=1,
            grid=(b, nt),
            in_specs=[pl.BlockSpec((1, POST_POS, nc, d), lambda bi, j, *_: (bi, j, 0, 0)),
                      pl.BlockSpec((TOP_K, tt), lambda bi, j, *_: (0, bi * nt + j)),
                      pl.BlockSpec((TOP_K, tt), lambda bi, j, *_: (0, bi * nt + j)),
                      pl.BlockSpec((1, 6, d), lambda bi, j, *_: (bi, 0, 0)),
                      pl.BlockSpec(memory_space=pl.ANY)],
            out_specs=pl.BlockSpec(memory_space=pl.ANY),
            scratch_shapes=[pltpu.VMEM((2, PANELS, _local_rows(tt), LANES), U32),
                            pltpu.VMEM((2, POST_POS, nc, d), F32),
                            pltpu.SemaphoreType.DMA((2,)), pltpu.SemaphoreType.DMA((2,))],
        ),
        out_shape=jax.ShapeDtypeStruct((b, nc, SSM_CHUNK, d), F32),
        compiler_params=_cparams(2),
        name="combine",
    )(tab, x1, ls, wts, mod, ys)
    return o4.reshape(b, s, d)


def kernel(x, c, w_ada, b_ada, norm_mix, w_in, b_in, q_norm, k_norm, sinks, lam_re, lam_im, log_dt, b_re, b_im,
           c_re, c_im, d_skip, w_glu, b_glu, attn_out_norm, ssm_out_norm, w_out, norm_ffn, w_router, b_router,
           w_gate_up, b_gate_up, w_down, b_down):
    b, s, d = x.shape
    t = b * s
    depth = w_ada.shape[0]
    n_tiles = -(-(t * TOP_K + N_EXPERTS * (RUN - 1 + FFN_ROWS - 1)) // FFN_ROWS)
    n_tiles += n_tiles % 2
    n_alloc = n_tiles + 2
    for l in range(depth):
        mod = _adaln(c, w_ada[l], b_ada[l]).reshape(b, 6, d)
        q, k, v, ut = _inproj(x, mod, norm_mix[l], w_in[l], b_in[l])
        attn = _attention(q, k, v, sinks[l], q_norm[l], k_norm[l], attn_out_norm[l])
        tt, wz, wyt, cs = _ssm_params(lam_re[l], lam_im[l], log_dt[l], b_re[l], b_im[l], c_re[l], c_im[l])
        yt = _ssm(ut, tt, wz, wyt, cs, d_skip[l])
        x1, h2, eidx, wts, lrank, r0, cnt = _post(x, attn, yt, mod, w_glu[l], b_glu[l], ssm_out_norm[l], w_out[l],
                                                  norm_ffn[l], w_router[l], b_router[l])
        ls, tab, te, nv, nx, pad = _route(eidx, lrank, r0, cnt, n_tiles)
        tab = tab.reshape(-1)
        nvt = jnp.stack([nv[0, 0], jnp.int32(n_alloc)])
        xs = _dispatch(tab, pad.reshape(-1), nvt, h2.reshape(t, d), ls, n_alloc * FFN_ROWS)
        ys = _ffn(te[0, :n_tiles], nv[0, :1], nx[0, :n_tiles], xs, w_gate_up[l], b_gate_up[l][:, None, :],
                  w_down[l], b_down[l][:, None, :], n_tiles)
        x = _combine(tab, x1, ls, wts, mod, ys)
    return x
```

```python
import functools
import math

import jax
import jax.numpy as jnp
from jax import lax
from jax.experimental import pallas as pl
from jax.experimental.pallas import tpu as pltpu

F32 = jnp.float32
BF16 = jnp.bfloat16
U32 = jnp.uint32
I32 = jnp.int32

D_MODEL = 1024
HEAD_DIM = 64
N_HEADS = 8
N_KV_HEADS = 2
Q_PER_KV = N_HEADS // N_KV_HEADS
D_ATTN = N_HEADS * HEAD_DIM
D_KV = N_KV_HEADS * HEAD_DIM
D_QKV = D_ATTN + 2 * D_KV
WINDOW = 128
BLOCK = 128
D_SSM = D_MODEL - D_ATTN
SSM_GROUP = 16
N_GROUPS = D_SSM // SSM_GROUP
STATE = 64
N_EXPERTS = 32
TOP_K = 4
D_FF = D_MODEL
SWIGLU_LIMIT = 7.0
SWIGLU_ALPHA = 1.702
EPS = 1e-6
NEG_INF = -1e30

LANES = 128
SSM_CHUNK = 16
SSM_ROW = SSM_CHUNK * SSM_GROUP
N_POW = 2 * SSM_CHUNK
PANEL_COLS = 2 * LANES
PANELS = D_MODEL // PANEL_COLS

SSM_GROUPS_PER_STEP = 4
POS_PER_STEP = 8
ATTN_ROWS = 512
POST_POS = 2
POST_SUB = 4
FFN_ROWS = 256
RUN = 8
RUN_SHIFT = 3
BIG = 2 * RUN
TABW = 128
SMALL0 = 80
PERM = 256
FFN_VMEM_BYTES = 40 * 1024 * 1024
ROW_VMEM_BYTES = 48 * 1024 * 1024

HIGHEST = lax.Precision.HIGHEST
_ARB = "arbitrary"


def _cparams(n, vmem=None):
    return pltpu.CompilerParams(dimension_semantics=(_ARB,) * n, vmem_limit_bytes=vmem)


def _rms(x, axis=-1):
    return x * lax.rsqrt(jnp.mean(x * x, axis=axis, keepdims=True) + EPS)


def _pack_panel(y, exact=False):
    hi, lo = y[:, :LANES], y[:, LANES:]
    if not exact:
        hi = hi.astype(BF16).astype(F32)
        lo = lo.astype(BF16).astype(F32)
    return lax.bitcast_convert_type(hi, U32) | (lax.bitcast_convert_type(lo, U32) >> 16)


def _unpack_panels(words):
    cols = []
    for w in words:
        cols.append(lax.bitcast_convert_type(w & jnp.uint32(0xFFFF0000), F32).astype(BF16))
        cols.append(lax.bitcast_convert_type(w << 16, F32).astype(BF16))
    return jnp.concatenate(cols, axis=-1)


def _prefetch_pos_rows(x4_hbm, buf, sems, n_pos):
    bi = pl.program_id(0)
    j = pl.program_id(1)
    nj = pl.num_programs(1)
    g = bi * nj + j
    slot = g % 2

    def copies(sl, b_, j_):
        return [pltpu.make_async_copy(x4_hbm.at[b_, :, n_pos * j_ + il, :], buf.at[sl, il], sems.at[sl])
                for il in range(n_pos)]

    @pl.when(g == 0)
    def _():
        for cp in copies(0, 0, 0):
            cp.start()

    @pl.when(g + 1 < pl.num_programs(0) * nj)
    def _():
        wrap = j + 1 == nj
        for cp in copies(1 - slot, jnp.where(wrap, bi + 1, bi), jnp.where(wrap, 0, j + 1)):
            cp.start()

    for cp in copies(slot, bi, j):
        cp.wait()
    return slot


def _to_lane_blocks(dst, src):
    for kb in range(dst.shape[0]):
        dst[kb] = src[:, kb * LANES:(kb + 1) * LANES]


def _adaln_kernel(c_ref, w_ref, b_ref, o_ref):
    c = c_ref[...]
    ca = c * jax.nn.sigmoid(c)
    o_ref[...] = jnp.dot(ca, w_ref[...], preferred_element_type=F32, precision=HIGHEST) + b_ref[...]


def _adaln(c, w_ada, b_ada):
    b, d = c.shape
    n = w_ada.shape[1] // d
    return pl.pallas_call(
        _adaln_kernel,
        grid=(n,),
        in_specs=[pl.BlockSpec((b, d), lambda j: (0, 0)),
                  pl.BlockSpec((d, d), lambda j: (0, j)),
                  pl.BlockSpec((1, d), lambda j: (0, j))],
        out_specs=pl.BlockSpec((b, d), lambda j: (0, j)),
        out_shape=jax.ShapeDtypeStruct((b, n * d), F32),
        compiler_params=_cparams(1),
        name="adaln",
    )(c, w_ada, b_ada.reshape(1, -1))


def _inproj_kernel(x4_hbm, x_ref, mod_ref, g_ref, wqkv_ref, bqkv_ref, wut_ref, but_ref, q_ref, k_ref, v_ref, ut_ref,
                   xp_buf, sems):
    nc = ut_ref.shape[3]
    slot = _prefetch_pos_rows(x4_hbm, xp_buf, sems, POS_PER_STEP)
    gain = g_ref[...]
    scale = 1.0 + mod_ref[0, 1:2, :]
    shift = mod_ref[0, 0:1, :]

    def norm_mod(x):
        return (_rms(x) * gain * scale + shift).astype(BF16)

    proj = jnp.dot(norm_mod(x_ref[0]), wqkv_ref[...], preferred_element_type=F32) + bqkv_ref[...]
    q_ref[0] = proj[:, :D_ATTN].astype(BF16)
    k_ref[0] = proj[:, D_ATTN:D_ATTN + D_KV].astype(BF16)
    v_ref[0] = proj[:, D_ATTN + D_KV:].astype(BF16)

    hs = jnp.concatenate([norm_mod(xp_buf[slot, il]) for il in range(POS_PER_STEP)], axis=0)
    ut = lax.dot_general(wut_ref[...], hs, (((1,), (1,)), ((), ())), preferred_element_type=F32) + but_ref[...]
    for il in range(POS_PER_STEP):
        piece = ut[:, il * nc:(il + 1) * nc].astype(BF16)
        ut_ref[0, :, il * SSM_GROUP:(il + 1) * SSM_GROUP, :] = piece.reshape(N_GROUPS, SSM_GROUP, nc)


def _inproj(x, mod, gain, w_in, b_in):
    b, s, d = x.shape
    nc = s // SSM_CHUNK
    rows = POS_PER_STEP * nc
    row = lambda bi, j: (bi, j, 0)
    const = lambda bi, j: (0, 0)
    w_qkv = w_in[:, :D_QKV].astype(BF16)
    w_ut = w_in[:, D_QKV:].T.astype(BF16)
    return pl.pallas_call(
        _inproj_kernel,
        grid=(b, SSM_CHUNK // POS_PER_STEP),
        in_specs=[pl.BlockSpec(memory_space=pl.ANY),
                  pl.BlockSpec((1, rows, d), row),
                  pl.BlockSpec((1, 6, d), lambda bi, j: (bi, 0, 0)),
                  pl.BlockSpec((1, d), const),
                  pl.BlockSpec((d, D_QKV), const),
                  pl.BlockSpec((1, D_QKV), const),
                  pl.BlockSpec((D_SSM, d), const),
                  pl.BlockSpec((D_SSM, 1), const)],
        out_specs=[pl.BlockSpec((1, rows, D_ATTN), row),
                   pl.BlockSpec((1, rows, D_KV), row),
                   pl.BlockSpec((1, rows, D_KV), row),
                   pl.BlockSpec((1, N_GROUPS, POS_PER_STEP * SSM_GROUP, nc), lambda bi, j: (bi, 0, j, 0))],
        out_shape=[jax.ShapeDtypeStruct((b, s, D_ATTN), BF16),
                   jax.ShapeDtypeStruct((b, s, D_KV), BF16),
                   jax.ShapeDtypeStruct((b, s, D_KV), BF16),
                   jax.ShapeDtypeStruct((b, N_GROUPS, SSM_ROW, nc), BF16)],
        scratch_shapes=[pltpu.VMEM((2, POS_PER_STEP, nc, d), F32), pltpu.SemaphoreType.DMA((2,))],
        compiler_params=_cparams(2),
        name="inproj",
    )(x.reshape(b, nc, SSM_CHUNK, d), x, mod, gain.reshape(1, d), w_qkv, b_in[:D_QKV].reshape(1, D_QKV), w_ut, b_in[D_QKV:].reshape(D_SSM, 1))


def _half_norm(x, half_ones):
    sq = x * x
    hi = sq.astype(BF16)
    lo = (sq - hi.astype(F32)).astype(BF16)
    ssq = (jnp.dot(hi, half_ones, preferred_element_type=F32) + jnp.dot(lo, half_ones, preferred_element_type=F32))
    return x * lax.rsqrt(ssq * (1.0 / HEAD_DIM) + EPS)


def _attn_block(first, q, k_prev, k_cur, v_prev, v_cur, sinks_ref, qn, low, half_ones, upper, rblk):
    no_prev = jnp.where(first, NEG_INF, 0.0)
    all_ones = jnp.ones((BLOCK, LANES), BF16)
    out_blocks = []
    for hk in range(N_KV_HEADS):
        qs = []
        for j in range(Q_PER_KV // 2):
            blk = hk * (Q_PER_KV // 2) + j
            qb = _half_norm(q[:, blk * LANES:(blk + 1) * LANES], half_ones) * qn * (1.0 / math.sqrt(HEAD_DIM))
            qs.append(jnp.where(low, qb, 0.0))
            qs.append(jnp.where(low, 0.0, qb))
        qg = jnp.concatenate(qs, axis=0).astype(BF16)
        nt = (((1,), (1,)), ((), ()))
        s_prev = lax.dot_general(qg, k_prev[hk], nt, preferred_element_type=F32)
        s_cur = lax.dot_general(qg, k_cur[hk], nt, preferred_element_type=F32)
        s = jnp.where(upper, s_prev + no_prev, s_cur)
        sink = jnp.zeros((Q_PER_KV * BLOCK, 1), F32)
        for g in range(Q_PER_KV):
            sink = jnp.where(rblk == g, sinks_ref[hk * Q_PER_KV + g], sink)
        m = jnp.maximum(jnp.max(s, axis=-1, keepdims=True), sink)
        p = jnp.exp(s - m).astype(BF16)
        zero = jnp.zeros_like(p)
        den = jnp.dot(p, all_ones, preferred_element_type=F32) + jnp.exp(sink - m)
        o = (jnp.dot(jnp.where(upper, p, zero), v_prev[hk], preferred_element_type=F32)
             + jnp.dot(jnp.where(upper, zero, p), v_cur[hk], preferred_element_type=F32)) / den
        for j in range(Q_PER_KV // 2):
            ev = o[(2 * j) * BLOCK:(2 * j + 1) * BLOCK]
            od = o[(2 * j + 1) * BLOCK:(2 * j + 2) * BLOCK]
            out_blocks.append(jnp.where(low, ev, od))
    return jnp.concatenate(out_blocks, axis=-1)


def _attn_kernel(sinks_ref, q_ref, k_ref, v_ref, qn_ref, kn_ref, on_ref, o_hbm, a_buf, sems, *, n_steps):
    step = pl.program_id(1)
    g = pl.program_id(0) * pl.num_programs(1) + step
    slot = g % 2
    cps = ATTN_ROWS // SSM_CHUNK
    nq = ATTN_ROWS // BLOCK

    def out_copies(sl, b_, s_):
        return [pltpu.make_async_copy(a_buf.at[sl, :, i, :], o_hbm.at[b_, i, pl.ds(s_ * cps, cps), :], sems.at[sl])
                for i in range(SSM_CHUNK)]

    @pl.when(g >= 2)
    def _():
        for cp in out_copies(slot, 0, 0):
            cp.wait()

    low = lax.broadcasted_iota(I32, (1, LANES), 1) < HEAD_DIM
    rows = Q_PER_KV * BLOCK
    upper = lax.broadcasted_iota(I32, (rows, BLOCK), 1) > lax.broadcasted_iota(I32, (rows, BLOCK), 0) % BLOCK
    rblk = lax.broadcasted_iota(I32, (rows, 1), 0) // BLOCK

    cur = pl.multiple_of(step * ATTN_ROWS, ATTN_ROWS)
    prev = pl.multiple_of(jnp.maximum(step * nq - 1, 0) * BLOCK, BLOCK)
    kall = jnp.concatenate([k_ref[0, pl.ds(prev, BLOCK), :], k_ref[0, pl.ds(cur, ATTN_ROWS), :]], axis=0).astype(F32)
    vall = jnp.concatenate([v_ref[0, pl.ds(prev, BLOCK), :], v_ref[0, pl.ds(cur, ATTN_ROWS), :]], axis=0).astype(F32)
    half_ones = (lax.broadcasted_iota(I32, (LANES, LANES), 0) // HEAD_DIM
                 == lax.broadcasted_iota(I32, (LANES, LANES), 1) // HEAD_DIM).astype(BF16)
    kall = _half_norm(kall, half_ones) * kn_ref[...]
    kswap = pltpu.roll(kall, HEAD_DIM, axis=1)
    vswap = pltpu.roll(vall, HEAD_DIM, axis=1)
    k_dup = [jnp.where(low, kall, kswap).astype(BF16), jnp.where(low, kswap, kall).astype(BF16)]
    v_dup = [jnp.where(low, vall, vswap).astype(BF16), jnp.where(low, vswap, vall).astype(BF16)]
    blk = lambda a, i: [a[hk][i * BLOCK:(i + 1) * BLOCK] for hk in range(N_KV_HEADS)]

    for qb in range(nq):
        q = q_ref[0, qb * BLOCK:(qb + 1) * BLOCK, :].astype(F32)
        attn = _attn_block((step == 0) if qb == 0 else False, q, blk(k_dup, qb), blk(k_dup, qb + 1),
                           blk(v_dup, qb), blk(v_dup, qb + 1), sinks_ref, qn_ref[...], low, half_ones, upper, rblk)
        attn = _rms(attn) * on_ref[...]
        cpb = BLOCK // SSM_CHUNK
        a_buf[slot, qb * cpb:(qb + 1) * cpb] = attn.reshape(cpb, SSM_CHUNK, D_ATTN)

    for cp in out_copies(slot, pl.program_id(0), step):
        cp.start()

    @pl.when(g == n_steps - 1)
    def _():
        for cp in out_copies(slot, 0, 0):
            cp.wait()
        if n_steps > 1:
            for cp in out_copies(1 - slot, 0, 0):
                cp.wait()


def _attention(q, k, v, sinks, q_norm, k_norm, out_norm):
    b, s, _ = q.shape
    tile2 = lambda g: jnp.tile(g.reshape(1, HEAD_DIM), (1, 2))
    cps = ATTN_ROWS // SSM_CHUNK
    return pl.pallas_call(
        functools.partial(_attn_kernel, n_steps=b * (s // ATTN_ROWS)),
        grid=(b, s // ATTN_ROWS),
        in_specs=[pl.BlockSpec(memory_space=pltpu.SMEM),
                  pl.BlockSpec((1, ATTN_ROWS, D_ATTN), lambda bi, n: (bi, n, 0)),
                  pl.BlockSpec((1, s, D_KV), lambda bi, n: (bi, 0, 0)),
                  pl.BlockSpec((1, s, D_KV), lambda bi, n: (bi, 0, 0)),
                  pl.BlockSpec((1, LANES), lambda bi, n: (0, 0)),
                  pl.BlockSpec((1, LANES), lambda bi, n: (0, 0)),
                  pl.BlockSpec((1, D_ATTN), lambda bi, n: (0, 0))],
        out_specs=pl.BlockSpec(memory_space=pl.ANY),
        out_shape=jax.ShapeDtypeStruct((b, SSM_CHUNK, s // SSM_CHUNK, D_ATTN), F32),
        scratch_shapes=[pltpu.VMEM((2, cps, SSM_CHUNK, D_ATTN), F32), pltpu.SemaphoreType.DMA((2,))],
        compiler_params=_cparams(2),
        name="attention",
    )(sinks, q, k, v, tile2(q_norm), tile2(k_norm), out_norm.reshape(1, D_ATTN))


def _cmul(ar, ai, br, bi):
    return ar * br - ai * bi, ar * bi + ai * br


def _ssm_param_kernel(*refs):
    for gi in range(refs[0].shape[0]):
        _ssm_param_group(gi, *refs)


def _ssm_param_group(gi, lam_ref, bre_ref, bim_ref, cre_ref, cim_ref, tt_ref, wz_ref, wyt_ref, cs_ref):
    f32dot = functools.partial(jnp.dot, preferred_element_type=F32, precision=HIGHEST)
    lr = lam_ref[gi, 0:1, :]
    li = lam_ref[gi, 1:2, :]
    dt = jnp.exp(lam_ref[gi, 2:3, :])
    rho = lr * dt
    th = li * dt
    imag_lane = lax.broadcasted_iota(I32, (1, LANES), 1) >= STATE

    kk = (lax.broadcasted_iota(I32, (N_POW, 1), 0) - (SSM_CHUNK - 1)).astype(F32)
    mag = jnp.exp(rho * kk)
    pw_r = mag * jnp.cos(th * kk)
    pw_i = mag * jnp.sin(th * kk)
    lb_r = pw_r[SSM_CHUNK:SSM_CHUNK + 1]
    lb_i = pw_i[SSM_CHUNK:SSM_CHUNK + 1]
    den = lr * lr + li * li
    coef_r = ((lb_r - 1.0) * lr + lb_i * li) / den
    coef_i = (lb_i * lr - (lb_r - 1.0) * li) / den

    eye = (lax.broadcasted_iota(I32, (SSM_GROUP, SSM_GROUP), 0)
           == lax.broadcasted_iota(I32, (SSM_GROUP, SSM_GROUP), 1)).astype(F32)
    lane_fold = (lax.broadcasted_iota(I32, (STATE, LANES), 1) % STATE
                 == lax.broadcasted_iota(I32, (STATE, LANES), 0)).astype(F32)

    def tile_pos(x):
        return jnp.concatenate([x] * SSM_CHUNK, axis=0)

    def power_rows(k_of_pos):
        idx = [k_of_pos(p) + (SSM_CHUNK - 1) for p in range(SSM_CHUNK)]
        rep = lambda t: jnp.concatenate([jnp.broadcast_to(t[r:r + 1], (SSM_GROUP, LANES)) for r in idx], axis=0)
        return rep(pw_r), rep(pw_i)

    def b_rows(b_ref):
        b2 = jnp.concatenate([b_ref[gi], b_ref[gi]], axis=0)
        return tile_pos(lax.dot_general(eye, b2, (((1,), (1,)), ((), ())), preferred_element_type=F32,
                                        precision=HIGHEST))

    def c_rows(c_ref):
        return tile_pos(f32dot(c_ref[gi], lane_fold))

    bbar_r, bbar_i = _cmul(coef_r, coef_i, b_rows(bre_ref), b_rows(bim_ref))
    c_r = c_rows(cre_ref)
    c_i = c_rows(cim_ref)

    a_r, a_i = _cmul(bbar_r, bbar_i, *power_rows(lambda p: -p))
    a2c = jnp.where(imag_lane, -a_i, a_r)
    m_r, m_i = _cmul(c_r, c_i, *power_rows(lambda p: p))
    bmc = jnp.where(imag_lane, m_i, m_r)
    tt = f32dot(bmc, a2c.T)
    causal = (lax.broadcasted_iota(I32, (SSM_ROW, 1), 0) // SSM_GROUP
              >= lax.broadcasted_iota(I32, (1, SSM_ROW), 1) // SSM_GROUP)
    tt_ref[gi] = jnp.where(causal, tt, 0.0).astype(BF16)

    w_r, w_i = _cmul(bbar_r, bbar_i, *power_rows(lambda p: SSM_CHUNK - 1 - p))
    wz_ref[gi, :, :LANES] = jnp.where(imag_lane, w_i, w_r).astype(BF16)
    wz_ref[gi, :, LANES:] = jnp.where(imag_lane, w_r, w_i).astype(BF16)

    y_r, y_i = _cmul(c_r, c_i, *power_rows(lambda p: p + 1))
    wyt_ref[gi] = jnp.where(imag_lane, -y_i, y_r).astype(BF16)

    cs_ref[gi, 0:1, :] = pw_r[N_POW - 1:N_POW]
    cs_ref[gi, 1:2, :] = jnp.where(imag_lane, pw_i[N_POW - 1:N_POW], -pw_i[N_POW - 1:N_POW])


def _ssm_params(lam_re, lam_im, log_dt, b_re, b_im, c_re, c_im):
    g = lam_re.shape[0]
    lam = jnp.stack([lam_re, lam_im, jnp.broadcast_to(log_dt[:, None], (g, STATE))], axis=1)
    lam = jnp.concatenate([lam, lam], axis=2)
    ng = SSM_GROUPS_PER_STEP
    blk = lambda *shape: pl.BlockSpec((ng,) + shape, lambda i: (i, 0, 0))
    return pl.pallas_call(
        _ssm_param_kernel,
        grid=(g // ng,),
        in_specs=[blk(3, LANES), blk(STATE, SSM_GROUP), blk(STATE, SSM_GROUP), blk(SSM_GROUP, STATE),
                  blk(SSM_GROUP, STATE)],
        out_specs=[blk(SSM_ROW, SSM_ROW), blk(SSM_ROW, SSM_ROW), blk(SSM_ROW, LANES), blk(2, LANES)],
        out_shape=[jax.ShapeDtypeStruct((g, SSM_ROW, SSM_ROW), BF16),
                   jax.ShapeDtypeStruct((g, SSM_ROW, SSM_ROW), BF16),
                   jax.ShapeDtypeStruct((g, SSM_ROW, LANES), BF16),
                   jax.ShapeDtypeStruct((g, 2, LANES), F32)],
        compiler_params=_cparams(1),
        name="ssm_params",
    )(lam, b_re, b_im, c_re, c_im)


def _ssm_kernel(ut_ref, tt_ref, wz_ref, wyt_ref, cs_ref, d_ref, yt_ref, z_scr, s_scr):
    batch, ng, _, nc = ut_ref.shape
    uts = [jnp.concatenate([ut_ref[b, gi] for b in range(batch)], axis=1) for gi in range(ng)]
    for gi in range(ng):
        z = lax.dot_general(uts[gi], wz_ref[gi], (((0,), (0,)), ((), ())), preferred_element_type=F32)
        _to_lane_blocks(z_scr.at[gi], z)
    c1 = [cs_ref[gi, 0:1, :] for gi in range(ng)]
    c2 = [cs_ref[gi, 1:2, :] for gi in range(ng)]

    def step(c, carry):
        rows = pl.ds(c, batch, stride=nc)
        out = []
        for gi in range(ng):
            s1, s2 = carry[gi]
            s_scr[gi, rows, :] = s1
            out.append((c1[gi] * s1 + c2[gi] * s2 + z_scr[gi, 0, rows, :],
                        c1[gi] * s2 - c2[gi] * s1 + z_scr[gi, 1, rows, :]))
        return tuple(out)

    zero = jnp.zeros((batch, LANES), F32)
    lax.fori_loop(0, nc, step, ((zero, zero),) * ng, unroll=8)
    for gi in range(ng):
        y = jnp.dot(tt_ref[gi], uts[gi], preferred_element_type=F32)
        y = y + lax.dot_general(wyt_ref[gi], s_scr[gi].astype(BF16), (((1,), (1,)), ((), ())),
                                preferred_element_type=F32)
        y = y + d_ref[gi] * uts[gi].astype(F32)
        for b in range(batch):
            yt_ref[b, gi] = y[:, b * nc:(b + 1) * nc]


def _ssm(ut, tt, wz, wyt, cs, d_skip):
    b, g, _, nc = ut.shape
    ng = SSM_GROUPS_PER_STEP
    d_col = jnp.tile(d_skip.reshape(g, 1, SSM_GROUP), (1, SSM_CHUNK, 1)).reshape(g, SSM_ROW, 1)
    blk = lambda *shape: pl.BlockSpec((ng,) + shape, lambda i: (i, 0, 0))
    act = pl.BlockSpec((b, ng, SSM_ROW, nc), lambda i: (0, i, 0, 0))
    return pl.pallas_call(
        _ssm_kernel,
        grid=(g // ng,),
        in_specs=[act, blk(SSM_ROW, SSM_ROW), blk(SSM_ROW, SSM_ROW), blk(SSM_ROW, LANES), blk(2, LANES),
                  blk(SSM_ROW, 1)],
        out_specs=act,
        out_shape=jax.ShapeDtypeStruct((b, g, SSM_ROW, nc), F32),
        scratch_shapes=[pltpu.VMEM((ng, SSM_ROW // LANES, b * nc, LANES), F32), pltpu.VMEM((ng, b * nc, LANES), F32)],
        compiler_params=_cparams(1),
        name="ssm",
    )(ut, tt, wz, wyt, cs, d_col)


def _post_kernel(x4_hbm, attn_ref, yt_ref, mod_ref, wglut_ref, bglu_ref, sn_ref, wout_ref, nf_ref, wr_ref, br_ref,
                 tri_ref, x1_ref, h2_ref, eidx_ref, wts_ref, lrank_ref, r0_ref, cnt_ref, carry_ref, xp_buf, sems):
    @pl.when((pl.program_id(0) == 0) & (pl.program_id(1) == 0))
    def _():
        carry_ref[...] = jnp.zeros_like(carry_ref)

    slot = _prefetch_pos_rows(x4_hbm, xp_buf, sems, POST_SUB * POST_POS)
    nc = attn_ref.shape[2]
    ts = POST_POS * nc
    d = x1_ref.shape[3]
    iota_e = lax.broadcasted_iota(I32, (N_EXPERTS, ts), 0).astype(F32)
    counts = []
    for sub in range(POST_SUB):
        pos = range(sub * POST_POS, (sub + 1) * POST_POS)
        lanes = slice(sub * ts, (sub + 1) * ts)
        yt = jnp.concatenate(
            [yt_ref[0, :, il * SSM_GROUP:(il + 1) * SSM_GROUP, :].reshape(D_SSM, nc) for il in pos], axis=1)
        g = jax.nn.gelu(yt)
        gate = jax.nn.sigmoid(jnp.dot(wglut_ref[...], g.astype(BF16), preferred_element_type=F32) + bglu_ref[...])
        ssm_t = _rms(g * gate, axis=0) * sn_ref[...]
        attn = attn_ref[0, sub * POST_POS:(sub + 1) * POST_POS].reshape(ts, D_ATTN)
        mixed = jnp.concatenate([attn.astype(BF16), ssm_t.T.astype(BF16)], axis=-1)
        o = jnp.dot(mixed, wout_ref[...], preferred_element_type=F32)
        x = jnp.concatenate([xp_buf[slot, il] for il in pos], axis=0)
        x1 = x + mod_ref[0, 2:3, :] * o
        x1_ref[0, sub * POST_POS:(sub + 1) * POST_POS] = x1.reshape(POST_POS, nc, d)
        h2 = _rms(x1) * nf_ref[...] * (1.0 + mod_ref[0, 4:5, :]) + mod_ref[0, 3:4, :]
        h2_ref[0, sub * POST_POS:(sub + 1) * POST_POS] = h2.astype(BF16).reshape(POST_POS, nc, d)

        logits = lax.dot_general(wr_ref[...], h2.astype(BF16), (((1,), (1,)), ((), ())),
                                 preferred_element_type=F32) + br_ref[...]
        l = logits
        idxs, vals = [], []
        for _ in range(TOP_K):
            m = jnp.max(l, axis=0, keepdims=True)
            idx = jnp.min(jnp.where(l == m, iota_e, float(N_EXPERTS)), axis=0, keepdims=True)
            idxs.append(idx)
            vals.append(m)
            l = jnp.where(iota_e == idx, -jnp.inf, l)
        es = [jnp.exp(v - vals[0]) for v in vals]
        tot = es[0] + es[1] + es[2] + es[3]
        member = jnp.zeros((N_EXPERTS, ts), F32)
        for idx in idxs:
            member = member + (iota_e == idx).astype(F32)
        before = jnp.dot(member.astype(BF16), tri_ref[...], preferred_element_type=F32)
        for k in range(TOP_K):
            eidx_ref[k:k + 1, lanes] = idxs[k].astype(I32)
            wts_ref[k:k + 1, lanes] = es[k] / tot
            lrank_ref[k:k + 1, lanes] = jnp.sum(jnp.where(iota_e == idxs[k], before, 0.0), axis=0,
                                                keepdims=True).astype(I32)
        counts.append(jnp.sum(member, axis=1, keepdims=True))

    carry = carry_ref[...]
    for sub in range(POST_SUB):
        r0_ref[sub] = carry.astype(I32)
        carry = carry + counts[sub]
    carry_ref[...] = carry
    cnt_ref[...] = carry.astype(I32)


def _post(x, attn, yt, mod, w_glu, b_glu, ssm_norm, w_out, norm_ffn, w_router, b_router):
    b, s, d = x.shape
    nc = s // SSM_CHUNK
    ts = POST_POS * nc
    npos = POST_SUB * POST_POS
    nt = SSM_CHUNK // npos
    t = b * s
    pm = lambda bi, j: (bi, j, 0, 0)
    const = lambda bi, j: (0, 0)
    tok = lambda bi, j: (0, bi * nt + j)
    tri = (lax.broadcasted_iota(I32, (ts, ts), 0) < lax.broadcasted_iota(I32, (ts, ts), 1)).astype(BF16)
    col = lambda a: a.reshape(-1, 1)
    return pl.pallas_call(
        _post_kernel,
        grid=(b, nt),
        in_specs=[pl.BlockSpec(memory_space=pl.ANY),
                  pl.BlockSpec((1, npos, nc, D_ATTN), pm),
                  pl.BlockSpec((1, N_GROUPS, npos * SSM_GROUP, nc), lambda bi, j: (bi, 0, j, 0)),
                  pl.BlockSpec((1, 6, d), lambda bi, j: (bi, 0, 0)),
                  pl.BlockSpec((D_SSM, D_SSM), const),
                  pl.BlockSpec((D_SSM, 1), const),
                  pl.BlockSpec((D_SSM, 1), const),
                  pl.BlockSpec((d, d), const),
                  pl.BlockSpec((1, d), const),
                  pl.BlockSpec((N_EXPERTS, d), const),
                  pl.BlockSpec((N_EXPERTS, 1), const),
                  pl.BlockSpec((ts, ts), const)],
        out_specs=[pl.BlockSpec((1, npos, nc, d), pm),
                   pl.BlockSpec((1, npos, nc, d), pm),
                   pl.BlockSpec((TOP_K, POST_SUB * ts), tok),
                   pl.BlockSpec((TOP_K, POST_SUB * ts), tok),
                   pl.BlockSpec((TOP_K, POST_SUB * ts), tok),
                   pl.BlockSpec((POST_SUB, N_EXPERTS, 1), lambda bi, j: (bi * nt + j, 0, 0)),
                   pl.BlockSpec((N_EXPERTS, 1), const)],
        out_shape=[jax.ShapeDtypeStruct((b, SSM_CHUNK, nc, d), F32),
                   jax.ShapeDtypeStruct((b, SSM_CHUNK, nc, d), BF16),
                   jax.ShapeDtypeStruct((TOP_K, t), I32),
                   jax.ShapeDtypeStruct((TOP_K, t), F32),
                   jax.ShapeDtypeStruct((TOP_K, t), I32),
                   jax.ShapeDtypeStruct((b * nt * POST_SUB, N_EXPERTS, 1), I32),
                   jax.ShapeDtypeStruct((N_EXPERTS, 1), I32)],
        scratch_shapes=[pltpu.VMEM((N_EXPERTS, 1), F32), pltpu.VMEM((2, npos, nc, d), F32),
                        pltpu.SemaphoreType.DMA((2,))],
        compiler_params=_cparams(2),
        name="post",
    )(x.reshape(b, nc, SSM_CHUNK, d), attn, yt, mod, w_glu.T.astype(BF16), col(b_glu), col(ssm_norm), w_out.astype(BF16),
      norm_ffn.reshape(1, -1), w_router.T.astype(BF16), col(b_router), tri)


def _route_kernel(eidx_ref, lrank_ref, r0_ref, cnt_ref, ls_ref, tab_ref, te_ref, nv_ref, nx_ref, pad_ref):
    cnt = cnt_ref[...]
    tiles = (cnt + (RUN - 1 + FFN_ROWS - 1)) // FFN_ROWS
    er = lax.broadcasted_iota(I32, (N_EXPERTS, N_EXPERTS), 0)
    ec = lax.broadcasted_iota(I32, (N_EXPERTS, N_EXPERTS), 1)
    ltri = (ec < er).astype(BF16)

    def excl_cumsum(v):
        vb = jnp.broadcast_to(v.astype(F32), (N_EXPERTS, LANES)).astype(BF16)
        return jnp.dot(ltri, vb, preferred_element_type=F32)[:, 0:1].astype(I32)

    start_t = excl_cumsum(tiles)
    end_t = start_t + tiles
    start = start_t * FFN_ROWS
    pad_ref[...] = start + cnt

    nb = r0_ref.shape[0]
    ts = eidx_ref.shape[1] // nb
    iota_e = lax.broadcasted_iota(I32, (N_EXPERTS, ts), 0)
    iota_t = lax.broadcasted_iota(I32, (N_EXPERTS, TABW), 0)
    lane = lax.broadcasted_iota(I32, (1, TABW), 1)

    def block(b, carry):
        lanes = pl.ds(pl.multiple_of(b * ts, ts), ts)
        sels = [iota_e == eidx_ref[k:k + 1, lanes] for k in range(TOP_K)]
        member = sels[0].astype(I32) + sels[1].astype(I32) + sels[2].astype(I32) + sels[3].astype(I32)
        units = lax.shift_right_logical(jnp.sum(member, axis=1, keepdims=True) + (RUN - 1), RUN_SHIFT)
        u0 = excl_cumsum(units)
        for k in range(TOP_K):
            first = jnp.sum(jnp.where(sels[k], u0, 0), axis=0, keepdims=True)
            ls_ref[k:k + 1, lanes] = first * RUN + lrank_ref[k:k + 1, lanes]
        n_big = lax.shift_right_logical(units, 1)
        n_small = units & 1
        slot0 = start + r0_ref[b]

        def chunk_rows(idx, counts):
            c0 = excl_cumsum(counts)
            sel = iota_t == jnp.sum((idx >= c0 + counts).astype(I32), axis=0, keepdims=True)
            pick = lambda v: jnp.sum(jnp.where(sel, v, 0), axis=0, keepdims=True)
            j = idx - pick(c0)
            return pick(slot0), pick(u0), j, pick(n_big), idx < jnp.max(c0 + counts, axis=0, keepdims=True)

        s_b, u_b, j_b, _, ok_b = chunk_rows(lane, n_big)
        s_s, u_s, _, nb_s, ok_s = chunk_rows(lane - SMALL0, n_small)
        small = lane >= SMALL0
        slot = jnp.where(small, s_s + nb_s * BIG, s_b + j_b * BIG)
        local = jnp.where(small, (u_s + 2 * nb_s) * RUN, (u_b + 2 * j_b) * RUN)
        ok = (small & ok_s) | (jnp.logical_not(small) & ok_b)
        counts = jnp.where(lane == TABW - 2, jnp.sum(n_big, axis=0, keepdims=True),
                           jnp.sum(n_small, axis=0, keepdims=True))
        tab_ref[b, 0:1, :] = jnp.where(lane >= TABW - 2, counts, jnp.where(ok, slot, -1))
        tab_ref[b, 1:2, :] = jnp.where(ok, local, 0)
        return carry

    lax.fori_loop(0, nb, block, 0, unroll=2)

    nv = jnp.max(end_t, axis=0, keepdims=True)
    width = te_ref.shape[1]
    ti = jnp.minimum(lax.broadcasted_iota(I32, (N_EXPERTS, width), 1), nv - 1)
    te = jnp.minimum(jnp.sum((ti >= end_t).astype(I32), axis=0, keepdims=True), N_EXPERTS - 1)
    te_ref[...] = te
    nv_ref[...] = jnp.broadcast_to(nv, nv_ref.shape)
    ie = lax.broadcasted_iota(I32, (N_EXPERTS, width), 0)
    own_end = jnp.sum(jnp.where(ie == te, end_t, 0), axis=0, keepdims=True)
    nxt = jnp.minimum(jnp.sum((own_end >= end_t).astype(I32), axis=0, keepdims=True), N_EXPERTS - 1)
    nx_ref[...] = jnp.where(own_end < nv, nxt, -1)


def _route(eidx, lrank, r0, cnt, n_tiles):
    t = eidx.shape[1]
    nb = r0.shape[0]
    width = -(-n_tiles // LANES) * LANES
    return pl.pallas_call(
        _route_kernel,
        out_shape=[jax.ShapeDtypeStruct((TOP_K, t), I32),
                   jax.ShapeDtypeStruct((nb, 2, TABW), I32),
                   jax.ShapeDtypeStruct((1, width), I32),
                   jax.ShapeDtypeStruct((1, LANES), I32),
                   jax.ShapeDtypeStruct((1, width), I32),
                   jax.ShapeDtypeStruct((N_EXPERTS, 1), I32)],
        name="route",
    )(eidx, lrank, r0, cnt)


def _for_chunk_pairs(n, fn):
    def body(i, carry):
        fn(2 * i, 0)

        @pl.when(2 * i + 1 < n)
        def _():
            fn(2 * i + 1, 1)
        return carry
    lax.fori_loop(0, lax.shift_right_logical(n + 1, 1), body, 0)


def _for_block_chunks(tab_ref, blk, fn):
    base = blk * (2 * TABW)
    for first, count_lane, n_rows in ((0, TABW - 2, BIG), (SMALL0, TABW - 1, RUN)):
        def visit(c, parity, first=first, n_rows=n_rows):
            fn(tab_ref[base + first + c], pl.multiple_of(tab_ref[base + TABW + first + c], RUN), n_rows, parity)
        _for_chunk_pairs(tab_ref[base + count_lane], visit)


def _local_rows(ts):
    return ts * TOP_K + N_EXPERTS * RUN


def _dispatch_kernel(tab_ref, pad_ref, nvt_ref, h_ref, ls_ref, xs_ref, buf, zbuf, sems, zsem):
    b = pl.program_id(0)
    slot = b % 2
    ts = h_ref.shape[0]
    local = buf.shape[2]

    def chunk_copy(sl, slot_row, local_row, n):
        return pltpu.make_async_copy(buf.at[sl, :, pl.ds(local_row, n), :], xs_ref.at[:, pl.ds(slot_row, n), :],
                                     sems.at[sl])

    @pl.when(b == 0)
    def _():
        zbuf[...] = jnp.zeros_like(zbuf)
        zrows = zbuf.shape[1]
        zero = lambda row: pltpu.make_async_copy(zbuf, xs_ref.at[:, pl.ds(row, zrows), :], zsem)
        for phase in range(3):
            for e in range(phase, N_EXPERTS, 3):
                zero(pad_ref[e]).start()
            for e in range(phase, N_EXPERTS, 3):
                zero(pad_ref[e]).wait()
        ztile = lambda i: pltpu.make_async_copy(zbuf.at[:, pl.ds(0, FFN_ROWS), :],
                                                xs_ref.at[:, pl.ds((nvt_ref[0] + i) * FFN_ROWS, FFN_ROWS), :], zsem)

        def tail_start(i, carry):
            ztile(i).start()
            return carry

        def tail_wait(i, carry):
            ztile(i).wait()
            return carry
        lax.fori_loop(0, nvt_ref[1] - nvt_ref[0], tail_start, 0)
        lax.fori_loop(0, nvt_ref[1] - nvt_ref[0], tail_wait, 0)

    r = lax.broadcasted_iota(I32, (local, ts), 0)
    hit = (r == ls_ref[0:1, :]) | (r == ls_ref[1:2, :]) | (r == ls_ref[2:3, :]) | (r == ls_ref[3:4, :])
    hit = hit.astype(BF16)
    for pb in range(PANELS):
        srt = jnp.dot(hit, h_ref[:, pb * PANEL_COLS:(pb + 1) * PANEL_COLS], preferred_element_type=F32)
        buf[slot, pb] = _pack_panel(srt, exact=True)

    @pl.when(b > 0)
    def _():
        _for_block_chunks(tab_ref, b - 1, lambda s, l, n, p: chunk_copy(1 - slot, s, l, n).wait())

    _for_block_chunks(tab_ref, b, lambda s, l, n, p: chunk_copy(slot, s, l, n).start(priority=p))

    @pl.when(b == pl.num_programs(0) - 1)
    def _():
        _for_block_chunks(tab_ref, b, lambda s, l, n, p: chunk_copy(slot, s, l, n).wait())


def _dispatch(tab, pad, nvt, h2, ls, n_rows):
    t, d = h2.shape
    nb = tab.shape[0] // (2 * TABW)
    ts = t // nb
    return pl.pallas_call(
        _dispatch_kernel,
        grid_spec=pltpu.PrefetchScalarGridSpec(
            num_scalar_prefetch=3,
            grid=(nb,),
            in_specs=[pl.BlockSpec((ts, d), lambda i, *_: (i, 0)),
                      pl.BlockSpec((TOP_K, ts), lambda i, *_: (0, i))],
            out_specs=pl.BlockSpec(memory_space=pl.ANY),
            scratch_shapes=[pltpu.VMEM((2, PANELS, _local_rows(ts), LANES), U32),
                            pltpu.VMEM((PANELS, FFN_ROWS + RUN, LANES), U32),
                            pltpu.SemaphoreType.DMA((2,)), pltpu.SemaphoreType.DMA],
        ),
        out_shape=jax.ShapeDtypeStruct((PANELS, n_rows, LANES), U32),
        compiler_params=_cparams(1, ROW_VMEM_BYTES),
        name="dispatch",
    )(tab, pad, nvt, h2, ls)


def _ffn_kernel(te_ref, nv_ref, nx_ref, xs_ref, wgu_hbm, bgu_ref, wd_hbm, bd_ref, perm_ref, ys_ref,
                wgu_stage, wd_stage, wg_scr, wu_scr, wd_scr, bg_scr, bu_scr, sems):
    p = pl.program_id(0)
    t0 = 2 * p
    t1 = t0 + 1
    e0 = te_ref[t0]
    e1 = te_ref[t1]
    v0 = t0 < nv_ref[0]
    v1 = t1 < nv_ref[0]
    new0 = (p == 0) | (e0 != te_ref[jnp.maximum(t0 - 1, 0)])
    same = v1 & (e1 == e0)

    def stage_copies(e):
        return (pltpu.make_async_copy(wgu_hbm.at[e], wgu_stage, sems.at[0]),
                pltpu.make_async_copy(wd_hbm.at[e], wd_stage, sems.at[1]))

    def load_expert(t, first):
        e = te_ref[t]
        if first:
            @pl.when(p == 0)
            def _():
                for cp in stage_copies(e):
                    cp.start()

        for cp in stage_copies(e):
            cp.wait()
        bias = bgu_ref[e]
        for c in range(2 * D_FF // PERM):
            cols = slice(c * PERM, (c + 1) * PERM)
            half = slice(c * (PERM // 2), (c + 1) * (PERM // 2))
            w = wgu_stage[:, cols].astype(BF16)
            pw = jnp.dot(w, perm_ref[...], preferred_element_type=F32).astype(BF16)
            wg_scr[:, half] = pw[:, :PERM // 2]
            wu_scr[:, half] = pw[:, PERM // 2:]
            b1 = bias[:, cols].astype(BF16)
            r1 = bias[:, cols] - b1.astype(F32)
            b2 = r1.astype(BF16)
            b3 = (r1 - b2.astype(F32)).astype(BF16)
            terms = jnp.concatenate([b1, b2, b3, jnp.zeros((5, PERM), BF16)], axis=0)
            pb = jnp.sum(jnp.dot(terms, perm_ref[...], preferred_element_type=F32), axis=0, keepdims=True)
            bg_scr[:, half] = pb[:, :PERM // 2]
            bu_scr[:, half] = pb[:, PERM // 2:]
        wd_scr[...] = wd_stage[...].astype(BF16)

        @pl.when(nx_ref[t] >= 0)
        def _():
            for cp in stage_copies(nx_ref[t]):
                cp.start()

    def run(lo, n, e):
        x = _unpack_panels([xs_ref[pb, lo:lo + n, :] for pb in range(PANELS)])
        gate = jnp.dot(x, wg_scr[...], preferred_element_type=F32) + bg_scr[...]
        up = jnp.dot(x, wu_scr[...], preferred_element_type=F32) + bu_scr[...]
        gate = jnp.minimum(gate, SWIGLU_LIMIT)
        up = jnp.clip(up, -SWIGLU_LIMIT, SWIGLU_LIMIT)
        act = ((up + 1.0) * (gate * jax.nn.sigmoid(SWIGLU_ALPHA * gate))).astype(BF16)
        bd = bd_ref[e]
        for pb in range(PANELS):
            cols = slice(pb * PANEL_COLS, (pb + 1) * PANEL_COLS)
            y = jnp.dot(act, wd_scr[:, cols], preferred_element_type=F32) + bd[:, cols]
            ys_ref[pb, lo:lo + n, :] = _pack_panel(y)

    @pl.when(v0 & new0)
    def _():
        load_expert(t0, True)

    @pl.when(same)
    def _():
        run(0, 2 * FFN_ROWS, e0)

    @pl.when(v0 & jnp.logical_not(same))
    def _():
        run(0, FFN_ROWS, e0)

    @pl.when(v1 & jnp.logical_not(same))
    def _():
        load_expert(t1, False)
        run(FFN_ROWS, FFN_ROWS, e1)

    @pl.when(v0 & jnp.logical_not(v1))
    def _():
        ys_ref[:, FFN_ROWS:, :] = xs_ref[:, FFN_ROWS:, :]


def _ffn(te, nv, nx, xs, w_gate_up, bgu, w_down, bd, n_tiles):
    d = D_MODEL
    pair = lambda i, te, nv, nx: (0, jnp.minimum(i, lax.shift_right_logical(nv[0] - 1, 1)), 0)
    whole = lambda i, te, nv, nx: (0, 0, 0)
    r = lax.broadcasted_iota(I32, (PERM, PERM), 0)
    c = lax.broadcasted_iota(I32, (PERM, PERM), 1)
    perm = (r == jnp.where(c < PERM // 2, 2 * c, 2 * (c - PERM // 2) + 1)).astype(BF16)
    return pl.pallas_call(
        _ffn_kernel,
        grid_spec=pltpu.PrefetchScalarGridSpec(
            num_scalar_prefetch=3,
            grid=(n_tiles // 2,),
            in_specs=[pl.BlockSpec((PANELS, 2 * FFN_ROWS, LANES), pair),
                      pl.BlockSpec(memory_space=pl.ANY),
                      pl.BlockSpec((N_EXPERTS, 1, 2 * D_FF), whole),
                      pl.BlockSpec(memory_space=pl.ANY),
                      pl.BlockSpec((N_EXPERTS, 1, d), whole),
                      pl.BlockSpec((PERM, PERM), lambda i, te, nv, nx: (0, 0))],
            out_specs=pl.BlockSpec((PANELS, 2 * FFN_ROWS, LANES), pair),
            scratch_shapes=[pltpu.VMEM((d, 2 * D_FF), F32), pltpu.VMEM((D_FF, d), F32),
                            pltpu.VMEM((d, D_FF), BF16), pltpu.VMEM((d, D_FF), BF16), pltpu.VMEM((D_FF, d), BF16),
                            pltpu.VMEM((1, D_FF), F32), pltpu.VMEM((1, D_FF), F32),
                            pltpu.SemaphoreType.DMA((2,))],
        ),
        out_shape=jax.ShapeDtypeStruct(xs.shape, U32),
        input_output_aliases={3: 0},
        compiler_params=_cparams(1, FFN_VMEM_BYTES),
        name="ffn",
    )(te, nv, nx, xs, w_gate_up, bgu, w_down, bd, perm)


def _combine_kernel(tab_ref, x1_ref, ls_ref, w_ref, mod_ref, ys_ref, o4_hbm, ybuf, ob_buf, sems, osems, *, n_blk):
    jj = pl.program_id(1)
    blk = pl.program_id(0) * pl.num_programs(1) + jj
    slot = blk % 2
    nc = x1_ref.shape[2]
    tt = POST_POS * nc
    d = x1_ref.shape[3]
    local = ybuf.shape[2]

    def chunk_copy(sl, slot_row, local_row, n):
        return pltpu.make_async_copy(ys_ref.at[:, pl.ds(slot_row, n), :], ybuf.at[sl, :, pl.ds(local_row, n), :],
                                     sems.at[sl])

    @pl.when(blk == 0)
    def _():
        ybuf[...] = jnp.zeros_like(ybuf)
        _for_block_chunks(tab_ref, 0, lambda s, l, n, p: chunk_copy(0, s, l, n).start(priority=p))

    @pl.when(blk + 1 < n_blk)
    def _():
        _for_block_chunks(tab_ref, blk + 1, lambda s, l, n, p: chunk_copy(1 - slot, s, l, n).start(priority=p))

    _for_block_chunks(tab_ref, blk, lambda s, l, n, p: chunk_copy(slot, s, l, n).wait())

    to_cols = lambda a: jnp.concatenate([a, jnp.zeros_like(a)], axis=0).T
    ls_c = to_cols(ls_ref[...].astype(F32))
    w_c = to_cols(w_ref[...])
    r = lax.broadcasted_iota(I32, (tt, local), 1).astype(F32)
    wm = jnp.zeros((tt, local), F32)
    for k in range(TOP_K):
        wm = wm + jnp.where(r == ls_c[:, k:k + 1], w_c[:, k:k + 1], 0.0)
    wm = wm.astype(BF16)
    acc = jnp.dot(wm, _unpack_panels([ybuf[slot, pb] for pb in range(PANELS)]), preferred_element_type=F32)
    out = x1_ref[0].reshape(tt, d) + mod_ref[0, 5:6, :] * acc

    def out_copies(sl, b_, j_):
        return [pltpu.make_async_copy(ob_buf.at[sl, il], o4_hbm.at[b_, :, POST_POS * j_ + il, :], osems.at[sl])
                for il in range(POST_POS)]

    @pl.when(blk >= 2)
    def _():
        for cp in out_copies(slot, 0, 0):
            cp.wait()

    for il in range(POST_POS):
        ob_buf[slot, il] = out[il * nc:(il + 1) * nc]
    for cp in out_copies(slot, pl.program_id(0), jj):
        cp.start()

    @pl.when(blk == n_blk - 1)
    def _():
        for cp in out_copies(slot, 0, 0):
            cp.wait()
        if n_blk > 1:
            for cp in out_copies(1 - slot, 0, 0):
                cp.wait()


def _combine(tab, x1, ls, wts, mod, ys):
    b, _, nc, d = x1.shape
    s = SSM_CHUNK * nc
    tt = POST_POS * nc
    nt = SSM_CHUNK // POST_POS
    o4 = pl.pallas_call(
        functools.partial(_combine_kernel, n_blk=b * nt),
        grid_spec=pltpu.PrefetchScalarGridSpec(
            num_scalar_prefetch=1,
            grid=(b, nt),
            in_specs=[pl.BlockSpec((1, POST_POS, nc, d), lambda bi, j, *_: (bi, j, 0, 0)),
                      pl.BlockSpec((TOP_K, tt), lambda bi, j, *_: (0, bi * nt + j)),
                      pl.BlockSpec((TOP_K, tt), lambda bi, j, *_: (0, bi * nt + j)),
                      pl.BlockSpec((1, 6, d), lambda bi, j, *_: (bi, 0, 0)),
                      pl.BlockSpec(memory_space=pl.ANY)],
            out_specs=pl.BlockSpec(memory_space=pl.ANY),
            scratch_shapes=[pltpu.VMEM((2, PANELS, _local_rows(tt), LANES), U32),
                            pltpu.VMEM((2, POST_POS, nc, d), F32),
                            pltpu.SemaphoreType.DMA((2,)), pltpu.SemaphoreType.DMA((2,))],
        ),
        out_shape=jax.ShapeDtypeStruct((b, nc, SSM_CHUNK, d), F32),
        compiler_params=_cparams(2),
        name="combine",
    )(tab, x1, ls, wts, mod, ys)
    return o4.reshape(b, s, d)


def kernel(x, c, w_ada, b_ada, norm_mix, w_in, b_in, q_norm, k_norm, sinks, lam_re, lam_im, log_dt, b_re, b_im,
           c_re, c_im, d_skip, w_glu, b_glu, attn_out_norm, ssm_out_norm, w_out, norm_ffn, w_router, b_router,
           w_gate_up, b_gate_up, w_down, b_down):
    b, s, d = x.shape
    t = b * s
    depth = w_ada.shape[0]
    n_tiles = -(-(t * TOP_K + N_EXPERTS * (RUN - 1 + FFN_ROWS - 1)) // FFN_ROWS)
    n_tiles += n_tiles % 2
    n_alloc = n_tiles + 2
    for l in range(depth):
        mod = _adaln(c, w_ada[l], b_ada[l]).reshape(b, 6, d)
        q, k, v, ut = _inproj(x, mod, norm_mix[l], w_in[l], b_in[l])
        attn = _attention(q, k, v, sinks[l], q_norm[l], k_norm[l], attn_out_norm[l])
        tt, wz, wyt, cs = _ssm_params(lam_re[l], lam_im[l], log_dt[l], b_re[l], b_im[l], c_re[l], c_im[l])
        yt = _ssm(ut, tt, wz, wyt, cs, d_skip[l])
        x1, h2, eidx, wts, lrank, r0, cnt = _post(x, attn, yt, mod, w_glu[l], b_glu[l], ssm_out_norm[l], w_out[l],
                                                  norm_ffn[l], w_router[l], b_router[l])
        ls, tab, te, nv, nx, pad = _route(eidx, lrank, r0, cnt, n_tiles)
        tab = tab.reshape(-1)
        nvt = jnp.stack([nv[0, 0], jnp.int32(n_alloc)])
        xs = _dispatch(tab, pad.reshape(-1), nvt, h2.reshape(t, d), ls, n_alloc * FFN_ROWS)
        ys = _ffn(te[0, :n_tiles], nv[0, :1], nx[0, :n_tiles], xs, w_gate_up[l], b_gate_up[l][:, None, :],
                  w_down[l], b_down[l][:, None, :], n_tiles)
        x = _combine(tab, x1, ls, wts, mod, ys)
    return x
```

```python
import functools
import math

import jax
import jax.numpy as jnp
from jax import lax
from jax.experimental import pallas as pl
from jax.experimental.pallas import tpu as pltpu

F32 = jnp.float32
BF16 = jnp.bfloat16
U32 = jnp.uint32
I32 = jnp.int32

D_MODEL = 1024
HEAD_DIM = 64
N_HEADS = 8
N_KV_HEADS = 2
Q_PER_KV = N_HEADS // N_KV_HEADS
D_ATTN = N_HEADS * HEAD_DIM
D_KV = N_KV_HEADS * HEAD_DIM
D_QKV = D_ATTN + 2 * D_KV
WINDOW = 128
BLOCK = 128
D_SSM = D_MODEL - D_ATTN
SSM_GROUP = 16
N_GROUPS = D_SSM // SSM_GROUP
STATE = 64
N_EXPERTS = 32
TOP_K = 4
D_FF = D_MODEL
SWIGLU_LIMIT = 7.0
SWIGLU_ALPHA = 1.702
EPS = 1e-6
NEG_INF = -1e30

LANES = 128
SSM_CHUNK = 16
SSM_ROW = SSM_CHUNK * SSM_GROUP
N_POW = 2 * SSM_CHUNK
PANEL_COLS = 2 * LANES
PANELS = D_MODEL // PANEL_COLS

SSM_GROUPS_PER_STEP = 4
POS_PER_STEP = 8
ATTN_ROWS = 512
POST_POS = 2
POST_SUB = 4
FFN_ROWS = 256
RUN = 8
RUN_SHIFT = 3
BIG = 2 * RUN
TABW = 128
SMALL0 = 80
PERM = 256
FFN_VMEM_BYTES = 40 * 1024 * 1024
ROW_VMEM_BYTES = 48 * 1024 * 1024

HIGHEST = lax.Precision.HIGHEST
_ARB = "arbitrary"


def _cparams(n, vmem=None):
    return pltpu.CompilerParams(dimension_semantics=(_ARB,) * n, vmem_limit_bytes=vmem)


def _rms(x, axis=-1):
    return x * lax.rsqrt(jnp.mean(x * x, axis=axis, keepdims=True) + EPS)


def _pack_panel(y, exact=False):
    hi, lo = y[:, :LANES], y[:, LANES:]
    if not exact:
        hi = hi.astype(BF16).astype(F32)
        lo = lo.astype(BF16).astype(F32)
    return lax.bitcast_convert_type(hi, U32) | (lax.bitcast_convert_type(lo, U32) >> 16)


def _unpack_panels(words):
    cols = []
    for w in words:
        cols.append(lax.bitcast_convert_type(w & jnp.uint32(0xFFFF0000), F32).astype(BF16))
        cols.append(lax.bitcast_convert_type(w << 16, F32).astype(BF16))
    return jnp.concatenate(cols, axis=-1)


def _prefetch_pos_rows(x4_hbm, buf, sems, n_pos):
    bi = pl.program_id(0)
    j = pl.program_id(1)
    nj = pl.num_programs(1)
    g = bi * nj + j
    slot = g % 2

    def copies(sl, b_, j_):
        return [pltpu.make_async_copy(x4_hbm.at[b_, :, n_pos * j_ + il, :], buf.at[sl, il], sems.at[sl])
                for il in range(n_pos)]

    @pl.when(g == 0)
    def _():
        for cp in copies(0, 0, 0):
            cp.start()

    @pl.when(g + 1 < pl.num_programs(0) * nj)
    def _():
        wrap = j + 1 == nj
        for cp in copies(1 - slot, jnp.where(wrap, bi + 1, bi), jnp.where(wrap, 0, j + 1)):
            cp.start()

    for cp in copies(slot, bi, j):
        cp.wait()
    return slot


def _to_lane_blocks(dst, src):
    for kb in range(dst.shape[0]):
        dst[kb] = src[:, kb * LANES:(kb + 1) * LANES]


def _adaln_kernel(c_ref, w_ref, b_ref, o_ref):
    c = c_ref[...]
    ca = c * jax.nn.sigmoid(c)
    o_ref[...] = jnp.dot(ca, w_ref[...], preferred_element_type=F32, precision=HIGHEST) + b_ref[...]


def _adaln(c, w_ada, b_ada):
    b, d = c.shape
    n = w_ada.shape[1] // d
    return pl.pallas_call(
        _adaln_kernel,
        grid=(n,),
        in_specs=[pl.BlockSpec((b, d), lambda j: (0, 0)),
                  pl.BlockSpec((d, d), lambda j: (0, j)),
                  pl.BlockSpec((1, d), lambda j: (0, j))],
        out_specs=pl.BlockSpec((b, d), lambda j: (0, j)),
        out_shape=jax.ShapeDtypeStruct((b, n * d), F32),
        compiler_params=_cparams(1),
        name="adaln",
    )(c, w_ada, b_ada.reshape(1, -1))


def _inproj_kernel(x4_hbm, x_ref, mod_ref, g_ref, wqkv_ref, bqkv_ref, wut_ref, but_ref, q_ref, k_ref, v_ref, ut_ref,
                   xp_buf, sems):
    nc = ut_ref.shape[3]
    slot = _prefetch_pos_rows(x4_hbm, xp_buf, sems, POS_PER_STEP)
    gain = g_ref[...]
    scale = 1.0 + mod_ref[0, 1:2, :]
    shift = mod_ref[0, 0:1, :]

    def norm_mod(x):
        return (_rms(x) * gain * scale + shift).astype(BF16)

    proj = jnp.dot(norm_mod(x_ref[0]), wqkv_ref[...], preferred_element_type=F32) + bqkv_ref[...]
    q_ref[0] = proj[:, :D_ATTN].astype(BF16)
    k_ref[0] = proj[:, D_ATTN:D_ATTN + D_KV].astype(BF16)
    v_ref[0] = proj[:, D_ATTN + D_KV:].astype(BF16)

    hs = jnp.concatenate([norm_mod(xp_buf[slot, il]) for il in range(POS_PER_STEP)], axis=0)
    ut = lax.dot_general(wut_ref[...], hs, (((1,), (1,)), ((), ())), preferred_element_type=F32) + but_ref[...]
    for il in range(POS_PER_STEP):
        piece = ut[:, il * nc:(il + 1) * nc].astype(BF16)
        ut_ref[0, :, il * SSM_GROUP:(il + 1) * SSM_GROUP, :] = piece.reshape(N_GROUPS, SSM_GROUP, nc)


def _inproj(x, mod, gain, w_in, b_in):
    b, s, d = x.shape
    nc = s // SSM_CHUNK
    rows = POS_PER_STEP * nc
    row = lambda bi, j: (bi, j, 0)
    const = lambda bi, j: (0, 0)
    w_qkv = w_in[:, :D_QKV].astype(BF16)
    w_ut = w_in[:, D_QKV:].T.astype(BF16)
    return pl.pallas_call(
        _inproj_kernel,
        grid=(b, SSM_CHUNK // POS_PER_STEP),
        in_specs=[pl.BlockSpec(memory_space=pl.ANY),
                  pl.BlockSpec((1, rows, d), row),
                  pl.BlockSpec((1, 6, d), lambda bi, j: (bi, 0, 0)),
                  pl.BlockSpec((1, d), const),
                  pl.BlockSpec((d, D_QKV), const),
                  pl.BlockSpec((1, D_QKV), const),
                  pl.BlockSpec((D_SSM, d), const),
                  pl.BlockSpec((D_SSM, 1), const)],
        out_specs=[pl.BlockSpec((1, rows, D_ATTN), row),
                   pl.BlockSpec((1, rows, D_KV), row),
                   pl.BlockSpec((1, rows, D_KV), row),
                   pl.BlockSpec((1, N_GROUPS, POS_PER_STEP * SSM_GROUP, nc), lambda bi, j: (bi, 0, j, 0))],
        out_shape=[jax.ShapeDtypeStruct((b, s, D_ATTN), BF16),
                   jax.ShapeDtypeStruct((b, s, D_KV), BF16),
                   jax.ShapeDtypeStruct((b, s, D_KV), BF16),
                   jax.ShapeDtypeStruct((b, N_GROUPS, SSM_ROW, nc), BF16)],
        scratch_shapes=[pltpu.VMEM((2, POS_PER_STEP, nc, d), F32), pltpu.SemaphoreType.DMA((2,))],
        compiler_params=_cparams(2),
        name="inproj",
    )(x.reshape(b, nc, SSM_CHUNK, d), x, mod, gain.reshape(1, d), w_qkv, b_in[:D_QKV].reshape(1, D_QKV), w_ut, b_in[D_QKV:].reshape(D_SSM, 1))


def _half_norm(x, half_ones):
    sq = x * x
    hi = sq.astype(BF16)
    lo = (sq - hi.astype(F32)).astype(BF16)
    ssq = (jnp.dot(hi, half_ones, preferred_element_type=F32) + jnp.dot(lo, half_ones, preferred_element_type=F32))
    return x * lax.rsqrt(ssq * (1.0 / HEAD_DIM) + EPS)


def _attn_block(first, q, k_prev, k_cur, v_prev, v_cur, sinks_ref, qn, low, half_ones, upper, rblk):
    no_prev = jnp.where(first, NEG_INF, 0.0)
    out_blocks = []
    for hk in range(N_KV_HEADS):
        qs = []
        for j in range(Q_PER_KV // 2):
            blk = hk * (Q_PER_KV // 2) + j
            qb = _half_norm(q[:, blk * LANES:(blk + 1) * LANES], half_ones) * qn * (1.0 / math.sqrt(HEAD_DIM))
            qs.append(jnp.where(low, qb, 0.0))
            qs.append(jnp.where(low, 0.0, qb))
        qg = jnp.concatenate(qs, axis=0).astype(BF16)
        nt = (((1,), (1,)), ((), ()))
        s_prev = lax.dot_general(qg, k_prev[hk], nt, preferred_element_type=F32)
        s_cur = lax.dot_general(qg, k_cur[hk], nt, preferred_element_type=F32)
        s = jnp.where(upper, s_prev + no_prev, s_cur)
        sink = jnp.zeros((Q_PER_KV * BLOCK, 1), F32)
        for g in range(Q_PER_KV):
            sink = jnp.where(rblk == g, sinks_ref[hk * Q_PER_KV + g], sink)
        m = jnp.maximum(jnp.max(s, axis=-1, keepdims=True), sink)
        p = jnp.exp(s - m)
        den = jnp.sum(p, axis=-1, keepdims=True) + jnp.exp(sink - m)
        o = (jnp.dot(jnp.where(upper, p, 0.0).astype(BF16), v_prev[hk], preferred_element_type=F32)
             + jnp.dot(jnp.where(upper, 0.0, p).astype(BF16), v_cur[hk], preferred_element_type=F32)) / den
        for j in range(Q_PER_KV // 2):
            ev = o[(2 * j) * BLOCK:(2 * j + 1) * BLOCK]
            od = o[(2 * j + 1) * BLOCK:(2 * j + 2) * BLOCK]
            out_blocks.append(jnp.where(low, ev, od))
    return jnp.concatenate(out_blocks, axis=-1)


def _attn_kernel(sinks_ref, q_ref, k_ref, v_ref, qn_ref, kn_ref, on_ref, o_hbm, a_buf, sems, *, n_steps):
    step = pl.program_id(1)
    g = pl.program_id(0) * pl.num_programs(1) + step
    slot = g % 2
    cps = ATTN_ROWS // SSM_CHUNK
    nq = ATTN_ROWS // BLOCK

    def out_copies(sl, b_, s_):
        return [pltpu.make_async_copy(a_buf.at[sl, :, i, :], o_hbm.at[b_, i, pl.ds(s_ * cps, cps), :], sems.at[sl])
                for i in range(SSM_CHUNK)]

    @pl.when(g >= 2)
    def _():
        for cp in out_copies(slot, 0, 0):
            cp.wait()

    low = lax.broadcasted_iota(I32, (1, LANES), 1) < HEAD_DIM
    rows = Q_PER_KV * BLOCK
    upper = lax.broadcasted_iota(I32, (rows, BLOCK), 1) > lax.broadcasted_iota(I32, (rows, BLOCK), 0) % BLOCK
    rblk = lax.broadcasted_iota(I32, (rows, 1), 0) // BLOCK

    cur = pl.multiple_of(step * ATTN_ROWS, ATTN_ROWS)
    prev = pl.multiple_of(jnp.maximum(step * nq - 1, 0) * BLOCK, BLOCK)
    kall = jnp.concatenate([k_ref[0, pl.ds(prev, BLOCK), :], k_ref[0, pl.ds(cur, ATTN_ROWS), :]], axis=0).astype(F32)
    vall = jnp.concatenate([v_ref[0, pl.ds(prev, BLOCK), :], v_ref[0, pl.ds(cur, ATTN_ROWS), :]], axis=0).astype(F32)
    half_ones = (lax.broadcasted_iota(I32, (LANES, LANES), 0) // HEAD_DIM
                 == lax.broadcasted_iota(I32, (LANES, LANES), 1) // HEAD_DIM).astype(BF16)
    kall = _half_norm(kall, half_ones) * kn_ref[...]
    kswap = pltpu.roll(kall, HEAD_DIM, axis=1)
    vswap = pltpu.roll(vall, HEAD_DIM, axis=1)
    k_dup = [jnp.where(low, kall, kswap).astype(BF16), jnp.where(low, kswap, kall).astype(BF16)]
    v_dup = [jnp.where(low, vall, vswap).astype(BF16), jnp.where(low, vswap, vall).astype(BF16)]
    blk = lambda a, i: [a[hk][i * BLOCK:(i + 1) * BLOCK] for hk in range(N_KV_HEADS)]

    for qb in range(nq):
        q = q_ref[0, qb * BLOCK:(qb + 1) * BLOCK, :].astype(F32)
        attn = _attn_block((step == 0) if qb == 0 else False, q, blk(k_dup, qb), blk(k_dup, qb + 1),
                           blk(v_dup, qb), blk(v_dup, qb + 1), sinks_ref, qn_ref[...], low, half_ones, upper, rblk)
        attn = _rms(attn) * on_ref[...]
        cpb = BLOCK // SSM_CHUNK
        a_buf[slot, qb * cpb:(qb + 1) * cpb] = attn.reshape(cpb, SSM_CHUNK, D_ATTN)

    for cp in out_copies(slot, pl.program_id(0), step):
        cp.start()

    @pl.when(g == n_steps - 1)
    def _():
        for cp in out_copies(slot, 0, 0):
            cp.wait()
        if n_steps > 1:
            for cp in out_copies(1 - slot, 0, 0):
                cp.wait()


def _attention(q, k, v, sinks, q_norm, k_norm, out_norm):
    b, s, _ = q.shape
    tile2 = lambda g: jnp.tile(g.reshape(1, HEAD_DIM), (1, 2))
    cps = ATTN_ROWS // SSM_CHUNK
    return pl.pallas_call(
        functools.partial(_attn_kernel, n_steps=b * (s // ATTN_ROWS)),
        grid=(b, s // ATTN_ROWS),
        in_specs=[pl.BlockSpec(memory_space=pltpu.SMEM),
                  pl.BlockSpec((1, ATTN_ROWS, D_ATTN), lambda bi, n: (bi, n, 0)),
                  pl.BlockSpec((1, s, D_KV), lambda bi, n: (bi, 0, 0)),
                  pl.BlockSpec((1, s, D_KV), lambda bi, n: (bi, 0, 0)),
                  pl.BlockSpec((1, LANES), lambda bi, n: (0, 0)),
                  pl.BlockSpec((1, LANES), lambda bi, n: (0, 0)),
                  pl.BlockSpec((1, D_ATTN), lambda bi, n: (0, 0))],
        out_specs=pl.BlockSpec(memory_space=pl.ANY),
        out_shape=jax.ShapeDtypeStruct((b, SSM_CHUNK, s // SSM_CHUNK, D_ATTN), F32),
        scratch_shapes=[pltpu.VMEM((2, cps, SSM_CHUNK, D_ATTN), F32), pltpu.SemaphoreType.DMA((2,))],
        compiler_params=_cparams(2),
        name="attention",
    )(sinks, q, k, v, tile2(q_norm), tile2(k_norm), out_norm.reshape(1, D_ATTN))


def _cmul(ar, ai, br, bi):
    return ar * br - ai * bi, ar * bi + ai * br


def _ssm_param_kernel(*refs):
    for gi in range(refs[0].shape[0]):
        _ssm_param_group(gi, *refs)


def _ssm_param_group(gi, lam_ref, bre_ref, bim_ref, cre_ref, cim_ref, tt_ref, wz_ref, wyt_ref, cs_ref):
    f32dot = functools.partial(jnp.dot, preferred_element_type=F32, precision=HIGHEST)
    lr = lam_ref[gi, 0:1, :]
    li = lam_ref[gi, 1:2, :]
    dt = jnp.exp(lam_ref[gi, 2:3, :])
    rho = lr * dt
    th = li * dt
    imag_lane = lax.broadcasted_iota(I32, (1, LANES), 1) >= STATE

    kk = (lax.broadcasted_iota(I32, (N_POW, 1), 0) - (SSM_CHUNK - 1)).astype(F32)
    mag = jnp.exp(rho * kk)
    pw_r = mag * jnp.cos(th * kk)
    pw_i = mag * jnp.sin(th * kk)
    lb_r = pw_r[SSM_CHUNK:SSM_CHUNK + 1]
    lb_i = pw_i[SSM_CHUNK:SSM_CHUNK + 1]
    den = lr * lr + li * li
    coef_r = ((lb_r - 1.0) * lr + lb_i * li) / den
    coef_i = (lb_i * lr - (lb_r - 1.0) * li) / den

    eye = (lax.broadcasted_iota(I32, (SSM_GROUP, SSM_GROUP), 0)
           == lax.broadcasted_iota(I32, (SSM_GROUP, SSM_GROUP), 1)).astype(F32)
    lane_fold = (lax.broadcasted_iota(I32, (STATE, LANES), 1) % STATE
                 == lax.broadcasted_iota(I32, (STATE, LANES), 0)).astype(F32)

    def tile_pos(x):
        return jnp.concatenate([x] * SSM_CHUNK, axis=0)

    def power_rows(k_of_pos):
        idx = [k_of_pos(p) + (SSM_CHUNK - 1) for p in range(SSM_CHUNK)]
        rep = lambda t: jnp.concatenate([jnp.broadcast_to(t[r:r + 1], (SSM_GROUP, LANES)) for r in idx], axis=0)
        return rep(pw_r), rep(pw_i)

    def b_rows(b_ref):
        b2 = jnp.concatenate([b_ref[gi], b_ref[gi]], axis=0)
        return tile_pos(lax.dot_general(eye, b2, (((1,), (1,)), ((), ())), preferred_element_type=F32,
                                        precision=HIGHEST))

    def c_rows(c_ref):
        return tile_pos(f32dot(c_ref[gi], lane_fold))

    bbar_r, bbar_i = _cmul(coef_r, coef_i, b_rows(bre_ref), b_rows(bim_ref))
    c_r = c_rows(cre_ref)
    c_i = c_rows(cim_ref)

    a_r, a_i = _cmul(bbar_r, bbar_i, *power_rows(lambda p: -p))
    a2c = jnp.where(imag_lane, -a_i, a_r)
    m_r, m_i = _cmul(c_r, c_i, *power_rows(lambda p: p))
    bmc = jnp.where(imag_lane, m_i, m_r)
    tt = f32dot(bmc, a2c.T)
    causal = (lax.broadcasted_iota(I32, (SSM_ROW, 1), 0) // SSM_GROUP
              >= lax.broadcasted_iota(I32, (1, SSM_ROW), 1) // SSM_GROUP)
    tt_ref[gi] = jnp.where(causal, tt, 0.0).astype(BF16)

    w_r, w_i = _cmul(bbar_r, bbar_i, *power_rows(lambda p: SSM_CHUNK - 1 - p))
    wz_ref[gi, :, :LANES] = jnp.where(imag_lane, w_i, w_r).astype(BF16)
    wz_ref[gi, :, LANES:] = jnp.where(imag_lane, w_r, w_i).astype(BF16)

    y_r, y_i = _cmul(c_r, c_i, *power_rows(lambda p: p + 1))
    wyt_ref[gi] = jnp.where(imag_lane, -y_i, y_r).astype(BF16)

    cs_ref[gi, 0:1, :] = pw_r[N_POW - 1:N_POW]
    cs_ref[gi, 1:2, :] = jnp.where(imag_lane, pw_i[N_POW - 1:N_POW], -pw_i[N_POW - 1:N_POW])


def _ssm_params(lam_re, lam_im, log_dt, b_re, b_im, c_re, c_im):
    g = lam_re.shape[0]
    lam = jnp.stack([lam_re, lam_im, jnp.broadcast_to(log_dt[:, None], (g, STATE))], axis=1)
    lam = jnp.concatenate([lam, lam], axis=2)
    ng = SSM_GROUPS_PER_STEP
    blk = lambda *shape: pl.BlockSpec((ng,) + shape, lambda i: (i, 0, 0))
    return pl.pallas_call(
        _ssm_param_kernel,
        grid=(g // ng,),
        in_specs=[blk(3, LANES), blk(STATE, SSM_GROUP), blk(STATE, SSM_GROUP), blk(SSM_GROUP, STATE),
                  blk(SSM_GROUP, STATE)],
        out_specs=[blk(SSM_ROW, SSM_ROW), blk(SSM_ROW, SSM_ROW), blk(SSM_ROW, LANES), blk(2, LANES)],
        out_shape=[jax.ShapeDtypeStruct((g, SSM_ROW, SSM_ROW), BF16),
                   jax.ShapeDtypeStruct((g, SSM_ROW, SSM_ROW), BF16),
                   jax.ShapeDtypeStruct((g, SSM_ROW, LANES), BF16),
                   jax.ShapeDtypeStruct((g, 2, LANES), F32)],
        compiler_params=_cparams(1),
        name="ssm_params",
    )(lam, b_re, b_im, c_re, c_im)


def _ssm_kernel(ut_ref, tt_ref, wz_ref, wyt_ref, cs_ref, d_ref, yt_ref, z_scr, s_scr):
    batch, ng, _, nc = ut_ref.shape
    uts = [jnp.concatenate([ut_ref[b, gi] for b in range(batch)], axis=1) for gi in range(ng)]
    for gi in range(ng):
        z = lax.dot_general(uts[gi], wz_ref[gi], (((0,), (0,)), ((), ())), preferred_element_type=F32)
        _to_lane_blocks(z_scr.at[gi], z)
    c1 = [cs_ref[gi, 0:1, :] for gi in range(ng)]
    c2 = [cs_ref[gi, 1:2, :] for gi in range(ng)]

    def step(c, carry):
        rows = pl.ds(c, batch, stride=nc)
        out = []
        for gi in range(ng):
            s1, s2 = carry[gi]
            s_scr[gi, rows, :] = s1
            out.append((c1[gi] * s1 + c2[gi] * s2 + z_scr[gi, 0, rows, :],
                        c1[gi] * s2 - c2[gi] * s1 + z_scr[gi, 1, rows, :]))
        return tuple(out)

    zero = jnp.zeros((batch, LANES), F32)
    lax.fori_loop(0, nc, step, ((zero, zero),) * ng, unroll=8)
    for gi in range(ng):
        y = jnp.dot(tt_ref[gi], uts[gi], preferred_element_type=F32)
        y = y + lax.dot_general(wyt_ref[gi], s_scr[gi].astype(BF16), (((1,), (1,)), ((), ())),
                                preferred_element_type=F32)
        y = y + d_ref[gi] * uts[gi].astype(F32)
        for b in range(batch):
            yt_ref[b, gi] = y[:, b * nc:(b + 1) * nc]


def _ssm(ut, tt, wz, wyt, cs, d_skip):
    b, g, _, nc = ut.shape
    ng = SSM_GROUPS_PER_STEP
    d_col = jnp.tile(d_skip.reshape(g, 1, SSM_GROUP), (1, SSM_CHUNK, 1)).reshape(g, SSM_ROW, 1)
    blk = lambda *shape: pl.BlockSpec((ng,) + shape, lambda i: (i, 0, 0))
    act = pl.BlockSpec((b, ng, SSM_ROW, nc), lambda i: (0, i, 0, 0))
    return pl.pallas_call(
        _ssm_kernel,
        grid=(g // ng,),
        in_specs=[act, blk(SSM_ROW, SSM_ROW), blk(SSM_ROW, SSM_ROW), blk(SSM_ROW, LANES), blk(2, LANES),
                  blk(SSM_ROW, 1)],
        out_specs=act,
        out_shape=jax.ShapeDtypeStruct((b, g, SSM_ROW, nc), F32),
        scratch_shapes=[pltpu.VMEM((ng, SSM_ROW // LANES, b * nc, LANES), F32), pltpu.VMEM((ng, b * nc, LANES), F32)],
        compiler_params=_cparams(1),
        name="ssm",
    )(ut, tt, wz, wyt, cs, d_col)


def _post_kernel(x4_hbm, attn_ref, yt_ref, mod_ref, wglut_ref, bglu_ref, sn_ref, wout_ref, nf_ref, wr_ref, br_ref,
                 tri_ref, x1_ref, h2_ref, eidx_ref, wts_ref, lrank_ref, r0_ref, cnt_ref, carry_ref, xp_buf, sems):
    @pl.when((pl.program_id(0) == 0) & (pl.program_id(1) == 0))
    def _():
        carry_ref[...] = jnp.zeros_like(carry_ref)

    slot = _prefetch_pos_rows(x4_hbm, xp_buf, sems, POST_SUB * POST_POS)
    nc = attn_ref.shape[2]
    ts = POST_POS * nc
    d = x1_ref.shape[3]
    iota_e = lax.broadcasted_iota(I32, (N_EXPERTS, ts), 0).astype(F32)
    counts = []
    for sub in range(POST_SUB):
        pos = range(sub * POST_POS, (sub + 1) * POST_POS)
        lanes = slice(sub * ts, (sub + 1) * ts)
        yt = jnp.concatenate(
            [yt_ref[0, :, il * SSM_GROUP:(il + 1) * SSM_GROUP, :].reshape(D_SSM, nc) for il in pos], axis=1)
        g = jax.nn.gelu(yt)
        gate = jax.nn.sigmoid(jnp.dot(wglut_ref[...], g.astype(BF16), preferred_element_type=F32) + bglu_ref[...])
        ssm_t = _rms(g * gate, axis=0) * sn_ref[...]
        attn = attn_ref[0, sub * POST_POS:(sub + 1) * POST_POS].reshape(ts, D_ATTN)
        mixed = jnp.concatenate([attn.astype(BF16), ssm_t.T.astype(BF16)], axis=-1)
        o = jnp.dot(mixed, wout_ref[...], preferred_element_type=F32)
        x = jnp.concatenate([xp_buf[slot, il] for il in pos], axis=0)
        x1 = x + mod_ref[0, 2:3, :] * o
        x1_ref[0, sub * POST_POS:(sub + 1) * POST_POS] = x1.reshape(POST_POS, nc, d)
        h2 = _rms(x1) * nf_ref[...] * (1.0 + mod_ref[0, 4:5, :]) + mod_ref[0, 3:4, :]
        h2_ref[0, sub * POST_POS:(sub + 1) * POST_POS] = h2.astype(BF16).reshape(POST_POS, nc, d)

        logits = lax.dot_general(wr_ref[...], h2.astype(BF16), (((1,), (1,)), ((), ())),
                                 preferred_element_type=F32) + br_ref[...]
        l = logits
        idxs, vals = [], []
        for _ in range(TOP_K):
            m = jnp.max(l, axis=0, keepdims=True)
            idx = jnp.min(jnp.where(l == m, iota_e, float(N_EXPERTS)), axis=0, keepdims=True)
            idxs.append(idx)
            vals.append(m)
            l = jnp.where(iota_e == idx, -jnp.inf, l)
        es = [jnp.exp(v - vals[0]) for v in vals]
        tot = es[0] + es[1] + es[2] + es[3]
        member = jnp.zeros((N_EXPERTS, ts), F32)
        for idx in idxs:
            member = member + (iota_e == idx).astype(F32)
        before = jnp.dot(member.astype(BF16), tri_ref[...], preferred_element_type=F32)
        for k in range(TOP_K):
            eidx_ref[k:k + 1, lanes] = idxs[k].astype(I32)
            wts_ref[k:k + 1, lanes] = es[k] / tot
            lrank_ref[k:k + 1, lanes] = jnp.sum(jnp.where(iota_e == idxs[k], before, 0.0), axis=0,
                                                keepdims=True).astype(I32)
        counts.append(jnp.sum(member, axis=1, keepdims=True))

    carry = carry_ref[...]
    for sub in range(POST_SUB):
        r0_ref[sub] = carry.astype(I32)
        carry = carry + counts[sub]
    carry_ref[...] = carry
    cnt_ref[...] = carry.astype(I32)


def _post(x, attn, yt, mod, w_glu, b_glu, ssm_norm, w_out, norm_ffn, w_router, b_router):
    b, s, d = x.shape
    nc = s // SSM_CHUNK
    ts = POST_POS * nc
    npos = POST_SUB * POST_POS
    nt = SSM_CHUNK // npos
    t = b * s
    pm = lambda bi, j: (bi, j, 0, 0)
    const = lambda bi, j: (0, 0)
    tok = lambda bi, j: (0, bi * nt + j)
    tri = (lax.broadcasted_iota(I32, (ts, ts), 0) < lax.broadcasted_iota(I32, (ts, ts), 1)).astype(BF16)
    col = lambda a: a.reshape(-1, 1)
    return pl.pallas_call(
        _post_kernel,
        grid=(b, nt),
        in_specs=[pl.BlockSpec(memory_space=pl.ANY),
                  pl.BlockSpec((1, npos, nc, D_ATTN), pm),
                  pl.BlockSpec((1, N_GROUPS, npos * SSM_GROUP, nc), lambda bi, j: (bi, 0, j, 0)),
                  pl.BlockSpec((1, 6, d), lambda bi, j: (bi, 0, 0)),
                  pl.BlockSpec((D_SSM, D_SSM), const),
                  pl.BlockSpec((D_SSM, 1), const),
                  pl.BlockSpec((D_SSM, 1), const),
                  pl.BlockSpec((d, d), const),
                  pl.BlockSpec((1, d), const),
                  pl.BlockSpec((N_EXPERTS, d), const),
                  pl.BlockSpec((N_EXPERTS, 1), const),
                  pl.BlockSpec((ts, ts), const)],
        out_specs=[pl.BlockSpec((1, npos, nc, d), pm),
                   pl.BlockSpec((1, npos, nc, d), pm),
                   pl.BlockSpec((TOP_K, POST_SUB * ts), tok),
                   pl.BlockSpec((TOP_K, POST_SUB * ts), tok),
                   pl.BlockSpec((TOP_K, POST_SUB * ts), tok),
                   pl.BlockSpec((POST_SUB, N_EXPERTS, 1), lambda bi, j: (bi * nt + j, 0, 0)),
                   pl.BlockSpec((N_EXPERTS, 1), const)],
        out_shape=[jax.ShapeDtypeStruct((b, SSM_CHUNK, nc, d), F32),
                   jax.ShapeDtypeStruct((b, SSM_CHUNK, nc, d), BF16),
                   jax.ShapeDtypeStruct((TOP_K, t), I32),
                   jax.ShapeDtypeStruct((TOP_K, t), F32),
                   jax.ShapeDtypeStruct((TOP_K, t), I32),
                   jax.ShapeDtypeStruct((b * nt * POST_SUB, N_EXPERTS, 1), I32),
                   jax.ShapeDtypeStruct((N_EXPERTS, 1), I32)],
        scratch_shapes=[pltpu.VMEM((N_EXPERTS, 1), F32), pltpu.VMEM((2, npos, nc, d), F32),
                        pltpu.SemaphoreType.DMA((2,))],
        compiler_params=_cparams(2),
        name="post",
    )(x.reshape(b, nc, SSM_CHUNK, d), attn, yt, mod, w_glu.T.astype(BF16), col(b_glu), col(ssm_norm), w_out.astype(BF16),
      norm_ffn.reshape(1, -1), w_router.T.astype(BF16), col(b_router), tri)


def _route_kernel(eidx_ref, lrank_ref, r0_ref, cnt_ref, ls_ref, tab_ref, te_ref, nv_ref, nx_ref, pad_ref):
    cnt = cnt_ref[...]
    tiles = (cnt + (RUN - 1 + FFN_ROWS - 1)) // FFN_ROWS
    er = lax.broadcasted_iota(I32, (N_EXPERTS, N_EXPERTS), 0)
    ec = lax.broadcasted_iota(I32, (N_EXPERTS, N_EXPERTS), 1)
    ltri = (ec < er).astype(BF16)

    def excl_cumsum(v):
        vb = jnp.broadcast_to(v.astype(F32), (N_EXPERTS, LANES)).astype(BF16)
        return jnp.dot(ltri, vb, preferred_element_type=F32)[:, 0:1].astype(I32)

    start_t = excl_cumsum(tiles)
    end_t = start_t + tiles
    start = start_t * FFN_ROWS
    pad_ref[...] = start + cnt

    nb = r0_ref.shape[0]
    ts = eidx_ref.shape[1] // nb
    iota_e = lax.broadcasted_iota(I32, (N_EXPERTS, ts), 0)
    iota_t = lax.broadcasted_iota(I32, (N_EXPERTS, TABW), 0)
    lane = lax.broadcasted_iota(I32, (1, TABW), 1)

    def block(b, carry):
        lanes = pl.ds(pl.multiple_of(b * ts, ts), ts)
        sels = [iota_e == eidx_ref[k:k + 1, lanes] for k in range(TOP_K)]
        member = sels[0].astype(I32) + sels[1].astype(I32) + sels[2].astype(I32) + sels[3].astype(I32)
        units = lax.shift_right_logical(jnp.sum(member, axis=1, keepdims=True) + (RUN - 1), RUN_SHIFT)
        u0 = excl_cumsum(units)
        for k in range(TOP_K):
            first = jnp.sum(jnp.where(sels[k], u0, 0), axis=0, keepdims=True)
            ls_ref[k:k + 1, lanes] = first * RUN + lrank_ref[k:k + 1, lanes]
        n_big = lax.shift_right_logical(units, 1)
        n_small = units & 1
        slot0 = start + r0_ref[b]

        def chunk_rows(idx, counts):
            c0 = excl_cumsum(counts)
            sel = iota_t == jnp.sum((idx >= c0 + counts).astype(I32), axis=0, keepdims=True)
            pick = lambda v: jnp.sum(jnp.where(sel, v, 0), axis=0, keepdims=True)
            j = idx - pick(c0)
            return pick(slot0), pick(u0), j, pick(n_big), idx < jnp.max(c0 + counts, axis=0, keepdims=True)

        s_b, u_b, j_b, _, ok_b = chunk_rows(lane, n_big)
        s_s, u_s, _, nb_s, ok_s = chunk_rows(lane - SMALL0, n_small)
        small = lane >= SMALL0
        slot = jnp.where(small, s_s + nb_s * BIG, s_b + j_b * BIG)
        local = jnp.where(small, (u_s + 2 * nb_s) * RUN, (u_b + 2 * j_b) * RUN)
        ok = (small & ok_s) | (jnp.logical_not(small) & ok_b)
        counts = jnp.where(lane == TABW - 2, jnp.sum(n_big, axis=0, keepdims=True),
                           jnp.sum(n_small, axis=0, keepdims=True))
        tab_ref[b, 0:1, :] = jnp.where(lane >= TABW - 2, counts, jnp.where(ok, slot, -1))
        tab_ref[b, 1:2, :] = jnp.where(ok, local, 0)
        return carry

    lax.fori_loop(0, nb, block, 0, unroll=2)

    nv = jnp.max(end_t, axis=0, keepdims=True)
    width = te_ref.shape[1]
    ti = jnp.minimum(lax.broadcasted_iota(I32, (N_EXPERTS, width), 1), nv - 1)
    te = jnp.minimum(jnp.sum((ti >= end_t).astype(I32), axis=0, keepdims=True), N_EXPERTS - 1)
    te_ref[...] = te
    nv_ref[...] = jnp.broadcast_to(nv, nv_ref.shape)
    ie = lax.broadcasted_iota(I32, (N_EXPERTS, width), 0)
    own_end = jnp.sum(jnp.where(ie == te, end_t, 0), axis=0, keepdims=True)
    nxt = jnp.minimum(jnp.sum((own_end >= end_t).astype(I32), axis=0, keepdims=True), N_EXPERTS - 1)
    nx_ref[...] = jnp.where(own_end < nv, nxt, -1)


def _route(eidx, lrank, r0, cnt, n_tiles):
    t = eidx.shape[1]
    nb = r0.shape[0]
    width = -(-n_tiles // LANES) * LANES
    return pl.pallas_call(
        _route_kernel,
        out_shape=[jax.ShapeDtypeStruct((TOP_K, t), I32),
                   jax.ShapeDtypeStruct((nb, 2, TABW), I32),
                   jax.ShapeDtypeStruct((1, width), I32),
                   jax.ShapeDtypeStruct((1, LANES), I32),
                   jax.ShapeDtypeStruct((1, width), I32),
                   jax.ShapeDtypeStruct((N_EXPERTS, 1), I32)],
        name="route",
    )(eidx, lrank, r0, cnt)


def _for_chunk_pairs(n, fn):
    def body(i, carry):
        fn(2 * i, 0)

        @pl.when(2 * i + 1 < n)
        def _():
            fn(2 * i + 1, 1)
        return carry
    lax.fori_loop(0, lax.shift_right_logical(n + 1, 1), body, 0)


def _for_block_chunks(tab_ref, blk, fn):
    base = blk * (2 * TABW)
    for first, count_lane, n_rows in ((0, TABW - 2, BIG), (SMALL0, TABW - 1, RUN)):
        def visit(c, parity, first=first, n_rows=n_rows):
            fn(tab_ref[base + first + c], pl.multiple_of(tab_ref[base + TABW + first + c], RUN), n_rows, parity)
        _for_chunk_pairs(tab_ref[base + count_lane], visit)


def _local_rows(ts):
    return ts * TOP_K + N_EXPERTS * RUN


def _dispatch_kernel(tab_ref, pad_ref, nvt_ref, h_ref, ls_ref, xs_ref, buf, zbuf, sems, zsem):
    b = pl.program_id(0)
    slot = b % 2
    ts = h_ref.shape[0]
    local = buf.shape[2]

    def chunk_copy(sl, slot_row, local_row, n):
        return pltpu.make_async_copy(buf.at[sl, :, pl.ds(local_row, n), :], xs_ref.at[:, pl.ds(slot_row, n), :],
                                     sems.at[sl])

    @pl.when(b == 0)
    def _():
        zbuf[...] = jnp.zeros_like(zbuf)
        zrows = zbuf.shape[1]
        zero = lambda row: pltpu.make_async_copy(zbuf, xs_ref.at[:, pl.ds(row, zrows), :], zsem)
        for phase in range(3):
            for e in range(phase, N_EXPERTS, 3):
                zero(pad_ref[e]).start()
            for e in range(phase, N_EXPERTS, 3):
                zero(pad_ref[e]).wait()
        ztile = lambda i: pltpu.make_async_copy(zbuf.at[:, pl.ds(0, FFN_ROWS), :],
                                                xs_ref.at[:, pl.ds((nvt_ref[0] + i) * FFN_ROWS, FFN_ROWS), :], zsem)

        def tail_start(i, carry):
            ztile(i).start()
            return carry

        def tail_wait(i, carry):
            ztile(i).wait()
            return carry
        lax.fori_loop(0, nvt_ref[1] - nvt_ref[0], tail_start, 0)
        lax.fori_loop(0, nvt_ref[1] - nvt_ref[0], tail_wait, 0)

    r = lax.broadcasted_iota(I32, (local, ts), 0)
    hit = (r == ls_ref[0:1, :]) | (r == ls_ref[1:2, :]) | (r == ls_ref[2:3, :]) | (r == ls_ref[3:4, :])
    hit = hit.astype(BF16)
    for pb in range(PANELS):
        srt = jnp.dot(hit, h_ref[:, pb * PANEL_COLS:(pb + 1) * PANEL_COLS], preferred_element_type=F32)
        buf[slot, pb] = _pack_panel(srt, exact=True)

    @pl.when(b > 0)
    def _():
        _for_block_chunks(tab_ref, b - 1, lambda s, l, n, p: chunk_copy(1 - slot, s, l, n).wait())

    _for_block_chunks(tab_ref, b, lambda s, l, n, p: chunk_copy(slot, s, l, n).start(priority=p))

    @pl.when(b == pl.num_programs(0) - 1)
    def _():
        _for_block_chunks(tab_ref, b, lambda s, l, n, p: chunk_copy(slot, s, l, n).wait())


def _dispatch(tab, pad, nvt, h2, ls, n_rows):
    t, d = h2.shape
    nb = tab.shape[0] // (2 * TABW)
    ts = t // nb
    return pl.pallas_call(
        _dispatch_kernel,
        grid_spec=pltpu.PrefetchScalarGridSpec(
            num_scalar_prefetch=3,
            grid=(nb,),
            in_specs=[pl.BlockSpec((ts, d), lambda i, *_: (i, 0)),
                      pl.BlockSpec((TOP_K, ts), lambda i, *_: (0, i))],
            out_specs=pl.BlockSpec(memory_space=pl.ANY),
            scratch_shapes=[pltpu.VMEM((2, PANELS, _local_rows(ts), LANES), U32),
                            pltpu.VMEM((PANELS, FFN_ROWS + RUN, LANES), U32),
                            pltpu.SemaphoreType.DMA((2,)), pltpu.SemaphoreType.DMA],
        ),
        out_shape=jax.ShapeDtypeStruct((PANELS, n_rows, LANES), U32),
        compiler_params=_cparams(1, ROW_VMEM_BYTES),
        name="dispatch",
    )(tab, pad, nvt, h2, ls)


def _ffn_kernel(te_ref, nv_ref, nx_ref, xs_ref, wgu_hbm, bgu_ref, wd_hbm, bd_ref, perm_ref, ys_ref,
                wgu_stage, wd_stage, wg_scr, wu_scr, wd_scr, bg_scr, bu_scr, sems):
    p = pl.program_id(0)
    t0 = 2 * p
    t1 = t0 + 1
    e0 = te_ref[t0]
    e1 = te_ref[t1]
    v0 = t0 < nv_ref[0]
    v1 = t1 < nv_ref[0]
    new0 = (p == 0) | (e0 != te_ref[jnp.maximum(t0 - 1, 0)])
    same = v1 & (e1 == e0)

    def stage_copies(e):
        return (pltpu.make_async_copy(wgu_hbm.at[e], wgu_stage, sems.at[0]),
                pltpu.make_async_copy(wd_hbm.at[e], wd_stage, sems.at[1]))

    def load_expert(t, first):
        e = te_ref[t]
        if first:
            @pl.when(p == 0)
            def _():
                for cp in stage_copies(e):
                    cp.start()

        for cp in stage_copies(e):
            cp.wait()
        bias = bgu_ref[e]
        for c in range(2 * D_FF // PERM):
            cols = slice(c * PERM, (c + 1) * PERM)
            half = slice(c * (PERM // 2), (c + 1) * (PERM // 2))
            w = wgu_stage[:, cols].astype(BF16)
            pw = jnp.dot(w, perm_ref[...], preferred_element_type=F32).astype(BF16)
            wg_scr[:, half] = pw[:, :PERM // 2]
            wu_scr[:, half] = pw[:, PERM // 2:]
            b1 = bias[:, cols].astype(BF16)
            r1 = bias[:, cols] - b1.astype(F32)
            b2 = r1.astype(BF16)
            b3 = (r1 - b2.astype(F32)).astype(BF16)
            terms = jnp.concatenate([b1, b2, b3, jnp.zeros((5, PERM), BF16)], axis=0)
            pb = jnp.sum(jnp.dot(terms, perm_ref[...], preferred_element_type=F32), axis=0, keepdims=True)
            bg_scr[:, half] = pb[:, :PERM // 2]
            bu_scr[:, half] = pb[:, PERM // 2:]
        wd_scr[...] = wd_stage[...].astype(BF16)

        @pl.when(nx_ref[t] >= 0)
        def _():
            for cp in stage_copies(nx_ref[t]):
                cp.start()

    def run(lo, n, e):
        x = _unpack_panels([xs_ref[pb, lo:lo + n, :] for pb in range(PANELS)])
        gate = jnp.dot(x, wg_scr[...], preferred_element_type=F32) + bg_scr[...]
        up = jnp.dot(x, wu_scr[...], preferred_element_type=F32) + bu_scr[...]
        gate = jnp.minimum(gate, SWIGLU_LIMIT)
        up = jnp.clip(up, -SWIGLU_LIMIT, SWIGLU_LIMIT)
        act = ((up + 1.0) * (gate * jax.nn.sigmoid(SWIGLU_ALPHA * gate))).astype(BF16)
        bd = bd_ref[e]
        for pb in range(PANELS):
            cols = slice(pb * PANEL_COLS, (pb + 1) * PANEL_COLS)
            y = jnp.dot(act, wd_scr[:, cols], preferred_element_type=F32) + bd[:, cols]
            ys_ref[pb, lo:lo + n, :] = _pack_panel(y)

    @pl.when(v0 & new0)
    def _():
        load_expert(t0, True)

    @pl.when(same)
    def _():
        run(0, 2 * FFN_ROWS, e0)

    @pl.when(v0 & jnp.logical_not(same))
    def _():
        run(0, FFN_ROWS, e0)

    @pl.when(v1 & jnp.logical_not(same))
    def _():
        load_expert(t1, False)
        run(FFN_ROWS, FFN_ROWS, e1)

    @pl.when(v0 & jnp.logical_not(v1))
    def _():
        ys_ref[:, FFN_ROWS:, :] = xs_ref[:, FFN_ROWS:, :]


def _ffn(te, nv, nx, xs, w_gate_up, bgu, w_down, bd, n_tiles):
    d = D_MODEL
    pair = lambda i, te, nv, nx: (0, jnp.minimum(i, lax.shift_right_logical(nv[0] - 1, 1)), 0)
    whole = lambda i, te, nv, nx: (0, 0, 0)
    r = lax.broadcasted_iota(I32, (PERM, PERM), 0)
    c = lax.broadcasted_iota(I32, (PERM, PERM), 1)
    perm = (r == jnp.where(c < PERM // 2, 2 * c, 2 * (c - PERM // 2) + 1)).astype(BF16)
    return pl.pallas_call(
        _ffn_kernel,
        grid_spec=pltpu.PrefetchScalarGridSpec(
            num_scalar_prefetch=3,
            grid=(n_tiles // 2,),
            in_specs=[pl.BlockSpec((PANELS, 2 * FFN_ROWS, LANES), pair),
                      pl.BlockSpec(memory_space=pl.ANY),
                      pl.BlockSpec((N_EXPERTS, 1, 2 * D_FF), whole),
                      pl.BlockSpec(memory_space=pl.ANY),
                      pl.BlockSpec((N_EXPERTS, 1, d), whole),
                      pl.BlockSpec((PERM, PERM), lambda i, te, nv, nx: (0, 0))],
            out_specs=pl.BlockSpec((PANELS, 2 * FFN_ROWS, LANES), pair),
            scratch_shapes=[pltpu.VMEM((d, 2 * D_FF), F32), pltpu.VMEM((D_FF, d), F32),
                            pltpu.VMEM((d, D_FF), BF16), pltpu.VMEM((d, D_FF), BF16), pltpu.VMEM((D_FF, d), BF16),
                            pltpu.VMEM((1, D_FF), F32), pltpu.VMEM((1, D_FF), F32),
                            pltpu.SemaphoreType.DMA((2,))],
        ),
        out_shape=jax.ShapeDtypeStruct(xs.shape, U32),
        input_output_aliases={3: 0},
        compiler_params=_cparams(1, FFN_VMEM_BYTES),
        name="ffn",
    )(te, nv, nx, xs, w_gate_up, bgu, w_down, bd, perm)


def _combine_kernel(tab_ref, x1_ref, ls_ref, w_ref, mod_ref, ys_ref, o4_hbm, ybuf, ob_buf, sems, osems, *, n_blk):
    jj = pl.program_id(1)
    blk = pl.program_id(0) * pl.num_programs(1) + jj
    slot = blk % 2
    nc = x1_ref.shape[2]
    tt = POST_POS * nc
    d = x1_ref.shape[3]
    local = ybuf.shape[2]

    def chunk_copy(sl, slot_row, local_row, n):
        return pltpu.make_async_copy(ys_ref.at[:, pl.ds(slot_row, n), :], ybuf.at[sl, :, pl.ds(local_row, n), :],
                                     sems.at[sl])

    @pl.when(blk == 0)
    def _():
        ybuf[...] = jnp.zeros_like(ybuf)
        _for_block_chunks(tab_ref, 0, lambda s, l, n, p: chunk_copy(0, s, l, n).start(priority=p))

    @pl.when(blk + 1 < n_blk)
    def _():
        _for_block_chunks(tab_ref, blk + 1, lambda s, l, n, p: chunk_copy(1 - slot, s, l, n).start(priority=p))

    _for_block_chunks(tab_ref, blk, lambda s, l, n, p: chunk_copy(slot, s, l, n).wait())

    to_cols = lambda a: jnp.concatenate([a, jnp.zeros_like(a)], axis=0).T
    ls_c = to_cols(ls_ref[...].astype(F32))
    w_c = to_cols(w_ref[...])
    r = lax.broadcasted_iota(I32, (tt, local), 1).astype(F32)
    wm = jnp.zeros((tt, local), F32)
    for k in range(TOP_K):
        wm = wm + jnp.where(r == ls_c[:, k:k + 1], w_c[:, k:k + 1], 0.0)
    wm = wm.astype(BF16)
    acc = jnp.dot(wm, _unpack_panels([ybuf[slot, pb] for pb in range(PANELS)]), preferred_element_type=F32)
    out = x1_ref[0].reshape(tt, d) + mod_ref[0, 5:6, :] * acc

    def out_copies(sl, b_, j_):
        return [pltpu.make_async_copy(ob_buf.at[sl, il], o4_hbm.at[b_, :, POST_POS * j_ + il, :], osems.at[sl])
                for il in range(POST_POS)]

    @pl.when(blk >= 2)
    def _():
        for cp in out_copies(slot, 0, 0):
            cp.wait()

    for il in range(POST_POS):
        ob_buf[slot, il] = out[il * nc:(il + 1) * nc]
    for cp in out_copies(slot, pl.program_id(0), jj):
        cp.start()

    @pl.when(blk == n_blk - 1)
    def _():
        for cp in out_copies(slot, 0, 0):
            cp.wait()
        if n_blk > 1:
            for cp in out_copies(1 - slot, 0, 0):
                cp.wait()


def _combine(tab, x1, ls, wts, mod, ys):
    b, _, nc, d = x1.shape
    s = SSM_CHUNK * nc
    tt = POST_POS * nc
    nt = SSM_CHUNK // POST_POS
    o4 = pl.pallas_call(
        functools.partial(_combine_kernel, n_blk=b * nt),
        grid_spec=pltpu.PrefetchScalarGridSpec(
            num_scalar_prefetch=1,
            grid=(b, nt),
            in_specs=[pl.BlockSpec((1, POST_POS, nc, d), lambda bi, j, *_: (bi, j, 0, 0)),
                      pl.BlockSpec((TOP_K, tt), lambda bi, j, *_: (0, bi * nt + j)),
                      pl.BlockSpec((TOP_K, tt), lambda bi, j, *_: (0, bi * nt + j)),
                      pl.BlockSpec((1, 6, d), lambda bi, j, *_: (bi, 0, 0)),
                      pl.BlockSpec(memory_space=pl.ANY)],
            out_specs=pl.BlockSpec(memory_space=pl.ANY),
            scratch_shapes=[pltpu.VMEM((2, PANELS, _local_rows(tt), LANES), U32),
                            pltpu.VMEM((2, POST_POS, nc, d), F32),
                            pltpu.SemaphoreType.DMA((2,)), pltpu.SemaphoreType.DMA((2,))],
        ),
        out_shape=jax.ShapeDtypeStruct((b, nc, SSM_CHUNK, d), F32),
        compiler_params=_cparams(2),
        name="combine",
    )(tab, x1, ls, wts, mod, ys)
    return o4.reshape(b, s, d)


def kernel(x, c, w_ada, b_ada, norm_mix, w_in, b_in, q_norm, k_norm, sinks, lam_re, lam_im, log_dt, b_re, b_im,
           c_re, c_im, d_skip, w_glu, b_glu, attn_out_norm, ssm_out_norm, w_out, norm_ffn, w_router, b_router,
           w_gate_up, b_gate_up, w_down, b_down):
    b, s, d = x.shape
    t = b * s
    depth = w_ada.shape[0]
    n_tiles = -(-(t * TOP_K + N_EXPERTS * (RUN - 1 + FFN_ROWS - 1)) // FFN_ROWS)
    n_tiles += n_tiles % 2
    n_alloc = n_tiles + 2
    for l in range(depth):
        mod = _adaln(c, w_ada[l], b_ada[l]).reshape(b, 6, d)
        q, k, v, ut = _inproj(x, mod, norm_mix[l], w_in[l], b_in[l])
        attn = _attention(q, k, v, sinks[l], q_norm[l], k_norm[l], attn_out_norm[l])
        tt, wz, wyt, cs = _ssm_params(lam_re[l], lam_im[l], log_dt[l], b_re[l], b_im[l], c_re[l], c_im[l])
        yt = _ssm(ut, tt, wz, wyt, cs, d_skip[l])
        x1, h2, eidx, wts, lrank, r0, cnt = _post(x, attn, yt, mod, w_glu[l], b_glu[l], ssm_out_norm[l], w_out[l],
                                                  norm_ffn[l], w_router[l], b_router[l])
        ls, tab, te, nv, nx, pad = _route(eidx, lrank, r0, cnt, n_tiles)
        tab = tab.reshape(-1)
        nvt = jnp.stack([nv[0, 0], jnp.int32(n_alloc)])
        xs = _dispatch(tab, pad.reshape(-1), nvt, h2.reshape(t, d), ls, n_alloc * FFN_ROWS)
        ys = _ffn(te[0, :n_tiles], nv[0, :1], nx[0, :n_tiles], xs, w_gate_up[l], b_gate_up[l][:, None, :],
                  w_down[l], b_down[l][:, None, :], n_tiles)
        x = _combine(tab, x1, ls, wts, mod, ys)
    return x
```

```python
import functools
import math

import jax
import jax.numpy as jnp
from jax import lax
from jax.experimental import pallas as pl
from jax.experimental.pallas import tpu as pltpu

F32 = jnp.float32
BF16 = jnp.bfloat16
U32 = jnp.uint32
I32 = jnp.int32

D_MODEL = 1024
HEAD_DIM = 64
N_HEADS = 8
N_KV_HEADS = 2
Q_PER_KV = N_HEADS // N_KV_HEADS
D_ATTN = N_HEADS * HEAD_DIM
D_KV = N_KV_HEADS * HEAD_DIM
D_QKV = D_ATTN + 2 * D_KV
WINDOW = 128
BLOCK = 128
D_SSM = D_MODEL - D_ATTN
SSM_GROUP = 16
N_GROUPS = D_SSM // SSM_GROUP
STATE = 64
N_EXPERTS = 32
TOP_K = 4
D_FF = D_MODEL
SWIGLU_LIMIT = 7.0
SWIGLU_ALPHA = 1.702
EPS = 1e-6
NEG_INF = -1e30

LANES = 128
SSM_CHUNK = 16
SSM_ROW = SSM_CHUNK * SSM_GROUP
N_POW = 2 * SSM_CHUNK
PANEL_COLS = 2 * LANES
PANELS = D_MODEL // PANEL_COLS

SSM_GROUPS_PER_STEP = 4
POS_PER_STEP = 8
ATTN_ROWS = 512
POST_POS = 2
POST_SUB = 4
COMB_SUB = 2
FFN_ROWS = 256
RUN = 8
RUN_SHIFT = 3
BIG = 2 * RUN
TABW = 128
SMALL0 = 80
PERM = 256
FFN_VMEM_BYTES = 40 * 1024 * 1024
ROW_VMEM_BYTES = 48 * 1024 * 1024

HIGHEST = lax.Precision.HIGHEST
_ARB = "arbitrary"


def _cparams(n, vmem=None):
    return pltpu.CompilerParams(dimension_semantics=(_ARB,) * n, vmem_limit_bytes=vmem)


def _rms(x, axis=-1):
    return x * lax.rsqrt(jnp.mean(x * x, axis=axis, keepdims=True) + EPS)


def _pack_panel(y, exact=False):
    hi, lo = y[:, :LANES], y[:, LANES:]
    if not exact:
        hi = hi.astype(BF16).astype(F32)
        lo = lo.astype(BF16).astype(F32)
    return lax.bitcast_convert_type(hi, U32) | (lax.bitcast_convert_type(lo, U32) >> 16)


def _unpack_panels(words):
    cols = []
    for w in words:
        cols.append(lax.bitcast_convert_type(w & jnp.uint32(0xFFFF0000), F32).astype(BF16))
        cols.append(lax.bitcast_convert_type(w << 16, F32).astype(BF16))
    return jnp.concatenate(cols, axis=-1)


def _prefetch_pos_rows(x4_hbm, buf, sems, n_pos):
    bi = pl.program_id(0)
    j = pl.program_id(1)
    nj = pl.num_programs(1)
    g = bi * nj + j
    slot = g % 2

    def copies(sl, b_, j_):
        return [pltpu.make_async_copy(x4_hbm.at[b_, :, n_pos * j_ + il, :], buf.at[sl, il], sems.at[sl])
                for il in range(n_pos)]

    @pl.when(g == 0)
    def _():
        for cp in copies(0, 0, 0):
            cp.start()

    @pl.when(g + 1 < pl.num_programs(0) * nj)
    def _():
        wrap = j + 1 == nj
        for cp in copies(1 - slot, jnp.where(wrap, bi + 1, bi), jnp.where(wrap, 0, j + 1)):
            cp.start()

    for cp in copies(slot, bi, j):
        cp.wait()
    return slot


def _to_lane_blocks(dst, src):
    for kb in range(dst.shape[0]):
        dst[kb] = src[:, kb * LANES:(kb + 1) * LANES]


def _adaln_kernel(c_ref, w_ref, b_ref, o_ref):
    c = c_ref[...]
    ca = c * jax.nn.sigmoid(c)
    o_ref[...] = jnp.dot(ca, w_ref[...], preferred_element_type=F32, precision=HIGHEST) + b_ref[...]


def _adaln(c, w_ada, b_ada):
    b, d = c.shape
    n = w_ada.shape[1] // d
    return pl.pallas_call(
        _adaln_kernel,
        grid=(n,),
        in_specs=[pl.BlockSpec((b, d), lambda j: (0, 0)),
                  pl.BlockSpec((d, d), lambda j: (0, j)),
                  pl.BlockSpec((1, d), lambda j: (0, j))],
        out_specs=pl.BlockSpec((b, d), lambda j: (0, j)),
        out_shape=jax.ShapeDtypeStruct((b, n * d), F32),
        compiler_params=_cparams(1),
        name="adaln",
    )(c, w_ada, b_ada.reshape(1, -1))


def _inproj_kernel(x4_hbm, x_ref, mod_ref, g_ref, wqkv_ref, bqkv_ref, wut_ref, but_ref, q_ref, k_ref, v_ref, ut_ref,
                   xp_buf, sems):
    nc = ut_ref.shape[3]
    slot = _prefetch_pos_rows(x4_hbm, xp_buf, sems, POS_PER_STEP)
    gain = g_ref[...]
    scale = 1.0 + mod_ref[0, 1:2, :]
    shift = mod_ref[0, 0:1, :]

    def norm_mod(x):
        return (_rms(x) * gain * scale + shift).astype(BF16)

    proj = jnp.dot(norm_mod(x_ref[0]), wqkv_ref[...], preferred_element_type=F32) + bqkv_ref[...]
    q_ref[0] = proj[:, :D_ATTN].astype(BF16)
    k_ref[0] = proj[:, D_ATTN:D_ATTN + D_KV].astype(BF16)
    v_ref[0] = proj[:, D_ATTN + D_KV:].astype(BF16)

    hs = jnp.concatenate([norm_mod(xp_buf[slot, il]) for il in range(POS_PER_STEP)], axis=0)
    ut = lax.dot_general(wut_ref[...], hs, (((1,), (1,)), ((), ())), preferred_element_type=F32) + but_ref[...]
    for il in range(POS_PER_STEP):
        piece = ut[:, il * nc:(il + 1) * nc].astype(BF16)
        ut_ref[0, :, il * SSM_GROUP:(il + 1) * SSM_GROUP, :] = piece.reshape(N_GROUPS, SSM_GROUP, nc)


def _inproj(x, mod, gain, w_in, b_in):
    b, s, d = x.shape
    nc = s // SSM_CHUNK
    rows = POS_PER_STEP * nc
    row = lambda bi, j: (bi, j, 0)
    const = lambda bi, j: (0, 0)
    w_qkv = w_in[:, :D_QKV].astype(BF16)
    w_ut = w_in[:, D_QKV:].T.astype(BF16)
    return pl.pallas_call(
        _inproj_kernel,
        grid=(b, SSM_CHUNK // POS_PER_STEP),
        in_specs=[pl.BlockSpec(memory_space=pl.ANY),
                  pl.BlockSpec((1, rows, d), row),
                  pl.BlockSpec((1, 6, d), lambda bi, j: (bi, 0, 0)),
                  pl.BlockSpec((1, d), const),
                  pl.BlockSpec((d, D_QKV), const),
                  pl.BlockSpec((1, D_QKV), const),
                  pl.BlockSpec((D_SSM, d), const),
                  pl.BlockSpec((D_SSM, 1), const)],
        out_specs=[pl.BlockSpec((1, rows, D_ATTN), row),
                   pl.BlockSpec((1, rows, D_KV), row),
                   pl.BlockSpec((1, rows, D_KV), row),
                   pl.BlockSpec((1, N_GROUPS, POS_PER_STEP * SSM_GROUP, nc), lambda bi, j: (bi, 0, j, 0))],
        out_shape=[jax.ShapeDtypeStruct((b, s, D_ATTN), BF16),
                   jax.ShapeDtypeStruct((b, s, D_KV), BF16),
                   jax.ShapeDtypeStruct((b, s, D_KV), BF16),
                   jax.ShapeDtypeStruct((b, N_GROUPS, SSM_ROW, nc), BF16)],
        scratch_shapes=[pltpu.VMEM((2, POS_PER_STEP, nc, d), F32), pltpu.SemaphoreType.DMA((2,))],
        compiler_params=_cparams(2),
        name="inproj",
    )(x.reshape(b, nc, SSM_CHUNK, d), x, mod, gain.reshape(1, d), w_qkv, b_in[:D_QKV].reshape(1, D_QKV), w_ut, b_in[D_QKV:].reshape(D_SSM, 1))


def _half_norm(x, low):
    sq = x * x
    s_lo = jnp.sum(jnp.where(low, sq, 0.0), axis=-1, keepdims=True)
    s_hi = jnp.sum(sq, axis=-1, keepdims=True) - s_lo
    inv = 1.0 / HEAD_DIM
    scale = jnp.where(low, lax.rsqrt(s_lo * inv + EPS), lax.rsqrt(s_hi * inv + EPS))
    return x * scale


def _attn_block(first, q, k_prev, k_cur, v_prev, v_cur, sinks_ref, qn, low, upper, rblk):
    no_prev = jnp.where(first, NEG_INF, 0.0)
    out_blocks = []
    for hk in range(N_KV_HEADS):
        qs = []
        for j in range(Q_PER_KV // 2):
            blk = hk * (Q_PER_KV // 2) + j
            qb = _half_norm(q[:, blk * LANES:(blk + 1) * LANES], low) * qn * (1.0 / math.sqrt(HEAD_DIM))
            qs.append(jnp.where(low, qb, 0.0))
            qs.append(jnp.where(low, 0.0, qb))
        qg = jnp.concatenate(qs, axis=0).astype(BF16)
        nt = (((1,), (1,)), ((), ()))
        s_prev = lax.dot_general(qg, k_prev[hk], nt, preferred_element_type=F32)
        s_cur = lax.dot_general(qg, k_cur[hk], nt, preferred_element_type=F32)
        s = jnp.where(upper, s_prev + no_prev, s_cur)
        sink = jnp.zeros((Q_PER_KV * BLOCK, 1), F32)
        for g in range(Q_PER_KV):
            sink = jnp.where(rblk == g, sinks_ref[hk * Q_PER_KV + g], sink)
        m = jnp.maximum(jnp.max(s, axis=-1, keepdims=True), sink)
        p = jnp.exp(s - m)
        den = jnp.sum(p, axis=-1, keepdims=True) + jnp.exp(sink - m)
        o = (jnp.dot(jnp.where(upper, p, 0.0).astype(BF16), v_prev[hk], preferred_element_type=F32)
             + jnp.dot(jnp.where(upper, 0.0, p).astype(BF16), v_cur[hk], preferred_element_type=F32)) / den
        for j in range(Q_PER_KV // 2):
            ev = o[(2 * j) * BLOCK:(2 * j + 1) * BLOCK]
            od = o[(2 * j + 1) * BLOCK:(2 * j + 2) * BLOCK]
            out_blocks.append(jnp.where(low, ev, od))
    return jnp.concatenate(out_blocks, axis=-1)


def _attn_kernel(sinks_ref, q_ref, k_ref, v_ref, qn_ref, kn_ref, on_ref, o_hbm, a_buf, sems, *, n_steps):
    step = pl.program_id(1)
    g = pl.program_id(0) * pl.num_programs(1) + step
    slot = g % 2
    cps = ATTN_ROWS // SSM_CHUNK
    nq = ATTN_ROWS // BLOCK

    def out_copies(sl, b_, s_):
        return [pltpu.make_async_copy(a_buf.at[sl, :, i, :], o_hbm.at[b_, i, pl.ds(s_ * cps, cps), :], sems.at[sl])
                for i in range(SSM_CHUNK)]

    @pl.when(g >= 2)
    def _():
        for cp in out_copies(slot, 0, 0):
            cp.wait()

    low = lax.broadcasted_iota(I32, (1, LANES), 1) < HEAD_DIM
    rows = Q_PER_KV * BLOCK
    upper = lax.broadcasted_iota(I32, (rows, BLOCK), 1) > lax.broadcasted_iota(I32, (rows, BLOCK), 0) % BLOCK
    rblk = lax.broadcasted_iota(I32, (rows, 1), 0) // BLOCK

    cur = pl.multiple_of(step * ATTN_ROWS, ATTN_ROWS)
    prev = pl.multiple_of(jnp.maximum(step * nq - 1, 0) * BLOCK, BLOCK)
    kall = jnp.concatenate([k_ref[0, pl.ds(prev, BLOCK), :], k_ref[0, pl.ds(cur, ATTN_ROWS), :]], axis=0).astype(F32)
    vall = jnp.concatenate([v_ref[0, pl.ds(prev, BLOCK), :], v_ref[0, pl.ds(cur, ATTN_ROWS), :]], axis=0).astype(F32)
    kall = _half_norm(kall, low) * kn_ref[...]
    kswap = pltpu.roll(kall, HEAD_DIM, axis=1)
    vswap = pltpu.roll(vall, HEAD_DIM, axis=1)
    k_dup = [jnp.where(low, kall, kswap).astype(BF16), jnp.where(low, kswap, kall).astype(BF16)]
    v_dup = [jnp.where(low, vall, vswap).astype(BF16), jnp.where(low, vswap, vall).astype(BF16)]
    blk = lambda a, i: [a[hk][i * BLOCK:(i + 1) * BLOCK] for hk in range(N_KV_HEADS)]

    for qb in range(nq):
        q = q_ref[0, qb * BLOCK:(qb + 1) * BLOCK, :].astype(F32)
        attn = _attn_block((step == 0) if qb == 0 else False, q, blk(k_dup, qb), blk(k_dup, qb + 1),
                           blk(v_dup, qb), blk(v_dup, qb + 1), sinks_ref, qn_ref[...], low, upper, rblk)
        attn = _rms(attn) * on_ref[...]
        cpb = BLOCK // SSM_CHUNK
        a_buf[slot, qb * cpb:(qb + 1) * cpb] = attn.reshape(cpb, SSM_CHUNK, D_ATTN)

    for cp in out_copies(slot, pl.program_id(0), step):
        cp.start()

    @pl.when(g == n_steps - 1)
    def _():
        for cp in out_copies(slot, 0, 0):
            cp.wait()
        if n_steps > 1:
            for cp in out_copies(1 - slot, 0, 0):
                cp.wait()


def _attention(q, k, v, sinks, q_norm, k_norm, out_norm):
    b, s, _ = q.shape
    tile2 = lambda g: jnp.tile(g.reshape(1, HEAD_DIM), (1, 2))
    cps = ATTN_ROWS // SSM_CHUNK
    return pl.pallas_call(
        functools.partial(_attn_kernel, n_steps=b * (s // ATTN_ROWS)),
        grid=(b, s // ATTN_ROWS),
        in_specs=[pl.BlockSpec(memory_space=pltpu.SMEM),
                  pl.BlockSpec((1, ATTN_ROWS, D_ATTN), lambda bi, n: (bi, n, 0)),
                  pl.BlockSpec((1, s, D_KV), lambda bi, n: (bi, 0, 0)),
                  pl.BlockSpec((1, s, D_KV), lambda bi, n: (bi, 0, 0)),
                  pl.BlockSpec((1, LANES), lambda bi, n: (0, 0)),
                  pl.BlockSpec((1, LANES), lambda bi, n: (0, 0)),
                  pl.BlockSpec((1, D_ATTN), lambda bi, n: (0, 0))],
        out_specs=pl.BlockSpec(memory_space=pl.ANY),
        out_shape=jax.ShapeDtypeStruct((b, SSM_CHUNK, s // SSM_CHUNK, D_ATTN), F32),
        scratch_shapes=[pltpu.VMEM((2, cps, SSM_CHUNK, D_ATTN), F32), pltpu.SemaphoreType.DMA((2,))],
        compiler_params=_cparams(2),
        name="attention",
    )(sinks, q, k, v, tile2(q_norm), tile2(k_norm), out_norm.reshape(1, D_ATTN))


def _cmul(ar, ai, br, bi):
    return ar * br - ai * bi, ar * bi + ai * br


def _ssm_param_kernel(*refs):
    for gi in range(refs[0].shape[0]):
        _ssm_param_group(gi, *refs)


def _ssm_param_group(gi, lam_ref, bre_ref, bim_ref, cre_ref, cim_ref, tt_ref, wz_ref, wyt_ref, cs_ref):
    f32dot = functools.partial(jnp.dot, preferred_element_type=F32, precision=HIGHEST)
    lr = lam_ref[gi, 0:1, :]
    li = lam_ref[gi, 1:2, :]
    dt = jnp.exp(lam_ref[gi, 2:3, :])
    rho = lr * dt
    th = li * dt
    imag_lane = lax.broadcasted_iota(I32, (1, LANES), 1) >= STATE

    kk = (lax.broadcasted_iota(I32, (N_POW, 1), 0) - (SSM_CHUNK - 1)).astype(F32)
    mag = jnp.exp(rho * kk)
    pw_r = mag * jnp.cos(th * kk)
    pw_i = mag * jnp.sin(th * kk)
    lb_r = pw_r[SSM_CHUNK:SSM_CHUNK + 1]
    lb_i = pw_i[SSM_CHUNK:SSM_CHUNK + 1]
    den = lr * lr + li * li
    coef_r = ((lb_r - 1.0) * lr + lb_i * li) / den
    coef_i = (lb_i * lr - (lb_r - 1.0) * li) / den

    eye = (lax.broadcasted_iota(I32, (SSM_GROUP, SSM_GROUP), 0)
           == lax.broadcasted_iota(I32, (SSM_GROUP, SSM_GROUP), 1)).astype(F32)
    lane_fold = (lax.broadcasted_iota(I32, (STATE, LANES), 1) % STATE
                 == lax.broadcasted_iota(I32, (STATE, LANES), 0)).astype(F32)

    def tile_pos(x):
        return jnp.concatenate([x] * SSM_CHUNK, axis=0)

    def power_rows(k_of_pos):
        idx = [k_of_pos(p) + (SSM_CHUNK - 1) for p in range(SSM_CHUNK)]
        rep = lambda t: jnp.concatenate([jnp.broadcast_to(t[r:r + 1], (SSM_GROUP, LANES)) for r in idx], axis=0)
        return rep(pw_r), rep(pw_i)

    def b_rows(b_ref):
        b2 = jnp.concatenate([b_ref[gi], b_ref[gi]], axis=0)
        return tile_pos(lax.dot_general(eye, b2, (((1,), (1,)), ((), ())), preferred_element_type=F32,
                                        precision=HIGHEST))

    def c_rows(c_ref):
        return tile_pos(f32dot(c_ref[gi], lane_fold))

    bbar_r, bbar_i = _cmul(coef_r, coef_i, b_rows(bre_ref), b_rows(bim_ref))
    c_r = c_rows(cre_ref)
    c_i = c_rows(cim_ref)

    a_r, a_i = _cmul(bbar_r, bbar_i, *power_rows(lambda p: -p))
    a2c = jnp.where(imag_lane, -a_i, a_r)
    m_r, m_i = _cmul(c_r, c_i, *power_rows(lambda p: p))
    bmc = jnp.where(imag_lane, m_i, m_r)
    tt = f32dot(bmc, a2c.T)
    causal = (lax.broadcasted_iota(I32, (SSM_ROW, 1), 0) // SSM_GROUP
              >= lax.broadcasted_iota(I32, (1, SSM_ROW), 1) // SSM_GROUP)
    tt_ref[gi] = jnp.where(causal, tt, 0.0).astype(BF16)

    w_r, w_i = _cmul(bbar_r, bbar_i, *power_rows(lambda p: SSM_CHUNK - 1 - p))
    wz_ref[gi, :, :LANES] = jnp.where(imag_lane, w_i, w_r).astype(BF16)
    wz_ref[gi, :, LANES:] = jnp.where(imag_lane, w_r, w_i).astype(BF16)

    y_r, y_i = _cmul(c_r, c_i, *power_rows(lambda p: p + 1))
    wyt_ref[gi] = jnp.where(imag_lane, -y_i, y_r).astype(BF16)

    cs_ref[gi, 0:1, :] = pw_r[N_POW - 1:N_POW]
    cs_ref[gi, 1:2, :] = jnp.where(imag_lane, pw_i[N_POW - 1:N_POW], -pw_i[N_POW - 1:N_POW])


def _ssm_params(lam_re, lam_im, log_dt, b_re, b_im, c_re, c_im):
    g = lam_re.shape[0]
    lam = jnp.stack([lam_re, lam_im, jnp.broadcast_to(log_dt[:, None], (g, STATE))], axis=1)
    lam = jnp.concatenate([lam, lam], axis=2)
    ng = SSM_GROUPS_PER_STEP
    blk = lambda *shape: pl.BlockSpec((ng,) + shape, lambda i: (i, 0, 0))
    return pl.pallas_call(
        _ssm_param_kernel,
        grid=(g // ng,),
        in_specs=[blk(3, LANES), blk(STATE, SSM_GROUP), blk(STATE, SSM_GROUP), blk(SSM_GROUP, STATE),
                  blk(SSM_GROUP, STATE)],
        out_specs=[blk(SSM_ROW, SSM_ROW), blk(SSM_ROW, SSM_ROW), blk(SSM_ROW, LANES), blk(2, LANES)],
        out_shape=[jax.ShapeDtypeStruct((g, SSM_ROW, SSM_ROW), BF16),
                   jax.ShapeDtypeStruct((g, SSM_ROW, SSM_ROW), BF16),
                   jax.ShapeDtypeStruct((g, SSM_ROW, LANES), BF16),
                   jax.ShapeDtypeStruct((g, 2, LANES), F32)],
        compiler_params=_cparams(1),
        name="ssm_params",
    )(lam, b_re, b_im, c_re, c_im)


def _ssm_kernel(ut_ref, tt_ref, wz_ref, wyt_ref, cs_ref, d_ref, yt_ref, z_scr, s_scr):
    batch, ng, _, nc = ut_ref.shape
    uts = [jnp.concatenate([ut_ref[b, gi] for b in range(batch)], axis=1) for gi in range(ng)]
    for gi in range(ng):
        z = lax.dot_general(uts[gi], wz_ref[gi], (((0,), (0,)), ((), ())), preferred_element_type=F32)
        _to_lane_blocks(z_scr.at[gi], z)
    c1 = [cs_ref[gi, 0:1, :] for gi in range(ng)]
    c2 = [cs_ref[gi, 1:2, :] for gi in range(ng)]

    def step(c, carry):
        rows = pl.ds(c, batch, stride=nc)
        out = []
        for gi in range(ng):
            s1, s2 = carry[gi]
            s_scr[gi, rows, :] = s1
            out.append((c1[gi] * s1 + c2[gi] * s2 + z_scr[gi, 0, rows, :],
                        c1[gi] * s2 - c2[gi] * s1 + z_scr[gi, 1, rows, :]))
        return tuple(out)

    zero = jnp.zeros((batch, LANES), F32)
    lax.fori_loop(0, nc, step, ((zero, zero),) * ng, unroll=8)
    for gi in range(ng):
        y = jnp.dot(tt_ref[gi], uts[gi], preferred_element_type=F32)
        y = y + lax.dot_general(wyt_ref[gi], s_scr[gi].astype(BF16), (((1,), (1,)), ((), ())),
                                preferred_element_type=F32)
        y = y + d_ref[gi] * uts[gi].astype(F32)
        for b in range(batch):
            yt_ref[b, gi] = y[:, b * nc:(b + 1) * nc]


def _ssm(ut, tt, wz, wyt, cs, d_skip):
    b, g, _, nc = ut.shape
    ng = SSM_GROUPS_PER_STEP
    d_col = jnp.tile(d_skip.reshape(g, 1, SSM_GROUP), (1, SSM_CHUNK, 1)).reshape(g, SSM_ROW, 1)
    blk = lambda *shape: pl.BlockSpec((ng,) + shape, lambda i: (i, 0, 0))
    act = pl.BlockSpec((b, ng, SSM_ROW, nc), lambda i: (0, i, 0, 0))
    return pl.pallas_call(
        _ssm_kernel,
        grid=(g // ng,),
        in_specs=[act, blk(SSM_ROW, SSM_ROW), blk(SSM_ROW, SSM_ROW), blk(SSM_ROW, LANES), blk(2, LANES),
                  blk(SSM_ROW, 1)],
        out_specs=act,
        out_shape=jax.ShapeDtypeStruct((b, g, SSM_ROW, nc), F32),
        scratch_shapes=[pltpu.VMEM((ng, SSM_ROW // LANES, b * nc, LANES), F32), pltpu.VMEM((ng, b * nc, LANES), F32)],
        compiler_params=_cparams(1),
        name="ssm",
    )(ut, tt, wz, wyt, cs, d_col)


def _post_kernel(x4_hbm, attn_ref, yt_ref, mod_ref, wglut_ref, bglu_ref, sn_ref, wout_ref, nf_ref, wr_ref, br_ref,
                 tri_ref, x1_ref, h2_ref, eidx_ref, wts_ref, lrank_ref, r0_ref, cnt_ref, carry_ref, xp_buf, sems):
    @pl.when((pl.program_id(0) == 0) & (pl.program_id(1) == 0))
    def _():
        carry_ref[...] = jnp.zeros_like(carry_ref)

    slot = _prefetch_pos_rows(x4_hbm, xp_buf, sems, POST_SUB * POST_POS)
    nc = attn_ref.shape[2]
    ts = POST_POS * nc
    d = x1_ref.shape[3]
    iota_e = lax.broadcasted_iota(I32, (N_EXPERTS, ts), 0).astype(F32)
    counts = []
    for sub in range(POST_SUB):
        pos = range(sub * POST_POS, (sub + 1) * POST_POS)
        lanes = slice(sub * ts, (sub + 1) * ts)
        yt = jnp.concatenate(
            [yt_ref[0, :, il * SSM_GROUP:(il + 1) * SSM_GROUP, :].reshape(D_SSM, nc) for il in pos], axis=1)
        g = jax.nn.gelu(yt)
        gate = jax.nn.sigmoid(jnp.dot(wglut_ref[...], g.astype(BF16), preferred_element_type=F32) + bglu_ref[...])
        ssm_t = _rms(g * gate, axis=0) * sn_ref[...]
        attn = attn_ref[0, sub * POST_POS:(sub + 1) * POST_POS].reshape(ts, D_ATTN)
        mixed = jnp.concatenate([attn.astype(BF16), ssm_t.T.astype(BF16)], axis=-1)
        o = jnp.dot(mixed, wout_ref[...], preferred_element_type=F32)
        x = jnp.concatenate([xp_buf[slot, il] for il in pos], axis=0)
        x1 = x + mod_ref[0, 2:3, :] * o
        x1_ref[0, sub * POST_POS:(sub + 1) * POST_POS] = x1.reshape(POST_POS, nc, d)
        h2 = _rms(x1) * nf_ref[...] * (1.0 + mod_ref[0, 4:5, :]) + mod_ref[0, 3:4, :]
        h2_ref[0, sub * POST_POS:(sub + 1) * POST_POS] = h2.astype(BF16).reshape(POST_POS, nc, d)

        logits = lax.dot_general(wr_ref[...], h2.astype(BF16), (((1,), (1,)), ((), ())),
                                 preferred_element_type=F32) + br_ref[...]
        l = logits
        idxs, vals = [], []
        for _ in range(TOP_K):
            m = jnp.max(l, axis=0, keepdims=True)
            idx = jnp.min(jnp.where(l == m, iota_e, float(N_EXPERTS)), axis=0, keepdims=True)
            idxs.append(idx)
            vals.append(m)
            l = jnp.where(iota_e == idx, -jnp.inf, l)
        es = [jnp.exp(v - vals[0]) for v in vals]
        tot = es[0] + es[1] + es[2] + es[3]
        member = jnp.zeros((N_EXPERTS, ts), F32)
        for idx in idxs:
            member = member + (iota_e == idx).astype(F32)
        before = jnp.dot(member.astype(BF16), tri_ref[...], preferred_element_type=F32)
        for k in range(TOP_K):
            eidx_ref[k:k + 1, lanes] = idxs[k].astype(I32)
            wts_ref[k:k + 1, lanes] = es[k] / tot
            lrank_ref[k:k + 1, lanes] = jnp.sum(jnp.where(iota_e == idxs[k], before, 0.0), axis=0,
                                                keepdims=True).astype(I32)
        counts.append(jnp.sum(member, axis=1, keepdims=True))

    carry = carry_ref[...]
    for sub in range(POST_SUB):
        r0_ref[sub] = carry.astype(I32)
        carry = carry + counts[sub]
    carry_ref[...] = carry
    cnt_ref[...] = carry.astype(I32)


def _post(x, attn, yt, mod, w_glu, b_glu, ssm_norm, w_out, norm_ffn, w_router, b_router):
    b, s, d = x.shape
    nc = s // SSM_CHUNK
    ts = POST_POS * nc
    npos = POST_SUB * POST_POS
    nt = SSM_CHUNK // npos
    t = b * s
    pm = lambda bi, j: (bi, j, 0, 0)
    const = lambda bi, j: (0, 0)
    tok = lambda bi, j: (0, bi * nt + j)
    tri = (lax.broadcasted_iota(I32, (ts, ts), 0) < lax.broadcasted_iota(I32, (ts, ts), 1)).astype(BF16)
    col = lambda a: a.reshape(-1, 1)
    return pl.pallas_call(
        _post_kernel,
        grid=(b, nt),
        in_specs=[pl.BlockSpec(memory_space=pl.ANY),
                  pl.BlockSpec((1, npos, nc, D_ATTN), pm),
                  pl.BlockSpec((1, N_GROUPS, npos * SSM_GROUP, nc), lambda bi, j: (bi, 0, j, 0)),
                  pl.BlockSpec((1, 6, d), lambda bi, j: (bi, 0, 0)),
                  pl.BlockSpec((D_SSM, D_SSM), const),
                  pl.BlockSpec((D_SSM, 1), const),
                  pl.BlockSpec((D_SSM, 1), const),
                  pl.BlockSpec((d, d), const),
                  pl.BlockSpec((1, d), const),
                  pl.BlockSpec((N_EXPERTS, d), const),
                  pl.BlockSpec((N_EXPERTS, 1), const),
                  pl.BlockSpec((ts, ts), const)],
        out_specs=[pl.BlockSpec((1, npos, nc, d), pm),
                   pl.BlockSpec((1, npos, nc, d), pm),
                   pl.BlockSpec((TOP_K, POST_SUB * ts), tok),
                   pl.BlockSpec((TOP_K, POST_SUB * ts), tok),
                   pl.BlockSpec((TOP_K, POST_SUB * ts), tok),
                   pl.BlockSpec((POST_SUB, N_EXPERTS, 1), lambda bi, j: (bi * nt + j, 0, 0)),
                   pl.BlockSpec((N_EXPERTS, 1), const)],
        out_shape=[jax.ShapeDtypeStruct((b, SSM_CHUNK, nc, d), F32),
                   jax.ShapeDtypeStruct((b, SSM_CHUNK, nc, d), BF16),
                   jax.ShapeDtypeStruct((TOP_K, t), I32),
                   jax.ShapeDtypeStruct((TOP_K, t), F32),
                   jax.ShapeDtypeStruct((TOP_K, t), I32),
                   jax.ShapeDtypeStruct((b * nt * POST_SUB, N_EXPERTS, 1), I32),
                   jax.ShapeDtypeStruct((N_EXPERTS, 1), I32)],
        scratch_shapes=[pltpu.VMEM((N_EXPERTS, 1), F32), pltpu.VMEM((2, npos, nc, d), F32),
                        pltpu.SemaphoreType.DMA((2,))],
        compiler_params=_cparams(2),
        name="post",
    )(x.reshape(b, nc, SSM_CHUNK, d), attn, yt, mod, w_glu.T.astype(BF16), col(b_glu), col(ssm_norm), w_out.astype(BF16),
      norm_ffn.reshape(1, -1), w_router.T.astype(BF16), col(b_router), tri)


def _route_kernel(eidx_ref, lrank_ref, r0_ref, cnt_ref, ls_ref, tab_ref, te_ref, nv_ref, nx_ref, pad_ref):
    cnt = cnt_ref[...]
    tiles = (cnt + (RUN - 1 + FFN_ROWS - 1)) // FFN_ROWS
    er = lax.broadcasted_iota(I32, (N_EXPERTS, N_EXPERTS), 0)
    ec = lax.broadcasted_iota(I32, (N_EXPERTS, N_EXPERTS), 1)
    ltri = (ec < er).astype(BF16)

    def excl_cumsum(v):
        vb = jnp.broadcast_to(v.astype(F32), (N_EXPERTS, LANES)).astype(BF16)
        return jnp.dot(ltri, vb, preferred_element_type=F32)[:, 0:1].astype(I32)

    start_t = excl_cumsum(tiles)
    end_t = start_t + tiles
    start = start_t * FFN_ROWS
    pad_ref[...] = start + cnt

    nb = r0_ref.shape[0]
    ts = eidx_ref.shape[1] // nb
    iota_e = lax.broadcasted_iota(I32, (N_EXPERTS, ts), 0)
    iota_t = lax.broadcasted_iota(I32, (N_EXPERTS, TABW), 0)
    lane = lax.broadcasted_iota(I32, (1, TABW), 1)

    def block(b, carry):
        lanes = pl.ds(pl.multiple_of(b * ts, ts), ts)
        sels = [iota_e == eidx_ref[k:k + 1, lanes] for k in range(TOP_K)]
        member = sels[0].astype(I32) + sels[1].astype(I32) + sels[2].astype(I32) + sels[3].astype(I32)
        units = lax.shift_right_logical(jnp.sum(member, axis=1, keepdims=True) + (RUN - 1), RUN_SHIFT)
        u0 = excl_cumsum(units)
        for k in range(TOP_K):
            first = jnp.sum(jnp.where(sels[k], u0, 0), axis=0, keepdims=True)
            ls_ref[k:k + 1, lanes] = first * RUN + lrank_ref[k:k + 1, lanes]
        n_big = lax.shift_right_logical(units, 1)
        n_small = units & 1
        slot0 = start + r0_ref[b]

        def chunk_rows(idx, counts):
            c0 = excl_cumsum(counts)
            sel = iota_t == jnp.sum((idx >= c0 + counts).astype(I32), axis=0, keepdims=True)
            pick = lambda v: jnp.sum(jnp.where(sel, v, 0), axis=0, keepdims=True)
            j = idx - pick(c0)
            return pick(slot0), pick(u0), j, pick(n_big), idx < jnp.max(c0 + counts, axis=0, keepdims=True)

        s_b, u_b, j_b, _, ok_b = chunk_rows(lane, n_big)
        s_s, u_s, _, nb_s, ok_s = chunk_rows(lane - SMALL0, n_small)
        small = lane >= SMALL0
        slot = jnp.where(small, s_s + nb_s * BIG, s_b + j_b * BIG)
        local = jnp.where(small, (u_s + 2 * nb_s) * RUN, (u_b + 2 * j_b) * RUN)
        ok = (small & ok_s) | (jnp.logical_not(small) & ok_b)
        counts = jnp.where(lane == TABW - 2, jnp.sum(n_big, axis=0, keepdims=True),
                           jnp.sum(n_small, axis=0, keepdims=True))
        tab_ref[b, 0:1, :] = jnp.where(lane >= TABW - 2, counts, jnp.where(ok, slot, -1))
        tab_ref[b, 1:2, :] = jnp.where(ok, local, 0)
        return carry

    lax.fori_loop(0, nb, block, 0, unroll=2)

    nv = jnp.max(end_t, axis=0, keepdims=True)
    width = te_ref.shape[1]
    ti = jnp.minimum(lax.broadcasted_iota(I32, (N_EXPERTS, width), 1), nv - 1)
    te = jnp.minimum(jnp.sum((ti >= end_t).astype(I32), axis=0, keepdims=True), N_EXPERTS - 1)
    te_ref[...] = te
    nv_ref[...] = jnp.broadcast_to(nv, nv_ref.shape)
    ie = lax.broadcasted_iota(I32, (N_EXPERTS, width), 0)
    own_end = jnp.sum(jnp.where(ie == te, end_t, 0), axis=0, keepdims=True)
    nxt = jnp.minimum(jnp.sum((own_end >= end_t).astype(I32), axis=0, keepdims=True), N_EXPERTS - 1)
    nx_ref[...] = jnp.where(own_end < nv, nxt, -1)


def _route(eidx, lrank, r0, cnt, n_tiles):
    t = eidx.shape[1]
    nb = r0.shape[0]
    width = -(-n_tiles // LANES) * LANES
    return pl.pallas_call(
        _route_kernel,
        out_shape=[jax.ShapeDtypeStruct((TOP_K, t), I32),
                   jax.ShapeDtypeStruct((nb, 2, TABW), I32),
                   jax.ShapeDtypeStruct((1, width), I32),
                   jax.ShapeDtypeStruct((1, LANES), I32),
                   jax.ShapeDtypeStruct((1, width), I32),
                   jax.ShapeDtypeStruct((N_EXPERTS, 1), I32)],
        name="route",
    )(eidx, lrank, r0, cnt)


def _for_chunk_pairs(n, fn):
    def body(i, carry):
        fn(2 * i, 0)

        @pl.when(2 * i + 1 < n)
        def _():
            fn(2 * i + 1, 1)
        return carry
    lax.fori_loop(0, lax.shift_right_logical(n + 1, 1), body, 0)


def _for_block_chunks(tab_ref, blk, fn):
    base = blk * (2 * TABW)
    for first, count_lane, n_rows in ((0, TABW - 2, BIG), (SMALL0, TABW - 1, RUN)):
        def visit(c, parity, first=first, n_rows=n_rows):
            fn(tab_ref[base + first + c], pl.multiple_of(tab_ref[base + TABW + first + c], RUN), n_rows, parity)
        _for_chunk_pairs(tab_ref[base + count_lane], visit)


def _local_rows(ts):
    return ts * TOP_K + N_EXPERTS * RUN


def _dispatch_kernel(tab_ref, pad_ref, nvt_ref, h_ref, ls_ref, xs_ref, buf, zbuf, sems, zsem):
    b = pl.program_id(0)
    slot = b % 2
    ts = h_ref.shape[0]
    local = buf.shape[2]

    def chunk_copy(sl, slot_row, local_row, n):
        return pltpu.make_async_copy(buf.at[sl, :, pl.ds(local_row, n), :], xs_ref.at[:, pl.ds(slot_row, n), :],
                                     sems.at[sl])

    @pl.when(b == 0)
    def _():
        zbuf[...] = jnp.zeros_like(zbuf)
        zrows = zbuf.shape[1]
        zero = lambda row: pltpu.make_async_copy(zbuf, xs_ref.at[:, pl.ds(row, zrows), :], zsem)
        for phase in range(3):
            for e in range(phase, N_EXPERTS, 3):
                zero(pad_ref[e]).start()
            for e in range(phase, N_EXPERTS, 3):
                zero(pad_ref[e]).wait()
        ztile = lambda i: pltpu.make_async_copy(zbuf.at[:, pl.ds(0, FFN_ROWS), :],
                                                xs_ref.at[:, pl.ds((nvt_ref[0] + i) * FFN_ROWS, FFN_ROWS), :], zsem)

        def tail_start(i, carry):
            ztile(i).start()
            return carry

        def tail_wait(i, carry):
            ztile(i).wait()
            return carry
        lax.fori_loop(0, nvt_ref[1] - nvt_ref[0], tail_start, 0)
        lax.fori_loop(0, nvt_ref[1] - nvt_ref[0], tail_wait, 0)

    r = lax.broadcasted_iota(I32, (local, ts), 0)
    hit = (r == ls_ref[0:1, :]) | (r == ls_ref[1:2, :]) | (r == ls_ref[2:3, :]) | (r == ls_ref[3:4, :])
    hit = hit.astype(BF16)
    for pb in range(PANELS):
        srt = jnp.dot(hit, h_ref[:, pb * PANEL_COLS:(pb + 1) * PANEL_COLS], preferred_element_type=F32)
        buf[slot, pb] = _pack_panel(srt, exact=True)

    @pl.when(b > 0)
    def _():
        _for_block_chunks(tab_ref, b - 1, lambda s, l, n, p: chunk_copy(1 - slot, s, l, n).wait())

    _for_block_chunks(tab_ref, b, lambda s, l, n, p: chunk_copy(slot, s, l, n).start(priority=p))

    @pl.when(b == pl.num_programs(0) - 1)
    def _():
        _for_block_chunks(tab_ref, b, lambda s, l, n, p: chunk_copy(slot, s, l, n).wait())


def _dispatch(tab, pad, nvt, h2, ls, n_rows):
    t, d = h2.shape
    nb = tab.shape[0] // (2 * TABW)
    ts = t // nb
    return pl.pallas_call(
        _dispatch_kernel,
        grid_spec=pltpu.PrefetchScalarGridSpec(
            num_scalar_prefetch=3,
            grid=(nb,),
            in_specs=[pl.BlockSpec((ts, d), lambda i, *_: (i, 0)),
                      pl.BlockSpec((TOP_K, ts), lambda i, *_: (0, i))],
            out_specs=pl.BlockSpec(memory_space=pl.ANY),
            scratch_shapes=[pltpu.VMEM((2, PANELS, _local_rows(ts), LANES), U32),
                            pltpu.VMEM((PANELS, FFN_ROWS + RUN, LANES), U32),
                            pltpu.SemaphoreType.DMA((2,)), pltpu.SemaphoreType.DMA],
        ),
        out_shape=jax.ShapeDtypeStruct((PANELS, n_rows, LANES), U32),
        compiler_params=_cparams(1, ROW_VMEM_BYTES),
        name="dispatch",
    )(tab, pad, nvt, h2, ls)


def _ffn_kernel(te_ref, nv_ref, nx_ref, xs_ref, wgu_hbm, bgu_ref, wd_hbm, bd_ref, perm_ref, ys_ref,
                wgu_stage, wd_stage, wg_scr, wu_scr, wd_scr, bg_scr, bu_scr, sems):
    p = pl.program_id(0)
    t0 = 2 * p
    t1 = t0 + 1
    e0 = te_ref[t0]
    e1 = te_ref[t1]
    v0 = t0 < nv_ref[0]
    v1 = t1 < nv_ref[0]
    new0 = (p == 0) | (e0 != te_ref[jnp.maximum(t0 - 1, 0)])
    same = v1 & (e1 == e0)

    def stage_copies(e):
        return (pltpu.make_async_copy(wgu_hbm.at[e], wgu_stage, sems.at[0]),
                pltpu.make_async_copy(wd_hbm.at[e], wd_stage, sems.at[1]))

    def load_expert(t, first):
        e = te_ref[t]
        if first:
            @pl.when(p == 0)
            def _():
                for cp in stage_copies(e):
                    cp.start()

        for cp in stage_copies(e):
            cp.wait()
        bias = bgu_ref[e]
        for c in range(2 * D_FF // PERM):
            cols = slice(c * PERM, (c + 1) * PERM)
            half = slice(c * (PERM // 2), (c + 1) * (PERM // 2))
            w = wgu_stage[:, cols].astype(BF16)
            pw = jnp.dot(w, perm_ref[...], preferred_element_type=F32).astype(BF16)
            wg_scr[:, half] = pw[:, :PERM // 2]
            wu_scr[:, half] = pw[:, PERM // 2:]
            b1 = bias[:, cols].astype(BF16)
            r1 = bias[:, cols] - b1.astype(F32)
            b2 = r1.astype(BF16)
            b3 = (r1 - b2.astype(F32)).astype(BF16)
            terms = jnp.concatenate([b1, b2, b3, jnp.zeros((5, PERM), BF16)], axis=0)
            pb = jnp.sum(jnp.dot(terms, perm_ref[...], preferred_element_type=F32), axis=0, keepdims=True)
            bg_scr[:, half] = pb[:, :PERM // 2]
            bu_scr[:, half] = pb[:, PERM // 2:]
        wd_scr[...] = wd_stage[...].astype(BF16)

        @pl.when(nx_ref[t] >= 0)
        def _():
            for cp in stage_copies(nx_ref[t]):
                cp.start()

    def run(lo, n, e):
        x = _unpack_panels([xs_ref[pb, lo:lo + n, :] for pb in range(PANELS)])
        gate = jnp.dot(x, wg_scr[...], preferred_element_type=F32) + bg_scr[...]
        up = jnp.dot(x, wu_scr[...], preferred_element_type=F32) + bu_scr[...]
        gate = jnp.minimum(gate, SWIGLU_LIMIT)
        up = jnp.clip(up, -SWIGLU_LIMIT, SWIGLU_LIMIT)
        act = ((up + 1.0) * (gate * jax.nn.sigmoid(SWIGLU_ALPHA * gate))).astype(BF16)
        bd = bd_ref[e]
        for pb in range(PANELS):
            cols = slice(pb * PANEL_COLS, (pb + 1) * PANEL_COLS)
            y = jnp.dot(act, wd_scr[:, cols], preferred_element_type=F32) + bd[:, cols]
            ys_ref[pb, lo:lo + n, :] = _pack_panel(y)

    @pl.when(v0 & new0)
    def _():
        load_expert(t0, True)

    @pl.when(same)
    def _():
        run(0, 2 * FFN_ROWS, e0)

    @pl.when(v0 & jnp.logical_not(same))
    def _():
        run(0, FFN_ROWS, e0)

    @pl.when(v1 & jnp.logical_not(same))
    def _():
        load_expert(t1, False)
        run(FFN_ROWS, FFN_ROWS, e1)

    @pl.when(v0 & jnp.logical_not(v1))
    def _():
        ys_ref[:, FFN_ROWS:, :] = xs_ref[:, FFN_ROWS:, :]


def _ffn(te, nv, nx, xs, w_gate_up, bgu, w_down, bd, n_tiles):
    d = D_MODEL
    pair = lambda i, te, nv, nx: (0, jnp.minimum(i, lax.shift_right_logical(nv[0] - 1, 1)), 0)
    whole = lambda i, te, nv, nx: (0, 0, 0)
    r = lax.broadcasted_iota(I32, (PERM, PERM), 0)
    c = lax.broadcasted_iota(I32, (PERM, PERM), 1)
    perm = (r == jnp.where(c < PERM // 2, 2 * c, 2 * (c - PERM // 2) + 1)).astype(BF16)
    return pl.pallas_call(
        _ffn_kernel,
        grid_spec=pltpu.PrefetchScalarGridSpec(
            num_scalar_prefetch=3,
            grid=(n_tiles // 2,),
            in_specs=[pl.BlockSpec((PANELS, 2 * FFN_ROWS, LANES), pair),
                      pl.BlockSpec(memory_space=pl.ANY),
                      pl.BlockSpec((N_EXPERTS, 1, 2 * D_FF), whole),
                      pl.BlockSpec(memory_space=pl.ANY),
                      pl.BlockSpec((N_EXPERTS, 1, d), whole),
                      pl.BlockSpec((PERM, PERM), lambda i, te, nv, nx: (0, 0))],
            out_specs=pl.BlockSpec((PANELS, 2 * FFN_ROWS, LANES), pair),
            scratch_shapes=[pltpu.VMEM((d, 2 * D_FF), F32), pltpu.VMEM((D_FF, d), F32),
                            pltpu.VMEM((d, D_FF), BF16), pltpu.VMEM((d, D_FF), BF16), pltpu.VMEM((D_FF, d), BF16),
                            pltpu.VMEM((1, D_FF), F32), pltpu.VMEM((1, D_FF), F32),
                            pltpu.SemaphoreType.DMA((2,))],
        ),
        out_shape=jax.ShapeDtypeStruct(xs.shape, U32),
        input_output_aliases={3: 0},
        compiler_params=_cparams(1, FFN_VMEM_BYTES),
        name="ffn",
    )(te, nv, nx, xs, w_gate_up, bgu, w_down, bd, perm)


def _combine_kernel(tab_ref, x1_ref, ls_ref, w_ref, mod_ref, ys_ref, o4_hbm, ybuf, ob_buf, sems, osems, *, n_steps):
    jj = pl.program_id(1)
    g = pl.program_id(0) * pl.num_programs(1) + jj
    slot = g % 2
    nc = x1_ref.shape[2]
    tt = POST_POS * nc
    d = x1_ref.shape[3]
    local = ybuf.shape[3]
    npos = COMB_SUB * POST_POS

    def for_step_chunks(step, sl, fn):
        for sub in range(COMB_SUB):
            def visit(s, l, n, p, sub=sub):
                fn(pltpu.make_async_copy(ys_ref.at[:, pl.ds(s, n), :], ybuf.at[sl, sub, :, pl.ds(l, n), :],
                                         sems.at[sl]), p)
            _for_block_chunks(tab_ref, step * COMB_SUB + sub, visit)

    @pl.when(g == 0)
    def _():
        ybuf[...] = jnp.zeros_like(ybuf)
        for_step_chunks(0, 0, lambda cp, p: cp.start(priority=p))

    @pl.when(g + 1 < n_steps)
    def _():
        for_step_chunks(g + 1, 1 - slot, lambda cp, p: cp.start(priority=p))

    for_step_chunks(g, slot, lambda cp, p: cp.wait())

    def out_copies(sl, b_, j_):
        return [pltpu.make_async_copy(ob_buf.at[sl, il], o4_hbm.at[b_, :, npos * j_ + il, :], osems.at[sl])
                for il in range(npos)]

    @pl.when(g >= 2)
    def _():
        for cp in out_copies(slot, 0, 0):
            cp.wait()

    r = lax.broadcasted_iota(I32, (tt, local), 1).astype(F32)
    to_cols = lambda a: jnp.concatenate([a, jnp.zeros_like(a)], axis=0).T
    for sub in range(COMB_SUB):
        lanes = slice(sub * tt, (sub + 1) * tt)
        ls_c = to_cols(ls_ref[:, lanes].astype(F32))
        w_c = to_cols(w_ref[:, lanes])
        wm = jnp.zeros((tt, local), F32)
        for k in range(TOP_K):
            wm = jnp.where(r == ls_c[:, k:k + 1], w_c[:, k:k + 1], wm)
        y = _unpack_panels([ybuf[slot, sub, pb] for pb in range(PANELS)])
        acc = jnp.dot(wm.astype(BF16), y, preferred_element_type=F32)
        out = x1_ref[0, sub * POST_POS:(sub + 1) * POST_POS].reshape(tt, d) + mod_ref[0, 5:6, :] * acc
        for il in range(POST_POS):
            ob_buf[slot, sub * POST_POS + il] = out[il * nc:(il + 1) * nc]

    for cp in out_copies(slot, pl.program_id(0), jj):
        cp.start()

    @pl.when(g == n_steps - 1)
    def _():
        for cp in out_copies(slot, 0, 0):
            cp.wait()
        if n_steps > 1:
            for cp in out_copies(1 - slot, 0, 0):
                cp.wait()


def _combine(tab, x1, ls, wts, mod, ys):
    b, _, nc, d = x1.shape
    s = SSM_CHUNK * nc
    tt = POST_POS * nc
    npos = COMB_SUB * POST_POS
    nt = SSM_CHUNK // npos
    o4 = pl.pallas_call(
        functools.partial(_combine_kernel, n_steps=b * nt),
        grid_spec=pltpu.PrefetchScalarGridSpec(
            num_scalar_prefetch=1,
            grid=(b, nt),
            in_specs=[pl.BlockSpec((1, npos, nc, d), lambda bi, j, *_: (bi, j, 0, 0)),
                      pl.BlockSpec((TOP_K, COMB_SUB * tt), lambda bi, j, *_: (0, bi * nt + j)),
                      pl.BlockSpec((TOP_K, COMB_SUB * tt), lambda bi, j, *_: (0, bi * nt + j)),
                      pl.BlockSpec((1, 6, d), lambda bi, j, *_: (bi, 0, 0)),
                      pl.BlockSpec(memory_space=pl.ANY)],
            out_specs=pl.BlockSpec(memory_space=pl.ANY),
            scratch_shapes=[pltpu.VMEM((2, COMB_SUB, PANELS, _local_rows(tt), LANES), U32),
                            pltpu.VMEM((2, npos, nc, d), F32),
                            pltpu.SemaphoreType.DMA((2,)), pltpu.SemaphoreType.DMA((2,))],
        ),
        out_shape=jax.ShapeDtypeStruct((b, nc, SSM_CHUNK, d), F32),
        compiler_params=_cparams(2),
        name="combine",
    )(tab, x1, ls, wts, mod, ys)
    return o4.reshape(b, s, d)


def kernel(x, c, w_ada, b_ada, norm_mix, w_in, b_in, q_norm, k_norm, sinks, lam_re, lam_im, log_dt, b_re, b_im,
           c_re, c_im, d_skip, w_glu, b_glu, attn_out_norm, ssm_out_norm, w_out, norm_ffn, w_router, b_router,
           w_gate_up, b_gate_up, w_down, b_down):
    b, s, d = x.shape
    t = b * s
    depth = w_ada.shape[0]
    n_tiles = -(-(t * TOP_K + N_EXPERTS * (RUN - 1 + FFN_ROWS - 1)) // FFN_ROWS)
    n_tiles += n_tiles % 2
    n_alloc = n_tiles + 2
    for l in range(depth):
        mod = _adaln(c, w_ada[l], b_ada[l]).reshape(b, 6, d)
        q, k, v, ut = _inproj(x, mod, norm_mix[l], w_in[l], b_in[l])
        attn = _attention(q, k, v, sinks[l], q_norm[l], k_norm[l], attn_out_norm[l])
        tt, wz, wyt, cs = _ssm_params(lam_re[l], lam_im[l], log_dt[l], b_re[l], b_im[l], c_re[l], c_im[l])
        yt = _ssm(ut, tt, wz, wyt, cs, d_skip[l])
        x1, h2, eidx, wts, lrank, r0, cnt = _post(x, attn, yt, mod, w_glu[l], b_glu[l], ssm_out_norm[l], w_out[l],
                                                  norm_ffn[l], w_router[l], b_router[l])
        ls, tab, te, nv, nx, pad = _route(eidx, lrank, r0, cnt, n_tiles)
        tab = tab.reshape(-1)
        nvt = jnp.stack([nv[0, 0], jnp.int32(n_alloc)])
        xs = _dispatch(tab, pad.reshape(-1), nvt, h2.reshape(t, d), ls, n_alloc * FFN_ROWS)
        ys = _ffn(te[0, :n_tiles], nv[0, :1], nx[0, :n_tiles], xs, w_gate_up[l], b_gate_up[l][:, None, :],
                  w_down[l], b_down[l][:, None, :], n_tiles)
        x = _combine(tab, x1, ls, wts, mod, ys)
    return x
```

```python
import functools
import math

import jax
import jax.numpy as jnp
from jax import lax
from jax.experimental import pallas as pl
from jax.experimental.pallas import tpu as pltpu

F32 = jnp.float32
BF16 = jnp.bfloat16
U32 = jnp.uint32
I32 = jnp.int32

D_MODEL = 1024
HEAD_DIM = 64
N_HEADS = 8
N_KV_HEADS = 2
Q_PER_KV = N_HEADS // N_KV_HEADS
D_ATTN = N_HEADS * HEAD_DIM
D_KV = N_KV_HEADS * HEAD_DIM
D_QKV = D_ATTN + 2 * D_KV
WINDOW = 128
BLOCK = 128
D_SSM = D_MODEL - D_ATTN
SSM_GROUP = 16
N_GROUPS = D_SSM // SSM_GROUP
STATE = 64
N_EXPERTS = 32
TOP_K = 4
D_FF = D_MODEL
SWIGLU_LIMIT = 7.0
SWIGLU_ALPHA = 1.702
EPS = 1e-6
NEG_INF = -1e30

LANES = 128
SSM_CHUNK = 16
SSM_ROW = SSM_CHUNK * SSM_GROUP
N_POW = 2 * SSM_CHUNK
PANEL_COLS = 2 * LANES
PANELS = D_MODEL // PANEL_COLS

SSM_GROUPS_PER_STEP = 4
POS_PER_STEP = 8
ATTN_ROWS = 512
POST_POS = 2
POST_SUB = 4
COMB_SUB = 2
FFN_ROWS = 256
RUN = 8
RUN_SHIFT = 3
BIG = 2 * RUN
TABW = 128
SMALL0 = 80
PERM = 256
FFN_VMEM_BYTES = 40 * 1024 * 1024
ROW_VMEM_BYTES = 48 * 1024 * 1024

HIGHEST = lax.Precision.HIGHEST
_ARB = "arbitrary"


def _cparams(n, vmem=None):
    return pltpu.CompilerParams(dimension_semantics=(_ARB,) * n, vmem_limit_bytes=vmem)


def _rms(x, axis=-1):
    return x * lax.rsqrt(jnp.mean(x * x, axis=axis, keepdims=True) + EPS)


def _pack_panel(y, exact=False):
    hi, lo = y[:, :LANES], y[:, LANES:]
    if not exact:
        hi = hi.astype(BF16).astype(F32)
        lo = lo.astype(BF16).astype(F32)
    return lax.bitcast_convert_type(hi, U32) | (lax.bitcast_convert_type(lo, U32) >> 16)


def _unpack_panels(words):
    cols = []
    for w in words:
        cols.append(lax.bitcast_convert_type(w & jnp.uint32(0xFFFF0000), F32).astype(BF16))
        cols.append(lax.bitcast_convert_type(w << 16, F32).astype(BF16))
    return jnp.concatenate(cols, axis=-1)


def _prefetch_pos_rows(x4_hbm, buf, sems, n_pos):
    bi = pl.program_id(0)
    j = pl.program_id(1)
    nj = pl.num_programs(1)
    g = bi * nj + j
    slot = g % 2

    def copies(sl, b_, j_):
        return [pltpu.make_async_copy(x4_hbm.at[b_, :, n_pos * j_ + il, :], buf.at[sl, il], sems.at[sl])
                for il in range(n_pos)]

    @pl.when(g == 0)
    def _():
        for cp in copies(0, 0, 0):
            cp.start()

    @pl.when(g + 1 < pl.num_programs(0) * nj)
    def _():
        wrap = j + 1 == nj
        for cp in copies(1 - slot, jnp.where(wrap, bi + 1, bi), jnp.where(wrap, 0, j + 1)):
            cp.start()

    for cp in copies(slot, bi, j):
        cp.wait()
    return slot


def _to_lane_blocks(dst, src):
    for kb in range(dst.shape[0]):
        dst[kb] = src[:, kb * LANES:(kb + 1) * LANES]


def _adaln_kernel(c_ref, w_ref, b_ref, o_ref):
    c = c_ref[...]
    ca = c * jax.nn.sigmoid(c)
    o_ref[...] = jnp.dot(ca, w_ref[...], preferred_element_type=F32, precision=HIGHEST) + b_ref[...]


def _adaln(c, w_ada, b_ada):
    b, d = c.shape
    n = w_ada.shape[1] // d
    return pl.pallas_call(
        _adaln_kernel,
        grid=(n,),
        in_specs=[pl.BlockSpec((b, d), lambda j: (0, 0)),
                  pl.BlockSpec((d, d), lambda j: (0, j)),
                  pl.BlockSpec((1, d), lambda j: (0, j))],
        out_specs=pl.BlockSpec((b, d), lambda j: (0, j)),
        out_shape=jax.ShapeDtypeStruct((b, n * d), F32),
        compiler_params=_cparams(1),
        name="adaln",
    )(c, w_ada, b_ada.reshape(1, -1))


def _inproj_kernel(x4_hbm, x_ref, mod_ref, g_ref, w_ref, bqkv_ref, but_ref, q_ref, k_ref, v_ref, ut_ref,
                   xp_buf, wqkv_scr, wut_scr, sems):
    nc = ut_ref.shape[3]

    @pl.when((pl.program_id(0) == 0) & (pl.program_id(1) == 0))
    def _():
        wqkv_scr[...] = w_ref[:, :D_QKV].astype(BF16)
        wut_scr[...] = w_ref[:, D_QKV:].T.astype(BF16)

    slot = _prefetch_pos_rows(x4_hbm, xp_buf, sems, POS_PER_STEP)
    gain = g_ref[...]
    scale = 1.0 + mod_ref[0, 1:2, :]
    shift = mod_ref[0, 0:1, :]

    def norm_mod(x):
        return (_rms(x) * gain * scale + shift).astype(BF16)

    proj = jnp.dot(norm_mod(x_ref[0]), wqkv_scr[...], preferred_element_type=F32) + bqkv_ref[...]
    q_ref[0] = proj[:, :D_ATTN].astype(BF16)
    k_ref[0] = proj[:, D_ATTN:D_ATTN + D_KV].astype(BF16)
    v_ref[0] = proj[:, D_ATTN + D_KV:].astype(BF16)

    hs = jnp.concatenate([norm_mod(xp_buf[slot, il]) for il in range(POS_PER_STEP)], axis=0)
    ut = lax.dot_general(wut_scr[...], hs, (((1,), (1,)), ((), ())), preferred_element_type=F32) + but_ref[...]
    for il in range(POS_PER_STEP):
        piece = ut[:, il * nc:(il + 1) * nc].astype(BF16)
        ut_ref[0, :, il * SSM_GROUP:(il + 1) * SSM_GROUP, :] = piece.reshape(N_GROUPS, SSM_GROUP, nc)


def _inproj(x, mod, gain, w_in, b_in):
    b, s, d = x.shape
    nc = s // SSM_CHUNK
    rows = POS_PER_STEP * nc
    row = lambda bi, j: (bi, j, 0)
    const = lambda bi, j: (0, 0)
    return pl.pallas_call(
        _inproj_kernel,
        grid=(b, SSM_CHUNK // POS_PER_STEP),
        in_specs=[pl.BlockSpec(memory_space=pl.ANY),
                  pl.BlockSpec((1, rows, d), row),
                  pl.BlockSpec((1, 6, d), lambda bi, j: (bi, 0, 0)),
                  pl.BlockSpec((1, d), const),
                  pl.BlockSpec((d, D_QKV + D_SSM), const),
                  pl.BlockSpec((1, D_QKV), const),
                  pl.BlockSpec((D_SSM, 1), const)],
        out_specs=[pl.BlockSpec((1, rows, D_ATTN), row),
                   pl.BlockSpec((1, rows, D_KV), row),
                   pl.BlockSpec((1, rows, D_KV), row),
                   pl.BlockSpec((1, N_GROUPS, POS_PER_STEP * SSM_GROUP, nc), lambda bi, j: (bi, 0, j, 0))],
        out_shape=[jax.ShapeDtypeStruct((b, s, D_ATTN), BF16),
                   jax.ShapeDtypeStruct((b, s, D_KV), BF16),
                   jax.ShapeDtypeStruct((b, s, D_KV), BF16),
                   jax.ShapeDtypeStruct((b, N_GROUPS, SSM_ROW, nc), BF16)],
        scratch_shapes=[pltpu.VMEM((2, POS_PER_STEP, nc, d), F32), pltpu.VMEM((d, D_QKV), BF16),
                        pltpu.VMEM((D_SSM, d), BF16), pltpu.SemaphoreType.DMA((2,))],
        compiler_params=_cparams(2, ROW_VMEM_BYTES),
        name="inproj",
    )(x.reshape(b, nc, SSM_CHUNK, d), x, mod, gain.reshape(1, d), w_in, b_in[:D_QKV].reshape(1, D_QKV),
      b_in[D_QKV:].reshape(D_SSM, 1))


def _half_norm(x, low):
    sq = x * x
    s_lo = jnp.sum(jnp.where(low, sq, 0.0), axis=-1, keepdims=True)
    s_hi = jnp.sum(sq, axis=-1, keepdims=True) - s_lo
    inv = 1.0 / HEAD_DIM
    scale = jnp.where(low, lax.rsqrt(s_lo * inv + EPS), lax.rsqrt(s_hi * inv + EPS))
    return x * scale


def _attn_block(first, q, k_prev, k_cur, v_prev, v_cur, sinks_ref, qn, low, upper, rblk):
    no_prev = jnp.where(first, NEG_INF, 0.0)
    out_blocks = []
    for hk in range(N_KV_HEADS):
        qs = []
        for j in range(Q_PER_KV // 2):
            blk = hk * (Q_PER_KV // 2) + j
            qb = _half_norm(q[:, blk * LANES:(blk + 1) * LANES], low) * qn * (1.0 / math.sqrt(HEAD_DIM))
            qs.append(jnp.where(low, qb, 0.0))
            qs.append(jnp.where(low, 0.0, qb))
        qg = jnp.concatenate(qs, axis=0).astype(BF16)
        nt = (((1,), (1,)), ((), ()))
        s_prev = lax.dot_general(qg, k_prev[hk], nt, preferred_element_type=F32)
        s_cur = lax.dot_general(qg, k_cur[hk], nt, preferred_element_type=F32)
        s = jnp.where(upper, s_prev + no_prev, s_cur)
        sink = jnp.zeros((Q_PER_KV * BLOCK, 1), F32)
        for g in range(Q_PER_KV):
            sink = jnp.where(rblk == g, sinks_ref[hk * Q_PER_KV + g], sink)
        m = jnp.maximum(jnp.max(s, axis=-1, keepdims=True), sink)
        p = jnp.exp(s - m)
        den = jnp.sum(p, axis=-1, keepdims=True) + jnp.exp(sink - m)
        o = (jnp.dot(jnp.where(upper, p, 0.0).astype(BF16), v_prev[hk], preferred_element_type=F32)
             + jnp.dot(jnp.where(upper, 0.0, p).astype(BF16), v_cur[hk], preferred_element_type=F32)) / den
        for j in range(Q_PER_KV // 2):
            ev = o[(2 * j) * BLOCK:(2 * j + 1) * BLOCK]
            od = o[(2 * j + 1) * BLOCK:(2 * j + 2) * BLOCK]
            out_blocks.append(jnp.where(low, ev, od))
    return jnp.concatenate(out_blocks, axis=-1)


def _attn_kernel(sinks_ref, q_ref, k_ref, v_ref, qn_ref, kn_ref, on_ref, o_hbm, a_buf, sems, *, n_steps):
    step = pl.program_id(1)
    g = pl.program_id(0) * pl.num_programs(1) + step
    slot = g % 2
    cps = ATTN_ROWS // SSM_CHUNK
    nq = ATTN_ROWS // BLOCK

    def out_copies(sl, b_, s_):
        return [pltpu.make_async_copy(a_buf.at[sl, :, i, :], o_hbm.at[b_, i, pl.ds(s_ * cps, cps), :], sems.at[sl])
                for i in range(SSM_CHUNK)]

    @pl.when(g >= 2)
    def _():
        for cp in out_copies(slot, 0, 0):
            cp.wait()

    low = lax.broadcasted_iota(I32, (1, LANES), 1) < HEAD_DIM
    rows = Q_PER_KV * BLOCK
    upper = lax.broadcasted_iota(I32, (rows, BLOCK), 1) > lax.broadcasted_iota(I32, (rows, BLOCK), 0) % BLOCK
    rblk = lax.broadcasted_iota(I32, (rows, 1), 0) // BLOCK

    cur = pl.multiple_of(step * ATTN_ROWS, ATTN_ROWS)
    prev = pl.multiple_of(jnp.maximum(step * nq - 1, 0) * BLOCK, BLOCK)
    kall = jnp.concatenate([k_ref[0, pl.ds(prev, BLOCK), :], k_ref[0, pl.ds(cur, ATTN_ROWS), :]], axis=0).astype(F32)
    vall = jnp.concatenate([v_ref[0, pl.ds(prev, BLOCK), :], v_ref[0, pl.ds(cur, ATTN_ROWS), :]], axis=0).astype(F32)
    kall = _half_norm(kall, low) * kn_ref[...]
    kswap = pltpu.roll(kall, HEAD_DIM, axis=1)
    vswap = pltpu.roll(vall, HEAD_DIM, axis=1)
    k_dup = [jnp.where(low, kall, kswap).astype(BF16), jnp.where(low, kswap, kall).astype(BF16)]
    v_dup = [jnp.where(low, vall, vswap).astype(BF16), jnp.where(low, vswap, vall).astype(BF16)]
    blk = lambda a, i: [a[hk][i * BLOCK:(i + 1) * BLOCK] for hk in range(N_KV_HEADS)]

    for qb in range(nq):
        q = q_ref[0, qb * BLOCK:(qb + 1) * BLOCK, :].astype(F32)
        attn = _attn_block((step == 0) if qb == 0 else False, q, blk(k_dup, qb), blk(k_dup, qb + 1),
                           blk(v_dup, qb), blk(v_dup, qb + 1), sinks_ref, qn_ref[...], low, upper, rblk)
        attn = _rms(attn) * on_ref[...]
        cpb = BLOCK // SSM_CHUNK
        a_buf[slot, qb * cpb:(qb + 1) * cpb] = attn.reshape(cpb, SSM_CHUNK, D_ATTN)

    for cp in out_copies(slot, pl.program_id(0), step):
        cp.start()

    @pl.when(g == n_steps - 1)
    def _():
        for cp in out_copies(slot, 0, 0):
            cp.wait()
        if n_steps > 1:
            for cp in out_copies(1 - slot, 0, 0):
                cp.wait()


def _attention(q, k, v, sinks, q_norm, k_norm, out_norm):
    b, s, _ = q.shape
    tile2 = lambda g: jnp.tile(g.reshape(1, HEAD_DIM), (1, 2))
    cps = ATTN_ROWS // SSM_CHUNK
    return pl.pallas_call(
        functools.partial(_attn_kernel, n_steps=b * (s // ATTN_ROWS)),
        grid=(b, s // ATTN_ROWS),
        in_specs=[pl.BlockSpec(memory_space=pltpu.SMEM),
                  pl.BlockSpec((1, ATTN_ROWS, D_ATTN), lambda bi, n: (bi, n, 0)),
                  pl.BlockSpec((1, s, D_KV), lambda bi, n: (bi, 0, 0)),
                  pl.BlockSpec((1, s, D_KV), lambda bi, n: (bi, 0, 0)),
                  pl.BlockSpec((1, LANES), lambda bi, n: (0, 0)),
                  pl.BlockSpec((1, LANES), lambda bi, n: (0, 0)),
                  pl.BlockSpec((1, D_ATTN), lambda bi, n: (0, 0))],
        out_specs=pl.BlockSpec(memory_space=pl.ANY),
        out_shape=jax.ShapeDtypeStruct((b, SSM_CHUNK, s // SSM_CHUNK, D_ATTN), F32),
        scratch_shapes=[pltpu.VMEM((2, cps, SSM_CHUNK, D_ATTN), F32), pltpu.SemaphoreType.DMA((2,))],
        compiler_params=_cparams(2),
        name="attention",
    )(sinks, q, k, v, tile2(q_norm), tile2(k_norm), out_norm.reshape(1, D_ATTN))


def _cmul(ar, ai, br, bi):
    return ar * br - ai * bi, ar * bi + ai * br


def _ssm_param_kernel(*refs):
    for gi in range(refs[0].shape[0]):
        _ssm_param_group(gi, *refs)


def _ssm_param_group(gi, lam_ref, bre_ref, bim_ref, cre_ref, cim_ref, tt_ref, wz_ref, wyt_ref, cs_ref):
    f32dot = functools.partial(jnp.dot, preferred_element_type=F32, precision=HIGHEST)
    lr = lam_ref[gi, 0:1, :]
    li = lam_ref[gi, 1:2, :]
    dt = jnp.exp(lam_ref[gi, 2:3, :])
    rho = lr * dt
    th = li * dt
    imag_lane = lax.broadcasted_iota(I32, (1, LANES), 1) >= STATE

    kk = (lax.broadcasted_iota(I32, (N_POW, 1), 0) - (SSM_CHUNK - 1)).astype(F32)
    mag = jnp.exp(rho * kk)
    pw_r = mag * jnp.cos(th * kk)
    pw_i = mag * jnp.sin(th * kk)
    lb_r = pw_r[SSM_CHUNK:SSM_CHUNK + 1]
    lb_i = pw_i[SSM_CHUNK:SSM_CHUNK + 1]
    den = lr * lr + li * li
    coef_r = ((lb_r - 1.0) * lr + lb_i * li) / den
    coef_i = (lb_i * lr - (lb_r - 1.0) * li) / den

    eye = (lax.broadcasted_iota(I32, (SSM_GROUP, SSM_GROUP), 0)
           == lax.broadcasted_iota(I32, (SSM_GROUP, SSM_GROUP), 1)).astype(F32)
    lane_fold = (lax.broadcasted_iota(I32, (STATE, LANES), 1) % STATE
                 == lax.broadcasted_iota(I32, (STATE, LANES), 0)).astype(F32)

    def tile_pos(x):
        return jnp.concatenate([x] * SSM_CHUNK, axis=0)

    def power_rows(k_of_pos):
        idx = [k_of_pos(p) + (SSM_CHUNK - 1) for p in range(SSM_CHUNK)]
        rep = lambda t: jnp.concatenate([jnp.broadcast_to(t[r:r + 1], (SSM_GROUP, LANES)) for r in idx], axis=0)
        return rep(pw_r), rep(pw_i)

    def b_rows(b_ref):
        b2 = jnp.concatenate([b_ref[gi], b_ref[gi]], axis=0)
        return tile_pos(lax.dot_general(eye, b2, (((1,), (1,)), ((), ())), preferred_element_type=F32,
                                        precision=HIGHEST))

    def c_rows(c_ref):
        return tile_pos(f32dot(c_ref[gi], lane_fold))

    bbar_r, bbar_i = _cmul(coef_r, coef_i, b_rows(bre_ref), b_rows(bim_ref))
    c_r = c_rows(cre_ref)
    c_i = c_rows(cim_ref)

    a_r, a_i = _cmul(bbar_r, bbar_i, *power_rows(lambda p: -p))
    a2c = jnp.where(imag_lane, -a_i, a_r)
    m_r, m_i = _cmul(c_r, c_i, *power_rows(lambda p: p))
    bmc = jnp.where(imag_lane, m_i, m_r)
    tt = f32dot(bmc, a2c.T)
    causal = (lax.broadcasted_iota(I32, (SSM_ROW, 1), 0) // SSM_GROUP
              >= lax.broadcasted_iota(I32, (1, SSM_ROW), 1) // SSM_GROUP)
    tt_ref[gi] = jnp.where(causal, tt, 0.0).astype(BF16)

    w_r, w_i = _cmul(bbar_r, bbar_i, *power_rows(lambda p: SSM_CHUNK - 1 - p))
    wz_ref[gi, :, :LANES] = jnp.where(imag_lane, w_i, w_r).astype(BF16)
    wz_ref[gi, :, LANES:] = jnp.where(imag_lane, w_r, w_i).astype(BF16)

    y_r, y_i = _cmul(c_r, c_i, *power_rows(lambda p: p + 1))
    wyt_ref[gi] = jnp.where(imag_lane, -y_i, y_r).astype(BF16)

    cs_ref[gi, 0:1, :] = pw_r[N_POW - 1:N_POW]
    cs_ref[gi, 1:2, :] = jnp.where(imag_lane, pw_i[N_POW - 1:N_POW], -pw_i[N_POW - 1:N_POW])


def _ssm_params(lam_re, lam_im, log_dt, b_re, b_im, c_re, c_im):
    g = lam_re.shape[0]
    lam = jnp.stack([lam_re, lam_im, jnp.broadcast_to(log_dt[:, None], (g, STATE))], axis=1)
    lam = jnp.concatenate([lam, lam], axis=2)
    ng = SSM_GROUPS_PER_STEP
    blk = lambda *shape: pl.BlockSpec((ng,) + shape, lambda i: (i, 0, 0))
    return pl.pallas_call(
        _ssm_param_kernel,
        grid=(g // ng,),
        in_specs=[blk(3, LANES), blk(STATE, SSM_GROUP), blk(STATE, SSM_GROUP), blk(SSM_GROUP, STATE),
                  blk(SSM_GROUP, STATE)],
        out_specs=[blk(SSM_ROW, SSM_ROW), blk(SSM_ROW, SSM_ROW), blk(SSM_ROW, LANES), blk(2, LANES)],
        out_shape=[jax.ShapeDtypeStruct((g, SSM_ROW, SSM_ROW), BF16),
                   jax.ShapeDtypeStruct((g, SSM_ROW, SSM_ROW), BF16),
                   jax.ShapeDtypeStruct((g, SSM_ROW, LANES), BF16),
                   jax.ShapeDtypeStruct((g, 2, LANES), F32)],
        compiler_params=_cparams(1),
        name="ssm_params",
    )(lam, b_re, b_im, c_re, c_im)


def _ssm_kernel(ut_ref, tt_ref, wz_ref, wyt_ref, cs_ref, d_ref, yt_ref, z_scr, s_scr):
    batch, ng, _, nc = ut_ref.shape
    uts = [jnp.concatenate([ut_ref[b, gi] for b in range(batch)], axis=1) for gi in range(ng)]
    for gi in range(ng):
        z = lax.dot_general(uts[gi], wz_ref[gi], (((0,), (0,)), ((), ())), preferred_element_type=F32)
        _to_lane_blocks(z_scr.at[gi], z)
    c1 = [cs_ref[gi, 0:1, :] for gi in range(ng)]
    c2 = [cs_ref[gi, 1:2, :] for gi in range(ng)]

    def step(c, carry):
        rows = pl.ds(c, batch, stride=nc)
        out = []
        for gi in range(ng):
            s1, s2 = carry[gi]
            s_scr[gi, rows, :] = s1
            out.append((c1[gi] * s1 + c2[gi] * s2 + z_scr[gi, 0, rows, :],
                        c1[gi] * s2 - c2[gi] * s1 + z_scr[gi, 1, rows, :]))
        return tuple(out)

    zero = jnp.zeros((batch, LANES), F32)
    lax.fori_loop(0, nc, step, ((zero, zero),) * ng, unroll=8)
    for gi in range(ng):
        y = jnp.dot(tt_ref[gi], uts[gi], preferred_element_type=F32)
        y = y + lax.dot_general(wyt_ref[gi], s_scr[gi].astype(BF16), (((1,), (1,)), ((), ())),
                                preferred_element_type=F32)
        y = y + d_ref[gi] * uts[gi].astype(F32)
        for b in range(batch):
            yt_ref[b, gi] = y[:, b * nc:(b + 1) * nc]


def _ssm(ut, tt, wz, wyt, cs, d_skip):
    b, g, _, nc = ut.shape
    ng = SSM_GROUPS_PER_STEP
    d_col = jnp.tile(d_skip.reshape(g, 1, SSM_GROUP), (1, SSM_CHUNK, 1)).reshape(g, SSM_ROW, 1)
    blk = lambda *shape: pl.BlockSpec((ng,) + shape, lambda i: (i, 0, 0))
    act = pl.BlockSpec((b, ng, SSM_ROW, nc), lambda i: (0, i, 0, 0))
    return pl.pallas_call(
        _ssm_kernel,
        grid=(g // ng,),
        in_specs=[act, blk(SSM_ROW, SSM_ROW), blk(SSM_ROW, SSM_ROW), blk(SSM_ROW, LANES), blk(2, LANES),
                  blk(SSM_ROW, 1)],
        out_specs=act,
        out_shape=jax.ShapeDtypeStruct((b, g, SSM_ROW, nc), F32),
        scratch_shapes=[pltpu.VMEM((ng, SSM_ROW // LANES, b * nc, LANES), F32), pltpu.VMEM((ng, b * nc, LANES), F32)],
        compiler_params=_cparams(1),
        name="ssm",
    )(ut, tt, wz, wyt, cs, d_col)


def _post_kernel(x4_hbm, attn_ref, yt_ref, mod_ref, wglu_ref, bglu_ref, sn_ref, wout_f32_ref, nf_ref, wrt_ref, br_ref,
                 tri_ref, x1_ref, h2_ref, eidx_ref, wts_ref, lrank_ref, r0_ref, cnt_ref, carry_ref, xp_buf,
                 wglut_ref, wout_ref, wr_ref, sems):
    @pl.when((pl.program_id(0) == 0) & (pl.program_id(1) == 0))
    def _():
        carry_ref[...] = jnp.zeros_like(carry_ref)
        wglut_ref[...] = wglu_ref[...].T.astype(BF16)
        wout_ref[...] = wout_f32_ref[...].astype(BF16)
        wr_ref[...] = wrt_ref[...].T.astype(BF16)

    slot = _prefetch_pos_rows(x4_hbm, xp_buf, sems, POST_SUB * POST_POS)
    nc = attn_ref.shape[2]
    ts = POST_POS * nc
    d = x1_ref.shape[3]
    iota_e = lax.broadcasted_iota(I32, (N_EXPERTS, ts), 0).astype(F32)
    counts = []
    for sub in range(POST_SUB):
        pos = range(sub * POST_POS, (sub + 1) * POST_POS)
        lanes = slice(sub * ts, (sub + 1) * ts)
        yt = jnp.concatenate(
            [yt_ref[0, :, il * SSM_GROUP:(il + 1) * SSM_GROUP, :].reshape(D_SSM, nc) for il in pos], axis=1)
        g = jax.nn.gelu(yt)
        gate = jax.nn.sigmoid(jnp.dot(wglut_ref[...], g.astype(BF16), preferred_element_type=F32) + bglu_ref[...])
        ssm_t = _rms(g * gate, axis=0) * sn_ref[...]
        attn = attn_ref[0, sub * POST_POS:(sub + 1) * POST_POS].reshape(ts, D_ATTN)
        mixed = jnp.concatenate([attn.astype(BF16), ssm_t.T.astype(BF16)], axis=-1)
        o = jnp.dot(mixed, wout_ref[...], preferred_element_type=F32)
        x = jnp.concatenate([xp_buf[slot, il] for il in pos], axis=0)
        x1 = x + mod_ref[0, 2:3, :] * o
        x1_ref[0, sub * POST_POS:(sub + 1) * POST_POS] = x1.reshape(POST_POS, nc, d)
        h2 = _rms(x1) * nf_ref[...] * (1.0 + mod_ref[0, 4:5, :]) + mod_ref[0, 3:4, :]
        h2_ref[0, sub * POST_POS:(sub + 1) * POST_POS] = h2.astype(BF16).reshape(POST_POS, nc, d)

        logits = lax.dot_general(wr_ref[...], h2.astype(BF16), (((1,), (1,)), ((), ())),
                                 preferred_element_type=F32) + br_ref[...]
        l = logits
        idxs, vals = [], []
        for _ in range(TOP_K):
            m = jnp.max(l, axis=0, keepdims=True)
            idx = jnp.min(jnp.where(l == m, iota_e, float(N_EXPERTS)), axis=0, keepdims=True)
            idxs.append(idx)
            vals.append(m)
            l = jnp.where(iota_e == idx, -jnp.inf, l)
        es = [jnp.exp(v - vals[0]) for v in vals]
        tot = es[0] + es[1] + es[2] + es[3]
        member = jnp.zeros((N_EXPERTS, ts), F32)
        for idx in idxs:
            member = member + (iota_e == idx).astype(F32)
        before = jnp.dot(member.astype(BF16), tri_ref[...], preferred_element_type=F32)
        for k in range(TOP_K):
            eidx_ref[k:k + 1, lanes] = idxs[k].astype(I32)
            wts_ref[k:k + 1, lanes] = es[k] / tot
            lrank_ref[k:k + 1, lanes] = jnp.sum(jnp.where(iota_e == idxs[k], before, 0.0), axis=0,
                                                keepdims=True).astype(I32)
        counts.append(jnp.sum(member, axis=1, keepdims=True))

    carry = carry_ref[...]
    for sub in range(POST_SUB):
        r0_ref[sub] = carry.astype(I32)
        carry = carry + counts[sub]
    carry_ref[...] = carry
    cnt_ref[...] = carry.astype(I32)


def _post(x, attn, yt, mod, w_glu, b_glu, ssm_norm, w_out, norm_ffn, w_router, b_router):
    b, s, d = x.shape
    nc = s // SSM_CHUNK
    ts = POST_POS * nc
    npos = POST_SUB * POST_POS
    nt = SSM_CHUNK // npos
    t = b * s
    pm = lambda bi, j: (bi, j, 0, 0)
    const = lambda bi, j: (0, 0)
    tok = lambda bi, j: (0, bi * nt + j)
    tri = (lax.broadcasted_iota(I32, (ts, ts), 0) < lax.broadcasted_iota(I32, (ts, ts), 1)).astype(BF16)
    col = lambda a: a.reshape(-1, 1)
    return pl.pallas_call(
        _post_kernel,
        grid=(b, nt),
        in_specs=[pl.BlockSpec(memory_space=pl.ANY),
                  pl.BlockSpec((1, npos, nc, D_ATTN), pm),
                  pl.BlockSpec((1, N_GROUPS, npos * SSM_GROUP, nc), lambda bi, j: (bi, 0, j, 0)),
                  pl.BlockSpec((1, 6, d), lambda bi, j: (bi, 0, 0)),
                  pl.BlockSpec((D_SSM, D_SSM), const),
                  pl.BlockSpec((D_SSM, 1), const),
                  pl.BlockSpec((D_SSM, 1), const),
                  pl.BlockSpec((d, d), const),
                  pl.BlockSpec((1, d), const),
                  pl.BlockSpec((d, N_EXPERTS), const),
                  pl.BlockSpec((N_EXPERTS, 1), const),
                  pl.BlockSpec((ts, ts), const)],
        out_specs=[pl.BlockSpec((1, npos, nc, d), pm),
                   pl.BlockSpec((1, npos, nc, d), pm),
                   pl.BlockSpec((TOP_K, POST_SUB * ts), tok),
                   pl.BlockSpec((TOP_K, POST_SUB * ts), tok),
                   pl.BlockSpec((TOP_K, POST_SUB * ts), tok),
                   pl.BlockSpec((POST_SUB, N_EXPERTS, 1), lambda bi, j: (bi * nt + j, 0, 0)),
                   pl.BlockSpec((N_EXPERTS, 1), const)],
        out_shape=[jax.ShapeDtypeStruct((b, SSM_CHUNK, nc, d), F32),
                   jax.ShapeDtypeStruct((b, SSM_CHUNK, nc, d), BF16),
                   jax.ShapeDtypeStruct((TOP_K, t), I32),
                   jax.ShapeDtypeStruct((TOP_K, t), F32),
                   jax.ShapeDtypeStruct((TOP_K, t), I32),
                   jax.ShapeDtypeStruct((b * nt * POST_SUB, N_EXPERTS, 1), I32),
                   jax.ShapeDtypeStruct((N_EXPERTS, 1), I32)],
        scratch_shapes=[pltpu.VMEM((N_EXPERTS, 1), F32), pltpu.VMEM((2, npos, nc, d), F32),
                        pltpu.VMEM((D_SSM, D_SSM), BF16), pltpu.VMEM((d, d), BF16), pltpu.VMEM((N_EXPERTS, d), BF16),
                        pltpu.SemaphoreType.DMA((2,))],
        compiler_params=_cparams(2, ROW_VMEM_BYTES),
        name="post",
    )(x.reshape(b, nc, SSM_CHUNK, d), attn, yt, mod, w_glu, col(b_glu), col(ssm_norm), w_out,
      norm_ffn.reshape(1, -1), w_router, col(b_router), tri)


def _route_kernel(eidx_ref, lrank_ref, r0_ref, cnt_ref, ls_ref, tab_ref, te_ref, nv_ref, nx_ref, pad_ref):
    cnt = cnt_ref[...]
    tiles = (cnt + (RUN - 1 + FFN_ROWS - 1)) // FFN_ROWS
    er = lax.broadcasted_iota(I32, (N_EXPERTS, N_EXPERTS), 0)
    ec = lax.broadcasted_iota(I32, (N_EXPERTS, N_EXPERTS), 1)
    ltri = (ec < er).astype(BF16)

    def excl_cumsum(v):
        vb = jnp.broadcast_to(v.astype(F32), (N_EXPERTS, LANES)).astype(BF16)
        return jnp.dot(ltri, vb, preferred_element_type=F32)[:, 0:1].astype(I32)

    start_t = excl_cumsum(tiles)
    end_t = start_t + tiles
    start = start_t * FFN_ROWS
    pad_ref[...] = start + cnt

    nb = r0_ref.shape[0]
    ts = eidx_ref.shape[1] // nb
    iota_e = lax.broadcasted_iota(I32, (N_EXPERTS, ts), 0)
    iota_t = lax.broadcasted_iota(I32, (N_EXPERTS, TABW), 0)
    lane = lax.broadcasted_iota(I32, (1, TABW), 1)

    def block(b, carry):
        lanes = pl.ds(pl.multiple_of(b * ts, ts), ts)
        sels = [iota_e == eidx_ref[k:k + 1, lanes] for k in range(TOP_K)]
        member = sels[0].astype(I32) + sels[1].astype(I32) + sels[2].astype(I32) + sels[3].astype(I32)
        units = lax.shift_right_logical(jnp.sum(member, axis=1, keepdims=True) + (RUN - 1), RUN_SHIFT)
        u0 = excl_cumsum(units)
        for k in range(TOP_K):
            first = jnp.sum(jnp.where(sels[k], u0, 0), axis=0, keepdims=True)
            ls_ref[k:k + 1, lanes] = first * RUN + lrank_ref[k:k + 1, lanes]
        n_big = lax.shift_right_logical(units, 1)
        n_small = units & 1
        slot0 = start + r0_ref[b]

        def chunk_rows(idx, counts):
            c0 = excl_cumsum(counts)
            sel = iota_t == jnp.sum((idx >= c0 + counts).astype(I32), axis=0, keepdims=True)
            pick = lambda v: jnp.sum(jnp.where(sel, v, 0), axis=0, keepdims=True)
            j = idx - pick(c0)
            return pick(slot0), pick(u0), j, pick(n_big), idx < jnp.max(c0 + counts, axis=0, keepdims=True)

        s_b, u_b, j_b, _, ok_b = chunk_rows(lane, n_big)
        s_s, u_s, _, nb_s, ok_s = chunk_rows(lane - SMALL0, n_small)
        small = lane >= SMALL0
        slot = jnp.where(small, s_s + nb_s * BIG, s_b + j_b * BIG)
        local = jnp.where(small, (u_s + 2 * nb_s) * RUN, (u_b + 2 * j_b) * RUN)
        ok = (small & ok_s) | (jnp.logical_not(small) & ok_b)
        counts = jnp.where(lane == TABW - 2, jnp.sum(n_big, axis=0, keepdims=True),
                           jnp.sum(n_small, axis=0, keepdims=True))
        tab_ref[b, 0:1, :] = jnp.where(lane >= TABW - 2, counts, jnp.where(ok, slot, -1))
        tab_ref[b, 1:2, :] = jnp.where(ok, local, 0)
        return carry

    lax.fori_loop(0, nb, block, 0, unroll=2)

    nv = jnp.max(end_t, axis=0, keepdims=True)
    width = te_ref.shape[1]
    ti = jnp.minimum(lax.broadcasted_iota(I32, (N_EXPERTS, width), 1), nv - 1)
    te = jnp.minimum(jnp.sum((ti >= end_t).astype(I32), axis=0, keepdims=True), N_EXPERTS - 1)
    te_ref[...] = te
    nv_ref[...] = jnp.broadcast_to(nv, nv_ref.shape)
    ie = lax.broadcasted_iota(I32, (N_EXPERTS, width), 0)
    own_end = jnp.sum(jnp.where(ie == te, end_t, 0), axis=0, keepdims=True)
    nxt = jnp.minimum(jnp.sum((own_end >= end_t).astype(I32), axis=0, keepdims=True), N_EXPERTS - 1)
    nx_ref[...] = jnp.where(own_end < nv, nxt, -1)


def _route(eidx, lrank, r0, cnt, n_tiles):
    t = eidx.shape[1]
    nb = r0.shape[0]
    width = -(-n_tiles // LANES) * LANES
    return pl.pallas_call(
        _route_kernel,
        out_shape=[jax.ShapeDtypeStruct((TOP_K, t), I32),
                   jax.ShapeDtypeStruct((nb, 2, TABW), I32),
                   jax.ShapeDtypeStruct((1, width), I32),
                   jax.ShapeDtypeStruct((1, LANES), I32),
                   jax.ShapeDtypeStruct((1, width), I32),
                   jax.ShapeDtypeStruct((N_EXPERTS, 1), I32)],
        name="route",
    )(eidx, lrank, r0, cnt)


def _for_chunk_pairs(n, fn):
    def body(i, carry):
        fn(2 * i, 0)

        @pl.when(2 * i + 1 < n)
        def _():
            fn(2 * i + 1, 1)
        return carry
    lax.fori_loop(0, lax.shift_right_logical(n + 1, 1), body, 0)


def _for_block_chunks(tab_ref, blk, fn):
    base = blk * (2 * TABW)
    for first, count_lane, n_rows in ((0, TABW - 2, BIG), (SMALL0, TABW - 1, RUN)):
        def visit(c, parity, first=first, n_rows=n_rows):
            fn(tab_ref[base + first + c], pl.multiple_of(tab_ref[base + TABW + first + c], RUN), n_rows, parity)
        _for_chunk_pairs(tab_ref[base + count_lane], visit)


def _local_rows(ts):
    return ts * TOP_K + N_EXPERTS * RUN


def _dispatch_kernel(tab_ref, pad_ref, nvt_ref, h_ref, ls_ref, xs_ref, buf, zbuf, sems, zsem):
    b = pl.program_id(0)
    slot = b % 2
    ts = h_ref.shape[0]
    local = buf.shape[2]

    def chunk_copy(sl, slot_row, local_row, n):
        return pltpu.make_async_copy(buf.at[sl, :, pl.ds(local_row, n), :], xs_ref.at[:, pl.ds(slot_row, n), :],
                                     sems.at[sl])

    @pl.when(b == 0)
    def _():
        zbuf[...] = jnp.zeros_like(zbuf)
        zrows = zbuf.shape[1]
        zero = lambda row: pltpu.make_async_copy(zbuf, xs_ref.at[:, pl.ds(row, zrows), :], zsem)
        for phase in range(3):
            for e in range(phase, N_EXPERTS, 3):
                zero(pad_ref[e]).start()
            for e in range(phase, N_EXPERTS, 3):
                zero(pad_ref[e]).wait()
        ztile = lambda i: pltpu.make_async_copy(zbuf.at[:, pl.ds(0, FFN_ROWS), :],
                                                xs_ref.at[:, pl.ds((nvt_ref[0] + i) * FFN_ROWS, FFN_ROWS), :], zsem)

        def tail_start(i, carry):
            ztile(i).start()
            return carry

        def tail_wait(i, carry):
            ztile(i).wait()
            return carry
        lax.fori_loop(0, nvt_ref[1] - nvt_ref[0], tail_start, 0)
        lax.fori_loop(0, nvt_ref[1] - nvt_ref[0], tail_wait, 0)

    r = lax.broadcasted_iota(I32, (local, ts), 0)
    hit = (r == ls_ref[0:1, :]) | (r == ls_ref[1:2, :]) | (r == ls_ref[2:3, :]) | (r == ls_ref[3:4, :])
    hit = hit.astype(BF16)
    for pb in range(PANELS):
        srt = jnp.dot(hit, h_ref[:, pb * PANEL_COLS:(pb + 1) * PANEL_COLS], preferred_element_type=F32)
        buf[slot, pb] = _pack_panel(srt, exact=True)

    @pl.when(b > 0)
    def _():
        _for_block_chunks(tab_ref, b - 1, lambda s, l, n, p: chunk_copy(1 - slot, s, l, n).wait())

    _for_block_chunks(tab_ref, b, lambda s, l, n, p: chunk_copy(slot, s, l, n).start(priority=p))

    @pl.when(b == pl.num_programs(0) - 1)
    def _():
        _for_block_chunks(tab_ref, b, lambda s, l, n, p: chunk_copy(slot, s, l, n).wait())


def _dispatch(tab, pad, nvt, h2, ls, n_rows):
    t, d = h2.shape
    nb = tab.shape[0] // (2 * TABW)
    ts = t // nb
    return pl.pallas_call(
        _dispatch_kernel,
        grid_spec=pltpu.PrefetchScalarGridSpec(
            num_scalar_prefetch=3,
            grid=(nb,),
            in_specs=[pl.BlockSpec((ts, d), lambda i, *_: (i, 0)),
                      pl.BlockSpec((TOP_K, ts), lambda i, *_: (0, i))],
            out_specs=pl.BlockSpec(memory_space=pl.ANY),
            scratch_shapes=[pltpu.VMEM((2, PANELS, _local_rows(ts), LANES), U32),
                            pltpu.VMEM((PANELS, FFN_ROWS + RUN, LANES), U32),
                            pltpu.SemaphoreType.DMA((2,)), pltpu.SemaphoreType.DMA],
        ),
        out_shape=jax.ShapeDtypeStruct((PANELS, n_rows, LANES), U32),
        compiler_params=_cparams(1, ROW_VMEM_BYTES),
        name="dispatch",
    )(tab, pad, nvt, h2, ls)


def _ffn_kernel(te_ref, nv_ref, nx_ref, xs_ref, wgu_hbm, bgu_ref, wd_hbm, bd_ref, perm_ref, ys_ref,
                wgu_stage, wd_stage, wg_scr, wu_scr, wd_scr, bg_scr, bu_scr, sems):
    p = pl.program_id(0)
    t0 = 2 * p
    t1 = t0 + 1
    e0 = te_ref[t0]
    e1 = te_ref[t1]
    v0 = t0 < nv_ref[0]
    v1 = t1 < nv_ref[0]
    new0 = (p == 0) | (e0 != te_ref[jnp.maximum(t0 - 1, 0)])
    same = v1 & (e1 == e0)

    def stage_copies(e):
        return (pltpu.make_async_copy(wgu_hbm.at[e], wgu_stage, sems.at[0]),
                pltpu.make_async_copy(wd_hbm.at[e], wd_stage, sems.at[1]))

    def load_expert(t, first):
        e = te_ref[t]
        if first:
            @pl.when(p == 0)
            def _():
                for cp in stage_copies(e):
                    cp.start()

        for cp in stage_copies(e):
            cp.wait()
        bias = bgu_ref[e]
        for c in range(2 * D_FF // PERM):
            cols = slice(c * PERM, (c + 1) * PERM)
            half = slice(c * (PERM // 2), (c + 1) * (PERM // 2))
            w = wgu_stage[:, cols].astype(BF16)
            pw = jnp.dot(w, perm_ref[...], preferred_element_type=F32).astype(BF16)
            wg_scr[:, half] = pw[:, :PERM // 2]
            wu_scr[:, half] = pw[:, PERM // 2:]
            b1 = bias[:, cols].astype(BF16)
            r1 = bias[:, cols] - b1.astype(F32)
            b2 = r1.astype(BF16)
            b3 = (r1 - b2.astype(F32)).astype(BF16)
            terms = jnp.concatenate([b1, b2, b3, jnp.zeros((5, PERM), BF16)], axis=0)
            pb = jnp.sum(jnp.dot(terms, perm_ref[...], preferred_element_type=F32), axis=0, keepdims=True)
            bg_scr[:, half] = pb[:, :PERM // 2]
            bu_scr[:, half] = pb[:, PERM // 2:]
        wd_scr[...] = wd_stage[...].astype(BF16)

        @pl.when(nx_ref[t] >= 0)
        def _():
            for cp in stage_copies(nx_ref[t]):
                cp.start()

    def run(lo, n, e):
        x = _unpack_panels([xs_ref[pb, lo:lo + n, :] for pb in range(PANELS)])
        gate = jnp.dot(x, wg_scr[...], preferred_element_type=F32) + bg_scr[...]
        up = jnp.dot(x, wu_scr[...], preferred_element_type=F32) + bu_scr[...]
        gate = jnp.minimum(gate, SWIGLU_LIMIT)
        up = jnp.clip(up, -SWIGLU_LIMIT, SWIGLU_LIMIT)
        act = ((up + 1.0) * (gate * jax.nn.sigmoid(SWIGLU_ALPHA * gate))).astype(BF16)
        bd = bd_ref[e]
        for pb in range(PANELS):
            cols = slice(pb * PANEL_COLS, (pb + 1) * PANEL_COLS)
            y = jnp.dot(act, wd_scr[:, cols], preferred_element_type=F32) + bd[:, cols]
            ys_ref[pb, lo:lo + n, :] = _pack_panel(y)

    @pl.when(v0 & new0)
    def _():
        load_expert(t0, True)

    @pl.when(same)
    def _():
        run(0, 2 * FFN_ROWS, e0)

    @pl.when(v0 & jnp.logical_not(same))
    def _():
        run(0, FFN_ROWS, e0)

    @pl.when(v1 & jnp.logical_not(same))
    def _():
        load_expert(t1, False)
        run(FFN_ROWS, FFN_ROWS, e1)

    @pl.when(v0 & jnp.logical_not(v1))
    def _():
        ys_ref[:, FFN_ROWS:, :] = xs_ref[:, FFN_ROWS:, :]


def _ffn(te, nv, nx, xs, w_gate_up, bgu, w_down, bd, n_tiles):
    d = D_MODEL
    pair = lambda i, te, nv, nx: (0, jnp.minimum(i, lax.shift_right_logical(nv[0] - 1, 1)), 0)
    whole = lambda i, te, nv, nx: (0, 0, 0)
    r = lax.broadcasted_iota(I32, (PERM, PERM), 0)
    c = lax.broadcasted_iota(I32, (PERM, PERM), 1)
    perm = (r == jnp.where(c < PERM // 2, 2 * c, 2 * (c - PERM // 2) + 1)).astype(BF16)
    return pl.pallas_call(
        _ffn_kernel,
        grid_spec=pltpu.PrefetchScalarGridSpec(
            num_scalar_prefetch=3,
            grid=(n_tiles // 2,),
            in_specs=[pl.BlockSpec((PANELS, 2 * FFN_ROWS, LANES), pair),
                      pl.BlockSpec(memory_space=pl.ANY),
                      pl.BlockSpec((N_EXPERTS, 1, 2 * D_FF), whole),
                      pl.BlockSpec(memory_space=pl.ANY),
                      pl.BlockSpec((N_EXPERTS, 1, d), whole),
                      pl.BlockSpec((PERM, PERM), lambda i, te, nv, nx: (0, 0))],
            out_specs=pl.BlockSpec((PANELS, 2 * FFN_ROWS, LANES), pair),
            scratch_shapes=[pltpu.VMEM((d, 2 * D_FF), F32), pltpu.VMEM((D_FF, d), F32),
                            pltpu.VMEM((d, D_FF), BF16), pltpu.VMEM((d, D_FF), BF16), pltpu.VMEM((D_FF, d), BF16),
                            pltpu.VMEM((1, D_FF), F32), pltpu.VMEM((1, D_FF), F32),
                            pltpu.SemaphoreType.DMA((2,))],
        ),
        out_shape=jax.ShapeDtypeStruct(xs.shape, U32),
        input_output_aliases={3: 0},
        compiler_params=_cparams(1, FFN_VMEM_BYTES),
        name="ffn",
    )(te, nv, nx, xs, w_gate_up, bgu, w_down, bd, perm)


def _combine_kernel(tab_ref, x1_ref, ls_ref, w_ref, mod_ref, ys_ref, o4_hbm, ybuf, ob_buf, sems, osems, *, n_steps):
    jj = pl.program_id(1)
    g = pl.program_id(0) * pl.num_programs(1) + jj
    slot = g % 2
    nc = x1_ref.shape[2]
    tt = POST_POS * nc
    d = x1_ref.shape[3]
    local = ybuf.shape[3]
    npos = COMB_SUB * POST_POS

    def for_step_chunks(step, sl, fn):
        for sub in range(COMB_SUB):
            def visit(s, l, n, p, sub=sub):
                fn(pltpu.make_async_copy(ys_ref.at[:, pl.ds(s, n), :], ybuf.at[sl, sub, :, pl.ds(l, n), :],
                                         sems.at[sl]), p)
            _for_block_chunks(tab_ref, step * COMB_SUB + sub, visit)

    @pl.when(g == 0)
    def _():
        ybuf[...] = jnp.zeros_like(ybuf)
        for_step_chunks(0, 0, lambda cp, p: cp.start(priority=p))

    @pl.when(g + 1 < n_steps)
    def _():
        for_step_chunks(g + 1, 1 - slot, lambda cp, p: cp.start(priority=p))

    for_step_chunks(g, slot, lambda cp, p: cp.wait())

    def out_copies(sl, b_, j_):
        return [pltpu.make_async_copy(ob_buf.at[sl, il], o4_hbm.at[b_, :, npos * j_ + il, :], osems.at[sl])
                for il in range(npos)]

    @pl.when(g >= 2)
    def _():
        for cp in out_copies(slot, 0, 0):
            cp.wait()

    r = lax.broadcasted_iota(I32, (tt, local), 1).astype(F32)
    to_cols = lambda a: jnp.concatenate([a, jnp.zeros_like(a)], axis=0).T
    for sub in range(COMB_SUB):
        lanes = slice(sub * tt, (sub + 1) * tt)
        ls_c = to_cols(ls_ref[:, lanes].astype(F32))
        w_c = to_cols(w_ref[:, lanes])
        wm = jnp.zeros((tt, local), F32)
        for k in range(TOP_K):
            wm = jnp.where(r == ls_c[:, k:k + 1], w_c[:, k:k + 1], wm)
        y = _unpack_panels([ybuf[slot, sub, pb] for pb in range(PANELS)])
        acc = jnp.dot(wm.astype(BF16), y, preferred_element_type=F32)
        out = x1_ref[0, sub * POST_POS:(sub + 1) * POST_POS].reshape(tt, d) + mod_ref[0, 5:6, :] * acc
        for il in range(POST_POS):
            ob_buf[slot, sub * POST_POS + il] = out[il * nc:(il + 1) * nc]

    for cp in out_copies(slot, pl.program_id(0), jj):
        cp.start()

    @pl.when(g == n_steps - 1)
    def _():
        for cp in out_copies(slot, 0, 0):
            cp.wait()
        if n_steps > 1:
            for cp in out_copies(1 - slot, 0, 0):
                cp.wait()


def _combine(tab, x1, ls, wts, mod, ys):
    b, _, nc, d = x1.shape
    s = SSM_CHUNK * nc
    tt = POST_POS * nc
    npos = COMB_SUB * POST_POS
    nt = SSM_CHUNK // npos
    o4 = pl.pallas_call(
        functools.partial(_combine_kernel, n_steps=b * nt),
        grid_spec=pltpu.PrefetchScalarGridSpec(
            num_scalar_prefetch=1,
            grid=(b, nt),
            in_specs=[pl.BlockSpec((1, npos, nc, d), lambda bi, j, *_: (bi, j, 0, 0)),
                      pl.BlockSpec((TOP_K, COMB_SUB * tt), lambda bi, j, *_: (0, bi * nt + j)),
                      pl.BlockSpec((TOP_K, COMB_SUB * tt), lambda bi, j, *_: (0, bi * nt + j)),
                      pl.BlockSpec((1, 6, d), lambda bi, j, *_: (bi, 0, 0)),
                      pl.BlockSpec(memory_space=pl.ANY)],
            out_specs=pl.BlockSpec(memory_space=pl.ANY),
            scratch_shapes=[pltpu.VMEM((2, COMB_SUB, PANELS, _local_rows(tt), LANES), U32),
                            pltpu.VMEM((2, npos, nc, d), F32),
                            pltpu.SemaphoreType.DMA((2,)), pltpu.SemaphoreType.DMA((2,))],
        ),
        out_shape=jax.ShapeDtypeStruct((b, nc, SSM_CHUNK, d), F32),
        compiler_params=_cparams(2),
        name="combine",
    )(tab, x1, ls, wts, mod, ys)
    return o4.reshape(b, s, d)


def kernel(x, c, w_ada, b_ada, norm_mix, w_in, b_in, q_norm, k_norm, sinks, lam_re, lam_im, log_dt, b_re, b_im,
           c_re, c_im, d_skip, w_glu, b_glu, attn_out_norm, ssm_out_norm, w_out, norm_ffn, w_router, b_router,
           w_gate_up, b_gate_up, w_down, b_down):
    b, s, d = x.shape
    t = b * s
    depth = w_ada.shape[0]
    n_tiles = -(-(t * TOP_K + N_EXPERTS * (RUN - 1 + FFN_ROWS - 1)) // FFN_ROWS)
    n_tiles += n_tiles % 2
    n_alloc = n_tiles + 2
    for l in range(depth):
        mod = _adaln(c, w_ada[l], b_ada[l]).reshape(b, 6, d)
        q, k, v, ut = _inproj(x, mod, norm_mix[l], w_in[l], b_in[l])
        attn = _attention(q, k, v, sinks[l], q_norm[l], k_norm[l], attn_out_norm[l])
        tt, wz, wyt, cs = _ssm_params(lam_re[l], lam_im[l], log_dt[l], b_re[l], b_im[l], c_re[l], c_im[l])
        yt = _ssm(ut, tt, wz, wyt, cs, d_skip[l])
        x1, h2, eidx, wts, lrank, r0, cnt = _post(x, attn, yt, mod, w_glu[l], b_glu[l], ssm_out_norm[l], w_out[l],
                                                  norm_ffn[l], w_router[l], b_router[l])
        ls, tab, te, nv, nx, pad = _route(eidx, lrank, r0, cnt, n_tiles)
        tab = tab.reshape(-1)
        nvt = jnp.stack([nv[0, 0], jnp.int32(n_alloc)])
        xs = _dispatch(tab, pad.reshape(-1), nvt, h2.reshape(t, d), ls, n_alloc * FFN_ROWS)
        ys = _ffn(te[0, :n_tiles], nv[0, :1], nx[0, :n_tiles], xs, w_gate_up[l], b_gate_up[l][:, None, :],
                  w_down[l], b_down[l][:, None, :], n_tiles)
        x = _combine(tab, x1, ls, wts, mod, ys)
    return x
```

```python
import functools
import math

import jax
import jax.numpy as jnp
from jax import lax
from jax.experimental import pallas as pl
from jax.experimental.pallas import tpu as pltpu

F32 = jnp.float32
BF16 = jnp.bfloat16
U32 = jnp.uint32
I32 = jnp.int32

D_MODEL = 1024
HEAD_DIM = 64
N_HEADS = 8
N_KV_HEADS = 2
Q_PER_KV = N_HEADS // N_KV_HEADS
D_ATTN = N_HEADS * HEAD_DIM
D_KV = N_KV_HEADS * HEAD_DIM
D_QKV = D_ATTN + 2 * D_KV
WINDOW = 128
BLOCK = 128
D_SSM = D_MODEL - D_ATTN
SSM_GROUP = 16
N_GROUPS = D_SSM // SSM_GROUP
STATE = 64
N_EXPERTS = 32
TOP_K = 4
D_FF = D_MODEL
SWIGLU_LIMIT = 7.0
SWIGLU_ALPHA = 1.702
EPS = 1e-6
NEG_INF = -1e30

LANES = 128
SSM_CHUNK = 16
SSM_ROW = SSM_CHUNK * SSM_GROUP
N_POW = 2 * SSM_CHUNK
PANEL_COLS = 2 * LANES
PANELS = D_MODEL // PANEL_COLS

SSM_GROUPS_PER_STEP = 8
POS_PER_STEP = 8
ATTN_ROWS = 512
POST_POS = 2
POST_SUB = 4
COMB_SUB = 2
FFN_ROWS = 256
RUN = 8
RUN_SHIFT = 3
BIG = 2 * RUN
TABW = 128
SMALL0 = 80
PERM = 256
FFN_VMEM_BYTES = 40 * 1024 * 1024
ROW_VMEM_BYTES = 48 * 1024 * 1024

HIGHEST = lax.Precision.HIGHEST
_ARB = "arbitrary"


def _cparams(n, vmem=None):
    return pltpu.CompilerParams(dimension_semantics=(_ARB,) * n, vmem_limit_bytes=vmem)


def _rms(x, axis=-1):
    return x * lax.rsqrt(jnp.mean(x * x, axis=axis, keepdims=True) + EPS)


def _pack_panel(y, exact=False):
    hi, lo = y[:, :LANES], y[:, LANES:]
    if not exact:
        hi = hi.astype(BF16).astype(F32)
        lo = lo.astype(BF16).astype(F32)
    return lax.bitcast_convert_type(hi, U32) | (lax.bitcast_convert_type(lo, U32) >> 16)


def _unpack_panels(words):
    cols = []
    for w in words:
        cols.append(lax.bitcast_convert_type(w & jnp.uint32(0xFFFF0000), F32).astype(BF16))
        cols.append(lax.bitcast_convert_type(w << 16, F32).astype(BF16))
    return jnp.concatenate(cols, axis=-1)


def _prefetch_pos_rows(x4_hbm, buf, sems, n_pos):
    bi = pl.program_id(0)
    j = pl.program_id(1)
    nj = pl.num_programs(1)
    g = bi * nj + j
    slot = g % 2

    def copies(sl, b_, j_):
        return [pltpu.make_async_copy(x4_hbm.at[b_, :, n_pos * j_ + il, :], buf.at[sl, il], sems.at[sl])
                for il in range(n_pos)]

    @pl.when(g == 0)
    def _():
        for cp in copies(0, 0, 0):
            cp.start()

    @pl.when(g + 1 < pl.num_programs(0) * nj)
    def _():
        wrap = j + 1 == nj
        for cp in copies(1 - slot, jnp.where(wrap, bi + 1, bi), jnp.where(wrap, 0, j + 1)):
            cp.start()

    for cp in copies(slot, bi, j):
        cp.wait()
    return slot


def _to_lane_blocks(dst, src):
    for kb in range(dst.shape[0]):
        dst[kb] = src[:, kb * LANES:(kb + 1) * LANES]


def _adaln_kernel(c_ref, w_ref, b_ref, o_ref):
    c = c_ref[...]
    ca = c * jax.nn.sigmoid(c)
    o_ref[...] = jnp.dot(ca, w_ref[...], preferred_element_type=F32, precision=HIGHEST) + b_ref[...]


def _adaln(c, w_ada, b_ada):
    b, d = c.shape
    n = w_ada.shape[1] // d
    return pl.pallas_call(
        _adaln_kernel,
        grid=(n,),
        in_specs=[pl.BlockSpec((b, d), lambda j: (0, 0)),
                  pl.BlockSpec((d, d), lambda j: (0, j)),
                  pl.BlockSpec((1, d), lambda j: (0, j))],
        out_specs=pl.BlockSpec((b, d), lambda j: (0, j)),
        out_shape=jax.ShapeDtypeStruct((b, n * d), F32),
        compiler_params=_cparams(1),
        name="adaln",
    )(c, w_ada, b_ada.reshape(1, -1))


def _inproj_kernel(x4_hbm, x_ref, mod_ref, g_ref, w_ref, bqkv_ref, but_ref, q_ref, k_ref, v_ref, ut_ref,
                   xp_buf, wqkv_scr, wut_scr, sems):
    nc = ut_ref.shape[3]

    @pl.when((pl.program_id(0) == 0) & (pl.program_id(1) == 0))
    def _():
        wqkv_scr[...] = w_ref[:, :D_QKV].astype(BF16)
        wut_scr[...] = w_ref[:, D_QKV:].T.astype(BF16)

    slot = _prefetch_pos_rows(x4_hbm, xp_buf, sems, POS_PER_STEP)
    gain = g_ref[...]
    scale = 1.0 + mod_ref[0, 1:2, :]
    shift = mod_ref[0, 0:1, :]

    def norm_mod(x):
        return (_rms(x) * gain * scale + shift).astype(BF16)

    proj = jnp.dot(norm_mod(x_ref[0]), wqkv_scr[...], preferred_element_type=F32) + bqkv_ref[...]
    q_ref[0] = proj[:, :D_ATTN].astype(BF16)
    k_ref[0] = proj[:, D_ATTN:D_ATTN + D_KV].astype(BF16)
    v_ref[0] = proj[:, D_ATTN + D_KV:].astype(BF16)

    hs = jnp.concatenate([norm_mod(xp_buf[slot, il]) for il in range(POS_PER_STEP)], axis=0)
    ut = lax.dot_general(wut_scr[...], hs, (((1,), (1,)), ((), ())), preferred_element_type=F32) + but_ref[...]
    for il in range(POS_PER_STEP):
        piece = ut[:, il * nc:(il + 1) * nc].astype(BF16)
        ut_ref[0, :, il * SSM_GROUP:(il + 1) * SSM_GROUP, :] = piece.reshape(N_GROUPS, SSM_GROUP, nc)


def _inproj(x, mod, gain, w_in, b_in):
    b, s, d = x.shape
    nc = s // SSM_CHUNK
    rows = POS_PER_STEP * nc
    row = lambda bi, j: (bi, j, 0)
    const = lambda bi, j: (0, 0)
    return pl.pallas_call(
        _inproj_kernel,
        grid=(b, SSM_CHUNK // POS_PER_STEP),
        in_specs=[pl.BlockSpec(memory_space=pl.ANY),
                  pl.BlockSpec((1, rows, d), row),
                  pl.BlockSpec((1, 6, d), lambda bi, j: (bi, 0, 0)),
                  pl.BlockSpec((1, d), const),
                  pl.BlockSpec((d, D_QKV + D_SSM), const),
                  pl.BlockSpec((1, D_QKV), const),
                  pl.BlockSpec((D_SSM, 1), const)],
        out_specs=[pl.BlockSpec((1, rows, D_ATTN), row),
                   pl.BlockSpec((1, rows, D_KV), row),
                   pl.BlockSpec((1, rows, D_KV), row),
                   pl.BlockSpec((1, N_GROUPS, POS_PER_STEP * SSM_GROUP, nc), lambda bi, j: (bi, 0, j, 0))],
        out_shape=[jax.ShapeDtypeStruct((b, s, D_ATTN), BF16),
                   jax.ShapeDtypeStruct((b, s, D_KV), BF16),
                   jax.ShapeDtypeStruct((b, s, D_KV), BF16),
                   jax.ShapeDtypeStruct((b, N_GROUPS, SSM_ROW, nc), BF16)],
        scratch_shapes=[pltpu.VMEM((2, POS_PER_STEP, nc, d), F32), pltpu.VMEM((d, D_QKV), BF16),
                        pltpu.VMEM((D_SSM, d), BF16), pltpu.SemaphoreType.DMA((2,))],
        compiler_params=_cparams(2, ROW_VMEM_BYTES),
        name="inproj",
    )(x.reshape(b, nc, SSM_CHUNK, d), x, mod, gain.reshape(1, d), w_in, b_in[:D_QKV].reshape(1, D_QKV),
      b_in[D_QKV:].reshape(D_SSM, 1))


def _half_norm(x, low):
    sq = x * x
    s_lo = jnp.sum(jnp.where(low, sq, 0.0), axis=-1, keepdims=True)
    s_hi = jnp.sum(sq, axis=-1, keepdims=True) - s_lo
    inv = 1.0 / HEAD_DIM
    scale = jnp.where(low, lax.rsqrt(s_lo * inv + EPS), lax.rsqrt(s_hi * inv + EPS))
    return x * scale


def _attn_block(first, q, k_prev, k_cur, v_prev, v_cur, sinks_ref, qn, low, upper, rblk):
    no_prev = jnp.where(first, NEG_INF, 0.0)
    out_blocks = []
    for hk in range(N_KV_HEADS):
        qs = []
        for j in range(Q_PER_KV // 2):
            blk = hk * (Q_PER_KV // 2) + j
            qb = _half_norm(q[:, blk * LANES:(blk + 1) * LANES], low) * qn * (1.0 / math.sqrt(HEAD_DIM))
            qs.append(jnp.where(low, qb, 0.0))
            qs.append(jnp.where(low, 0.0, qb))
        qg = jnp.concatenate(qs, axis=0).astype(BF16)
        nt = (((1,), (1,)), ((), ()))
        s_prev = lax.dot_general(qg, k_prev[hk], nt, preferred_element_type=F32)
        s_cur = lax.dot_general(qg, k_cur[hk], nt, preferred_element_type=F32)
        s = jnp.where(upper, s_prev + no_prev, s_cur)
        sink = jnp.zeros((Q_PER_KV * BLOCK, 1), F32)
        for g in range(Q_PER_KV):
            sink = jnp.where(rblk == g, sinks_ref[hk * Q_PER_KV + g], sink)
        m = jnp.maximum(jnp.max(s, axis=-1, keepdims=True), sink)
        p = jnp.exp(s - m)
        den = jnp.sum(p, axis=-1, keepdims=True) + jnp.exp(sink - m)
        o = (jnp.dot(jnp.where(upper, p, 0.0).astype(BF16), v_prev[hk], preferred_element_type=F32)
             + jnp.dot(jnp.where(upper, 0.0, p).astype(BF16), v_cur[hk], preferred_element_type=F32)) / den
        for j in range(Q_PER_KV // 2):
            ev = o[(2 * j) * BLOCK:(2 * j + 1) * BLOCK]
            od = o[(2 * j + 1) * BLOCK:(2 * j + 2) * BLOCK]
            out_blocks.append(jnp.where(low, ev, od))
    return jnp.concatenate(out_blocks, axis=-1)


def _attn_kernel(sinks_ref, q_ref, k_ref, v_ref, qn_ref, kn_ref, on_ref, o_hbm, a_buf, sems, *, n_steps):
    step = pl.program_id(1)
    g = pl.program_id(0) * pl.num_programs(1) + step
    slot = g % 2
    cps = ATTN_ROWS // SSM_CHUNK
    nq = ATTN_ROWS // BLOCK

    def out_copies(sl, b_, s_):
        return [pltpu.make_async_copy(a_buf.at[sl, :, i, :], o_hbm.at[b_, i, pl.ds(s_ * cps, cps), :], sems.at[sl])
                for i in range(SSM_CHUNK)]

    @pl.when(g >= 2)
    def _():
        for cp in out_copies(slot, 0, 0):
            cp.wait()

    low = lax.broadcasted_iota(I32, (1, LANES), 1) < HEAD_DIM
    rows = Q_PER_KV * BLOCK
    upper = lax.broadcasted_iota(I32, (rows, BLOCK), 1) > lax.broadcasted_iota(I32, (rows, BLOCK), 0) % BLOCK
    rblk = lax.broadcasted_iota(I32, (rows, 1), 0) // BLOCK

    cur = pl.multiple_of(step * ATTN_ROWS, ATTN_ROWS)
    prev = pl.multiple_of(jnp.maximum(step * nq - 1, 0) * BLOCK, BLOCK)
    kall = jnp.concatenate([k_ref[0, pl.ds(prev, BLOCK), :], k_ref[0, pl.ds(cur, ATTN_ROWS), :]], axis=0).astype(F32)
    vall = jnp.concatenate([v_ref[0, pl.ds(prev, BLOCK), :], v_ref[0, pl.ds(cur, ATTN_ROWS), :]], axis=0).astype(F32)
    kall = _half_norm(kall, low) * kn_ref[...]
    kswap = pltpu.roll(kall, HEAD_DIM, axis=1)
    vswap = pltpu.roll(vall, HEAD_DIM, axis=1)
    k_dup = [jnp.where(low, kall, kswap).astype(BF16), jnp.where(low, kswap, kall).astype(BF16)]
    v_dup = [jnp.where(low, vall, vswap).astype(BF16), jnp.where(low, vswap, vall).astype(BF16)]
    blk = lambda a, i: [a[hk][i * BLOCK:(i + 1) * BLOCK] for hk in range(N_KV_HEADS)]

    for qb in range(nq):
        q = q_ref[0, qb * BLOCK:(qb + 1) * BLOCK, :].astype(F32)
        attn = _attn_block((step == 0) if qb == 0 else False, q, blk(k_dup, qb), blk(k_dup, qb + 1),
                           blk(v_dup, qb), blk(v_dup, qb + 1), sinks_ref, qn_ref[...], low, upper, rblk)
        attn = _rms(attn) * on_ref[...]
        cpb = BLOCK // SSM_CHUNK
        a_buf[slot, qb * cpb:(qb + 1) * cpb] = attn.reshape(cpb, SSM_CHUNK, D_ATTN)

    for cp in out_copies(slot, pl.program_id(0), step):
        cp.start()

    @pl.when(g == n_steps - 1)
    def _():
        for cp in out_copies(slot, 0, 0):
            cp.wait()
        if n_steps > 1:
            for cp in out_copies(1 - slot, 0, 0):
                cp.wait()


def _attention(q, k, v, sinks, q_norm, k_norm, out_norm):
    b, s, _ = q.shape
    tile2 = lambda g: jnp.tile(g.reshape(1, HEAD_DIM), (1, 2))
    cps = ATTN_ROWS // SSM_CHUNK
    return pl.pallas_call(
        functools.partial(_attn_kernel, n_steps=b * (s // ATTN_ROWS)),
        grid=(b, s // ATTN_ROWS),
        in_specs=[pl.BlockSpec(memory_space=pltpu.SMEM),
                  pl.BlockSpec((1, ATTN_ROWS, D_ATTN), lambda bi, n: (bi, n, 0)),
                  pl.BlockSpec((1, s, D_KV), lambda bi, n: (bi, 0, 0)),
                  pl.BlockSpec((1, s, D_KV), lambda bi, n: (bi, 0, 0)),
                  pl.BlockSpec((1, LANES), lambda bi, n: (0, 0)),
                  pl.BlockSpec((1, LANES), lambda bi, n: (0, 0)),
                  pl.BlockSpec((1, D_ATTN), lambda bi, n: (0, 0))],
        out_specs=pl.BlockSpec(memory_space=pl.ANY),
        out_shape=jax.ShapeDtypeStruct((b, SSM_CHUNK, s // SSM_CHUNK, D_ATTN), F32),
        scratch_shapes=[pltpu.VMEM((2, cps, SSM_CHUNK, D_ATTN), F32), pltpu.SemaphoreType.DMA((2,))],
        compiler_params=_cparams(2),
        name="attention",
    )(sinks, q, k, v, tile2(q_norm), tile2(k_norm), out_norm.reshape(1, D_ATTN))


def _cmul(ar, ai, br, bi):
    return ar * br - ai * bi, ar * bi + ai * br


def _ssm_param_kernel(*refs):
    for gi in range(refs[0].shape[0]):
        _ssm_param_group(gi, *refs)


def _ssm_param_group(gi, lam_ref, bre_ref, bim_ref, cre_ref, cim_ref, tt_ref, wz_ref, wyt_ref, cs_ref):
    f32dot = functools.partial(jnp.dot, preferred_element_type=F32, precision=HIGHEST)
    lr = lam_ref[gi, 0:1, :]
    li = lam_ref[gi, 1:2, :]
    dt = jnp.exp(lam_ref[gi, 2:3, :])
    rho = lr * dt
    th = li * dt
    imag_lane = lax.broadcasted_iota(I32, (1, LANES), 1) >= STATE

    kk = (lax.broadcasted_iota(I32, (N_POW, 1), 0) - (SSM_CHUNK - 1)).astype(F32)
    mag = jnp.exp(rho * kk)
    pw_r = mag * jnp.cos(th * kk)
    pw_i = mag * jnp.sin(th * kk)
    lb_r = pw_r[SSM_CHUNK:SSM_CHUNK + 1]
    lb_i = pw_i[SSM_CHUNK:SSM_CHUNK + 1]
    den = lr * lr + li * li
    coef_r = ((lb_r - 1.0) * lr + lb_i * li) / den
    coef_i = (lb_i * lr - (lb_r - 1.0) * li) / den

    eye = (lax.broadcasted_iota(I32, (SSM_GROUP, SSM_GROUP), 0)
           == lax.broadcasted_iota(I32, (SSM_GROUP, SSM_GROUP), 1)).astype(F32)
    lane_fold = (lax.broadcasted_iota(I32, (STATE, LANES), 1) % STATE
                 == lax.broadcasted_iota(I32, (STATE, LANES), 0)).astype(F32)

    def tile_pos(x):
        return jnp.concatenate([x] * SSM_CHUNK, axis=0)

    def power_rows(k_of_pos):
        idx = [k_of_pos(p) + (SSM_CHUNK - 1) for p in range(SSM_CHUNK)]
        rep = lambda t: jnp.concatenate([jnp.broadcast_to(t[r:r + 1], (SSM_GROUP, LANES)) for r in idx], axis=0)
        return rep(pw_r), rep(pw_i)

    def b_rows(b_ref):
        b2 = jnp.concatenate([b_ref[gi], b_ref[gi]], axis=0)
        return tile_pos(lax.dot_general(eye, b2, (((1,), (1,)), ((), ())), preferred_element_type=F32,
                                        precision=HIGHEST))

    def c_rows(c_ref):
        return tile_pos(f32dot(c_ref[gi], lane_fold))

    bbar_r, bbar_i = _cmul(coef_r, coef_i, b_rows(bre_ref), b_rows(bim_ref))
    c_r = c_rows(cre_ref)
    c_i = c_rows(cim_ref)

    a_r, a_i = _cmul(bbar_r, bbar_i, *power_rows(lambda p: -p))
    a2c = jnp.where(imag_lane, -a_i, a_r)
    m_r, m_i = _cmul(c_r, c_i, *power_rows(lambda p: p))
    bmc = jnp.where(imag_lane, m_i, m_r)
    tt = f32dot(bmc, a2c.T)
    causal = (lax.broadcasted_iota(I32, (SSM_ROW, 1), 0) // SSM_GROUP
              >= lax.broadcasted_iota(I32, (1, SSM_ROW), 1) // SSM_GROUP)
    tt_ref[gi] = jnp.where(causal, tt, 0.0).astype(BF16)

    w_r, w_i = _cmul(bbar_r, bbar_i, *power_rows(lambda p: SSM_CHUNK - 1 - p))
    wz_ref[gi, :, :LANES] = jnp.where(imag_lane, w_i, w_r).astype(BF16)
    wz_ref[gi, :, LANES:] = jnp.where(imag_lane, w_r, w_i).astype(BF16)

    y_r, y_i = _cmul(c_r, c_i, *power_rows(lambda p: p + 1))
    wyt_ref[gi] = jnp.where(imag_lane, -y_i, y_r).astype(BF16)

    cs_ref[gi, 0:1, :] = pw_r[N_POW - 1:N_POW]
    cs_ref[gi, 1:2, :] = jnp.where(imag_lane, pw_i[N_POW - 1:N_POW], -pw_i[N_POW - 1:N_POW])


def _ssm_params(lam_re, lam_im, log_dt, b_re, b_im, c_re, c_im):
    g = lam_re.shape[0]
    lam = jnp.stack([lam_re, lam_im, jnp.broadcast_to(log_dt[:, None], (g, STATE))], axis=1)
    lam = jnp.concatenate([lam, lam], axis=2)
    ng = SSM_GROUPS_PER_STEP
    blk = lambda *shape: pl.BlockSpec((ng,) + shape, lambda i: (i, 0, 0))
    return pl.pallas_call(
        _ssm_param_kernel,
        grid=(g // ng,),
        in_specs=[blk(3, LANES), blk(STATE, SSM_GROUP), blk(STATE, SSM_GROUP), blk(SSM_GROUP, STATE),
                  blk(SSM_GROUP, STATE)],
        out_specs=[blk(SSM_ROW, SSM_ROW), blk(SSM_ROW, SSM_ROW), blk(SSM_ROW, LANES), blk(2, LANES)],
        out_shape=[jax.ShapeDtypeStruct((g, SSM_ROW, SSM_ROW), BF16),
                   jax.ShapeDtypeStruct((g, SSM_ROW, SSM_ROW), BF16),
                   jax.ShapeDtypeStruct((g, SSM_ROW, LANES), BF16),
                   jax.ShapeDtypeStruct((g, 2, LANES), F32)],
        compiler_params=_cparams(1),
        name="ssm_params",
    )(lam, b_re, b_im, c_re, c_im)


def _ssm_kernel(ut_ref, tt_ref, wz_ref, wyt_ref, cs_ref, d_ref, yt_ref, z_scr, s_scr):
    batch, ng, _, nc = ut_ref.shape
    uts = [jnp.concatenate([ut_ref[b, gi] for b in range(batch)], axis=1) for gi in range(ng)]
    for gi in range(ng):
        z = lax.dot_general(uts[gi], wz_ref[gi], (((0,), (0,)), ((), ())), preferred_element_type=F32)
        _to_lane_blocks(z_scr.at[gi], z)
    c1 = [cs_ref[gi, 0:1, :] for gi in range(ng)]
    c2 = [cs_ref[gi, 1:2, :] for gi in range(ng)]

    def step(c, carry):
        rows = pl.ds(c, batch, stride=nc)
        out = []
        for gi in range(ng):
            s1, s2 = carry[gi]
            s_scr[gi, rows, :] = s1
            out.append((c1[gi] * s1 + c2[gi] * s2 + z_scr[gi, 0, rows, :],
                        c1[gi] * s2 - c2[gi] * s1 + z_scr[gi, 1, rows, :]))
        return tuple(out)

    zero = jnp.zeros((batch, LANES), F32)
    lax.fori_loop(0, nc, step, ((zero, zero),) * ng, unroll=8)
    for gi in range(ng):
        y = jnp.dot(tt_ref[gi], uts[gi], preferred_element_type=F32)
        y = y + lax.dot_general(wyt_ref[gi], s_scr[gi].astype(BF16), (((1,), (1,)), ((), ())),
                                preferred_element_type=F32)
        y = y + d_ref[gi] * uts[gi].astype(F32)
        for b in range(batch):
            yt_ref[b, gi] = y[:, b * nc:(b + 1) * nc]


def _ssm(ut, tt, wz, wyt, cs, d_skip):
    b, g, _, nc = ut.shape
    ng = SSM_GROUPS_PER_STEP
    d_col = jnp.tile(d_skip.reshape(g, 1, SSM_GROUP), (1, SSM_CHUNK, 1)).reshape(g, SSM_ROW, 1)
    blk = lambda *shape: pl.BlockSpec((ng,) + shape, lambda i: (i, 0, 0))
    act = pl.BlockSpec((b, ng, SSM_ROW, nc), lambda i: (0, i, 0, 0))
    return pl.pallas_call(
        _ssm_kernel,
        grid=(g // ng,),
        in_specs=[act, blk(SSM_ROW, SSM_ROW), blk(SSM_ROW, SSM_ROW), blk(SSM_ROW, LANES), blk(2, LANES),
                  blk(SSM_ROW, 1)],
        out_specs=act,
        out_shape=jax.ShapeDtypeStruct((b, g, SSM_ROW, nc), F32),
        scratch_shapes=[pltpu.VMEM((ng, SSM_ROW // LANES, b * nc, LANES), F32), pltpu.VMEM((ng, b * nc, LANES), F32)],
        compiler_params=_cparams(1),
        name="ssm",
    )(ut, tt, wz, wyt, cs, d_col)


def _post_kernel(x4_hbm, attn_ref, yt_ref, mod_ref, wglu_ref, bglu_ref, sn_ref, wout_f32_ref, nf_ref, wrt_ref, br_ref,
                 tri_ref, x1_ref, h2_ref, eidx_ref, wts_ref, lrank_ref, r0_ref, cnt_ref, carry_ref, xp_buf,
                 wglut_ref, wout_ref, wr_ref, sems):
    @pl.when((pl.program_id(0) == 0) & (pl.program_id(1) == 0))
    def _():
        carry_ref[...] = jnp.zeros_like(carry_ref)
        wglut_ref[...] = wglu_ref[...].T.astype(BF16)
        wout_ref[...] = wout_f32_ref[...].astype(BF16)
        wr_ref[...] = wrt_ref[...].T.astype(BF16)

    slot = _prefetch_pos_rows(x4_hbm, xp_buf, sems, POST_SUB * POST_POS)
    nc = attn_ref.shape[2]
    ts = POST_POS * nc
    d = x1_ref.shape[3]
    iota_e = lax.broadcasted_iota(I32, (N_EXPERTS, ts), 0).astype(F32)
    counts = []
    for sub in range(POST_SUB):
        pos = range(sub * POST_POS, (sub + 1) * POST_POS)
        lanes = slice(sub * ts, (sub + 1) * ts)
        yt = jnp.concatenate(
            [yt_ref[0, :, il * SSM_GROUP:(il + 1) * SSM_GROUP, :].reshape(D_SSM, nc) for il in pos], axis=1)
        g = jax.nn.gelu(yt)
        gate = jax.nn.sigmoid(jnp.dot(wglut_ref[...], g.astype(BF16), preferred_element_type=F32) + bglu_ref[...])
        ssm_t = _rms(g * gate, axis=0) * sn_ref[...]
        attn = attn_ref[0, sub * POST_POS:(sub + 1) * POST_POS].reshape(ts, D_ATTN)
        mixed = jnp.concatenate([attn.astype(BF16), ssm_t.T.astype(BF16)], axis=-1)
        o = jnp.dot(mixed, wout_ref[...], preferred_element_type=F32)
        x = jnp.concatenate([xp_buf[slot, il] for il in pos], axis=0)
        x1 = x + mod_ref[0, 2:3, :] * o
        x1_ref[0, sub * POST_POS:(sub + 1) * POST_POS] = x1.reshape(POST_POS, nc, d)
        h2 = _rms(x1) * nf_ref[...] * (1.0 + mod_ref[0, 4:5, :]) + mod_ref[0, 3:4, :]
        h2_ref[0, sub * POST_POS:(sub + 1) * POST_POS] = h2.astype(BF16).reshape(POST_POS, nc, d)

        logits = lax.dot_general(wr_ref[...], h2.astype(BF16), (((1,), (1,)), ((), ())),
                                 preferred_element_type=F32) + br_ref[...]
        l = logits
        idxs, vals = [], []
        for _ in range(TOP_K):
            m = jnp.max(l, axis=0, keepdims=True)
            idx = jnp.min(jnp.where(l == m, iota_e, float(N_EXPERTS)), axis=0, keepdims=True)
            idxs.append(idx)
            vals.append(m)
            l = jnp.where(iota_e == idx, -jnp.inf, l)
        es = [jnp.exp(v - vals[0]) for v in vals]
        tot = es[0] + es[1] + es[2] + es[3]
        member = jnp.zeros((N_EXPERTS, ts), F32)
        for idx in idxs:
            member = member + (iota_e == idx).astype(F32)
        before = jnp.dot(member.astype(BF16), tri_ref[...], preferred_element_type=F32)
        for k in range(TOP_K):
            eidx_ref[k:k + 1, lanes] = idxs[k].astype(I32)
            wts_ref[k:k + 1, lanes] = es[k] / tot
            lrank_ref[k:k + 1, lanes] = jnp.sum(jnp.where(iota_e == idxs[k], before, 0.0), axis=0,
                                                keepdims=True).astype(I32)
        counts.append(jnp.sum(member, axis=1, keepdims=True))

    carry = carry_ref[...]
    for sub in range(POST_SUB):
        r0_ref[sub] = carry.astype(I32)
        carry = carry + counts[sub]
    carry_ref[...] = carry
    cnt_ref[...] = carry.astype(I32)


def _post(x, attn, yt, mod, w_glu, b_glu, ssm_norm, w_out, norm_ffn, w_router, b_router):
    b, s, d = x.shape
    nc = s // SSM_CHUNK
    ts = POST_POS * nc
    npos = POST_SUB * POST_POS
    nt = SSM_CHUNK // npos
    t = b * s
    pm = lambda bi, j: (bi, j, 0, 0)
    const = lambda bi, j: (0, 0)
    tok = lambda bi, j: (0, bi * nt + j)
    tri = (lax.broadcasted_iota(I32, (ts, ts), 0) < lax.broadcasted_iota(I32, (ts, ts), 1)).astype(BF16)
    col = lambda a: a.reshape(-1, 1)
    return pl.pallas_call(
        _post_kernel,
        grid=(b, nt),
        in_specs=[pl.BlockSpec(memory_space=pl.ANY),
                  pl.BlockSpec((1, npos, nc, D_ATTN), pm),
                  pl.BlockSpec((1, N_GROUPS, npos * SSM_GROUP, nc), lambda bi, j: (bi, 0, j, 0)),
                  pl.BlockSpec((1, 6, d), lambda bi, j: (bi, 0, 0)),
                  pl.BlockSpec((D_SSM, D_SSM), const),
                  pl.BlockSpec((D_SSM, 1), const),
                  pl.BlockSpec((D_SSM, 1), const),
                  pl.BlockSpec((d, d), const),
                  pl.BlockSpec((1, d), const),
                  pl.BlockSpec((d, N_EXPERTS), const),
                  pl.BlockSpec((N_EXPERTS, 1), const),
                  pl.BlockSpec((ts, ts), const)],
        out_specs=[pl.BlockSpec((1, npos, nc, d), pm),
                   pl.BlockSpec((1, npos, nc, d), pm),
                   pl.BlockSpec((TOP_K, POST_SUB * ts), tok),
                   pl.BlockSpec((TOP_K, POST_SUB * ts), tok),
                   pl.BlockSpec((TOP_K, POST_SUB * ts), tok),
                   pl.BlockSpec((POST_SUB, N_EXPERTS, 1), lambda bi, j: (bi * nt + j, 0, 0)),
                   pl.BlockSpec((N_EXPERTS, 1), const)],
        out_shape=[jax.ShapeDtypeStruct((b, SSM_CHUNK, nc, d), F32),
                   jax.ShapeDtypeStruct((b, SSM_CHUNK, nc, d), BF16),
                   jax.ShapeDtypeStruct((TOP_K, t), I32),
                   jax.ShapeDtypeStruct((TOP_K, t), F32),
                   jax.ShapeDtypeStruct((TOP_K, t), I32),
                   jax.ShapeDtypeStruct((b * nt * POST_SUB, N_EXPERTS, 1), I32),
                   jax.ShapeDtypeStruct((N_EXPERTS, 1), I32)],
        scratch_shapes=[pltpu.VMEM((N_EXPERTS, 1), F32), pltpu.VMEM((2, npos, nc, d), F32),
                        pltpu.VMEM((D_SSM, D_SSM), BF16), pltpu.VMEM((d, d), BF16), pltpu.VMEM((N_EXPERTS, d), BF16),
                        pltpu.SemaphoreType.DMA((2,))],
        compiler_params=_cparams(2, ROW_VMEM_BYTES),
        name="post",
    )(x.reshape(b, nc, SSM_CHUNK, d), attn, yt, mod, w_glu, col(b_glu), col(ssm_norm), w_out,
      norm_ffn.reshape(1, -1), w_router, col(b_router), tri)


def _route_kernel(eidx_ref, lrank_ref, r0_ref, cnt_ref, ls_ref, tab_ref, te_ref, nv_ref, nx_ref, pad_ref):
    cnt = cnt_ref[...]
    tiles = (cnt + (RUN - 1 + FFN_ROWS - 1)) // FFN_ROWS
    er = lax.broadcasted_iota(I32, (N_EXPERTS, N_EXPERTS), 0)
    ec = lax.broadcasted_iota(I32, (N_EXPERTS, N_EXPERTS), 1)
    ltri = (ec < er).astype(BF16)

    def excl_cumsum(v):
        vb = jnp.broadcast_to(v.astype(F32), (N_EXPERTS, LANES)).astype(BF16)
        return jnp.dot(ltri, vb, preferred_element_type=F32)[:, 0:1].astype(I32)

    start_t = excl_cumsum(tiles)
    end_t = start_t + tiles
    start = start_t * FFN_ROWS
    pad_ref[...] = start + cnt

    nb = r0_ref.shape[0]
    ts = eidx_ref.shape[1] // nb
    iota_e = lax.broadcasted_iota(I32, (N_EXPERTS, ts), 0)
    iota_t = lax.broadcasted_iota(I32, (N_EXPERTS, TABW), 0)
    lane = lax.broadcasted_iota(I32, (1, TABW), 1)

    def block(b, carry):
        lanes = pl.ds(pl.multiple_of(b * ts, ts), ts)
        sels = [iota_e == eidx_ref[k:k + 1, lanes] for k in range(TOP_K)]
        member = sels[0].astype(I32) + sels[1].astype(I32) + sels[2].astype(I32) + sels[3].astype(I32)
        units = lax.shift_right_logical(jnp.sum(member, axis=1, keepdims=True) + (RUN - 1), RUN_SHIFT)
        u0 = excl_cumsum(units)
        for k in range(TOP_K):
            first = jnp.sum(jnp.where(sels[k], u0, 0), axis=0, keepdims=True)
            ls_ref[k:k + 1, lanes] = first * RUN + lrank_ref[k:k + 1, lanes]
        n_big = lax.shift_right_logical(units, 1)
        n_small = units & 1
        slot0 = start + r0_ref[b]

        def chunk_rows(idx, counts):
            c0 = excl_cumsum(counts)
            sel = iota_t == jnp.sum((idx >= c0 + counts).astype(I32), axis=0, keepdims=True)
            pick = lambda v: jnp.sum(jnp.where(sel, v, 0), axis=0, keepdims=True)
            j = idx - pick(c0)
            return pick(slot0), pick(u0), j, pick(n_big), idx < jnp.max(c0 + counts, axis=0, keepdims=True)

        s_b, u_b, j_b, _, ok_b = chunk_rows(lane, n_big)
        s_s, u_s, _, nb_s, ok_s = chunk_rows(lane - SMALL0, n_small)
        small = lane >= SMALL0
        slot = jnp.where(small, s_s + nb_s * BIG, s_b + j_b * BIG)
        local = jnp.where(small, (u_s + 2 * nb_s) * RUN, (u_b + 2 * j_b) * RUN)
        ok = (small & ok_s) | (jnp.logical_not(small) & ok_b)
        counts = jnp.where(lane == TABW - 2, jnp.sum(n_big, axis=0, keepdims=True),
                           jnp.sum(n_small, axis=0, keepdims=True))
        tab_ref[b, 0:1, :] = jnp.where(lane >= TABW - 2, counts, jnp.where(ok, slot, -1))
        tab_ref[b, 1:2, :] = jnp.where(ok, local, 0)
        return carry

    lax.fori_loop(0, nb, block, 0, unroll=2)

    nv = jnp.max(end_t, axis=0, keepdims=True)
    width = te_ref.shape[1]
    ti = jnp.minimum(lax.broadcasted_iota(I32, (N_EXPERTS, width), 1), nv - 1)
    te = jnp.minimum(jnp.sum((ti >= end_t).astype(I32), axis=0, keepdims=True), N_EXPERTS - 1)
    te_ref[...] = te
    nv_ref[...] = jnp.broadcast_to(nv, nv_ref.shape)
    ie = lax.broadcasted_iota(I32, (N_EXPERTS, width), 0)
    own_end = jnp.sum(jnp.where(ie == te, end_t, 0), axis=0, keepdims=True)
    nxt = jnp.minimum(jnp.sum((own_end >= end_t).astype(I32), axis=0, keepdims=True), N_EXPERTS - 1)
    nx_ref[...] = jnp.where(own_end < nv, nxt, -1)


def _route(eidx, lrank, r0, cnt, n_tiles):
    t = eidx.shape[1]
    nb = r0.shape[0]
    width = -(-n_tiles // LANES) * LANES
    return pl.pallas_call(
        _route_kernel,
        out_shape=[jax.ShapeDtypeStruct((TOP_K, t), I32),
                   jax.ShapeDtypeStruct((nb, 2, TABW), I32),
                   jax.ShapeDtypeStruct((1, width), I32),
                   jax.ShapeDtypeStruct((1, LANES), I32),
                   jax.ShapeDtypeStruct((1, width), I32),
                   jax.ShapeDtypeStruct((N_EXPERTS, 1), I32)],
        name="route",
    )(eidx, lrank, r0, cnt)


def _for_chunk_pairs(n, fn):
    def body(i, carry):
        fn(2 * i, 0)

        @pl.when(2 * i + 1 < n)
        def _():
            fn(2 * i + 1, 1)
        return carry
    lax.fori_loop(0, lax.shift_right_logical(n + 1, 1), body, 0)


def _for_block_chunks(tab_ref, blk, fn):
    base = blk * (2 * TABW)
    for first, count_lane, n_rows in ((0, TABW - 2, BIG), (SMALL0, TABW - 1, RUN)):
        def visit(c, parity, first=first, n_rows=n_rows):
            fn(tab_ref[base + first + c], pl.multiple_of(tab_ref[base + TABW + first + c], RUN), n_rows, parity)
        _for_chunk_pairs(tab_ref[base + count_lane], visit)


def _local_rows(ts):
    return ts * TOP_K + N_EXPERTS * RUN


def _dispatch_kernel(tab_ref, pad_ref, nvt_ref, h_ref, ls_ref, xs_ref, buf, zbuf, sems, zsem):
    b = pl.program_id(0)
    slot = b % 2
    ts = h_ref.shape[0]
    local = buf.shape[2]

    def chunk_copy(sl, slot_row, local_row, n):
        return pltpu.make_async_copy(buf.at[sl, :, pl.ds(local_row, n), :], xs_ref.at[:, pl.ds(slot_row, n), :],
                                     sems.at[sl])

    @pl.when(b == 0)
    def _():
        zbuf[...] = jnp.zeros_like(zbuf)
        zrows = zbuf.shape[1]
        zero = lambda row: pltpu.make_async_copy(zbuf, xs_ref.at[:, pl.ds(row, zrows), :], zsem)
        for phase in range(3):
            for e in range(phase, N_EXPERTS, 3):
                zero(pad_ref[e]).start()
            for e in range(phase, N_EXPERTS, 3):
                zero(pad_ref[e]).wait()
        ztile = lambda i: pltpu.make_async_copy(zbuf.at[:, pl.ds(0, FFN_ROWS), :],
                                                xs_ref.at[:, pl.ds((nvt_ref[0] + i) * FFN_ROWS, FFN_ROWS), :], zsem)

        def tail_start(i, carry):
            ztile(i).start()
            return carry

        def tail_wait(i, carry):
            ztile(i).wait()
            return carry
        lax.fori_loop(0, nvt_ref[1] - nvt_ref[0], tail_start, 0)
        lax.fori_loop(0, nvt_ref[1] - nvt_ref[0], tail_wait, 0)

    r = lax.broadcasted_iota(I32, (local, ts), 0)
    hit = (r == ls_ref[0:1, :]) | (r == ls_ref[1:2, :]) | (r == ls_ref[2:3, :]) | (r == ls_ref[3:4, :])
    hit = hit.astype(BF16)
    for pb in range(PANELS):
        srt = jnp.dot(hit, h_ref[:, pb * PANEL_COLS:(pb + 1) * PANEL_COLS], preferred_element_type=F32)
        buf[slot, pb] = _pack_panel(srt, exact=True)

    _for_block_chunks(tab_ref, b, lambda s, l, n, p: chunk_copy(slot, s, l, n).start(priority=p))

    @pl.when(b > 0)
    def _():
        _for_block_chunks(tab_ref, b - 1, lambda s, l, n, p: chunk_copy(1 - slot, s, l, n).wait())

    @pl.when(b == pl.num_programs(0) - 1)
    def _():
        _for_block_chunks(tab_ref, b, lambda s, l, n, p: chunk_copy(slot, s, l, n).wait())


def _dispatch(tab, pad, nvt, h2, ls, n_rows):
    t, d = h2.shape
    nb = tab.shape[0] // (2 * TABW)
    ts = t // nb
    return pl.pallas_call(
        _dispatch_kernel,
        grid_spec=pltpu.PrefetchScalarGridSpec(
            num_scalar_prefetch=3,
            grid=(nb,),
            in_specs=[pl.BlockSpec((ts, d), lambda i, *_: (i, 0)),
                      pl.BlockSpec((TOP_K, ts), lambda i, *_: (0, i))],
            out_specs=pl.BlockSpec(memory_space=pl.ANY),
            scratch_shapes=[pltpu.VMEM((2, PANELS, _local_rows(ts), LANES), U32),
                            pltpu.VMEM((PANELS, FFN_ROWS + RUN, LANES), U32),
                            pltpu.SemaphoreType.DMA((2,)), pltpu.SemaphoreType.DMA],
        ),
        out_shape=jax.ShapeDtypeStruct((PANELS, n_rows, LANES), U32),
        compiler_params=_cparams(1, ROW_VMEM_BYTES),
        name="dispatch",
    )(tab, pad, nvt, h2, ls)


def _ffn_kernel(te_ref, nv_ref, nx_ref, xs_ref, wgu_hbm, bgu_ref, wd_hbm, bd_ref, perm_ref, ys_ref,
                wgu_stage, wd_stage, wg_scr, wu_scr, wd_scr, bg_scr, bu_scr, sems):
    p = pl.program_id(0)
    t0 = 2 * p
    t1 = t0 + 1
    e0 = te_ref[t0]
    e1 = te_ref[t1]
    v0 = t0 < nv_ref[0]
    v1 = t1 < nv_ref[0]
    new0 = (p == 0) | (e0 != te_ref[jnp.maximum(t0 - 1, 0)])
    same = v1 & (e1 == e0)

    def stage_copies(e):
        return (pltpu.make_async_copy(wgu_hbm.at[e], wgu_stage, sems.at[0]),
                pltpu.make_async_copy(wd_hbm.at[e], wd_stage, sems.at[1]))

    def load_expert(t, first):
        e = te_ref[t]
        if first:
            @pl.when(p == 0)
            def _():
                for cp in stage_copies(e):
                    cp.start()

        for cp in stage_copies(e):
            cp.wait()
        bias = bgu_ref[e]
        for c in range(2 * D_FF // PERM):
            cols = slice(c * PERM, (c + 1) * PERM)
            half = slice(c * (PERM // 2), (c + 1) * (PERM // 2))
            w = wgu_stage[:, cols].astype(BF16)
            pw = jnp.dot(w, perm_ref[...], preferred_element_type=F32).astype(BF16)
            wg_scr[:, half] = pw[:, :PERM // 2]
            wu_scr[:, half] = pw[:, PERM // 2:]
            b1 = bias[:, cols].astype(BF16)
            r1 = bias[:, cols] - b1.astype(F32)
            b2 = r1.astype(BF16)
            b3 = (r1 - b2.astype(F32)).astype(BF16)
            terms = jnp.concatenate([b1, b2, b3, jnp.zeros((5, PERM), BF16)], axis=0)
            pb = jnp.sum(jnp.dot(terms, perm_ref[...], preferred_element_type=F32), axis=0, keepdims=True)
            bg_scr[:, half] = pb[:, :PERM // 2]
            bu_scr[:, half] = pb[:, PERM // 2:]
        wd_scr[...] = wd_stage[...].astype(BF16)

        @pl.when(nx_ref[t] >= 0)
        def _():
            for cp in stage_copies(nx_ref[t]):
                cp.start()

    def run(lo, n, e):
        x = _unpack_panels([xs_ref[pb, lo:lo + n, :] for pb in range(PANELS)])
        gate = jnp.dot(x, wg_scr[...], preferred_element_type=F32) + bg_scr[...]
        up = jnp.dot(x, wu_scr[...], preferred_element_type=F32) + bu_scr[...]
        gate = jnp.minimum(gate, SWIGLU_LIMIT)
        up = jnp.clip(up, -SWIGLU_LIMIT, SWIGLU_LIMIT)
        act = ((up + 1.0) * (gate * jax.nn.sigmoid(SWIGLU_ALPHA * gate))).astype(BF16)
        bd = bd_ref[e]
        for pb in range(PANELS):
            cols = slice(pb * PANEL_COLS, (pb + 1) * PANEL_COLS)
            y = jnp.dot(act, wd_scr[:, cols], preferred_element_type=F32) + bd[:, cols]
            ys_ref[pb, lo:lo + n, :] = _pack_panel(y)

    @pl.when(v0 & new0)
    def _():
        load_expert(t0, True)

    @pl.when(same)
    def _():
        run(0, 2 * FFN_ROWS, e0)

    @pl.when(v0 & jnp.logical_not(same))
    def _():
        run(0, FFN_ROWS, e0)

    @pl.when(v1 & jnp.logical_not(same))
    def _():
        load_expert(t1, False)
        run(FFN_ROWS, FFN_ROWS, e1)

    @pl.when(v0 & jnp.logical_not(v1))
    def _():
        ys_ref[:, FFN_ROWS:, :] = xs_ref[:, FFN_ROWS:, :]


def _ffn(te, nv, nx, xs, w_gate_up, bgu, w_down, bd, n_tiles):
    d = D_MODEL
    pair = lambda i, te, nv, nx: (0, jnp.minimum(i, lax.shift_right_logical(nv[0] - 1, 1)), 0)
    whole = lambda i, te, nv, nx: (0, 0, 0)
    r = lax.broadcasted_iota(I32, (PERM, PERM), 0)
    c = lax.broadcasted_iota(I32, (PERM, PERM), 1)
    perm = (r == jnp.where(c < PERM // 2, 2 * c, 2 * (c - PERM // 2) + 1)).astype(BF16)
    return pl.pallas_call(
        _ffn_kernel,
        grid_spec=pltpu.PrefetchScalarGridSpec(
            num_scalar_prefetch=3,
            grid=(n_tiles // 2,),
            in_specs=[pl.BlockSpec((PANELS, 2 * FFN_ROWS, LANES), pair),
                      pl.BlockSpec(memory_space=pl.ANY),
                      pl.BlockSpec((N_EXPERTS, 1, 2 * D_FF), whole),
                      pl.BlockSpec(memory_space=pl.ANY),
                      pl.BlockSpec((N_EXPERTS, 1, d), whole),
                      pl.BlockSpec((PERM, PERM), lambda i, te, nv, nx: (0, 0))],
            out_specs=pl.BlockSpec((PANELS, 2 * FFN_ROWS, LANES), pair),
            scratch_shapes=[pltpu.VMEM((d, 2 * D_FF), F32), pltpu.VMEM((D_FF, d), F32),
                            pltpu.VMEM((d, D_FF), BF16), pltpu.VMEM((d, D_FF), BF16), pltpu.VMEM((D_FF, d), BF16),
                            pltpu.VMEM((1, D_FF), F32), pltpu.VMEM((1, D_FF), F32),
                            pltpu.SemaphoreType.DMA((2,))],
        ),
        out_shape=jax.ShapeDtypeStruct(xs.shape, U32),
        input_output_aliases={3: 0},
        compiler_params=_cparams(1, FFN_VMEM_BYTES),
        name="ffn",
    )(te, nv, nx, xs, w_gate_up, bgu, w_down, bd, perm)


def _combine_kernel(tab_ref, x1_ref, ls_ref, w_ref, mod_ref, ys_ref, o4_hbm, ybuf, ob_buf, sems, osems, *, n_steps):
    jj = pl.program_id(1)
    g = pl.program_id(0) * pl.num_programs(1) + jj
    slot = g % 2
    nc = x1_ref.shape[2]
    tt = POST_POS * nc
    d = x1_ref.shape[3]
    local = ybuf.shape[3]
    npos = COMB_SUB * POST_POS

    def for_step_chunks(step, sl, fn):
        for sub in range(COMB_SUB):
            def visit(s, l, n, p, sub=sub):
                fn(pltpu.make_async_copy(ys_ref.at[:, pl.ds(s, n), :], ybuf.at[sl, sub, :, pl.ds(l, n), :],
                                         sems.at[sl]), p)
            _for_block_chunks(tab_ref, step * COMB_SUB + sub, visit)

    @pl.when(g == 0)
    def _():
        ybuf[...] = jnp.zeros_like(ybuf)
        for_step_chunks(0, 0, lambda cp, p: cp.start(priority=p))

    @pl.when(g + 1 < n_steps)
    def _():
        for_step_chunks(g + 1, 1 - slot, lambda cp, p: cp.start(priority=p))

    for_step_chunks(g, slot, lambda cp, p: cp.wait())

    def out_copies(sl, b_, j_):
        return [pltpu.make_async_copy(ob_buf.at[sl, il], o4_hbm.at[b_, :, npos * j_ + il, :], osems.at[sl])
                for il in range(npos)]

    @pl.when(g >= 2)
    def _():
        for cp in out_copies(slot, 0, 0):
            cp.wait()

    r = lax.broadcasted_iota(I32, (tt, local), 1).astype(F32)
    to_cols = lambda a: jnp.concatenate([a, jnp.zeros_like(a)], axis=0).T
    for sub in range(COMB_SUB):
        lanes = slice(sub * tt, (sub + 1) * tt)
        ls_c = to_cols(ls_ref[:, lanes].astype(F32))
        w_c = to_cols(w_ref[:, lanes])
        wm = jnp.zeros((tt, local), F32)
        for k in range(TOP_K):
            wm = jnp.where(r == ls_c[:, k:k + 1], w_c[:, k:k + 1], wm)
        y = _unpack_panels([ybuf[slot, sub, pb] for pb in range(PANELS)])
        acc = jnp.dot(wm.astype(BF16), y, preferred_element_type=F32)
        out = x1_ref[0, sub * POST_POS:(sub + 1) * POST_POS].reshape(tt, d) + mod_ref[0, 5:6, :] * acc
        for il in range(POST_POS):
            ob_buf[slot, sub * POST_POS + il] = out[il * nc:(il + 1) * nc]

    for cp in out_copies(slot, pl.program_id(0), jj):
        cp.start()

    @pl.when(g == n_steps - 1)
    def _():
        for cp in out_copies(slot, 0, 0):
            cp.wait()
        if n_steps > 1:
            for cp in out_copies(1 - slot, 0, 0):
                cp.wait()


def _combine(tab, x1, ls, wts, mod, ys):
    b, _, nc, d = x1.shape
    s = SSM_CHUNK * nc
    tt = POST_POS * nc
    npos = COMB_SUB * POST_POS
    nt = SSM_CHUNK // npos
    o4 = pl.pallas_call(
        functools.partial(_combine_kernel, n_steps=b * nt),
        grid_spec=pltpu.PrefetchScalarGridSpec(
            num_scalar_prefetch=1,
            grid=(b, nt),
            in_specs=[pl.BlockSpec((1, npos, nc, d), lambda bi, j, *_: (bi, j, 0, 0)),
                      pl.BlockSpec((TOP_K, COMB_SUB * tt), lambda bi, j, *_: (0, bi * nt + j)),
                      pl.BlockSpec((TOP_K, COMB_SUB * tt), lambda bi, j, *_: (0, bi * nt + j)),
                      pl.BlockSpec((1, 6, d), lambda bi, j, *_: (bi, 0, 0)),
                      pl.BlockSpec(memory_space=pl.ANY)],
            out_specs=pl.BlockSpec(memory_space=pl.ANY),
            scratch_shapes=[pltpu.VMEM((2, COMB_SUB, PANELS, _local_rows(tt), LANES), U32),
                            pltpu.VMEM((2, npos, nc, d), F32),
                            pltpu.SemaphoreType.DMA((2,)), pltpu.SemaphoreType.DMA((2,))],
        ),
        out_shape=jax.ShapeDtypeStruct((b, nc, SSM_CHUNK, d), F32),
        compiler_params=_cparams(2),
        name="combine",
    )(tab, x1, ls, wts, mod, ys)
    return o4.reshape(b, s, d)


def kernel(x, c, w_ada, b_ada, norm_mix, w_in, b_in, q_norm, k_norm, sinks, lam_re, lam_im, log_dt, b_re, b_im,
           c_re, c_im, d_skip, w_glu, b_glu, attn_out_norm, ssm_out_norm, w_out, norm_ffn, w_router, b_router,
           w_gate_up, b_gate_up, w_down, b_down):
    b, s, d = x.shape
    t = b * s
    depth = w_ada.shape[0]
    n_tiles = -(-(t * TOP_K + N_EXPERTS * (RUN - 1 + FFN_ROWS - 1)) // FFN_ROWS)
    n_tiles += n_tiles % 2
    n_alloc = n_tiles + 2
    for l in range(depth):
        mod = _adaln(c, w_ada[l], b_ada[l]).reshape(b, 6, d)
        q, k, v, ut = _inproj(x, mod, norm_mix[l], w_in[l], b_in[l])
        attn = _attention(q, k, v, sinks[l], q_norm[l], k_norm[l], attn_out_norm[l])
        tt, wz, wyt, cs = _ssm_params(lam_re[l], lam_im[l], log_dt[l], b_re[l], b_im[l], c_re[l], c_im[l])
        yt = _ssm(ut, tt, wz, wyt, cs, d_skip[l])
        x1, h2, eidx, wts, lrank, r0, cnt = _post(x, attn, yt, mod, w_glu[l], b_glu[l], ssm_out_norm[l], w_out[l],
                                                  norm_ffn[l], w_router[l], b_router[l])
        ls, tab, te, nv, nx, pad = _route(eidx, lrank, r0, cnt, n_tiles)
        tab = tab.reshape(-1)
        nvt = jnp.stack([nv[0, 0], jnp.int32(n_alloc)])
        xs = _dispatch(tab, pad.reshape(-1), nvt, h2.reshape(t, d), ls, n_alloc * FFN_ROWS)
        ys = _ffn(te[0, :n_tiles], nv[0, :1], nx[0, :n_tiles], xs, w_gate_up[l], b_gate_up[l][:, None, :],
                  w_down[l], b_down[l][:, None, :], n_tiles)
        x = _combine(tab, x1, ls, wts, mod, ys)
    return x
```

```python
import functools
import math

import jax
import jax.numpy as jnp
from jax import lax
from jax.experimental import pallas as pl
from jax.experimental.pallas import tpu as pltpu

F32 = jnp.float32
BF16 = jnp.bfloat16
U32 = jnp.uint32
I32 = jnp.int32

D_MODEL = 1024
HEAD_DIM = 64
N_HEADS = 8
N_KV_HEADS = 2
Q_PER_KV = N_HEADS // N_KV_HEADS
D_ATTN = N_HEADS * HEAD_DIM
D_KV = N_KV_HEADS * HEAD_DIM
D_QKV = D_ATTN + 2 * D_KV
WINDOW = 128
BLOCK = 128
D_SSM = D_MODEL - D_ATTN
SSM_GROUP = 16
N_GROUPS = D_SSM // SSM_GROUP
STATE = 64
N_EXPERTS = 32
TOP_K = 4
D_FF = D_MODEL
SWIGLU_LIMIT = 7.0
SWIGLU_ALPHA = 1.702
EPS = 1e-6
NEG_INF = -1e30

LANES = 128
SSM_CHUNK = 16
SSM_ROW = SSM_CHUNK * SSM_GROUP
N_POW = 2 * SSM_CHUNK
PANEL_COLS = 2 * LANES
PANELS = D_MODEL // PANEL_COLS

SSM_GROUPS_PER_STEP = 8
POS_PER_STEP = 8
ATTN_ROWS = 512
POST_POS = 2
POST_SUB = 4
COMB_SUB = 2
FFN_ROWS = 256
RUN = 8
RUN_SHIFT = 3
BIG = 2 * RUN
TABW = 128
SMALL0 = 80
PERM = 256
FFN_VMEM_BYTES = 40 * 1024 * 1024
ROW_VMEM_BYTES = 48 * 1024 * 1024

HIGHEST = lax.Precision.HIGHEST
_ARB = "arbitrary"


def _cparams(n, vmem=None):
    return pltpu.CompilerParams(dimension_semantics=(_ARB,) * n, vmem_limit_bytes=vmem)


def _rms(x, axis=-1):
    return x * lax.rsqrt(jnp.mean(x * x, axis=axis, keepdims=True) + EPS)


def _pack_panel(y, exact=False):
    hi, lo = y[:, :LANES], y[:, LANES:]
    if not exact:
        hi = hi.astype(BF16).astype(F32)
        lo = lo.astype(BF16).astype(F32)
    return lax.bitcast_convert_type(hi, U32) | (lax.bitcast_convert_type(lo, U32) >> 16)


def _unpack_panels(words):
    cols = []
    for w in words:
        cols.append(lax.bitcast_convert_type(w & jnp.uint32(0xFFFF0000), F32).astype(BF16))
        cols.append(lax.bitcast_convert_type(w << 16, F32).astype(BF16))
    return jnp.concatenate(cols, axis=-1)


def _prefetch_pos_rows(x4_hbm, buf, sems, n_pos):
    bi = pl.program_id(0)
    j = pl.program_id(1)
    nj = pl.num_programs(1)
    g = bi * nj + j
    slot = g % 2

    def copies(sl, b_, j_):
        return [pltpu.make_async_copy(x4_hbm.at[b_, :, n_pos * j_ + il, :], buf.at[sl, il], sems.at[sl])
                for il in range(n_pos)]

    @pl.when(g == 0)
    def _():
        for cp in copies(0, 0, 0):
            cp.start()

    @pl.when(g + 1 < pl.num_programs(0) * nj)
    def _():
        wrap = j + 1 == nj
        for cp in copies(1 - slot, jnp.where(wrap, bi + 1, bi), jnp.where(wrap, 0, j + 1)):
            cp.start()

    for cp in copies(slot, bi, j):
        cp.wait()
    return slot


def _to_lane_blocks(dst, src):
    for kb in range(dst.shape[0]):
        dst[kb] = src[:, kb * LANES:(kb + 1) * LANES]


def _adaln_kernel(c_ref, w_ref, b_ref, o_ref):
    c = c_ref[...]
    ca = c * jax.nn.sigmoid(c)
    o_ref[...] = jnp.dot(ca, w_ref[...], preferred_element_type=F32, precision=HIGHEST) + b_ref[...]


def _adaln(c, w_ada, b_ada):
    b, d = c.shape
    n = w_ada.shape[1] // d
    return pl.pallas_call(
        _adaln_kernel,
        grid=(n,),
        in_specs=[pl.BlockSpec((b, d), lambda j: (0, 0)),
                  pl.BlockSpec((d, d), lambda j: (0, j)),
                  pl.BlockSpec((1, d), lambda j: (0, j))],
        out_specs=pl.BlockSpec((b, d), lambda j: (0, j)),
        out_shape=jax.ShapeDtypeStruct((b, n * d), F32),
        compiler_params=_cparams(1),
        name="adaln",
    )(c, w_ada, b_ada.reshape(1, -1))


def _inproj_kernel(x4_hbm, x_ref, mod_ref, g_ref, w_ref, bqkv_ref, but_ref, q_ref, k_ref, v_ref, ut_ref,
                   xp_buf, wqkv_scr, wut_scr, sems):
    nc = ut_ref.shape[3]

    @pl.when((pl.program_id(0) == 0) & (pl.program_id(1) == 0))
    def _():
        wqkv_scr[...] = w_ref[:, :D_QKV].astype(BF16)
        wut_scr[...] = w_ref[:, D_QKV:].T.astype(BF16)

    slot = _prefetch_pos_rows(x4_hbm, xp_buf, sems, POS_PER_STEP)
    gain = g_ref[...]
    scale = 1.0 + mod_ref[0, 1:2, :]
    shift = mod_ref[0, 0:1, :]

    def norm_mod(x):
        return (_rms(x) * gain * scale + shift).astype(BF16)

    proj = jnp.dot(norm_mod(x_ref[0]), wqkv_scr[...], preferred_element_type=F32) + bqkv_ref[...]
    q_ref[0] = proj[:, :D_ATTN].astype(BF16)
    k_ref[0] = proj[:, D_ATTN:D_ATTN + D_KV].astype(BF16)
    v_ref[0] = proj[:, D_ATTN + D_KV:].astype(BF16)

    hs = jnp.concatenate([norm_mod(xp_buf[slot, il]) for il in range(POS_PER_STEP)], axis=0)
    ut = lax.dot_general(wut_scr[...], hs, (((1,), (1,)), ((), ())), preferred_element_type=F32) + but_ref[...]
    for il in range(POS_PER_STEP):
        piece = ut[:, il * nc:(il + 1) * nc].astype(BF16)
        ut_ref[0, :, il * SSM_GROUP:(il + 1) * SSM_GROUP, :] = piece.reshape(N_GROUPS, SSM_GROUP, nc)


def _inproj(x, mod, gain, w_in, b_in):
    b, s, d = x.shape
    nc = s // SSM_CHUNK
    rows = POS_PER_STEP * nc
    row = lambda bi, j: (bi, j, 0)
    const = lambda bi, j: (0, 0)
    return pl.pallas_call(
        _inproj_kernel,
        grid=(b, SSM_CHUNK // POS_PER_STEP),
        in_specs=[pl.BlockSpec(memory_space=pl.ANY),
                  pl.BlockSpec((1, rows, d), row),
                  pl.BlockSpec((1, 6, d), lambda bi, j: (bi, 0, 0)),
                  pl.BlockSpec((1, d), const),
                  pl.BlockSpec((d, D_QKV + D_SSM), const),
                  pl.BlockSpec((1, D_QKV), const),
                  pl.BlockSpec((D_SSM, 1), const)],
        out_specs=[pl.BlockSpec((1, rows, D_ATTN), row),
                   pl.BlockSpec((1, rows, D_KV), row),
                   pl.BlockSpec((1, rows, D_KV), row),
                   pl.BlockSpec((1, N_GROUPS, POS_PER_STEP * SSM_GROUP, nc), lambda bi, j: (bi, 0, j, 0))],
        out_shape=[jax.ShapeDtypeStruct((b, s, D_ATTN), BF16),
                   jax.ShapeDtypeStruct((b, s, D_KV), BF16),
                   jax.ShapeDtypeStruct((b, s, D_KV), BF16),
                   jax.ShapeDtypeStruct((b, N_GROUPS, SSM_ROW, nc), BF16)],
        scratch_shapes=[pltpu.VMEM((2, POS_PER_STEP, nc, d), F32), pltpu.VMEM((d, D_QKV), BF16),
                        pltpu.VMEM((D_SSM, d), BF16), pltpu.SemaphoreType.DMA((2,))],
        compiler_params=_cparams(2, ROW_VMEM_BYTES),
        name="inproj",
    )(x.reshape(b, nc, SSM_CHUNK, d), x, mod, gain.reshape(1, d), w_in, b_in[:D_QKV].reshape(1, D_QKV),
      b_in[D_QKV:].reshape(D_SSM, 1))


def _half_norm(x, low):
    sq = x * x
    s_lo = jnp.sum(jnp.where(low, sq, 0.0), axis=-1, keepdims=True)
    s_hi = jnp.sum(sq, axis=-1, keepdims=True) - s_lo
    inv = 1.0 / HEAD_DIM
    scale = jnp.where(low, lax.rsqrt(s_lo * inv + EPS), lax.rsqrt(s_hi * inv + EPS))
    return x * scale


def _attn_block(first, q, k_prev, k_cur, v_prev, v_cur, sinks_ref, qn, low, upper, rblk):
    no_prev = jnp.where(first, NEG_INF, 0.0)
    out_blocks = []
    for hk in range(N_KV_HEADS):
        qs = []
        for j in range(Q_PER_KV // 2):
            blk = hk * (Q_PER_KV // 2) + j
            qb = _half_norm(q[:, blk * LANES:(blk + 1) * LANES], low) * qn * (1.0 / math.sqrt(HEAD_DIM))
            qs.append(jnp.where(low, qb, 0.0))
            qs.append(jnp.where(low, 0.0, qb))
        qg = jnp.concatenate(qs, axis=0).astype(BF16)
        nt = (((1,), (1,)), ((), ()))
        s_prev = lax.dot_general(qg, k_prev[hk], nt, preferred_element_type=F32)
        s_cur = lax.dot_general(qg, k_cur[hk], nt, preferred_element_type=F32)
        s = jnp.where(upper, s_prev + no_prev, s_cur)
        sink = jnp.zeros((Q_PER_KV * BLOCK, 1), F32)
        for g in range(Q_PER_KV):
            sink = jnp.where(rblk == g, sinks_ref[hk * Q_PER_KV + g], sink)
        m = jnp.maximum(jnp.max(s, axis=-1, keepdims=True), sink)
        p = jnp.exp(s - m)
        den = jnp.sum(p, axis=-1, keepdims=True) + jnp.exp(sink - m)
        o = (jnp.dot(jnp.where(upper, p, 0.0).astype(BF16), v_prev[hk], preferred_element_type=F32)
             + jnp.dot(jnp.where(upper, 0.0, p).astype(BF16), v_cur[hk], preferred_element_type=F32)) / den
        for j in range(Q_PER_KV // 2):
            ev = o[(2 * j) * BLOCK:(2 * j + 1) * BLOCK]
            od = o[(2 * j + 1) * BLOCK:(2 * j + 2) * BLOCK]
            out_blocks.append(jnp.where(low, ev, od))
    return jnp.concatenate(out_blocks, axis=-1)


def _attn_kernel(sinks_ref, q_ref, k_ref, v_ref, qn_ref, kn_ref, on_ref, o_hbm, a_buf, sems, *, n_steps):
    step = pl.program_id(1)
    g = pl.program_id(0) * pl.num_programs(1) + step
    slot = g % 2
    cps = ATTN_ROWS // SSM_CHUNK
    nq = ATTN_ROWS // BLOCK

    def out_copies(sl, b_, s_):
        return [pltpu.make_async_copy(a_buf.at[sl, :, i, :], o_hbm.at[b_, i, pl.ds(s_ * cps, cps), :], sems.at[sl])
                for i in range(SSM_CHUNK)]

    @pl.when(g >= 2)
    def _():
        for cp in out_copies(slot, 0, 0):
            cp.wait()

    low = lax.broadcasted_iota(I32, (1, LANES), 1) < HEAD_DIM
    rows = Q_PER_KV * BLOCK
    upper = lax.broadcasted_iota(I32, (rows, BLOCK), 1) > lax.broadcasted_iota(I32, (rows, BLOCK), 0) % BLOCK
    rblk = lax.broadcasted_iota(I32, (rows, 1), 0) // BLOCK

    cur = pl.multiple_of(step * ATTN_ROWS, ATTN_ROWS)
    prev = pl.multiple_of(jnp.maximum(step * nq - 1, 0) * BLOCK, BLOCK)
    kall = jnp.concatenate([k_ref[0, pl.ds(prev, BLOCK), :], k_ref[0, pl.ds(cur, ATTN_ROWS), :]], axis=0).astype(F32)
    vall = jnp.concatenate([v_ref[0, pl.ds(prev, BLOCK), :], v_ref[0, pl.ds(cur, ATTN_ROWS), :]], axis=0).astype(F32)
    kall = _half_norm(kall, low) * kn_ref[...]
    kswap = pltpu.roll(kall, HEAD_DIM, axis=1)
    vswap = pltpu.roll(vall, HEAD_DIM, axis=1)
    k_dup = [jnp.where(low, kall, kswap).astype(BF16), jnp.where(low, kswap, kall).astype(BF16)]
    v_dup = [jnp.where(low, vall, vswap).astype(BF16), jnp.where(low, vswap, vall).astype(BF16)]
    blk = lambda a, i: [a[hk][i * BLOCK:(i + 1) * BLOCK] for hk in range(N_KV_HEADS)]

    for qb in range(nq):
        q = q_ref[0, qb * BLOCK:(qb + 1) * BLOCK, :].astype(F32)
        attn = _attn_block((step == 0) if qb == 0 else False, q, blk(k_dup, qb), blk(k_dup, qb + 1),
                           blk(v_dup, qb), blk(v_dup, qb + 1), sinks_ref, qn_ref[...], low, upper, rblk)
        attn = _rms(attn) * on_ref[...]
        cpb = BLOCK // SSM_CHUNK
        a_buf[slot, qb * cpb:(qb + 1) * cpb] = attn.reshape(cpb, SSM_CHUNK, D_ATTN)

    for cp in out_copies(slot, pl.program_id(0), step):
        cp.start()

    @pl.when(g == n_steps - 1)
    def _():
        for cp in out_copies(slot, 0, 0):
            cp.wait()
        if n_steps > 1:
            for cp in out_copies(1 - slot, 0, 0):
                cp.wait()


def _attention(q, k, v, sinks, q_norm, k_norm, out_norm):
    b, s, _ = q.shape
    tile2 = lambda g: jnp.tile(g.reshape(1, HEAD_DIM), (1, 2))
    cps = ATTN_ROWS // SSM_CHUNK
    return pl.pallas_call(
        functools.partial(_attn_kernel, n_steps=b * (s // ATTN_ROWS)),
        grid=(b, s // ATTN_ROWS),
        in_specs=[pl.BlockSpec(memory_space=pltpu.SMEM),
                  pl.BlockSpec((1, ATTN_ROWS, D_ATTN), lambda bi, n: (bi, n, 0)),
                  pl.BlockSpec((1, s, D_KV), lambda bi, n: (bi, 0, 0)),
                  pl.BlockSpec((1, s, D_KV), lambda bi, n: (bi, 0, 0)),
                  pl.BlockSpec((1, LANES), lambda bi, n: (0, 0)),
                  pl.BlockSpec((1, LANES), lambda bi, n: (0, 0)),
                  pl.BlockSpec((1, D_ATTN), lambda bi, n: (0, 0))],
        out_specs=pl.BlockSpec(memory_space=pl.ANY),
        out_shape=jax.ShapeDtypeStruct((b, SSM_CHUNK, s // SSM_CHUNK, D_ATTN), F32),
        scratch_shapes=[pltpu.VMEM((2, cps, SSM_CHUNK, D_ATTN), F32), pltpu.SemaphoreType.DMA((2,))],
        compiler_params=_cparams(2),
        name="attention",
    )(sinks, q, k, v, tile2(q_norm), tile2(k_norm), out_norm.reshape(1, D_ATTN))


def _cmul(ar, ai, br, bi):
    return ar * br - ai * bi, ar * bi + ai * br


def _ssm_param_kernel(*refs):
    for gi in range(refs[0].shape[0]):
        _ssm_param_group(gi, *refs)


def _ssm_param_group(gi, lam_ref, bre_ref, bim_ref, cre_ref, cim_ref, tt_ref, wz_ref, wyt_ref, cs_ref):
    f32dot = functools.partial(jnp.dot, preferred_element_type=F32, precision=HIGHEST)
    lr = lam_ref[gi, 0:1, :]
    li = lam_ref[gi, 1:2, :]
    dt = jnp.exp(lam_ref[gi, 2:3, :])
    rho = lr * dt
    th = li * dt
    imag_lane = lax.broadcasted_iota(I32, (1, LANES), 1) >= STATE

    kk = (lax.broadcasted_iota(I32, (N_POW, 1), 0) - (SSM_CHUNK - 1)).astype(F32)
    mag = jnp.exp(rho * kk)
    pw_r = mag * jnp.cos(th * kk)
    pw_i = mag * jnp.sin(th * kk)
    lb_r = pw_r[SSM_CHUNK:SSM_CHUNK + 1]
    lb_i = pw_i[SSM_CHUNK:SSM_CHUNK + 1]
    den = lr * lr + li * li
    coef_r = ((lb_r - 1.0) * lr + lb_i * li) / den
    coef_i = (lb_i * lr - (lb_r - 1.0) * li) / den

    eye = (lax.broadcasted_iota(I32, (SSM_GROUP, SSM_GROUP), 0)
           == lax.broadcasted_iota(I32, (SSM_GROUP, SSM_GROUP), 1)).astype(F32)
    lane_fold = (lax.broadcasted_iota(I32, (STATE, LANES), 1) % STATE
                 == lax.broadcasted_iota(I32, (STATE, LANES), 0)).astype(F32)

    def tile_pos(x):
        return jnp.concatenate([x] * SSM_CHUNK, axis=0)

    def power_rows(k_of_pos):
        idx = [k_of_pos(p) + (SSM_CHUNK - 1) for p in range(SSM_CHUNK)]
        rep = lambda t: jnp.concatenate([jnp.broadcast_to(t[r:r + 1], (SSM_GROUP, LANES)) for r in idx], axis=0)
        return rep(pw_r), rep(pw_i)

    def b_rows(b_ref):
        b2 = jnp.concatenate([b_ref[gi], b_ref[gi]], axis=0)
        return tile_pos(lax.dot_general(eye, b2, (((1,), (1,)), ((), ())), preferred_element_type=F32,
                                        precision=HIGHEST))

    def c_rows(c_ref):
        return tile_pos(f32dot(c_ref[gi], lane_fold))

    bbar_r, bbar_i = _cmul(coef_r, coef_i, b_rows(bre_ref), b_rows(bim_ref))
    c_r = c_rows(cre_ref)
    c_i = c_rows(cim_ref)

    a_r, a_i = _cmul(bbar_r, bbar_i, *power_rows(lambda p: -p))
    a2c = jnp.where(imag_lane, -a_i, a_r)
    m_r, m_i = _cmul(c_r, c_i, *power_rows(lambda p: p))
    bmc = jnp.where(imag_lane, m_i, m_r)
    tt = f32dot(bmc, a2c.T)
    causal = (lax.broadcasted_iota(I32, (SSM_ROW, 1), 0) // SSM_GROUP
              >= lax.broadcasted_iota(I32, (1, SSM_ROW), 1) // SSM_GROUP)
    tt_ref[gi] = jnp.where(causal, tt, 0.0).astype(BF16)

    w_r, w_i = _cmul(bbar_r, bbar_i, *power_rows(lambda p: SSM_CHUNK - 1 - p))
    wz_ref[gi, :, :LANES] = jnp.where(imag_lane, w_i, w_r).astype(BF16)
    wz_ref[gi, :, LANES:] = jnp.where(imag_lane, w_r, w_i).astype(BF16)

    y_r, y_i = _cmul(c_r, c_i, *power_rows(lambda p: p + 1))
    wyt_ref[gi] = jnp.where(imag_lane, -y_i, y_r).astype(BF16)

    cs_ref[gi, 0:1, :] = pw_r[N_POW - 1:N_POW]
    cs_ref[gi, 1:2, :] = jnp.where(imag_lane, pw_i[N_POW - 1:N_POW], -pw_i[N_POW - 1:N_POW])


def _ssm_params(lam_re, lam_im, log_dt, b_re, b_im, c_re, c_im):
    g = lam_re.shape[0]
    lam = jnp.stack([lam_re, lam_im, jnp.broadcast_to(log_dt[:, None], (g, STATE))], axis=1)
    lam = jnp.concatenate([lam, lam], axis=2)
    ng = SSM_GROUPS_PER_STEP
    blk = lambda *shape: pl.BlockSpec((ng,) + shape, lambda i: (i, 0, 0))
    return pl.pallas_call(
        _ssm_param_kernel,
        grid=(g // ng,),
        in_specs=[blk(3, LANES), blk(STATE, SSM_GROUP), blk(STATE, SSM_GROUP), blk(SSM_GROUP, STATE),
                  blk(SSM_GROUP, STATE)],
        out_specs=[blk(SSM_ROW, SSM_ROW), blk(SSM_ROW, SSM_ROW), blk(SSM_ROW, LANES), blk(2, LANES)],
        out_shape=[jax.ShapeDtypeStruct((g, SSM_ROW, SSM_ROW), BF16),
                   jax.ShapeDtypeStruct((g, SSM_ROW, SSM_ROW), BF16),
                   jax.ShapeDtypeStruct((g, SSM_ROW, LANES), BF16),
                   jax.ShapeDtypeStruct((g, 2, LANES), F32)],
        compiler_params=_cparams(1),
        name="ssm_params",
    )(lam, b_re, b_im, c_re, c_im)


def _ssm_kernel(ut_ref, tt_ref, wz_ref, wyt_ref, cs_ref, d_ref, yt_ref, z_scr, s_scr):
    batch, ng, _, nc = ut_ref.shape
    uts = [jnp.concatenate([ut_ref[b, gi] for b in range(batch)], axis=1) for gi in range(ng)]
    for gi in range(ng):
        z = lax.dot_general(uts[gi], wz_ref[gi], (((0,), (0,)), ((), ())), preferred_element_type=F32)
        _to_lane_blocks(z_scr.at[gi], z)
    c1 = [cs_ref[gi, 0:1, :] for gi in range(ng)]
    c2 = [cs_ref[gi, 1:2, :] for gi in range(ng)]

    def step(c, carry):
        rows = pl.ds(c, batch, stride=nc)
        out = []
        for gi in range(ng):
            s1, s2 = carry[gi]
            s_scr[gi, rows, :] = s1
            out.append((c1[gi] * s1 + c2[gi] * s2 + z_scr[gi, 0, rows, :],
                        c1[gi] * s2 - c2[gi] * s1 + z_scr[gi, 1, rows, :]))
        return tuple(out)

    zero = jnp.zeros((batch, LANES), F32)
    lax.fori_loop(0, nc, step, ((zero, zero),) * ng, unroll=8)
    for gi in range(ng):
        y = jnp.dot(tt_ref[gi], uts[gi], preferred_element_type=F32)
        y = y + lax.dot_general(wyt_ref[gi], s_scr[gi].astype(BF16), (((1,), (1,)), ((), ())),
                                preferred_element_type=F32)
        y = y + d_ref[gi] * uts[gi].astype(F32)
        for b in range(batch):
            yt_ref[b, gi] = y[:, b * nc:(b + 1) * nc]


def _ssm(ut, tt, wz, wyt, cs, d_skip):
    b, g, _, nc = ut.shape
    ng = SSM_GROUPS_PER_STEP
    d_col = jnp.tile(d_skip.reshape(g, 1, SSM_GROUP), (1, SSM_CHUNK, 1)).reshape(g, SSM_ROW, 1)
    blk = lambda *shape: pl.BlockSpec((ng,) + shape, lambda i: (i, 0, 0))
    act = pl.BlockSpec((b, ng, SSM_ROW, nc), lambda i: (0, i, 0, 0))
    return pl.pallas_call(
        _ssm_kernel,
        grid=(g // ng,),
        in_specs=[act, blk(SSM_ROW, SSM_ROW), blk(SSM_ROW, SSM_ROW), blk(SSM_ROW, LANES), blk(2, LANES),
                  blk(SSM_ROW, 1)],
        out_specs=act,
        out_shape=jax.ShapeDtypeStruct((b, g, SSM_ROW, nc), F32),
        scratch_shapes=[pltpu.VMEM((ng, SSM_ROW // LANES, b * nc, LANES), F32), pltpu.VMEM((ng, b * nc, LANES), F32)],
        compiler_params=_cparams(1),
        name="ssm",
    )(ut, tt, wz, wyt, cs, d_col)


def _post_kernel(x4_hbm, attn_ref, yt_ref, mod_ref, wglu_ref, bglu_ref, sn_ref, wout_f32_ref, nf_ref, wrt_ref, br_ref,
                 tri_ref, x1_ref, h2_ref, eidx_ref, wts_ref, lrank_ref, r0_ref, cnt_ref, carry_ref, xp_buf,
                 wglut_ref, wout_ref, wr_ref, sems):
    @pl.when((pl.program_id(0) == 0) & (pl.program_id(1) == 0))
    def _():
        carry_ref[...] = jnp.zeros_like(carry_ref)
        wglut_ref[...] = wglu_ref[...].T.astype(BF16)
        wout_ref[...] = wout_f32_ref[...].astype(BF16)
        wr_ref[...] = wrt_ref[...].T.astype(BF16)

    slot = _prefetch_pos_rows(x4_hbm, xp_buf, sems, POST_SUB * POST_POS)
    nc = attn_ref.shape[2]
    ts = POST_POS * nc
    d = x1_ref.shape[3]
    iota_e = lax.broadcasted_iota(I32, (N_EXPERTS, ts), 0).astype(F32)
    counts = []
    for sub in range(POST_SUB):
        pos = range(sub * POST_POS, (sub + 1) * POST_POS)
        lanes = slice(sub * ts, (sub + 1) * ts)
        yt = jnp.concatenate(
            [yt_ref[0, :, il * SSM_GROUP:(il + 1) * SSM_GROUP, :].reshape(D_SSM, nc) for il in pos], axis=1)
        g = jax.nn.gelu(yt)
        gate = jax.nn.sigmoid(jnp.dot(wglut_ref[...], g.astype(BF16), preferred_element_type=F32) + bglu_ref[...])
        ssm_t = _rms(g * gate, axis=0) * sn_ref[...]
        attn = attn_ref[0, sub * POST_POS:(sub + 1) * POST_POS].reshape(ts, D_ATTN)
        mixed = jnp.concatenate([attn.astype(BF16), ssm_t.T.astype(BF16)], axis=-1)
        o = jnp.dot(mixed, wout_ref[...], preferred_element_type=F32)
        x = jnp.concatenate([xp_buf[slot, il] for il in pos], axis=0)
        x1 = x + mod_ref[0, 2:3, :] * o
        x1_ref[0, sub * POST_POS:(sub + 1) * POST_POS] = x1.reshape(POST_POS, nc, d)
        h2 = _rms(x1) * nf_ref[...] * (1.0 + mod_ref[0, 4:5, :]) + mod_ref[0, 3:4, :]
        h2_ref[0, sub * POST_POS:(sub + 1) * POST_POS] = h2.astype(BF16).reshape(POST_POS, nc, d)

        logits = lax.dot_general(wr_ref[...], h2.astype(BF16), (((1,), (1,)), ((), ())),
                                 preferred_element_type=F32) + br_ref[...]
        l = logits
        idxs, vals = [], []
        for _ in range(TOP_K):
            m = jnp.max(l, axis=0, keepdims=True)
            idx = jnp.min(jnp.where(l == m, iota_e, float(N_EXPERTS)), axis=0, keepdims=True)
            idxs.append(idx)
            vals.append(m)
            l = jnp.where(iota_e == idx, -jnp.inf, l)
        es = [jnp.exp(v - vals[0]) for v in vals]
        tot = es[0] + es[1] + es[2] + es[3]
        member = jnp.zeros((N_EXPERTS, ts), F32)
        for idx in idxs:
            member = member + (iota_e == idx).astype(F32)
        before = jnp.dot(member.astype(BF16), tri_ref[...], preferred_element_type=F32)
        for k in range(TOP_K):
            eidx_ref[k:k + 1, lanes] = idxs[k].astype(I32)
            wts_ref[k:k + 1, lanes] = es[k] / tot
            lrank_ref[k:k + 1, lanes] = jnp.sum(jnp.where(iota_e == idxs[k], before, 0.0), axis=0,
                                                keepdims=True).astype(I32)
        counts.append(jnp.sum(member, axis=1, keepdims=True))

    carry = carry_ref[...]
    for sub in range(POST_SUB):
        r0_ref[sub] = carry.astype(I32)
        carry = carry + counts[sub]
    carry_ref[...] = carry
    cnt_ref[...] = carry.astype(I32)


def _post(x, attn, yt, mod, w_glu, b_glu, ssm_norm, w_out, norm_ffn, w_router, b_router):
    b, s, d = x.shape
    nc = s // SSM_CHUNK
    ts = POST_POS * nc
    npos = POST_SUB * POST_POS
    nt = SSM_CHUNK // npos
    t = b * s
    pm = lambda bi, j: (bi, j, 0, 0)
    const = lambda bi, j: (0, 0)
    tok = lambda bi, j: (0, bi * nt + j)
    tri = (lax.broadcasted_iota(I32, (ts, ts), 0) < lax.broadcasted_iota(I32, (ts, ts), 1)).astype(BF16)
    col = lambda a: a.reshape(-1, 1)
    return pl.pallas_call(
        _post_kernel,
        grid=(b, nt),
        in_specs=[pl.BlockSpec(memory_space=pl.ANY),
                  pl.BlockSpec((1, npos, nc, D_ATTN), pm),
                  pl.BlockSpec((1, N_GROUPS, npos * SSM_GROUP, nc), lambda bi, j: (bi, 0, j, 0)),
                  pl.BlockSpec((1, 6, d), lambda bi, j: (bi, 0, 0)),
                  pl.BlockSpec((D_SSM, D_SSM), const),
                  pl.BlockSpec((D_SSM, 1), const),
                  pl.BlockSpec((D_SSM, 1), const),
                  pl.BlockSpec((d, d), const),
                  pl.BlockSpec((1, d), const),
                  pl.BlockSpec((d, N_EXPERTS), const),
                  pl.BlockSpec((N_EXPERTS, 1), const),
                  pl.BlockSpec((ts, ts), const)],
        out_specs=[pl.BlockSpec((1, npos, nc, d), pm),
                   pl.BlockSpec((1, npos, nc, d), pm),
                   pl.BlockSpec((TOP_K, POST_SUB * ts), tok),
                   pl.BlockSpec((TOP_K, POST_SUB * ts), tok),
                   pl.BlockSpec((TOP_K, POST_SUB * ts), tok),
                   pl.BlockSpec((POST_SUB, N_EXPERTS, 1), lambda bi, j: (bi * nt + j, 0, 0)),
                   pl.BlockSpec((N_EXPERTS, 1), const)],
        out_shape=[jax.ShapeDtypeStruct((b, SSM_CHUNK, nc, d), F32),
                   jax.ShapeDtypeStruct((b, SSM_CHUNK, nc, d), BF16),
                   jax.ShapeDtypeStruct((TOP_K, t), I32),
                   jax.ShapeDtypeStruct((TOP_K, t), F32),
                   jax.ShapeDtypeStruct((TOP_K, t), I32),
                   jax.ShapeDtypeStruct((b * nt * POST_SUB, N_EXPERTS, 1), I32),
                   jax.ShapeDtypeStruct((N_EXPERTS, 1), I32)],
        scratch_shapes=[pltpu.VMEM((N_EXPERTS, 1), F32), pltpu.VMEM((2, npos, nc, d), F32),
                        pltpu.VMEM((D_SSM, D_SSM), BF16), pltpu.VMEM((d, d), BF16), pltpu.VMEM((N_EXPERTS, d), BF16),
                        pltpu.SemaphoreType.DMA((2,))],
        compiler_params=_cparams(2, ROW_VMEM_BYTES),
        name="post",
    )(x.reshape(b, nc, SSM_CHUNK, d), attn, yt, mod, w_glu, col(b_glu), col(ssm_norm), w_out,
      norm_ffn.reshape(1, -1), w_router, col(b_router), tri)


def _route_kernel(eidx_ref, lrank_ref, r0_ref, cnt_ref, ls_ref, tab_ref, te_ref, nv_ref, nx_ref, pad_ref):
    cnt = cnt_ref[...]
    tiles = (cnt + (RUN - 1 + FFN_ROWS - 1)) // FFN_ROWS
    er = lax.broadcasted_iota(I32, (N_EXPERTS, N_EXPERTS), 0)
    ec = lax.broadcasted_iota(I32, (N_EXPERTS, N_EXPERTS), 1)
    ltri = (ec < er).astype(BF16)

    def excl_cumsum(v):
        vb = jnp.broadcast_to(v.astype(F32), (N_EXPERTS, LANES)).astype(BF16)
        return jnp.dot(ltri, vb, preferred_element_type=F32)[:, 0:1].astype(I32)

    start_t = excl_cumsum(tiles)
    end_t = start_t + tiles
    start = start_t * FFN_ROWS
    pad_ref[...] = start + cnt

    nb = r0_ref.shape[0]
    ts = eidx_ref.shape[1] // nb
    iota_e = lax.broadcasted_iota(I32, (N_EXPERTS, ts), 0)
    iota_t = lax.broadcasted_iota(I32, (N_EXPERTS, TABW), 0)
    lane = lax.broadcasted_iota(I32, (1, TABW), 1)

    def block(b, carry):
        lanes = pl.ds(pl.multiple_of(b * ts, ts), ts)
        sels = [iota_e == eidx_ref[k:k + 1, lanes] for k in range(TOP_K)]
        member = sels[0].astype(I32) + sels[1].astype(I32) + sels[2].astype(I32) + sels[3].astype(I32)
        units = lax.shift_right_logical(jnp.sum(member, axis=1, keepdims=True) + (RUN - 1), RUN_SHIFT)
        u0 = excl_cumsum(units)
        for k in range(TOP_K):
            first = jnp.sum(jnp.where(sels[k], u0, 0), axis=0, keepdims=True)
            ls_ref[k:k + 1, lanes] = first * RUN + lrank_ref[k:k + 1, lanes]
        n_big = lax.shift_right_logical(units, 1)
        n_small = units & 1
        slot0 = start + r0_ref[b]

        def chunk_rows(idx, counts):
            c0 = excl_cumsum(counts)
            sel = iota_t == jnp.sum((idx >= c0 + counts).astype(I32), axis=0, keepdims=True)
            pick = lambda v: jnp.sum(jnp.where(sel, v, 0), axis=0, keepdims=True)
            j = idx - pick(c0)
            return pick(slot0), pick(u0), j, pick(n_big), idx < jnp.max(c0 + counts, axis=0, keepdims=True)

        s_b, u_b, j_b, _, ok_b = chunk_rows(lane, n_big)
        s_s, u_s, _, nb_s, ok_s = chunk_rows(lane - SMALL0, n_small)
        small = lane >= SMALL0
        slot = jnp.where(small, s_s + nb_s * BIG, s_b + j_b * BIG)
        local = jnp.where(small, (u_s + 2 * nb_s) * RUN, (u_b + 2 * j_b) * RUN)
        ok = (small & ok_s) | (jnp.logical_not(small) & ok_b)
        counts = jnp.where(lane == TABW - 2, jnp.sum(n_big, axis=0, keepdims=True),
                           jnp.sum(n_small, axis=0, keepdims=True))
        tab_ref[b, 0:1, :] = jnp.where(lane >= TABW - 2, counts, jnp.where(ok, slot, -1))
        tab_ref[b, 1:2, :] = jnp.where(ok, local, 0)
        return carry

    lax.fori_loop(0, nb, block, 0, unroll=2)

    nv = jnp.max(end_t, axis=0, keepdims=True)
    width = te_ref.shape[1]
    ti = jnp.minimum(lax.broadcasted_iota(I32, (N_EXPERTS, width), 1), nv - 1)
    te = jnp.minimum(jnp.sum((ti >= end_t).astype(I32), axis=0, keepdims=True), N_EXPERTS - 1)
    te_ref[...] = te
    nv_ref[...] = jnp.broadcast_to(nv, nv_ref.shape)
    ie = lax.broadcasted_iota(I32, (N_EXPERTS, width), 0)
    own_end = jnp.sum(jnp.where(ie == te, end_t, 0), axis=0, keepdims=True)
    nxt = jnp.minimum(jnp.sum((own_end >= end_t).astype(I32), axis=0, keepdims=True), N_EXPERTS - 1)
    nx_ref[...] = jnp.where(own_end < nv, nxt, -1)


def _route(eidx, lrank, r0, cnt, n_tiles):
    t = eidx.shape[1]
    nb = r0.shape[0]
    width = -(-n_tiles // LANES) * LANES
    return pl.pallas_call(
        _route_kernel,
        out_shape=[jax.ShapeDtypeStruct((TOP_K, t), I32),
                   jax.ShapeDtypeStruct((nb, 2, TABW), I32),
                   jax.ShapeDtypeStruct((1, width), I32),
                   jax.ShapeDtypeStruct((1, LANES), I32),
                   jax.ShapeDtypeStruct((1, width), I32),
                   jax.ShapeDtypeStruct((N_EXPERTS, 1), I32)],
        name="route",
    )(eidx, lrank, r0, cnt)


def _for_chunk_pairs(n, fn):
    def body(i, carry):
        fn(2 * i, 0)

        @pl.when(2 * i + 1 < n)
        def _():
            fn(2 * i + 1, 1)
        return carry
    lax.fori_loop(0, lax.shift_right_logical(n + 1, 1), body, 0)


def _for_block_chunks(tab_ref, blk, fn):
    base = blk * (2 * TABW)
    for first, count_lane, n_rows in ((0, TABW - 2, BIG), (SMALL0, TABW - 1, RUN)):
        def visit(c, parity, first=first, n_rows=n_rows):
            fn(tab_ref[base + first + c], pl.multiple_of(tab_ref[base + TABW + first + c], RUN), n_rows, parity)
        _for_chunk_pairs(tab_ref[base + count_lane], visit)


def _local_rows(ts):
    return ts * TOP_K + N_EXPERTS * RUN


def _dispatch_kernel(tab_ref, pad_ref, nvt_ref, h_ref, ls_ref, xs_ref, buf, zbuf, sems, zsem):
    b = pl.program_id(0)
    slot = b % 2
    ts = h_ref.shape[0]
    local = buf.shape[2]

    def chunk_copy(sl, slot_row, local_row, n):
        return pltpu.make_async_copy(buf.at[sl, :, pl.ds(local_row, n), :], xs_ref.at[:, pl.ds(slot_row, n), :],
                                     sems.at[sl])

    @pl.when(b == 0)
    def _():
        zbuf[...] = jnp.zeros_like(zbuf)
        zrows = zbuf.shape[1]
        zero = lambda row: pltpu.make_async_copy(zbuf, xs_ref.at[:, pl.ds(row, zrows), :], zsem)
        for phase in range(3):
            for e in range(phase, N_EXPERTS, 3):
                zero(pad_ref[e]).start()
            for e in range(phase, N_EXPERTS, 3):
                zero(pad_ref[e]).wait()
        ztile = lambda i: pltpu.make_async_copy(zbuf.at[:, pl.ds(0, FFN_ROWS), :],
                                                xs_ref.at[:, pl.ds((nvt_ref[0] + i) * FFN_ROWS, FFN_ROWS), :], zsem)

        def tail_start(i, carry):
            ztile(i).start()
            return carry

        def tail_wait(i, carry):
            ztile(i).wait()
            return carry
        lax.fori_loop(0, nvt_ref[1] - nvt_ref[0], tail_start, 0)
        lax.fori_loop(0, nvt_ref[1] - nvt_ref[0], tail_wait, 0)

    r = lax.broadcasted_iota(I32, (local, ts), 0)
    hit = (r == ls_ref[0:1, :]) | (r == ls_ref[1:2, :]) | (r == ls_ref[2:3, :]) | (r == ls_ref[3:4, :])
    hit = hit.astype(BF16)
    for pb in range(PANELS):
        srt = jnp.dot(hit, h_ref[:, pb * PANEL_COLS:(pb + 1) * PANEL_COLS], preferred_element_type=F32)
        buf[slot, pb] = _pack_panel(srt, exact=True)

    @pl.when(b > 0)
    def _():
        _for_block_chunks(tab_ref, b - 1, lambda s, l, n, p: chunk_copy(1 - slot, s, l, n).wait())

    _for_block_chunks(tab_ref, b, lambda s, l, n, p: chunk_copy(slot, s, l, n).start(priority=p))

    @pl.when(b == pl.num_programs(0) - 1)
    def _():
        _for_block_chunks(tab_ref, b, lambda s, l, n, p: chunk_copy(slot, s, l, n).wait())


def _dispatch(tab, pad, nvt, h2, ls, n_rows):
    t, d = h2.shape
    nb = tab.shape[0] // (2 * TABW)
    ts = t // nb
    return pl.pallas_call(
        _dispatch_kernel,
        grid_spec=pltpu.PrefetchScalarGridSpec(
            num_scalar_prefetch=3,
            grid=(nb,),
            in_specs=[pl.BlockSpec((ts, d), lambda i, *_: (i, 0)),
                      pl.BlockSpec((TOP_K, ts), lambda i, *_: (0, i))],
            out_specs=pl.BlockSpec(memory_space=pl.ANY),
            scratch_shapes=[pltpu.VMEM((2, PANELS, _local_rows(ts), LANES), U32),
                            pltpu.VMEM((PANELS, FFN_ROWS + RUN, LANES), U32),
                            pltpu.SemaphoreType.DMA((2,)), pltpu.SemaphoreType.DMA],
        ),
        out_shape=jax.ShapeDtypeStruct((PANELS, n_rows, LANES), U32),
        compiler_params=_cparams(1, ROW_VMEM_BYTES),
        name="dispatch",
    )(tab, pad, nvt, h2, ls)


def _ffn_kernel(te_ref, nv_ref, nx_ref, xs_ref, wgu_hbm, bgu_ref, wd_hbm, bd_ref, perm_ref, ys_ref,
                wgu_stage, wd_stage, wg_scr, wu_scr, wd_scr, bg_scr, bu_scr, sems):
    p = pl.program_id(0)
    t0 = 2 * p
    t1 = t0 + 1
    e0 = te_ref[t0]
    e1 = te_ref[t1]
    v0 = t0 < nv_ref[0]
    v1 = t1 < nv_ref[0]
    new0 = (p == 0) | (e0 != te_ref[jnp.maximum(t0 - 1, 0)])
    same = v1 & (e1 == e0)

    def stage_copies(e):
        return (pltpu.make_async_copy(wgu_hbm.at[e], wgu_stage, sems.at[0]),
                pltpu.make_async_copy(wd_hbm.at[e], wd_stage, sems.at[1]))

    def load_expert(t, first):
        e = te_ref[t]
        if first:
            @pl.when(p == 0)
            def _():
                for cp in stage_copies(e):
                    cp.start()

        for cp in stage_copies(e):
            cp.wait()
        bias = bgu_ref[e]
        for c in range(2 * D_FF // PERM):
            cols = slice(c * PERM, (c + 1) * PERM)
            half = slice(c * (PERM // 2), (c + 1) * (PERM // 2))
            w = wgu_stage[:, cols].astype(BF16)
            pw = jnp.dot(w, perm_ref[...], preferred_element_type=F32).astype(BF16)
            wg_scr[:, half] = pw[:, :PERM // 2]
            wu_scr[:, half] = pw[:, PERM // 2:]
            b1 = bias[:, cols].astype(BF16)
            r1 = bias[:, cols] - b1.astype(F32)
            b2 = r1.astype(BF16)
            b3 = (r1 - b2.astype(F32)).astype(BF16)
            terms = jnp.concatenate([b1, b2, b3, jnp.zeros((5, PERM), BF16)], axis=0)
            pb = jnp.sum(jnp.dot(terms, perm_ref[...], preferred_element_type=F32), axis=0, keepdims=True)
            bg_scr[:, half] = pb[:, :PERM // 2]
            bu_scr[:, half] = pb[:, PERM // 2:]
        wd_scr[...] = wd_stage[...].astype(BF16)

        @pl.when(nx_ref[t] >= 0)
        def _():
            for cp in stage_copies(nx_ref[t]):
                cp.start()

    def run(lo, n, e):
        x = _unpack_panels([xs_ref[pb, lo:lo + n, :] for pb in range(PANELS)])
        gate = jnp.dot(x, wg_scr[...], preferred_element_type=F32) + bg_scr[...]
        up = jnp.dot(x, wu_scr[...], preferred_element_type=F32) + bu_scr[...]
        gate = jnp.minimum(gate, SWIGLU_LIMIT)
        up = jnp.clip(up, -SWIGLU_LIMIT, SWIGLU_LIMIT)
        act = ((up + 1.0) * (gate * jax.nn.sigmoid(SWIGLU_ALPHA * gate))).astype(BF16)
        bd = bd_ref[e]
        for pb in range(PANELS):
            cols = slice(pb * PANEL_COLS, (pb + 1) * PANEL_COLS)
            y = jnp.dot(act, wd_scr[:, cols], preferred_element_type=F32) + bd[:, cols]
            ys_ref[pb, lo:lo + n, :] = _pack_panel(y)

    @pl.when(v0 & new0)
    def _():
        load_expert(t0, True)

    @pl.when(same)
    def _():
        run(0, 2 * FFN_ROWS, e0)

    @pl.when(v0 & jnp.logical_not(same))
    def _():
        run(0, FFN_ROWS, e0)

    @pl.when(v1 & jnp.logical_not(same))
    def _():
        load_expert(t1, False)
        run(FFN_ROWS, FFN_ROWS, e1)

    @pl.when(v0 & jnp.logical_not(v1))
    def _():
        ys_ref[:, FFN_ROWS:, :] = xs_ref[:, FFN_ROWS:, :]


def _ffn(te, nv, nx, xs, w_gate_up, bgu, w_down, bd, n_tiles):
    d = D_MODEL
    pair = lambda i, te, nv, nx: (0, jnp.minimum(i, lax.shift_right_logical(nv[0] - 1, 1)), 0)
    whole = lambda i, te, nv, nx: (0, 0, 0)
    r = lax.broadcasted_iota(I32, (PERM, PERM), 0)
    c = lax.broadcasted_iota(I32, (PERM, PERM), 1)
    perm = (r == jnp.where(c < PERM // 2, 2 * c, 2 * (c - PERM // 2) + 1)).astype(BF16)
    return pl.pallas_call(
        _ffn_kernel,
        grid_spec=pltpu.PrefetchScalarGridSpec(
            num_scalar_prefetch=3,
            grid=(n_tiles // 2,),
            in_specs=[pl.BlockSpec((PANELS, 2 * FFN_ROWS, LANES), pair),
                      pl.BlockSpec(memory_space=pl.ANY),
                      pl.BlockSpec((N_EXPERTS, 1, 2 * D_FF), whole),
                      pl.BlockSpec(memory_space=pl.ANY),
                      pl.BlockSpec((N_EXPERTS, 1, d), whole),
                      pl.BlockSpec((PERM, PERM), lambda i, te, nv, nx: (0, 0))],
            out_specs=pl.BlockSpec((PANELS, 2 * FFN_ROWS, LANES), pair),
            scratch_shapes=[pltpu.VMEM((d, 2 * D_FF), F32), pltpu.VMEM((D_FF, d), F32),
                            pltpu.VMEM((d, D_FF), BF16), pltpu.VMEM((d, D_FF), BF16), pltpu.VMEM((D_FF, d), BF16),
                            pltpu.VMEM((1, D_FF), F32), pltpu.VMEM((1, D_FF), F32),
                            pltpu.SemaphoreType.DMA((2,))],
        ),
        out_shape=jax.ShapeDtypeStruct(xs.shape, U32),
        input_output_aliases={3: 0},
        compiler_params=_cparams(1, FFN_VMEM_BYTES),
        name="ffn",
    )(te, nv, nx, xs, w_gate_up, bgu, w_down, bd, perm)


def _combine_kernel(tab_ref, x1_ref, ls_ref, w_ref, mod_ref, ys_ref, o4_hbm, ybuf, ob_buf, sems, osems, *, n_steps):
    jj = pl.program_id(1)
    g = pl.program_id(0) * pl.num_programs(1) + jj
    slot = g % 2
    nc = x1_ref.shape[2]
    tt = POST_POS * nc
    d = x1_ref.shape[3]
    local = ybuf.shape[3]
    npos = COMB_SUB * POST_POS

    def for_step_chunks(step, sl, fn):
        for sub in range(COMB_SUB):
            def visit(s, l, n, p, sub=sub):
                fn(pltpu.make_async_copy(ys_ref.at[:, pl.ds(s, n), :], ybuf.at[sl, sub, :, pl.ds(l, n), :],
                                         sems.at[sl]), p)
            _for_block_chunks(tab_ref, step * COMB_SUB + sub, visit)

    @pl.when(g == 0)
    def _():
        ybuf[...] = jnp.zeros_like(ybuf)
        for_step_chunks(0, 0, lambda cp, p: cp.start(priority=p))

    @pl.when(g + 1 < n_steps)
    def _():
        for_step_chunks(g + 1, 1 - slot, lambda cp, p: cp.start(priority=p))

    for_step_chunks(g, slot, lambda cp, p: cp.wait())

    def out_copies(sl, b_, j_):
        return [pltpu.make_async_copy(ob_buf.at[sl, il], o4_hbm.at[b_, :, npos * j_ + il, :], osems.at[sl])
                for il in range(npos)]

    @pl.when(g >= 2)
    def _():
        for cp in out_copies(slot, 0, 0):
            cp.wait()

    r = lax.broadcasted_iota(I32, (tt, local), 1).astype(F32)
    to_cols = lambda a: jnp.concatenate([a, jnp.zeros_like(a)], axis=0).T
    for sub in range(COMB_SUB):
        lanes = slice(sub * tt, (sub + 1) * tt)
        ls_c = to_cols(ls_ref[:, lanes].astype(F32))
        w_c = to_cols(w_ref[:, lanes])
        wm = jnp.zeros((tt, local), F32)
        for k in range(TOP_K):
            wm = jnp.where(r == ls_c[:, k:k + 1], w_c[:, k:k + 1], wm)
        y = _unpack_panels([ybuf[slot, sub, pb] for pb in range(PANELS)])
        acc = jnp.dot(wm.astype(BF16), y, preferred_element_type=F32)
        out = x1_ref[0, sub * POST_POS:(sub + 1) * POST_POS].reshape(tt, d) + mod_ref[0, 5:6, :] * acc
        for il in range(POST_POS):
            ob_buf[slot, sub * POST_POS + il] = out[il * nc:(il + 1) * nc]

    for cp in out_copies(slot, pl.program_id(0), jj):
        cp.start()

    @pl.when(g == n_steps - 1)
    def _():
        for cp in out_copies(slot, 0, 0):
            cp.wait()
        if n_steps > 1:
            for cp in out_copies(1 - slot, 0, 0):
                cp.wait()


def _combine(tab, x1, ls, wts, mod, ys):
    b, _, nc, d = x1.shape
    s = SSM_CHUNK * nc
    tt = POST_POS * nc
    npos = COMB_SUB * POST_POS
    nt = SSM_CHUNK // npos
    o4 = pl.pallas_call(
        functools.partial(_combine_kernel, n_steps=b * nt),
        grid_spec=pltpu.PrefetchScalarGridSpec(
            num_scalar_prefetch=1,
            grid=(b, nt),
            in_specs=[pl.BlockSpec((1, npos, nc, d), lambda bi, j, *_: (bi, j, 0, 0)),
                      pl.BlockSpec((TOP_K, COMB_SUB * tt), lambda bi, j, *_: (0, bi * nt + j)),
                      pl.BlockSpec((TOP_K, COMB_SUB * tt), lambda bi, j, *_: (0, bi * nt + j)),
                      pl.BlockSpec((1, 6, d), lambda bi, j, *_: (bi, 0, 0)),
                      pl.BlockSpec(memory_space=pl.ANY)],
            out_specs=pl.BlockSpec(memory_space=pl.ANY),
            scratch_shapes=[pltpu.VMEM((2, COMB_SUB, PANELS, _local_rows(tt), LANES), U32),
                            pltpu.VMEM((2, npos, nc, d), F32),
                            pltpu.SemaphoreType.DMA((2,)), pltpu.SemaphoreType.DMA((2,))],
        ),
        out_shape=jax.ShapeDtypeStruct((b, nc, SSM_CHUNK, d), F32),
        compiler_params=_cparams(2),
        name="combine",
    )(tab, x1, ls, wts, mod, ys)
    return o4.reshape(b, s, d)


def kernel(x, c, w_ada, b_ada, norm_mix, w_in, b_in, q_norm, k_norm, sinks, lam_re, lam_im, log_dt, b_re, b_im,
           c_re, c_im, d_skip, w_glu, b_glu, attn_out_norm, ssm_out_norm, w_out, norm_ffn, w_router, b_router,
           w_gate_up, b_gate_up, w_down, b_down):
    b, s, d = x.shape
    t = b * s
    depth = w_ada.shape[0]
    n_tiles = -(-(t * TOP_K + N_EXPERTS * (RUN - 1 + FFN_ROWS - 1)) // FFN_ROWS)
    n_tiles += n_tiles % 2
    n_alloc = n_tiles + 2
    for l in range(depth):
        mod = _adaln(c, w_ada[l], b_ada[l]).reshape(b, 6, d)
        q, k, v, ut = _inproj(x, mod, norm_mix[l], w_in[l], b_in[l])
        attn = _attention(q, k, v, sinks[l], q_norm[l], k_norm[l], attn_out_norm[l])
        tt, wz, wyt, cs = _ssm_params(lam_re[l], lam_im[l], log_dt[l], b_re[l], b_im[l], c_re[l], c_im[l])
        yt = _ssm(ut, tt, wz, wyt, cs, d_skip[l])
        x1, h2, eidx, wts, lrank, r0, cnt = _post(x, attn, yt, mod, w_glu[l], b_glu[l], ssm_out_norm[l], w_out[l],
                                                  norm_ffn[l], w_router[l], b_router[l])
        ls, tab, te, nv, nx, pad = _route(eidx, lrank, r0, cnt, n_tiles)
        tab = tab.reshape(-1)
        nvt = jnp.stack([nv[0, 0], jnp.int32(n_alloc)])
        xs = _dispatch(tab, pad.reshape(-1), nvt, h2.reshape(t, d), ls, n_alloc * FFN_ROWS)
        ys = _ffn(te[0, :n_tiles], nv[0, :1], nx[0, :n_tiles], xs, w_gate_up[l], b_gate_up[l][:, None, :],
                  w_down[l], b_down[l][:, None, :], n_tiles)
        x = _combine(tab, x1, ls, wts, mod, ys)
    return x
```

```python
import functools
import math

import jax
import jax.numpy as jnp
from jax import lax
from jax.experimental import pallas as pl
from jax.experimental.pallas import tpu as pltpu

F32 = jnp.float32
BF16 = jnp.bfloat16
U32 = jnp.uint32
I32 = jnp.int32

D_MODEL = 1024
HEAD_DIM = 64
N_HEADS = 8
N_KV_HEADS = 2
Q_PER_KV = N_HEADS // N_KV_HEADS
D_ATTN = N_HEADS * HEAD_DIM
D_KV = N_KV_HEADS * HEAD_DIM
D_QKV = D_ATTN + 2 * D_KV
WINDOW = 128
BLOCK = 128
D_SSM = D_MODEL - D_ATTN
SSM_GROUP = 16
N_GROUPS = D_SSM // SSM_GROUP
STATE = 64
N_EXPERTS = 32
TOP_K = 4
D_FF = D_MODEL
SWIGLU_LIMIT = 7.0
SWIGLU_ALPHA = 1.702
EPS = 1e-6
NEG_INF = -1e30

LANES = 128
SSM_CHUNK = 16
SSM_ROW = SSM_CHUNK * SSM_GROUP
N_POW = 2 * SSM_CHUNK
PANEL_COLS = 2 * LANES
PANELS = D_MODEL // PANEL_COLS

SSM_GROUPS_PER_STEP = 8
POS_PER_STEP = 8
ATTN_ROWS = 512
POST_POS = 2
POST_SUB = 4
COMB_SUB = 2
FFN_ROWS = 256
RUN = 8
RUN_SHIFT = 3
BIG = 2 * RUN
TABW = 128
SMALL0 = 80
PERM = 256
FFN_VMEM_BYTES = 40 * 1024 * 1024
ROW_VMEM_BYTES = 48 * 1024 * 1024

HIGHEST = lax.Precision.HIGHEST
_ARB = "arbitrary"


def _cparams(n, vmem=None):
    return pltpu.CompilerParams(dimension_semantics=(_ARB,) * n, vmem_limit_bytes=vmem)


def _rms(x, axis=-1):
    return x * lax.rsqrt(jnp.mean(x * x, axis=axis, keepdims=True) + EPS)


def _pack_panel(y, exact=False):
    hi, lo = y[:, :LANES], y[:, LANES:]
    if not exact:
        hi = hi.astype(BF16).astype(F32)
        lo = lo.astype(BF16).astype(F32)
    return lax.bitcast_convert_type(hi, U32) | (lax.bitcast_convert_type(lo, U32) >> 16)


def _unpack_panels(words):
    cols = []
    for w in words:
        cols.append(lax.bitcast_convert_type(w & jnp.uint32(0xFFFF0000), F32).astype(BF16))
        cols.append(lax.bitcast_convert_type(w << 16, F32).astype(BF16))
    return jnp.concatenate(cols, axis=-1)


def _prefetch_pos_rows(x4_hbm, buf, sems, n_pos):
    bi = pl.program_id(0)
    j = pl.program_id(1)
    nj = pl.num_programs(1)
    g = bi * nj + j
    slot = g % 2

    def copies(sl, b_, j_):
        return [pltpu.make_async_copy(x4_hbm.at[b_, :, n_pos * j_ + il, :], buf.at[sl, il], sems.at[sl])
                for il in range(n_pos)]

    @pl.when(g == 0)
    def _():
        for cp in copies(0, 0, 0):
            cp.start()

    @pl.when(g + 1 < pl.num_programs(0) * nj)
    def _():
        wrap = j + 1 == nj
        for cp in copies(1 - slot, jnp.where(wrap, bi + 1, bi), jnp.where(wrap, 0, j + 1)):
            cp.start()

    for cp in copies(slot, bi, j):
        cp.wait()
    return slot


def _to_lane_blocks(dst, src):
    for kb in range(dst.shape[0]):
        dst[kb] = src[:, kb * LANES:(kb + 1) * LANES]


def _adaln_kernel(c_ref, w_ref, b_ref, o_ref):
    c = c_ref[...]
    ca = c * jax.nn.sigmoid(c)
    o_ref[...] = jnp.dot(ca, w_ref[...], preferred_element_type=F32, precision=HIGHEST) + b_ref[...]


def _adaln(c, w_ada, b_ada):
    b, d = c.shape
    n = w_ada.shape[1] // d
    return pl.pallas_call(
        _adaln_kernel,
        grid=(n,),
        in_specs=[pl.BlockSpec((b, d), lambda j: (0, 0)),
                  pl.BlockSpec((d, d), lambda j: (0, j)),
                  pl.BlockSpec((1, d), lambda j: (0, j))],
        out_specs=pl.BlockSpec((b, d), lambda j: (0, j)),
        out_shape=jax.ShapeDtypeStruct((b, n * d), F32),
        compiler_params=_cparams(1),
        name="adaln",
    )(c, w_ada, b_ada.reshape(1, -1))


def _inproj_kernel(x4_hbm, x_ref, mod_ref, g_ref, w_ref, bqkv_ref, but_ref, q_ref, k_ref, v_ref, ut_ref,
                   xp_buf, wqkv_scr, wut_scr, sems):
    nc = ut_ref.shape[3]

    @pl.when((pl.program_id(0) == 0) & (pl.program_id(1) == 0))
    def _():
        wqkv_scr[...] = w_ref[:, :D_QKV].astype(BF16)
        wut_scr[...] = w_ref[:, D_QKV:].T.astype(BF16)

    slot = _prefetch_pos_rows(x4_hbm, xp_buf, sems, POS_PER_STEP)
    gain = g_ref[...]
    scale = 1.0 + mod_ref[0, 1:2, :]
    shift = mod_ref[0, 0:1, :]

    def norm_mod(x):
        return (_rms(x) * gain * scale + shift).astype(BF16)

    proj = jnp.dot(norm_mod(x_ref[0]), wqkv_scr[...], preferred_element_type=F32) + bqkv_ref[...]
    q_ref[0] = proj[:, :D_ATTN].astype(BF16)
    k_ref[0] = proj[:, D_ATTN:D_ATTN + D_KV].astype(BF16)
    v_ref[0] = proj[:, D_ATTN + D_KV:].astype(BF16)

    hs = jnp.concatenate([norm_mod(xp_buf[slot, il]) for il in range(POS_PER_STEP)], axis=0)
    ut = lax.dot_general(wut_scr[...], hs, (((1,), (1,)), ((), ())), preferred_element_type=F32) + but_ref[...]
    for il in range(POS_PER_STEP):
        piece = ut[:, il * nc:(il + 1) * nc].astype(BF16)
        ut_ref[0, :, il * SSM_GROUP:(il + 1) * SSM_GROUP, :] = piece.reshape(N_GROUPS, SSM_GROUP, nc)


def _inproj(x, mod, gain, w_in, b_in):
    b, s, d = x.shape
    nc = s // SSM_CHUNK
    rows = POS_PER_STEP * nc
    row = lambda bi, j: (bi, j, 0)
    const = lambda bi, j: (0, 0)
    return pl.pallas_call(
        _inproj_kernel,
        grid=(b, SSM_CHUNK // POS_PER_STEP),
        in_specs=[pl.BlockSpec(memory_space=pl.ANY),
                  pl.BlockSpec((1, rows, d), row),
                  pl.BlockSpec((1, 6, d), lambda bi, j: (bi, 0, 0)),
                  pl.BlockSpec((1, d), const),
                  pl.BlockSpec((d, D_QKV + D_SSM), const),
                  pl.BlockSpec((1, D_QKV), const),
                  pl.BlockSpec((D_SSM, 1), const)],
        out_specs=[pl.BlockSpec((1, rows, D_ATTN), row),
                   pl.BlockSpec((1, rows, D_KV), row),
                   pl.BlockSpec((1, rows, D_KV), row),
                   pl.BlockSpec((1, N_GROUPS, POS_PER_STEP * SSM_GROUP, nc), lambda bi, j: (bi, 0, j, 0))],
        out_shape=[jax.ShapeDtypeStruct((b, s, D_ATTN), BF16),
                   jax.ShapeDtypeStruct((b, s, D_KV), BF16),
                   jax.ShapeDtypeStruct((b, s, D_KV), BF16),
                   jax.ShapeDtypeStruct((b, N_GROUPS, SSM_ROW, nc), BF16)],
        scratch_shapes=[pltpu.VMEM((2, POS_PER_STEP, nc, d), F32), pltpu.VMEM((d, D_QKV), BF16),
                        pltpu.VMEM((D_SSM, d), BF16), pltpu.SemaphoreType.DMA((2,))],
        compiler_params=_cparams(2, ROW_VMEM_BYTES),
        name="inproj",
    )(x.reshape(b, nc, SSM_CHUNK, d), x, mod, gain.reshape(1, d), w_in, b_in[:D_QKV].reshape(1, D_QKV),
      b_in[D_QKV:].reshape(D_SSM, 1))


def _half_norm(x, low):
    sq = x * x
    s_lo = jnp.sum(jnp.where(low, sq, 0.0), axis=-1, keepdims=True)
    s_hi = jnp.sum(sq, axis=-1, keepdims=True) - s_lo
    inv = 1.0 / HEAD_DIM
    scale = jnp.where(low, lax.rsqrt(s_lo * inv + EPS), lax.rsqrt(s_hi * inv + EPS))
    return x * scale


def _attn_block(first, q, k_prev, k_cur, v_prev, v_cur, sinks_ref, qn, low, upper, rblk):
    no_prev = jnp.where(first, NEG_INF, 0.0)
    out_blocks = []
    for hk in range(N_KV_HEADS):
        qs = []
        for j in range(Q_PER_KV // 2):
            blk = hk * (Q_PER_KV // 2) + j
            qb = _half_norm(q[:, blk * LANES:(blk + 1) * LANES], low) * qn * (1.0 / math.sqrt(HEAD_DIM))
            qs.append(jnp.where(low, qb, 0.0))
            qs.append(jnp.where(low, 0.0, qb))
        qg = jnp.concatenate(qs, axis=0).astype(BF16)
        nt = (((1,), (1,)), ((), ()))
        s_prev = lax.dot_general(qg, k_prev[hk], nt, preferred_element_type=F32)
        s_cur = lax.dot_general(qg, k_cur[hk], nt, preferred_element_type=F32)
        s = jnp.where(upper, s_prev + no_prev, s_cur)
        sink = jnp.zeros((Q_PER_KV * BLOCK, 1), F32)
        for g in range(Q_PER_KV):
            sink = jnp.where(rblk == g, sinks_ref[hk * Q_PER_KV + g], sink)
        m = jnp.maximum(jnp.max(s, axis=-1, keepdims=True), sink)
        p = jnp.exp(s - m)
        den = jnp.sum(p, axis=-1, keepdims=True) + jnp.exp(sink - m)
        o = (jnp.dot(jnp.where(upper, p, 0.0).astype(BF16), v_prev[hk], preferred_element_type=F32)
             + jnp.dot(jnp.where(upper, 0.0, p).astype(BF16), v_cur[hk], preferred_element_type=F32)) / den
        for j in range(Q_PER_KV // 2):
            ev = o[(2 * j) * BLOCK:(2 * j + 1) * BLOCK]
            od = o[(2 * j + 1) * BLOCK:(2 * j + 2) * BLOCK]
            out_blocks.append(jnp.where(low, ev, od))
    return jnp.concatenate(out_blocks, axis=-1)


def _attn_kernel(sinks_ref, q_ref, k_ref, v_ref, qn_ref, kn_ref, on_ref, o_hbm, a_buf, sems, *, n_steps):
    step = pl.program_id(1)
    g = pl.program_id(0) * pl.num_programs(1) + step
    slot = g % 2
    cps = ATTN_ROWS // SSM_CHUNK
    nq = ATTN_ROWS // BLOCK

    def out_copies(sl, b_, s_):
        return [pltpu.make_async_copy(a_buf.at[sl, :, i, :], o_hbm.at[b_, i, pl.ds(s_ * cps, cps), :], sems.at[sl])
                for i in range(SSM_CHUNK)]

    @pl.when(g >= 2)
    def _():
        for cp in out_copies(slot, 0, 0):
            cp.wait()

    low = lax.broadcasted_iota(I32, (1, LANES), 1) < HEAD_DIM
    rows = Q_PER_KV * BLOCK
    upper = lax.broadcasted_iota(I32, (rows, BLOCK), 1) > lax.broadcasted_iota(I32, (rows, BLOCK), 0) % BLOCK
    rblk = lax.broadcasted_iota(I32, (rows, 1), 0) // BLOCK

    cur = pl.multiple_of(step * ATTN_ROWS, ATTN_ROWS)
    prev = pl.multiple_of(jnp.maximum(step * nq - 1, 0) * BLOCK, BLOCK)
    kall = jnp.concatenate([k_ref[0, pl.ds(prev, BLOCK), :], k_ref[0, pl.ds(cur, ATTN_ROWS), :]], axis=0).astype(F32)
    vall = jnp.concatenate([v_ref[0, pl.ds(prev, BLOCK), :], v_ref[0, pl.ds(cur, ATTN_ROWS), :]], axis=0).astype(F32)
    kall = _half_norm(kall, low) * kn_ref[...]
    kswap = pltpu.roll(kall, HEAD_DIM, axis=1)
    vswap = pltpu.roll(vall, HEAD_DIM, axis=1)
    k_dup = [jnp.where(low, kall, kswap).astype(BF16), jnp.where(low, kswap, kall).astype(BF16)]
    v_dup = [jnp.where(low, vall, vswap).astype(BF16), jnp.where(low, vswap, vall).astype(BF16)]
    blk = lambda a, i: [a[hk][i * BLOCK:(i + 1) * BLOCK] for hk in range(N_KV_HEADS)]

    for qb in range(nq):
        q = q_ref[0, qb * BLOCK:(qb + 1) * BLOCK, :].astype(F32)
        attn = _attn_block((step == 0) if qb == 0 else False, q, blk(k_dup, qb), blk(k_dup, qb + 1),
                           blk(v_dup, qb), blk(v_dup, qb + 1), sinks_ref, qn_ref[...], low, upper, rblk)
        attn = _rms(attn) * on_ref[...]
        cpb = BLOCK // SSM_CHUNK
        a_buf[slot, qb * cpb:(qb + 1) * cpb] = attn.reshape(cpb, SSM_CHUNK, D_ATTN)

    for cp in out_copies(slot, pl.program_id(0), step):
        cp.start()

    @pl.when(g == n_steps - 1)
    def _():
        for cp in out_copies(slot, 0, 0):
            cp.wait()
        if n_steps > 1:
            for cp in out_copies(1 - slot, 0, 0):
                cp.wait()


def _attention(q, k, v, sinks, q_norm, k_norm, out_norm):
    b, s, _ = q.shape
    tile2 = lambda g: jnp.tile(g.reshape(1, HEAD_DIM), (1, 2))
    cps = ATTN_ROWS // SSM_CHUNK
    return pl.pallas_call(
        functools.partial(_attn_kernel, n_steps=b * (s // ATTN_ROWS)),
        grid=(b, s // ATTN_ROWS),
        in_specs=[pl.BlockSpec(memory_space=pltpu.SMEM),
                  pl.BlockSpec((1, ATTN_ROWS, D_ATTN), lambda bi, n: (bi, n, 0)),
                  pl.BlockSpec((1, s, D_KV), lambda bi, n: (bi, 0, 0)),
                  pl.BlockSpec((1, s, D_KV), lambda bi, n: (bi, 0, 0)),
                  pl.BlockSpec((1, LANES), lambda bi, n: (0, 0)),
                  pl.BlockSpec((1, LANES), lambda bi, n: (0, 0)),
                  pl.BlockSpec((1, D_ATTN), lambda bi, n: (0, 0))],
        out_specs=pl.BlockSpec(memory_space=pl.ANY),
        out_shape=jax.ShapeDtypeStruct((b, SSM_CHUNK, s // SSM_CHUNK, D_ATTN), F32),
        scratch_shapes=[pltpu.VMEM((2, cps, SSM_CHUNK, D_ATTN), F32), pltpu.SemaphoreType.DMA((2,))],
        compiler_params=_cparams(2),
        name="attention",
    )(sinks, q, k, v, tile2(q_norm), tile2(k_norm), out_norm.reshape(1, D_ATTN))


def _cmul(ar, ai, br, bi):
    return ar * br - ai * bi, ar * bi + ai * br


def _ssm_param_kernel(*refs):
    for gi in range(refs[0].shape[0]):
        _ssm_param_group(gi, *refs)


def _ssm_param_group(gi, lam_ref, bre_ref, bim_ref, cre_ref, cim_ref, tt_ref, wz_ref, wyt_ref, cs_ref):
    f32dot = functools.partial(jnp.dot, preferred_element_type=F32, precision=HIGHEST)
    lr = lam_ref[gi, 0:1, :]
    li = lam_ref[gi, 1:2, :]
    dt = jnp.exp(lam_ref[gi, 2:3, :])
    rho = lr * dt
    th = li * dt
    imag_lane = lax.broadcasted_iota(I32, (1, LANES), 1) >= STATE

    kk = (lax.broadcasted_iota(I32, (N_POW, 1), 0) - (SSM_CHUNK - 1)).astype(F32)
    mag = jnp.exp(rho * kk)
    pw_r = mag * jnp.cos(th * kk)
    pw_i = mag * jnp.sin(th * kk)
    lb_r = pw_r[SSM_CHUNK:SSM_CHUNK + 1]
    lb_i = pw_i[SSM_CHUNK:SSM_CHUNK + 1]
    den = lr * lr + li * li
    coef_r = ((lb_r - 1.0) * lr + lb_i * li) / den
    coef_i = (lb_i * lr - (lb_r - 1.0) * li) / den

    eye = (lax.broadcasted_iota(I32, (SSM_GROUP, SSM_GROUP), 0)
           == lax.broadcasted_iota(I32, (SSM_GROUP, SSM_GROUP), 1)).astype(F32)
    lane_fold = (lax.broadcasted_iota(I32, (STATE, LANES), 1) % STATE
                 == lax.broadcasted_iota(I32, (STATE, LANES), 0)).astype(F32)

    def tile_pos(x):
        return jnp.concatenate([x] * SSM_CHUNK, axis=0)

    def power_rows(k_of_pos):
        idx = [k_of_pos(p) + (SSM_CHUNK - 1) for p in range(SSM_CHUNK)]
        rep = lambda t: jnp.concatenate([jnp.broadcast_to(t[r:r + 1], (SSM_GROUP, LANES)) for r in idx], axis=0)
        return rep(pw_r), rep(pw_i)

    def b_rows(b_ref):
        b2 = jnp.concatenate([b_ref[gi], b_ref[gi]], axis=0)
        return tile_pos(lax.dot_general(eye, b2, (((1,), (1,)), ((), ())), preferred_element_type=F32,
                                        precision=HIGHEST))

    def c_rows(c_ref):
        return tile_pos(f32dot(c_ref[gi], lane_fold))

    bbar_r, bbar_i = _cmul(coef_r, coef_i, b_rows(bre_ref), b_rows(bim_ref))
    c_r = c_rows(cre_ref)
    c_i = c_rows(cim_ref)

    a_r, a_i = _cmul(bbar_r, bbar_i, *power_rows(lambda p: -p))
    a2c = jnp.where(imag_lane, -a_i, a_r)
    m_r, m_i = _cmul(c_r, c_i, *power_rows(lambda p: p))
    bmc = jnp.where(imag_lane, m_i, m_r)
    tt = f32dot(bmc, a2c.T)
    causal = (lax.broadcasted_iota(I32, (SSM_ROW, 1), 0) // SSM_GROUP
              >= lax.broadcasted_iota(I32, (1, SSM_ROW), 1) // SSM_GROUP)
    tt_ref[gi] = jnp.where(causal, tt, 0.0).astype(BF16)

    w_r, w_i = _cmul(bbar_r, bbar_i, *power_rows(lambda p: SSM_CHUNK - 1 - p))
    wz_ref[gi, :, :LANES] = jnp.where(imag_lane, w_i, w_r).astype(BF16)
    wz_ref[gi, :, LANES:] = jnp.where(imag_lane, w_r, w_i).astype(BF16)

    y_r, y_i = _cmul(c_r, c_i, *power_rows(lambda p: p + 1))
    wyt_ref[gi] = jnp.where(imag_lane, -y_i, y_r).astype(BF16)

    cs_ref[gi, 0:1, :] = pw_r[N_POW - 1:N_POW]
    cs_ref[gi, 1:2, :] = jnp.where(imag_lane, pw_i[N_POW - 1:N_POW], -pw_i[N_POW - 1:N_POW])


def _ssm_params(lam_re, lam_im, log_dt, b_re, b_im, c_re, c_im):
    g = lam_re.shape[0]
    lam = jnp.stack([lam_re, lam_im, jnp.broadcast_to(log_dt[:, None], (g, STATE))], axis=1)
    lam = jnp.concatenate([lam, lam], axis=2)
    ng = SSM_GROUPS_PER_STEP
    blk = lambda *shape: pl.BlockSpec((ng,) + shape, lambda i: (i, 0, 0))
    return pl.pallas_call(
        _ssm_param_kernel,
        grid=(g // ng,),
        in_specs=[blk(3, LANES), blk(STATE, SSM_GROUP), blk(STATE, SSM_GROUP), blk(SSM_GROUP, STATE),
                  blk(SSM_GROUP, STATE)],
        out_specs=[blk(SSM_ROW, SSM_ROW), blk(SSM_ROW, SSM_ROW), blk(SSM_ROW, LANES), blk(2, LANES)],
        out_shape=[jax.ShapeDtypeStruct((g, SSM_ROW, SSM_ROW), BF16),
                   jax.ShapeDtypeStruct((g, SSM_ROW, SSM_ROW), BF16),
                   jax.ShapeDtypeStruct((g, SSM_ROW, LANES), BF16),
                   jax.ShapeDtypeStruct((g, 2, LANES), F32)],
        compiler_params=_cparams(1),
        name="ssm_params",
    )(lam, b_re, b_im, c_re, c_im)


def _ssm_kernel(ut_ref, tt_ref, wz_ref, wyt_ref, cs_ref, d_ref, yt_ref, z_scr, s_scr):
    batch, ng, _, nc = ut_ref.shape
    uts = [jnp.concatenate([ut_ref[b, gi] for b in range(batch)], axis=1) for gi in range(ng)]
    for gi in range(ng):
        z = lax.dot_general(uts[gi], wz_ref[gi], (((0,), (0,)), ((), ())), preferred_element_type=F32)
        _to_lane_blocks(z_scr.at[gi], z)
    c1 = [cs_ref[gi, 0:1, :] for gi in range(ng)]
    c2 = [cs_ref[gi, 1:2, :] for gi in range(ng)]

    def step(c, carry):
        rows = pl.ds(c, batch, stride=nc)
        out = []
        for gi in range(ng):
            s1, s2 = carry[gi]
            s_scr[gi, rows, :] = s1
            out.append((c1[gi] * s1 + c2[gi] * s2 + z_scr[gi, 0, rows, :],
                        c1[gi] * s2 - c2[gi] * s1 + z_scr[gi, 1, rows, :]))
        return tuple(out)

    zero = jnp.zeros((batch, LANES), F32)
    lax.fori_loop(0, nc, step, ((zero, zero),) * ng, unroll=8)
    for gi in range(ng):
        y = jnp.dot(tt_ref[gi], uts[gi], preferred_element_type=F32)
        y = y + lax.dot_general(wyt_ref[gi], s_scr[gi].astype(BF16), (((1,), (1,)), ((), ())),
                                preferred_element_type=F32)
        y = y + d_ref[gi] * uts[gi].astype(F32)
        for b in range(batch):
            yt_ref[b, gi] = y[:, b * nc:(b + 1) * nc]


def _ssm(ut, tt, wz, wyt, cs, d_skip):
    b, g, _, nc = ut.shape
    ng = SSM_GROUPS_PER_STEP
    d_col = jnp.tile(d_skip.reshape(g, 1, SSM_GROUP), (1, SSM_CHUNK, 1)).reshape(g, SSM_ROW, 1)
    blk = lambda *shape: pl.BlockSpec((ng,) + shape, lambda i: (i, 0, 0))
    act = pl.BlockSpec((b, ng, SSM_ROW, nc), lambda i: (0, i, 0, 0))
    return pl.pallas_call(
        _ssm_kernel,
        grid=(g // ng,),
        in_specs=[act, blk(SSM_ROW, SSM_ROW), blk(SSM_ROW, SSM_ROW), blk(SSM_ROW, LANES), blk(2, LANES),
                  blk(SSM_ROW, 1)],
        out_specs=act,
        out_shape=jax.ShapeDtypeStruct((b, g, SSM_ROW, nc), F32),
        scratch_shapes=[pltpu.VMEM((ng, SSM_ROW // LANES, b * nc, LANES), F32), pltpu.VMEM((ng, b * nc, LANES), F32)],
        compiler_params=_cparams(1),
        name="ssm",
    )(ut, tt, wz, wyt, cs, d_col)


def _post_kernel(x4_hbm, attn_ref, yt_ref, mod_ref, wglu_ref, bglu_ref, sn_ref, wout_f32_ref, nf_ref, wrt_ref, br_ref,
                 tri_ref, x1_ref, h2_ref, eidx_ref, wts_ref, lrank_ref, r0_ref, cnt_ref, carry_ref, xp_buf,
                 wglut_ref, wout_ref, wr_ref, sems):
    @pl.when((pl.program_id(0) == 0) & (pl.program_id(1) == 0))
    def _():
        carry_ref[...] = jnp.zeros_like(carry_ref)
        wglut_ref[...] = wglu_ref[...].T.astype(BF16)
        wout_ref[...] = wout_f32_ref[...].astype(BF16)
        wr_ref[...] = wrt_ref[...].T.astype(BF16)

    slot = _prefetch_pos_rows(x4_hbm, xp_buf, sems, POST_SUB * POST_POS)
    nc = attn_ref.shape[2]
    ts = POST_POS * nc
    d = x1_ref.shape[3]
    iota_e = lax.broadcasted_iota(I32, (N_EXPERTS, ts), 0).astype(F32)
    counts = []
    for sub in range(POST_SUB):
        pos = range(sub * POST_POS, (sub + 1) * POST_POS)
        lanes = slice(sub * ts, (sub + 1) * ts)
        yt = jnp.concatenate(
            [yt_ref[0, :, il * SSM_GROUP:(il + 1) * SSM_GROUP, :].reshape(D_SSM, nc) for il in pos], axis=1)
        g = jax.nn.gelu(yt)
        gate = jax.nn.sigmoid(jnp.dot(wglut_ref[...], g.astype(BF16), preferred_element_type=F32) + bglu_ref[...])
        ssm_t = _rms(g * gate, axis=0) * sn_ref[...]
        attn = attn_ref[0, sub * POST_POS:(sub + 1) * POST_POS].reshape(ts, D_ATTN)
        mixed = jnp.concatenate([attn.astype(BF16), ssm_t.T.astype(BF16)], axis=-1)
        o = jnp.dot(mixed, wout_ref[...], preferred_element_type=F32)
        x = jnp.concatenate([xp_buf[slot, il] for il in pos], axis=0)
        x1 = x + mod_ref[0, 2:3, :] * o
        x1_ref[0, sub * POST_POS:(sub + 1) * POST_POS] = x1.reshape(POST_POS, nc, d)
        h2 = _rms(x1) * nf_ref[...] * (1.0 + mod_ref[0, 4:5, :]) + mod_ref[0, 3:4, :]
        h2_ref[0, sub * POST_POS:(sub + 1) * POST_POS] = h2.astype(BF16).reshape(POST_POS, nc, d)

        logits = lax.dot_general(wr_ref[...], h2.astype(BF16), (((1,), (1,)), ((), ())),
                                 preferred_element_type=F32) + br_ref[...]
        l = logits
        idxs, vals = [], []
        for _ in range(TOP_K):
            m = jnp.max(l, axis=0, keepdims=True)
            idx = jnp.min(jnp.where(l == m, iota_e, float(N_EXPERTS)), axis=0, keepdims=True)
            idxs.append(idx)
            vals.append(m)
            l = jnp.where(iota_e == idx, -jnp.inf, l)
        es = [jnp.exp(v - vals[0]) for v in vals]
        tot = es[0] + es[1] + es[2] + es[3]
        member = jnp.zeros((N_EXPERTS, ts), F32)
        for idx in idxs:
            member = member + (iota_e == idx).astype(F32)
        before = jnp.dot(member.astype(BF16), tri_ref[...], preferred_element_type=F32)
        for k in range(TOP_K):
            eidx_ref[k:k + 1, lanes] = idxs[k].astype(I32)
            wts_ref[k:k + 1, lanes] = es[k] / tot
            lrank_ref[k:k + 1, lanes] = jnp.sum(jnp.where(iota_e == idxs[k], before, 0.0), axis=0,
                                                keepdims=True).astype(I32)
        counts.append(jnp.sum(member, axis=1, keepdims=True))

    carry = carry_ref[...]
    for sub in range(POST_SUB):
        r0_ref[sub] = carry.astype(I32)
        carry = carry + counts[sub]
    carry_ref[...] = carry
    cnt_ref[...] = carry.astype(I32)


def _post(x, attn, yt, mod, w_glu, b_glu, ssm_norm, w_out, norm_ffn, w_router, b_router):
    b, s, d = x.shape
    nc = s // SSM_CHUNK
    ts = POST_POS * nc
    npos = POST_SUB * POST_POS
    nt = SSM_CHUNK // npos
    t = b * s
    pm = lambda bi, j: (bi, j, 0, 0)
    const = lambda bi, j: (0, 0)
    tok = lambda bi, j: (0, bi * nt + j)
    tri = (lax.broadcasted_iota(I32, (ts, ts), 0) < lax.broadcasted_iota(I32, (ts, ts), 1)).astype(BF16)
    col = lambda a: a.reshape(-1, 1)
    return pl.pallas_call(
        _post_kernel,
        grid=(b, nt),
        in_specs=[pl.BlockSpec(memory_space=pl.ANY),
                  pl.BlockSpec((1, npos, nc, D_ATTN), pm),
                  pl.BlockSpec((1, N_GROUPS, npos * SSM_GROUP, nc), lambda bi, j: (bi, 0, j, 0)),
                  pl.BlockSpec((1, 6, d), lambda bi, j: (bi, 0, 0)),
                  pl.BlockSpec((D_SSM, D_SSM), const),
                  pl.BlockSpec((D_SSM, 1), const),
                  pl.BlockSpec((D_SSM, 1), const),
                  pl.BlockSpec((d, d), const),
                  pl.BlockSpec((1, d), const),
                  pl.BlockSpec((d, N_EXPERTS), const),
                  pl.BlockSpec((N_EXPERTS, 1), const),
                  pl.BlockSpec((ts, ts), const)],
        out_specs=[pl.BlockSpec((1, npos, nc, d), pm),
                   pl.BlockSpec((1, npos, nc, d), pm),
                   pl.BlockSpec((TOP_K, POST_SUB * ts), tok),
                   pl.BlockSpec((TOP_K, POST_SUB * ts), tok),
                   pl.BlockSpec((TOP_K, POST_SUB * ts), tok),
                   pl.BlockSpec((POST_SUB, N_EXPERTS, 1), lambda bi, j: (bi * nt + j, 0, 0)),
                   pl.BlockSpec((N_EXPERTS, 1), const)],
        out_shape=[jax.ShapeDtypeStruct((b, SSM_CHUNK, nc, d), F32),
                   jax.ShapeDtypeStruct((b, SSM_CHUNK, nc, d), BF16),
                   jax.ShapeDtypeStruct((TOP_K, t), I32),
                   jax.ShapeDtypeStruct((TOP_K, t), F32),
                   jax.ShapeDtypeStruct((TOP_K, t), I32),
                   jax.ShapeDtypeStruct((b * nt * POST_SUB, N_EXPERTS, 1), I32),
                   jax.ShapeDtypeStruct((N_EXPERTS, 1), I32)],
        scratch_shapes=[pltpu.VMEM((N_EXPERTS, 1), F32), pltpu.VMEM((2, npos, nc, d), F32),
                        pltpu.VMEM((D_SSM, D_SSM), BF16), pltpu.VMEM((d, d), BF16), pltpu.VMEM((N_EXPERTS, d), BF16),
                        pltpu.SemaphoreType.DMA((2,))],
        compiler_params=_cparams(2, ROW_VMEM_BYTES),
        name="post",
    )(x.reshape(b, nc, SSM_CHUNK, d), attn, yt, mod, w_glu, col(b_glu), col(ssm_norm), w_out,
      norm_ffn.reshape(1, -1), w_router, col(b_router), tri)


def _route_kernel(eidx_ref, lrank_ref, r0_ref, cnt_ref, ls_ref, tab_ref, te_ref, nv_ref, nx_ref, pad_ref):
    cnt = cnt_ref[...]
    tiles = (cnt + (RUN - 1 + FFN_ROWS - 1)) // FFN_ROWS
    er = lax.broadcasted_iota(I32, (N_EXPERTS, N_EXPERTS), 0)
    ec = lax.broadcasted_iota(I32, (N_EXPERTS, N_EXPERTS), 1)
    ltri = (ec < er).astype(BF16)

    def excl_cumsum(v):
        vb = jnp.broadcast_to(v.astype(F32), (N_EXPERTS, LANES)).astype(BF16)
        return jnp.dot(ltri, vb, preferred_element_type=F32)[:, 0:1].astype(I32)

    start_t = excl_cumsum(tiles)
    end_t = start_t + tiles
    start = start_t * FFN_ROWS
    pad_ref[...] = start + cnt

    nb = r0_ref.shape[0]
    ts = eidx_ref.shape[1] // nb
    iota_e = lax.broadcasted_iota(I32, (N_EXPERTS, ts), 0)
    iota_t = lax.broadcasted_iota(I32, (N_EXPERTS, TABW), 0)
    lane = lax.broadcasted_iota(I32, (1, TABW), 1)

    def block(b, carry):
        lanes = pl.ds(pl.multiple_of(b * ts, ts), ts)
        sels = [iota_e == eidx_ref[k:k + 1, lanes] for k in range(TOP_K)]
        member = sels[0].astype(I32) + sels[1].astype(I32) + sels[2].astype(I32) + sels[3].astype(I32)
        units = lax.shift_right_logical(jnp.sum(member, axis=1, keepdims=True) + (RUN - 1), RUN_SHIFT)
        u0 = excl_cumsum(units)
        for k in range(TOP_K):
            first = jnp.sum(jnp.where(sels[k], u0, 0), axis=0, keepdims=True)
            ls_ref[k:k + 1, lanes] = first * RUN + lrank_ref[k:k + 1, lanes]
        n_big = lax.shift_right_logical(units, 1)
        n_small = units & 1
        slot0 = start + r0_ref[b]

        def chunk_rows(idx, counts):
            c0 = excl_cumsum(counts)
            sel = iota_t == jnp.sum((idx >= c0 + counts).astype(I32), axis=0, keepdims=True)
            pick = lambda v: jnp.sum(jnp.where(sel, v, 0), axis=0, keepdims=True)
            j = idx - pick(c0)
            return pick(slot0), pick(u0), j, pick(n_big), idx < jnp.max(c0 + counts, axis=0, keepdims=True)

        s_b, u_b, j_b, _, ok_b = chunk_rows(lane, n_big)
        s_s, u_s, _, nb_s, ok_s = chunk_rows(lane - SMALL0, n_small)
        small = lane >= SMALL0
        slot = jnp.where(small, s_s + nb_s * BIG, s_b + j_b * BIG)
        local = jnp.where(small, (u_s + 2 * nb_s) * RUN, (u_b + 2 * j_b) * RUN)
        ok = (small & ok_s) | (jnp.logical_not(small) & ok_b)
        counts = jnp.where(lane == TABW - 2, jnp.sum(n_big, axis=0, keepdims=True),
                           jnp.sum(n_small, axis=0, keepdims=True))
        tab_ref[b, 0:1, :] = jnp.where(lane >= TABW - 2, counts, jnp.where(ok, slot, -1))
        tab_ref[b, 1:2, :] = jnp.where(ok, local, 0)
        return carry

    lax.fori_loop(0, nb, block, 0, unroll=2)

    nv = jnp.max(end_t, axis=0, keepdims=True)
    width = te_ref.shape[1]
    ti = jnp.minimum(lax.broadcasted_iota(I32, (N_EXPERTS, width), 1), nv - 1)
    te = jnp.minimum(jnp.sum((ti >= end_t).astype(I32), axis=0, keepdims=True), N_EXPERTS - 1)
    te_ref[...] = te
    nv_ref[...] = jnp.broadcast_to(nv, nv_ref.shape)
    ie = lax.broadcasted_iota(I32, (N_EXPERTS, width), 0)
    own_end = jnp.sum(jnp.where(ie == te, end_t, 0), axis=0, keepdims=True)
    nxt = jnp.minimum(jnp.sum((own_end >= end_t).astype(I32), axis=0, keepdims=True), N_EXPERTS - 1)
    nx_ref[...] = jnp.where(own_end < nv, nxt, -1)


def _route(eidx, lrank, r0, cnt, n_tiles):
    t = eidx.shape[1]
    nb = r0.shape[0]
    width = -(-n_tiles // LANES) * LANES
    return pl.pallas_call(
        _route_kernel,
        out_shape=[jax.ShapeDtypeStruct((TOP_K, t), I32),
                   jax.ShapeDtypeStruct((nb, 2, TABW), I32),
                   jax.ShapeDtypeStruct((1, width), I32),
                   jax.ShapeDtypeStruct((1, LANES), I32),
                   jax.ShapeDtypeStruct((1, width), I32),
                   jax.ShapeDtypeStruct((N_EXPERTS, 1), I32)],
        name="route",
    )(eidx, lrank, r0, cnt)


def _for_chunk_pairs(n, fn):
    def body(i, carry):
        fn(2 * i, 0)

        @pl.when(2 * i + 1 < n)
        def _():
            fn(2 * i + 1, 1)
        return carry
    lax.fori_loop(0, lax.shift_right_logical(n + 1, 1), body, 0)


def _for_block_chunks(tab_ref, blk, fn):
    base = blk * (2 * TABW)
    for first, count_lane, n_rows in ((0, TABW - 2, BIG), (SMALL0, TABW - 1, RUN)):
        def visit(c, parity, first=first, n_rows=n_rows):
            fn(tab_ref[base + first + c], pl.multiple_of(tab_ref[base + TABW + first + c], RUN), n_rows, parity)
        _for_chunk_pairs(tab_ref[base + count_lane], visit)


def _local_rows(ts):
    return ts * TOP_K + N_EXPERTS * RUN


def _dispatch_kernel(tab_ref, pad_ref, nv_ref, h_ref, ls_ref, xs_ref, buf, zbuf, sems, zsem, *, n_alloc):
    b = pl.program_id(0)
    slot = b % 2
    ts = h_ref.shape[0]
    local = buf.shape[2]

    def chunk_copy(sl, slot_row, local_row, n):
        return pltpu.make_async_copy(buf.at[sl, :, pl.ds(local_row, n), :], xs_ref.at[:, pl.ds(slot_row, n), :],
                                     sems.at[sl])

    @pl.when(b == 0)
    def _():
        zbuf[...] = jnp.zeros_like(zbuf)
        zrows = zbuf.shape[1]
        zero = lambda row: pltpu.make_async_copy(zbuf, xs_ref.at[:, pl.ds(row, zrows), :], zsem)
        for phase in range(3):
            for e in range(phase, N_EXPERTS, 3):
                zero(pad_ref[e, 0]).start()
            for e in range(phase, N_EXPERTS, 3):
                zero(pad_ref[e, 0]).wait()
        ztile = lambda i: pltpu.make_async_copy(zbuf.at[:, pl.ds(0, FFN_ROWS), :],
                                                xs_ref.at[:, pl.ds((nv_ref[0, 0] + i) * FFN_ROWS, FFN_ROWS), :], zsem)

        def tail_start(i, carry):
            ztile(i).start()
            return carry

        def tail_wait(i, carry):
            ztile(i).wait()
            return carry
        lax.fori_loop(0, n_alloc - nv_ref[0, 0], tail_start, 0)
        lax.fori_loop(0, n_alloc - nv_ref[0, 0], tail_wait, 0)

    r = lax.broadcasted_iota(I32, (local, ts), 0)
    hit = (r == ls_ref[0:1, :]) | (r == ls_ref[1:2, :]) | (r == ls_ref[2:3, :]) | (r == ls_ref[3:4, :])
    hit = hit.astype(BF16)
    for pb in range(PANELS):
        srt = jnp.dot(hit, h_ref[:, pb * PANEL_COLS:(pb + 1) * PANEL_COLS], preferred_element_type=F32)
        buf[slot, pb] = _pack_panel(srt, exact=True)

    @pl.when(b > 0)
    def _():
        _for_block_chunks(tab_ref, b - 1, lambda s, l, n, p: chunk_copy(1 - slot, s, l, n).wait())

    _for_block_chunks(tab_ref, b, lambda s, l, n, p: chunk_copy(slot, s, l, n).start(priority=p))

    @pl.when(b == pl.num_programs(0) - 1)
    def _():
        _for_block_chunks(tab_ref, b, lambda s, l, n, p: chunk_copy(slot, s, l, n).wait())


def _dispatch(tab, pad, nv, h2, ls, n_alloc):
    t, d = h2.shape
    nb = tab.shape[0] // (2 * TABW)
    ts = t // nb
    return pl.pallas_call(
        functools.partial(_dispatch_kernel, n_alloc=n_alloc),
        grid_spec=pltpu.PrefetchScalarGridSpec(
            num_scalar_prefetch=3,
            grid=(nb,),
            in_specs=[pl.BlockSpec((ts, d), lambda i, *_: (i, 0)),
                      pl.BlockSpec((TOP_K, ts), lambda i, *_: (0, i))],
            out_specs=pl.BlockSpec(memory_space=pl.ANY),
            scratch_shapes=[pltpu.VMEM((2, PANELS, _local_rows(ts), LANES), U32),
                            pltpu.VMEM((PANELS, FFN_ROWS + RUN, LANES), U32),
                            pltpu.SemaphoreType.DMA((2,)), pltpu.SemaphoreType.DMA],
        ),
        out_shape=jax.ShapeDtypeStruct((PANELS, n_alloc * FFN_ROWS, LANES), U32),
        compiler_params=_cparams(1, ROW_VMEM_BYTES),
        name="dispatch",
    )(tab, pad, nv, h2, ls)


def _ffn_kernel(te_ref, nv_ref, nx_ref, xs_ref, wgu_hbm, bgu_ref, wd_hbm, bd_ref, perm_ref, ys_ref,
                wgu_stage, wd_stage, wg_scr, wu_scr, wd_scr, bg_scr, bu_scr, sems):
    p = pl.program_id(0)
    t0 = 2 * p
    t1 = t0 + 1
    e0 = te_ref[0, t0]
    e1 = te_ref[0, t1]
    v0 = t0 < nv_ref[0, 0]
    v1 = t1 < nv_ref[0, 0]
    new0 = (p == 0) | (e0 != te_ref[0, jnp.maximum(t0 - 1, 0)])
    same = v1 & (e1 == e0)

    def stage_copies(e):
        return (pltpu.make_async_copy(wgu_hbm.at[e], wgu_stage, sems.at[0]),
                pltpu.make_async_copy(wd_hbm.at[e], wd_stage, sems.at[1]))

    def load_expert(t, first):
        e = te_ref[0, t]
        if first:
            @pl.when(p == 0)
            def _():
                for cp in stage_copies(e):
                    cp.start()

        for cp in stage_copies(e):
            cp.wait()
        bias = bgu_ref[e]
        for c in range(2 * D_FF // PERM):
            cols = slice(c * PERM, (c + 1) * PERM)
            half = slice(c * (PERM // 2), (c + 1) * (PERM // 2))
            w = wgu_stage[:, cols].astype(BF16)
            pw = jnp.dot(w, perm_ref[...], preferred_element_type=F32).astype(BF16)
            wg_scr[:, half] = pw[:, :PERM // 2]
            wu_scr[:, half] = pw[:, PERM // 2:]
            b1 = bias[:, cols].astype(BF16)
            r1 = bias[:, cols] - b1.astype(F32)
            b2 = r1.astype(BF16)
            b3 = (r1 - b2.astype(F32)).astype(BF16)
            terms = jnp.concatenate([b1, b2, b3, jnp.zeros((5, PERM), BF16)], axis=0)
            pb = jnp.sum(jnp.dot(terms, perm_ref[...], preferred_element_type=F32), axis=0, keepdims=True)
            bg_scr[:, half] = pb[:, :PERM // 2]
            bu_scr[:, half] = pb[:, PERM // 2:]
        wd_scr[...] = wd_stage[...].astype(BF16)

        @pl.when(nx_ref[0, t] >= 0)
        def _():
            for cp in stage_copies(nx_ref[0, t]):
                cp.start()

    def run(lo, n, e):
        x = _unpack_panels([xs_ref[pb, lo:lo + n, :] for pb in range(PANELS)])
        gate = jnp.dot(x, wg_scr[...], preferred_element_type=F32) + bg_scr[...]
        up = jnp.dot(x, wu_scr[...], preferred_element_type=F32) + bu_scr[...]
        gate = jnp.minimum(gate, SWIGLU_LIMIT)
        up = jnp.clip(up, -SWIGLU_LIMIT, SWIGLU_LIMIT)
        act = ((up + 1.0) * (gate * jax.nn.sigmoid(SWIGLU_ALPHA * gate))).astype(BF16)
        bd = bd_ref[e]
        for pb in range(PANELS):
            cols = slice(pb * PANEL_COLS, (pb + 1) * PANEL_COLS)
            y = jnp.dot(act, wd_scr[:, cols], preferred_element_type=F32) + bd[:, cols]
            ys_ref[pb, lo:lo + n, :] = _pack_panel(y)

    @pl.when(v0 & new0)
    def _():
        load_expert(t0, True)

    @pl.when(same)
    def _():
        run(0, 2 * FFN_ROWS, e0)

    @pl.when(v0 & jnp.logical_not(same))
    def _():
        run(0, FFN_ROWS, e0)

    @pl.when(v1 & jnp.logical_not(same))
    def _():
        load_expert(t1, False)
        run(FFN_ROWS, FFN_ROWS, e1)

    @pl.when(v0 & jnp.logical_not(v1))
    def _():
        ys_ref[:, FFN_ROWS:, :] = xs_ref[:, FFN_ROWS:, :]


def _ffn(te, nv, nx, xs, w_gate_up, bgu, w_down, bd, n_tiles):
    d = D_MODEL
    pair = lambda i, te, nv, nx: (0, jnp.minimum(i, lax.shift_right_logical(nv[0, 0] - 1, 1)), 0)
    whole = lambda i, te, nv, nx: (0, 0, 0)
    r = lax.broadcasted_iota(I32, (PERM, PERM), 0)
    c = lax.broadcasted_iota(I32, (PERM, PERM), 1)
    perm = (r == jnp.where(c < PERM // 2, 2 * c, 2 * (c - PERM // 2) + 1)).astype(BF16)
    return pl.pallas_call(
        _ffn_kernel,
        grid_spec=pltpu.PrefetchScalarGridSpec(
            num_scalar_prefetch=3,
            grid=(n_tiles // 2,),
            in_specs=[pl.BlockSpec((PANELS, 2 * FFN_ROWS, LANES), pair),
                      pl.BlockSpec(memory_space=pl.ANY),
                      pl.BlockSpec((N_EXPERTS, 1, 2 * D_FF), whole),
                      pl.BlockSpec(memory_space=pl.ANY),
                      pl.BlockSpec((N_EXPERTS, 1, d), whole),
                      pl.BlockSpec((PERM, PERM), lambda i, te, nv, nx: (0, 0))],
            out_specs=pl.BlockSpec((PANELS, 2 * FFN_ROWS, LANES), pair),
            scratch_shapes=[pltpu.VMEM((d, 2 * D_FF), F32), pltpu.VMEM((D_FF, d), F32),
                            pltpu.VMEM((d, D_FF), BF16), pltpu.VMEM((d, D_FF), BF16), pltpu.VMEM((D_FF, d), BF16),
                            pltpu.VMEM((1, D_FF), F32), pltpu.VMEM((1, D_FF), F32),
                            pltpu.SemaphoreType.DMA((2,))],
        ),
        out_shape=jax.ShapeDtypeStruct(xs.shape, U32),
        input_output_aliases={3: 0},
        compiler_params=_cparams(1, FFN_VMEM_BYTES),
        name="ffn",
    )(te, nv, nx, xs, w_gate_up, bgu, w_down, bd, perm)


def _combine_kernel(tab_ref, x1_ref, ls_ref, w_ref, mod_ref, ys_ref, o4_hbm, ybuf, ob_buf, sems, osems, *, n_steps):
    jj = pl.program_id(1)
    g = pl.program_id(0) * pl.num_programs(1) + jj
    slot = g % 2
    nc = x1_ref.shape[2]
    tt = POST_POS * nc
    d = x1_ref.shape[3]
    local = ybuf.shape[3]
    npos = COMB_SUB * POST_POS

    def for_step_chunks(step, sl, fn):
        for sub in range(COMB_SUB):
            def visit(s, l, n, p, sub=sub):
                fn(pltpu.make_async_copy(ys_ref.at[:, pl.ds(s, n), :], ybuf.at[sl, sub, :, pl.ds(l, n), :],
                                         sems.at[sl]), p)
            _for_block_chunks(tab_ref, step * COMB_SUB + sub, visit)

    @pl.when(g == 0)
    def _():
        ybuf[...] = jnp.zeros_like(ybuf)
        for_step_chunks(0, 0, lambda cp, p: cp.start(priority=p))

    @pl.when(g + 1 < n_steps)
    def _():
        for_step_chunks(g + 1, 1 - slot, lambda cp, p: cp.start(priority=p))

    for_step_chunks(g, slot, lambda cp, p: cp.wait())

    def out_copies(sl, b_, j_):
        return [pltpu.make_async_copy(ob_buf.at[sl, il], o4_hbm.at[b_, :, npos * j_ + il, :], osems.at[sl])
                for il in range(npos)]

    @pl.when(g >= 2)
    def _():
        for cp in out_copies(slot, 0, 0):
            cp.wait()

    r = lax.broadcasted_iota(I32, (tt, local), 1).astype(F32)
    to_cols = lambda a: jnp.concatenate([a, jnp.zeros_like(a)], axis=0).T
    for sub in range(COMB_SUB):
        lanes = slice(sub * tt, (sub + 1) * tt)
        ls_c = to_cols(ls_ref[:, lanes].astype(F32))
        w_c = to_cols(w_ref[:, lanes])
        wm = jnp.zeros((tt, local), F32)
        for k in range(TOP_K):
            wm = jnp.where(r == ls_c[:, k:k + 1], w_c[:, k:k + 1], wm)
        y = _unpack_panels([ybuf[slot, sub, pb] for pb in range(PANELS)])
        acc = jnp.dot(wm.astype(BF16), y, preferred_element_type=F32)
        out = x1_ref[0, sub * POST_POS:(sub + 1) * POST_POS].reshape(tt, d) + mod_ref[0, 5:6, :] * acc
        for il in range(POST_POS):
            ob_buf[slot, sub * POST_POS + il] = out[il * nc:(il + 1) * nc]

    for cp in out_copies(slot, pl.program_id(0), jj):
        cp.start()

    @pl.when(g == n_steps - 1)
    def _():
        for cp in out_copies(slot, 0, 0):
            cp.wait()
        if n_steps > 1:
            for cp in out_copies(1 - slot, 0, 0):
                cp.wait()


def _combine(tab, x1, ls, wts, mod, ys):
    b, _, nc, d = x1.shape
    s = SSM_CHUNK * nc
    tt = POST_POS * nc
    npos = COMB_SUB * POST_POS
    nt = SSM_CHUNK // npos
    o4 = pl.pallas_call(
        functools.partial(_combine_kernel, n_steps=b * nt),
        grid_spec=pltpu.PrefetchScalarGridSpec(
            num_scalar_prefetch=1,
            grid=(b, nt),
            in_specs=[pl.BlockSpec((1, npos, nc, d), lambda bi, j, *_: (bi, j, 0, 0)),
                      pl.BlockSpec((TOP_K, COMB_SUB * tt), lambda bi, j, *_: (0, bi * nt + j)),
                      pl.BlockSpec((TOP_K, COMB_SUB * tt), lambda bi, j, *_: (0, bi * nt + j)),
                      pl.BlockSpec((1, 6, d), lambda bi, j, *_: (bi, 0, 0)),
                      pl.BlockSpec(memory_space=pl.ANY)],
            out_specs=pl.BlockSpec(memory_space=pl.ANY),
            scratch_shapes=[pltpu.VMEM((2, COMB_SUB, PANELS, _local_rows(tt), LANES), U32),
                            pltpu.VMEM((2, npos, nc, d), F32),
                            pltpu.SemaphoreType.DMA((2,)), pltpu.SemaphoreType.DMA((2,))],
        ),
        out_shape=jax.ShapeDtypeStruct((b, nc, SSM_CHUNK, d), F32),
        compiler_params=_cparams(2),
        name="combine",
    )(tab, x1, ls, wts, mod, ys)
    return o4.reshape(b, s, d)


def kernel(x, c, w_ada, b_ada, norm_mix, w_in, b_in, q_norm, k_norm, sinks, lam_re, lam_im, log_dt, b_re, b_im,
           c_re, c_im, d_skip, w_glu, b_glu, attn_out_norm, ssm_out_norm, w_out, norm_ffn, w_router, b_router,
           w_gate_up, b_gate_up, w_down, b_down):
    b, s, d = x.shape
    t = b * s
    depth = w_ada.shape[0]
    n_tiles = -(-(t * TOP_K + N_EXPERTS * (RUN - 1 + FFN_ROWS - 1)) // FFN_ROWS)
    n_tiles += n_tiles % 2
    n_alloc = n_tiles + 2
    for l in range(depth):
        mod = _adaln(c, w_ada[l], b_ada[l]).reshape(b, 6, d)
        q, k, v, ut = _inproj(x, mod, norm_mix[l], w_in[l], b_in[l])
        attn = _attention(q, k, v, sinks[l], q_norm[l], k_norm[l], attn_out_norm[l])
        tt, wz, wyt, cs = _ssm_params(lam_re[l], lam_im[l], log_dt[l], b_re[l], b_im[l], c_re[l], c_im[l])
        yt = _ssm(ut, tt, wz, wyt, cs, d_skip[l])
        x1, h2, eidx, wts, lrank, r0, cnt = _post(x, attn, yt, mod, w_glu[l], b_glu[l], ssm_out_norm[l], w_out[l],
                                                  norm_ffn[l], w_router[l], b_router[l])
        ls, tab, te, nv, nx, pad = _route(eidx, lrank, r0, cnt, n_tiles)
        tab = tab.reshape(-1)
        xs = _dispatch(tab, pad, nv, h2.reshape(t, d), ls, n_alloc)
        ys = _ffn(te, nv, nx, xs, w_gate_up[l], b_gate_up[l][:, None, :],
                  w_down[l], b_down[l][:, None, :], n_tiles)
        x = _combine(tab, x1, ls, wts, mod, ys)
    return x
```

```python
import functools
import math

import jax
import jax.numpy as jnp
from jax import lax
from jax.experimental import pallas as pl
from jax.experimental.pallas import tpu as pltpu

F32 = jnp.float32
BF16 = jnp.bfloat16
U32 = jnp.uint32
I32 = jnp.int32

D_MODEL = 1024
HEAD_DIM = 64
N_HEADS = 8
N_KV_HEADS = 2
Q_PER_KV = N_HEADS // N_KV_HEADS
D_ATTN = N_HEADS * HEAD_DIM
D_KV = N_KV_HEADS * HEAD_DIM
D_QKV = D_ATTN + 2 * D_KV
WINDOW = 128
BLOCK = 128
D_SSM = D_MODEL - D_ATTN
SSM_GROUP = 16
N_GROUPS = D_SSM // SSM_GROUP
STATE = 64
N_EXPERTS = 32
TOP_K = 4
D_FF = D_MODEL
SWIGLU_LIMIT = 7.0
SWIGLU_ALPHA = 1.702
EPS = 1e-6
NEG_INF = -1e30

LANES = 128
SSM_CHUNK = 16
SSM_ROW = SSM_CHUNK * SSM_GROUP
N_POW = 2 * SSM_CHUNK
PANEL_COLS = 2 * LANES
PANELS = D_MODEL // PANEL_COLS

SSM_GROUPS_PER_STEP = 8
POS_PER_STEP = 8
ATTN_ROWS = 512
POST_POS = 2
POST_SUB = 4
COMB_SUB = 2
FFN_ROWS = 256
RUN = 8
RUN_SHIFT = 3
BIG = 2 * RUN
TABW = 128
SMALL0 = 80
FFN_VMEM_BYTES = 40 * 1024 * 1024
ROW_VMEM_BYTES = 48 * 1024 * 1024

HIGHEST = lax.Precision.HIGHEST
_ARB = "arbitrary"


def _cparams(n, vmem=None):
    return pltpu.CompilerParams(dimension_semantics=(_ARB,) * n, vmem_limit_bytes=vmem)


def _rms(x, axis=-1):
    return x * lax.rsqrt(jnp.mean(x * x, axis=axis, keepdims=True) + EPS)


def _pack_panel(y, exact=False):
    hi, lo = y[:, :LANES], y[:, LANES:]
    if not exact:
        hi = hi.astype(BF16).astype(F32)
        lo = lo.astype(BF16).astype(F32)
    return lax.bitcast_convert_type(hi, U32) | (lax.bitcast_convert_type(lo, U32) >> 16)


def _unpack_panels(words):
    cols = []
    for w in words:
        cols.append(lax.bitcast_convert_type(w & jnp.uint32(0xFFFF0000), F32).astype(BF16))
        cols.append(lax.bitcast_convert_type(w << 16, F32).astype(BF16))
    return jnp.concatenate(cols, axis=-1)


def _prefetch_pos_rows(x4_hbm, buf, sems, n_pos):
    bi = pl.program_id(0)
    j = pl.program_id(1)
    nj = pl.num_programs(1)
    g = bi * nj + j
    slot = g % 2

    def copies(sl, b_, j_):
        return [pltpu.make_async_copy(x4_hbm.at[b_, :, n_pos * j_ + il, :], buf.at[sl, il], sems.at[sl])
                for il in range(n_pos)]

    @pl.when(g == 0)
    def _():
        for cp in copies(0, 0, 0):
            cp.start()

    @pl.when(g + 1 < pl.num_programs(0) * nj)
    def _():
        wrap = j + 1 == nj
        for cp in copies(1 - slot, jnp.where(wrap, bi + 1, bi), jnp.where(wrap, 0, j + 1)):
            cp.start()

    for cp in copies(slot, bi, j):
        cp.wait()
    return slot


def _to_lane_blocks(dst, src):
    for kb in range(dst.shape[0]):
        dst[kb] = src[:, kb * LANES:(kb + 1) * LANES]


def _adaln_kernel(c_ref, w_ref, b_ref, o_ref):
    c = c_ref[...]
    ca = c * jax.nn.sigmoid(c)
    o_ref[...] = jnp.dot(ca, w_ref[...], preferred_element_type=F32, precision=HIGHEST) + b_ref[...]


def _adaln(c, w_ada, b_ada):
    b, d = c.shape
    n = w_ada.shape[1] // d
    return pl.pallas_call(
        _adaln_kernel,
        grid=(n,),
        in_specs=[pl.BlockSpec((b, d), lambda j: (0, 0)),
                  pl.BlockSpec((d, d), lambda j: (0, j)),
                  pl.BlockSpec((1, d), lambda j: (0, j))],
        out_specs=pl.BlockSpec((b, d), lambda j: (0, j)),
        out_shape=jax.ShapeDtypeStruct((b, n * d), F32),
        compiler_params=_cparams(1),
        name="adaln",
    )(c, w_ada, b_ada.reshape(1, -1))


def _inproj_kernel(x4_hbm, x_ref, mod_ref, g_ref, w_ref, bqkv_ref, but_ref, q_ref, k_ref, v_ref, ut_ref,
                   xp_buf, wqkv_scr, wut_scr, sems):
    nc = ut_ref.shape[3]

    @pl.when((pl.program_id(0) == 0) & (pl.program_id(1) == 0))
    def _():
        wqkv_scr[...] = w_ref[:, :D_QKV].astype(BF16)
        wut_scr[...] = w_ref[:, D_QKV:].T.astype(BF16)

    slot = _prefetch_pos_rows(x4_hbm, xp_buf, sems, POS_PER_STEP)
    gain = g_ref[...]
    scale = 1.0 + mod_ref[0, 1:2, :]
    shift = mod_ref[0, 0:1, :]

    def norm_mod(x):
        return (_rms(x) * gain * scale + shift).astype(BF16)

    proj = jnp.dot(norm_mod(x_ref[0]), wqkv_scr[...], preferred_element_type=F32) + bqkv_ref[...]
    q_ref[0] = proj[:, :D_ATTN].astype(BF16)
    k_ref[0] = proj[:, D_ATTN:D_ATTN + D_KV].astype(BF16)
    v_ref[0] = proj[:, D_ATTN + D_KV:].astype(BF16)

    hs = jnp.concatenate([norm_mod(xp_buf[slot, il]) for il in range(POS_PER_STEP)], axis=0)
    ut = lax.dot_general(wut_scr[...], hs, (((1,), (1,)), ((), ())), preferred_element_type=F32) + but_ref[...]
    for il in range(POS_PER_STEP):
        piece = ut[:, il * nc:(il + 1) * nc].astype(BF16)
        ut_ref[0, :, il * SSM_GROUP:(il + 1) * SSM_GROUP, :] = piece.reshape(N_GROUPS, SSM_GROUP, nc)


def _inproj(x, mod, gain, w_in, b_in):
    b, s, d = x.shape
    nc = s // SSM_CHUNK
    rows = POS_PER_STEP * nc
    row = lambda bi, j: (bi, j, 0)
    const = lambda bi, j: (0, 0)
    return pl.pallas_call(
        _inproj_kernel,
        grid=(b, SSM_CHUNK // POS_PER_STEP),
        in_specs=[pl.BlockSpec(memory_space=pl.ANY),
                  pl.BlockSpec((1, rows, d), row),
                  pl.BlockSpec((1, 6, d), lambda bi, j: (bi, 0, 0)),
                  pl.BlockSpec((1, d), const),
                  pl.BlockSpec((d, D_QKV + D_SSM), const),
                  pl.BlockSpec((1, D_QKV), const),
                  pl.BlockSpec((D_SSM, 1), const)],
        out_specs=[pl.BlockSpec((1, rows, D_ATTN), row),
                   pl.BlockSpec((1, rows, D_KV), row),
                   pl.BlockSpec((1, rows, D_KV), row),
                   pl.BlockSpec((1, N_GROUPS, POS_PER_STEP * SSM_GROUP, nc), lambda bi, j: (bi, 0, j, 0))],
        out_shape=[jax.ShapeDtypeStruct((b, s, D_ATTN), BF16),
                   jax.ShapeDtypeStruct((b, s, D_KV), BF16),
                   jax.ShapeDtypeStruct((b, s, D_KV), BF16),
                   jax.ShapeDtypeStruct((b, N_GROUPS, SSM_ROW, nc), BF16)],
        scratch_shapes=[pltpu.VMEM((2, POS_PER_STEP, nc, d), F32), pltpu.VMEM((d, D_QKV), BF16),
                        pltpu.VMEM((D_SSM, d), BF16), pltpu.SemaphoreType.DMA((2,))],
        compiler_params=_cparams(2, ROW_VMEM_BYTES),
        name="inproj",
    )(x.reshape(b, nc, SSM_CHUNK, d), x, mod, gain.reshape(1, d), w_in, b_in[:D_QKV].reshape(1, D_QKV),
      b_in[D_QKV:].reshape(D_SSM, 1))


def _half_norm(x, low):
    sq = x * x
    s_lo = jnp.sum(jnp.where(low, sq, 0.0), axis=-1, keepdims=True)
    s_hi = jnp.sum(sq, axis=-1, keepdims=True) - s_lo
    inv = 1.0 / HEAD_DIM
    scale = jnp.where(low, lax.rsqrt(s_lo * inv + EPS), lax.rsqrt(s_hi * inv + EPS))
    return x * scale


def _attn_block(first, q, k_prev, k_cur, v_prev, v_cur, sinks_ref, qn, low, upper, rblk):
    no_prev = jnp.where(first, NEG_INF, 0.0)
    out_blocks = []
    for hk in range(N_KV_HEADS):
        qs = []
        for j in range(Q_PER_KV // 2):
            blk = hk * (Q_PER_KV // 2) + j
            qb = _half_norm(q[:, blk * LANES:(blk + 1) * LANES], low) * qn * (1.0 / math.sqrt(HEAD_DIM))
            qs.append(jnp.where(low, qb, 0.0))
            qs.append(jnp.where(low, 0.0, qb))
        qg = jnp.concatenate(qs, axis=0).astype(BF16)
        nt = (((1,), (1,)), ((), ()))
        s_prev = lax.dot_general(qg, k_prev[hk], nt, preferred_element_type=F32)
        s_cur = lax.dot_general(qg, k_cur[hk], nt, preferred_element_type=F32)
        s = jnp.where(upper, s_prev + no_prev, s_cur)
        sink = jnp.zeros((Q_PER_KV * BLOCK, 1), F32)
        for g in range(Q_PER_KV):
            sink = jnp.where(rblk == g, sinks_ref[hk * Q_PER_KV + g], sink)
        m = jnp.maximum(jnp.max(s, axis=-1, keepdims=True), sink)
        p = jnp.exp(s - m)
        den = jnp.sum(p, axis=-1, keepdims=True) + jnp.exp(sink - m)
        o = (jnp.dot(jnp.where(upper, p, 0.0).astype(BF16), v_prev[hk], preferred_element_type=F32)
             + jnp.dot(jnp.where(upper, 0.0, p).astype(BF16), v_cur[hk], preferred_element_type=F32)) / den
        for j in range(Q_PER_KV // 2):
            ev = o[(2 * j) * BLOCK:(2 * j + 1) * BLOCK]
            od = o[(2 * j + 1) * BLOCK:(2 * j + 2) * BLOCK]
            out_blocks.append(jnp.where(low, ev, od))
    return jnp.concatenate(out_blocks, axis=-1)


def _attn_kernel(sinks_ref, q_ref, k_ref, v_ref, qn_ref, kn_ref, on_ref, o_hbm, a_buf, sems, *, n_steps):
    step = pl.program_id(1)
    g = pl.program_id(0) * pl.num_programs(1) + step
    slot = g % 2
    cps = ATTN_ROWS // SSM_CHUNK
    nq = ATTN_ROWS // BLOCK

    def out_copies(sl, b_, s_):
        return [pltpu.make_async_copy(a_buf.at[sl, :, i, :], o_hbm.at[b_, i, pl.ds(s_ * cps, cps), :], sems.at[sl])
                for i in range(SSM_CHUNK)]

    @pl.when(g >= 2)
    def _():
        for cp in out_copies(slot, 0, 0):
            cp.wait()

    low = lax.broadcasted_iota(I32, (1, LANES), 1) < HEAD_DIM
    rows = Q_PER_KV * BLOCK
    upper = lax.broadcasted_iota(I32, (rows, BLOCK), 1) > lax.broadcasted_iota(I32, (rows, BLOCK), 0) % BLOCK
    rblk = lax.broadcasted_iota(I32, (rows, 1), 0) // BLOCK

    cur = pl.multiple_of(step * ATTN_ROWS, ATTN_ROWS)
    prev = pl.multiple_of(jnp.maximum(step * nq - 1, 0) * BLOCK, BLOCK)
    kall = jnp.concatenate([k_ref[0, pl.ds(prev, BLOCK), :], k_ref[0, pl.ds(cur, ATTN_ROWS), :]], axis=0).astype(F32)
    vall = jnp.concatenate([v_ref[0, pl.ds(prev, BLOCK), :], v_ref[0, pl.ds(cur, ATTN_ROWS), :]], axis=0).astype(F32)
    kall = _half_norm(kall, low) * kn_ref[...]
    kswap = pltpu.roll(kall, HEAD_DIM, axis=1)
    vswap = pltpu.roll(vall, HEAD_DIM, axis=1)
    k_dup = [jnp.where(low, kall, kswap).astype(BF16), jnp.where(low, kswap, kall).astype(BF16)]
    v_dup = [jnp.where(low, vall, vswap).astype(BF16), jnp.where(low, vswap, vall).astype(BF16)]
    blk = lambda a, i: [a[hk][i * BLOCK:(i + 1) * BLOCK] for hk in range(N_KV_HEADS)]

    for qb in range(nq):
        q = q_ref[0, qb * BLOCK:(qb + 1) * BLOCK, :].astype(F32)
        attn = _attn_block((step == 0) if qb == 0 else False, q, blk(k_dup, qb), blk(k_dup, qb + 1),
                           blk(v_dup, qb), blk(v_dup, qb + 1), sinks_ref, qn_ref[...], low, upper, rblk)
        attn = _rms(attn) * on_ref[...]
        cpb = BLOCK // SSM_CHUNK
        a_buf[slot, qb * cpb:(qb + 1) * cpb] = attn.reshape(cpb, SSM_CHUNK, D_ATTN)

    for cp in out_copies(slot, pl.program_id(0), step):
        cp.start()

    @pl.when(g == n_steps - 1)
    def _():
        for cp in out_copies(slot, 0, 0):
            cp.wait()
        if n_steps > 1:
            for cp in out_copies(1 - slot, 0, 0):
                cp.wait()


def _attention(q, k, v, sinks, q_norm, k_norm, out_norm):
    b, s, _ = q.shape
    tile2 = lambda g: jnp.tile(g.reshape(1, HEAD_DIM), (1, 2))
    cps = ATTN_ROWS // SSM_CHUNK
    return pl.pallas_call(
        functools.partial(_attn_kernel, n_steps=b * (s // ATTN_ROWS)),
        grid=(b, s // ATTN_ROWS),
        in_specs=[pl.BlockSpec(memory_space=pltpu.SMEM),
                  pl.BlockSpec((1, ATTN_ROWS, D_ATTN), lambda bi, n: (bi, n, 0)),
                  pl.BlockSpec((1, s, D_KV), lambda bi, n: (bi, 0, 0)),
                  pl.BlockSpec((1, s, D_KV), lambda bi, n: (bi, 0, 0)),
                  pl.BlockSpec((1, LANES), lambda bi, n: (0, 0)),
                  pl.BlockSpec((1, LANES), lambda bi, n: (0, 0)),
                  pl.BlockSpec((1, D_ATTN), lambda bi, n: (0, 0))],
        out_specs=pl.BlockSpec(memory_space=pl.ANY),
        out_shape=jax.ShapeDtypeStruct((b, SSM_CHUNK, s // SSM_CHUNK, D_ATTN), F32),
        scratch_shapes=[pltpu.VMEM((2, cps, SSM_CHUNK, D_ATTN), F32), pltpu.SemaphoreType.DMA((2,))],
        compiler_params=_cparams(2),
        name="attention",
    )(sinks, q, k, v, tile2(q_norm), tile2(k_norm), out_norm.reshape(1, D_ATTN))


def _cmul(ar, ai, br, bi):
    return ar * br - ai * bi, ar * bi + ai * br


def _ssm_param_kernel(*refs):
    for gi in range(refs[0].shape[0]):
        _ssm_param_group(gi, *refs)


def _ssm_param_group(gi, lam_ref, bre_ref, bim_ref, cre_ref, cim_ref, tt_ref, wz_ref, wyt_ref, cs_ref):
    f32dot = functools.partial(jnp.dot, preferred_element_type=F32, precision=HIGHEST)
    lr = lam_ref[gi, 0:1, :]
    li = lam_ref[gi, 1:2, :]
    dt = jnp.exp(lam_ref[gi, 2:3, :])
    rho = lr * dt
    th = li * dt
    imag_lane = lax.broadcasted_iota(I32, (1, LANES), 1) >= STATE

    kk = (lax.broadcasted_iota(I32, (N_POW, 1), 0) - (SSM_CHUNK - 1)).astype(F32)
    mag = jnp.exp(rho * kk)
    pw_r = mag * jnp.cos(th * kk)
    pw_i = mag * jnp.sin(th * kk)
    lb_r = pw_r[SSM_CHUNK:SSM_CHUNK + 1]
    lb_i = pw_i[SSM_CHUNK:SSM_CHUNK + 1]
    den = lr * lr + li * li
    coef_r = ((lb_r - 1.0) * lr + lb_i * li) / den
    coef_i = (lb_i * lr - (lb_r - 1.0) * li) / den

    eye = (lax.broadcasted_iota(I32, (SSM_GROUP, SSM_GROUP), 0)
           == lax.broadcasted_iota(I32, (SSM_GROUP, SSM_GROUP), 1)).astype(F32)
    lane_fold = (lax.broadcasted_iota(I32, (STATE, LANES), 1) % STATE
                 == lax.broadcasted_iota(I32, (STATE, LANES), 0)).astype(F32)

    def tile_pos(x):
        return jnp.concatenate([x] * SSM_CHUNK, axis=0)

    def power_rows(k_of_pos):
        idx = [k_of_pos(p) + (SSM_CHUNK - 1) for p in range(SSM_CHUNK)]
        rep = lambda t: jnp.concatenate([jnp.broadcast_to(t[r:r + 1], (SSM_GROUP, LANES)) for r in idx], axis=0)
        return rep(pw_r), rep(pw_i)

    def b_rows(b_ref):
        b2 = jnp.concatenate([b_ref[gi], b_ref[gi]], axis=0)
        return tile_pos(lax.dot_general(eye, b2, (((1,), (1,)), ((), ())), preferred_element_type=F32,
                                        precision=HIGHEST))

    def c_rows(c_ref):
        return tile_pos(f32dot(c_ref[gi], lane_fold))

    bbar_r, bbar_i = _cmul(coef_r, coef_i, b_rows(bre_ref), b_rows(bim_ref))
    c_r = c_rows(cre_ref)
    c_i = c_rows(cim_ref)

    a_r, a_i = _cmul(bbar_r, bbar_i, *power_rows(lambda p: -p))
    a2c = jnp.where(imag_lane, -a_i, a_r)
    m_r, m_i = _cmul(c_r, c_i, *power_rows(lambda p: p))
    bmc = jnp.where(imag_lane, m_i, m_r)
    tt = f32dot(bmc, a2c.T)
    causal = (lax.broadcasted_iota(I32, (SSM_ROW, 1), 0) // SSM_GROUP
              >= lax.broadcasted_iota(I32, (1, SSM_ROW), 1) // SSM_GROUP)
    tt_ref[gi] = jnp.where(causal, tt, 0.0).astype(BF16)

    w_r, w_i = _cmul(bbar_r, bbar_i, *power_rows(lambda p: SSM_CHUNK - 1 - p))
    wz_ref[gi, :, :LANES] = jnp.where(imag_lane, w_i, w_r).astype(BF16)
    wz_ref[gi, :, LANES:] = jnp.where(imag_lane, w_r, w_i).astype(BF16)

    y_r, y_i = _cmul(c_r, c_i, *power_rows(lambda p: p + 1))
    wyt_ref[gi] = jnp.where(imag_lane, -y_i, y_r).astype(BF16)

    cs_ref[gi, 0:1, :] = pw_r[N_POW - 1:N_POW]
    cs_ref[gi, 1:2, :] = jnp.where(imag_lane, pw_i[N_POW - 1:N_POW], -pw_i[N_POW - 1:N_POW])


def _ssm_params(lam_re, lam_im, log_dt, b_re, b_im, c_re, c_im):
    g = lam_re.shape[0]
    lam = jnp.stack([lam_re, lam_im, jnp.broadcast_to(log_dt[:, None], (g, STATE))], axis=1)
    lam = jnp.concatenate([lam, lam], axis=2)
    ng = SSM_GROUPS_PER_STEP
    blk = lambda *shape: pl.BlockSpec((ng,) + shape, lambda i: (i, 0, 0))
    return pl.pallas_call(
        _ssm_param_kernel,
        grid=(g // ng,),
        in_specs=[blk(3, LANES), blk(STATE, SSM_GROUP), blk(STATE, SSM_GROUP), blk(SSM_GROUP, STATE),
                  blk(SSM_GROUP, STATE)],
        out_specs=[blk(SSM_ROW, SSM_ROW), blk(SSM_ROW, SSM_ROW), blk(SSM_ROW, LANES), blk(2, LANES)],
        out_shape=[jax.ShapeDtypeStruct((g, SSM_ROW, SSM_ROW), BF16),
                   jax.ShapeDtypeStruct((g, SSM_ROW, SSM_ROW), BF16),
                   jax.ShapeDtypeStruct((g, SSM_ROW, LANES), BF16),
                   jax.ShapeDtypeStruct((g, 2, LANES), F32)],
        compiler_params=_cparams(1),
        name="ssm_params",
    )(lam, b_re, b_im, c_re, c_im)


def _ssm_kernel(ut_ref, tt_ref, wz_ref, wyt_ref, cs_ref, d_ref, yt_ref, z_scr, s_scr):
    batch, ng, _, nc = ut_ref.shape
    uts = [jnp.concatenate([ut_ref[b, gi] for b in range(batch)], axis=1) for gi in range(ng)]
    for gi in range(ng):
        z = lax.dot_general(uts[gi], wz_ref[gi], (((0,), (0,)), ((), ())), preferred_element_type=F32)
        _to_lane_blocks(z_scr.at[gi], z)
    c1 = [cs_ref[gi, 0:1, :] for gi in range(ng)]
    c2 = [cs_ref[gi, 1:2, :] for gi in range(ng)]

    def step(c, carry):
        rows = pl.ds(c, batch, stride=nc)
        out = []
        for gi in range(ng):
            s1, s2 = carry[gi]
            s_scr[gi, rows, :] = s1
            out.append((c1[gi] * s1 + c2[gi] * s2 + z_scr[gi, 0, rows, :],
                        c1[gi] * s2 - c2[gi] * s1 + z_scr[gi, 1, rows, :]))
        return tuple(out)

    zero = jnp.zeros((batch, LANES), F32)
    lax.fori_loop(0, nc, step, ((zero, zero),) * ng, unroll=8)
    for gi in range(ng):
        y = jnp.dot(tt_ref[gi], uts[gi], preferred_element_type=F32)
        y = y + lax.dot_general(wyt_ref[gi], s_scr[gi].astype(BF16), (((1,), (1,)), ((), ())),
                                preferred_element_type=F32)
        y = y + d_ref[gi] * uts[gi].astype(F32)
        for b in range(batch):
            yt_ref[b, gi] = y[:, b * nc:(b + 1) * nc]


def _ssm(ut, tt, wz, wyt, cs, d_skip):
    b, g, _, nc = ut.shape
    ng = SSM_GROUPS_PER_STEP
    d_col = jnp.tile(d_skip.reshape(g, 1, SSM_GROUP), (1, SSM_CHUNK, 1)).reshape(g, SSM_ROW, 1)
    blk = lambda *shape: pl.BlockSpec((ng,) + shape, lambda i: (i, 0, 0))
    act = pl.BlockSpec((b, ng, SSM_ROW, nc), lambda i: (0, i, 0, 0))
    return pl.pallas_call(
        _ssm_kernel,
        grid=(g // ng,),
        in_specs=[act, blk(SSM_ROW, SSM_ROW), blk(SSM_ROW, SSM_ROW), blk(SSM_ROW, LANES), blk(2, LANES),
                  blk(SSM_ROW, 1)],
        out_specs=act,
        out_shape=jax.ShapeDtypeStruct((b, g, SSM_ROW, nc), F32),
        scratch_shapes=[pltpu.VMEM((ng, SSM_ROW // LANES, b * nc, LANES), F32), pltpu.VMEM((ng, b * nc, LANES), F32)],
        compiler_params=_cparams(1),
        name="ssm",
    )(ut, tt, wz, wyt, cs, d_col)


def _post_kernel(x4_hbm, attn_ref, yt_ref, mod_ref, wglu_ref, bglu_ref, sn_ref, wout_f32_ref, nf_ref, wrt_ref, br_ref,
                 tri_ref, x1_ref, h2_ref, eidx_ref, wts_ref, lrank_ref, r0_ref, cnt_ref, carry_ref, xp_buf,
                 wglut_ref, wout_ref, wr_ref, sems):
    @pl.when((pl.program_id(0) == 0) & (pl.program_id(1) == 0))
    def _():
        carry_ref[...] = jnp.zeros_like(carry_ref)
        wglut_ref[...] = wglu_ref[...].T.astype(BF16)
        wout_ref[...] = wout_f32_ref[...].astype(BF16)
        wr_ref[...] = wrt_ref[...].T.astype(BF16)

    slot = _prefetch_pos_rows(x4_hbm, xp_buf, sems, POST_SUB * POST_POS)
    nc = attn_ref.shape[2]
    ts = POST_POS * nc
    d = x1_ref.shape[3]
    iota_e = lax.broadcasted_iota(I32, (N_EXPERTS, ts), 0).astype(F32)
    counts = []
    for sub in range(POST_SUB):
        pos = range(sub * POST_POS, (sub + 1) * POST_POS)
        lanes = slice(sub * ts, (sub + 1) * ts)
        yt = jnp.concatenate(
            [yt_ref[0, :, il * SSM_GROUP:(il + 1) * SSM_GROUP, :].reshape(D_SSM, nc) for il in pos], axis=1)
        g = jax.nn.gelu(yt)
        gate = jax.nn.sigmoid(jnp.dot(wglut_ref[...], g.astype(BF16), preferred_element_type=F32) + bglu_ref[...])
        ssm_t = _rms(g * gate, axis=0) * sn_ref[...]
        attn = attn_ref[0, sub * POST_POS:(sub + 1) * POST_POS].reshape(ts, D_ATTN)
        mixed = jnp.concatenate([attn.astype(BF16), ssm_t.T.astype(BF16)], axis=-1)
        o = jnp.dot(mixed, wout_ref[...], preferred_element_type=F32)
        x = jnp.concatenate([xp_buf[slot, il] for il in pos], axis=0)
        x1 = x + mod_ref[0, 2:3, :] * o
        x1_ref[0, sub * POST_POS:(sub + 1) * POST_POS] = x1.reshape(POST_POS, nc, d)
        h2 = _rms(x1) * nf_ref[...] * (1.0 + mod_ref[0, 4:5, :]) + mod_ref[0, 3:4, :]
        h2_ref[0, sub * POST_POS:(sub + 1) * POST_POS] = h2.astype(BF16).reshape(POST_POS, nc, d)

        logits = lax.dot_general(wr_ref[...], h2.astype(BF16), (((1,), (1,)), ((), ())),
                                 preferred_element_type=F32) + br_ref[...]
        l = logits
        idxs, vals = [], []
        for _ in range(TOP_K):
            m = jnp.max(l, axis=0, keepdims=True)
            idx = jnp.min(jnp.where(l == m, iota_e, float(N_EXPERTS)), axis=0, keepdims=True)
            idxs.append(idx)
            vals.append(m)
            l = jnp.where(iota_e == idx, -jnp.inf, l)
        es = [jnp.exp(v - vals[0]) for v in vals]
        tot = es[0] + es[1] + es[2] + es[3]
        member = jnp.zeros((N_EXPERTS, ts), F32)
        for idx in idxs:
            member = member + (iota_e == idx).astype(F32)
        before = jnp.dot(member.astype(BF16), tri_ref[...], preferred_element_type=F32)
        for k in range(TOP_K):
            eidx_ref[k:k + 1, lanes] = idxs[k].astype(I32)
            wts_ref[k:k + 1, lanes] = es[k] / tot
            lrank_ref[k:k + 1, lanes] = jnp.sum(jnp.where(iota_e == idxs[k], before, 0.0), axis=0,
                                                keepdims=True).astype(I32)
        counts.append(jnp.sum(member, axis=1, keepdims=True))

    carry = carry_ref[...]
    for sub in range(POST_SUB):
        r0_ref[sub] = carry.astype(I32)
        carry = carry + counts[sub]
    carry_ref[...] = carry
    cnt_ref[...] = carry.astype(I32)


def _post(x, attn, yt, mod, w_glu, b_glu, ssm_norm, w_out, norm_ffn, w_router, b_router):
    b, s, d = x.shape
    nc = s // SSM_CHUNK
    ts = POST_POS * nc
    npos = POST_SUB * POST_POS
    nt = SSM_CHUNK // npos
    t = b * s
    pm = lambda bi, j: (bi, j, 0, 0)
    const = lambda bi, j: (0, 0)
    tok = lambda bi, j: (0, bi * nt + j)
    tri = (lax.broadcasted_iota(I32, (ts, ts), 0) < lax.broadcasted_iota(I32, (ts, ts), 1)).astype(BF16)
    col = lambda a: a.reshape(-1, 1)
    return pl.pallas_call(
        _post_kernel,
        grid=(b, nt),
        in_specs=[pl.BlockSpec(memory_space=pl.ANY),
                  pl.BlockSpec((1, npos, nc, D_ATTN), pm),
                  pl.BlockSpec((1, N_GROUPS, npos * SSM_GROUP, nc), lambda bi, j: (bi, 0, j, 0)),
                  pl.BlockSpec((1, 6, d), lambda bi, j: (bi, 0, 0)),
                  pl.BlockSpec((D_SSM, D_SSM), const),
                  pl.BlockSpec((D_SSM, 1), const),
                  pl.BlockSpec((D_SSM, 1), const),
                  pl.BlockSpec((d, d), const),
                  pl.BlockSpec((1, d), const),
                  pl.BlockSpec((d, N_EXPERTS), const),
                  pl.BlockSpec((N_EXPERTS, 1), const),
                  pl.BlockSpec((ts, ts), const)],
        out_specs=[pl.BlockSpec((1, npos, nc, d), pm),
                   pl.BlockSpec((1, npos, nc, d), pm),
                   pl.BlockSpec((TOP_K, POST_SUB * ts), tok),
                   pl.BlockSpec((TOP_K, POST_SUB * ts), tok),
                   pl.BlockSpec((TOP_K, POST_SUB * ts), tok),
                   pl.BlockSpec((POST_SUB, N_EXPERTS, 1), lambda bi, j: (bi * nt + j, 0, 0)),
                   pl.BlockSpec((N_EXPERTS, 1), const)],
        out_shape=[jax.ShapeDtypeStruct((b, SSM_CHUNK, nc, d), F32),
                   jax.ShapeDtypeStruct((b, SSM_CHUNK, nc, d), BF16),
                   jax.ShapeDtypeStruct((TOP_K, t), I32),
                   jax.ShapeDtypeStruct((TOP_K, t), F32),
                   jax.ShapeDtypeStruct((TOP_K, t), I32),
                   jax.ShapeDtypeStruct((b * nt * POST_SUB, N_EXPERTS, 1), I32),
                   jax.ShapeDtypeStruct((N_EXPERTS, 1), I32)],
        scratch_shapes=[pltpu.VMEM((N_EXPERTS, 1), F32), pltpu.VMEM((2, npos, nc, d), F32),
                        pltpu.VMEM((D_SSM, D_SSM), BF16), pltpu.VMEM((d, d), BF16), pltpu.VMEM((N_EXPERTS, d), BF16),
                        pltpu.SemaphoreType.DMA((2,))],
        compiler_params=_cparams(2, ROW_VMEM_BYTES),
        name="post",
    )(x.reshape(b, nc, SSM_CHUNK, d), attn, yt, mod, w_glu, col(b_glu), col(ssm_norm), w_out,
      norm_ffn.reshape(1, -1), w_router, col(b_router), tri)


def _route_kernel(eidx_ref, lrank_ref, r0_ref, cnt_ref, ls_ref, tab_ref, te_ref, nv_ref, nx_ref, pad_ref):
    cnt = cnt_ref[...]
    tiles = (cnt + (RUN - 1 + FFN_ROWS - 1)) // FFN_ROWS
    er = lax.broadcasted_iota(I32, (N_EXPERTS, N_EXPERTS), 0)
    ec = lax.broadcasted_iota(I32, (N_EXPERTS, N_EXPERTS), 1)
    ltri = (ec < er).astype(BF16)

    def excl_cumsum(v):
        vb = jnp.broadcast_to(v.astype(F32), (N_EXPERTS, LANES)).astype(BF16)
        return jnp.dot(ltri, vb, preferred_element_type=F32)[:, 0:1].astype(I32)

    start_t = excl_cumsum(tiles)
    end_t = start_t + tiles
    start = start_t * FFN_ROWS
    pad_ref[...] = start + cnt

    nb = r0_ref.shape[0]
    ts = eidx_ref.shape[1] // nb
    iota_e = lax.broadcasted_iota(I32, (N_EXPERTS, ts), 0)
    iota_t = lax.broadcasted_iota(I32, (N_EXPERTS, TABW), 0)
    lane = lax.broadcasted_iota(I32, (1, TABW), 1)

    def block(b, carry):
        lanes = pl.ds(pl.multiple_of(b * ts, ts), ts)
        sels = [iota_e == eidx_ref[k:k + 1, lanes] for k in range(TOP_K)]
        member = sels[0].astype(I32) + sels[1].astype(I32) + sels[2].astype(I32) + sels[3].astype(I32)
        units = lax.shift_right_logical(jnp.sum(member, axis=1, keepdims=True) + (RUN - 1), RUN_SHIFT)
        u0 = excl_cumsum(units)
        for k in range(TOP_K):
            first = jnp.sum(jnp.where(sels[k], u0, 0), axis=0, keepdims=True)
            ls_ref[k:k + 1, lanes] = first * RUN + lrank_ref[k:k + 1, lanes]
        n_big = lax.shift_right_logical(units, 1)
        n_small = units & 1
        slot0 = start + r0_ref[b]

        def chunk_rows(idx, counts):
            c0 = excl_cumsum(counts)
            sel = iota_t == jnp.sum((idx >= c0 + counts).astype(I32), axis=0, keepdims=True)
            pick = lambda v: jnp.sum(jnp.where(sel, v, 0), axis=0, keepdims=True)
            j = idx - pick(c0)
            return pick(slot0), pick(u0), j, pick(n_big), idx < jnp.max(c0 + counts, axis=0, keepdims=True)

        s_b, u_b, j_b, _, ok_b = chunk_rows(lane, n_big)
        s_s, u_s, _, nb_s, ok_s = chunk_rows(lane - SMALL0, n_small)
        small = lane >= SMALL0
        slot = jnp.where(small, s_s + nb_s * BIG, s_b + j_b * BIG)
        local = jnp.where(small, (u_s + 2 * nb_s) * RUN, (u_b + 2 * j_b) * RUN)
        ok = (small & ok_s) | (jnp.logical_not(small) & ok_b)
        counts = jnp.where(lane == TABW - 2, jnp.sum(n_big, axis=0, keepdims=True),
                           jnp.sum(n_small, axis=0, keepdims=True))
        tab_ref[b, 0:1, :] = jnp.where(lane >= TABW - 2, counts, jnp.where(ok, slot, -1))
        tab_ref[b, 1:2, :] = jnp.where(ok, local, 0)
        return carry

    lax.fori_loop(0, nb, block, 0, unroll=2)

    nv = jnp.max(end_t, axis=0, keepdims=True)
    width = te_ref.shape[1]
    ti = jnp.minimum(lax.broadcasted_iota(I32, (N_EXPERTS, width), 1), nv - 1)
    te = jnp.minimum(jnp.sum((ti >= end_t).astype(I32), axis=0, keepdims=True), N_EXPERTS - 1)
    te_ref[...] = te
    nv_ref[...] = jnp.broadcast_to(nv, nv_ref.shape)
    ie = lax.broadcasted_iota(I32, (N_EXPERTS, width), 0)
    own_end = jnp.sum(jnp.where(ie == te, end_t, 0), axis=0, keepdims=True)
    nxt = jnp.minimum(jnp.sum((own_end >= end_t).astype(I32), axis=0, keepdims=True), N_EXPERTS - 1)
    nx_ref[...] = jnp.where(own_end < nv, nxt, -1)


def _route(eidx, lrank, r0, cnt, n_tiles):
    t = eidx.shape[1]
    nb = r0.shape[0]
    width = -(-n_tiles // LANES) * LANES
    return pl.pallas_call(
        _route_kernel,
        out_shape=[jax.ShapeDtypeStruct((TOP_K, t), I32),
                   jax.ShapeDtypeStruct((nb, 2, TABW), I32),
                   jax.ShapeDtypeStruct((1, width), I32),
                   jax.ShapeDtypeStruct((1, LANES), I32),
                   jax.ShapeDtypeStruct((1, width), I32),
                   jax.ShapeDtypeStruct((N_EXPERTS, 1), I32)],
        name="route",
    )(eidx, lrank, r0, cnt)


def _for_chunk_pairs(n, fn):
    def body(i, carry):
        fn(2 * i, 0)

        @pl.when(2 * i + 1 < n)
        def _():
            fn(2 * i + 1, 1)
        return carry
    lax.fori_loop(0, lax.shift_right_logical(n + 1, 1), body, 0)


def _for_block_chunks(tab_ref, blk, fn):
    base = blk * (2 * TABW)
    for first, count_lane, n_rows in ((0, TABW - 2, BIG), (SMALL0, TABW - 1, RUN)):
        def visit(c, parity, first=first, n_rows=n_rows):
            fn(tab_ref[base + first + c], pl.multiple_of(tab_ref[base + TABW + first + c], RUN), n_rows, parity)
        _for_chunk_pairs(tab_ref[base + count_lane], visit)


def _local_rows(ts):
    return ts * TOP_K + N_EXPERTS * RUN


def _dispatch_kernel(tab_ref, pad_ref, nv_ref, h_ref, ls_ref, xs_ref, buf, zbuf, sems, zsem, *, n_alloc):
    b = pl.program_id(0)
    slot = b % 2
    ts = h_ref.shape[0]
    local = buf.shape[2]

    def chunk_copy(sl, slot_row, local_row, n):
        return pltpu.make_async_copy(buf.at[sl, :, pl.ds(local_row, n), :], xs_ref.at[:, pl.ds(slot_row, n), :],
                                     sems.at[sl])

    @pl.when(b == 0)
    def _():
        zbuf[...] = jnp.zeros_like(zbuf)
        zrows = zbuf.shape[1]
        zero = lambda row: pltpu.make_async_copy(zbuf, xs_ref.at[:, pl.ds(row, zrows), :], zsem)
        for phase in range(3):
            for e in range(phase, N_EXPERTS, 3):
                zero(pad_ref[e, 0]).start()
            for e in range(phase, N_EXPERTS, 3):
                zero(pad_ref[e, 0]).wait()
        ztile = lambda i: pltpu.make_async_copy(zbuf.at[:, pl.ds(0, FFN_ROWS), :],
                                                xs_ref.at[:, pl.ds((nv_ref[0, 0] + i) * FFN_ROWS, FFN_ROWS), :], zsem)

        def tail_start(i, carry):
            ztile(i).start()
            return carry

        def tail_wait(i, carry):
            ztile(i).wait()
            return carry
        lax.fori_loop(0, n_alloc - nv_ref[0, 0], tail_start, 0)
        lax.fori_loop(0, n_alloc - nv_ref[0, 0], tail_wait, 0)

    r = lax.broadcasted_iota(I32, (local, ts), 0)
    hit = (r == ls_ref[0:1, :]) | (r == ls_ref[1:2, :]) | (r == ls_ref[2:3, :]) | (r == ls_ref[3:4, :])
    hit = hit.astype(BF16)
    for pb in range(PANELS):
        srt = jnp.dot(hit, h_ref[:, pb * PANEL_COLS:(pb + 1) * PANEL_COLS], preferred_element_type=F32)
        buf[slot, pb] = _pack_panel(srt, exact=True)

    @pl.when(b > 0)
    def _():
        _for_block_chunks(tab_ref, b - 1, lambda s, l, n, p: chunk_copy(1 - slot, s, l, n).wait())

    _for_block_chunks(tab_ref, b, lambda s, l, n, p: chunk_copy(slot, s, l, n).start(priority=p))

    @pl.when(b == pl.num_programs(0) - 1)
    def _():
        _for_block_chunks(tab_ref, b, lambda s, l, n, p: chunk_copy(slot, s, l, n).wait())


def _dispatch(tab, pad, nv, h2, ls, n_alloc):
    t, d = h2.shape
    nb = tab.shape[0] // (2 * TABW)
    ts = t // nb
    return pl.pallas_call(
        functools.partial(_dispatch_kernel, n_alloc=n_alloc),
        grid_spec=pltpu.PrefetchScalarGridSpec(
            num_scalar_prefetch=3,
            grid=(nb,),
            in_specs=[pl.BlockSpec((ts, d), lambda i, *_: (i, 0)),
                      pl.BlockSpec((TOP_K, ts), lambda i, *_: (0, i))],
            out_specs=pl.BlockSpec(memory_space=pl.ANY),
            scratch_shapes=[pltpu.VMEM((2, PANELS, _local_rows(ts), LANES), U32),
                            pltpu.VMEM((PANELS, FFN_ROWS + RUN, LANES), U32),
                            pltpu.SemaphoreType.DMA((2,)), pltpu.SemaphoreType.DMA],
        ),
        out_shape=jax.ShapeDtypeStruct((PANELS, n_alloc * FFN_ROWS, LANES), U32),
        compiler_params=_cparams(1, ROW_VMEM_BYTES),
        name="dispatch",
    )(tab, pad, nv, h2, ls)


def _ffn_kernel(te_ref, nv_ref, nx_ref, xs_ref, wgu_hbm, bgu_ref, wd_hbm, bd_ref, ys_ref,
                wgu_stage, wd_stage, wgu_scr, wd_scr, sems):
    p = pl.program_id(0)
    t0 = 2 * p
    t1 = t0 + 1
    e0 = te_ref[0, t0]
    e1 = te_ref[0, t1]
    v0 = t0 < nv_ref[0, 0]
    v1 = t1 < nv_ref[0, 0]
    new0 = (p == 0) | (e0 != te_ref[0, jnp.maximum(t0 - 1, 0)])
    same = v1 & (e1 == e0)

    def stage_copies(e):
        return (pltpu.make_async_copy(wgu_hbm.at[e], wgu_stage, sems.at[0]),
                pltpu.make_async_copy(wd_hbm.at[e], wd_stage, sems.at[1]))

    def load_expert(t, first):
        e = te_ref[0, t]
        if first:
            @pl.when(p == 0)
            def _():
                for cp in stage_copies(e):
                    cp.start()

        for cp in stage_copies(e):
            cp.wait()
        for c in range(2 * D_FF // PANEL_COLS):
            cols = slice(c * PANEL_COLS, (c + 1) * PANEL_COLS)
            wgu_scr[:, cols] = wgu_stage[:, cols].astype(BF16)
        for c in range(D_MODEL // PANEL_COLS):
            cols = slice(c * PANEL_COLS, (c + 1) * PANEL_COLS)
            first_half = wd_stage[:D_FF // 2, cols].astype(BF16).astype(F32)
            second_half = wd_stage[D_FF // 2:, cols].astype(BF16).astype(F32)
            words = lax.bitcast_convert_type(second_half, U32) | (lax.bitcast_convert_type(first_half, U32) >> 16)
            wd_scr[:, cols] = pltpu.bitcast(words, BF16)

        @pl.when(nx_ref[0, t] >= 0)
        def _():
            for cp in stage_copies(nx_ref[0, t]):
                cp.start()

    def run(lo, n, e):
        x = _unpack_panels([xs_ref[pb, lo:lo + n, :] for pb in range(PANELS)])
        h = jnp.dot(x, wgu_scr[...], preferred_element_type=F32) + bgu_ref[e]
        even = (lax.broadcasted_iota(I32, (n, LANES), 1) & 1) == 0
        acts = []
        for k in range(D_FF // LANES):
            a = h[:, k * LANES:(k + 1) * LANES]
            b = h[:, D_FF + k * LANES:D_FF + (k + 1) * LANES]
            gate = jnp.where(even, a, pltpu.roll(b, 1, 1))
            up = jnp.where(even, pltpu.roll(a, LANES - 1, 1), b)
            gate = jnp.minimum(gate, SWIGLU_LIMIT)
            up = jnp.clip(up, -SWIGLU_LIMIT, SWIGLU_LIMIT)
            acts.append(((up + 1.0) * (gate * jax.nn.sigmoid(SWIGLU_ALPHA * gate))).astype(BF16))
        act = jnp.concatenate(acts, axis=1)
        bd = bd_ref[e]
        for pb in range(PANELS):
            cols = slice(pb * PANEL_COLS, (pb + 1) * PANEL_COLS)
            y = jnp.dot(act, wd_scr[:, cols], preferred_element_type=F32) + bd[:, cols]
            ys_ref[pb, lo:lo + n, :] = _pack_panel(y)

    @pl.when(v0 & new0)
    def _():
        load_expert(t0, True)

    @pl.when(same)
    def _():
        run(0, 2 * FFN_ROWS, e0)

    @pl.when(v0 & jnp.logical_not(same))
    def _():
        run(0, FFN_ROWS, e0)

    @pl.when(v1 & jnp.logical_not(same))
    def _():
        load_expert(t1, False)
        run(FFN_ROWS, FFN_ROWS, e1)

    @pl.when(v0 & jnp.logical_not(v1))
    def _():
        ys_ref[:, FFN_ROWS:, :] = xs_ref[:, FFN_ROWS:, :]


def _ffn(te, nv, nx, xs, w_gate_up, bgu, w_down, bd, n_tiles):
    d = D_MODEL
    pair = lambda i, te, nv, nx: (0, jnp.minimum(i, lax.shift_right_logical(nv[0, 0] - 1, 1)), 0)
    whole = lambda i, te, nv, nx: (0, 0, 0)
    return pl.pallas_call(
        _ffn_kernel,
        grid_spec=pltpu.PrefetchScalarGridSpec(
            num_scalar_prefetch=3,
            grid=(n_tiles // 2,),
            in_specs=[pl.BlockSpec((PANELS, 2 * FFN_ROWS, LANES), pair),
                      pl.BlockSpec(memory_space=pl.ANY),
                      pl.BlockSpec((N_EXPERTS, 1, 2 * D_FF), whole),
                      pl.BlockSpec(memory_space=pl.ANY),
                      pl.BlockSpec((N_EXPERTS, 1, d), whole)],
            out_specs=pl.BlockSpec((PANELS, 2 * FFN_ROWS, LANES), pair),
            scratch_shapes=[pltpu.VMEM((d, 2 * D_FF), F32), pltpu.VMEM((D_FF, d), F32),
                            pltpu.VMEM((d, 2 * D_FF), BF16), pltpu.VMEM((D_FF, d), BF16),
                            pltpu.SemaphoreType.DMA((2,))],
        ),
        out_shape=jax.ShapeDtypeStruct(xs.shape, U32),
        input_output_aliases={3: 0},
        compiler_params=_cparams(1, FFN_VMEM_BYTES),
        name="ffn",
    )(te, nv, nx, xs, w_gate_up, bgu, w_down, bd)


def _combine_kernel(tab_ref, x1_ref, ls_ref, w_ref, mod_ref, ys_ref, o4_hbm, ybuf, ob_buf, sems, osems, *, n_steps):
    jj = pl.program_id(1)
    g = pl.program_id(0) * pl.num_programs(1) + jj
    slot = g % 2
    nc = x1_ref.shape[2]
    tt = POST_POS * nc
    d = x1_ref.shape[3]
    local = ybuf.shape[3]
    npos = COMB_SUB * POST_POS

    def for_step_chunks(step, sl, fn):
        for sub in range(COMB_SUB):
            def visit(s, l, n, p, sub=sub):
                fn(pltpu.make_async_copy(ys_ref.at[:, pl.ds(s, n), :], ybuf.at[sl, sub, :, pl.ds(l, n), :],
                                         sems.at[sl]), p)
            _for_block_chunks(tab_ref, step * COMB_SUB + sub, visit)

    @pl.when(g == 0)
    def _():
        ybuf[...] = jnp.zeros_like(ybuf)
        for_step_chunks(0, 0, lambda cp, p: cp.start(priority=p))

    @pl.when(g + 1 < n_steps)
    def _():
        for_step_chunks(g + 1, 1 - slot, lambda cp, p: cp.start(priority=p))

    for_step_chunks(g, slot, lambda cp, p: cp.wait())

    def out_copies(sl, b_, j_):
        return [pltpu.make_async_copy(ob_buf.at[sl, il], o4_hbm.at[b_, :, npos * j_ + il, :], osems.at[sl])
                for il in range(npos)]

    @pl.when(g >= 2)
    def _():
        for cp in out_copies(slot, 0, 0):
            cp.wait()

    r = lax.broadcasted_iota(I32, (tt, local), 1).astype(F32)
    to_cols = lambda a: jnp.concatenate([a, jnp.zeros_like(a)], axis=0).T
    for sub in range(COMB_SUB):
        lanes = slice(sub * tt, (sub + 1) * tt)
        ls_c = to_cols(ls_ref[:, lanes].astype(F32))
        w_c = to_cols(w_ref[:, lanes])
        wm = jnp.zeros((tt, local), F32)
        for k in range(TOP_K):
            wm = jnp.where(r == ls_c[:, k:k + 1], w_c[:, k:k + 1], wm)
        y = _unpack_panels([ybuf[slot, sub, pb] for pb in range(PANELS)])
        acc = jnp.dot(wm.astype(BF16), y, preferred_element_type=F32)
        out = x1_ref[0, sub * POST_POS:(sub + 1) * POST_POS].reshape(tt, d) + mod_ref[0, 5:6, :] * acc
        for il in range(POST_POS):
            ob_buf[slot, sub * POST_POS + il] = out[il * nc:(il + 1) * nc]

    for cp in out_copies(slot, pl.program_id(0), jj):
        cp.start()

    @pl.when(g == n_steps - 1)
    def _():
        for cp in out_copies(slot, 0, 0):
            cp.wait()
        if n_steps > 1:
            for cp in out_copies(1 - slot, 0, 0):
                cp.wait()


def _combine(tab, x1, ls, wts, mod, ys):
    b, _, nc, d = x1.shape
    s = SSM_CHUNK * nc
    tt = POST_POS * nc
    npos = COMB_SUB * POST_POS
    nt = SSM_CHUNK // npos
    o4 = pl.pallas_call(
        functools.partial(_combine_kernel, n_steps=b * nt),
        grid_spec=pltpu.PrefetchScalarGridSpec(
            num_scalar_prefetch=1,
            grid=(b, nt),
            in_specs=[pl.BlockSpec((1, npos, nc, d), lambda bi, j, *_: (bi, j, 0, 0)),
                      pl.BlockSpec((TOP_K, COMB_SUB * tt), lambda bi, j, *_: (0, bi * nt + j)),
                      pl.BlockSpec((TOP_K, COMB_SUB * tt), lambda bi, j, *_: (0, bi * nt + j)),
                      pl.BlockSpec((1, 6, d), lambda bi, j, *_: (bi, 0, 0)),
                      pl.BlockSpec(memory_space=pl.ANY)],
            out_specs=pl.BlockSpec(memory_space=pl.ANY),
            scratch_shapes=[pltpu.VMEM((2, COMB_SUB, PANELS, _local_rows(tt), LANES), U32),
                            pltpu.VMEM((2, npos, nc, d), F32),
                            pltpu.SemaphoreType.DMA((2,)), pltpu.SemaphoreType.DMA((2,))],
        ),
        out_shape=jax.ShapeDtypeStruct((b, nc, SSM_CHUNK, d), F32),
        compiler_params=_cparams(2),
        name="combine",
    )(tab, x1, ls, wts, mod, ys)
    return o4.reshape(b, s, d)


def kernel(x, c, w_ada, b_ada, norm_mix, w_in, b_in, q_norm, k_norm, sinks, lam_re, lam_im, log_dt, b_re, b_im,
           c_re, c_im, d_skip, w_glu, b_glu, attn_out_norm, ssm_out_norm, w_out, norm_ffn, w_router, b_router,
           w_gate_up, b_gate_up, w_down, b_down):
    b, s, d = x.shape
    t = b * s
    depth = w_ada.shape[0]
    n_tiles = -(-(t * TOP_K + N_EXPERTS * (RUN - 1 + FFN_ROWS - 1)) // FFN_ROWS)
    n_tiles += n_tiles % 2
    n_alloc = n_tiles + 2
    for l in range(depth):
        mod = _adaln(c, w_ada[l], b_ada[l]).reshape(b, 6, d)
        q, k, v, ut = _inproj(x, mod, norm_mix[l], w_in[l], b_in[l])
        attn = _attention(q, k, v, sinks[l], q_norm[l], k_norm[l], attn_out_norm[l])
        tt, wz, wyt, cs = _ssm_params(lam_re[l], lam_im[l], log_dt[l], b_re[l], b_im[l], c_re[l], c_im[l])
        yt = _ssm(ut, tt, wz, wyt, cs, d_skip[l])
        x1, h2, eidx, wts, lrank, r0, cnt = _post(x, attn, yt, mod, w_glu[l], b_glu[l], ssm_out_norm[l], w_out[l],
                                                  norm_ffn[l], w_router[l], b_router[l])
        ls, tab, te, nv, nx, pad = _route(eidx, lrank, r0, cnt, n_tiles)
        tab = tab.reshape(-1)
        xs = _dispatch(tab, pad, nv, h2.reshape(t, d), ls, n_alloc)
        ys = _ffn(te, nv, nx, xs, w_gate_up[l], b_gate_up[l][:, None, :],
                  w_down[l], b_down[l][:, None, :], n_tiles)
        x = _combine(tab, x1, ls, wts, mod, ys)
    return x
```

```python
import functools
import math

import jax
import jax.numpy as jnp
from jax import lax
from jax.experimental import pallas as pl
from jax.experimental.pallas import tpu as pltpu

F32 = jnp.float32
BF16 = jnp.bfloat16
U32 = jnp.uint32
I32 = jnp.int32

D_MODEL = 1024
HEAD_DIM = 64
N_HEADS = 8
N_KV_HEADS = 2
Q_PER_KV = N_HEADS // N_KV_HEADS
D_ATTN = N_HEADS * HEAD_DIM
D_KV = N_KV_HEADS * HEAD_DIM
D_QKV = D_ATTN + 2 * D_KV
WINDOW = 128
BLOCK = 128
D_SSM = D_MODEL - D_ATTN
SSM_GROUP = 16
N_GROUPS = D_SSM // SSM_GROUP
STATE = 64
N_EXPERTS = 32
TOP_K = 4
D_FF = D_MODEL
SWIGLU_LIMIT = 7.0
SWIGLU_ALPHA = 1.702
EPS = 1e-6
NEG_INF = -1e30

LANES = 128
SSM_CHUNK = 16
SSM_ROW = SSM_CHUNK * SSM_GROUP
N_POW = 2 * SSM_CHUNK
PANEL_COLS = 2 * LANES
PANELS = D_MODEL // PANEL_COLS

SSM_GROUPS_PER_STEP = 8
POS_PER_STEP = 8
ATTN_ROWS = 512
POST_POS = 2
POST_SUB = 4
COMB_SUB = 2
FFN_ROWS = 256
RUN = 8
RUN_SHIFT = 3
BIG = 2 * RUN
TABW = 128
SMALL0 = 80
PERM = 256
FFN_VMEM_BYTES = 40 * 1024 * 1024
ROW_VMEM_BYTES = 48 * 1024 * 1024

HIGHEST = lax.Precision.HIGHEST
_ARB = "arbitrary"


def _cparams(n, vmem=None):
    return pltpu.CompilerParams(dimension_semantics=(_ARB,) * n, vmem_limit_bytes=vmem)


def _rms(x, axis=-1):
    return x * lax.rsqrt(jnp.mean(x * x, axis=axis, keepdims=True) + EPS)


def _pack_panel(y, exact=False):
    hi, lo = y[:, :LANES], y[:, LANES:]
    if not exact:
        hi = hi.astype(BF16).astype(F32)
        lo = lo.astype(BF16).astype(F32)
    return lax.bitcast_convert_type(hi, U32) | (lax.bitcast_convert_type(lo, U32) >> 16)


def _unpack_panels(words):
    cols = []
    for w in words:
        cols.append(lax.bitcast_convert_type(w & jnp.uint32(0xFFFF0000), F32).astype(BF16))
        cols.append(lax.bitcast_convert_type(w << 16, F32).astype(BF16))
    return jnp.concatenate(cols, axis=-1)


def _prefetch_pos_rows(x4_hbm, buf, sems, n_pos):
    bi = pl.program_id(0)
    j = pl.program_id(1)
    nj = pl.num_programs(1)
    g = bi * nj + j
    slot = g % 2

    def copies(sl, b_, j_):
        return [pltpu.make_async_copy(x4_hbm.at[b_, :, n_pos * j_ + il, :], buf.at[sl, il], sems.at[sl])
                for il in range(n_pos)]

    @pl.when(g == 0)
    def _():
        for cp in copies(0, 0, 0):
            cp.start()

    @pl.when(g + 1 < pl.num_programs(0) * nj)
    def _():
        wrap = j + 1 == nj
        for cp in copies(1 - slot, jnp.where(wrap, bi + 1, bi), jnp.where(wrap, 0, j + 1)):
            cp.start()

    for cp in copies(slot, bi, j):
        cp.wait()
    return slot


def _to_lane_blocks(dst, src):
    for kb in range(dst.shape[0]):
        dst[kb] = src[:, kb * LANES:(kb + 1) * LANES]


def _adaln_kernel(c_ref, w_ref, b_ref, o_ref):
    c = c_ref[...]
    ca = c * jax.nn.sigmoid(c)
    o_ref[...] = jnp.dot(ca, w_ref[...], preferred_element_type=F32, precision=HIGHEST) + b_ref[...]


def _inproj_kernel(x4_hbm, x_ref, mod_ref, g_ref, w_ref, bqkv_ref, but_ref, q_ref, k_ref, v_ref, ut_ref,
                   xp_buf, wqkv_scr, wut_scr, sems):
    nc = ut_ref.shape[3]

    @pl.when((pl.program_id(0) == 0) & (pl.program_id(1) == 0))
    def _():
        wqkv_scr[...] = w_ref[:, :D_QKV].astype(BF16)
        wut_scr[...] = w_ref[:, D_QKV:].T.astype(BF16)

    slot = _prefetch_pos_rows(x4_hbm, xp_buf, sems, POS_PER_STEP)
    gain = g_ref[...]
    scale = 1.0 + mod_ref[0, 1:2, :]
    shift = mod_ref[0, 0:1, :]

    def norm_mod(x):
        return (_rms(x) * gain * scale + shift).astype(BF16)

    proj = jnp.dot(norm_mod(x_ref[0]), wqkv_scr[...], preferred_element_type=F32) + bqkv_ref[...]
    q_ref[0] = proj[:, :D_ATTN].astype(BF16)
    k_ref[0] = proj[:, D_ATTN:D_ATTN + D_KV].astype(BF16)
    v_ref[0] = proj[:, D_ATTN + D_KV:].astype(BF16)

    hs = jnp.concatenate([norm_mod(xp_buf[slot, il]) for il in range(POS_PER_STEP)], axis=0)
    ut = lax.dot_general(wut_scr[...], hs, (((1,), (1,)), ((), ())), preferred_element_type=F32) + but_ref[...]
    for il in range(POS_PER_STEP):
        piece = ut[:, il * nc:(il + 1) * nc].astype(BF16)
        ut_ref[0, :, il * SSM_GROUP:(il + 1) * SSM_GROUP, :] = piece.reshape(N_GROUPS, SSM_GROUP, nc)


def _inproj(x, mod, gain, w_in, b_in):
    b, s, d = x.shape
    nc = s // SSM_CHUNK
    rows = POS_PER_STEP * nc
    row = lambda bi, j: (bi, j, 0)
    const = lambda bi, j: (0, 0)
    return pl.pallas_call(
        _inproj_kernel,
        grid=(b, SSM_CHUNK // POS_PER_STEP),
        in_specs=[pl.BlockSpec(memory_space=pl.ANY),
                  pl.BlockSpec((1, rows, d), row),
                  pl.BlockSpec((1, 6, d), lambda bi, j: (bi, 0, 0)),
                  pl.BlockSpec((1, d), const),
                  pl.BlockSpec((d, D_QKV + D_SSM), const),
                  pl.BlockSpec((1, D_QKV), const),
                  pl.BlockSpec((D_SSM, 1), const)],
        out_specs=[pl.BlockSpec((1, rows, D_ATTN), row),
                   pl.BlockSpec((1, rows, D_KV), row),
                   pl.BlockSpec((1, rows, D_KV), row),
                   pl.BlockSpec((1, N_GROUPS, POS_PER_STEP * SSM_GROUP, nc), lambda bi, j: (bi, 0, j, 0))],
        out_shape=[jax.ShapeDtypeStruct((b, s, D_ATTN), BF16),
                   jax.ShapeDtypeStruct((b, s, D_KV), BF16),
                   jax.ShapeDtypeStruct((b, s, D_KV), BF16),
                   jax.ShapeDtypeStruct((b, N_GROUPS, SSM_ROW, nc), BF16)],
        scratch_shapes=[pltpu.VMEM((2, POS_PER_STEP, nc, d), F32), pltpu.VMEM((d, D_QKV), BF16),
                        pltpu.VMEM((D_SSM, d), BF16), pltpu.SemaphoreType.DMA((2,))],
        compiler_params=_cparams(2, ROW_VMEM_BYTES),
        name="inproj",
    )(x.reshape(b, nc, SSM_CHUNK, d), x, mod, gain.reshape(1, d), w_in, b_in[:D_QKV].reshape(1, D_QKV),
      b_in[D_QKV:].reshape(D_SSM, 1))


def _half_norm(x, low):
    sq = x * x
    s_lo = jnp.sum(jnp.where(low, sq, 0.0), axis=-1, keepdims=True)
    s_hi = jnp.sum(sq, axis=-1, keepdims=True) - s_lo
    inv = 1.0 / HEAD_DIM
    scale = jnp.where(low, lax.rsqrt(s_lo * inv + EPS), lax.rsqrt(s_hi * inv + EPS))
    return x * scale


def _attn_block(first, q, k_prev, k_cur, v_prev, v_cur, sinks_ref, qn, low, upper, rblk):
    no_prev = jnp.where(first, NEG_INF, 0.0)
    out_blocks = []
    for hk in range(N_KV_HEADS):
        qs = []
        for j in range(Q_PER_KV // 2):
            blk = hk * (Q_PER_KV // 2) + j
            qb = _half_norm(q[:, blk * LANES:(blk + 1) * LANES], low) * qn * (1.0 / math.sqrt(HEAD_DIM))
            qs.append(jnp.where(low, qb, 0.0))
            qs.append(jnp.where(low, 0.0, qb))
        qg = jnp.concatenate(qs, axis=0).astype(BF16)
        nt = (((1,), (1,)), ((), ()))
        s_prev = lax.dot_general(qg, k_prev[hk], nt, preferred_element_type=F32)
        s_cur = lax.dot_general(qg, k_cur[hk], nt, preferred_element_type=F32)
        s = jnp.where(upper, s_prev + no_prev, s_cur)
        sink = jnp.zeros((Q_PER_KV * BLOCK, 1), F32)
        for g in range(Q_PER_KV):
            sink = jnp.where(rblk == g, sinks_ref[hk * Q_PER_KV + g], sink)
        m = jnp.maximum(jnp.max(s, axis=-1, keepdims=True), sink)
        p = jnp.exp(s - m)
        den = jnp.sum(p, axis=-1, keepdims=True) + jnp.exp(sink - m)
        o = (jnp.dot(jnp.where(upper, p, 0.0).astype(BF16), v_prev[hk], preferred_element_type=F32)
             + jnp.dot(jnp.where(upper, 0.0, p).astype(BF16), v_cur[hk], preferred_element_type=F32)) / den
        for j in range(Q_PER_KV // 2):
            ev = o[(2 * j) * BLOCK:(2 * j + 1) * BLOCK]
            od = o[(2 * j + 1) * BLOCK:(2 * j + 2) * BLOCK]
            out_blocks.append(jnp.where(low, ev, od))
    return jnp.concatenate(out_blocks, axis=-1)


def _attn_kernel(sinks_ref, q_ref, k_ref, v_ref, qn_ref, kn_ref, on_ref, o_hbm, a_buf, sems, *, n_steps):
    step = pl.program_id(1)
    g = pl.program_id(0) * pl.num_programs(1) + step
    slot = g % 2
    cps = ATTN_ROWS // SSM_CHUNK
    nq = ATTN_ROWS // BLOCK

    def out_copies(sl, b_, s_):
        return [pltpu.make_async_copy(a_buf.at[sl, :, i, :], o_hbm.at[b_, i, pl.ds(s_ * cps, cps), :], sems.at[sl])
                for i in range(SSM_CHUNK)]

    @pl.when(g >= 2)
    def _():
        for cp in out_copies(slot, 0, 0):
            cp.wait()

    low = lax.broadcasted_iota(I32, (1, LANES), 1) < HEAD_DIM
    rows = Q_PER_KV * BLOCK
    upper = lax.broadcasted_iota(I32, (rows, BLOCK), 1) > lax.broadcasted_iota(I32, (rows, BLOCK), 0) % BLOCK
    rblk = lax.broadcasted_iota(I32, (rows, 1), 0) // BLOCK

    cur = pl.multiple_of(step * ATTN_ROWS, ATTN_ROWS)
    prev = pl.multiple_of(jnp.maximum(step * nq - 1, 0) * BLOCK, BLOCK)
    kall = jnp.concatenate([k_ref[0, pl.ds(prev, BLOCK), :], k_ref[0, pl.ds(cur, ATTN_ROWS), :]], axis=0).astype(F32)
    vall = jnp.concatenate([v_ref[0, pl.ds(prev, BLOCK), :], v_ref[0, pl.ds(cur, ATTN_ROWS), :]], axis=0).astype(F32)
    kall = _half_norm(kall, low) * kn_ref[...]
    kswap = pltpu.roll(kall, HEAD_DIM, axis=1)
    vswap = pltpu.roll(vall, HEAD_DIM, axis=1)
    k_dup = [jnp.where(low, kall, kswap).astype(BF16), jnp.where(low, kswap, kall).astype(BF16)]
    v_dup = [jnp.where(low, vall, vswap).astype(BF16), jnp.where(low, vswap, vall).astype(BF16)]
    blk = lambda a, i: [a[hk][i * BLOCK:(i + 1) * BLOCK] for hk in range(N_KV_HEADS)]

    for qb in range(nq):
        q = q_ref[0, qb * BLOCK:(qb + 1) * BLOCK, :].astype(F32)
        attn = _attn_block((step == 0) if qb == 0 else False, q, blk(k_dup, qb), blk(k_dup, qb + 1),
                           blk(v_dup, qb), blk(v_dup, qb + 1), sinks_ref, qn_ref[...], low, upper, rblk)
        attn = _rms(attn) * on_ref[...]
        cpb = BLOCK // SSM_CHUNK
        a_buf[slot, qb * cpb:(qb + 1) * cpb] = attn.reshape(cpb, SSM_CHUNK, D_ATTN)

    for cp in out_copies(slot, pl.program_id(0), step):
        cp.start()

    @pl.when(g == n_steps - 1)
    def _():
        for cp in out_copies(slot, 0, 0):
            cp.wait()
        if n_steps > 1:
            for cp in out_copies(1 - slot, 0, 0):
                cp.wait()


def _attention(q, k, v, sinks, q_norm, k_norm, out_norm):
    b, s, _ = q.shape
    tile2 = lambda g: jnp.tile(g.reshape(1, HEAD_DIM), (1, 2))
    cps = ATTN_ROWS // SSM_CHUNK
    return pl.pallas_call(
        functools.partial(_attn_kernel, n_steps=b * (s // ATTN_ROWS)),
        grid=(b, s // ATTN_ROWS),
        in_specs=[pl.BlockSpec(memory_space=pltpu.SMEM),
                  pl.BlockSpec((1, ATTN_ROWS, D_ATTN), lambda bi, n: (bi, n, 0)),
                  pl.BlockSpec((1, s, D_KV), lambda bi, n: (bi, 0, 0)),
                  pl.BlockSpec((1, s, D_KV), lambda bi, n: (bi, 0, 0)),
                  pl.BlockSpec((1, LANES), lambda bi, n: (0, 0)),
                  pl.BlockSpec((1, LANES), lambda bi, n: (0, 0)),
                  pl.BlockSpec((1, D_ATTN), lambda bi, n: (0, 0))],
        out_specs=pl.BlockSpec(memory_space=pl.ANY),
        out_shape=jax.ShapeDtypeStruct((b, SSM_CHUNK, s // SSM_CHUNK, D_ATTN), F32),
        scratch_shapes=[pltpu.VMEM((2, cps, SSM_CHUNK, D_ATTN), F32), pltpu.SemaphoreType.DMA((2,))],
        compiler_params=_cparams(2),
        name="attention",
    )(sinks, q, k, v, tile2(q_norm), tile2(k_norm), out_norm.reshape(1, D_ATTN))


def _cmul(ar, ai, br, bi):
    return ar * br - ai * bi, ar * bi + ai * br


def _prep_kernel(c_ref, w_ref, b_ref, *refs):
    mod_ref, refs = refs[5], refs[:5] + refs[6:]
    _adaln_kernel(c_ref, w_ref, b_ref, mod_ref)
    for gi in range(refs[0].shape[0]):
        _ssm_param_group(gi, *refs)


def _ssm_param_group(gi, lam_ref, bre_ref, bim_ref, cre_ref, cim_ref, tt_ref, wz_ref, wyt_ref, cs_ref):
    f32dot = functools.partial(jnp.dot, preferred_element_type=F32, precision=HIGHEST)
    lr = lam_ref[gi, 0:1, :]
    li = lam_ref[gi, 1:2, :]
    dt = jnp.exp(lam_ref[gi, 2:3, :])
    rho = lr * dt
    th = li * dt
    imag_lane = lax.broadcasted_iota(I32, (1, LANES), 1) >= STATE

    kk = (lax.broadcasted_iota(I32, (N_POW, 1), 0) - (SSM_CHUNK - 1)).astype(F32)
    mag = jnp.exp(rho * kk)
    pw_r = mag * jnp.cos(th * kk)
    pw_i = mag * jnp.sin(th * kk)
    lb_r = pw_r[SSM_CHUNK:SSM_CHUNK + 1]
    lb_i = pw_i[SSM_CHUNK:SSM_CHUNK + 1]
    den = lr * lr + li * li
    coef_r = ((lb_r - 1.0) * lr + lb_i * li) / den
    coef_i = (lb_i * lr - (lb_r - 1.0) * li) / den

    eye = (lax.broadcasted_iota(I32, (SSM_GROUP, SSM_GROUP), 0)
           == lax.broadcasted_iota(I32, (SSM_GROUP, SSM_GROUP), 1)).astype(F32)
    lane_fold = (lax.broadcasted_iota(I32, (STATE, LANES), 1) % STATE
                 == lax.broadcasted_iota(I32, (STATE, LANES), 0)).astype(F32)

    def tile_pos(x):
        return jnp.concatenate([x] * SSM_CHUNK, axis=0)

    def power_rows(k_of_pos):
        idx = [k_of_pos(p) + (SSM_CHUNK - 1) for p in range(SSM_CHUNK)]
        rep = lambda t: jnp.concatenate([jnp.broadcast_to(t[r:r + 1], (SSM_GROUP, LANES)) for r in idx], axis=0)
        return rep(pw_r), rep(pw_i)

    def b_rows(b_ref):
        b2 = jnp.concatenate([b_ref[gi], b_ref[gi]], axis=0)
        return tile_pos(lax.dot_general(eye, b2, (((1,), (1,)), ((), ())), preferred_element_type=F32,
                                        precision=HIGHEST))

    def c_rows(c_ref):
        return tile_pos(f32dot(c_ref[gi], lane_fold))

    bbar_r, bbar_i = _cmul(coef_r, coef_i, b_rows(bre_ref), b_rows(bim_ref))
    c_r = c_rows(cre_ref)
    c_i = c_rows(cim_ref)

    a_r, a_i = _cmul(bbar_r, bbar_i, *power_rows(lambda p: -p))
    a2c = jnp.where(imag_lane, -a_i, a_r)
    m_r, m_i = _cmul(c_r, c_i, *power_rows(lambda p: p))
    bmc = jnp.where(imag_lane, m_i, m_r)
    tt = f32dot(bmc, a2c.T)
    causal = (lax.broadcasted_iota(I32, (SSM_ROW, 1), 0) // SSM_GROUP
              >= lax.broadcasted_iota(I32, (1, SSM_ROW), 1) // SSM_GROUP)
    tt_ref[gi] = jnp.where(causal, tt, 0.0).astype(BF16)

    w_r, w_i = _cmul(bbar_r, bbar_i, *power_rows(lambda p: SSM_CHUNK - 1 - p))
    wz_ref[gi, :, :LANES] = jnp.where(imag_lane, w_i, w_r).astype(BF16)
    wz_ref[gi, :, LANES:] = jnp.where(imag_lane, w_r, w_i).astype(BF16)

    y_r, y_i = _cmul(c_r, c_i, *power_rows(lambda p: p + 1))
    wyt_ref[gi] = jnp.where(imag_lane, -y_i, y_r).astype(BF16)

    cs_ref[gi, 0:1, :] = pw_r[N_POW - 1:N_POW]
    cs_ref[gi, 1:2, :] = jnp.where(imag_lane, pw_i[N_POW - 1:N_POW], -pw_i[N_POW - 1:N_POW])


def _prep(c, w_ada, b_ada, lam_re, lam_im, log_dt, b_re, b_im, c_re, c_im):
    b, d = c.shape
    g = lam_re.shape[0]
    lam = jnp.stack([lam_re, lam_im, jnp.broadcast_to(log_dt[:, None], (g, STATE))], axis=1)
    lam = jnp.concatenate([lam, lam], axis=2)
    ng = SSM_GROUPS_PER_STEP
    steps = g // ng
    cols = w_ada.shape[1] // steps
    blk = lambda *shape: pl.BlockSpec((ng,) + shape, lambda i: (i, 0, 0))
    return pl.pallas_call(
        _prep_kernel,
        grid=(steps,),
        in_specs=[pl.BlockSpec((b, d), lambda i: (0, 0)),
                  pl.BlockSpec((d, cols), lambda i: (0, i)),
                  pl.BlockSpec((1, cols), lambda i: (0, i)),
                  blk(3, LANES), blk(STATE, SSM_GROUP), blk(STATE, SSM_GROUP), blk(SSM_GROUP, STATE),
                  blk(SSM_GROUP, STATE)],
        out_specs=[pl.BlockSpec((b, cols), lambda i: (0, i)),
                   blk(SSM_ROW, SSM_ROW), blk(SSM_ROW, SSM_ROW), blk(SSM_ROW, LANES), blk(2, LANES)],
        out_shape=[jax.ShapeDtypeStruct((b, w_ada.shape[1]), F32),
                   jax.ShapeDtypeStruct((g, SSM_ROW, SSM_ROW), BF16),
                   jax.ShapeDtypeStruct((g, SSM_ROW, SSM_ROW), BF16),
                   jax.ShapeDtypeStruct((g, SSM_ROW, LANES), BF16),
                   jax.ShapeDtypeStruct((g, 2, LANES), F32)],
        compiler_params=_cparams(1, ROW_VMEM_BYTES),
        name="prep",
    )(c, w_ada, b_ada.reshape(1, -1), lam, b_re, b_im, c_re, c_im)


def _ssm_kernel(ut_ref, tt_ref, wz_ref, wyt_ref, cs_ref, d_ref, yt_ref, z_scr, s_scr):
    batch, ng, _, nc = ut_ref.shape
    uts = [jnp.concatenate([ut_ref[b, gi] for b in range(batch)], axis=1) for gi in range(ng)]
    for gi in range(ng):
        z = lax.dot_general(uts[gi], wz_ref[gi], (((0,), (0,)), ((), ())), preferred_element_type=F32)
        _to_lane_blocks(z_scr.at[gi], z)
    c1 = [cs_ref[gi, 0:1, :] for gi in range(ng)]
    c2 = [cs_ref[gi, 1:2, :] for gi in range(ng)]

    def step(c, carry):
        rows = pl.ds(c, batch, stride=nc)
        out = []
        for gi in range(ng):
            s1, s2 = carry[gi]
            s_scr[gi, rows, :] = s1
            out.append((c1[gi] * s1 + c2[gi] * s2 + z_scr[gi, 0, rows, :],
                        c1[gi] * s2 - c2[gi] * s1 + z_scr[gi, 1, rows, :]))
        return tuple(out)

    zero = jnp.zeros((batch, LANES), F32)
    lax.fori_loop(0, nc, step, ((zero, zero),) * ng, unroll=8)
    for gi in range(ng):
        y = jnp.dot(tt_ref[gi], uts[gi], preferred_element_type=F32)
        y = y + lax.dot_general(wyt_ref[gi], s_scr[gi].astype(BF16), (((1,), (1,)), ((), ())),
                                preferred_element_type=F32)
        y = y + d_ref[gi] * uts[gi].astype(F32)
        for b in range(batch):
            yt_ref[b, gi] = y[:, b * nc:(b + 1) * nc]


def _ssm(ut, tt, wz, wyt, cs, d_skip):
    b, g, _, nc = ut.shape
    ng = SSM_GROUPS_PER_STEP
    d_col = jnp.tile(d_skip.reshape(g, 1, SSM_GROUP), (1, SSM_CHUNK, 1)).reshape(g, SSM_ROW, 1)
    blk = lambda *shape: pl.BlockSpec((ng,) + shape, lambda i: (i, 0, 0))
    act = pl.BlockSpec((b, ng, SSM_ROW, nc), lambda i: (0, i, 0, 0))
    return pl.pallas_call(
        _ssm_kernel,
        grid=(g // ng,),
        in_specs=[act, blk(SSM_ROW, SSM_ROW), blk(SSM_ROW, SSM_ROW), blk(SSM_ROW, LANES), blk(2, LANES),
                  blk(SSM_ROW, 1)],
        out_specs=act,
        out_shape=jax.ShapeDtypeStruct((b, g, SSM_ROW, nc), F32),
        scratch_shapes=[pltpu.VMEM((ng, SSM_ROW // LANES, b * nc, LANES), F32), pltpu.VMEM((ng, b * nc, LANES), F32)],
        compiler_params=_cparams(1),
        name="ssm",
    )(ut, tt, wz, wyt, cs, d_col)


def _post_kernel(x4_hbm, attn_ref, yt_ref, mod_ref, wglu_ref, bglu_ref, sn_ref, wout_f32_ref, nf_ref, wrt_ref, br_ref,
                 tri_ref, x1_ref, h2_ref, eidx_ref, wts_ref, lrank_ref, r0_ref, cnt_ref, carry_ref, xp_buf,
                 wglut_ref, wout_ref, wr_ref, sems):
    @pl.when((pl.program_id(0) == 0) & (pl.program_id(1) == 0))
    def _():
        carry_ref[...] = jnp.zeros_like(carry_ref)
        wglut_ref[...] = wglu_ref[...].T.astype(BF16)
        wout_ref[...] = wout_f32_ref[...].astype(BF16)
        wr_ref[...] = wrt_ref[...].T.astype(BF16)

    slot = _prefetch_pos_rows(x4_hbm, xp_buf, sems, POST_SUB * POST_POS)
    nc = attn_ref.shape[2]
    ts = POST_POS * nc
    d = x1_ref.shape[3]
    iota_e = lax.broadcasted_iota(I32, (N_EXPERTS, ts), 0).astype(F32)
    counts = []
    for sub in range(POST_SUB):
        pos = range(sub * POST_POS, (sub + 1) * POST_POS)
        lanes = slice(sub * ts, (sub + 1) * ts)
        yt = jnp.concatenate(
            [yt_ref[0, :, il * SSM_GROUP:(il + 1) * SSM_GROUP, :].reshape(D_SSM, nc) for il in pos], axis=1)
        g = jax.nn.gelu(yt)
        gate = jax.nn.sigmoid(jnp.dot(wglut_ref[...], g.astype(BF16), preferred_element_type=F32) + bglu_ref[...])
        ssm_t = _rms(g * gate, axis=0) * sn_ref[...]
        attn = attn_ref[0, sub * POST_POS:(sub + 1) * POST_POS].reshape(ts, D_ATTN)
        mixed = jnp.concatenate([attn.astype(BF16), ssm_t.T.astype(BF16)], axis=-1)
        o = jnp.dot(mixed, wout_ref[...], preferred_element_type=F32)
        x = jnp.concatenate([xp_buf[slot, il] for il in pos], axis=0)
        x1 = x + mod_ref[0, 2:3, :] * o
        x1_ref[0, sub * POST_POS:(sub + 1) * POST_POS] = x1.reshape(POST_POS, nc, d)
        h2 = _rms(x1) * nf_ref[...] * (1.0 + mod_ref[0, 4:5, :]) + mod_ref[0, 3:4, :]
        h2_ref[0, sub * POST_POS:(sub + 1) * POST_POS] = h2.astype(BF16).reshape(POST_POS, nc, d)

        logits = lax.dot_general(wr_ref[...], h2.astype(BF16), (((1,), (1,)), ((), ())),
                                 preferred_element_type=F32) + br_ref[...]
        l = logits
        idxs, vals = [], []
        for _ in range(TOP_K):
            m = jnp.max(l, axis=0, keepdims=True)
            idx = jnp.min(jnp.where(l == m, iota_e, float(N_EXPERTS)), axis=0, keepdims=True)
            idxs.append(idx)
            vals.append(m)
            l = jnp.where(iota_e == idx, -jnp.inf, l)
        es = [jnp.exp(v - vals[0]) for v in vals]
        tot = es[0] + es[1] + es[2] + es[3]
        member = jnp.zeros((N_EXPERTS, ts), F32)
        for idx in idxs:
            member = member + (iota_e == idx).astype(F32)
        before = jnp.dot(member.astype(BF16), tri_ref[...], preferred_element_type=F32)
        for k in range(TOP_K):
            eidx_ref[k:k + 1, lanes] = idxs[k].astype(I32)
            wts_ref[k:k + 1, lanes] = es[k] / tot
            lrank_ref[k:k + 1, lanes] = jnp.sum(jnp.where(iota_e == idxs[k], before, 0.0), axis=0,
                                                keepdims=True).astype(I32)
        counts.append(jnp.sum(member, axis=1, keepdims=True))

    carry = carry_ref[...]
    for sub in range(POST_SUB):
        r0_ref[sub] = carry.astype(I32)
        carry = carry + counts[sub]
    carry_ref[...] = carry
    cnt_ref[...] = carry.astype(I32)


def _post(x, attn, yt, mod, w_glu, b_glu, ssm_norm, w_out, norm_ffn, w_router, b_router):
    b, s, d = x.shape
    nc = s // SSM_CHUNK
    ts = POST_POS * nc
    npos = POST_SUB * POST_POS
    nt = SSM_CHUNK // npos
    t = b * s
    pm = lambda bi, j: (bi, j, 0, 0)
    const = lambda bi, j: (0, 0)
    tok = lambda bi, j: (0, bi * nt + j)
    tri = (lax.broadcasted_iota(I32, (ts, ts), 0) < lax.broadcasted_iota(I32, (ts, ts), 1)).astype(BF16)
    col = lambda a: a.reshape(-1, 1)
    return pl.pallas_call(
        _post_kernel,
        grid=(b, nt),
        in_specs=[pl.BlockSpec(memory_space=pl.ANY),
                  pl.BlockSpec((1, npos, nc, D_ATTN), pm),
                  pl.BlockSpec((1, N_GROUPS, npos * SSM_GROUP, nc), lambda bi, j: (bi, 0, j, 0)),
                  pl.BlockSpec((1, 6, d), lambda bi, j: (bi, 0, 0)),
                  pl.BlockSpec((D_SSM, D_SSM), const),
                  pl.BlockSpec((D_SSM, 1), const),
                  pl.BlockSpec((D_SSM, 1), const),
                  pl.BlockSpec((d, d), const),
                  pl.BlockSpec((1, d), const),
                  pl.BlockSpec((d, N_EXPERTS), const),
                  pl.BlockSpec((N_EXPERTS, 1), const),
                  pl.BlockSpec((ts, ts), const)],
        out_specs=[pl.BlockSpec((1, npos, nc, d), pm),
                   pl.BlockSpec((1, npos, nc, d), pm),
                   pl.BlockSpec((TOP_K, POST_SUB * ts), tok),
                   pl.BlockSpec((TOP_K, POST_SUB * ts), tok),
                   pl.BlockSpec((TOP_K, POST_SUB * ts), tok),
                   pl.BlockSpec((POST_SUB, N_EXPERTS, 1), lambda bi, j: (bi * nt + j, 0, 0)),
                   pl.BlockSpec((N_EXPERTS, 1), const)],
        out_shape=[jax.ShapeDtypeStruct((b, SSM_CHUNK, nc, d), F32),
                   jax.ShapeDtypeStruct((b, SSM_CHUNK, nc, d), BF16),
                   jax.ShapeDtypeStruct((TOP_K, t), I32),
                   jax.ShapeDtypeStruct((TOP_K, t), F32),
                   jax.ShapeDtypeStruct((TOP_K, t), I32),
                   jax.ShapeDtypeStruct((b * nt * POST_SUB, N_EXPERTS, 1), I32),
                   jax.ShapeDtypeStruct((N_EXPERTS, 1), I32)],
        scratch_shapes=[pltpu.VMEM((N_EXPERTS, 1), F32), pltpu.VMEM((2, npos, nc, d), F32),
                        pltpu.VMEM((D_SSM, D_SSM), BF16), pltpu.VMEM((d, d), BF16), pltpu.VMEM((N_EXPERTS, d), BF16),
                        pltpu.SemaphoreType.DMA((2,))],
        compiler_params=_cparams(2, ROW_VMEM_BYTES),
        name="post",
    )(x.reshape(b, nc, SSM_CHUNK, d), attn, yt, mod, w_glu, col(b_glu), col(ssm_norm), w_out,
      norm_ffn.reshape(1, -1), w_router, col(b_router), tri)


def _route_kernel(eidx_ref, lrank_ref, r0_ref, cnt_ref, ls_ref, tab_ref, te_ref, nv_ref, nx_ref, pad_ref):
    cnt = cnt_ref[...]
    tiles = (cnt + (RUN - 1 + FFN_ROWS - 1)) // FFN_ROWS
    er = lax.broadcasted_iota(I32, (N_EXPERTS, N_EXPERTS), 0)
    ec = lax.broadcasted_iota(I32, (N_EXPERTS, N_EXPERTS), 1)
    ltri = (ec < er).astype(BF16)

    def excl_cumsum(v):
        vb = jnp.broadcast_to(v.astype(F32), (N_EXPERTS, LANES)).astype(BF16)
        return jnp.dot(ltri, vb, preferred_element_type=F32)[:, 0:1].astype(I32)

    start_t = excl_cumsum(tiles)
    end_t = start_t + tiles
    start = start_t * FFN_ROWS
    pad_ref[...] = start + cnt

    nb = r0_ref.shape[0]
    ts = eidx_ref.shape[1] // nb
    iota_e = lax.broadcasted_iota(I32, (N_EXPERTS, ts), 0)
    iota_t = lax.broadcasted_iota(I32, (N_EXPERTS, TABW), 0)
    lane = lax.broadcasted_iota(I32, (1, TABW), 1)

    def block(b, carry):
        lanes = pl.ds(pl.multiple_of(b * ts, ts), ts)
        sels = [iota_e == eidx_ref[k:k + 1, lanes] for k in range(TOP_K)]
        member = sels[0].astype(I32) + sels[1].astype(I32) + sels[2].astype(I32) + sels[3].astype(I32)
        units = lax.shift_right_logical(jnp.sum(member, axis=1, keepdims=True) + (RUN - 1), RUN_SHIFT)
        u0 = excl_cumsum(units)
        for k in range(TOP_K):
            first = jnp.sum(jnp.where(sels[k], u0, 0), axis=0, keepdims=True)
            ls_ref[k:k + 1, lanes] = first * RUN + lrank_ref[k:k + 1, lanes]
        n_big = lax.shift_right_logical(units, 1)
        n_small = units & 1
        slot0 = start + r0_ref[b]

        def chunk_rows(idx, counts):
            c0 = excl_cumsum(counts)
            sel = iota_t == jnp.sum((idx >= c0 + counts).astype(I32), axis=0, keepdims=True)
            pick = lambda v: jnp.sum(jnp.where(sel, v, 0), axis=0, keepdims=True)
            j = idx - pick(c0)
            return pick(slot0), pick(u0), j, pick(n_big), idx < jnp.max(c0 + counts, axis=0, keepdims=True)

        s_b, u_b, j_b, _, ok_b = chunk_rows(lane, n_big)
        s_s, u_s, _, nb_s, ok_s = chunk_rows(lane - SMALL0, n_small)
        small = lane >= SMALL0
        slot = jnp.where(small, s_s + nb_s * BIG, s_b + j_b * BIG)
        local = jnp.where(small, (u_s + 2 * nb_s) * RUN, (u_b + 2 * j_b) * RUN)
        ok = (small & ok_s) | (jnp.logical_not(small) & ok_b)
        counts = jnp.where(lane == TABW - 2, jnp.sum(n_big, axis=0, keepdims=True),
                           jnp.sum(n_small, axis=0, keepdims=True))
        tab_ref[b, 0:1, :] = jnp.where(lane >= TABW - 2, counts, jnp.where(ok, slot, -1))
        tab_ref[b, 1:2, :] = jnp.where(ok, local, 0)
        return carry

    lax.fori_loop(0, nb, block, 0, unroll=2)

    nv = jnp.max(end_t, axis=0, keepdims=True)
    width = te_ref.shape[1]
    ti = jnp.minimum(lax.broadcasted_iota(I32, (N_EXPERTS, width), 1), nv - 1)
    te = jnp.minimum(jnp.sum((ti >= end_t).astype(I32), axis=0, keepdims=True), N_EXPERTS - 1)
    te_ref[...] = te
    nv_ref[...] = jnp.broadcast_to(nv, nv_ref.shape)
    ie = lax.broadcasted_iota(I32, (N_EXPERTS, width), 0)
    own_end = jnp.sum(jnp.where(ie == te, end_t, 0), axis=0, keepdims=True)
    nxt = jnp.minimum(jnp.sum((own_end >= end_t).astype(I32), axis=0, keepdims=True), N_EXPERTS - 1)
    nx_ref[...] = jnp.where(own_end < nv, nxt, -1)


def _route(eidx, lrank, r0, cnt, n_tiles):
    t = eidx.shape[1]
    nb = r0.shape[0]
    width = -(-n_tiles // LANES) * LANES
    return pl.pallas_call(
        _route_kernel,
        out_shape=[jax.ShapeDtypeStruct((TOP_K, t), I32),
                   jax.ShapeDtypeStruct((nb, 2, TABW), I32),
                   jax.ShapeDtypeStruct((1, width), I32),
                   jax.ShapeDtypeStruct((1, LANES), I32),
                   jax.ShapeDtypeStruct((1, width), I32),
                   jax.ShapeDtypeStruct((N_EXPERTS, 1), I32)],
        name="route",
    )(eidx, lrank, r0, cnt)


def _for_chunk_pairs(n, fn):
    def body(i, carry):
        fn(2 * i, 0)

        @pl.when(2 * i + 1 < n)
        def _():
            fn(2 * i + 1, 1)
        return carry
    lax.fori_loop(0, lax.shift_right_logical(n + 1, 1), body, 0)


def _for_block_chunks(tab_ref, blk, fn):
    base = blk * (2 * TABW)
    for first, count_lane, n_rows in ((0, TABW - 2, BIG), (SMALL0, TABW - 1, RUN)):
        def visit(c, parity, first=first, n_rows=n_rows):
            fn(tab_ref[base + first + c], pl.multiple_of(tab_ref[base + TABW + first + c], RUN), n_rows, parity)
        _for_chunk_pairs(tab_ref[base + count_lane], visit)


def _local_rows(ts):
    return ts * TOP_K + N_EXPERTS * RUN


def _dispatch_kernel(tab_ref, pad_ref, nv_ref, h_ref, ls_ref, xs_ref, buf, zbuf, sems, zsem, *, n_alloc):
    b = pl.program_id(0)
    slot = b % 2
    ts = h_ref.shape[0]
    local = buf.shape[2]

    def chunk_copy(sl, slot_row, local_row, n):
        return pltpu.make_async_copy(buf.at[sl, :, pl.ds(local_row, n), :], xs_ref.at[:, pl.ds(slot_row, n), :],
                                     sems.at[sl])

    @pl.when(b == 0)
    def _():
        zbuf[...] = jnp.zeros_like(zbuf)
        zrows = zbuf.shape[1]
        zero = lambda row: pltpu.make_async_copy(zbuf, xs_ref.at[:, pl.ds(row, zrows), :], zsem)
        for phase in range(3):
            for e in range(phase, N_EXPERTS, 3):
                zero(pad_ref[e, 0]).start()
            for e in range(phase, N_EXPERTS, 3):
                zero(pad_ref[e, 0]).wait()
        ztile = lambda i: pltpu.make_async_copy(zbuf.at[:, pl.ds(0, FFN_ROWS), :],
                                                xs_ref.at[:, pl.ds((nv_ref[0, 0] + i) * FFN_ROWS, FFN_ROWS), :], zsem)

        def tail_start(i, carry):
            ztile(i).start()
            return carry

        def tail_wait(i, carry):
            ztile(i).wait()
            return carry
        lax.fori_loop(0, n_alloc - nv_ref[0, 0], tail_start, 0)
        lax.fori_loop(0, n_alloc - nv_ref[0, 0], tail_wait, 0)

    r = lax.broadcasted_iota(I32, (local, ts), 0)
    hit = (r == ls_ref[0:1, :]) | (r == ls_ref[1:2, :]) | (r == ls_ref[2:3, :]) | (r == ls_ref[3:4, :])
    hit = hit.astype(BF16)
    for pb in range(PANELS):
        srt = jnp.dot(hit, h_ref[:, pb * PANEL_COLS:(pb + 1) * PANEL_COLS], preferred_element_type=F32)
        buf[slot, pb] = _pack_panel(srt, exact=True)

    @pl.when(b > 0)
    def _():
        _for_block_chunks(tab_ref, b - 1, lambda s, l, n, p: chunk_copy(1 - slot, s, l, n).wait())

    _for_block_chunks(tab_ref, b, lambda s, l, n, p: chunk_copy(slot, s, l, n).start(priority=p))

    @pl.when(b == pl.num_programs(0) - 1)
    def _():
        _for_block_chunks(tab_ref, b, lambda s, l, n, p: chunk_copy(slot, s, l, n).wait())


def _dispatch(tab, pad, nv, h2, ls, n_alloc):
    t, d = h2.shape
    nb = tab.shape[0] // (2 * TABW)
    ts = t // nb
    return pl.pallas_call(
        functools.partial(_dispatch_kernel, n_alloc=n_alloc),
        grid_spec=pltpu.PrefetchScalarGridSpec(
            num_scalar_prefetch=3,
            grid=(nb,),
            in_specs=[pl.BlockSpec((ts, d), lambda i, *_: (i, 0)),
                      pl.BlockSpec((TOP_K, ts), lambda i, *_: (0, i))],
            out_specs=pl.BlockSpec(memory_space=pl.ANY),
            scratch_shapes=[pltpu.VMEM((2, PANELS, _local_rows(ts), LANES), U32),
                            pltpu.VMEM((PANELS, FFN_ROWS + RUN, LANES), U32),
                            pltpu.SemaphoreType.DMA((2,)), pltpu.SemaphoreType.DMA],
        ),
        out_shape=jax.ShapeDtypeStruct((PANELS, n_alloc * FFN_ROWS, LANES), U32),
        compiler_params=_cparams(1, ROW_VMEM_BYTES),
        name="dispatch",
    )(tab, pad, nv, h2, ls)


def _ffn_kernel(te_ref, nv_ref, nx_ref, xs_ref, wgu_hbm, bgu_ref, wd_hbm, bd_ref, perm_ref, ys_ref,
                wgu_stage, wd_stage, wg_scr, wu_scr, wd_scr, bg_scr, bu_scr, sems):
    p = pl.program_id(0)
    t0 = 2 * p
    t1 = t0 + 1
    e0 = te_ref[0, t0]
    e1 = te_ref[0, t1]
    v0 = t0 < nv_ref[0, 0]
    v1 = t1 < nv_ref[0, 0]
    new0 = (p == 0) | (e0 != te_ref[0, jnp.maximum(t0 - 1, 0)])
    same = v1 & (e1 == e0)

    def stage_copies(e):
        return (pltpu.make_async_copy(wgu_hbm.at[e], wgu_stage, sems.at[0]),
                pltpu.make_async_copy(wd_hbm.at[e], wd_stage, sems.at[1]))

    def load_expert(t, first):
        e = te_ref[0, t]
        if first:
            @pl.when(p == 0)
            def _():
                for cp in stage_copies(e):
                    cp.start()

        for cp in stage_copies(e):
            cp.wait()
        bias = bgu_ref[e]
        for c in range(2 * D_FF // PERM):
            cols = slice(c * PERM, (c + 1) * PERM)
            half = slice(c * (PERM // 2), (c + 1) * (PERM // 2))
            w = wgu_stage[:, cols].astype(BF16)
            pw = jnp.dot(w, perm_ref[...], preferred_element_type=F32).astype(BF16)
            wg_scr[:, half] = pw[:, :PERM // 2]
            wu_scr[:, half] = pw[:, PERM // 2:]
            b1 = bias[:, cols].astype(BF16)
            r1 = bias[:, cols] - b1.astype(F32)
            b2 = r1.astype(BF16)
            b3 = (r1 - b2.astype(F32)).astype(BF16)
            terms = jnp.concatenate([b1, b2, b3, jnp.zeros((5, PERM), BF16)], axis=0)
            pb = jnp.sum(jnp.dot(terms, perm_ref[...], preferred_element_type=F32), axis=0, keepdims=True)
            bg_scr[:, half] = pb[:, :PERM // 2]
            bu_scr[:, half] = pb[:, PERM // 2:]
        wd_scr[...] = wd_stage[...].astype(BF16)

        @pl.when(nx_ref[0, t] >= 0)
        def _():
            for cp in stage_copies(nx_ref[0, t]):
                cp.start()

    def run(lo, n, e):
        x = _unpack_panels([xs_ref[pb, lo:lo + n, :] for pb in range(PANELS)])
        gate = jnp.dot(x, wg_scr[...], preferred_element_type=F32) + bg_scr[...]
        up = jnp.dot(x, wu_scr[...], preferred_element_type=F32) + bu_scr[...]
        gate = jnp.minimum(gate, SWIGLU_LIMIT)
        up = jnp.clip(up, -SWIGLU_LIMIT, SWIGLU_LIMIT)
        act = ((up + 1.0) * (gate * jax.nn.sigmoid(SWIGLU_ALPHA * gate))).astype(BF16)
        bd = bd_ref[e]
        for pb in range(PANELS):
            cols = slice(pb * PANEL_COLS, (pb + 1) * PANEL_COLS)
            y = jnp.dot(act, wd_scr[:, cols], preferred_element_type=F32) + bd[:, cols]
            ys_ref[pb, lo:lo + n, :] = _pack_panel(y)

    @pl.when(v0 & new0)
    def _():
        load_expert(t0, True)

    @pl.when(same)
    def _():
        run(0, 2 * FFN_ROWS, e0)

    @pl.when(v0 & jnp.logical_not(same))
    def _():
        run(0, FFN_ROWS, e0)

    @pl.when(v1 & jnp.logical_not(same))
    def _():
        load_expert(t1, False)
        run(FFN_ROWS, FFN_ROWS, e1)

    @pl.when(v0 & jnp.logical_not(v1))
    def _():
        ys_ref[:, FFN_ROWS:, :] = xs_ref[:, FFN_ROWS:, :]


def _ffn(te, nv, nx, xs, w_gate_up, bgu, w_down, bd, n_tiles):
    d = D_MODEL
    pair = lambda i, te, nv, nx: (0, jnp.minimum(i, lax.shift_right_logical(nv[0, 0] - 1, 1)), 0)
    whole = lambda i, te, nv, nx: (0, 0, 0)
    r = lax.broadcasted_iota(I32, (PERM, PERM), 0)
    c = lax.broadcasted_iota(I32, (PERM, PERM), 1)
    perm = (r == jnp.where(c < PERM // 2, 2 * c, 2 * (c - PERM // 2) + 1)).astype(BF16)
    return pl.pallas_call(
        _ffn_kernel,
        grid_spec=pltpu.PrefetchScalarGridSpec(
            num_scalar_prefetch=3,
            grid=(n_tiles // 2,),
            in_specs=[pl.BlockSpec((PANELS, 2 * FFN_ROWS, LANES), pair),
                      pl.BlockSpec(memory_space=pl.ANY),
                      pl.BlockSpec((N_EXPERTS, 1, 2 * D_FF), whole),
                      pl.BlockSpec(memory_space=pl.ANY),
                      pl.BlockSpec((N_EXPERTS, 1, d), whole),
                      pl.BlockSpec((PERM, PERM), lambda i, te, nv, nx: (0, 0))],
            out_specs=pl.BlockSpec((PANELS, 2 * FFN_ROWS, LANES), pair),
            scratch_shapes=[pltpu.VMEM((d, 2 * D_FF), F32), pltpu.VMEM((D_FF, d), F32),
                            pltpu.VMEM((d, D_FF), BF16), pltpu.VMEM((d, D_FF), BF16), pltpu.VMEM((D_FF, d), BF16),
                            pltpu.VMEM((1, D_FF), F32), pltpu.VMEM((1, D_FF), F32),
                            pltpu.SemaphoreType.DMA((2,))],
        ),
        out_shape=jax.ShapeDtypeStruct(xs.shape, U32),
        input_output_aliases={3: 0},
        compiler_params=_cparams(1, FFN_VMEM_BYTES),
        name="ffn",
    )(te, nv, nx, xs, w_gate_up, bgu, w_down, bd, perm)


def _combine_kernel(tab_ref, x1_ref, ls_ref, w_ref, mod_ref, ys_ref, o4_hbm, ybuf, ob_buf, sems, osems, *, n_steps):
    jj = pl.program_id(1)
    g = pl.program_id(0) * pl.num_programs(1) + jj
    slot = g % 2
    nc = x1_ref.shape[2]
    tt = POST_POS * nc
    d = x1_ref.shape[3]
    local = ybuf.shape[3]
    npos = COMB_SUB * POST_POS

    def for_step_chunks(step, sl, fn):
        for sub in range(COMB_SUB):
            def visit(s, l, n, p, sub=sub):
                fn(pltpu.make_async_copy(ys_ref.at[:, pl.ds(s, n), :], ybuf.at[sl, sub, :, pl.ds(l, n), :],
                                         sems.at[sl]), p)
            _for_block_chunks(tab_ref, step * COMB_SUB + sub, visit)

    @pl.when(g == 0)
    def _():
        ybuf[...] = jnp.zeros_like(ybuf)
        for_step_chunks(0, 0, lambda cp, p: cp.start(priority=p))

    @pl.when(g + 1 < n_steps)
    def _():
        for_step_chunks(g + 1, 1 - slot, lambda cp, p: cp.start(priority=p))

    for_step_chunks(g, slot, lambda cp, p: cp.wait())

    def out_copies(sl, b_, j_):
        return [pltpu.make_async_copy(ob_buf.at[sl, il], o4_hbm.at[b_, :, npos * j_ + il, :], osems.at[sl])
                for il in range(npos)]

    @pl.when(g >= 2)
    def _():
        for cp in out_copies(slot, 0, 0):
            cp.wait()

    r = lax.broadcasted_iota(I32, (tt, local), 1).astype(F32)
    to_cols = lambda a: jnp.concatenate([a, jnp.zeros_like(a)], axis=0).T
    for sub in range(COMB_SUB):
        lanes = slice(sub * tt, (sub + 1) * tt)
        ls_c = to_cols(ls_ref[:, lanes].astype(F32))
        w_c = to_cols(w_ref[:, lanes])
        wm = jnp.zeros((tt, local), F32)
        for k in range(TOP_K):
            wm = jnp.where(r == ls_c[:, k:k + 1], w_c[:, k:k + 1], wm)
        y = _unpack_panels([ybuf[slot, sub, pb] for pb in range(PANELS)])
        acc = jnp.dot(wm.astype(BF16), y, preferred_element_type=F32)
        out = x1_ref[0, sub * POST_POS:(sub + 1) * POST_POS].reshape(tt, d) + mod_ref[0, 5:6, :] * acc
        for il in range(POST_POS):
            ob_buf[slot, sub * POST_POS + il] = out[il * nc:(il + 1) * nc]

    for cp in out_copies(slot, pl.program_id(0), jj):
        cp.start()

    @pl.when(g == n_steps - 1)
    def _():
        for cp in out_copies(slot, 0, 0):
            cp.wait()
        if n_steps > 1:
            for cp in out_copies(1 - slot, 0, 0):
                cp.wait()


def _combine(tab, x1, ls, wts, mod, ys):
    b, _, nc, d = x1.shape
    s = SSM_CHUNK * nc
    tt = POST_POS * nc
    npos = COMB_SUB * POST_POS
    nt = SSM_CHUNK // npos
    o4 = pl.pallas_call(
        functools.partial(_combine_kernel, n_steps=b * nt),
        grid_spec=pltpu.PrefetchScalarGridSpec(
            num_scalar_prefetch=1,
            grid=(b, nt),
            in_specs=[pl.BlockSpec((1, npos, nc, d), lambda bi, j, *_: (bi, j, 0, 0)),
                      pl.BlockSpec((TOP_K, COMB_SUB * tt), lambda bi, j, *_: (0, bi * nt + j)),
                      pl.BlockSpec((TOP_K, COMB_SUB * tt), lambda bi, j, *_: (0, bi * nt + j)),
                      pl.BlockSpec((1, 6, d), lambda bi, j, *_: (bi, 0, 0)),
                      pl.BlockSpec(memory_space=pl.ANY)],
            out_specs=pl.BlockSpec(memory_space=pl.ANY),
            scratch_shapes=[pltpu.VMEM((2, COMB_SUB, PANELS, _local_rows(tt), LANES), U32),
                            pltpu.VMEM((2, npos, nc, d), F32),
                            pltpu.SemaphoreType.DMA((2,)), pltpu.SemaphoreType.DMA((2,))],
        ),
        out_shape=jax.ShapeDtypeStruct((b, nc, SSM_CHUNK, d), F32),
        compiler_params=_cparams(2),
        name="combine",
    )(tab, x1, ls, wts, mod, ys)
    return o4.reshape(b, s, d)


def kernel(x, c, w_ada, b_ada, norm_mix, w_in, b_in, q_norm, k_norm, sinks, lam_re, lam_im, log_dt, b_re, b_im,
           c_re, c_im, d_skip, w_glu, b_glu, attn_out_norm, ssm_out_norm, w_out, norm_ffn, w_router, b_router,
           w_gate_up, b_gate_up, w_down, b_down):
    b, s, d = x.shape
    t = b * s
    depth = w_ada.shape[0]
    n_tiles = -(-(t * TOP_K + N_EXPERTS * (RUN - 1 + FFN_ROWS - 1)) // FFN_ROWS)
    n_tiles += n_tiles % 2
    n_alloc = n_tiles + 2
    for l in range(depth):
        mod, tt, wz, wyt, cs = _prep(c, w_ada[l], b_ada[l], lam_re[l], lam_im[l], log_dt[l], b_re[l], b_im[l],
                                     c_re[l], c_im[l])
        mod = mod.reshape(b, 6, d)
        q, k, v, ut = _inproj(x, mod, norm_mix[l], w_in[l], b_in[l])
        attn = _attention(q, k, v, sinks[l], q_norm[l], k_norm[l], attn_out_norm[l])
        yt = _ssm(ut, tt, wz, wyt, cs, d_skip[l])
        x1, h2, eidx, wts, lrank, r0, cnt = _post(x, attn, yt, mod, w_glu[l], b_glu[l], ssm_out_norm[l], w_out[l],
                                                  norm_ffn[l], w_router[l], b_router[l])
        ls, tab, te, nv, nx, pad = _route(eidx, lrank, r0, cnt, n_tiles)
        tab = tab.reshape(-1)
        xs = _dispatch(tab, pad, nv, h2.reshape(t, d), ls, n_alloc)
        ys = _ffn(te, nv, nx, xs, w_gate_up[l], b_gate_up[l][:, None, :],
                  w_down[l], b_down[l][:, None, :], n_tiles)
        x = _combine(tab, x1, ls, wts, mod, ys)
    return x
```

```python
import functools
import math

import jax
import jax.numpy as jnp
from jax import lax
from jax.experimental import pallas as pl
from jax.experimental.pallas import tpu as pltpu

F32 = jnp.float32
BF16 = jnp.bfloat16
U32 = jnp.uint32
I32 = jnp.int32

D_MODEL = 1024
HEAD_DIM = 64
N_HEADS = 8
N_KV_HEADS = 2
Q_PER_KV = N_HEADS // N_KV_HEADS
D_ATTN = N_HEADS * HEAD_DIM
D_KV = N_KV_HEADS * HEAD_DIM
D_QKV = D_ATTN + 2 * D_KV
WINDOW = 128
BLOCK = 128
D_SSM = D_MODEL - D_ATTN
SSM_GROUP = 16
N_GROUPS = D_SSM // SSM_GROUP
STATE = 64
N_EXPERTS = 32
TOP_K = 4
D_FF = D_MODEL
SWIGLU_LIMIT = 7.0
SWIGLU_ALPHA = 1.702
EPS = 1e-6
NEG_INF = -1e30

LANES = 128
SSM_CHUNK = 16
SSM_ROW = SSM_CHUNK * SSM_GROUP
N_POW = 2 * SSM_CHUNK
PANEL_COLS = 2 * LANES
PANELS = D_MODEL // PANEL_COLS

SSM_GROUPS_PER_STEP = 8
POS_PER_STEP = 8
ATTN_ROWS = 512
POST_POS = 2
POST_SUB = 4
COMB_SUB = 2
FFN_ROWS = 256
RUN = 8
RUN_SHIFT = 3
BIG = 2 * RUN
TABW = 128
SMALL0 = 80
PERM = 256
FFN_VMEM_BYTES = 40 * 1024 * 1024
ROW_VMEM_BYTES = 48 * 1024 * 1024

HIGHEST = lax.Precision.HIGHEST
_ARB = "arbitrary"


def _cparams(n, vmem=None):
    return pltpu.CompilerParams(dimension_semantics=(_ARB,) * n, vmem_limit_bytes=vmem)


def _rms(x, axis=-1):
    return x * lax.rsqrt(jnp.mean(x * x, axis=axis, keepdims=True) + EPS)


def _pack_panel(y, exact=False):
    hi, lo = y[:, :LANES], y[:, LANES:]
    if not exact:
        hi = hi.astype(BF16).astype(F32)
        lo = lo.astype(BF16).astype(F32)
    return lax.bitcast_convert_type(hi, U32) | (lax.bitcast_convert_type(lo, U32) >> 16)


def _unpack_panels(words):
    cols = []
    for w in words:
        cols.append(lax.bitcast_convert_type(w & jnp.uint32(0xFFFF0000), F32).astype(BF16))
        cols.append(lax.bitcast_convert_type(w << 16, F32).astype(BF16))
    return jnp.concatenate(cols, axis=-1)


def _prefetch_pos_rows(x4_hbm, buf, sems, n_pos):
    bi = pl.program_id(0)
    j = pl.program_id(1)
    nj = pl.num_programs(1)
    g = bi * nj + j
    slot = g % 2

    def copies(sl, b_, j_):
        return [pltpu.make_async_copy(x4_hbm.at[b_, :, n_pos * j_ + il, :], buf.at[sl, il], sems.at[sl])
                for il in range(n_pos)]

    @pl.when(g == 0)
    def _():
        for cp in copies(0, 0, 0):
            cp.start()

    @pl.when(g + 1 < pl.num_programs(0) * nj)
    def _():
        wrap = j + 1 == nj
        for cp in copies(1 - slot, jnp.where(wrap, bi + 1, bi), jnp.where(wrap, 0, j + 1)):
            cp.start()

    for cp in copies(slot, bi, j):
        cp.wait()
    return slot


def _to_lane_blocks(dst, src):
    for kb in range(dst.shape[0]):
        dst[kb] = src[:, kb * LANES:(kb + 1) * LANES]


def _adaln_kernel(c_ref, w_ref, b_ref, o_ref):
    c = c_ref[...]
    ca = c * jax.nn.sigmoid(c)
    o_ref[...] = jnp.dot(ca, w_ref[...], preferred_element_type=F32, precision=HIGHEST) + b_ref[...]


def _adaln(c, w_ada, b_ada):
    b, d = c.shape
    n = w_ada.shape[1] // d
    return pl.pallas_call(
        _adaln_kernel,
        grid=(n,),
        in_specs=[pl.BlockSpec((b, d), lambda j: (0, 0)),
                  pl.BlockSpec((d, d), lambda j: (0, j)),
                  pl.BlockSpec((1, d), lambda j: (0, j))],
        out_specs=pl.BlockSpec((b, d), lambda j: (0, j)),
        out_shape=jax.ShapeDtypeStruct((b, n * d), F32),
        compiler_params=_cparams(1),
        name="adaln",
    )(c, w_ada, b_ada.reshape(1, -1))


def _inproj_kernel(x4_hbm, x_ref, mod_ref, g_ref, w_ref, bqkv_ref, but_ref, q_ref, k_ref, v_ref, ut_ref,
                   xp_buf, wqkv_scr, wut_scr, sems):
    nc = ut_ref.shape[3]

    @pl.when((pl.program_id(0) == 0) & (pl.program_id(1) == 0))
    def _():
        wqkv_scr[...] = w_ref[:, :D_QKV].astype(BF16)
        wut_scr[...] = w_ref[:, D_QKV:].T.astype(BF16)

    slot = _prefetch_pos_rows(x4_hbm, xp_buf, sems, POS_PER_STEP)
    gain = g_ref[...]
    scale = 1.0 + mod_ref[0, 1:2, :]
    shift = mod_ref[0, 0:1, :]

    def norm_mod(x):
        return (_rms(x) * gain * scale + shift).astype(BF16)

    proj = jnp.dot(norm_mod(x_ref[0]), wqkv_scr[...], preferred_element_type=F32) + bqkv_ref[...]
    q_ref[0] = proj[:, :D_ATTN].astype(BF16)
    k_ref[0] = proj[:, D_ATTN:D_ATTN + D_KV].astype(BF16)
    v_ref[0] = proj[:, D_ATTN + D_KV:].astype(BF16)

    hs = jnp.concatenate([norm_mod(xp_buf[slot, il]) for il in range(POS_PER_STEP)], axis=0)
    ut = lax.dot_general(wut_scr[...], hs, (((1,), (1,)), ((), ())), preferred_element_type=F32) + but_ref[...]
    for il in range(POS_PER_STEP):
        piece = ut[:, il * nc:(il + 1) * nc].astype(BF16)
        ut_ref[0, :, il * SSM_GROUP:(il + 1) * SSM_GROUP, :] = piece.reshape(N_GROUPS, SSM_GROUP, nc)


def _inproj(x, mod, gain, w_in, b_in):
    b, s, d = x.shape
    nc = s // SSM_CHUNK
    rows = POS_PER_STEP * nc
    row = lambda bi, j: (bi, j, 0)
    const = lambda bi, j: (0, 0)
    return pl.pallas_call(
        _inproj_kernel,
        grid=(b, SSM_CHUNK // POS_PER_STEP),
        in_specs=[pl.BlockSpec(memory_space=pl.ANY),
                  pl.BlockSpec((1, rows, d), row),
                  pl.BlockSpec((1, 6, d), lambda bi, j: (bi, 0, 0)),
                  pl.BlockSpec((1, d), const),
                  pl.BlockSpec((d, D_QKV + D_SSM), const),
                  pl.BlockSpec((1, D_QKV), const),
                  pl.BlockSpec((D_SSM, 1), const)],
        out_specs=[pl.BlockSpec((1, rows, D_ATTN), row),
                   pl.BlockSpec((1, rows, D_KV), row),
                   pl.BlockSpec((1, rows, D_KV), row),
                   pl.BlockSpec((1, N_GROUPS, POS_PER_STEP * SSM_GROUP, nc), lambda bi, j: (bi, 0, j, 0))],
        out_shape=[jax.ShapeDtypeStruct((b, s, D_ATTN), BF16),
                   jax.ShapeDtypeStruct((b, s, D_KV), BF16),
                   jax.ShapeDtypeStruct((b, s, D_KV), BF16),
                   jax.ShapeDtypeStruct((b, N_GROUPS, SSM_ROW, nc), BF16)],
        scratch_shapes=[pltpu.VMEM((2, POS_PER_STEP, nc, d), F32), pltpu.VMEM((d, D_QKV), BF16),
                        pltpu.VMEM((D_SSM, d), BF16), pltpu.SemaphoreType.DMA((2,))],
        compiler_params=_cparams(2, ROW_VMEM_BYTES),
        name="inproj",
    )(x.reshape(b, nc, SSM_CHUNK, d), x, mod, gain.reshape(1, d), w_in, b_in[:D_QKV].reshape(1, D_QKV),
      b_in[D_QKV:].reshape(D_SSM, 1))


def _half_norm(x, low):
    sq = x * x
    s_lo = jnp.sum(jnp.where(low, sq, 0.0), axis=-1, keepdims=True)
    s_hi = jnp.sum(sq, axis=-1, keepdims=True) - s_lo
    inv = 1.0 / HEAD_DIM
    scale = jnp.where(low, lax.rsqrt(s_lo * inv + EPS), lax.rsqrt(s_hi * inv + EPS))
    return x * scale


def _attn_block(first, q, k_prev, k_cur, v_prev, v_cur, sinks_ref, qn, low, upper, rblk):
    no_prev = jnp.where(first, NEG_INF, 0.0)
    out_blocks = []
    for hk in range(N_KV_HEADS):
        qs = []
        for j in range(Q_PER_KV // 2):
            blk = hk * (Q_PER_KV // 2) + j
            qb = _half_norm(q[:, blk * LANES:(blk + 1) * LANES], low) * qn * (1.0 / math.sqrt(HEAD_DIM))
            qs.append(jnp.where(low, qb, 0.0))
            qs.append(jnp.where(low, 0.0, qb))
        qg = jnp.concatenate(qs, axis=0).astype(BF16)
        nt = (((1,), (1,)), ((), ()))
        s_prev = lax.dot_general(qg, k_prev[hk], nt, preferred_element_type=F32)
        s_cur = lax.dot_general(qg, k_cur[hk], nt, preferred_element_type=F32)
        s = jnp.where(upper, s_prev + no_prev, s_cur)
        sink = jnp.zeros((Q_PER_KV * BLOCK, 1), F32)
        for g in range(Q_PER_KV):
            sink = jnp.where(rblk == g, sinks_ref[hk * Q_PER_KV + g], sink)
        m = jnp.maximum(jnp.max(s, axis=-1, keepdims=True), sink)
        p = jnp.exp(s - m)
        den = jnp.sum(p, axis=-1, keepdims=True) + jnp.exp(sink - m)
        o = (jnp.dot(jnp.where(upper, p, 0.0).astype(BF16), v_prev[hk], preferred_element_type=F32)
             + jnp.dot(jnp.where(upper, 0.0, p).astype(BF16), v_cur[hk], preferred_element_type=F32)) / den
        for j in range(Q_PER_KV // 2):
            ev = o[(2 * j) * BLOCK:(2 * j + 1) * BLOCK]
            od = o[(2 * j + 1) * BLOCK:(2 * j + 2) * BLOCK]
            out_blocks.append(jnp.where(low, ev, od))
    return jnp.concatenate(out_blocks, axis=-1)


def _attn_kernel(sinks_ref, q_ref, k_ref, v_ref, qn_ref, kn_ref, on_ref, o_hbm, a_buf, sems, *, n_steps):
    step = pl.program_id(1)
    g = pl.program_id(0) * pl.num_programs(1) + step
    slot = g % 2
    cps = ATTN_ROWS // SSM_CHUNK
    nq = ATTN_ROWS // BLOCK

    def out_copies(sl, b_, s_):
        return [pltpu.make_async_copy(a_buf.at[sl, :, i, :], o_hbm.at[b_, i, pl.ds(s_ * cps, cps), :], sems.at[sl])
                for i in range(SSM_CHUNK)]

    @pl.when(g >= 2)
    def _():
        for cp in out_copies(slot, 0, 0):
            cp.wait()

    low = lax.broadcasted_iota(I32, (1, LANES), 1) < HEAD_DIM
    rows = Q_PER_KV * BLOCK
    upper = lax.broadcasted_iota(I32, (rows, BLOCK), 1) > lax.broadcasted_iota(I32, (rows, BLOCK), 0) % BLOCK
    rblk = lax.broadcasted_iota(I32, (rows, 1), 0) // BLOCK

    cur = pl.multiple_of(step * ATTN_ROWS, ATTN_ROWS)
    prev = pl.multiple_of(jnp.maximum(step * nq - 1, 0) * BLOCK, BLOCK)
    kall = jnp.concatenate([k_ref[0, pl.ds(prev, BLOCK), :], k_ref[0, pl.ds(cur, ATTN_ROWS), :]], axis=0).astype(F32)
    vall = jnp.concatenate([v_ref[0, pl.ds(prev, BLOCK), :], v_ref[0, pl.ds(cur, ATTN_ROWS), :]], axis=0).astype(F32)
    kall = _half_norm(kall, low) * kn_ref[...]
    kswap = pltpu.roll(kall, HEAD_DIM, axis=1)
    vswap = pltpu.roll(vall, HEAD_DIM, axis=1)
    k_dup = [jnp.where(low, kall, kswap).astype(BF16), jnp.where(low, kswap, kall).astype(BF16)]
    v_dup = [jnp.where(low, vall, vswap).astype(BF16), jnp.where(low, vswap, vall).astype(BF16)]
    blk = lambda a, i: [a[hk][i * BLOCK:(i + 1) * BLOCK] for hk in range(N_KV_HEADS)]

    for qb in range(nq):
        q = q_ref[0, qb * BLOCK:(qb + 1) * BLOCK, :].astype(F32)
        attn = _attn_block((step == 0) if qb == 0 else False, q, blk(k_dup, qb), blk(k_dup, qb + 1),
                           blk(v_dup, qb), blk(v_dup, qb + 1), sinks_ref, qn_ref[...], low, upper, rblk)
        attn = _rms(attn) * on_ref[...]
        cpb = BLOCK // SSM_CHUNK
        a_buf[slot, qb * cpb:(qb + 1) * cpb] = attn.reshape(cpb, SSM_CHUNK, D_ATTN)

    for cp in out_copies(slot, pl.program_id(0), step):
        cp.start()

    @pl.when(g == n_steps - 1)
    def _():
        for cp in out_copies(slot, 0, 0):
            cp.wait()
        if n_steps > 1:
            for cp in out_copies(1 - slot, 0, 0):
                cp.wait()


def _attention(q, k, v, sinks, q_norm, k_norm, out_norm):
    b, s, _ = q.shape
    tile2 = lambda g: jnp.tile(g.reshape(1, HEAD_DIM), (1, 2))
    cps = ATTN_ROWS // SSM_CHUNK
    return pl.pallas_call(
        functools.partial(_attn_kernel, n_steps=b * (s // ATTN_ROWS)),
        grid=(b, s // ATTN_ROWS),
        in_specs=[pl.BlockSpec(memory_space=pltpu.SMEM),
                  pl.BlockSpec((1, ATTN_ROWS, D_ATTN), lambda bi, n: (bi, n, 0)),
                  pl.BlockSpec((1, s, D_KV), lambda bi, n: (bi, 0, 0)),
                  pl.BlockSpec((1, s, D_KV), lambda bi, n: (bi, 0, 0)),
                  pl.BlockSpec((1, LANES), lambda bi, n: (0, 0)),
                  pl.BlockSpec((1, LANES), lambda bi, n: (0, 0)),
                  pl.BlockSpec((1, D_ATTN), lambda bi, n: (0, 0))],
        out_specs=pl.BlockSpec(memory_space=pl.ANY),
        out_shape=jax.ShapeDtypeStruct((b, SSM_CHUNK, s // SSM_CHUNK, D_ATTN), F32),
        scratch_shapes=[pltpu.VMEM((2, cps, SSM_CHUNK, D_ATTN), F32), pltpu.SemaphoreType.DMA((2,))],
        compiler_params=_cparams(2),
        name="attention",
    )(sinks, q, k, v, tile2(q_norm), tile2(k_norm), out_norm.reshape(1, D_ATTN))


def _cmul(ar, ai, br, bi):
    return ar * br - ai * bi, ar * bi + ai * br


def _ssm_param_kernel(*refs):
    for gi in range(refs[0].shape[0]):
        _ssm_param_group(gi, *refs)


def _ssm_param_group(gi, lam_ref, bre_ref, bim_ref, cre_ref, cim_ref, tt_ref, wz_ref, wyt_ref, cs_ref):
    f32dot = functools.partial(jnp.dot, preferred_element_type=F32, precision=HIGHEST)
    lr = lam_ref[gi, 0:1, :]
    li = lam_ref[gi, 1:2, :]
    dt = jnp.exp(lam_ref[gi, 2:3, :])
    rho = lr * dt
    th = li * dt
    imag_lane = lax.broadcasted_iota(I32, (1, LANES), 1) >= STATE

    kk = (lax.broadcasted_iota(I32, (N_POW, 1), 0) - (SSM_CHUNK - 1)).astype(F32)
    mag = jnp.exp(rho * kk)
    pw_r = mag * jnp.cos(th * kk)
    pw_i = mag * jnp.sin(th * kk)
    lb_r = pw_r[SSM_CHUNK:SSM_CHUNK + 1]
    lb_i = pw_i[SSM_CHUNK:SSM_CHUNK + 1]
    den = lr * lr + li * li
    coef_r = ((lb_r - 1.0) * lr + lb_i * li) / den
    coef_i = (lb_i * lr - (lb_r - 1.0) * li) / den

    eye = (lax.broadcasted_iota(I32, (SSM_GROUP, SSM_GROUP), 0)
           == lax.broadcasted_iota(I32, (SSM_GROUP, SSM_GROUP), 1)).astype(F32)
    lane_fold = (lax.broadcasted_iota(I32, (STATE, LANES), 1) % STATE
                 == lax.broadcasted_iota(I32, (STATE, LANES), 0)).astype(F32)

    def tile_pos(x):
        return jnp.concatenate([x] * SSM_CHUNK, axis=0)

    def power_rows(k_of_pos):
        idx = [k_of_pos(p) + (SSM_CHUNK - 1) for p in range(SSM_CHUNK)]
        rep = lambda t: jnp.concatenate([jnp.broadcast_to(t[r:r + 1], (SSM_GROUP, LANES)) for r in idx], axis=0)
        return rep(pw_r), rep(pw_i)

    def b_rows(b_ref):
        b2 = jnp.concatenate([b_ref[gi], b_ref[gi]], axis=0)
        return tile_pos(lax.dot_general(eye, b2, (((1,), (1,)), ((), ())), preferred_element_type=F32,
                                        precision=HIGHEST))

    def c_rows(c_ref):
        return tile_pos(f32dot(c_ref[gi], lane_fold))

    bbar_r, bbar_i = _cmul(coef_r, coef_i, b_rows(bre_ref), b_rows(bim_ref))
    c_r = c_rows(cre_ref)
    c_i = c_rows(cim_ref)

    a_r, a_i = _cmul(bbar_r, bbar_i, *power_rows(lambda p: -p))
    a2c = jnp.where(imag_lane, -a_i, a_r)
    m_r, m_i = _cmul(c_r, c_i, *power_rows(lambda p: p))
    bmc = jnp.where(imag_lane, m_i, m_r)
    tt = f32dot(bmc, a2c.T)
    causal = (lax.broadcasted_iota(I32, (SSM_ROW, 1), 0) // SSM_GROUP
              >= lax.broadcasted_iota(I32, (1, SSM_ROW), 1) // SSM_GROUP)
    tt_ref[gi] = jnp.where(causal, tt, 0.0).astype(BF16)

    w_r, w_i = _cmul(bbar_r, bbar_i, *power_rows(lambda p: SSM_CHUNK - 1 - p))
    wz_ref[gi, :, :LANES] = jnp.where(imag_lane, w_i, w_r).astype(BF16)
    wz_ref[gi, :, LANES:] = jnp.where(imag_lane, w_r, w_i).astype(BF16)

    y_r, y_i = _cmul(c_r, c_i, *power_rows(lambda p: p + 1))
    wyt_ref[gi] = jnp.where(imag_lane, -y_i, y_r).astype(BF16)

    cs_ref[gi, 0:1, :] = pw_r[N_POW - 1:N_POW]
    cs_ref[gi, 1:2, :] = jnp.where(imag_lane, pw_i[N_POW - 1:N_POW], -pw_i[N_POW - 1:N_POW])


def _ssm_params(lam_re, lam_im, log_dt, b_re, b_im, c_re, c_im):
    g = lam_re.shape[0]
    lam = jnp.stack([lam_re, lam_im, jnp.broadcast_to(log_dt[:, None], (g, STATE))], axis=1)
    lam = jnp.concatenate([lam, lam], axis=2)
    ng = SSM_GROUPS_PER_STEP
    blk = lambda *shape: pl.BlockSpec((ng,) + shape, lambda i: (i, 0, 0))
    return pl.pallas_call(
        _ssm_param_kernel,
        grid=(g // ng,),
        in_specs=[blk(3, LANES), blk(STATE, SSM_GROUP), blk(STATE, SSM_GROUP), blk(SSM_GROUP, STATE),
                  blk(SSM_GROUP, STATE)],
        out_specs=[blk(SSM_ROW, SSM_ROW), blk(SSM_ROW, SSM_ROW), blk(SSM_ROW, LANES), blk(2, LANES)],
        out_shape=[jax.ShapeDtypeStruct((g, SSM_ROW, SSM_ROW), BF16),
                   jax.ShapeDtypeStruct((g, SSM_ROW, SSM_ROW), BF16),
                   jax.ShapeDtypeStruct((g, SSM_ROW, LANES), BF16),
                   jax.ShapeDtypeStruct((g, 2, LANES), F32)],
        compiler_params=_cparams(1),
        name="ssm_params",
    )(lam, b_re, b_im, c_re, c_im)


def _ssm_kernel(ut_ref, tt_ref, wz_ref, wyt_ref, cs_ref, d_ref, yt_ref, z_scr, s_scr):
    batch, ng, _, nc = ut_ref.shape
    uts = [jnp.concatenate([ut_ref[b, gi] for b in range(batch)], axis=1) for gi in range(ng)]
    for gi in range(ng):
        z = lax.dot_general(uts[gi], wz_ref[gi], (((0,), (0,)), ((), ())), preferred_element_type=F32)
        _to_lane_blocks(z_scr.at[gi], z)
    c1 = [cs_ref[gi, 0:1, :] for gi in range(ng)]
    c2 = [cs_ref[gi, 1:2, :] for gi in range(ng)]

    def step(c, carry):
        rows = pl.ds(c, batch, stride=nc)
        out = []
        for gi in range(ng):
            s1, s2 = carry[gi]
            s_scr[gi, rows, :] = s1
            out.append((c1[gi] * s1 + c2[gi] * s2 + z_scr[gi, 0, rows, :],
                        c1[gi] * s2 - c2[gi] * s1 + z_scr[gi, 1, rows, :]))
        return tuple(out)

    zero = jnp.zeros((batch, LANES), F32)
    lax.fori_loop(0, nc, step, ((zero, zero),) * ng, unroll=8)
    for gi in range(ng):
        y = jnp.dot(tt_ref[gi], uts[gi], preferred_element_type=F32)
        y = y + lax.dot_general(wyt_ref[gi], s_scr[gi].astype(BF16), (((1,), (1,)), ((), ())),
                                preferred_element_type=F32)
        y = y + d_ref[gi] * uts[gi].astype(F32)
        for b in range(batch):
            yt_ref[b, gi] = y[:, b * nc:(b + 1) * nc]


def _ssm(ut, tt, wz, wyt, cs, d_skip):
    b, g, _, nc = ut.shape
    ng = SSM_GROUPS_PER_STEP
    d_col = jnp.tile(d_skip.reshape(g, 1, SSM_GROUP), (1, SSM_CHUNK, 1)).reshape(g, SSM_ROW, 1)
    blk = lambda *shape: pl.BlockSpec((ng,) + shape, lambda i: (i, 0, 0))
    act = pl.BlockSpec((b, ng, SSM_ROW, nc), lambda i: (0, i, 0, 0))
    return pl.pallas_call(
        _ssm_kernel,
        grid=(g // ng,),
        in_specs=[act, blk(SSM_ROW, SSM_ROW), blk(SSM_ROW, SSM_ROW), blk(SSM_ROW, LANES), blk(2, LANES),
                  blk(SSM_ROW, 1)],
        out_specs=act,
        out_shape=jax.ShapeDtypeStruct((b, g, SSM_ROW, nc), F32),
        scratch_shapes=[pltpu.VMEM((ng, SSM_ROW // LANES, b * nc, LANES), F32), pltpu.VMEM((ng, b * nc, LANES), F32)],
        compiler_params=_cparams(1),
        name="ssm",
    )(ut, tt, wz, wyt, cs, d_col)


def _post_kernel(x4_hbm, attn_ref, yt_ref, mod_ref, wglu_ref, bglu_ref, sn_ref, wout_f32_ref, nf_ref, wrt_ref, br_ref,
                 tri_ref, x1_ref, h2_ref, eidx_ref, wts_ref, lrank_ref, r0_ref, cnt_ref, carry_ref, xp_buf,
                 wglut_ref, wout_ref, wr_ref, sems):
    @pl.when((pl.program_id(0) == 0) & (pl.program_id(1) == 0))
    def _():
        carry_ref[...] = jnp.zeros_like(carry_ref)
        wglut_ref[...] = wglu_ref[...].T.astype(BF16)
        wout_ref[...] = wout_f32_ref[...].astype(BF16)
        wr_ref[...] = wrt_ref[...].T.astype(BF16)

    slot = _prefetch_pos_rows(x4_hbm, xp_buf, sems, POST_SUB * POST_POS)
    nc = attn_ref.shape[2]
    ts = POST_POS * nc
    d = x1_ref.shape[3]
    iota_e = lax.broadcasted_iota(I32, (N_EXPERTS, ts), 0).astype(F32)
    counts = []
    for sub in range(POST_SUB):
        pos = range(sub * POST_POS, (sub + 1) * POST_POS)
        lanes = slice(sub * ts, (sub + 1) * ts)
        yt = jnp.concatenate(
            [yt_ref[0, :, il * SSM_GROUP:(il + 1) * SSM_GROUP, :].reshape(D_SSM, nc) for il in pos], axis=1)
        g = jax.nn.gelu(yt)
        gate = jax.nn.sigmoid(jnp.dot(wglut_ref[...], g.astype(BF16), preferred_element_type=F32) + bglu_ref[...])
        ssm_t = _rms(g * gate, axis=0) * sn_ref[...]
        attn = attn_ref[0, sub * POST_POS:(sub + 1) * POST_POS].reshape(ts, D_ATTN)
        mixed = jnp.concatenate([attn.astype(BF16), ssm_t.T.astype(BF16)], axis=-1)
        o = jnp.dot(mixed, wout_ref[...], preferred_element_type=F32)
        x = jnp.concatenate([xp_buf[slot, il] for il in pos], axis=0)
        x1 = x + mod_ref[0, 2:3, :] * o
        x1_ref[0, sub * POST_POS:(sub + 1) * POST_POS] = x1.reshape(POST_POS, nc, d)
        h2 = _rms(x1) * nf_ref[...] * (1.0 + mod_ref[0, 4:5, :]) + mod_ref[0, 3:4, :]
        h2_ref[0, sub * POST_POS:(sub + 1) * POST_POS] = h2.astype(BF16).reshape(POST_POS, nc, d)

        logits = lax.dot_general(wr_ref[...], h2.astype(BF16), (((1,), (1,)), ((), ())),
                                 preferred_element_type=F32) + br_ref[...]
        l = logits
        idxs, vals = [], []
        for _ in range(TOP_K):
            m = jnp.max(l, axis=0, keepdims=True)
            idx = jnp.min(jnp.where(l == m, iota_e, float(N_EXPERTS)), axis=0, keepdims=True)
            idxs.append(idx)
            vals.append(m)
            l = jnp.where(iota_e == idx, -jnp.inf, l)
        es = [jnp.exp(v - vals[0]) for v in vals]
        tot = es[0] + es[1] + es[2] + es[3]
        member = jnp.zeros((N_EXPERTS, ts), F32)
        for idx in idxs:
            member = member + (iota_e == idx).astype(F32)
        before = jnp.dot(member.astype(BF16), tri_ref[...], preferred_element_type=F32)
        for k in range(TOP_K):
            eidx_ref[k:k + 1, lanes] = idxs[k].astype(I32)
            wts_ref[k:k + 1, lanes] = es[k] / tot
            lrank_ref[k:k + 1, lanes] = jnp.sum(jnp.where(iota_e == idxs[k], before, 0.0), axis=0,
                                                keepdims=True).astype(I32)
        counts.append(jnp.sum(member, axis=1, keepdims=True))

    carry = carry_ref[...]
    for sub in range(POST_SUB):
        r0_ref[sub] = carry.astype(I32)
        carry = carry + counts[sub]
    carry_ref[...] = carry
    cnt_ref[...] = carry.astype(I32)


def _post(x, attn, yt, mod, w_glu, b_glu, ssm_norm, w_out, norm_ffn, w_router, b_router):
    b, s, d = x.shape
    nc = s // SSM_CHUNK
    ts = POST_POS * nc
    npos = POST_SUB * POST_POS
    nt = SSM_CHUNK // npos
    t = b * s
    pm = lambda bi, j: (bi, j, 0, 0)
    const = lambda bi, j: (0, 0)
    tok = lambda bi, j: (0, bi * nt + j)
    tri = (lax.broadcasted_iota(I32, (ts, ts), 0) < lax.broadcasted_iota(I32, (ts, ts), 1)).astype(BF16)
    col = lambda a: a.reshape(-1, 1)
    return pl.pallas_call(
        _post_kernel,
        grid=(b, nt),
        in_specs=[pl.BlockSpec(memory_space=pl.ANY),
                  pl.BlockSpec((1, npos, nc, D_ATTN), pm),
                  pl.BlockSpec((1, N_GROUPS, npos * SSM_GROUP, nc), lambda bi, j: (bi, 0, j, 0)),
                  pl.BlockSpec((1, 6, d), lambda bi, j: (bi, 0, 0)),
                  pl.BlockSpec((D_SSM, D_SSM), const),
                  pl.BlockSpec((D_SSM, 1), const),
                  pl.BlockSpec((D_SSM, 1), const),
                  pl.BlockSpec((d, d), const),
                  pl.BlockSpec((1, d), const),
                  pl.BlockSpec((d, N_EXPERTS), const),
                  pl.BlockSpec((N_EXPERTS, 1), const),
                  pl.BlockSpec((ts, ts), const)],
        out_specs=[pl.BlockSpec((1, npos, nc, d), pm),
                   pl.BlockSpec((1, npos, nc, d), pm),
                   pl.BlockSpec((TOP_K, POST_SUB * ts), tok),
                   pl.BlockSpec((TOP_K, POST_SUB * ts), tok),
                   pl.BlockSpec((TOP_K, POST_SUB * ts), tok),
                   pl.BlockSpec((POST_SUB, N_EXPERTS, 1), lambda bi, j: (bi * nt + j, 0, 0)),
                   pl.BlockSpec((N_EXPERTS, 1), const)],
        out_shape=[jax.ShapeDtypeStruct((b, SSM_CHUNK, nc, d), F32),
                   jax.ShapeDtypeStruct((b, SSM_CHUNK, nc, d), BF16),
                   jax.ShapeDtypeStruct((TOP_K, t), I32),
                   jax.ShapeDtypeStruct((TOP_K, t), F32),
                   jax.ShapeDtypeStruct((TOP_K, t), I32),
                   jax.ShapeDtypeStruct((b * nt * POST_SUB, N_EXPERTS, 1), I32),
                   jax.ShapeDtypeStruct((N_EXPERTS, 1), I32)],
        scratch_shapes=[pltpu.VMEM((N_EXPERTS, 1), F32), pltpu.VMEM((2, npos, nc, d), F32),
                        pltpu.VMEM((D_SSM, D_SSM), BF16), pltpu.VMEM((d, d), BF16), pltpu.VMEM((N_EXPERTS, d), BF16),
                        pltpu.SemaphoreType.DMA((2,))],
        compiler_params=_cparams(2, ROW_VMEM_BYTES),
        name="post",
    )(x.reshape(b, nc, SSM_CHUNK, d), attn, yt, mod, w_glu, col(b_glu), col(ssm_norm), w_out,
      norm_ffn.reshape(1, -1), w_router, col(b_router), tri)


def _route_kernel(eidx_ref, lrank_ref, r0_ref, cnt_ref, ls_ref, tab_ref, te_ref, nv_ref, nx_ref, pad_ref):
    cnt = cnt_ref[...]
    tiles = (cnt + (RUN - 1 + FFN_ROWS - 1)) // FFN_ROWS
    er = lax.broadcasted_iota(I32, (N_EXPERTS, N_EXPERTS), 0)
    ec = lax.broadcasted_iota(I32, (N_EXPERTS, N_EXPERTS), 1)
    ltri = (ec < er).astype(BF16)

    def excl_cumsum(v):
        vb = jnp.broadcast_to(v.astype(F32), (N_EXPERTS, LANES)).astype(BF16)
        return jnp.dot(ltri, vb, preferred_element_type=F32)[:, 0:1].astype(I32)

    start_t = excl_cumsum(tiles)
    end_t = start_t + tiles
    start = start_t * FFN_ROWS
    pad_ref[...] = start + cnt

    nb = r0_ref.shape[0]
    ts = eidx_ref.shape[1] // nb
    iota_e = lax.broadcasted_iota(I32, (N_EXPERTS, ts), 0)
    iota_t = lax.broadcasted_iota(I32, (N_EXPERTS, TABW), 0)
    lane = lax.broadcasted_iota(I32, (1, TABW), 1)

    def block(b, carry):
        lanes = pl.ds(pl.multiple_of(b * ts, ts), ts)
        sels = [iota_e == eidx_ref[k:k + 1, lanes] for k in range(TOP_K)]
        member = sels[0].astype(I32) + sels[1].astype(I32) + sels[2].astype(I32) + sels[3].astype(I32)
        units = lax.shift_right_logical(jnp.sum(member, axis=1, keepdims=True) + (RUN - 1), RUN_SHIFT)
        u0 = excl_cumsum(units)
        for k in range(TOP_K):
            first = jnp.sum(jnp.where(sels[k], u0, 0), axis=0, keepdims=True)
            ls_ref[k:k + 1, lanes] = first * RUN + lrank_ref[k:k + 1, lanes]
        n_big = lax.shift_right_logical(units, 1)
        n_small = units & 1
        slot0 = start + r0_ref[b]

        def chunk_rows(idx, counts):
            c0 = excl_cumsum(counts)
            sel = iota_t == jnp.sum((idx >= c0 + counts).astype(I32), axis=0, keepdims=True)
            pick = lambda v: jnp.sum(jnp.where(sel, v, 0), axis=0, keepdims=True)
            j = idx - pick(c0)
            return pick(slot0), pick(u0), j, pick(n_big), idx < jnp.max(c0 + counts, axis=0, keepdims=True)

        s_b, u_b, j_b, _, ok_b = chunk_rows(lane, n_big)
        s_s, u_s, _, nb_s, ok_s = chunk_rows(lane - SMALL0, n_small)
        small = lane >= SMALL0
        slot = jnp.where(small, s_s + nb_s * BIG, s_b + j_b * BIG)
        local = jnp.where(small, (u_s + 2 * nb_s) * RUN, (u_b + 2 * j_b) * RUN)
        ok = (small & ok_s) | (jnp.logical_not(small) & ok_b)
        counts = jnp.where(lane == TABW - 2, jnp.sum(n_big, axis=0, keepdims=True),
                           jnp.sum(n_small, axis=0, keepdims=True))
        tab_ref[b, 0:1, :] = jnp.where(lane >= TABW - 2, counts, jnp.where(ok, slot, -1))
        tab_ref[b, 1:2, :] = jnp.where(ok, local, 0)
        return carry

    lax.fori_loop(0, nb, block, 0, unroll=2)

    nv = jnp.max(end_t, axis=0, keepdims=True)
    width = te_ref.shape[1]
    ti = jnp.minimum(lax.broadcasted_iota(I32, (N_EXPERTS, width), 1), nv - 1)
    te = jnp.minimum(jnp.sum((ti >= end_t).astype(I32), axis=0, keepdims=True), N_EXPERTS - 1)
    te_ref[...] = te
    nv_ref[...] = jnp.broadcast_to(nv, nv_ref.shape)
    ie = lax.broadcasted_iota(I32, (N_EXPERTS, width), 0)
    own_end = jnp.sum(jnp.where(ie == te, end_t, 0), axis=0, keepdims=True)
    nxt = jnp.minimum(jnp.sum((own_end >= end_t).astype(I32), axis=0, keepdims=True), N_EXPERTS - 1)
    nx_ref[...] = jnp.where(own_end < nv, nxt, -1)


def _route(eidx, lrank, r0, cnt, n_tiles):
    t = eidx.shape[1]
    nb = r0.shape[0]
    width = -(-n_tiles // LANES) * LANES
    return pl.pallas_call(
        _route_kernel,
        out_shape=[jax.ShapeDtypeStruct((TOP_K, t), I32),
                   jax.ShapeDtypeStruct((nb, 2, TABW), I32),
                   jax.ShapeDtypeStruct((1, width), I32),
                   jax.ShapeDtypeStruct((1, LANES), I32),
                   jax.ShapeDtypeStruct((1, width), I32),
                   jax.ShapeDtypeStruct((N_EXPERTS, 1), I32)],
        name="route",
    )(eidx, lrank, r0, cnt)


def _for_chunk_pairs(n, fn):
    def body(i, carry):
        fn(2 * i, 0)

        @pl.when(2 * i + 1 < n)
        def _():
            fn(2 * i + 1, 1)
        return carry
    lax.fori_loop(0, lax.shift_right_logical(n + 1, 1), body, 0)


def _for_block_chunks(tab_ref, blk, fn):
    base = blk * (2 * TABW)
    for first, count_lane, n_rows in ((0, TABW - 2, BIG), (SMALL0, TABW - 1, RUN)):
        def visit(c, parity, first=first, n_rows=n_rows):
            fn(tab_ref[base + first + c], pl.multiple_of(tab_ref[base + TABW + first + c], RUN), n_rows, parity)
        _for_chunk_pairs(tab_ref[base + count_lane], visit)


def _local_rows(ts):
    return ts * TOP_K + N_EXPERTS * RUN


def _dispatch_kernel(tab_ref, pad_ref, nv_ref, h_ref, ls_ref, xs_ref, buf, zbuf, sems, zsem, *, n_alloc):
    b = pl.program_id(0)
    slot = b % 2
    ts = h_ref.shape[0]
    local = buf.shape[2]

    def chunk_copy(sl, slot_row, local_row, n):
        return pltpu.make_async_copy(buf.at[sl, :, pl.ds(local_row, n), :], xs_ref.at[:, pl.ds(slot_row, n), :],
                                     sems.at[sl])

    @pl.when(b == 0)
    def _():
        zbuf[...] = jnp.zeros_like(zbuf)
        zrows = zbuf.shape[1]
        zero = lambda row: pltpu.make_async_copy(zbuf, xs_ref.at[:, pl.ds(row, zrows), :], zsem)
        for phase in range(3):
            for e in range(phase, N_EXPERTS, 3):
                zero(pad_ref[e, 0]).start()
            for e in range(phase, N_EXPERTS, 3):
                zero(pad_ref[e, 0]).wait()
        ztile = lambda i: pltpu.make_async_copy(zbuf.at[:, pl.ds(0, FFN_ROWS), :],
                                                xs_ref.at[:, pl.ds((nv_ref[0, 0] + i) * FFN_ROWS, FFN_ROWS), :], zsem)

        def tail_start(i, carry):
            ztile(i).start()
            return carry

        def tail_wait(i, carry):
            ztile(i).wait()
            return carry
        lax.fori_loop(0, n_alloc - nv_ref[0, 0], tail_start, 0)
        lax.fori_loop(0, n_alloc - nv_ref[0, 0], tail_wait, 0)

    r = lax.broadcasted_iota(I32, (local, ts), 0)
    hit = (r == ls_ref[0:1, :]) | (r == ls_ref[1:2, :]) | (r == ls_ref[2:3, :]) | (r == ls_ref[3:4, :])
    hit = hit.astype(BF16)
    for pb in range(PANELS):
        srt = jnp.dot(hit, h_ref[:, pb * PANEL_COLS:(pb + 1) * PANEL_COLS], preferred_element_type=F32)
        buf[slot, pb] = _pack_panel(srt, exact=True)

    @pl.when(b > 0)
    def _():
        _for_block_chunks(tab_ref, b - 1, lambda s, l, n, p: chunk_copy(1 - slot, s, l, n).wait())

    _for_block_chunks(tab_ref, b, lambda s, l, n, p: chunk_copy(slot, s, l, n).start(priority=p))

    @pl.when(b == pl.num_programs(0) - 1)
    def _():
        _for_block_chunks(tab_ref, b, lambda s, l, n, p: chunk_copy(slot, s, l, n).wait())


def _dispatch(tab, pad, nv, h2, ls, n_alloc):
    t, d = h2.shape
    nb = tab.shape[0] // (2 * TABW)
    ts = t // nb
    return pl.pallas_call(
        functools.partial(_dispatch_kernel, n_alloc=n_alloc),
        grid_spec=pltpu.PrefetchScalarGridSpec(
            num_scalar_prefetch=3,
            grid=(nb,),
            in_specs=[pl.BlockSpec((ts, d), lambda i, *_: (i, 0)),
                      pl.BlockSpec((TOP_K, ts), lambda i, *_: (0, i))],
            out_specs=pl.BlockSpec(memory_space=pl.ANY),
            scratch_shapes=[pltpu.VMEM((2, PANELS, _local_rows(ts), LANES), U32),
                            pltpu.VMEM((PANELS, FFN_ROWS + RUN, LANES), U32),
                            pltpu.SemaphoreType.DMA((2,)), pltpu.SemaphoreType.DMA],
        ),
        out_shape=jax.ShapeDtypeStruct((PANELS, n_alloc * FFN_ROWS, LANES), U32),
        compiler_params=_cparams(1, ROW_VMEM_BYTES),
        name="dispatch",
    )(tab, pad, nv, h2, ls)


def _ffn_kernel(te_ref, nv_ref, nx_ref, xs_ref, wgu_hbm, bgu_ref, wd_hbm, bd_ref, perm_ref, ys_ref,
                wgu_stage, wd_stage, wg_scr, wu_scr, wd_scr, bg_scr, bu_scr, sems):
    p = pl.program_id(0)
    t0 = 2 * p
    t1 = t0 + 1
    e0 = te_ref[0, t0]
    e1 = te_ref[0, t1]
    v0 = t0 < nv_ref[0, 0]
    v1 = t1 < nv_ref[0, 0]
    new0 = (p == 0) | (e0 != te_ref[0, jnp.maximum(t0 - 1, 0)])
    same = v1 & (e1 == e0)

    def stage_copies(e):
        return (pltpu.make_async_copy(wgu_hbm.at[e], wgu_stage, sems.at[0]),
                pltpu.make_async_copy(wd_hbm.at[e], wd_stage, sems.at[1]))

    def load_expert(t, first):
        e = te_ref[0, t]
        if first:
            @pl.when(p == 0)
            def _():
                for cp in stage_copies(e):
                    cp.start()

        for cp in stage_copies(e):
            cp.wait()
        bias = bgu_ref[e]
        for c in range(2 * D_FF // PERM):
            cols = slice(c * PERM, (c + 1) * PERM)
            half = slice(c * (PERM // 2), (c + 1) * (PERM // 2))
            w = wgu_stage[:, cols].astype(BF16)
            pw = jnp.dot(w, perm_ref[...], preferred_element_type=F32).astype(BF16)
            wg_scr[:, half] = pw[:, :PERM // 2]
            wu_scr[:, half] = pw[:, PERM // 2:]
            b1 = bias[:, cols].astype(BF16)
            r1 = bias[:, cols] - b1.astype(F32)
            b2 = r1.astype(BF16)
            b3 = (r1 - b2.astype(F32)).astype(BF16)
            terms = jnp.concatenate([b1, b2, b3, jnp.zeros((5, PERM), BF16)], axis=0)
            pb = jnp.sum(jnp.dot(terms, perm_ref[...], preferred_element_type=F32), axis=0, keepdims=True)
            bg_scr[:, half] = pb[:, :PERM // 2]
            bu_scr[:, half] = pb[:, PERM // 2:]

        @pl.when(nx_ref[0, t] >= 0)
        def _():
            stage_copies(nx_ref[0, t])[0].start()

    def run(lo, n, e, first_of=None):
        if first_of is not None:
            wd_scr[...] = wd_stage[...].astype(BF16)
        x = _unpack_panels([xs_ref[pb, lo:lo + n, :] for pb in range(PANELS)])
        gate = jnp.dot(x, wg_scr[...], preferred_element_type=F32) + bg_scr[...]
        up = jnp.dot(x, wu_scr[...], preferred_element_type=F32) + bu_scr[...]
        gate = jnp.minimum(gate, SWIGLU_LIMIT)
        up = jnp.clip(up, -SWIGLU_LIMIT, SWIGLU_LIMIT)
        act = ((up + 1.0) * (gate * jax.nn.sigmoid(SWIGLU_ALPHA * gate))).astype(BF16)
        bd = bd_ref[e]
        for pb in range(PANELS):
            cols = slice(pb * PANEL_COLS, (pb + 1) * PANEL_COLS)
            y = jnp.dot(act, wd_scr[:, cols], preferred_element_type=F32) + bd[:, cols]
            ys_ref[pb, lo:lo + n, :] = _pack_panel(y)
        if first_of is not None:
            @pl.when(nx_ref[0, first_of] >= 0)
            def _():
                stage_copies(nx_ref[0, first_of])[1].start()

    old0 = jnp.logical_not(new0)
    single = v0 & jnp.logical_not(same)

    @pl.when(v0 & new0)
    def _():
        load_expert(t0, True)

    @pl.when(same & new0)
    def _():
        run(0, 2 * FFN_ROWS, e0, first_of=t0)

    @pl.when(same & old0)
    def _():
        run(0, 2 * FFN_ROWS, e0)

    @pl.when(single & new0)
    def _():
        run(0, FFN_ROWS, e0, first_of=t0)

    @pl.when(single & old0)
    def _():
        run(0, FFN_ROWS, e0)

    @pl.when(v1 & jnp.logical_not(same))
    def _():
        load_expert(t1, False)
        run(FFN_ROWS, FFN_ROWS, e1, first_of=t1)

    @pl.when(v0 & jnp.logical_not(v1))
    def _():
        ys_ref[:, FFN_ROWS:, :] = xs_ref[:, FFN_ROWS:, :]


def _ffn(te, nv, nx, xs, w_gate_up, bgu, w_down, bd, n_tiles):
    d = D_MODEL
    pair = lambda i, te, nv, nx: (0, jnp.minimum(i, lax.shift_right_logical(nv[0, 0] - 1, 1)), 0)
    whole = lambda i, te, nv, nx: (0, 0, 0)
    r = lax.broadcasted_iota(I32, (PERM, PERM), 0)
    c = lax.broadcasted_iota(I32, (PERM, PERM), 1)
    perm = (r == jnp.where(c < PERM // 2, 2 * c, 2 * (c - PERM // 2) + 1)).astype(BF16)
    return pl.pallas_call(
        _ffn_kernel,
        grid_spec=pltpu.PrefetchScalarGridSpec(
            num_scalar_prefetch=3,
            grid=(n_tiles // 2,),
            in_specs=[pl.BlockSpec((PANELS, 2 * FFN_ROWS, LANES), pair),
                      pl.BlockSpec(memory_space=pl.ANY),
                      pl.BlockSpec((N_EXPERTS, 1, 2 * D_FF), whole),
                      pl.BlockSpec(memory_space=pl.ANY),
                      pl.BlockSpec((N_EXPERTS, 1, d), whole),
                      pl.BlockSpec((PERM, PERM), lambda i, te, nv, nx: (0, 0))],
            out_specs=pl.BlockSpec((PANELS, 2 * FFN_ROWS, LANES), pair),
            scratch_shapes=[pltpu.VMEM((d, 2 * D_FF), F32), pltpu.VMEM((D_FF, d), F32),
                            pltpu.VMEM((d, D_FF), BF16), pltpu.VMEM((d, D_FF), BF16), pltpu.VMEM((D_FF, d), BF16),
                            pltpu.VMEM((1, D_FF), F32), pltpu.VMEM((1, D_FF), F32),
                            pltpu.SemaphoreType.DMA((2,))],
        ),
        out_shape=jax.ShapeDtypeStruct(xs.shape, U32),
        input_output_aliases={3: 0},
        compiler_params=_cparams(1, FFN_VMEM_BYTES),
        name="ffn",
    )(te, nv, nx, xs, w_gate_up, bgu, w_down, bd, perm)


def _combine_kernel(tab_ref, x1_ref, ls_ref, w_ref, mod_ref, ys_ref, o4_hbm, ybuf, ob_buf, sems, osems, *, n_steps):
    jj = pl.program_id(1)
    g = pl.program_id(0) * pl.num_programs(1) + jj
    slot = g % 2
    nc = x1_ref.shape[2]
    tt = POST_POS * nc
    d = x1_ref.shape[3]
    local = ybuf.shape[3]
    npos = COMB_SUB * POST_POS

    def for_step_chunks(step, sl, fn):
        for sub in range(COMB_SUB):
            def visit(s, l, n, p, sub=sub):
                fn(pltpu.make_async_copy(ys_ref.at[:, pl.ds(s, n), :], ybuf.at[sl, sub, :, pl.ds(l, n), :],
                                         sems.at[sl]), p)
            _for_block_chunks(tab_ref, step * COMB_SUB + sub, visit)

    @pl.when(g == 0)
    def _():
        ybuf[...] = jnp.zeros_like(ybuf)
        for_step_chunks(0, 0, lambda cp, p: cp.start(priority=p))

    @pl.when(g + 1 < n_steps)
    def _():
        for_step_chunks(g + 1, 1 - slot, lambda cp, p: cp.start(priority=p))

    for_step_chunks(g, slot, lambda cp, p: cp.wait())

    def out_copies(sl, b_, j_):
        return [pltpu.make_async_copy(ob_buf.at[sl, il], o4_hbm.at[b_, :, npos * j_ + il, :], osems.at[sl])
                for il in range(npos)]

    @pl.when(g >= 2)
    def _():
        for cp in out_copies(slot, 0, 0):
            cp.wait()

    r = lax.broadcasted_iota(I32, (tt, local), 1).astype(F32)
    to_cols = lambda a: jnp.concatenate([a, jnp.zeros_like(a)], axis=0).T
    for sub in range(COMB_SUB):
        lanes = slice(sub * tt, (sub + 1) * tt)
        ls_c = to_cols(ls_ref[:, lanes].astype(F32))
        w_c = to_cols(w_ref[:, lanes])
        wm = jnp.zeros((tt, local), F32)
        for k in range(TOP_K):
            wm = jnp.where(r == ls_c[:, k:k + 1], w_c[:, k:k + 1], wm)
        y = _unpack_panels([ybuf[slot, sub, pb] for pb in range(PANELS)])
        acc = jnp.dot(wm.astype(BF16), y, preferred_element_type=F32)
        out = x1_ref[0, sub * POST_POS:(sub + 1) * POST_POS].reshape(tt, d) + mod_ref[0, 5:6, :] * acc
        for il in range(POST_POS):
            ob_buf[slot, sub * POST_POS + il] = out[il * nc:(il + 1) * nc]

    for cp in out_copies(slot, pl.program_id(0), jj):
        cp.start()

    @pl.when(g == n_steps - 1)
    def _():
        for cp in out_copies(slot, 0, 0):
            cp.wait()
        if n_steps > 1:
            for cp in out_copies(1 - slot, 0, 0):
                cp.wait()


def _combine(tab, x1, ls, wts, mod, ys):
    b, _, nc, d = x1.shape
    s = SSM_CHUNK * nc
    tt = POST_POS * nc
    npos = COMB_SUB * POST_POS
    nt = SSM_CHUNK // npos
    o4 = pl.pallas_call(
        functools.partial(_combine_kernel, n_steps=b * nt),
        grid_spec=pltpu.PrefetchScalarGridSpec(
            num_scalar_prefetch=1,
            grid=(b, nt),
            in_specs=[pl.BlockSpec((1, npos, nc, d), lambda bi, j, *_: (bi, j, 0, 0)),
                      pl.BlockSpec((TOP_K, COMB_SUB * tt), lambda bi, j, *_: (0, bi * nt + j)),
                      pl.BlockSpec((TOP_K, COMB_SUB * tt), lambda bi, j, *_: (0, bi * nt + j)),
                      pl.BlockSpec((1, 6, d), lambda bi, j, *_: (bi, 0, 0)),
                      pl.BlockSpec(memory_space=pl.ANY)],
            out_specs=pl.BlockSpec(memory_space=pl.ANY),
            scratch_shapes=[pltpu.VMEM((2, COMB_SUB, PANELS, _local_rows(tt), LANES), U32),
                            pltpu.VMEM((2, npos, nc, d), F32),
                            pltpu.SemaphoreType.DMA((2,)), pltpu.SemaphoreType.DMA((2,))],
        ),
        out_shape=jax.ShapeDtypeStruct((b, nc, SSM_CHUNK, d), F32),
        compiler_params=_cparams(2),
        name="combine",
    )(tab, x1, ls, wts, mod, ys)
    return o4.reshape(b, s, d)


def kernel(x, c, w_ada, b_ada, norm_mix, w_in, b_in, q_norm, k_norm, sinks, lam_re, lam_im, log_dt, b_re, b_im,
           c_re, c_im, d_skip, w_glu, b_glu, attn_out_norm, ssm_out_norm, w_out, norm_ffn, w_router, b_router,
           w_gate_up, b_gate_up, w_down, b_down):
    b, s, d = x.shape
    t = b * s
    depth = w_ada.shape[0]
    n_tiles = -(-(t * TOP_K + N_EXPERTS * (RUN - 1 + FFN_ROWS - 1)) // FFN_ROWS)
    n_tiles += n_tiles % 2
    n_alloc = n_tiles + 2
    for l in range(depth):
        mod = _adaln(c, w_ada[l], b_ada[l]).reshape(b, 6, d)
        q, k, v, ut = _inproj(x, mod, norm_mix[l], w_in[l], b_in[l])
        attn = _attention(q, k, v, sinks[l], q_norm[l], k_norm[l], attn_out_norm[l])
        tt, wz, wyt, cs = _ssm_params(lam_re[l], lam_im[l], log_dt[l], b_re[l], b_im[l], c_re[l], c_im[l])
        yt = _ssm(ut, tt, wz, wyt, cs, d_skip[l])
        x1, h2, eidx, wts, lrank, r0, cnt = _post(x, attn, yt, mod, w_glu[l], b_glu[l], ssm_out_norm[l], w_out[l],
                                                  norm_ffn[l], w_router[l], b_router[l])
        ls, tab, te, nv, nx, pad = _route(eidx, lrank, r0, cnt, n_tiles)
        tab = tab.reshape(-1)
        xs = _dispatch(tab, pad, nv, h2.reshape(t, d), ls, n_alloc)
        ys = _ffn(te, nv, nx, xs, w_gate_up[l], b_gate_up[l][:, None, :],
                  w_down[l], b_down[l][:, None, :], n_tiles)
        x = _combine(tab, x1, ls, wts, mod, ys)
    return x
```

```python
import functools
import math

import jax
import jax.numpy as jnp
from jax import lax
from jax.experimental import pallas as pl
from jax.experimental.pallas import tpu as pltpu

F32 = jnp.float32
BF16 = jnp.bfloat16
U32 = jnp.uint32
I32 = jnp.int32

D_MODEL = 1024
HEAD_DIM = 64
N_HEADS = 8
N_KV_HEADS = 2
Q_PER_KV = N_HEADS // N_KV_HEADS
D_ATTN = N_HEADS * HEAD_DIM
D_KV = N_KV_HEADS * HEAD_DIM
D_QKV = D_ATTN + 2 * D_KV
WINDOW = 128
BLOCK = 128
D_SSM = D_MODEL - D_ATTN
SSM_GROUP = 16
N_GROUPS = D_SSM // SSM_GROUP
STATE = 64
N_EXPERTS = 32
TOP_K = 4
D_FF = D_MODEL
SWIGLU_LIMIT = 7.0
SWIGLU_ALPHA = 1.702
EPS = 1e-6
NEG_INF = -1e30

LANES = 128
SSM_CHUNK = 16
SSM_ROW = SSM_CHUNK * SSM_GROUP
N_POW = 2 * SSM_CHUNK
PANEL_COLS = 2 * LANES
PANELS = D_MODEL // PANEL_COLS

SSM_GROUPS_PER_STEP = 8
POS_PER_STEP = 8
ATTN_ROWS = 512
POST_POS = 2
POST_SUB = 4
COMB_SUB = 2
FFN_ROWS = 256
RUN = 8
RUN_SHIFT = 3
BIG = 2 * RUN
TABW = 128
SMALL0 = 80
PERM = 256
FFN_VMEM_BYTES = 40 * 1024 * 1024
ROW_VMEM_BYTES = 48 * 1024 * 1024

HIGHEST = lax.Precision.HIGHEST
_ARB = "arbitrary"


def _cparams(n, vmem=None):
    return pltpu.CompilerParams(dimension_semantics=(_ARB,) * n, vmem_limit_bytes=vmem)


def _rms(x, axis=-1):
    return x * lax.rsqrt(jnp.mean(x * x, axis=axis, keepdims=True) + EPS)


def _pack_panel(y, exact=False):
    hi, lo = y[:, :LANES], y[:, LANES:]
    if not exact:
        hi = hi.astype(BF16).astype(F32)
        lo = lo.astype(BF16).astype(F32)
    return lax.bitcast_convert_type(hi, U32) | (lax.bitcast_convert_type(lo, U32) >> 16)


def _unpack_panels(words):
    cols = []
    for w in words:
        cols.append(lax.bitcast_convert_type(w & jnp.uint32(0xFFFF0000), F32).astype(BF16))
        cols.append(lax.bitcast_convert_type(w << 16, F32).astype(BF16))
    return jnp.concatenate(cols, axis=-1)


def _prefetch_pos_rows(x4_hbm, buf, sems, n_pos):
    bi = pl.program_id(0)
    j = pl.program_id(1)
    nj = pl.num_programs(1)
    g = bi * nj + j
    slot = g % 2

    def copies(sl, b_, j_):
        return [pltpu.make_async_copy(x4_hbm.at[b_, :, n_pos * j_ + il, :], buf.at[sl, il], sems.at[sl])
                for il in range(n_pos)]

    @pl.when(g == 0)
    def _():
        for cp in copies(0, 0, 0):
            cp.start()

    @pl.when(g + 1 < pl.num_programs(0) * nj)
    def _():
        wrap = j + 1 == nj
        for cp in copies(1 - slot, jnp.where(wrap, bi + 1, bi), jnp.where(wrap, 0, j + 1)):
            cp.start()

    for cp in copies(slot, bi, j):
        cp.wait()
    return slot


def _to_lane_blocks(dst, src):
    for kb in range(dst.shape[0]):
        dst[kb] = src[:, kb * LANES:(kb + 1) * LANES]


def _adaln_kernel(c_ref, w_ref, b_ref, o_ref):
    c = c_ref[...]
    ca = c * jax.nn.sigmoid(c)
    o_ref[...] = jnp.dot(ca, w_ref[...], preferred_element_type=F32, precision=HIGHEST) + b_ref[...]


def _adaln(c, w_ada, b_ada):
    b, d = c.shape
    n = w_ada.shape[1] // d
    return pl.pallas_call(
        _adaln_kernel,
        grid=(n,),
        in_specs=[pl.BlockSpec((b, d), lambda j: (0, 0)),
                  pl.BlockSpec((d, d), lambda j: (0, j)),
                  pl.BlockSpec((1, d), lambda j: (0, j))],
        out_specs=pl.BlockSpec((b, d), lambda j: (0, j)),
        out_shape=jax.ShapeDtypeStruct((b, n * d), F32),
        compiler_params=_cparams(1),
        name="adaln",
    )(c, w_ada, b_ada.reshape(1, -1))


def _inproj_kernel(x4_hbm, x_ref, mod_ref, g_ref, w_ref, bqkv_ref, but_ref, q_ref, k_ref, v_ref, ut_ref,
                   xp_buf, wqkv_scr, wut_scr, sems):
    nc = ut_ref.shape[3]

    @pl.when((pl.program_id(0) == 0) & (pl.program_id(1) == 0))
    def _():
        wqkv_scr[...] = w_ref[:, :D_QKV].astype(BF16)
        wut_scr[...] = w_ref[:, D_QKV:].T.astype(BF16)

    slot = _prefetch_pos_rows(x4_hbm, xp_buf, sems, POS_PER_STEP)
    gain = g_ref[...]
    scale = 1.0 + mod_ref[0, 1:2, :]
    shift = mod_ref[0, 0:1, :]

    def norm_mod(x):
        return (_rms(x) * gain * scale + shift).astype(BF16)

    proj = jnp.dot(norm_mod(x_ref[0]), wqkv_scr[...], preferred_element_type=F32) + bqkv_ref[...]
    q_ref[0] = proj[:, :D_ATTN].astype(BF16)
    k_ref[0] = proj[:, D_ATTN:D_ATTN + D_KV].astype(BF16)
    v_ref[0] = proj[:, D_ATTN + D_KV:].astype(BF16)

    hs = jnp.concatenate([norm_mod(xp_buf[slot, il]) for il in range(POS_PER_STEP)], axis=0)
    ut = lax.dot_general(wut_scr[...], hs, (((1,), (1,)), ((), ())), preferred_element_type=F32) + but_ref[...]
    for il in range(POS_PER_STEP):
        piece = ut[:, il * nc:(il + 1) * nc].astype(BF16)
        ut_ref[0, :, il * SSM_GROUP:(il + 1) * SSM_GROUP, :] = piece.reshape(N_GROUPS, SSM_GROUP, nc)


def _inproj(x, mod, gain, w_in, b_in):
    b, s, d = x.shape
    nc = s // SSM_CHUNK
    rows = POS_PER_STEP * nc
    row = lambda bi, j: (bi, j, 0)
    const = lambda bi, j: (0, 0)
    return pl.pallas_call(
        _inproj_kernel,
        grid=(b, SSM_CHUNK // POS_PER_STEP),
        in_specs=[pl.BlockSpec(memory_space=pl.ANY),
                  pl.BlockSpec((1, rows, d), row),
                  pl.BlockSpec((1, 6, d), lambda bi, j: (bi, 0, 0)),
                  pl.BlockSpec((1, d), const),
                  pl.BlockSpec((d, D_QKV + D_SSM), const),
                  pl.BlockSpec((1, D_QKV), const),
                  pl.BlockSpec((D_SSM, 1), const)],
        out_specs=[pl.BlockSpec((1, rows, D_ATTN), row),
                   pl.BlockSpec((1, rows, D_KV), row),
                   pl.BlockSpec((1, rows, D_KV), row),
                   pl.BlockSpec((1, N_GROUPS, POS_PER_STEP * SSM_GROUP, nc), lambda bi, j: (bi, 0, j, 0))],
        out_shape=[jax.ShapeDtypeStruct((b, s, D_ATTN), BF16),
                   jax.ShapeDtypeStruct((b, s, D_KV), BF16),
                   jax.ShapeDtypeStruct((b, s, D_KV), BF16),
                   jax.ShapeDtypeStruct((b, N_GROUPS, SSM_ROW, nc), BF16)],
        scratch_shapes=[pltpu.VMEM((2, POS_PER_STEP, nc, d), F32), pltpu.VMEM((d, D_QKV), BF16),
                        pltpu.VMEM((D_SSM, d), BF16), pltpu.SemaphoreType.DMA((2,))],
        compiler_params=_cparams(2, ROW_VMEM_BYTES),
        name="inproj",
    )(x.reshape(b, nc, SSM_CHUNK, d), x, mod, gain.reshape(1, d), w_in, b_in[:D_QKV].reshape(1, D_QKV),
      b_in[D_QKV:].reshape(D_SSM, 1))


def _half_norm(x, low):
    sq = x * x
    s_lo = jnp.sum(jnp.where(low, sq, 0.0), axis=-1, keepdims=True)
    s_hi = jnp.sum(sq, axis=-1, keepdims=True) - s_lo
    inv = 1.0 / HEAD_DIM
    scale = jnp.where(low, lax.rsqrt(s_lo * inv + EPS), lax.rsqrt(s_hi * inv + EPS))
    return x * scale


def _attn_block(first, q, k_prev, k_cur, v_prev, v_cur, sinks_ref, qn, low, upper, rblk):
    no_prev = jnp.where(first, NEG_INF, 0.0)
    out_blocks = []
    for hk in range(N_KV_HEADS):
        qs = []
        for j in range(Q_PER_KV // 2):
            blk = hk * (Q_PER_KV // 2) + j
            qb = _half_norm(q[:, blk * LANES:(blk + 1) * LANES], low) * qn * (1.0 / math.sqrt(HEAD_DIM))
            qs.append(jnp.where(low, qb, 0.0))
            qs.append(jnp.where(low, 0.0, qb))
        qg = jnp.concatenate(qs, axis=0).astype(BF16)
        nt = (((1,), (1,)), ((), ()))
        s_prev = lax.dot_general(qg, k_prev[hk], nt, preferred_element_type=F32)
        s_cur = lax.dot_general(qg, k_cur[hk], nt, preferred_element_type=F32)
        s = jnp.where(upper, s_prev + no_prev, s_cur)
        sink = jnp.zeros((Q_PER_KV * BLOCK, 1), F32)
        for g in range(Q_PER_KV):
            sink = jnp.where(rblk == g, sinks_ref[hk * Q_PER_KV + g], sink)
        m = jnp.maximum(jnp.max(s, axis=-1, keepdims=True), sink)
        p = jnp.exp(s - m)
        den = jnp.sum(p, axis=-1, keepdims=True) + jnp.exp(sink - m)
        o = (jnp.dot(jnp.where(upper, p, 0.0).astype(BF16), v_prev[hk], preferred_element_type=F32)
             + jnp.dot(jnp.where(upper, 0.0, p).astype(BF16), v_cur[hk], preferred_element_type=F32)) / den
        for j in range(Q_PER_KV // 2):
            ev = o[(2 * j) * BLOCK:(2 * j + 1) * BLOCK]
            od = o[(2 * j + 1) * BLOCK:(2 * j + 2) * BLOCK]
            out_blocks.append(jnp.where(low, ev, od))
    return jnp.concatenate(out_blocks, axis=-1)


def _attn_kernel(sinks_ref, q_ref, k_ref, v_ref, qn_ref, kn_ref, on_ref, o_hbm, a_buf, sems, *, n_steps):
    step = pl.program_id(1)
    g = pl.program_id(0) * pl.num_programs(1) + step
    slot = g % 2
    cps = ATTN_ROWS // SSM_CHUNK
    nq = ATTN_ROWS // BLOCK

    def out_copies(sl, b_, s_):
        return [pltpu.make_async_copy(a_buf.at[sl, :, i, :], o_hbm.at[b_, i, pl.ds(s_ * cps, cps), :], sems.at[sl])
                for i in range(SSM_CHUNK)]

    @pl.when(g >= 2)
    def _():
        for cp in out_copies(slot, 0, 0):
            cp.wait()

    low = lax.broadcasted_iota(I32, (1, LANES), 1) < HEAD_DIM
    rows = Q_PER_KV * BLOCK
    upper = lax.broadcasted_iota(I32, (rows, BLOCK), 1) > lax.broadcasted_iota(I32, (rows, BLOCK), 0) % BLOCK
    rblk = lax.broadcasted_iota(I32, (rows, 1), 0) // BLOCK

    cur = pl.multiple_of(step * ATTN_ROWS, ATTN_ROWS)
    prev = pl.multiple_of(jnp.maximum(step * nq - 1, 0) * BLOCK, BLOCK)
    kall = jnp.concatenate([k_ref[0, pl.ds(prev, BLOCK), :], k_ref[0, pl.ds(cur, ATTN_ROWS), :]], axis=0).astype(F32)
    vall = jnp.concatenate([v_ref[0, pl.ds(prev, BLOCK), :], v_ref[0, pl.ds(cur, ATTN_ROWS), :]], axis=0).astype(F32)
    kall = _half_norm(kall, low) * kn_ref[...]
    kswap = pltpu.roll(kall, HEAD_DIM, axis=1)
    vswap = pltpu.roll(vall, HEAD_DIM, axis=1)
    k_dup = [jnp.where(low, kall, kswap).astype(BF16), jnp.where(low, kswap, kall).astype(BF16)]
    v_dup = [jnp.where(low, vall, vswap).astype(BF16), jnp.where(low, vswap, vall).astype(BF16)]
    blk = lambda a, i: [a[hk][i * BLOCK:(i + 1) * BLOCK] for hk in range(N_KV_HEADS)]

    for qb in range(nq):
        q = q_ref[0, qb * BLOCK:(qb + 1) * BLOCK, :].astype(F32)
        attn = _attn_block((step == 0) if qb == 0 else False, q, blk(k_dup, qb), blk(k_dup, qb + 1),
                           blk(v_dup, qb), blk(v_dup, qb + 1), sinks_ref, qn_ref[...], low, upper, rblk)
        attn = _rms(attn) * on_ref[...]
        cpb = BLOCK // SSM_CHUNK
        a_buf[slot, qb * cpb:(qb + 1) * cpb] = attn.reshape(cpb, SSM_CHUNK, D_ATTN)

    for cp in out_copies(slot, pl.program_id(0), step):
        cp.start()

    @pl.when(g == n_steps - 1)
    def _():
        for cp in out_copies(slot, 0, 0):
            cp.wait()
        if n_steps > 1:
            for cp in out_copies(1 - slot, 0, 0):
                cp.wait()


def _attention(q, k, v, sinks, q_norm, k_norm, out_norm):
    b, s, _ = q.shape
    tile2 = lambda g: jnp.tile(g.reshape(1, HEAD_DIM), (1, 2))
    cps = ATTN_ROWS // SSM_CHUNK
    return pl.pallas_call(
        functools.partial(_attn_kernel, n_steps=b * (s // ATTN_ROWS)),
        grid=(b, s // ATTN_ROWS),
        in_specs=[pl.BlockSpec(memory_space=pltpu.SMEM),
                  pl.BlockSpec((1, ATTN_ROWS, D_ATTN), lambda bi, n: (bi, n, 0)),
                  pl.BlockSpec((1, s, D_KV), lambda bi, n: (bi, 0, 0)),
                  pl.BlockSpec((1, s, D_KV), lambda bi, n: (bi, 0, 0)),
                  pl.BlockSpec((1, LANES), lambda bi, n: (0, 0)),
                  pl.BlockSpec((1, LANES), lambda bi, n: (0, 0)),
                  pl.BlockSpec((1, D_ATTN), lambda bi, n: (0, 0))],
        out_specs=pl.BlockSpec(memory_space=pl.ANY),
        out_shape=jax.ShapeDtypeStruct((b, SSM_CHUNK, s // SSM_CHUNK, D_ATTN), F32),
        scratch_shapes=[pltpu.VMEM((2, cps, SSM_CHUNK, D_ATTN), F32), pltpu.SemaphoreType.DMA((2,))],
        compiler_params=_cparams(2),
        name="attention",
    )(sinks, q, k, v, tile2(q_norm), tile2(k_norm), out_norm.reshape(1, D_ATTN))


def _cmul(ar, ai, br, bi):
    return ar * br - ai * bi, ar * bi + ai * br


def _ssm_param_kernel(*refs):
    for gi in range(refs[0].shape[0]):
        _ssm_param_group(gi, *refs)


def _ssm_param_group(gi, lam_ref, bre_ref, bim_ref, cre_ref, cim_ref, tt_ref, wz_ref, wyt_ref, cs_ref):
    f32dot = functools.partial(jnp.dot, preferred_element_type=F32, precision=HIGHEST)
    lr = lam_ref[gi, 0:1, :]
    li = lam_ref[gi, 1:2, :]
    dt = jnp.exp(lam_ref[gi, 2:3, :])
    rho = lr * dt
    th = li * dt
    imag_lane = lax.broadcasted_iota(I32, (1, LANES), 1) >= STATE

    kk = (lax.broadcasted_iota(I32, (N_POW, 1), 0) - (SSM_CHUNK - 1)).astype(F32)
    mag = jnp.exp(rho * kk)
    pw_r = mag * jnp.cos(th * kk)
    pw_i = mag * jnp.sin(th * kk)
    lb_r = pw_r[SSM_CHUNK:SSM_CHUNK + 1]
    lb_i = pw_i[SSM_CHUNK:SSM_CHUNK + 1]
    den = lr * lr + li * li
    coef_r = ((lb_r - 1.0) * lr + lb_i * li) / den
    coef_i = (lb_i * lr - (lb_r - 1.0) * li) / den

    eye = (lax.broadcasted_iota(I32, (SSM_GROUP, SSM_GROUP), 0)
           == lax.broadcasted_iota(I32, (SSM_GROUP, SSM_GROUP), 1)).astype(F32)
    lane_fold = (lax.broadcasted_iota(I32, (STATE, LANES), 1) % STATE
                 == lax.broadcasted_iota(I32, (STATE, LANES), 0)).astype(F32)

    def tile_pos(x):
        return jnp.concatenate([x] * SSM_CHUNK, axis=0)

    def power_rows(k_of_pos):
        idx = [k_of_pos(p) + (SSM_CHUNK - 1) for p in range(SSM_CHUNK)]
        rep = lambda t: jnp.concatenate([jnp.broadcast_to(t[r:r + 1], (SSM_GROUP, LANES)) for r in idx], axis=0)
        return rep(pw_r), rep(pw_i)

    def b_rows(b_ref):
        b2 = jnp.concatenate([b_ref[gi], b_ref[gi]], axis=0)
        return tile_pos(lax.dot_general(eye, b2, (((1,), (1,)), ((), ())), preferred_element_type=F32,
                                        precision=HIGHEST))

    def c_rows(c_ref):
        return tile_pos(f32dot(c_ref[gi], lane_fold))

    bbar_r, bbar_i = _cmul(coef_r, coef_i, b_rows(bre_ref), b_rows(bim_ref))
    c_r = c_rows(cre_ref)
    c_i = c_rows(cim_ref)

    a_r, a_i = _cmul(bbar_r, bbar_i, *power_rows(lambda p: -p))
    a2c = jnp.where(imag_lane, -a_i, a_r)
    m_r, m_i = _cmul(c_r, c_i, *power_rows(lambda p: p))
    bmc = jnp.where(imag_lane, m_i, m_r)
    tt = f32dot(bmc, a2c.T)
    causal = (lax.broadcasted_iota(I32, (SSM_ROW, 1), 0) // SSM_GROUP
              >= lax.broadcasted_iota(I32, (1, SSM_ROW), 1) // SSM_GROUP)
    tt_ref[gi] = jnp.where(causal, tt, 0.0).astype(BF16)

    w_r, w_i = _cmul(bbar_r, bbar_i, *power_rows(lambda p: SSM_CHUNK - 1 - p))
    wz_ref[gi, :, :LANES] = jnp.where(imag_lane, w_i, w_r).astype(BF16)
    wz_ref[gi, :, LANES:] = jnp.where(imag_lane, w_r, w_i).astype(BF16)

    y_r, y_i = _cmul(c_r, c_i, *power_rows(lambda p: p + 1))
    wyt_ref[gi] = jnp.where(imag_lane, -y_i, y_r).astype(BF16)

    cs_ref[gi, 0:1, :] = pw_r[N_POW - 1:N_POW]
    cs_ref[gi, 1:2, :] = jnp.where(imag_lane, pw_i[N_POW - 1:N_POW], -pw_i[N_POW - 1:N_POW])


def _ssm_params(lam_re, lam_im, log_dt, b_re, b_im, c_re, c_im):
    g = lam_re.shape[0]
    lam = jnp.stack([lam_re, lam_im, jnp.broadcast_to(log_dt[:, None], (g, STATE))], axis=1)
    lam = jnp.concatenate([lam, lam], axis=2)
    ng = SSM_GROUPS_PER_STEP
    blk = lambda *shape: pl.BlockSpec((ng,) + shape, lambda i: (i, 0, 0))
    return pl.pallas_call(
        _ssm_param_kernel,
        grid=(g // ng,),
        in_specs=[blk(3, LANES), blk(STATE, SSM_GROUP), blk(STATE, SSM_GROUP), blk(SSM_GROUP, STATE),
                  blk(SSM_GROUP, STATE)],
        out_specs=[blk(SSM_ROW, SSM_ROW), blk(SSM_ROW, SSM_ROW), blk(SSM_ROW, LANES), blk(2, LANES)],
        out_shape=[jax.ShapeDtypeStruct((g, SSM_ROW, SSM_ROW), BF16),
                   jax.ShapeDtypeStruct((g, SSM_ROW, SSM_ROW), BF16),
                   jax.ShapeDtypeStruct((g, SSM_ROW, LANES), BF16),
                   jax.ShapeDtypeStruct((g, 2, LANES), F32)],
        compiler_params=_cparams(1),
        name="ssm_params",
    )(lam, b_re, b_im, c_re, c_im)


def _ssm_kernel(ut_ref, tt_ref, wz_ref, wyt_ref, cs_ref, d_ref, yt_ref, z_scr, s_scr):
    batch, ng, _, nc = ut_ref.shape
    uts = [jnp.concatenate([ut_ref[b, gi] for b in range(batch)], axis=1) for gi in range(ng)]
    for gi in range(ng):
        z = lax.dot_general(uts[gi], wz_ref[gi], (((0,), (0,)), ((), ())), preferred_element_type=F32)
        _to_lane_blocks(z_scr.at[gi], z)
    c1 = [cs_ref[gi, 0:1, :] for gi in range(ng)]
    c2 = [cs_ref[gi, 1:2, :] for gi in range(ng)]

    def step(c, carry):
        rows = pl.ds(c, batch, stride=nc)
        out = []
        for gi in range(ng):
            s1, s2 = carry[gi]
            s_scr[gi, rows, :] = s1
            out.append((c1[gi] * s1 + c2[gi] * s2 + z_scr[gi, 0, rows, :],
                        c1[gi] * s2 - c2[gi] * s1 + z_scr[gi, 1, rows, :]))
        return tuple(out)

    zero = jnp.zeros((batch, LANES), F32)
    lax.fori_loop(0, nc, step, ((zero, zero),) * ng, unroll=8)
    eye = (lax.broadcasted_iota(I32, (SSM_GROUP, SSM_GROUP), 0)
           == lax.broadcasted_iota(I32, (SSM_GROUP, SSM_GROUP), 1))
    for gi in range(ng):
        y = jnp.dot(tt_ref[gi], uts[gi], preferred_element_type=F32)
        y = y + lax.dot_general(wyt_ref[gi], s_scr[gi].astype(BF16), (((1,), (1,)), ((), ())),
                                preferred_element_type=F32)
        d_col = jnp.sum(jnp.where(eye, d_ref[gi], 0.0), axis=1, keepdims=True)
        y = y + jnp.concatenate([d_col] * SSM_CHUNK, axis=0) * uts[gi].astype(F32)
        for b in range(batch):
            yt_ref[b, gi] = y[:, b * nc:(b + 1) * nc]


def _ssm(ut, tt, wz, wyt, cs, d_skip):
    b, g, _, nc = ut.shape
    ng = SSM_GROUPS_PER_STEP
    blk = lambda *shape: pl.BlockSpec((ng,) + shape, lambda i: (i, 0, 0))
    act = pl.BlockSpec((b, ng, SSM_ROW, nc), lambda i: (0, i, 0, 0))
    return pl.pallas_call(
        _ssm_kernel,
        grid=(g // ng,),
        in_specs=[act, blk(SSM_ROW, SSM_ROW), blk(SSM_ROW, SSM_ROW), blk(SSM_ROW, LANES), blk(2, LANES),
                  blk(1, SSM_GROUP)],
        out_specs=act,
        out_shape=jax.ShapeDtypeStruct((b, g, SSM_ROW, nc), F32),
        scratch_shapes=[pltpu.VMEM((ng, SSM_ROW // LANES, b * nc, LANES), F32), pltpu.VMEM((ng, b * nc, LANES), F32)],
        compiler_params=_cparams(1),
        name="ssm",
    )(ut, tt, wz, wyt, cs, d_skip.reshape(g, 1, SSM_GROUP))


def _post_kernel(x4_hbm, attn_ref, yt_ref, mod_ref, wglu_ref, bglu_ref, sn_ref, wout_f32_ref, nf_ref, wrt_ref, br_ref,
                 tri_ref, x1_ref, h2_ref, eidx_ref, wts_ref, lrank_ref, r0_ref, cnt_ref, carry_ref, xp_buf,
                 wglut_ref, wout_ref, wr_ref, sems):
    @pl.when((pl.program_id(0) == 0) & (pl.program_id(1) == 0))
    def _():
        carry_ref[...] = jnp.zeros_like(carry_ref)
        wglut_ref[...] = wglu_ref[...].T.astype(BF16)
        wout_ref[...] = wout_f32_ref[...].astype(BF16)
        wr_ref[...] = wrt_ref[...].T.astype(BF16)

    slot = _prefetch_pos_rows(x4_hbm, xp_buf, sems, POST_SUB * POST_POS)
    nc = attn_ref.shape[2]
    ts = POST_POS * nc
    d = x1_ref.shape[3]
    iota_e = lax.broadcasted_iota(I32, (N_EXPERTS, ts), 0).astype(F32)
    counts = []
    for sub in range(POST_SUB):
        pos = range(sub * POST_POS, (sub + 1) * POST_POS)
        lanes = slice(sub * ts, (sub + 1) * ts)
        yt = jnp.concatenate(
            [yt_ref[0, :, il * SSM_GROUP:(il + 1) * SSM_GROUP, :].reshape(D_SSM, nc) for il in pos], axis=1)
        g = jax.nn.gelu(yt)
        gate = jax.nn.sigmoid(jnp.dot(wglut_ref[...], g.astype(BF16), preferred_element_type=F32) + bglu_ref[...])
        ssm_t = _rms(g * gate, axis=0) * sn_ref[...]
        attn = attn_ref[0, sub * POST_POS:(sub + 1) * POST_POS].reshape(ts, D_ATTN)
        mixed = jnp.concatenate([attn.astype(BF16), ssm_t.T.astype(BF16)], axis=-1)
        o = jnp.dot(mixed, wout_ref[...], preferred_element_type=F32)
        x = jnp.concatenate([xp_buf[slot, il] for il in pos], axis=0)
        x1 = x + mod_ref[0, 2:3, :] * o
        x1_ref[0, sub * POST_POS:(sub + 1) * POST_POS] = x1.reshape(POST_POS, nc, d)
        h2 = _rms(x1) * nf_ref[...] * (1.0 + mod_ref[0, 4:5, :]) + mod_ref[0, 3:4, :]
        h2_ref[0, sub * POST_POS:(sub + 1) * POST_POS] = h2.astype(BF16).reshape(POST_POS, nc, d)

        logits = lax.dot_general(wr_ref[...], h2.astype(BF16), (((1,), (1,)), ((), ())),
                                 preferred_element_type=F32) + br_ref[...]
        l = logits
        idxs, vals = [], []
        for _ in range(TOP_K):
            m = jnp.max(l, axis=0, keepdims=True)
            idx = jnp.min(jnp.where(l == m, iota_e, float(N_EXPERTS)), axis=0, keepdims=True)
            idxs.append(idx)
            vals.append(m)
            l = jnp.where(iota_e == idx, -jnp.inf, l)
        es = [jnp.exp(v - vals[0]) for v in vals]
        tot = es[0] + es[1] + es[2] + es[3]
        member = jnp.zeros((N_EXPERTS, ts), F32)
        for idx in idxs:
            member = member + (iota_e == idx).astype(F32)
        before = jnp.dot(member.astype(BF16), tri_ref[...], preferred_element_type=F32)
        for k in range(TOP_K):
            eidx_ref[k:k + 1, lanes] = idxs[k].astype(I32)
            wts_ref[k:k + 1, lanes] = es[k] / tot
            lrank_ref[k:k + 1, lanes] = jnp.sum(jnp.where(iota_e == idxs[k], before, 0.0), axis=0,
                                                keepdims=True).astype(I32)
        counts.append(jnp.sum(member, axis=1, keepdims=True))

    carry = carry_ref[...]
    for sub in range(POST_SUB):
        r0_ref[sub] = carry.astype(I32)
        carry = carry + counts[sub]
    carry_ref[...] = carry
    cnt_ref[...] = carry.astype(I32)


def _post(x, attn, yt, mod, w_glu, b_glu, ssm_norm, w_out, norm_ffn, w_router, b_router):
    b, s, d = x.shape
    nc = s // SSM_CHUNK
    ts = POST_POS * nc
    npos = POST_SUB * POST_POS
    nt = SSM_CHUNK // npos
    t = b * s
    pm = lambda bi, j: (bi, j, 0, 0)
    const = lambda bi, j: (0, 0)
    tok = lambda bi, j: (0, bi * nt + j)
    tri = (lax.broadcasted_iota(I32, (ts, ts), 0) < lax.broadcasted_iota(I32, (ts, ts), 1)).astype(BF16)
    col = lambda a: a.reshape(-1, 1)
    return pl.pallas_call(
        _post_kernel,
        grid=(b, nt),
        in_specs=[pl.BlockSpec(memory_space=pl.ANY),
                  pl.BlockSpec((1, npos, nc, D_ATTN), pm),
                  pl.BlockSpec((1, N_GROUPS, npos * SSM_GROUP, nc), lambda bi, j: (bi, 0, j, 0)),
                  pl.BlockSpec((1, 6, d), lambda bi, j: (bi, 0, 0)),
                  pl.BlockSpec((D_SSM, D_SSM), const),
                  pl.BlockSpec((D_SSM, 1), const),
                  pl.BlockSpec((D_SSM, 1), const),
                  pl.BlockSpec((d, d), const),
                  pl.BlockSpec((1, d), const),
                  pl.BlockSpec((d, N_EXPERTS), const),
                  pl.BlockSpec((N_EXPERTS, 1), const),
                  pl.BlockSpec((ts, ts), const)],
        out_specs=[pl.BlockSpec((1, npos, nc, d), pm),
                   pl.BlockSpec((1, npos, nc, d), pm),
                   pl.BlockSpec((TOP_K, POST_SUB * ts), tok),
                   pl.BlockSpec((TOP_K, POST_SUB * ts), tok),
                   pl.BlockSpec((TOP_K, POST_SUB * ts), tok),
                   pl.BlockSpec((POST_SUB, N_EXPERTS, 1), lambda bi, j: (bi * nt + j, 0, 0)),
                   pl.BlockSpec((N_EXPERTS, 1), const)],
        out_shape=[jax.ShapeDtypeStruct((b, SSM_CHUNK, nc, d), F32),
                   jax.ShapeDtypeStruct((b, SSM_CHUNK, nc, d), BF16),
                   jax.ShapeDtypeStruct((TOP_K, t), I32),
                   jax.ShapeDtypeStruct((TOP_K, t), F32),
                   jax.ShapeDtypeStruct((TOP_K, t), I32),
                   jax.ShapeDtypeStruct((b * nt * POST_SUB, N_EXPERTS, 1), I32),
                   jax.ShapeDtypeStruct((N_EXPERTS, 1), I32)],
        scratch_shapes=[pltpu.VMEM((N_EXPERTS, 1), F32), pltpu.VMEM((2, npos, nc, d), F32),
                        pltpu.VMEM((D_SSM, D_SSM), BF16), pltpu.VMEM((d, d), BF16), pltpu.VMEM((N_EXPERTS, d), BF16),
                        pltpu.SemaphoreType.DMA((2,))],
        compiler_params=_cparams(2, ROW_VMEM_BYTES),
        name="post",
    )(x.reshape(b, nc, SSM_CHUNK, d), attn, yt, mod, w_glu, col(b_glu), col(ssm_norm), w_out,
      norm_ffn.reshape(1, -1), w_router, col(b_router), tri)


def _route_kernel(eidx_ref, lrank_ref, r0_ref, cnt_ref, ls_ref, tab_ref, te_ref, nv_ref, nx_ref, pad_ref):
    cnt = cnt_ref[...]
    tiles = (cnt + (RUN - 1 + FFN_ROWS - 1)) // FFN_ROWS
    er = lax.broadcasted_iota(I32, (N_EXPERTS, N_EXPERTS), 0)
    ec = lax.broadcasted_iota(I32, (N_EXPERTS, N_EXPERTS), 1)
    ltri = (ec < er).astype(BF16)

    def excl_cumsum(v):
        vb = jnp.broadcast_to(v.astype(F32), (N_EXPERTS, LANES)).astype(BF16)
        return jnp.dot(ltri, vb, preferred_element_type=F32)[:, 0:1].astype(I32)

    start_t = excl_cumsum(tiles)
    end_t = start_t + tiles
    start = start_t * FFN_ROWS
    pad_ref[...] = start + cnt

    nb = r0_ref.shape[0]
    ts = eidx_ref.shape[1] // nb
    iota_e = lax.broadcasted_iota(I32, (N_EXPERTS, ts), 0)
    iota_t = lax.broadcasted_iota(I32, (N_EXPERTS, TABW), 0)
    lane = lax.broadcasted_iota(I32, (1, TABW), 1)

    def block(b, carry):
        lanes = pl.ds(pl.multiple_of(b * ts, ts), ts)
        sels = [iota_e == eidx_ref[k:k + 1, lanes] for k in range(TOP_K)]
        member = sels[0].astype(I32) + sels[1].astype(I32) + sels[2].astype(I32) + sels[3].astype(I32)
        units = lax.shift_right_logical(jnp.sum(member, axis=1, keepdims=True) + (RUN - 1), RUN_SHIFT)
        u0 = excl_cumsum(units)
        for k in range(TOP_K):
            first = jnp.sum(jnp.where(sels[k], u0, 0), axis=0, keepdims=True)
            ls_ref[k:k + 1, lanes] = first * RUN + lrank_ref[k:k + 1, lanes]
        n_big = lax.shift_right_logical(units, 1)
        n_small = units & 1
        slot0 = start + r0_ref[b]

        def chunk_rows(idx, counts):
            c0 = excl_cumsum(counts)
            sel = iota_t == jnp.sum((idx >= c0 + counts).astype(I32), axis=0, keepdims=True)
            pick = lambda v: jnp.sum(jnp.where(sel, v, 0), axis=0, keepdims=True)
            j = idx - pick(c0)
            return pick(slot0), pick(u0), j, pick(n_big), idx < jnp.max(c0 + counts, axis=0, keepdims=True)

        s_b, u_b, j_b, _, ok_b = chunk_rows(lane, n_big)
        s_s, u_s, _, nb_s, ok_s = chunk_rows(lane - SMALL0, n_small)
        small = lane >= SMALL0
        slot = jnp.where(small, s_s + nb_s * BIG, s_b + j_b * BIG)
        local = jnp.where(small, (u_s + 2 * nb_s) * RUN, (u_b + 2 * j_b) * RUN)
        ok = (small & ok_s) | (jnp.logical_not(small) & ok_b)
        counts = jnp.where(lane == TABW - 2, jnp.sum(n_big, axis=0, keepdims=True),
                           jnp.sum(n_small, axis=0, keepdims=True))
        tab_ref[b, 0:1, :] = jnp.where(lane >= TABW - 2, counts, jnp.where(ok, slot, -1))
        tab_ref[b, 1:2, :] = jnp.where(ok, local, 0)
        return carry

    lax.fori_loop(0, nb, block, 0, unroll=2)

    nv = jnp.max(end_t, axis=0, keepdims=True)
    width = te_ref.shape[1]
    ti = jnp.minimum(lax.broadcasted_iota(I32, (N_EXPERTS, width), 1), nv - 1)
    te = jnp.minimum(jnp.sum((ti >= end_t).astype(I32), axis=0, keepdims=True), N_EXPERTS - 1)
    te_ref[...] = te
    nv_ref[...] = jnp.broadcast_to(nv, nv_ref.shape)
    ie = lax.broadcasted_iota(I32, (N_EXPERTS, width), 0)
    own_end = jnp.sum(jnp.where(ie == te, end_t, 0), axis=0, keepdims=True)
    nxt = jnp.minimum(jnp.sum((own_end >= end_t).astype(I32), axis=0, keepdims=True), N_EXPERTS - 1)
    nx_ref[...] = jnp.where(own_end < nv, nxt, -1)


def _route(eidx, lrank, r0, cnt, n_tiles):
    t = eidx.shape[1]
    nb = r0.shape[0]
    width = -(-n_tiles // LANES) * LANES
    return pl.pallas_call(
        _route_kernel,
        out_shape=[jax.ShapeDtypeStruct((TOP_K, t), I32),
                   jax.ShapeDtypeStruct((nb, 2, TABW), I32),
                   jax.ShapeDtypeStruct((1, width), I32),
                   jax.ShapeDtypeStruct((1, LANES), I32),
                   jax.ShapeDtypeStruct((1, width), I32),
                   jax.ShapeDtypeStruct((N_EXPERTS, 1), I32)],
        name="route",
    )(eidx, lrank, r0, cnt)


def _for_chunk_pairs(n, fn):
    def body(i, carry):
        fn(2 * i, 0)

        @pl.when(2 * i + 1 < n)
        def _():
            fn(2 * i + 1, 1)
        return carry
    lax.fori_loop(0, lax.shift_right_logical(n + 1, 1), body, 0)


def _for_block_chunks(tab_ref, blk, fn):
    base = blk * (2 * TABW)
    for first, count_lane, n_rows in ((0, TABW - 2, BIG), (SMALL0, TABW - 1, RUN)):
        def visit(c, parity, first=first, n_rows=n_rows):
            fn(tab_ref[base + first + c], pl.multiple_of(tab_ref[base + TABW + first + c], RUN), n_rows, parity)
        _for_chunk_pairs(tab_ref[base + count_lane], visit)


def _local_rows(ts):
    return ts * TOP_K + N_EXPERTS * RUN


def _dispatch_kernel(tab_ref, pad_ref, nv_ref, h_ref, ls_ref, xs_ref, buf, zbuf, sems, zsem, *, n_alloc):
    b = pl.program_id(0)
    slot = b % 2
    ts = h_ref.shape[0]
    local = buf.shape[2]

    def chunk_copy(sl, slot_row, local_row, n):
        return pltpu.make_async_copy(buf.at[sl, :, pl.ds(local_row, n), :], xs_ref.at[:, pl.ds(slot_row, n), :],
                                     sems.at[sl])

    @pl.when(b == 0)
    def _():
        zbuf[...] = jnp.zeros_like(zbuf)
        zrows = zbuf.shape[1]
        zero = lambda row: pltpu.make_async_copy(zbuf, xs_ref.at[:, pl.ds(row, zrows), :], zsem)
        for phase in range(3):
            for e in range(phase, N_EXPERTS, 3):
                zero(pad_ref[e, 0]).start()
            for e in range(phase, N_EXPERTS, 3):
                zero(pad_ref[e, 0]).wait()
        ztile = lambda i: pltpu.make_async_copy(zbuf.at[:, pl.ds(0, FFN_ROWS), :],
                                                xs_ref.at[:, pl.ds((nv_ref[0, 0] + i) * FFN_ROWS, FFN_ROWS), :], zsem)

        def tail_start(i, carry):
            ztile(i).start()
            return carry

        def tail_wait(i, carry):
            ztile(i).wait()
            return carry
        lax.fori_loop(0, n_alloc - nv_ref[0, 0], tail_start, 0)
        lax.fori_loop(0, n_alloc - nv_ref[0, 0], tail_wait, 0)

    r = lax.broadcasted_iota(I32, (local, ts), 0)
    hit = (r == ls_ref[0:1, :]) | (r == ls_ref[1:2, :]) | (r == ls_ref[2:3, :]) | (r == ls_ref[3:4, :])
    hit = hit.astype(BF16)
    for pb in range(PANELS):
        srt = jnp.dot(hit, h_ref[:, pb * PANEL_COLS:(pb + 1) * PANEL_COLS], preferred_element_type=F32)
        buf[slot, pb] = _pack_panel(srt, exact=True)

    @pl.when(b > 0)
    def _():
        _for_block_chunks(tab_ref, b - 1, lambda s, l, n, p: chunk_copy(1 - slot, s, l, n).wait())

    _for_block_chunks(tab_ref, b, lambda s, l, n, p: chunk_copy(slot, s, l, n).start(priority=p))

    @pl.when(b == pl.num_programs(0) - 1)
    def _():
        _for_block_chunks(tab_ref, b, lambda s, l, n, p: chunk_copy(slot, s, l, n).wait())


def _dispatch(tab, pad, nv, h2, ls, n_alloc):
    t, d = h2.shape
    nb = tab.shape[0] // (2 * TABW)
    ts = t // nb
    return pl.pallas_call(
        functools.partial(_dispatch_kernel, n_alloc=n_alloc),
        grid_spec=pltpu.PrefetchScalarGridSpec(
            num_scalar_prefetch=3,
            grid=(nb,),
            in_specs=[pl.BlockSpec((ts, d), lambda i, *_: (i, 0)),
                      pl.BlockSpec((TOP_K, ts), lambda i, *_: (0, i))],
            out_specs=pl.BlockSpec(memory_space=pl.ANY),
            scratch_shapes=[pltpu.VMEM((2, PANELS, _local_rows(ts), LANES), U32),
                            pltpu.VMEM((PANELS, FFN_ROWS + RUN, LANES), U32),
                            pltpu.SemaphoreType.DMA((2,)), pltpu.SemaphoreType.DMA],
        ),
        out_shape=jax.ShapeDtypeStruct((PANELS, n_alloc * FFN_ROWS, LANES), U32),
        compiler_params=_cparams(1, ROW_VMEM_BYTES),
        name="dispatch",
    )(tab, pad, nv, h2, ls)


def _ffn_kernel(te_ref, nv_ref, nx_ref, xs_ref, wgu_hbm, bgu_ref, wd_hbm, bd_ref, perm_ref, ys_ref,
                wgu_stage, wd_stage, wg_scr, wu_scr, wd_scr, bg_scr, bu_scr, sems):
    p = pl.program_id(0)
    t0 = 2 * p
    t1 = t0 + 1
    e0 = te_ref[0, t0]
    e1 = te_ref[0, t1]
    v0 = t0 < nv_ref[0, 0]
    v1 = t1 < nv_ref[0, 0]
    new0 = (p == 0) | (e0 != te_ref[0, jnp.maximum(t0 - 1, 0)])
    same = v1 & (e1 == e0)

    def stage_copies(e):
        return (pltpu.make_async_copy(wgu_hbm.at[e], wgu_stage, sems.at[0]),
                pltpu.make_async_copy(wd_hbm.at[e], wd_stage, sems.at[1]))

    def load_expert(t, first):
        e = te_ref[0, t]
        if first:
            @pl.when(p == 0)
            def _():
                for cp in stage_copies(e):
                    cp.start()

        for cp in stage_copies(e):
            cp.wait()
        bias = bgu_ref[e]
        for c in range(2 * D_FF // PERM):
            cols = slice(c * PERM, (c + 1) * PERM)
            half = slice(c * (PERM // 2), (c + 1) * (PERM // 2))
            w = wgu_stage[:, cols].astype(BF16)
            pw = jnp.dot(w, perm_ref[...], preferred_element_type=F32).astype(BF16)
            wg_scr[:, half] = pw[:, :PERM // 2]
            wu_scr[:, half] = pw[:, PERM // 2:]
            b1 = bias[:, cols].astype(BF16)
            r1 = bias[:, cols] - b1.astype(F32)
            b2 = r1.astype(BF16)
            b3 = (r1 - b2.astype(F32)).astype(BF16)
            terms = jnp.concatenate([b1, b2, b3, jnp.zeros((5, PERM), BF16)], axis=0)
            pb = jnp.sum(jnp.dot(terms, perm_ref[...], preferred_element_type=F32), axis=0, keepdims=True)
            bg_scr[:, half] = pb[:, :PERM // 2]
            bu_scr[:, half] = pb[:, PERM // 2:]

        @pl.when(nx_ref[0, t] >= 0)
        def _():
            stage_copies(nx_ref[0, t])[0].start()

    def run(lo, n, e, first_of=None):
        if first_of is not None:
            wd_scr[...] = wd_stage[...].astype(BF16)
        x = _unpack_panels([xs_ref[pb, lo:lo + n, :] for pb in range(PANELS)])
        gate = jnp.dot(x, wg_scr[...], preferred_element_type=F32) + bg_scr[...]
        up = jnp.dot(x, wu_scr[...], preferred_element_type=F32) + bu_scr[...]
        gate = jnp.minimum(gate, SWIGLU_LIMIT)
        up = jnp.clip(up, -SWIGLU_LIMIT, SWIGLU_LIMIT)
        act = ((up + 1.0) * (gate * jax.nn.sigmoid(SWIGLU_ALPHA * gate))).astype(BF16)
        bd = bd_ref[e]
        for pb in range(PANELS):
            cols = slice(pb * PANEL_COLS, (pb + 1) * PANEL_COLS)
            y = jnp.dot(act, wd_scr[:, cols], preferred_element_type=F32) + bd[:, cols]
            ys_ref[pb, lo:lo + n, :] = _pack_panel(y)
        if first_of is not None:
            @pl.when(nx_ref[0, first_of] >= 0)
            def _():
                stage_copies(nx_ref[0, first_of])[1].start()

    old0 = jnp.logical_not(new0)
    single = v0 & jnp.logical_not(same)

    @pl.when(v0 & new0)
    def _():
        load_expert(t0, True)

    @pl.when(same & new0)
    def _():
        run(0, 2 * FFN_ROWS, e0, first_of=t0)

    @pl.when(same & old0)
    def _():
        run(0, 2 * FFN_ROWS, e0)

    @pl.when(single & new0)
    def _():
        run(0, FFN_ROWS, e0, first_of=t0)

    @pl.when(single & old0)
    def _():
        run(0, FFN_ROWS, e0)

    @pl.when(v1 & jnp.logical_not(same))
    def _():
        load_expert(t1, False)
        run(FFN_ROWS, FFN_ROWS, e1, first_of=t1)

    @pl.when(v0 & jnp.logical_not(v1))
    def _():
        ys_ref[:, FFN_ROWS:, :] = xs_ref[:, FFN_ROWS:, :]


def _ffn(te, nv, nx, xs, w_gate_up, bgu, w_down, bd, n_tiles):
    d = D_MODEL
    pair = lambda i, te, nv, nx: (0, jnp.minimum(i, lax.shift_right_logical(nv[0, 0] - 1, 1)), 0)
    whole = lambda i, te, nv, nx: (0, 0, 0)
    r = lax.broadcasted_iota(I32, (PERM, PERM), 0)
    c = lax.broadcasted_iota(I32, (PERM, PERM), 1)
    perm = (r == jnp.where(c < PERM // 2, 2 * c, 2 * (c - PERM // 2) + 1)).astype(BF16)
    return pl.pallas_call(
        _ffn_kernel,
        grid_spec=pltpu.PrefetchScalarGridSpec(
            num_scalar_prefetch=3,
            grid=(n_tiles // 2,),
            in_specs=[pl.BlockSpec((PANELS, 2 * FFN_ROWS, LANES), pair),
                      pl.BlockSpec(memory_space=pl.ANY),
                      pl.BlockSpec((N_EXPERTS, 1, 2 * D_FF), whole),
                      pl.BlockSpec(memory_space=pl.ANY),
                      pl.BlockSpec((N_EXPERTS, 1, d), whole),
                      pl.BlockSpec((PERM, PERM), lambda i, te, nv, nx: (0, 0))],
            out_specs=pl.BlockSpec((PANELS, 2 * FFN_ROWS, LANES), pair),
            scratch_shapes=[pltpu.VMEM((d, 2 * D_FF), F32), pltpu.VMEM((D_FF, d), F32),
                            pltpu.VMEM((d, D_FF), BF16), pltpu.VMEM((d, D_FF), BF16), pltpu.VMEM((D_FF, d), BF16),
                            pltpu.VMEM((1, D_FF), F32), pltpu.VMEM((1, D_FF), F32),
                            pltpu.SemaphoreType.DMA((2,))],
        ),
        out_shape=jax.ShapeDtypeStruct(xs.shape, U32),
        input_output_aliases={3: 0},
        compiler_params=_cparams(1, FFN_VMEM_BYTES),
        name="ffn",
    )(te, nv, nx, xs, w_gate_up, bgu, w_down, bd, perm)


def _combine_kernel(tab_ref, x1_ref, ls_ref, w_ref, mod_ref, ys_ref, o4_hbm, ybuf, ob_buf, sems, osems, *, n_steps):
    jj = pl.program_id(1)
    g = pl.program_id(0) * pl.num_programs(1) + jj
    slot = g % 2
    nc = x1_ref.shape[2]
    tt = POST_POS * nc
    d = x1_ref.shape[3]
    local = ybuf.shape[3]
    npos = COMB_SUB * POST_POS

    def for_step_chunks(step, sl, fn):
        for sub in range(COMB_SUB):
            def visit(s, l, n, p, sub=sub):
                fn(pltpu.make_async_copy(ys_ref.at[:, pl.ds(s, n), :], ybuf.at[sl, sub, :, pl.ds(l, n), :],
                                         sems.at[sl]), p)
            _for_block_chunks(tab_ref, step * COMB_SUB + sub, visit)

    @pl.when(g == 0)
    def _():
        ybuf[...] = jnp.zeros_like(ybuf)
        for_step_chunks(0, 0, lambda cp, p: cp.start(priority=p))

    @pl.when(g + 1 < n_steps)
    def _():
        for_step_chunks(g + 1, 1 - slot, lambda cp, p: cp.start(priority=p))

    for_step_chunks(g, slot, lambda cp, p: cp.wait())

    def out_copies(sl, b_, j_):
        return [pltpu.make_async_copy(ob_buf.at[sl, il], o4_hbm.at[b_, :, npos * j_ + il, :], osems.at[sl])
                for il in range(npos)]

    @pl.when(g >= 2)
    def _():
        for cp in out_copies(slot, 0, 0):
            cp.wait()

    r = lax.broadcasted_iota(I32, (tt, local), 1).astype(F32)
    to_cols = lambda a: jnp.concatenate([a, jnp.zeros_like(a)], axis=0).T
    for sub in range(COMB_SUB):
        lanes = slice(sub * tt, (sub + 1) * tt)
        ls_c = to_cols(ls_ref[:, lanes].astype(F32))
        w_c = to_cols(w_ref[:, lanes])
        wm = jnp.zeros((tt, local), F32)
        for k in range(TOP_K):
            wm = jnp.where(r == ls_c[:, k:k + 1], w_c[:, k:k + 1], wm)
        y = _unpack_panels([ybuf[slot, sub, pb] for pb in range(PANELS)])
        acc = jnp.dot(wm.astype(BF16), y, preferred_element_type=F32)
        out = x1_ref[0, sub * POST_POS:(sub + 1) * POST_POS].reshape(tt, d) + mod_ref[0, 5:6, :] * acc
        for il in range(POST_POS):
            ob_buf[slot, sub * POST_POS + il] = out[il * nc:(il + 1) * nc]

    for cp in out_copies(slot, pl.program_id(0), jj):
        cp.start()

    @pl.when(g == n_steps - 1)
    def _():
        for cp in out_copies(slot, 0, 0):
            cp.wait()
        if n_steps > 1:
            for cp in out_copies(1 - slot, 0, 0):
                cp.wait()


def _combine(tab, x1, ls, wts, mod, ys):
    b, _, nc, d = x1.shape
    s = SSM_CHUNK * nc
    tt = POST_POS * nc
    npos = COMB_SUB * POST_POS
    nt = SSM_CHUNK // npos
    o4 = pl.pallas_call(
        functools.partial(_combine_kernel, n_steps=b * nt),
        grid_spec=pltpu.PrefetchScalarGridSpec(
            num_scalar_prefetch=1,
            grid=(b, nt),
            in_specs=[pl.BlockSpec((1, npos, nc, d), lambda bi, j, *_: (bi, j, 0, 0)),
                      pl.BlockSpec((TOP_K, COMB_SUB * tt), lambda bi, j, *_: (0, bi * nt + j)),
                      pl.BlockSpec((TOP_K, COMB_SUB * tt), lambda bi, j, *_: (0, bi * nt + j)),
                      pl.BlockSpec((1, 6, d), lambda bi, j, *_: (bi, 0, 0)),
                      pl.BlockSpec(memory_space=pl.ANY)],
            out_specs=pl.BlockSpec(memory_space=pl.ANY),
            scratch_shapes=[pltpu.VMEM((2, COMB_SUB, PANELS, _local_rows(tt), LANES), U32),
                            pltpu.VMEM((2, npos, nc, d), F32),
                            pltpu.SemaphoreType.DMA((2,)), pltpu.SemaphoreType.DMA((2,))],
        ),
        out_shape=jax.ShapeDtypeStruct((b, nc, SSM_CHUNK, d), F32),
        compiler_params=_cparams(2),
        name="combine",
    )(tab, x1, ls, wts, mod, ys)
    return o4.reshape(b, s, d)


def kernel(x, c, w_ada, b_ada, norm_mix, w_in, b_in, q_norm, k_norm, sinks, lam_re, lam_im, log_dt, b_re, b_im,
           c_re, c_im, d_skip, w_glu, b_glu, attn_out_norm, ssm_out_norm, w_out, norm_ffn, w_router, b_router,
           w_gate_up, b_gate_up, w_down, b_down):
    b, s, d = x.shape
    t = b * s
    depth = w_ada.shape[0]
    n_tiles = -(-(t * TOP_K + N_EXPERTS * (RUN - 1 + FFN_ROWS - 1)) // FFN_ROWS)
    n_tiles += n_tiles % 2
    n_alloc = n_tiles + 2
    for l in range(depth):
        mod = _adaln(c, w_ada[l], b_ada[l]).reshape(b, 6, d)
        q, k, v, ut = _inproj(x, mod, norm_mix[l], w_in[l], b_in[l])
        attn = _attention(q, k, v, sinks[l], q_norm[l], k_norm[l], attn_out_norm[l])
        tt, wz, wyt, cs = _ssm_params(lam_re[l], lam_im[l], log_dt[l], b_re[l], b_im[l], c_re[l], c_im[l])
        yt = _ssm(ut, tt, wz, wyt, cs, d_skip[l])
        x1, h2, eidx, wts, lrank, r0, cnt = _post(x, attn, yt, mod, w_glu[l], b_glu[l], ssm_out_norm[l], w_out[l],
                                                  norm_ffn[l], w_router[l], b_router[l])
        ls, tab, te, nv, nx, pad = _route(eidx, lrank, r0, cnt, n_tiles)
        tab = tab.reshape(-1)
        xs = _dispatch(tab, pad, nv, h2.reshape(t, d), ls, n_alloc)
        ys = _ffn(te, nv, nx, xs, w_gate_up[l], b_gate_up[l][:, None, :],
                  w_down[l], b_down[l][:, None, :], n_tiles)
        x = _combine(tab, x1, ls, wts, mod, ys)
    return x
```
